```python
import math
import jax, jax.numpy as jnp
from jax import lax
import numpy as np

D_MODEL = 1024
BATCH = 32
SEQ = 256
DEPTH = 2
DEC_BATCH = 8
DEC_SEQ = 1024
PAST_LEN = 512

GRID_W = 64
ROPE_BASE = 10000.0
DIFF_HEADS = 4
DIFF_DH = 64
DIFF_WIDTH = DIFF_HEADS * 2 * DIFF_DH
POOL_GROUPS = 4
POOL_WIDTH = D_MODEL - DIFF_WIDTH
POOL_GC = POOL_WIDTH // POOL_GROUPS
POOL_WINDOWS = (2, 4, 8, 16)
Q_BLOCK = 128
W_IN_A = 3 * DIFF_WIDTH + POOL_WIDTH
MLSTM_HEADS = 8
MLSTM_DH = D_MODEL // MLSTM_HEADS
MLSTM_CHUNK = 64
N_GATES = 4
W_IN_C = 4 * D_MODEL + N_GATES * MLSTM_HEADS
D_FF = 2816
N_EXPERTS = 8
TOP_K = 2
D_FF_EXPERT = 1792
LN_EPS = 1e-5

kernel_name = "diffattn_pool_mlstm_moe_prefix_dit_step"


def _layer_norm(x, g, b):
    xf = x.astype(jnp.float32)
    mu = jnp.mean(xf, axis=-1, keepdims=True)
    var = jnp.mean(jnp.square(xf - mu), axis=-1, keepdims=True)
    y = (xf - mu) * lax.rsqrt(var + LN_EPS)
    return (y * g.astype(jnp.float32) + b.astype(jnp.float32)).astype(x.dtype)


def _modulation(cvec, ada_w, ada_b):
    m = jax.nn.silu(cvec) @ ada_w + ada_b
    return jnp.split(m, 6, axis=-1)


def _modulate(x, shift, scale):
    return x * (1.0 + scale) + shift


def _post_norm(x, out, gate, g, b, alpha):
    return _layer_norm(alpha * x + gate * out, g, b)


def _swiglu(h, w_gu, w_down):
    gate, up = jnp.split(h @ w_gu, 2, axis=-1)
    return (jax.nn.silu(gate) * up) @ w_down


def _axial_rope_tables(n_tokens):
    rows = n_tokens // GRID_W
    row_pos = jnp.repeat(jnp.arange(rows), GRID_W).astype(jnp.float32)
    col_pos = jnp.tile(jnp.arange(GRID_W), rows).astype(jnp.float32)
    n_freq = DIFF_DH // 4
    inv_freq = jnp.power(ROPE_BASE, -jnp.arange(n_freq, dtype=jnp.float32) / n_freq)
    ang = jnp.stack([row_pos[:, None] * inv_freq, col_pos[:, None] * inv_freq], axis=1)
    return jnp.cos(ang), jnp.sin(ang)


def _rope2d(x, cos, sin):
    shp = x.shape
    xr = x.reshape(shp[:-1] + (2, 2, DIFF_DH // 4)).astype(jnp.float32)
    x1, x2 = xr[..., 0, :], xr[..., 1, :]
    c, s = cos[:, None], sin[:, None]
    out = jnp.stack([x1 * c - x2 * s, x2 * c + x1 * s], axis=-2)
    return out.reshape(shp).astype(x.dtype)


def _diff_lambda(lam_p, layer_idx):
    lam_init = 0.8 - 0.6 * math.exp(-0.3 * layer_idx)
    lp = lam_p.astype(jnp.float32)
    lam = jnp.exp(jnp.sum(lp[0] * lp[1])) - jnp.exp(jnp.sum(lp[2] * lp[3])) + lam_init
    return lam, lam_init


def _diff_attention(q, k, v, lam, lam_init, norm_g):
    B, H, Tq = q.shape[:3]
    nb = Tq // Q_BLOCK
    scale = DIFF_DH ** -0.5
    qb = jnp.moveaxis(q.reshape(B, H, nb, Q_BLOCK, 2, DIFF_DH), 2, 0)

    def block(qblk):
        s = jnp.einsum('bhqjd,bhkjd->jbhqk', qblk, k, preferred_element_type=jnp.float32) * scale
        p = jax.nn.softmax(s, axis=-1)
        a = p[0] - lam * p[1]
        return jnp.einsum('bhqk,bhkv->bhqv', a.astype(v.dtype), v)

    o = lax.map(block, qb)
    o = jnp.moveaxis(o, 0, 2).reshape(B, H, Tq, 2 * DIFF_DH).astype(jnp.float32)
    o = o * lax.rsqrt(jnp.mean(jnp.square(o), axis=-1, keepdims=True) + LN_EPS)
    o = o * norm_g.astype(jnp.float32) * (1.0 - lam_init)
    return o.astype(v.dtype).transpose(0, 2, 1, 3).reshape(B, Tq, DIFF_WIDTH)


def _multiscale_pool(u, pool_w, pool_scale):
    B, T, _ = u.shape
    uf = u.astype(jnp.float32).reshape(B, T, POOL_GROUPS, POOL_GC)
    csum = jnp.concatenate([jnp.zeros((B, 1, POOL_GROUPS, POOL_GC), jnp.float32),
                            jnp.cumsum(uf, axis=1)], axis=1)
    t = jnp.arange(T)
    outs = []
    for g, w in enumerate(POOL_WINDOWS):
        lo = jnp.clip(t - w // 2, 0, T - 1)
        hi = jnp.clip(t + w // 2 - 1, 0, T - 1)
        cg = csum[:, :, g]
        win_sum = cg[:, hi + 1] - cg[:, lo]
        cnt = (hi - lo + 1).astype(jnp.float32)[None, :, None]
        outs.append(win_sum / cnt - uf[:, :, g])
    pooled = jnp.stack(outs, axis=2)
    mixed = jnp.einsum('btgc,gcd->btgd', pooled, pool_w.astype(jnp.float32))
    return (mixed.reshape(B, T, POOL_WIDTH) * pool_scale.astype(jnp.float32)).astype(u.dtype)


def _split_attn_pool(proj):
    B, T, _ = proj.shape
    q, k, v, p = jnp.split(proj, [DIFF_WIDTH, 2 * DIFF_WIDTH, 3 * DIFF_WIDTH], axis=-1)
    q = q.reshape(B, T, DIFF_HEADS, 2, DIFF_DH).transpose(0, 2, 1, 3, 4)
    k = k.reshape(B, T, DIFF_HEADS, 2, DIFF_DH).transpose(0, 2, 1, 3, 4)
    v = v.reshape(B, T, DIFF_HEADS, 2 * DIFF_DH).transpose(0, 2, 1, 3)
    return q, k, v, p


def _attn_pool_context(h, w_in, lam, lam_init, norm_g, pool_w, pool_scale, w_out):
    B, T, _ = h.shape
    q, k, v, p = _split_attn_pool(h @ w_in)
    a = _diff_attention(q, k, v, lam, lam_init, norm_g)
    out = jnp.concatenate([a, _multiscale_pool(p, pool_w, pool_scale)], axis=-1) @ w_out
    return out, k.reshape(B, DIFF_HEADS, T, 2 * DIFF_DH), v


def _attn_pool_latent(h, ctx_k, ctx_v, w_in, lam, lam_init, norm_g, pool_w, pool_scale, w_out):
    B, T, _ = h.shape
    Tc = ctx_k.shape[2]
    q, k, v, p = _split_attn_pool(h @ w_in)
    cos, sin = _axial_rope_tables(T)
    q = _rope2d(q, cos, sin)
    k = _rope2d(k, cos, sin)
    k_all = jnp.concatenate([ctx_k.reshape(B, DIFF_HEADS, Tc, 2, DIFF_DH).astype(k.dtype), k], axis=2)
    v_all = jnp.concatenate([ctx_v.astype(v.dtype), v], axis=2)
    a = _diff_attention(q, k_all, v_all, lam, lam_init, norm_g)
    return jnp.concatenate([a, _multiscale_pool(p, pool_w, pool_scale)], axis=-1) @ w_out


def _mlstm_chunkwise(q, k, v, ig, lf, C0, n0, m0):
    B, H, T, d = q.shape
    nc = T // MLSTM_CHUNK

    def chunks(x):
        return jnp.moveaxis(x.reshape(x.shape[:2] + (nc, MLSTM_CHUNK) + x.shape[3:]), 2, 0)

    causal = jnp.tril(jnp.ones((MLSTM_CHUNK, MLSTM_CHUNK), dtype=bool))

    def step(carry, xs):
        C, n, m = carry
        qc, kc, vc, ic, fc = xs
        b = jnp.cumsum(fc, axis=-1)
        dmat = jnp.where(causal, b[..., :, None] - b[..., None, :] + ic[..., None, :], -jnp.inf)
        m_inter = b + m[..., None]
        m_t = jnp.maximum(m_inter, jnp.max(dmat, axis=-1))
        w_inter = jnp.exp(m_inter - m_t)
        s = jnp.einsum('bhtd,bhsd->bhts', qc, kc) * jnp.exp(dmat - m_t[..., None])
        num = w_inter[..., None] * jnp.einsum('bhtd,bhde->bhte', qc, C) + jnp.einsum('bhts,bhse->bhte', s, vc)
        den = w_inter * jnp.einsum('bhtd,bhd->bht', qc, n) + jnp.sum(s, axis=-1)
        h = num / jnp.maximum(jnp.abs(den), jnp.exp(-m_t))[..., None]
        m_new = m_t[..., -1]
        w_c = jnp.exp(b[..., -1] + m - m_new)
        w_s = jnp.exp(b[..., -1:] - b + ic - m_new[..., None])
        C_new = w_c[..., None, None] * C + jnp.einsum('bhs,bhsd,bhse->bhde', w_s, kc, vc)
        n_new = w_c[..., None] * n + jnp.einsum('bhs,bhsd->bhd', w_s, kc)
        return (C_new, n_new, m_new), h

    (C, n, m), h = lax.scan(step, (C0, n0, m0), (chunks(q), chunks(k), chunks(v), chunks(ig), chunks(lf)))
    h = jnp.moveaxis(h, 0, 2).reshape(B, H, T, d)
    return h, C, n, m


def _mlstm_mixer(h, w_in, b_gates, head_g, w_out, C0, n0, m0):
    B, T, _ = h.shape
    proj = h @ w_in
    q, k, v, o, g = jnp.split(proj, [D_MODEL, 2 * D_MODEL, 3 * D_MODEL, 4 * D_MODEL], axis=-1)

    def heads(x):
        return x.reshape(B, T, MLSTM_HEADS, MLSTM_DH).transpose(0, 2, 1, 3).astype(jnp.float32)

    q, k, v = heads(q), heads(k) * (MLSTM_DH ** -0.5), heads(v)
    g = (g.astype(jnp.float32) + b_gates.astype(jnp.float32)).reshape(B, T, N_GATES, MLSTM_HEADS)
    g = g.transpose(2, 0, 3, 1)
    ig = g[0::2]
    lf = jax.nn.log_sigmoid(g[1::2])
    C0, n0, m0 = C0.astype(jnp.float32), n0.astype(jnp.float32), m0.astype(jnp.float32)

    def flip(x):
        return jnp.flip(x, axis=2)

    h_f, Cf, nf, mf = _mlstm_chunkwise(q, k, v, ig[0], lf[0], C0[:, 0], n0[:, 0], m0[:, 0])
    h_b, Cb, nb, mb = _mlstm_chunkwise(flip(q), flip(k), flip(v), flip(ig[1]), flip(lf[1]),
                                       C0[:, 1], n0[:, 1], m0[:, 1])
    hs = h_f + flip(h_b)
    mu = jnp.mean(hs, axis=-1, keepdims=True)
    var = jnp.mean(jnp.square(hs - mu), axis=-1, keepdims=True)
    hn = ((hs - mu) * lax.rsqrt(var + LN_EPS)).transpose(0, 2, 1, 3).reshape(B, T, D_MODEL)
    hn = hn * head_g.astype(jnp.float32)
    out = (jax.nn.sigmoid(o.astype(jnp.float32)) * hn).astype(h.dtype) @ w_out
    return out, jnp.stack([Cf, Cb], axis=1), jnp.stack([nf, nb], axis=1), jnp.stack([mf, mb], axis=1)


def _moe_swiglu(h, router_w, router_b, w_gu, w_down):
    logits = jnp.einsum('btd,de->bte', h, router_w, preferred_element_type=jnp.float32)
    logits = logits + router_b.astype(jnp.float32)
    top_val, top_idx = lax.top_k(logits, TOP_K)
    top_w = jax.nn.softmax(top_val, axis=-1)
    combine = jnp.sum(jax.nn.one_hot(top_idx, N_EXPERTS, dtype=jnp.float32) * top_w[..., None], axis=-2)
    out = jnp.zeros(h.shape, jnp.float32)
    for e in range(N_EXPERTS):
        out = out + combine[..., e:e + 1] * _swiglu(h, w_gu[e], w_down[e]).astype(jnp.float32)
    return out.astype(h.dtype)


def setup_inputs(seed: int = 0) -> dict:
    key = jax.random.key(seed)
    keys = list(jax.random.split(key, 40))
    f32 = jnp.float32
    n_attn = (DEPTH + 1) // 2
    n_rec = DEPTH // 2
    beta = (8.0 * DEPTH) ** -0.25

    def normal(i, shape, scale=1.0):
        return scale * jax.random.normal(keys[i], shape, f32)

    H = MLSTM_HEADS
    igate_b = normal(30, (n_rec, 2, 1, H), 0.1)
    fgate_b = jnp.linspace(3.0, 6.0, H, dtype=f32)[None, None, None, :] + normal(31, (n_rec, 2, 1, H), 0.1)
    b_gates_c = jnp.concatenate([igate_b, fgate_b], axis=2).reshape(n_rec, N_GATES * H)
    return {
        "x_prompt": normal(0, (BATCH, SEQ, D_MODEL)),
        "x_sample": normal(1, (DEC_BATCH, DEC_SEQ, D_MODEL)),
        "c": normal(2, (DEC_BATCH, D_MODEL)),
        "cache_k": normal(3, (DEC_BATCH, n_attn, DIFF_HEADS, PAST_LEN, 2 * DIFF_DH)),
        "cache_v": normal(4, (DEC_BATCH, n_attn, DIFF_HEADS, PAST_LEN, 2 * DIFF_DH)),
        "state_C": normal(5, (DEC_BATCH, n_rec, 2, H, MLSTM_DH, MLSTM_DH), 0.1),
        "state_n": normal(6, (DEC_BATCH, n_rec, 2, H, MLSTM_DH), 0.1),
        "state_m": normal(7, (DEC_BATCH, n_rec, 2, H)),
        "c_ctx": normal(8, (D_MODEL,)),
        "ada_w": normal(9, (DEPTH, D_MODEL, 6 * D_MODEL), D_MODEL ** -0.5),
        "ada_b": normal(10, (DEPTH, 6 * D_MODEL), 0.02),
        "ln_g": 1.0 + normal(11, (DEPTH, 2, D_MODEL), 0.02),
        "ln_b": normal(12, (DEPTH, 2, D_MODEL), 0.02),
        "w_in_a": normal(13, (n_attn, D_MODEL, W_IN_A), D_MODEL ** -0.5),
        "diff_lambda": normal(14, (n_attn, 4, DIFF_DH), 0.1),
        "diff_norm_g": 1.0 + normal(15, (n_attn, 2 * DIFF_DH), 0.02),
        "pool_w": normal(16, (n_attn, POOL_GROUPS, POOL_GC, POOL_GC), POOL_GC ** -0.5),
        "pool_scale": 1.0 + normal(17, (n_attn, POOL_WIDTH), 0.1),
        "w_out_a": normal(18, (n_attn, D_MODEL, D_MODEL), beta * D_MODEL ** -0.5),
        "ffn_w_gu": normal(19, (n_attn, D_MODEL, 2 * D_FF), D_MODEL ** -0.5),
        "ffn_w_down": normal(20, (n_attn, D_FF, D_MODEL), beta * D_FF ** -0.5),
        "w_in_c": normal(21, (n_rec, D_MODEL, W_IN_C), D_MODEL ** -0.5),
        "b_gates_c": b_gates_c,
        "mlstm_norm_g": 1.0 + normal(22, (n_rec, D_MODEL), 0.02),
        "w_out_c": normal(23, (n_rec, D_MODEL, D_MODEL), beta * D_MODEL ** -0.5),
        "router_w": normal(24, (n_rec, D_MODEL, N_EXPERTS), D_MODEL ** -0.5),
        "router_b": normal(25, (n_rec, N_EXPERTS), 0.01),
        "moe_w_gu": normal(26, (n_rec, N_EXPERTS, D_MODEL, 2 * D_FF_EXPERT), D_MODEL ** -0.5),
        "moe_w_down": normal(27, (n_rec, N_EXPERTS, D_FF_EXPERT, D_MODEL), beta * D_FF_EXPERT ** -0.5),
    }


def reference(x_prompt, x_sample, c, cache_k, cache_v, state_C, state_n, state_m, c_ctx,
              ada_w, ada_b, ln_g, ln_b,
              w_in_a, diff_lambda, diff_norm_g, pool_w, pool_scale, w_out_a, ffn_w_gu, ffn_w_down,
              w_in_c, b_gates_c, mlstm_norm_g, w_out_c, router_w, router_b, moe_w_gu, moe_w_down):
    alpha = (2.0 * DEPTH) ** 0.25
    Bp = x_prompt.shape[0]
    xp, xs = x_prompt, x_sample
    zero_C = jnp.zeros((Bp, 2, MLSTM_HEADS, MLSTM_DH, MLSTM_DH), jnp.float32)
    zero_n = jnp.zeros((Bp, 2, MLSTM_HEADS, MLSTM_DH), jnp.float32)
    zero_m = jnp.zeros((Bp, 2, MLSTM_HEADS), jnp.float32)
    new_k, new_v, new_C, new_n, new_m = [], [], [], [], []
    for l in range(DEPTH):
        p_mod = _modulation(c_ctx, ada_w[l], ada_b[l])
        s_mod = _modulation(c[:, None, :], ada_w[l], ada_b[l])
        hp = _modulate(xp, p_mod[0], p_mod[1])
        hs = _modulate(xs, s_mod[0], s_mod[1])
        j = l // 2
        if l % 2 == 0:
            lam, lam_init = _diff_lambda(diff_lambda[j], l)
            args = (w_in_a[j], lam, lam_init, diff_norm_g[j], pool_w[j], pool_scale[j], w_out_a[j])
            out_p, k_ctx, v_ctx = _attn_pool_context(hp, *args)
            out_s = _attn_pool_latent(hs, cache_k[:, j], cache_v[:, j], *args)
            new_k.append(k_ctx)
            new_v.append(v_ctx)
        else:
            args = (w_in_c[j], b_gates_c[j], mlstm_norm_g[j], w_out_c[j])
            out_p, Cc, nc, mc = _mlstm_mixer(hp, *args, zero_C, zero_n, zero_m)
            out_s, _, _, _ = _mlstm_mixer(hs, *args, state_C[:, j], state_n[:, j], state_m[:, j])
            new_C.append(Cc)
            new_n.append(nc)
            new_m.append(mc)
        xp = _post_norm(xp, out_p, p_mod[2], ln_g[l, 0], ln_b[l, 0], alpha)
        xs = _post_norm(xs, out_s, s_mod[2], ln_g[l, 0], ln_b[l, 0], alpha)
        hp = _modulate(xp, p_mod[3], p_mod[4])
        hs = _modulate(xs, s_mod[3], s_mod[4])
        if l % 2 == 0:
            ff_p = _swiglu(hp, ffn_w_gu[j], ffn_w_down[j])
            ff_s = _swiglu(hs, ffn_w_gu[j], ffn_w_down[j])
        else:
            ff_p = _moe_swiglu(hp, router_w[j], router_b[j], moe_w_gu[j], moe_w_down[j])
            ff_s = _moe_swiglu(hs, router_w[j], router_b[j], moe_w_gu[j], moe_w_down[j])
        xp = _post_norm(xp, ff_p, p_mod[5], ln_g[l, 1], ln_b[l, 1], alpha)
        xs = _post_norm(xs, ff_s, s_mod[5], ln_g[l, 1], ln_b[l, 1], alpha)
    return (xp, xs, jnp.stack(new_k, axis=1), jnp.stack(new_v, axis=1),
            jnp.stack(new_C, axis=1), jnp.stack(new_n, axis=1), jnp.stack(new_m, axis=1))
```

```python
import functools
import math

import jax
import jax.numpy as jnp
from jax import lax
from jax.experimental import pallas as pl
from jax.experimental.pallas import tpu as pltpu

F32 = jnp.float32
BF16 = jnp.bfloat16

D_MODEL = 1024
GRID_W = 64
ROPE_BASE = 10000.0
DIFF_HEADS = 4
DIFF_DH = 64
DIFF_WIDTH = DIFF_HEADS * 2 * DIFF_DH
POOL_GROUPS = 4
POOL_GC = 128
POOL_WIDTH = POOL_GROUPS * POOL_GC
POOL_WINDOWS = (2, 4, 8, 16)
W_IN_A = 3 * DIFF_WIDTH + POOL_WIDTH
MLSTM_HEADS = 8
MLSTM_DH = 128
MLSTM_CHUNK = 64
N_GATES = 4
D_FF = 2816
N_EXPERTS = 8
D_FF_EXPERT = 1792
LN_EPS = 1e-5
DEPTH = 2
ALPHA = (2.0 * DEPTH) ** 0.25

LANES = 128
FF_CHUNK = 256
VMEM_LIMIT = 56 * 1024 * 1024


def _cparams(sem):
    return pltpu.CompilerParams(dimension_semantics=sem, vmem_limit_bytes=VMEM_LIMIT)


def _resident(shape, index_map):
    return pl.BlockSpec(shape, index_map, pipeline_mode=pl.Buffered(1))


def _layer_norm_rows(z, g, b):
    mu = jnp.mean(z, axis=-1, keepdims=True)
    zc = z - mu
    var = jnp.mean(zc * zc, axis=-1, keepdims=True)
    return zc * lax.rsqrt(var + LN_EPS) * g + b


def _sigmoid(x):
    return 1.0 / (1.0 + jnp.exp(-x))


def _split_bf16(x):
    hi = x.astype(BF16)
    lo = (x - hi.astype(F32)).astype(BF16)
    return hi, lo


def _mod_kernel(c_ref, w_ref, b_ref, o_ref):
    c = c_ref[...]
    h = (c * _sigmoid(c)).astype(BF16)
    o_ref[...] = jnp.dot(h, w_ref[...].astype(BF16), preferred_element_type=F32) + b_ref[...]


def _modulation(cvec, ada_w, ada_b):
    depth, d, n = ada_w.shape
    g = cvec.shape[0]
    tn = 1536
    return pl.pallas_call(
        _mod_kernel,
        out_shape=jax.ShapeDtypeStruct((depth, g, n), F32),
        grid=(depth, n // tn),
        in_specs=[
            pl.BlockSpec((g, d), lambda l, j: (0, 0)),
            pl.BlockSpec((None, d, tn), lambda l, j: (l, 0, j)),
            pl.BlockSpec((None, 1, tn), lambda l, j: (l, 0, j)),
        ],
        out_specs=pl.BlockSpec((None, g, tn), lambda l, j: (l, 0, j)),
        compiler_params=_cparams(("arbitrary", "arbitrary")),
        name="modulation",
    )(cvec, ada_w, ada_b.reshape(depth, 1, n))


def _rot_half16(x):
    lane = lax.broadcasted_iota(jnp.int32, x.shape, 1)
    return jnp.where((lane % 32) < 16, pltpu.roll(x, LANES - 16, 1), pltpu.roll(x, 16, 1))


def _inproj_a_kernel(x_ref, mod_ref, w_ref, cos_ref, sin_ref, o_ref):
    h = x_ref[...] * (1.0 + mod_ref[1:2, :]) + mod_ref[0:1, :]
    p = jnp.dot(h.astype(BF16), w_ref[...], preferred_element_type=F32)
    cos = cos_ref[...]
    sin = sin_ref[...]
    n_rope = 2 * DIFF_WIDTH // LANES
    for j in range(n_rope):
        blk = p[:, j * LANES:(j + 1) * LANES]
        o_ref[:, j * LANES:(j + 1) * LANES] = blk * cos + _rot_half16(blk) * sin
    o_ref[:, n_rope * LANES:] = p[:, n_rope * LANES:]


def _inproj_a(x, mod, w_bf, cos_t, sin_t, n_latent_groups, tm=512):
    g, s, d = x.shape
    n = w_bf.shape[1]

    def table_map(gi, ti):
        return (jnp.where(gi >= n_latent_groups, 1, 0), ti, 0)

    return pl.pallas_call(
        _inproj_a_kernel,
        out_shape=jax.ShapeDtypeStruct((g, s, n), F32),
        grid=(g, s // tm),
        in_specs=[
            pl.BlockSpec((None, tm, d), lambda gi, ti: (gi, ti, 0)),
            pl.BlockSpec((None, 6, d), lambda gi, ti: (gi, 0, 0)),
            _resident((d, n), lambda gi, ti: (0, 0)),
            pl.BlockSpec((None, tm, LANES), table_map),
            pl.BlockSpec((None, tm, LANES), table_map),
        ],
        out_specs=pl.BlockSpec((None, tm, n), lambda gi, ti: (gi, ti, 0)),
        compiler_params=_cparams(("arbitrary", "arbitrary")),
        name="inproj_a",
    )(x, mod, w_bf, cos_t, sin_t)


def _rope_tables(n_tokens):
    rows = n_tokens // GRID_W
    row_pos = jnp.repeat(jnp.arange(rows), GRID_W).astype(F32)
    col_pos = jnp.tile(jnp.arange(GRID_W), rows).astype(F32)
    n_freq = DIFF_DH // 4
    inv_freq = jnp.power(ROPE_BASE, -jnp.arange(n_freq, dtype=F32) / n_freq)
    ang = jnp.stack([row_pos[:, None] * inv_freq, col_pos[:, None] * inv_freq], axis=1)
    cos, sin = jnp.cos(ang), jnp.sin(ang)
    cos64 = jnp.concatenate([cos[:, 0], cos[:, 0], cos[:, 1], cos[:, 1]], axis=-1)
    sin64 = jnp.concatenate([-sin[:, 0], sin[:, 0], -sin[:, 1], sin[:, 1]], axis=-1)
    cos_l = jnp.tile(cos64, (1, LANES // DIFF_DH))
    sin_l = jnp.tile(sin64, (1, LANES // DIFF_DH))
    cos_t = jnp.stack([cos_l, jnp.ones_like(cos_l)])
    sin_t = jnp.stack([sin_l, jnp.zeros_like(sin_l)])
    return cos_t, sin_t


def _diff_attn_kernel(*refs, n_pieces, lam_init, emit_kv):
    lam_ref, ng_ref, q_ref = refs[:3]
    kv_refs = refs[3:3 + 2 * n_pieces]
    o_ref = refs[3 + 2 * n_pieces]

    lp = lam_ref[...]
    lam = (jnp.exp(jnp.sum(lp[0:1] * lp[1:2], axis=-1, keepdims=True))
           - jnp.exp(jnp.sum(lp[2:3] * lp[3:4], axis=-1, keepdims=True)) + lam_init)

    q = q_ref[...] * (DIFF_DH ** -0.5)
    lane = lax.broadcasted_iota(jnp.int32, q.shape, 1)
    q1 = jnp.where(lane < DIFF_DH, q, 0.0).astype(BF16)
    q2 = jnp.where(lane >= DIFF_DH, q, 0.0).astype(BF16)

    nt = (((1,), (1,)), ((), ()))
    s1, s2, vs = [], [], []
    for i in range(n_pieces):
        kb = kv_refs[2 * i][...].astype(BF16)
        vs.append(kv_refs[2 * i + 1][...].astype(BF16))
        s1.append(lax.dot_general(q1, kb, nt, preferred_element_type=F32))
        s2.append(lax.dot_general(q2, kb, nt, preferred_element_type=F32))

    def softmax_pieces(ss):
        m = functools.reduce(jnp.maximum, [jnp.max(s, axis=-1, keepdims=True) for s in ss])
        es = [jnp.exp(s - m) for s in ss]
        l = functools.reduce(jnp.add, [jnp.sum(e, axis=-1, keepdims=True) for e in es])
        return [e / l for e in es]

    p1 = softmax_pieces(s1)
    p2 = softmax_pieces(s2)
    o = None
    for i in range(n_pieces):
        a = (p1[i] - lam * p2[i]).astype(BF16)
        t = jnp.dot(a, vs[i], preferred_element_type=F32)
        o = t if o is None else o + t
    o = o * lax.rsqrt(jnp.mean(o * o, axis=-1, keepdims=True) + LN_EPS)
    o_ref[...] = (o * ng_ref[...] * (1.0 - lam_init)).astype(o_ref.dtype)
    if emit_kv:
        ko_ref, vo_ref = refs[4 + 2 * n_pieces:]
        ko_ref[...] = kv_refs[0][...]
        vo_ref[...] = kv_refs[1][...]


def _attn_context(proj, lam_p, norm_g, n_latent_groups, n_seq, seq, lam_init):
    g, s, _ = proj.shape
    per_group = s // seq
    blk = (None, seq, LANES)

    def tok_map(col0):
        return lambda b, h: (n_latent_groups + b // per_group, b % per_group, col0 + h)

    cache_shape = jax.ShapeDtypeStruct((n_seq, 1, DIFF_HEADS, seq, LANES), F32)
    cache_spec = pl.BlockSpec((None, None, None, seq, LANES), lambda b, h: (b, 0, h, 0, 0))
    out_spec = pl.BlockSpec(blk, lambda b, h: (b // per_group, b % per_group, h))
    return pl.pallas_call(
        functools.partial(_diff_attn_kernel, n_pieces=1, lam_init=lam_init, emit_kv=True),
        out_shape=(jax.ShapeDtypeStruct((g - n_latent_groups, s, DIFF_WIDTH), BF16), cache_shape, cache_shape),
        grid=(n_seq, DIFF_HEADS),
        in_specs=[
            pl.BlockSpec((4, DIFF_DH), lambda b, h: (0, 0)),
            pl.BlockSpec((1, LANES), lambda b, h: (0, 0)),
            pl.BlockSpec(blk, tok_map(0)),
            pl.BlockSpec(blk, tok_map(DIFF_HEADS)),
            pl.BlockSpec(blk, tok_map(2 * DIFF_HEADS)),
        ],
        out_specs=(out_spec, cache_spec, cache_spec),
        compiler_params=_cparams(("arbitrary", "arbitrary")),
        name="attn_context",
    )(lam_p, norm_g, proj, proj, proj)


def _attn_latent(proj, cache_k, cache_v, lam_p, norm_g, n_latent_groups, lam_init, tq=256):
    g, s, _ = proj.shape
    past = cache_k.shape[3]
    cache_spec = pl.BlockSpec((None, None, None, past, LANES), lambda b, h, qi: (b, 0, h, 0, 0))
    return pl.pallas_call(
        functools.partial(_diff_attn_kernel, n_pieces=2, lam_init=lam_init, emit_kv=False),
        out_shape=jax.ShapeDtypeStruct((n_latent_groups, s, DIFF_WIDTH), BF16),
        grid=(n_latent_groups, DIFF_HEADS, s // tq),
        in_specs=[
            pl.BlockSpec((4, DIFF_DH), lambda b, h, qi: (0, 0)),
            pl.BlockSpec((1, LANES), lambda b, h, qi: (0, 0)),
            pl.BlockSpec((None, tq, LANES), lambda b, h, qi: (b, qi, h)),
            cache_spec,
            cache_spec,
            pl.BlockSpec((None, s, LANES), lambda b, h, qi: (b, 0, DIFF_HEADS + h)),
            pl.BlockSpec((None, s, LANES), lambda b, h, qi: (b, 0, 2 * DIFF_HEADS + h)),
        ],
        out_specs=pl.BlockSpec((None, tq, LANES), lambda b, h, qi: (b, qi, h)),
        compiler_params=_cparams(("arbitrary", "arbitrary", "arbitrary")),
        name="attn_latent",
    )(lam_p, norm_g, proj, cache_k, cache_v, proj, proj)


def _pool_kernel(p_ref, w_ref, sc_ref, o_ref, band_ref, *, seq):
    @pl.when((pl.program_id(0) == 0) & (pl.program_id(1) == 0))
    def _():
        t = lax.broadcasted_iota(jnp.int32, (seq, seq), 0)
        s_ = lax.broadcasted_iota(jnp.int32, (seq, seq), 1)
        for gi, w in enumerate(POOL_WINDOWS):
            inside = (s_ >= t - w // 2) & (s_ <= t + w // 2 - 1)
            band_ref[gi] = jnp.where(inside, 1.0, 0.0).astype(BF16)

    tcol = lax.broadcasted_iota(jnp.int32, (seq, 1), 0)
    for gi, w in enumerate(POOL_WINDOWS):
        u = p_ref[:, gi * POOL_GC:(gi + 1) * POOL_GC]
        hi, lo = _split_bf16(u)
        band = band_ref[gi]
        win = (jnp.dot(band, hi, preferred_element_type=F32)
               + jnp.dot(band, lo, preferred_element_type=F32))
        cnt = (jnp.minimum(tcol + (w // 2 - 1), seq - 1) - jnp.maximum(tcol - w // 2, 0) + 1).astype(F32)
        pooled = win / cnt - u
        mixed = jnp.dot(pooled.astype(BF16), w_ref[gi].astype(BF16), preferred_element_type=F32)
        o_ref[:, gi * POOL_GC:(gi + 1) * POOL_GC] = (
            mixed * sc_ref[:, gi * POOL_GC:(gi + 1) * POOL_GC]).astype(o_ref.dtype)


def _pool(proj, pool_w, pool_scale, g0, n_groups, seq):
    g, s, _ = proj.shape
    col = 3 * DIFF_WIDTH // POOL_WIDTH
    return pl.pallas_call(
        functools.partial(_pool_kernel, seq=seq),
        out_shape=jax.ShapeDtypeStruct((n_groups, s, POOL_WIDTH), BF16),
        grid=(n_groups, s // seq),
        in_specs=[
            pl.BlockSpec((None, seq, POOL_WIDTH), lambda gi, ti: (g0 + gi, ti, col)),
            pl.BlockSpec((POOL_GROUPS, POOL_GC, POOL_GC), lambda gi, ti: (0, 0, 0)),
            pl.BlockSpec((1, POOL_WIDTH), lambda gi, ti: (0, 0)),
        ],
        out_specs=pl.BlockSpec((None, seq, POOL_WIDTH), lambda gi, ti: (gi, ti, 0)),
        scratch_shapes=[pltpu.VMEM((POOL_GROUPS, seq, seq), BF16)],
        compiler_params=_cparams(("arbitrary", "arbitrary")),
        name="pool_%d" % seq,
    )(proj, pool_w, pool_scale.reshape(1, POOL_WIDTH))


def _outproj_kernel(*refs, n_in, gate_row, n_latent_groups):
    a_refs = refs[:2 * n_in]
    w_refs = refs[2 * n_in:3 * n_in]
    x_ref, mod_ref, g_ref, b_ref, o_ref = refs[3 * n_in:]
    is_latent = pl.program_id(0) < n_latent_groups
    acc = None
    for i, w_ref in enumerate(w_refs):
        a = jnp.where(is_latent, a_refs[2 * i][...], a_refs[2 * i + 1][...])
        t = jnp.dot(a, w_ref[...], preferred_element_type=F32)
        acc = t if acc is None else acc + t
    z = ALPHA * x_ref[...] + mod_ref[gate_row:gate_row + 1, :] * acc
    o_ref[...] = _layer_norm_rows(z, g_ref[...], b_ref[...])


def _outproj(acts, weights, x, mod, ln_g, ln_b, gate_row, tm=512):
    g, s, d = x.shape
    n_in = len(acts)
    gl = acts[0][0].shape[0]
    in_specs = []
    flat_acts = []
    for a_lat, a_ctx in acts:
        k = a_lat.shape[-1]
        in_specs.append(pl.BlockSpec((None, tm, k), lambda gi, ti: (jnp.minimum(gi, gl - 1), jnp.where(gi < gl, ti, 0), 0)))
        in_specs.append(pl.BlockSpec((None, tm, k), lambda gi, ti: (jnp.maximum(gi - gl, 0), jnp.where(gi < gl, 0, ti), 0)))
        flat_acts += [a_lat, a_ctx]
    in_specs += [_resident(w.shape, lambda gi, ti: (0, 0)) for w in weights]
    in_specs += [
        pl.BlockSpec((None, tm, d), lambda gi, ti: (gi, ti, 0)),
        pl.BlockSpec((None, 6, d), lambda gi, ti: (gi, 0, 0)),
        pl.BlockSpec((1, d), lambda gi, ti: (0, 0)),
        pl.BlockSpec((1, d), lambda gi, ti: (0, 0)),
    ]
    return pl.pallas_call(
        functools.partial(_outproj_kernel, n_in=n_in, gate_row=gate_row, n_latent_groups=gl),
        out_shape=jax.ShapeDtypeStruct((g, s, d), F32),
        grid=(g, s // tm),
        in_specs=in_specs,
        out_specs=pl.BlockSpec((None, tm, d), lambda gi, ti: (gi, ti, 0)),
        compiler_params=_cparams(("arbitrary", "arbitrary")),
        name="outproj",
    )(*flat_acts, *weights, x, mod, ln_g.reshape(1, d), ln_b.reshape(1, d))


def _swiglu_chunks(h_bf, wgu_ref, wd_ref, d_ff):
    acc = None
    for j in range(d_ff // FF_CHUNK):
        lo = j * FF_CHUNK
        gate = jnp.dot(h_bf, wgu_ref[:, lo:lo + FF_CHUNK], preferred_element_type=F32)
        up = jnp.dot(h_bf, wgu_ref[:, d_ff + lo:d_ff + lo + FF_CHUNK], preferred_element_type=F32)
        act = (gate * _sigmoid(gate) * up).astype(BF16)
        t = jnp.dot(act, wd_ref[lo:lo + FF_CHUNK, :], preferred_element_type=F32)
        acc = t if acc is None else acc + t
    return acc


def _ffn_kernel(x_ref, mod_ref, wgu_ref, wd_ref, g_ref, b_ref, o_ref):
    x = x_ref[...]
    h = (x * (1.0 + mod_ref[4:5, :]) + mod_ref[3:4, :]).astype(BF16)
    acc = _swiglu_chunks(h, wgu_ref, wd_ref, D_FF)
    z = ALPHA * x + mod_ref[5:6, :] * acc
    o_ref[...] = _layer_norm_rows(z, g_ref[...], b_ref[...])


def _ffn(x, mod, wgu_bf, wd_bf, ln_g, ln_b, tm=256):
    g, s, d = x.shape
    return pl.pallas_call(
        _ffn_kernel,
        out_shape=jax.ShapeDtypeStruct((g, s, d), F32),
        grid=(g, s // tm),
        in_specs=[
            pl.BlockSpec((None, tm, d), lambda gi, ti: (gi, ti, 0)),
            pl.BlockSpec((None, 6, d), lambda gi, ti: (gi, 0, 0)),
            _resident(wgu_bf.shape, lambda gi, ti: (0, 0)),
            _resident(wd_bf.shape, lambda gi, ti: (0, 0)),
            pl.BlockSpec((1, d), lambda gi, ti: (0, 0)),
            pl.BlockSpec((1, d), lambda gi, ti: (0, 0)),
        ],
        out_specs=pl.BlockSpec((None, tm, d), lambda gi, ti: (gi, ti, 0)),
        compiler_params=_cparams(("arbitrary", "arbitrary")),
        name="ffn",
    )(x, mod, wgu_bf, wd_bf, ln_g.reshape(1, d), ln_b.reshape(1, d))


def _router_combine(h, rw_ref, rb_ref):
    h_hi, h_lo = _split_bf16(h)
    w_hi, w_lo = _split_bf16(rw_ref[...])
    logits = (jnp.dot(h_hi, w_hi, preferred_element_type=F32)
              + jnp.dot(h_lo, w_hi, preferred_element_type=F32)
              + jnp.dot(h_hi, w_lo, preferred_element_type=F32)) + rb_ref[...]
    lane = lax.broadcasted_iota(jnp.int32, logits.shape, 1).astype(F32)
    neg = -jnp.inf
    logits = jnp.where(lane < N_EXPERTS, logits, neg)
    m1 = jnp.max(logits, axis=-1, keepdims=True)
    i1 = jnp.min(jnp.where(logits == m1, lane, float(LANES)), axis=-1, keepdims=True)
    rest = jnp.where(lane == i1, neg, logits)
    m2 = jnp.max(rest, axis=-1, keepdims=True)
    i2 = jnp.min(jnp.where(rest == m2, lane, float(LANES)), axis=-1, keepdims=True)
    e2 = jnp.exp(m2 - m1)
    w1 = 1.0 / (1.0 + e2)
    w2 = e2 / (1.0 + e2)
    return jnp.where(lane == i1, w1, 0.0) + jnp.where(lane == i2, w2, 0.0)


def _moe_kernel(x_ref, mod_ref, rw_ref, rb_ref, wgu_ref, wd_ref, g_ref, b_ref, o_ref,
                h_ref, comb_ref, acc_ref):
    e = pl.program_id(2)

    @pl.when(e == 0)
    def _():
        h = x_ref[...] * (1.0 + mod_ref[4:5, :]) + mod_ref[3:4, :]
        h_ref[...] = h.astype(BF16)
        comb_ref[...] = _router_combine(h, rw_ref, rb_ref)
        acc_ref[...] = jnp.zeros_like(acc_ref)

    y = _swiglu_chunks(h_ref[...], wgu_ref, wd_ref, D_FF_EXPERT)
    comb = comb_ref[...]
    lane = lax.broadcasted_iota(jnp.int32, comb.shape, 1)
    c_e = jnp.sum(jnp.where(lane == e, comb, 0.0), axis=-1, keepdims=True)
    acc_ref[...] += c_e * y

    @pl.when(e == N_EXPERTS - 1)
    def _():
        z = ALPHA * x_ref[...] + mod_ref[5:6, :] * acc_ref[...]
        o_ref[...] = _layer_norm_rows(z, g_ref[...], b_ref[...])


def _moe(x, mod, router_w, router_b, wgu_bf, wd_bf, ln_g, ln_b, tm=512):
    g, s, d = x.shape
    rw = jnp.pad(router_w, ((0, 0), (0, LANES - N_EXPERTS)))
    rb = jnp.pad(router_b, (0, LANES - N_EXPERTS)).reshape(1, LANES)
    return pl.pallas_call(
        _moe_kernel,
        out_shape=jax.ShapeDtypeStruct((g, s, d), F32),
        grid=(g, s // tm, N_EXPERTS),
        in_specs=[
            pl.BlockSpec((None, tm, d), lambda gi, ti, e: (gi, ti, 0)),
            pl.BlockSpec((None, 6, d), lambda gi, ti, e: (gi, 0, 0)),
            pl.BlockSpec((d, LANES), lambda gi, ti, e: (0, 0)),
            pl.BlockSpec((1, LANES), lambda gi, ti, e: (0, 0)),
            pl.BlockSpec((None, d, 2 * D_FF_EXPERT), lambda gi, ti, e: (e, 0, 0)),
            pl.BlockSpec((None, D_FF_EXPERT, d), lambda gi, ti, e: (e, 0, 0)),
            pl.BlockSpec((1, d), lambda gi, ti, e: (0, 0)),
            pl.BlockSpec((1, d), lambda gi, ti, e: (0, 0)),
        ],
        out_specs=pl.BlockSpec((None, tm, d), lambda gi, ti, e: (gi, ti, 0)),
        scratch_shapes=[pltpu.VMEM((tm, d), BF16), pltpu.VMEM((tm, LANES), F32), pltpu.VMEM((tm, d), F32)],
        compiler_params=_cparams(("arbitrary", "arbitrary", "arbitrary")),
        name="moe",
    )(x, mod, rw, rb, wgu_bf, wd_bf, ln_g.reshape(1, d), ln_b.reshape(1, d))


def _inproj_c_kernel(x_ref, mod_ref, w_ref, wg_ref, bg_ref, o_ref, og_ref):
    h = (x_ref[...] * (1.0 + mod_ref[1:2, :]) + mod_ref[0:1, :]).astype(BF16)
    o_ref[...] = jnp.dot(h, w_ref[...], preferred_element_type=F32)
    og_ref[...] = jnp.dot(h, wg_ref[...], preferred_element_type=F32) + bg_ref[...]


def _inproj_c(x, mod, w_bf, wg_bf, bg, tm=512):
    g, s, d = x.shape
    n = w_bf.shape[1]
    return pl.pallas_call(
        _inproj_c_kernel,
        out_shape=(jax.ShapeDtypeStruct((g, s, n), F32), jax.ShapeDtypeStruct((g, s, LANES), F32)),
        grid=(g, s // tm),
        in_specs=[
            pl.BlockSpec((None, tm, d), lambda gi, ti: (gi, ti, 0)),
            pl.BlockSpec((None, 6, d), lambda gi, ti: (gi, 0, 0)),
            _resident((d, n), lambda gi, ti: (0, 0)),
            _resident((d, LANES), lambda gi, ti: (0, 0)),
            pl.BlockSpec((1, LANES), lambda gi, ti: (0, 0)),
        ],
        out_specs=(pl.BlockSpec((None, tm, n), lambda gi, ti: (gi, ti, 0)),
                   pl.BlockSpec((None, tm, LANES), lambda gi, ti: (gi, ti, 0))),
        compiler_params=_cparams(("arbitrary", "arbitrary")),
        name="inproj_c",
    )(x, mod, w_bf, wg_bf, bg)


def _log_sigmoid(x):
    return jnp.minimum(x, 0.0) - jnp.log(1.0 + jnp.exp(-jnp.abs(x)))


def _mlstm_kernel(*refs, seq, hg, has_init, emit_state):
    q_ref, k_ref, v_ref, o_ref, gc_ref, gr_ref, hgain_ref = refs[:7]
    pos = 7
    if has_init:
        c0_ref, n0_ref, m0_ref = refs[pos:pos + 3]
        pos += 3
    out_ref = refs[pos]
    pos += 1
    if emit_state:
        co_ref, no_ref, mo_ref = refs[pos:pos + 3]
        pos += 3
    cext_ref, hf_ref, hb_ref = refs[pos:pos + 3]

    L = MLSTM_CHUNK
    dh = MLSTM_DH
    nc = seq // L
    head0 = pl.program_id(1) * hg
    neg = -jnp.inf

    lane_d = lax.broadcasted_iota(jnp.int32, (dh, dh), 1)
    for d in range(2):
        for hh in range(hg):
            idx = d * hg + hh
            if has_init:
                cext_ref[idx, :, 0:dh] = c0_ref[d, hh]
                cext_ref[idx, :, dh:2 * dh] = jnp.where(lane_d == 0, n0_ref[d, hh], 0.0)
            else:
                cext_ref[idx] = jnp.zeros((dh, 2 * dh), F32)

    row = lax.broadcasted_iota(jnp.int32, (L, L), 0)
    col = lax.broadcasted_iota(jnp.int32, (L, L), 1)
    lower = col <= row
    upper = col >= row
    lane_g = lax.broadcasted_iota(jnp.int32, (L, LANES), 1)
    ones_col = jnp.where(lane_g == 0, 1.0, 0.0).astype(BF16)
    nt = (((1,), (1,)), ((), ()))
    tn = (((0,), (0,)), ((), ()))

    def one_direction(d, hh, c, s_qk, q_bf, k_s, v_ext, v_bf, m_prev):
        idx = d * hg + hh
        head = head0 + hh
        causal, anti = (lower, upper) if d == 0 else (upper, lower)
        gates_c = gc_ref[pl.ds(c * L, L), :]

        def col_of(j):
            return jnp.sum(jnp.where(lane_g == j, gates_c, 0.0), axis=-1, keepdims=True)

        i_c = col_of((2 * d) * MLSTM_HEADS + head)
        f_c = _log_sigmoid(col_of((2 * d + 1) * MLSTM_HEADS + head))
        i_r = gr_ref[2 * d, hh, pl.ds(c, 1), :]
        f_r = _log_sigmoid(gr_ref[2 * d + 1, hh, pl.ds(c, 1), :])

        b_c = jnp.sum(jnp.where(causal, f_r, 0.0), axis=1, keepdims=True)
        b_r = jnp.sum(jnp.where(anti, f_c, 0.0), axis=0, keepdims=True)
        b_tot = jnp.sum(f_r, axis=1, keepdims=True)
        dmat = jnp.where(causal, b_c - b_r + i_r, neg)
        m_inter = b_c + m_prev
        m_t = jnp.maximum(m_inter, jnp.max(dmat, axis=-1, keepdims=True))
        w_inter = jnp.exp(m_inter - m_t)
        p = s_qk * jnp.exp(dmat - m_t)
        qc = jnp.dot(q_bf, cext_ref[idx].astype(BF16), preferred_element_type=F32)
        num = w_inter * qc[:, 0:dh] + jnp.dot(p.astype(BF16), v_bf, preferred_element_type=F32)
        den = w_inter * qc[:, dh:dh + 1] + jnp.sum(p, axis=-1, keepdims=True)
        h = num / jnp.maximum(jnp.abs(den), jnp.exp(-m_t))
        last = L - 1 if d == 0 else 0
        m_new = m_t[last:last + 1, :]
        w_c = jnp.exp(b_tot + m_prev - m_new)
        w_s = jnp.exp(b_tot - b_c + i_c - m_new)
        upd = lax.dot_general((w_s * k_s).astype(BF16), v_ext, tn, preferred_element_type=F32)
        cext_ref[idx] = w_c * cext_ref[idx] + upd
        return h, m_new

    def load_chunk(hh, c):
        sl = (pl.ds(c * L, L), slice(hh * dh, (hh + 1) * dh))
        q_bf = q_ref[sl].astype(BF16)
        k_s = k_ref[sl] * (dh ** -0.5)
        v_bf = v_ref[sl].astype(BF16)
        v_ext = jnp.concatenate([v_bf, ones_col], axis=-1)
        s_qk = lax.dot_general(q_bf, k_s.astype(BF16), nt, preferred_element_type=F32)
        return s_qk, q_bf, k_s, v_ext, v_bf

    def step(c, ms):
        cb = nc - 1 - c
        new_ms = []
        for hh in range(hg):
            h, m_new = one_direction(0, hh, c, *load_chunk(hh, c), ms[hh])
            hf_ref[pl.ds(c * L, L), hh * dh:(hh + 1) * dh] = h
            new_ms.append(m_new)
        for hh in range(hg):
            h, m_new = one_direction(1, hh, cb, *load_chunk(hh, cb), ms[hg + hh])
            hb_ref[pl.ds(cb * L, L), hh * dh:(hh + 1) * dh] = h
            new_ms.append(m_new)
        return tuple(new_ms)

    if has_init:
        ms0 = tuple(m0_ref[d, hh] for d in range(2) for hh in range(hg))
    else:
        ms0 = tuple(jnp.zeros((1, 1), F32) for _ in range(2 * hg))
    ms = lax.fori_loop(0, nc, step, ms0)

    for hh in range(hg):
        cs = slice(hh * dh, (hh + 1) * dh)
        hs = hf_ref[:, cs] + hb_ref[:, cs]
        mu = jnp.mean(hs, axis=-1, keepdims=True)
        hc = hs - mu
        var = jnp.mean(hc * hc, axis=-1, keepdims=True)
        hn = hc * lax.rsqrt(var + LN_EPS) * hgain_ref[:, cs]
        out_ref[:, cs] = (_sigmoid(o_ref[:, cs]) * hn).astype(out_ref.dtype)

    if emit_state:
        for d in range(2):
            for hh in range(hg):
                idx = d * hg + hh
                co_ref[d, hh] = cext_ref[idx, :, 0:dh]
                no_ref[d, hh] = cext_ref[idx, :, dh:dh + 1]
                mo_ref[d, hh] = jnp.broadcast_to(ms[idx], (1, LANES))


def _mlstm(proj, gates, head_g, g0, n_seq, seq, hg, init=None, emit_state=False):
    g, s, _ = proj.shape
    per_group = s // seq
    n_hg = MLSTM_HEADS // hg
    w = hg * MLSTM_DH
    nc = seq // MLSTM_CHUNK
    n_blocks = D_MODEL // w

    g_seq = gates[g0:g0 + n_seq // per_group].reshape(n_seq, seq, LANES)
    g_row = g_seq[:, :, :N_GATES * MLSTM_HEADS].transpose(0, 2, 1).reshape(
        n_seq, N_GATES, MLSTM_HEADS, nc, MLSTM_CHUNK)

    def tok_map(colblock):
        return lambda b, hi: (g0 + b // per_group, b % per_group, colblock * n_blocks + hi)

    args = [proj, proj, proj, proj, g_seq, g_row, head_g.reshape(1, D_MODEL)]
    in_specs = [
        pl.BlockSpec((None, seq, w), tok_map(0)),
        pl.BlockSpec((None, seq, w), tok_map(1)),
        pl.BlockSpec((None, seq, w), tok_map(2)),
        pl.BlockSpec((None, seq, w), tok_map(3)),
        pl.BlockSpec((None, seq, LANES), lambda b, hi: (b, 0, 0)),
        pl.BlockSpec((None, N_GATES, hg, nc, MLSTM_CHUNK), lambda b, hi: (b, 0, hi, 0, 0)),
        pl.BlockSpec((1, w), lambda b, hi: (0, hi)),
    ]
    if init is not None:
        c0, n0, m0 = init
        args += [c0, n0.reshape(n0.shape + (1,)), m0.reshape(m0.shape + (1, 1))]
        in_specs += [
            pl.BlockSpec((None, 2, hg, MLSTM_DH, MLSTM_DH), lambda b, hi: (b, 0, hi, 0, 0)),
            pl.BlockSpec((None, 2, hg, MLSTM_DH, 1), lambda b, hi: (b, 0, hi, 0, 0)),
            pl.BlockSpec((None, 2, hg, 1, 1), lambda b, hi: (b, 0, hi, 0, 0)),
        ]
    out_shape = [jax.ShapeDtypeStruct((n_seq // per_group, s, D_MODEL), BF16)]
    out_specs = [pl.BlockSpec((None, seq, w), lambda b, hi: (b // per_group, b % per_group, hi))]
    if emit_state:
        out_shape += [
            jax.ShapeDtypeStruct((n_seq, 2, MLSTM_HEADS, MLSTM_DH, MLSTM_DH), F32),
            jax.ShapeDtypeStruct((n_seq, 2, MLSTM_HEADS, MLSTM_DH, 1), F32),
            jax.ShapeDtypeStruct((n_seq, 2, MLSTM_HEADS, 1, LANES), F32),
        ]
        out_specs += [
            pl.BlockSpec((None, 2, hg, MLSTM_DH, MLSTM_DH), lambda b, hi: (b, 0, hi, 0, 0)),
            pl.BlockSpec((None, 2, hg, MLSTM_DH, 1), lambda b, hi: (b, 0, hi, 0, 0)),
            pl.BlockSpec((None, 2, hg, 1, LANES), lambda b, hi: (b, 0, hi, 0, 0)),
        ]

    return pl.pallas_call(
        functools.partial(_mlstm_kernel, seq=seq, hg=hg, has_init=init is not None, emit_state=emit_state),
        out_shape=tuple(out_shape),
        grid=(n_seq, n_hg),
        in_specs=in_specs,
        out_specs=tuple(out_specs),
        scratch_shapes=[
            pltpu.VMEM((2 * hg, MLSTM_DH, 2 * MLSTM_DH), F32),
            pltpu.VMEM((seq, w), F32),
            pltpu.VMEM((seq, w), F32),
        ],
        compiler_params=_cparams(("arbitrary", "arbitrary")),
        name="mlstm_%d" % seq,
    )(*args)


def kernel(x_prompt, x_sample, c, cache_k, cache_v, state_C, state_n, state_m, c_ctx, ada_w, ada_b, ln_g, ln_b, w_in_a, diff_lambda, diff_norm_g, pool_w, pool_scale, w_out_a, ffn_w_gu, ffn_w_down, w_in_c, b_gates_c, mlstm_norm_g, w_out_c, router_w, router_b, moe_w_gu, moe_w_down):
    n_ctx, seq_ctx, d = x_prompt.shape
    n_lat, seq_lat, _ = x_sample.shape
    assert d == D_MODEL and (n_ctx * seq_ctx) % seq_lat == 0 and seq_lat % seq_ctx == 0
    gl = n_lat
    gc = n_ctx * seq_ctx // seq_lat
    s = seq_lat

    x = jnp.concatenate([x_sample, x_prompt.reshape(gc, s, d)], axis=0)
    cvec = jnp.concatenate([c, jnp.broadcast_to(c_ctx[None, :], (gc, d))], axis=0)
    mod_all = _modulation(cvec, ada_w, ada_b).reshape(DEPTH, gl + gc, 6, d)

    mod = mod_all[0]
    lam_init = 0.8 - 0.6 * math.exp(-0.3 * 0)
    cos_t, sin_t = _rope_tables(s)
    proj = _inproj_a(x, mod, w_in_a[0].astype(BF16), cos_t, sin_t, gl)
    norm_g = diff_norm_g[0].reshape(1, LANES)
    attn_c, new_k, new_v = _attn_context(proj, diff_lambda[0], norm_g, gl, n_ctx, seq_ctx, lam_init)
    attn_l = _attn_latent(proj, cache_k, cache_v, diff_lambda[0], norm_g, gl, lam_init)
    pool_c = _pool(proj, pool_w[0], pool_scale[0], gl, gc, seq_ctx)
    pool_l = _pool(proj, pool_w[0], pool_scale[0], 0, gl, seq_lat)
    w_out = w_out_a[0].astype(BF16)
    x = _outproj([(attn_l, attn_c), (pool_l, pool_c)], [w_out[:DIFF_WIDTH], w_out[DIFF_WIDTH:]],
                 x, mod, ln_g[0, 0], ln_b[0, 0], 2)
    x = _ffn(x, mod, ffn_w_gu[0].astype(BF16), ffn_w_down[0].astype(BF16), ln_g[0, 1], ln_b[0, 1])

    mod = mod_all[1]
    n_main = 4 * D_MODEL
    w_main = w_in_c[0][:, :n_main].astype(BF16)
    n_g = N_GATES * MLSTM_HEADS
    w_gate = jnp.pad(w_in_c[0][:, n_main:], ((0, 0), (0, LANES - n_g))).astype(BF16)
    b_gate = jnp.pad(b_gates_c[0], (0, LANES - n_g)).reshape(1, LANES)
    proj, gates = _inproj_c(x, mod, w_main, w_gate, b_gate)
    mix_c, new_c, new_n, new_m = _mlstm(proj, gates, mlstm_norm_g[0], gl, n_ctx, seq_ctx, 4, emit_state=True)
    (mix_l,) = _mlstm(proj, gates, mlstm_norm_g[0], 0, n_lat, seq_lat, 4,
                      init=(state_C[:, 0], state_n[:, 0], state_m[:, 0]))
    x = _outproj([(mix_l, mix_c)], [w_out_c[0].astype(BF16)], x, mod, ln_g[1, 0], ln_b[1, 0], 2)
    x = _moe(x, mod, router_w[0], router_b[0], moe_w_gu[0].astype(BF16), moe_w_down[0].astype(BF16),
             ln_g[1, 1], ln_b[1, 1])

    y_sample = x[:gl]
    y_prompt = x[gl:].reshape(n_ctx, seq_ctx, d)
    return (y_prompt, y_sample, new_k, new_v,
            new_c[:, None], new_n[..., 0][:, None], new_m[..., 0, 0][:, None])
```

```python
import functools
import math

import jax
import jax.numpy as jnp
from jax import lax
from jax.experimental import pallas as pl
from jax.experimental.pallas import tpu as pltpu

F32 = jnp.float32
BF16 = jnp.bfloat16

D_MODEL = 1024
GRID_W = 64
ROPE_BASE = 10000.0
DIFF_HEADS = 4
DIFF_DH = 64
DIFF_WIDTH = DIFF_HEADS * 2 * DIFF_DH
POOL_GROUPS = 4
POOL_GC = 128
POOL_WIDTH = POOL_GROUPS * POOL_GC
POOL_WINDOWS = (2, 4, 8, 16)
W_IN_A = 3 * DIFF_WIDTH + POOL_WIDTH
MLSTM_HEADS = 8
MLSTM_DH = 128
MLSTM_CHUNK = 64
N_GATES = 4
D_FF = 2816
N_EXPERTS = 8
D_FF_EXPERT = 1792
LN_EPS = 1e-5
DEPTH = 2
ALPHA = (2.0 * DEPTH) ** 0.25

LANES = 128
FF_CHUNK = 256
VMEM_LIMIT = 56 * 1024 * 1024


def _cparams(sem):
    return pltpu.CompilerParams(dimension_semantics=sem, vmem_limit_bytes=VMEM_LIMIT)


def _resident(shape, index_map):
    return pl.BlockSpec(shape, index_map, pipeline_mode=pl.Buffered(1))


def _layer_norm_rows(z, g, b):
    mu = jnp.mean(z, axis=-1, keepdims=True)
    zc = z - mu
    var = jnp.mean(zc * zc, axis=-1, keepdims=True)
    return zc * lax.rsqrt(var + LN_EPS) * g + b


def _sigmoid(x):
    return 1.0 / (1.0 + jnp.exp(-x))


def _split_bf16(x):
    hi = x.astype(BF16)
    lo = (x - hi.astype(F32)).astype(BF16)
    return hi, lo


def _mod_kernel(c_ref, w_ref, b_ref, o_ref):
    c = c_ref[...]
    h = (c * _sigmoid(c)).astype(BF16)
    o_ref[...] = jnp.dot(h, w_ref[...].astype(BF16), preferred_element_type=F32) + b_ref[...]


def _modulation(cvec, ada_w, ada_b):
    depth, d, n = ada_w.shape
    g = cvec.shape[0]
    tn = 1536
    return pl.pallas_call(
        _mod_kernel,
        out_shape=jax.ShapeDtypeStruct((depth, g, n), F32),
        grid=(depth, n // tn),
        in_specs=[
            pl.BlockSpec((g, d), lambda l, j: (0, 0)),
            pl.BlockSpec((None, d, tn), lambda l, j: (l, 0, j)),
            pl.BlockSpec((None, 1, tn), lambda l, j: (l, 0, j)),
        ],
        out_specs=pl.BlockSpec((None, g, tn), lambda l, j: (l, 0, j)),
        compiler_params=_cparams(("arbitrary", "arbitrary")),
        name="modulation",
    )(cvec, ada_w, ada_b.reshape(depth, 1, n))


def _rot_half16(x):
    lane = lax.broadcasted_iota(jnp.int32, x.shape, 1)
    return jnp.where((lane % 32) < 16, pltpu.roll(x, LANES - 16, 1), pltpu.roll(x, 16, 1))


def _inproj_a_kernel(x_ref, mod_ref, w_ref, cos_ref, sin_ref, o_ref):
    h = x_ref[...] * (1.0 + mod_ref[1:2, :]) + mod_ref[0:1, :]
    p = jnp.dot(h.astype(BF16), w_ref[...], preferred_element_type=F32)
    cos = cos_ref[...]
    sin = sin_ref[...]
    n_rope = 2 * DIFF_WIDTH // LANES
    for j in range(n_rope):
        blk = p[:, j * LANES:(j + 1) * LANES]
        o_ref[:, j * LANES:(j + 1) * LANES] = blk * cos + _rot_half16(blk) * sin
    o_ref[:, n_rope * LANES:] = p[:, n_rope * LANES:]


def _inproj_a(x, mod, w_bf, cos_t, sin_t, n_latent_groups, tm=512):
    g, s, d = x.shape
    n = w_bf.shape[1]

    def table_map(gi, ti):
        return (jnp.where(gi >= n_latent_groups, 1, 0), ti, 0)

    return pl.pallas_call(
        _inproj_a_kernel,
        out_shape=jax.ShapeDtypeStruct((g, s, n), F32),
        grid=(g, s // tm),
        in_specs=[
            pl.BlockSpec((None, tm, d), lambda gi, ti: (gi, ti, 0)),
            pl.BlockSpec((None, 6, d), lambda gi, ti: (gi, 0, 0)),
            _resident((d, n), lambda gi, ti: (0, 0)),
            pl.BlockSpec((None, tm, LANES), table_map),
            pl.BlockSpec((None, tm, LANES), table_map),
        ],
        out_specs=pl.BlockSpec((None, tm, n), lambda gi, ti: (gi, ti, 0)),
        compiler_params=_cparams(("arbitrary", "arbitrary")),
        name="inproj_a",
    )(x, mod, w_bf, cos_t, sin_t)


def _rope_tables(n_tokens):
    rows = n_tokens // GRID_W
    row_pos = jnp.repeat(jnp.arange(rows), GRID_W).astype(F32)
    col_pos = jnp.tile(jnp.arange(GRID_W), rows).astype(F32)
    n_freq = DIFF_DH // 4
    inv_freq = jnp.power(ROPE_BASE, -jnp.arange(n_freq, dtype=F32) / n_freq)
    ang = jnp.stack([row_pos[:, None] * inv_freq, col_pos[:, None] * inv_freq], axis=1)
    cos, sin = jnp.cos(ang), jnp.sin(ang)
    cos64 = jnp.concatenate([cos[:, 0], cos[:, 0], cos[:, 1], cos[:, 1]], axis=-1)
    sin64 = jnp.concatenate([-sin[:, 0], sin[:, 0], -sin[:, 1], sin[:, 1]], axis=-1)
    cos_l = jnp.tile(cos64, (1, LANES // DIFF_DH))
    sin_l = jnp.tile(sin64, (1, LANES // DIFF_DH))
    cos_t = jnp.stack([cos_l, jnp.ones_like(cos_l)])
    sin_t = jnp.stack([sin_l, jnp.zeros_like(sin_l)])
    return cos_t, sin_t


def _diff_attn_kernel(*refs, n_pieces, lam_init, emit_kv):
    lam_ref, ng_ref, q_ref = refs[:3]
    kv_refs = refs[3:3 + 2 * n_pieces]
    o_ref = refs[3 + 2 * n_pieces]

    lp = lam_ref[...]
    lam = (jnp.exp(jnp.sum(lp[0:1] * lp[1:2], axis=-1, keepdims=True))
           - jnp.exp(jnp.sum(lp[2:3] * lp[3:4], axis=-1, keepdims=True)) + lam_init)

    q = q_ref[...] * (DIFF_DH ** -0.5)
    lane = lax.broadcasted_iota(jnp.int32, q.shape, 1)
    q1 = jnp.where(lane < DIFF_DH, q, 0.0).astype(BF16)
    q2 = jnp.where(lane >= DIFF_DH, q, 0.0).astype(BF16)

    nt = (((1,), (1,)), ((), ()))
    s1, s2, vs = [], [], []
    for i in range(n_pieces):
        kb = kv_refs[2 * i][...].astype(BF16)
        vs.append(kv_refs[2 * i + 1][...].astype(BF16))
        s1.append(lax.dot_general(q1, kb, nt, preferred_element_type=F32))
        s2.append(lax.dot_general(q2, kb, nt, preferred_element_type=F32))

    def softmax_pieces(ss):
        m = functools.reduce(jnp.maximum, [jnp.max(s, axis=-1, keepdims=True) for s in ss])
        es = [jnp.exp(s - m) for s in ss]
        l = functools.reduce(jnp.add, [jnp.sum(e, axis=-1, keepdims=True) for e in es])
        return [e / l for e in es]

    p1 = softmax_pieces(s1)
    p2 = softmax_pieces(s2)
    o = None
    for i in range(n_pieces):
        a = (p1[i] - lam * p2[i]).astype(BF16)
        t = jnp.dot(a, vs[i], preferred_element_type=F32)
        o = t if o is None else o + t
    o = o * lax.rsqrt(jnp.mean(o * o, axis=-1, keepdims=True) + LN_EPS)
    o_ref[...] = (o * ng_ref[...] * (1.0 - lam_init)).astype(o_ref.dtype)
    if emit_kv:
        ko_ref, vo_ref = refs[4 + 2 * n_pieces:]
        ko_ref[...] = kv_refs[0][...]
        vo_ref[...] = kv_refs[1][...]


def _attn_context(proj, lam_p, norm_g, n_latent_groups, n_seq, seq, lam_init):
    g, s, _ = proj.shape
    per_group = s // seq
    blk = (None, seq, LANES)

    def tok_map(col0):
        return lambda b, h: (n_latent_groups + b // per_group, b % per_group, col0 + h)

    cache_shape = jax.ShapeDtypeStruct((n_seq, 1, DIFF_HEADS, seq, LANES), F32)
    cache_spec = pl.BlockSpec((None, None, None, seq, LANES), lambda b, h: (b, 0, h, 0, 0))
    out_spec = pl.BlockSpec(blk, lambda b, h: (b // per_group, b % per_group, h))
    return pl.pallas_call(
        functools.partial(_diff_attn_kernel, n_pieces=1, lam_init=lam_init, emit_kv=True),
        out_shape=(jax.ShapeDtypeStruct((g - n_latent_groups, s, DIFF_WIDTH), BF16), cache_shape, cache_shape),
        grid=(n_seq, DIFF_HEADS),
        in_specs=[
            pl.BlockSpec((4, DIFF_DH), lambda b, h: (0, 0)),
            pl.BlockSpec((1, LANES), lambda b, h: (0, 0)),
            pl.BlockSpec(blk, tok_map(0)),
            pl.BlockSpec(blk, tok_map(DIFF_HEADS)),
            pl.BlockSpec(blk, tok_map(2 * DIFF_HEADS)),
        ],
        out_specs=(out_spec, cache_spec, cache_spec),
        compiler_params=_cparams(("arbitrary", "arbitrary")),
        name="attn_context",
    )(lam_p, norm_g, proj, proj, proj)


def _attn_latent(proj, cache_k, cache_v, lam_p, norm_g, n_latent_groups, lam_init, tq=256):
    g, s, _ = proj.shape
    past = cache_k.shape[3]
    cache_spec = pl.BlockSpec((None, None, None, past, LANES), lambda b, h, qi: (b, 0, h, 0, 0))
    return pl.pallas_call(
        functools.partial(_diff_attn_kernel, n_pieces=2, lam_init=lam_init, emit_kv=False),
        out_shape=jax.ShapeDtypeStruct((n_latent_groups, s, DIFF_WIDTH), BF16),
        grid=(n_latent_groups, DIFF_HEADS, s // tq),
        in_specs=[
            pl.BlockSpec((4, DIFF_DH), lambda b, h, qi: (0, 0)),
            pl.BlockSpec((1, LANES), lambda b, h, qi: (0, 0)),
            pl.BlockSpec((None, tq, LANES), lambda b, h, qi: (b, qi, h)),
            cache_spec,
            cache_spec,
            pl.BlockSpec((None, s, LANES), lambda b, h, qi: (b, 0, DIFF_HEADS + h)),
            pl.BlockSpec((None, s, LANES), lambda b, h, qi: (b, 0, 2 * DIFF_HEADS + h)),
        ],
        out_specs=pl.BlockSpec((None, tq, LANES), lambda b, h, qi: (b, qi, h)),
        compiler_params=_cparams(("arbitrary", "arbitrary", "arbitrary")),
        name="attn_latent",
    )(lam_p, norm_g, proj, cache_k, cache_v, proj, proj)


def _pool_kernel(p_ref, w_ref, sc_ref, o_ref, band_ref, *, seq):
    @pl.when((pl.program_id(0) == 0) & (pl.program_id(1) == 0))
    def _():
        t = lax.broadcasted_iota(jnp.int32, (seq, seq), 0)
        s_ = lax.broadcasted_iota(jnp.int32, (seq, seq), 1)
        for gi, w in enumerate(POOL_WINDOWS):
            inside = (s_ >= t - w // 2) & (s_ <= t + w // 2 - 1)
            band_ref[gi] = jnp.where(inside, 1.0, 0.0).astype(BF16)

    tcol = lax.broadcasted_iota(jnp.int32, (seq, 1), 0)
    for gi, w in enumerate(POOL_WINDOWS):
        u = p_ref[:, gi * POOL_GC:(gi + 1) * POOL_GC]
        hi, lo = _split_bf16(u)
        band = band_ref[gi]
        win = (jnp.dot(band, hi, preferred_element_type=F32)
               + jnp.dot(band, lo, preferred_element_type=F32))
        cnt = (jnp.minimum(tcol + (w // 2 - 1), seq - 1) - jnp.maximum(tcol - w // 2, 0) + 1).astype(F32)
        pooled = win / cnt - u
        mixed = jnp.dot(pooled.astype(BF16), w_ref[gi].astype(BF16), preferred_element_type=F32)
        o_ref[:, gi * POOL_GC:(gi + 1) * POOL_GC] = (
            mixed * sc_ref[:, gi * POOL_GC:(gi + 1) * POOL_GC]).astype(o_ref.dtype)


def _pool(proj, pool_w, pool_scale, g0, n_groups, seq):
    g, s, _ = proj.shape
    col = 3 * DIFF_WIDTH // POOL_WIDTH
    return pl.pallas_call(
        functools.partial(_pool_kernel, seq=seq),
        out_shape=jax.ShapeDtypeStruct((n_groups, s, POOL_WIDTH), BF16),
        grid=(n_groups, s // seq),
        in_specs=[
            pl.BlockSpec((None, seq, POOL_WIDTH), lambda gi, ti: (g0 + gi, ti, col)),
            pl.BlockSpec((POOL_GROUPS, POOL_GC, POOL_GC), lambda gi, ti: (0, 0, 0)),
            pl.BlockSpec((1, POOL_WIDTH), lambda gi, ti: (0, 0)),
        ],
        out_specs=pl.BlockSpec((None, seq, POOL_WIDTH), lambda gi, ti: (gi, ti, 0)),
        scratch_shapes=[pltpu.VMEM((POOL_GROUPS, seq, seq), BF16)],
        compiler_params=_cparams(("arbitrary", "arbitrary")),
        name="pool_%d" % seq,
    )(proj, pool_w, pool_scale.reshape(1, POOL_WIDTH))


def _outproj_kernel(*refs, n_in, gate_row, n_latent_groups):
    a_refs = refs[:2 * n_in]
    w_refs = refs[2 * n_in:3 * n_in]
    x_ref, mod_ref, g_ref, b_ref, o_ref = refs[3 * n_in:]
    is_latent = pl.program_id(0) < n_latent_groups
    acc = None
    for i, w_ref in enumerate(w_refs):
        a = jnp.where(is_latent, a_refs[2 * i][...], a_refs[2 * i + 1][...])
        t = jnp.dot(a, w_ref[...], preferred_element_type=F32)
        acc = t if acc is None else acc + t
    z = ALPHA * x_ref[...] + mod_ref[gate_row:gate_row + 1, :] * acc
    o_ref[...] = _layer_norm_rows(z, g_ref[...], b_ref[...])


def _outproj(acts, weights, x, mod, ln_g, ln_b, gate_row, tm=512):
    g, s, d = x.shape
    n_in = len(acts)
    gl = acts[0][0].shape[0]
    in_specs = []
    flat_acts = []
    for a_lat, a_ctx in acts:
        k = a_lat.shape[-1]
        in_specs.append(pl.BlockSpec((None, tm, k), lambda gi, ti: (jnp.minimum(gi, gl - 1), jnp.where(gi < gl, ti, 0), 0)))
        in_specs.append(pl.BlockSpec((None, tm, k), lambda gi, ti: (jnp.maximum(gi - gl, 0), jnp.where(gi < gl, 0, ti), 0)))
        flat_acts += [a_lat, a_ctx]
    in_specs += [_resident(w.shape, lambda gi, ti: (0, 0)) for w in weights]
    in_specs += [
        pl.BlockSpec((None, tm, d), lambda gi, ti: (gi, ti, 0)),
        pl.BlockSpec((None, 6, d), lambda gi, ti: (gi, 0, 0)),
        pl.BlockSpec((1, d), lambda gi, ti: (0, 0)),
        pl.BlockSpec((1, d), lambda gi, ti: (0, 0)),
    ]
    return pl.pallas_call(
        functools.partial(_outproj_kernel, n_in=n_in, gate_row=gate_row, n_latent_groups=gl),
        out_shape=jax.ShapeDtypeStruct((g, s, d), F32),
        grid=(g, s // tm),
        in_specs=in_specs,
        out_specs=pl.BlockSpec((None, tm, d), lambda gi, ti: (gi, ti, 0)),
        compiler_params=_cparams(("arbitrary", "arbitrary")),
        name="outproj",
    )(*flat_acts, *weights, x, mod, ln_g.reshape(1, d), ln_b.reshape(1, d))


def _swiglu_chunks(h_bf, wgu_ref, wd_ref, d_ff):
    acc = None
    for j in range(d_ff // FF_CHUNK):
        lo = j * FF_CHUNK
        gate = jnp.dot(h_bf, wgu_ref[:, lo:lo + FF_CHUNK], preferred_element_type=F32)
        up = jnp.dot(h_bf, wgu_ref[:, d_ff + lo:d_ff + lo + FF_CHUNK], preferred_element_type=F32)
        act = (gate * _sigmoid(gate) * up).astype(BF16)
        t = jnp.dot(act, wd_ref[lo:lo + FF_CHUNK, :], preferred_element_type=F32)
        acc = t if acc is None else acc + t
    return acc


def _ffn_kernel(x_ref, mod_ref, wgu_ref, wd_ref, g_ref, b_ref, o_ref):
    x = x_ref[...]
    h = (x * (1.0 + mod_ref[4:5, :]) + mod_ref[3:4, :]).astype(BF16)
    acc = _swiglu_chunks(h, wgu_ref, wd_ref, D_FF)
    z = ALPHA * x + mod_ref[5:6, :] * acc
    o_ref[...] = _layer_norm_rows(z, g_ref[...], b_ref[...])


def _ffn(x, mod, wgu_bf, wd_bf, ln_g, ln_b, tm=256):
    g, s, d = x.shape
    return pl.pallas_call(
        _ffn_kernel,
        out_shape=jax.ShapeDtypeStruct((g, s, d), F32),
        grid=(g, s // tm),
        in_specs=[
            pl.BlockSpec((None, tm, d), lambda gi, ti: (gi, ti, 0)),
            pl.BlockSpec((None, 6, d), lambda gi, ti: (gi, 0, 0)),
            _resident(wgu_bf.shape, lambda gi, ti: (0, 0)),
            _resident(wd_bf.shape, lambda gi, ti: (0, 0)),
            pl.BlockSpec((1, d), lambda gi, ti: (0, 0)),
            pl.BlockSpec((1, d), lambda gi, ti: (0, 0)),
        ],
        out_specs=pl.BlockSpec((None, tm, d), lambda gi, ti: (gi, ti, 0)),
        compiler_params=_cparams(("arbitrary", "arbitrary")),
        name="ffn",
    )(x, mod, wgu_bf, wd_bf, ln_g.reshape(1, d), ln_b.reshape(1, d))


def _router_combine(h, rw_ref, rb_ref):
    h_hi, h_lo = _split_bf16(h)
    w_hi, w_lo = _split_bf16(rw_ref[...])
    logits = (jnp.dot(h_hi, w_hi, preferred_element_type=F32)
              + jnp.dot(h_lo, w_hi, preferred_element_type=F32)
              + jnp.dot(h_hi, w_lo, preferred_element_type=F32)) + rb_ref[...]
    lane = lax.broadcasted_iota(jnp.int32, logits.shape, 1).astype(F32)
    neg = -jnp.inf
    logits = jnp.where(lane < N_EXPERTS, logits, neg)
    m1 = jnp.max(logits, axis=-1, keepdims=True)
    i1 = jnp.min(jnp.where(logits == m1, lane, float(LANES)), axis=-1, keepdims=True)
    rest = jnp.where(lane == i1, neg, logits)
    m2 = jnp.max(rest, axis=-1, keepdims=True)
    i2 = jnp.min(jnp.where(rest == m2, lane, float(LANES)), axis=-1, keepdims=True)
    e2 = jnp.exp(m2 - m1)
    w1 = 1.0 / (1.0 + e2)
    w2 = e2 / (1.0 + e2)
    member = (lane == i1) | (lane == i2)
    return jnp.where(lane == i1, w1, 0.0) + jnp.where(lane == i2, w2, 0.0), member


MOE_BM = 256
MOE_TC = 256
MOE_TMC = 512


def _route_kernel(x_ref, mod_ref, rw_ref, rb_ref, h_ref, cw_ref, srank_ref, cnt_ref, tri_ref, run_ref):
    tm = x_ref.shape[0]

    @pl.when(pl.program_id(0) == 0)
    def _():
        r = lax.broadcasted_iota(jnp.int32, (tm, tm), 0)
        c = lax.broadcasted_iota(jnp.int32, (tm, tm), 1)
        tri_ref[...] = jnp.where(c <= r, 1.0, 0.0).astype(BF16)
        run_ref[...] = jnp.zeros_like(run_ref)

    h = x_ref[...] * (1.0 + mod_ref[4:5, :]) + mod_ref[3:4, :]
    h_ref[...] = h.astype(BF16)
    cw, member = _router_combine(h, rw_ref, rb_ref)
    cw_ref[...] = cw
    mem = jnp.where(member, 1.0, 0.0)
    rank = jnp.dot(tri_ref[...], mem.astype(BF16), preferred_element_type=F32) + run_ref[...]
    srank_ref[...] = jnp.where(member, rank, -rank)
    run_ref[...] = rank[tm - 1:tm, :]
    cnt_ref[...] = rank[tm - 1:tm, :]


def _route(x, mod, router_w, router_b, tm=512):
    g, s, d = x.shape
    n = g * s
    per = s // tm
    rw = jnp.pad(router_w, ((0, 0), (0, LANES - N_EXPERTS)))
    rb = jnp.pad(router_b, (0, LANES - N_EXPERTS)).reshape(1, LANES)
    return pl.pallas_call(
        _route_kernel,
        out_shape=(jax.ShapeDtypeStruct((n, d), BF16), jax.ShapeDtypeStruct((n, LANES), F32),
                   jax.ShapeDtypeStruct((n, LANES), F32), jax.ShapeDtypeStruct((1, LANES), F32)),
        grid=(n // tm,),
        in_specs=[
            pl.BlockSpec((None, tm, d), lambda i: (i // per, i % per, 0)),
            pl.BlockSpec((None, 6, d), lambda i: (i // per, 0, 0)),
            pl.BlockSpec((d, LANES), lambda i: (0, 0)),
            pl.BlockSpec((1, LANES), lambda i: (0, 0)),
        ],
        out_specs=(pl.BlockSpec((tm, d), lambda i: (i, 0)), pl.BlockSpec((tm, LANES), lambda i: (i, 0)),
                   pl.BlockSpec((tm, LANES), lambda i: (i, 0)), pl.BlockSpec((1, LANES), lambda i: (0, 0))),
        scratch_shapes=[pltpu.VMEM((tm, tm), BF16), pltpu.VMEM((1, LANES), F32)],
        compiler_params=_cparams(("arbitrary",)),
        name="moe_route",
    )(x, mod, rw, rb)


def _moe_plan(srank, counts, n_blocks, n_items):
    e_n = N_EXPERTS
    n = srank.shape[0]
    i32 = jnp.int32
    cnt = counts[0, :e_n].astype(i32)
    nb = (cnt + MOE_BM - 1) // MOE_BM
    nb_incl = jnp.cumsum(nb)
    gstart = nb_incl - nb
    n_used = nb_incl[-1]
    sr = srank[:, :e_n]
    rank = jnp.abs(sr).astype(i32)
    pos_tok = jnp.where(sr > 0, rank - 1 + MOE_BM * gstart[None, :], -1)
    rank_t = rank.T

    r = jnp.arange(n_blocks, dtype=i32)
    used = r < n_used
    rc = jnp.minimum(r, n_used - 1)
    e_r = jnp.minimum(jnp.sum(nb_incl[None, :] <= rc[:, None], axis=1, dtype=i32), e_n - 1)
    b = rc - gstart[e_r]
    lo = b * MOE_BM + 1
    hi = jnp.minimum((b + 1) * MOE_BM, cnt[e_r])
    def find(e, v):
        return jnp.sum(rank_t[e] < v[:, None], axis=1, dtype=i32)

    jlo = jnp.where(used, find(e_r, lo) // MOE_TC, 0)
    jhi = jnp.where(used, find(e_r, hi) // MOE_TC, -1)

    n_tiles = n // MOE_TMC
    ends = rank[MOE_TMC - 1::MOE_TMC]
    starts = jnp.concatenate([jnp.zeros((1, e_n), i32), ends[:-1]], axis=0)
    fb = gstart[None, :] + starts // MOE_BM
    lb = gstart[None, :] + (ends - 1) // MOE_BM
    n_pe = jnp.where(ends > starts, lb - fb + 1, 0).reshape(-1)
    incl = jnp.cumsum(n_pe)
    off = incl - n_pe
    total = incl[-1]
    w = jnp.arange(n_items, dtype=i32)
    valid = w < total
    wc = jnp.minimum(w, total - 1)
    p = jnp.sum(incl[None, :] <= wc[:, None], axis=1, dtype=i32)
    it_tile = p // e_n
    it_e = p % e_n
    it_blk = fb.reshape(-1)[p] + (wc - off[p])
    tile_off = jnp.concatenate([off[::e_n], total[None]])
    it_first = (wc == tile_off[it_tile]).astype(i32)
    it_last = (wc == tile_off[it_tile + 1] - 1).astype(i32)
    return (pos_tok, e_r, jlo, jhi, n_used.reshape(1),
            it_tile, it_blk, it_e, it_first, it_last, valid.astype(i32))


def _experts_kernel(be_ref, jlo_ref, jhi_ref, nused_ref, h_ref, pos_ref, wgu_ref, wd_ref, y_ref, xg_ref):
    r = pl.program_id(0)

    @pl.when(r < nused_ref[0])
    def _():
        e = be_ref[r]
        slot = r * MOE_BM + lax.broadcasted_iota(jnp.int32, (MOE_BM, MOE_TC), 0)
        xg_ref[...] = jnp.zeros_like(xg_ref)

        def chunk(j, carry):
            onehot = jnp.where(pos_ref[e, pl.ds(j, 1), :] == slot, 1.0, 0.0).astype(BF16)
            rows = h_ref[pl.ds(pl.multiple_of(j * MOE_TC, MOE_TC), MOE_TC), :]
            xg_ref[...] += jnp.dot(onehot, rows, preferred_element_type=F32)
            return carry

        lax.fori_loop(jlo_ref[r], jhi_ref[r] + 1, chunk, 0)
        y_ref[...] = _swiglu_chunks(xg_ref[...].astype(BF16), wgu_ref, wd_ref, D_FF_EXPERT)

    @pl.when(r >= nused_ref[0])
    def _():
        y_ref[...] = jnp.zeros_like(y_ref)


def _experts(h, pos_row, plan, wgu_bf, wd_bf, n_blocks):
    n, d = h.shape
    e_r, jlo, jhi, n_used = plan
    grid_spec = pltpu.PrefetchScalarGridSpec(
        num_scalar_prefetch=4,
        grid=(n_blocks,),
        in_specs=[
            _resident((n, d), lambda r, be, lo, hi, nu: (0, 0)),
            _resident(pos_row.shape, lambda r, be, lo, hi, nu: (0, 0, 0)),
            pl.BlockSpec((None, d, 2 * D_FF_EXPERT), lambda r, be, lo, hi, nu: (be[r], 0, 0),
                         pipeline_mode=pl.Buffered(1)),
            pl.BlockSpec((None, D_FF_EXPERT, d), lambda r, be, lo, hi, nu: (be[r], 0, 0)),
        ],
        out_specs=pl.BlockSpec((MOE_BM, d), lambda r, be, lo, hi, nu: (r, 0)),
        scratch_shapes=[pltpu.VMEM((MOE_BM, d), F32)],
    )
    return pl.pallas_call(
        _experts_kernel,
        out_shape=jax.ShapeDtypeStruct((n_blocks * MOE_BM, d), F32),
        grid_spec=grid_spec,
        compiler_params=_cparams(("arbitrary",)),
        name="moe_experts",
    )(e_r, jlo, jhi, n_used, h, pos_row, wgu_bf, wd_bf)


def _combine_kernel(tile_ref, blk_ref, e_ref, first_ref, last_ref, valid_ref,
                    y_ref, pos_ref, cw_ref, x_ref, mod_ref, g_ref, b_ref, o_ref, acc_ref):
    w = pl.program_id(0)

    @pl.when(valid_ref[w] == 1)
    def _():
        @pl.when(first_ref[w] == 1)
        def _():
            acc_ref[...] = jnp.zeros_like(acc_ref)

        e = e_ref[w]
        lane = lax.broadcasted_iota(jnp.int32, (MOE_TMC, LANES), 1)
        pos_e = jnp.sum(jnp.where(lane == e, pos_ref[...], 0.0), axis=-1, keepdims=True)
        cw_e = jnp.sum(jnp.where(lane == e, cw_ref[...], 0.0), axis=-1, keepdims=True)
        slot = (blk_ref[w] * MOE_BM + lax.broadcasted_iota(jnp.int32, (MOE_TMC, MOE_BM), 1)).astype(F32)
        onehot = jnp.where(pos_e == slot, 1.0, 0.0).astype(BF16)
        y_hi, y_lo = _split_bf16(y_ref[...])
        part = (jnp.dot(onehot, y_hi, preferred_element_type=F32)
                + jnp.dot(onehot, y_lo, preferred_element_type=F32))
        acc_ref[...] += cw_e * part

        @pl.when(last_ref[w] == 1)
        def _():
            z = ALPHA * x_ref[...] + mod_ref[5:6, :] * acc_ref[...]
            o_ref[...] = _layer_norm_rows(z, g_ref[...], b_ref[...])


def _combine(y, pos_tok_f, cw, x, mod, ln_g, ln_b, items, n_items):
    g, s, d = x.shape
    per = s // MOE_TMC

    def tok2(w, tile, *_):
        return (tile[w], 0)

    def tok3(w, tile, *_):
        return (tile[w] // per, tile[w] % per, 0)

    grid_spec = pltpu.PrefetchScalarGridSpec(
        num_scalar_prefetch=6,
        grid=(n_items,),
        in_specs=[
            pl.BlockSpec((MOE_BM, d), lambda w, tile, blk, *_: (blk[w], 0)),
            pl.BlockSpec((MOE_TMC, LANES), tok2),
            pl.BlockSpec((MOE_TMC, LANES), tok2),
            pl.BlockSpec((None, MOE_TMC, d), tok3),
            pl.BlockSpec((None, 6, d), lambda w, tile, *_: (tile[w] // per, 0, 0)),
            pl.BlockSpec((1, d), lambda w, *_: (0, 0)),
            pl.BlockSpec((1, d), lambda w, *_: (0, 0)),
        ],
        out_specs=pl.BlockSpec((None, MOE_TMC, d), tok3),
        scratch_shapes=[pltpu.VMEM((MOE_TMC, d), F32)],
    )
    return pl.pallas_call(
        _combine_kernel,
        out_shape=jax.ShapeDtypeStruct((g, s, d), F32),
        grid_spec=grid_spec,
        compiler_params=_cparams(("arbitrary",)),
        name="moe_combine",
    )(*items, y, pos_tok_f, cw, x, mod, ln_g.reshape(1, d), ln_b.reshape(1, d))


def _moe(x, mod, router_w, router_b, wgu_bf, wd_bf, ln_g, ln_b):
    g, s, d = x.shape
    n = g * s
    n_slots = 2 * n
    n_blocks = n_slots // MOE_BM + N_EXPERTS
    n_items = n_blocks + N_EXPERTS * (n // MOE_TMC)
    h, cw, srank, counts = _route(x, mod, router_w, router_b)
    plan = _moe_plan(srank, counts, n_blocks, n_items)
    pos_tok = plan[0]
    pos_row = pos_tok.T.reshape(N_EXPERTS, n // MOE_TC, MOE_TC)
    y = _experts(h, pos_row, plan[1:5], wgu_bf, wd_bf, n_blocks)
    pos_tok_f = jnp.pad(pos_tok.astype(F32), ((0, 0), (0, LANES - N_EXPERTS)), constant_values=-1.0)
    return _combine(y, pos_tok_f, cw, x, mod, ln_g, ln_b, plan[5:], n_items)


def _inproj_c_kernel(x_ref, mod_ref, w_ref, wg_ref, bg_ref, o_ref, og_ref):
    h = (x_ref[...] * (1.0 + mod_ref[1:2, :]) + mod_ref[0:1, :]).astype(BF16)
    o_ref[...] = jnp.dot(h, w_ref[...], preferred_element_type=F32)
    og_ref[...] = jnp.dot(h, wg_ref[...], preferred_element_type=F32) + bg_ref[...]


def _inproj_c(x, mod, w_bf, wg_bf, bg, tm=512):
    g, s, d = x.shape
    n = w_bf.shape[1]
    return pl.pallas_call(
        _inproj_c_kernel,
        out_shape=(jax.ShapeDtypeStruct((g, s, n), F32), jax.ShapeDtypeStruct((g, s, LANES), F32)),
        grid=(g, s // tm),
        in_specs=[
            pl.BlockSpec((None, tm, d), lambda gi, ti: (gi, ti, 0)),
            pl.BlockSpec((None, 6, d), lambda gi, ti: (gi, 0, 0)),
            _resident((d, n), lambda gi, ti: (0, 0)),
            _resident((d, LANES), lambda gi, ti: (0, 0)),
            pl.BlockSpec((1, LANES), lambda gi, ti: (0, 0)),
        ],
        out_specs=(pl.BlockSpec((None, tm, n), lambda gi, ti: (gi, ti, 0)),
                   pl.BlockSpec((None, tm, LANES), lambda gi, ti: (gi, ti, 0))),
        compiler_params=_cparams(("arbitrary", "arbitrary")),
        name="inproj_c",
    )(x, mod, w_bf, wg_bf, bg)


def _log_sigmoid(x):
    return jnp.minimum(x, 0.0) - jnp.log(1.0 + jnp.exp(-jnp.abs(x)))


def _mlstm_kernel(*refs, seq, hg, has_init, emit_state):
    q_ref, k_ref, v_ref, o_ref, gc_ref, gr_ref, hgain_ref = refs[:7]
    pos = 7
    if has_init:
        c0_ref, n0_ref, m0_ref = refs[pos:pos + 3]
        pos += 3
    out_ref = refs[pos]
    pos += 1
    if emit_state:
        co_ref, no_ref, mo_ref = refs[pos:pos + 3]
        pos += 3
    cext_ref, hf_ref, hb_ref = refs[pos:pos + 3]

    L = MLSTM_CHUNK
    dh = MLSTM_DH
    nc = seq // L
    head0 = pl.program_id(1) * hg
    neg = -jnp.inf

    lane_d = lax.broadcasted_iota(jnp.int32, (dh, dh), 1)
    for d in range(2):
        for hh in range(hg):
            idx = d * hg + hh
            if has_init:
                cext_ref[idx, :, 0:dh] = c0_ref[d, hh]
                cext_ref[idx, :, dh:2 * dh] = jnp.where(lane_d == 0, n0_ref[d, hh], 0.0)
            else:
                cext_ref[idx] = jnp.zeros((dh, 2 * dh), F32)

    row = lax.broadcasted_iota(jnp.int32, (L, L), 0)
    col = lax.broadcasted_iota(jnp.int32, (L, L), 1)
    lower = col <= row
    upper = col >= row
    lane_g = lax.broadcasted_iota(jnp.int32, (L, LANES), 1)
    ones_col = jnp.where(lane_g == 0, 1.0, 0.0).astype(BF16)
    nt = (((1,), (1,)), ((), ()))
    tn = (((0,), (0,)), ((), ()))

    def one_direction(d, hh, c, s_qk, q_bf, k_s, v_ext, v_bf, m_prev):
        idx = d * hg + hh
        head = head0 + hh
        causal, anti = (lower, upper) if d == 0 else (upper, lower)
        gates_c = gc_ref[pl.ds(c * L, L), :]

        def col_of(j):
            return jnp.sum(jnp.where(lane_g == j, gates_c, 0.0), axis=-1, keepdims=True)

        i_c = col_of((2 * d) * MLSTM_HEADS + head)
        f_c = _log_sigmoid(col_of((2 * d + 1) * MLSTM_HEADS + head))
        i_r = gr_ref[2 * d, hh, pl.ds(c, 1), :]
        f_r = _log_sigmoid(gr_ref[2 * d + 1, hh, pl.ds(c, 1), :])

        b_c = jnp.sum(jnp.where(causal, f_r, 0.0), axis=1, keepdims=True)
        b_r = jnp.sum(jnp.where(anti, f_c, 0.0), axis=0, keepdims=True)
        b_tot = jnp.sum(f_r, axis=1, keepdims=True)
        dmat = jnp.where(causal, b_c - b_r + i_r, neg)
        m_inter = b_c + m_prev
        m_t = jnp.maximum(m_inter, jnp.max(dmat, axis=-1, keepdims=True))
        w_inter = jnp.exp(m_inter - m_t)
        p = s_qk * jnp.exp(dmat - m_t)
        qc = jnp.dot(q_bf, cext_ref[idx].astype(BF16), preferred_element_type=F32)
        num = w_inter * qc[:, 0:dh] + jnp.dot(p.astype(BF16), v_bf, preferred_element_type=F32)
        den = w_inter * qc[:, dh:dh + 1] + jnp.sum(p, axis=-1, keepdims=True)
        h = num / jnp.maximum(jnp.abs(den), jnp.exp(-m_t))
        last = L - 1 if d == 0 else 0
        m_new = m_t[last:last + 1, :]
        w_c = jnp.exp(b_tot + m_prev - m_new)
        w_s = jnp.exp(b_tot - b_c + i_c - m_new)
        upd = lax.dot_general((w_s * k_s).astype(BF16), v_ext, tn, preferred_element_type=F32)
        cext_ref[idx] = w_c * cext_ref[idx] + upd
        return h, m_new

    def load_chunk(hh, c):
        sl = (pl.ds(c * L, L), slice(hh * dh, (hh + 1) * dh))
        q_bf = q_ref[sl].astype(BF16)
        k_s = k_ref[sl] * (dh ** -0.5)
        v_bf = v_ref[sl].astype(BF16)
        v_ext = jnp.concatenate([v_bf, ones_col], axis=-1)
        s_qk = lax.dot_general(q_bf, k_s.astype(BF16), nt, preferred_element_type=F32)
        return s_qk, q_bf, k_s, v_ext, v_bf

    def step(c, ms):
        cb = nc - 1 - c
        new_ms = []
        for hh in range(hg):
            h, m_new = one_direction(0, hh, c, *load_chunk(hh, c), ms[hh])
            hf_ref[pl.ds(c * L, L), hh * dh:(hh + 1) * dh] = h
            new_ms.append(m_new)
        for hh in range(hg):
            h, m_new = one_direction(1, hh, cb, *load_chunk(hh, cb), ms[hg + hh])
            hb_ref[pl.ds(cb * L, L), hh * dh:(hh + 1) * dh] = h
            new_ms.append(m_new)
        return tuple(new_ms)

    if has_init:
        ms0 = tuple(m0_ref[d, hh] for d in range(2) for hh in range(hg))
    else:
        ms0 = tuple(jnp.zeros((1, 1), F32) for _ in range(2 * hg))
    ms = lax.fori_loop(0, nc, step, ms0)

    for hh in range(hg):
        cs = slice(hh * dh, (hh + 1) * dh)
        hs = hf_ref[:, cs] + hb_ref[:, cs]
        mu = jnp.mean(hs, axis=-1, keepdims=True)
        hc = hs - mu
        var = jnp.mean(hc * hc, axis=-1, keepdims=True)
        hn = hc * lax.rsqrt(var + LN_EPS) * hgain_ref[:, cs]
        out_ref[:, cs] = (_sigmoid(o_ref[:, cs]) * hn).astype(out_ref.dtype)

    if emit_state:
        for d in range(2):
            for hh in range(hg):
                idx = d * hg + hh
                co_ref[d, hh] = cext_ref[idx, :, 0:dh]
                no_ref[d, hh] = cext_ref[idx, :, dh:dh + 1]
                mo_ref[d, hh] = jnp.broadcast_to(ms[idx], (1, LANES))


def _mlstm(proj, gates, head_g, g0, n_seq, seq, hg, init=None, emit_state=False):
    g, s, _ = proj.shape
    per_group = s // seq
    n_hg = MLSTM_HEADS // hg
    w = hg * MLSTM_DH
    nc = seq // MLSTM_CHUNK
    n_blocks = D_MODEL // w

    g_seq = gates[g0:g0 + n_seq // per_group].reshape(n_seq, seq, LANES)
    g_row = g_seq[:, :, :N_GATES * MLSTM_HEADS].transpose(0, 2, 1).reshape(
        n_seq, N_GATES, MLSTM_HEADS, nc, MLSTM_CHUNK)

    def tok_map(colblock):
        return lambda b, hi: (g0 + b // per_group, b % per_group, colblock * n_blocks + hi)

    args = [proj, proj, proj, proj, g_seq, g_row, head_g.reshape(1, D_MODEL)]
    in_specs = [
        pl.BlockSpec((None, seq, w), tok_map(0)),
        pl.BlockSpec((None, seq, w), tok_map(1)),
        pl.BlockSpec((None, seq, w), tok_map(2)),
        pl.BlockSpec((None, seq, w), tok_map(3)),
        pl.BlockSpec((None, seq, LANES), lambda b, hi: (b, 0, 0)),
        pl.BlockSpec((None, N_GATES, hg, nc, MLSTM_CHUNK), lambda b, hi: (b, 0, hi, 0, 0)),
        pl.BlockSpec((1, w), lambda b, hi: (0, hi)),
    ]
    if init is not None:
        c0, n0, m0 = init
        args += [c0, n0.reshape(n0.shape + (1,)), m0.reshape(m0.shape + (1, 1))]
        in_specs += [
            pl.BlockSpec((None, 2, hg, MLSTM_DH, MLSTM_DH), lambda b, hi: (b, 0, hi, 0, 0)),
            pl.BlockSpec((None, 2, hg, MLSTM_DH, 1), lambda b, hi: (b, 0, hi, 0, 0)),
            pl.BlockSpec((None, 2, hg, 1, 1), lambda b, hi: (b, 0, hi, 0, 0)),
        ]
    out_shape = [jax.ShapeDtypeStruct((n_seq // per_group, s, D_MODEL), BF16)]
    out_specs = [pl.BlockSpec((None, seq, w), lambda b, hi: (b // per_group, b % per_group, hi))]
    if emit_state:
        out_shape += [
            jax.ShapeDtypeStruct((n_seq, 2, MLSTM_HEADS, MLSTM_DH, MLSTM_DH), F32),
            jax.ShapeDtypeStruct((n_seq, 2, MLSTM_HEADS, MLSTM_DH, 1), F32),
            jax.ShapeDtypeStruct((n_seq, 2, MLSTM_HEADS, 1, LANES), F32),
        ]
        out_specs += [
            pl.BlockSpec((None, 2, hg, MLSTM_DH, MLSTM_DH), lambda b, hi: (b, 0, hi, 0, 0)),
            pl.BlockSpec((None, 2, hg, MLSTM_DH, 1), lambda b, hi: (b, 0, hi, 0, 0)),
            pl.BlockSpec((None, 2, hg, 1, LANES), lambda b, hi: (b, 0, hi, 0, 0)),
        ]

    return pl.pallas_call(
        functools.partial(_mlstm_kernel, seq=seq, hg=hg, has_init=init is not None, emit_state=emit_state),
        out_shape=tuple(out_shape),
        grid=(n_seq, n_hg),
        in_specs=in_specs,
        out_specs=tuple(out_specs),
        scratch_shapes=[
            pltpu.VMEM((2 * hg, MLSTM_DH, 2 * MLSTM_DH), F32),
            pltpu.VMEM((seq, w), F32),
            pltpu.VMEM((seq, w), F32),
        ],
        compiler_params=_cparams(("arbitrary", "arbitrary")),
        name="mlstm_%d" % seq,
    )(*args)


def kernel(x_prompt, x_sample, c, cache_k, cache_v, state_C, state_n, state_m, c_ctx, ada_w, ada_b, ln_g, ln_b, w_in_a, diff_lambda, diff_norm_g, pool_w, pool_scale, w_out_a, ffn_w_gu, ffn_w_down, w_in_c, b_gates_c, mlstm_norm_g, w_out_c, router_w, router_b, moe_w_gu, moe_w_down):
    n_ctx, seq_ctx, d = x_prompt.shape
    n_lat, seq_lat, _ = x_sample.shape
    assert d == D_MODEL and (n_ctx * seq_ctx) % seq_lat == 0 and seq_lat % seq_ctx == 0
    gl = n_lat
    gc = n_ctx * seq_ctx // seq_lat
    s = seq_lat

    x = jnp.concatenate([x_sample, x_prompt.reshape(gc, s, d)], axis=0)
    cvec = jnp.concatenate([c, jnp.broadcast_to(c_ctx[None, :], (gc, d))], axis=0)
    mod_all = _modulation(cvec, ada_w, ada_b).reshape(DEPTH, gl + gc, 6, d)

    mod = mod_all[0]
    lam_init = 0.8 - 0.6 * math.exp(-0.3 * 0)
    cos_t, sin_t = _rope_tables(s)
    proj = _inproj_a(x, mod, w_in_a[0].astype(BF16), cos_t, sin_t, gl)
    norm_g = diff_norm_g[0].reshape(1, LANES)
    attn_c, new_k, new_v = _attn_context(proj, diff_lambda[0], norm_g, gl, n_ctx, seq_ctx, lam_init)
    attn_l = _attn_latent(proj, cache_k, cache_v, diff_lambda[0], norm_g, gl, lam_init)
    pool_c = _pool(proj, pool_w[0], pool_scale[0], gl, gc, seq_ctx)
    pool_l = _pool(proj, pool_w[0], pool_scale[0], 0, gl, seq_lat)
    w_out = w_out_a[0].astype(BF16)
    x = _outproj([(attn_l, attn_c), (pool_l, pool_c)], [w_out[:DIFF_WIDTH], w_out[DIFF_WIDTH:]],
                 x, mod, ln_g[0, 0], ln_b[0, 0], 2)
    x = _ffn(x, mod, ffn_w_gu[0].astype(BF16), ffn_w_down[0].astype(BF16), ln_g[0, 1], ln_b[0, 1])

    mod = mod_all[1]
    n_main = 4 * D_MODEL
    w_main = w_in_c[0][:, :n_main].astype(BF16)
    n_g = N_GATES * MLSTM_HEADS
    w_gate = jnp.pad(w_in_c[0][:, n_main:], ((0, 0), (0, LANES - n_g))).astype(BF16)
    b_gate = jnp.pad(b_gates_c[0], (0, LANES - n_g)).reshape(1, LANES)
    proj, gates = _inproj_c(x, mod, w_main, w_gate, b_gate)
    mix_c, new_c, new_n, new_m = _mlstm(proj, gates, mlstm_norm_g[0], gl, n_ctx, seq_ctx, 4, emit_state=True)
    (mix_l,) = _mlstm(proj, gates, mlstm_norm_g[0], 0, n_lat, seq_lat, 4,
                      init=(state_C[:, 0], state_n[:, 0], state_m[:, 0]))
    x = _outproj([(mix_l, mix_c)], [w_out_c[0].astype(BF16)], x, mod, ln_g[1, 0], ln_b[1, 0], 2)
    x = _moe(x, mod, router_w[0], router_b[0], moe_w_gu[0].astype(BF16), moe_w_down[0].astype(BF16),
             ln_g[1, 1], ln_b[1, 1])

    y_sample = x[:gl]
    y_prompt = x[gl:].reshape(n_ctx, seq_ctx, d)
    return (y_prompt, y_sample, new_k, new_v,
            new_c[:, None], new_n[..., 0][:, None], new_m[..., 0, 0][:, None])
```

```python
import functools
import math

import jax
import jax.numpy as jnp
from jax import lax
from jax.experimental import pallas as pl
from jax.experimental.pallas import tpu as pltpu

F32 = jnp.float32
BF16 = jnp.bfloat16

D_MODEL = 1024
GRID_W = 64
ROPE_BASE = 10000.0
DIFF_HEADS = 4
DIFF_DH = 64
DIFF_WIDTH = DIFF_HEADS * 2 * DIFF_DH
POOL_GROUPS = 4
POOL_GC = 128
POOL_WIDTH = POOL_GROUPS * POOL_GC
POOL_WINDOWS = (2, 4, 8, 16)
W_IN_A = 3 * DIFF_WIDTH + POOL_WIDTH
MLSTM_HEADS = 8
MLSTM_DH = 128
MLSTM_CHUNK = 64
N_GATES = 4
D_FF = 2816
N_EXPERTS = 8
D_FF_EXPERT = 1792
LN_EPS = 1e-5
DEPTH = 2
ALPHA = (2.0 * DEPTH) ** 0.25

LANES = 128
FF_CHUNK = 256
VMEM_LIMIT = 56 * 1024 * 1024


def _cparams(sem):
    return pltpu.CompilerParams(dimension_semantics=sem, vmem_limit_bytes=VMEM_LIMIT)


def _resident(shape, index_map):
    return pl.BlockSpec(shape, index_map, pipeline_mode=pl.Buffered(1))


def _layer_norm_rows(z, g, b):
    mu = jnp.mean(z, axis=-1, keepdims=True)
    zc = z - mu
    var = jnp.mean(zc * zc, axis=-1, keepdims=True)
    return zc * lax.rsqrt(var + LN_EPS) * g + b


def _sigmoid(x):
    return 1.0 / (1.0 + jnp.exp(-x))


def _split_bf16(x):
    hi = x.astype(BF16)
    lo = (x - hi.astype(F32)).astype(BF16)
    return hi, lo


def _mod_kernel(c_ref, w_ref, b_ref, o_ref):
    c = c_ref[...]
    h = (c * _sigmoid(c)).astype(BF16)
    o_ref[...] = jnp.dot(h, w_ref[...].astype(BF16), preferred_element_type=F32) + b_ref[...]


def _modulation(cvec, ada_w, ada_b):
    depth, d, n = ada_w.shape
    g = cvec.shape[0]
    tn = 1536
    return pl.pallas_call(
        _mod_kernel,
        out_shape=jax.ShapeDtypeStruct((depth, g, n), F32),
        grid=(depth, n // tn),
        in_specs=[
            pl.BlockSpec((g, d), lambda l, j: (0, 0)),
            pl.BlockSpec((None, d, tn), lambda l, j: (l, 0, j)),
            pl.BlockSpec((None, 1, tn), lambda l, j: (l, 0, j)),
        ],
        out_specs=pl.BlockSpec((None, g, tn), lambda l, j: (l, 0, j)),
        compiler_params=_cparams(("arbitrary", "arbitrary")),
        name="modulation",
    )(cvec, ada_w, ada_b.reshape(depth, 1, n))


def _rot_half16(x):
    lane = lax.broadcasted_iota(jnp.int32, x.shape, 1)
    return jnp.where((lane % 32) < 16, pltpu.roll(x, LANES - 16, 1), pltpu.roll(x, 16, 1))


def _inproj_a_kernel(x_ref, mod_ref, w_ref, cos_ref, sin_ref, o_ref):
    h = x_ref[...] * (1.0 + mod_ref[1:2, :]) + mod_ref[0:1, :]
    p = jnp.dot(h.astype(BF16), w_ref[...], preferred_element_type=F32)
    cos = cos_ref[...]
    sin = sin_ref[...]
    n_rope = 2 * DIFF_WIDTH // LANES
    for j in range(n_rope):
        blk = p[:, j * LANES:(j + 1) * LANES]
        o_ref[:, j * LANES:(j + 1) * LANES] = blk * cos + _rot_half16(blk) * sin
    o_ref[:, n_rope * LANES:] = p[:, n_rope * LANES:]


def _inproj_a(x, mod, w_bf, cos_t, sin_t, n_latent_groups, tm=512):
    g, s, d = x.shape
    n = w_bf.shape[1]

    def table_map(gi, ti):
        return (jnp.where(gi >= n_latent_groups, 1, 0), ti, 0)

    return pl.pallas_call(
        _inproj_a_kernel,
        out_shape=jax.ShapeDtypeStruct((g, s, n), F32),
        grid=(g, s // tm),
        in_specs=[
            pl.BlockSpec((None, tm, d), lambda gi, ti: (gi, ti, 0)),
            pl.BlockSpec((None, 6, d), lambda gi, ti: (gi, 0, 0)),
            _resident((d, n), lambda gi, ti: (0, 0)),
            pl.BlockSpec((None, tm, LANES), table_map),
            pl.BlockSpec((None, tm, LANES), table_map),
        ],
        out_specs=pl.BlockSpec((None, tm, n), lambda gi, ti: (gi, ti, 0)),
        compiler_params=_cparams(("arbitrary", "arbitrary")),
        name="inproj_a",
    )(x, mod, w_bf, cos_t, sin_t)


def _rope_tables(n_tokens):
    rows = n_tokens // GRID_W
    row_pos = jnp.repeat(jnp.arange(rows), GRID_W).astype(F32)
    col_pos = jnp.tile(jnp.arange(GRID_W), rows).astype(F32)
    n_freq = DIFF_DH // 4
    inv_freq = jnp.power(ROPE_BASE, -jnp.arange(n_freq, dtype=F32) / n_freq)
    ang = jnp.stack([row_pos[:, None] * inv_freq, col_pos[:, None] * inv_freq], axis=1)
    cos, sin = jnp.cos(ang), jnp.sin(ang)
    cos64 = jnp.concatenate([cos[:, 0], cos[:, 0], cos[:, 1], cos[:, 1]], axis=-1)
    sin64 = jnp.concatenate([-sin[:, 0], sin[:, 0], -sin[:, 1], sin[:, 1]], axis=-1)
    cos_l = jnp.tile(cos64, (1, LANES // DIFF_DH))
    sin_l = jnp.tile(sin64, (1, LANES // DIFF_DH))
    cos_t = jnp.stack([cos_l, jnp.ones_like(cos_l)])
    sin_t = jnp.stack([sin_l, jnp.zeros_like(sin_l)])
    return cos_t, sin_t


def _diff_attn_kernel(*refs, n_pieces, lam_init, emit_kv):
    lam_ref, ng_ref, q_ref = refs[:3]
    kv_refs = refs[3:3 + 2 * n_pieces]
    o_ref = refs[3 + 2 * n_pieces]

    lp = lam_ref[...]
    lam = (jnp.exp(jnp.sum(lp[0:1] * lp[1:2], axis=-1, keepdims=True))
           - jnp.exp(jnp.sum(lp[2:3] * lp[3:4], axis=-1, keepdims=True)) + lam_init)

    q = q_ref[...] * (DIFF_DH ** -0.5)
    lane = lax.broadcasted_iota(jnp.int32, q.shape, 1)
    q1 = jnp.where(lane < DIFF_DH, q, 0.0).astype(BF16)
    q2 = jnp.where(lane >= DIFF_DH, q, 0.0).astype(BF16)

    nt = (((1,), (1,)), ((), ()))
    s1, s2, vs = [], [], []
    for i in range(n_pieces):
        kb = kv_refs[2 * i][...].astype(BF16)
        vs.append(kv_refs[2 * i + 1][...].astype(BF16))
        s1.append(lax.dot_general(q1, kb, nt, preferred_element_type=F32))
        s2.append(lax.dot_general(q2, kb, nt, preferred_element_type=F32))

    def softmax_pieces(ss):
        m = functools.reduce(jnp.maximum, [jnp.max(s, axis=-1, keepdims=True) for s in ss])
        es = [jnp.exp(s - m) for s in ss]
        l = functools.reduce(jnp.add, [jnp.sum(e, axis=-1, keepdims=True) for e in es])
        return [e / l for e in es]

    p1 = softmax_pieces(s1)
    p2 = softmax_pieces(s2)
    o = None
    for i in range(n_pieces):
        a = (p1[i] - lam * p2[i]).astype(BF16)
        t = jnp.dot(a, vs[i], preferred_element_type=F32)
        o = t if o is None else o + t
    o = o * lax.rsqrt(jnp.mean(o * o, axis=-1, keepdims=True) + LN_EPS)
    o_ref[...] = (o * ng_ref[...] * (1.0 - lam_init)).astype(o_ref.dtype)
    if emit_kv:
        ko_ref, vo_ref = refs[4 + 2 * n_pieces:]
        ko_ref[...] = kv_refs[0][...]
        vo_ref[...] = kv_refs[1][...]


def _attn_context(proj, lam_p, norm_g, n_latent_groups, n_seq, seq, lam_init):
    g, s, _ = proj.shape
    per_group = s // seq
    blk = (None, seq, LANES)

    def tok_map(col0):
        return lambda b, h: (n_latent_groups + b // per_group, b % per_group, col0 + h)

    cache_shape = jax.ShapeDtypeStruct((n_seq, 1, DIFF_HEADS, seq, LANES), F32)
    cache_spec = pl.BlockSpec((None, None, None, seq, LANES), lambda b, h: (b, 0, h, 0, 0))
    out_spec = pl.BlockSpec(blk, lambda b, h: (b // per_group, b % per_group, h))
    return pl.pallas_call(
        functools.partial(_diff_attn_kernel, n_pieces=1, lam_init=lam_init, emit_kv=True),
        out_shape=(jax.ShapeDtypeStruct((g - n_latent_groups, s, DIFF_WIDTH), BF16), cache_shape, cache_shape),
        grid=(n_seq, DIFF_HEADS),
        in_specs=[
            pl.BlockSpec((4, DIFF_DH), lambda b, h: (0, 0)),
            pl.BlockSpec((1, LANES), lambda b, h: (0, 0)),
            pl.BlockSpec(blk, tok_map(0)),
            pl.BlockSpec(blk, tok_map(DIFF_HEADS)),
            pl.BlockSpec(blk, tok_map(2 * DIFF_HEADS)),
        ],
        out_specs=(out_spec, cache_spec, cache_spec),
        compiler_params=_cparams(("arbitrary", "arbitrary")),
        name="attn_context",
    )(lam_p, norm_g, proj, proj, proj)


def _attn_latent(proj, cache_k, cache_v, lam_p, norm_g, n_latent_groups, lam_init, tq=256):
    g, s, _ = proj.shape
    past = cache_k.shape[3]
    cache_spec = pl.BlockSpec((None, None, None, past, LANES), lambda b, h, qi: (b, 0, h, 0, 0))
    return pl.pallas_call(
        functools.partial(_diff_attn_kernel, n_pieces=2, lam_init=lam_init, emit_kv=False),
        out_shape=jax.ShapeDtypeStruct((n_latent_groups, s, DIFF_WIDTH), BF16),
        grid=(n_latent_groups, DIFF_HEADS, s // tq),
        in_specs=[
            pl.BlockSpec((4, DIFF_DH), lambda b, h, qi: (0, 0)),
            pl.BlockSpec((1, LANES), lambda b, h, qi: (0, 0)),
            pl.BlockSpec((None, tq, LANES), lambda b, h, qi: (b, qi, h)),
            cache_spec,
            cache_spec,
            pl.BlockSpec((None, s, LANES), lambda b, h, qi: (b, 0, DIFF_HEADS + h)),
            pl.BlockSpec((None, s, LANES), lambda b, h, qi: (b, 0, 2 * DIFF_HEADS + h)),
        ],
        out_specs=pl.BlockSpec((None, tq, LANES), lambda b, h, qi: (b, qi, h)),
        compiler_params=_cparams(("arbitrary", "arbitrary", "arbitrary")),
        name="attn_latent",
    )(lam_p, norm_g, proj, cache_k, cache_v, proj, proj)


def _pool_kernel(p_ref, w_ref, sc_ref, o_ref, band_ref, *, seq):
    @pl.when((pl.program_id(0) == 0) & (pl.program_id(1) == 0))
    def _():
        t = lax.broadcasted_iota(jnp.int32, (seq, seq), 0)
        s_ = lax.broadcasted_iota(jnp.int32, (seq, seq), 1)
        for gi, w in enumerate(POOL_WINDOWS):
            inside = (s_ >= t - w // 2) & (s_ <= t + w // 2 - 1)
            band_ref[gi] = jnp.where(inside, 1.0, 0.0).astype(BF16)

    tcol = lax.broadcasted_iota(jnp.int32, (seq, 1), 0)
    for gi, w in enumerate(POOL_WINDOWS):
        u = p_ref[:, gi * POOL_GC:(gi + 1) * POOL_GC]
        hi, lo = _split_bf16(u)
        band = band_ref[gi]
        win = (jnp.dot(band, hi, preferred_element_type=F32)
               + jnp.dot(band, lo, preferred_element_type=F32))
        cnt = (jnp.minimum(tcol + (w // 2 - 1), seq - 1) - jnp.maximum(tcol - w // 2, 0) + 1).astype(F32)
        pooled = win / cnt - u
        mixed = jnp.dot(pooled.astype(BF16), w_ref[gi].astype(BF16), preferred_element_type=F32)
        o_ref[:, gi * POOL_GC:(gi + 1) * POOL_GC] = (
            mixed * sc_ref[:, gi * POOL_GC:(gi + 1) * POOL_GC]).astype(o_ref.dtype)


def _pool(proj, pool_w, pool_scale, g0, n_groups, seq):
    g, s, _ = proj.shape
    col = 3 * DIFF_WIDTH // POOL_WIDTH
    return pl.pallas_call(
        functools.partial(_pool_kernel, seq=seq),
        out_shape=jax.ShapeDtypeStruct((n_groups, s, POOL_WIDTH), BF16),
        grid=(n_groups, s // seq),
        in_specs=[
            pl.BlockSpec((None, seq, POOL_WIDTH), lambda gi, ti: (g0 + gi, ti, col)),
            pl.BlockSpec((POOL_GROUPS, POOL_GC, POOL_GC), lambda gi, ti: (0, 0, 0)),
            pl.BlockSpec((1, POOL_WIDTH), lambda gi, ti: (0, 0)),
        ],
        out_specs=pl.BlockSpec((None, seq, POOL_WIDTH), lambda gi, ti: (gi, ti, 0)),
        scratch_shapes=[pltpu.VMEM((POOL_GROUPS, seq, seq), BF16)],
        compiler_params=_cparams(("arbitrary", "arbitrary")),
        name="pool_%d" % seq,
    )(proj, pool_w, pool_scale.reshape(1, POOL_WIDTH))


def _outproj_kernel(*refs, n_in, gate_row, n_latent_groups):
    a_refs = refs[:2 * n_in]
    w_refs = refs[2 * n_in:3 * n_in]
    x_ref, mod_ref, g_ref, b_ref, o_ref = refs[3 * n_in:]
    is_latent = pl.program_id(0) < n_latent_groups
    acc = None
    for i, w_ref in enumerate(w_refs):
        a = jnp.where(is_latent, a_refs[2 * i][...], a_refs[2 * i + 1][...])
        t = jnp.dot(a, w_ref[...], preferred_element_type=F32)
        acc = t if acc is None else acc + t
    z = ALPHA * x_ref[...] + mod_ref[gate_row:gate_row + 1, :] * acc
    o_ref[...] = _layer_norm_rows(z, g_ref[...], b_ref[...])


def _outproj(acts, weights, x, mod, ln_g, ln_b, gate_row, tm=512):
    g, s, d = x.shape
    n_in = len(acts)
    gl = acts[0][0].shape[0]
    in_specs = []
    flat_acts = []
    for a_lat, a_ctx in acts:
        k = a_lat.shape[-1]
        in_specs.append(pl.BlockSpec((None, tm, k), lambda gi, ti: (jnp.minimum(gi, gl - 1), jnp.where(gi < gl, ti, 0), 0)))
        in_specs.append(pl.BlockSpec((None, tm, k), lambda gi, ti: (jnp.maximum(gi - gl, 0), jnp.where(gi < gl, 0, ti), 0)))
        flat_acts += [a_lat, a_ctx]
    in_specs += [_resident(w.shape, lambda gi, ti: (0, 0)) for w in weights]
    in_specs += [
        pl.BlockSpec((None, tm, d), lambda gi, ti: (gi, ti, 0)),
        pl.BlockSpec((None, 6, d), lambda gi, ti: (gi, 0, 0)),
        pl.BlockSpec((1, d), lambda gi, ti: (0, 0)),
        pl.BlockSpec((1, d), lambda gi, ti: (0, 0)),
    ]
    return pl.pallas_call(
        functools.partial(_outproj_kernel, n_in=n_in, gate_row=gate_row, n_latent_groups=gl),
        out_shape=jax.ShapeDtypeStruct((g, s, d), F32),
        grid=(g, s // tm),
        in_specs=in_specs,
        out_specs=pl.BlockSpec((None, tm, d), lambda gi, ti: (gi, ti, 0)),
        compiler_params=_cparams(("arbitrary", "arbitrary")),
        name="outproj",
    )(*flat_acts, *weights, x, mod, ln_g.reshape(1, d), ln_b.reshape(1, d))


def _swiglu_chunks(h_bf, wgu_ref, wd_ref, d_ff):
    acc = None
    for j in range(d_ff // FF_CHUNK):
        lo = j * FF_CHUNK
        gate = jnp.dot(h_bf, wgu_ref[:, lo:lo + FF_CHUNK], preferred_element_type=F32)
        up = jnp.dot(h_bf, wgu_ref[:, d_ff + lo:d_ff + lo + FF_CHUNK], preferred_element_type=F32)
        act = (gate * _sigmoid(gate) * up).astype(BF16)
        t = jnp.dot(act, wd_ref[lo:lo + FF_CHUNK, :], preferred_element_type=F32)
        acc = t if acc is None else acc + t
    return acc


def _ffn_kernel(x_ref, mod_ref, wgu_ref, wd_ref, g_ref, b_ref, o_ref):
    x = x_ref[...]
    h = (x * (1.0 + mod_ref[4:5, :]) + mod_ref[3:4, :]).astype(BF16)
    acc = _swiglu_chunks(h, wgu_ref, wd_ref, D_FF)
    z = ALPHA * x + mod_ref[5:6, :] * acc
    o_ref[...] = _layer_norm_rows(z, g_ref[...], b_ref[...])


def _ffn(x, mod, wgu_bf, wd_bf, ln_g, ln_b, tm=256):
    g, s, d = x.shape
    return pl.pallas_call(
        _ffn_kernel,
        out_shape=jax.ShapeDtypeStruct((g, s, d), F32),
        grid=(g, s // tm),
        in_specs=[
            pl.BlockSpec((None, tm, d), lambda gi, ti: (gi, ti, 0)),
            pl.BlockSpec((None, 6, d), lambda gi, ti: (gi, 0, 0)),
            _resident(wgu_bf.shape, lambda gi, ti: (0, 0)),
            _resident(wd_bf.shape, lambda gi, ti: (0, 0)),
            pl.BlockSpec((1, d), lambda gi, ti: (0, 0)),
            pl.BlockSpec((1, d), lambda gi, ti: (0, 0)),
        ],
        out_specs=pl.BlockSpec((None, tm, d), lambda gi, ti: (gi, ti, 0)),
        compiler_params=_cparams(("arbitrary", "arbitrary")),
        name="ffn",
    )(x, mod, wgu_bf, wd_bf, ln_g.reshape(1, d), ln_b.reshape(1, d))


def _router_combine(h, rw_ref, rb_ref):
    h_hi, h_lo = _split_bf16(h)
    w_hi, w_lo = _split_bf16(rw_ref[...])
    logits = (jnp.dot(h_hi, w_hi, preferred_element_type=F32)
              + jnp.dot(h_lo, w_hi, preferred_element_type=F32)
              + jnp.dot(h_hi, w_lo, preferred_element_type=F32)) + rb_ref[...]
    lane = lax.broadcasted_iota(jnp.int32, logits.shape, 1).astype(F32)
    neg = -jnp.inf
    logits = jnp.where(lane < N_EXPERTS, logits, neg)
    m1 = jnp.max(logits, axis=-1, keepdims=True)
    i1 = jnp.min(jnp.where(logits == m1, lane, float(LANES)), axis=-1, keepdims=True)
    rest = jnp.where(lane == i1, neg, logits)
    m2 = jnp.max(rest, axis=-1, keepdims=True)
    i2 = jnp.min(jnp.where(rest == m2, lane, float(LANES)), axis=-1, keepdims=True)
    e2 = jnp.exp(m2 - m1)
    w1 = 1.0 / (1.0 + e2)
    w2 = e2 / (1.0 + e2)
    member = (lane == i1) | (lane == i2)
    return jnp.where(lane == i1, w1, 0.0) + jnp.where(lane == i2, w2, 0.0), member


MOE_BM = 256
MOE_TC = 256
MOE_TMC = 512


def _route_kernel(x_ref, mod_ref, rw_ref, rb_ref, h_ref, cw_ref, srank_ref, cnt_ref, tri_ref, run_ref):
    tm = x_ref.shape[0]

    @pl.when(pl.program_id(0) == 0)
    def _():
        r = lax.broadcasted_iota(jnp.int32, (tm, tm), 0)
        c = lax.broadcasted_iota(jnp.int32, (tm, tm), 1)
        tri_ref[...] = jnp.where(c <= r, 1.0, 0.0).astype(BF16)
        run_ref[...] = jnp.zeros_like(run_ref)

    h = x_ref[...] * (1.0 + mod_ref[4:5, :]) + mod_ref[3:4, :]
    h_ref[...] = h.astype(BF16)
    cw, member = _router_combine(h, rw_ref, rb_ref)
    cw_ref[...] = cw
    mem = jnp.where(member, 1.0, 0.0)
    rank = jnp.dot(tri_ref[...], mem.astype(BF16), preferred_element_type=F32) + run_ref[...]
    srank_ref[...] = jnp.where(member, rank, -rank)
    run_ref[...] = rank[tm - 1:tm, :]
    cnt_ref[...] = rank[tm - 1:tm, :]


def _route(x, mod, router_w, router_b, tm=512):
    g, s, d = x.shape
    n = g * s
    per = s // tm
    rw = jnp.pad(router_w, ((0, 0), (0, LANES - N_EXPERTS)))
    rb = jnp.pad(router_b, (0, LANES - N_EXPERTS)).reshape(1, LANES)
    return pl.pallas_call(
        _route_kernel,
        out_shape=(jax.ShapeDtypeStruct((n, d), BF16), jax.ShapeDtypeStruct((n, LANES), F32),
                   jax.ShapeDtypeStruct((n, LANES), F32), jax.ShapeDtypeStruct((1, LANES), F32)),
        grid=(n // tm,),
        in_specs=[
            pl.BlockSpec((None, tm, d), lambda i: (i // per, i % per, 0)),
            pl.BlockSpec((None, 6, d), lambda i: (i // per, 0, 0)),
            pl.BlockSpec((d, LANES), lambda i: (0, 0)),
            pl.BlockSpec((1, LANES), lambda i: (0, 0)),
        ],
        out_specs=(pl.BlockSpec((tm, d), lambda i: (i, 0)), pl.BlockSpec((tm, LANES), lambda i: (i, 0)),
                   pl.BlockSpec((tm, LANES), lambda i: (i, 0)), pl.BlockSpec((1, LANES), lambda i: (0, 0))),
        scratch_shapes=[pltpu.VMEM((tm, tm), BF16), pltpu.VMEM((1, LANES), F32)],
        compiler_params=_cparams(("arbitrary",)),
        name="moe_route",
    )(x, mod, rw, rb)


def _moe_plan(srank, counts, n_blocks, n_items):
    e_n = N_EXPERTS
    n = srank.shape[0]
    i32 = jnp.int32
    cnt = counts[0, :e_n].astype(i32)
    nb = (cnt + MOE_BM - 1) // MOE_BM
    nb_incl = jnp.cumsum(nb)
    gstart = nb_incl - nb
    n_used = nb_incl[-1]
    sr = srank[:, :e_n]
    rank = jnp.abs(sr).astype(i32)
    pos_tok = jnp.where(sr > 0, rank - 1 + MOE_BM * gstart[None, :], -1)
    rank_t = rank.T

    r = jnp.arange(n_blocks, dtype=i32)
    used = r < n_used
    rc = jnp.minimum(r, n_used - 1)
    e_r = jnp.minimum(jnp.sum(nb_incl[None, :] <= rc[:, None], axis=1, dtype=i32), e_n - 1)
    b = rc - gstart[e_r]
    lo = b * MOE_BM + 1
    hi = jnp.minimum((b + 1) * MOE_BM, cnt[e_r])
    def find(e, v):
        return jnp.sum(rank_t[e] < v[:, None], axis=1, dtype=i32)

    jlo = jnp.where(used, find(e_r, lo) // MOE_TC, 0)
    jhi = jnp.where(used, find(e_r, hi) // MOE_TC, -1)

    n_tiles = n // MOE_TMC
    ends = rank[MOE_TMC - 1::MOE_TMC]
    starts = jnp.concatenate([jnp.zeros((1, e_n), i32), ends[:-1]], axis=0)
    fb = gstart[None, :] + starts // MOE_BM
    lb = gstart[None, :] + (ends - 1) // MOE_BM
    n_pe = jnp.where(ends > starts, lb - fb + 1, 0).reshape(-1)
    incl = jnp.cumsum(n_pe)
    off = incl - n_pe
    total = incl[-1]
    w = jnp.arange(n_items, dtype=i32)
    valid = w < total
    wc = jnp.minimum(w, total - 1)
    p = jnp.sum(incl[None, :] <= wc[:, None], axis=1, dtype=i32)
    it_tile = p // e_n
    it_e = p % e_n
    it_blk = fb.reshape(-1)[p] + (wc - off[p])
    tile_off = jnp.concatenate([off[::e_n], total[None]])
    it_first = (wc == tile_off[it_tile]).astype(i32)
    it_last = (wc == tile_off[it_tile + 1] - 1).astype(i32)
    return (pos_tok, e_r, jlo, jhi, n_used.reshape(1),
            it_tile, it_blk, it_e, it_first, it_last, valid.astype(i32))


def _experts_kernel(be_ref, jlo_ref, jhi_ref, nused_ref, h_ref, pos_ref, wgu_ref, wd_ref, y_ref, xg_ref):
    r = pl.program_id(0)

    @pl.when(r < nused_ref[0])
    def _():
        e = be_ref[r]
        slot = r * MOE_BM + lax.broadcasted_iota(jnp.int32, (MOE_BM, MOE_TC), 0)
        xg_ref[...] = jnp.zeros_like(xg_ref)

        def chunk(j, carry):
            onehot = jnp.where(pos_ref[e, pl.ds(j, 1), :] == slot, 1.0, 0.0).astype(BF16)
            rows = h_ref[pl.ds(pl.multiple_of(j * MOE_TC, MOE_TC), MOE_TC), :]
            xg_ref[...] += jnp.dot(onehot, rows, preferred_element_type=F32)
            return carry

        lax.fori_loop(jlo_ref[r], jhi_ref[r] + 1, chunk, 0)
        y_ref[...] = _swiglu_chunks(xg_ref[...].astype(BF16), wgu_ref, wd_ref, D_FF_EXPERT)

    @pl.when(r >= nused_ref[0])
    def _():
        y_ref[...] = jnp.zeros_like(y_ref)


def _experts(h, pos_row, plan, wgu_bf, wd_bf, n_blocks):
    n, d = h.shape
    e_r, jlo, jhi, n_used = plan
    grid_spec = pltpu.PrefetchScalarGridSpec(
        num_scalar_prefetch=4,
        grid=(n_blocks,),
        in_specs=[
            _resident((n, d), lambda r, be, lo, hi, nu: (0, 0)),
            _resident(pos_row.shape, lambda r, be, lo, hi, nu: (0, 0, 0)),
            pl.BlockSpec((None, d, 2 * D_FF_EXPERT), lambda r, be, lo, hi, nu: (be[r], 0, 0),
                         pipeline_mode=pl.Buffered(1)),
            pl.BlockSpec((None, D_FF_EXPERT, d), lambda r, be, lo, hi, nu: (be[r], 0, 0)),
        ],
        out_specs=pl.BlockSpec((MOE_BM, d), lambda r, be, lo, hi, nu: (r, 0)),
        scratch_shapes=[pltpu.VMEM((MOE_BM, d), F32)],
    )
    return pl.pallas_call(
        _experts_kernel,
        out_shape=jax.ShapeDtypeStruct((n_blocks * MOE_BM, d), F32),
        grid_spec=grid_spec,
        compiler_params=_cparams(("arbitrary",)),
        name="moe_experts",
    )(e_r, jlo, jhi, n_used, h, pos_row, wgu_bf, wd_bf)


def _combine_kernel(tile_ref, blk_ref, e_ref, first_ref, last_ref, valid_ref,
                    y_ref, pos_ref, cw_ref, x_ref, mod_ref, g_ref, b_ref, o_ref, acc_ref):
    w = pl.program_id(0)

    @pl.when(valid_ref[w] == 1)
    def _():
        @pl.when(first_ref[w] == 1)
        def _():
            acc_ref[...] = jnp.zeros_like(acc_ref)

        e = e_ref[w]
        lane = lax.broadcasted_iota(jnp.int32, (MOE_TMC, LANES), 1)
        pos_e = jnp.sum(jnp.where(lane == e, pos_ref[...], 0.0), axis=-1, keepdims=True)
        cw_e = jnp.sum(jnp.where(lane == e, cw_ref[...], 0.0), axis=-1, keepdims=True)
        slot = (blk_ref[w] * MOE_BM + lax.broadcasted_iota(jnp.int32, (MOE_TMC, MOE_BM), 1)).astype(F32)
        onehot = jnp.where(pos_e == slot, 1.0, 0.0).astype(BF16)
        y_hi, y_lo = _split_bf16(y_ref[...])
        part = (jnp.dot(onehot, y_hi, preferred_element_type=F32)
                + jnp.dot(onehot, y_lo, preferred_element_type=F32))
        acc_ref[...] += cw_e * part

        @pl.when(last_ref[w] == 1)
        def _():
            z = ALPHA * x_ref[...] + mod_ref[5:6, :] * acc_ref[...]
            o_ref[...] = _layer_norm_rows(z, g_ref[...], b_ref[...])


def _combine(y, pos_tok_f, cw, x, mod, ln_g, ln_b, items, n_items):
    g, s, d = x.shape
    per = s // MOE_TMC

    def tok2(w, tile, *_):
        return (tile[w], 0)

    def tok3(w, tile, *_):
        return (tile[w] // per, tile[w] % per, 0)

    grid_spec = pltpu.PrefetchScalarGridSpec(
        num_scalar_prefetch=6,
        grid=(n_items,),
        in_specs=[
            pl.BlockSpec((MOE_BM, d), lambda w, tile, blk, *_: (blk[w], 0)),
            pl.BlockSpec((MOE_TMC, LANES), tok2),
            pl.BlockSpec((MOE_TMC, LANES), tok2),
            pl.BlockSpec((None, MOE_TMC, d), tok3),
            pl.BlockSpec((None, 6, d), lambda w, tile, *_: (tile[w] // per, 0, 0)),
            pl.BlockSpec((1, d), lambda w, *_: (0, 0)),
            pl.BlockSpec((1, d), lambda w, *_: (0, 0)),
        ],
        out_specs=pl.BlockSpec((None, MOE_TMC, d), tok3),
        scratch_shapes=[pltpu.VMEM((MOE_TMC, d), F32)],
    )
    return pl.pallas_call(
        _combine_kernel,
        out_shape=jax.ShapeDtypeStruct((g, s, d), F32),
        grid_spec=grid_spec,
        compiler_params=_cparams(("arbitrary",)),
        name="moe_combine",
    )(*items, y, pos_tok_f, cw, x, mod, ln_g.reshape(1, d), ln_b.reshape(1, d))


def _moe(x, mod, router_w, router_b, wgu_bf, wd_bf, ln_g, ln_b):
    g, s, d = x.shape
    n = g * s
    n_slots = 2 * n
    n_blocks = n_slots // MOE_BM + N_EXPERTS
    n_items = n_blocks + N_EXPERTS * (n // MOE_TMC)
    h, cw, srank, counts = _route(x, mod, router_w, router_b)
    plan = _moe_plan(srank, counts, n_blocks, n_items)
    pos_tok = plan[0]
    pos_row = pos_tok.T.reshape(N_EXPERTS, n // MOE_TC, MOE_TC)
    y = _experts(h, pos_row, plan[1:5], wgu_bf, wd_bf, n_blocks)
    pos_tok_f = jnp.pad(pos_tok.astype(F32), ((0, 0), (0, LANES - N_EXPERTS)), constant_values=-1.0)
    return _combine(y, pos_tok_f, cw, x, mod, ln_g, ln_b, plan[5:], n_items)


def _inproj_c_kernel(x_ref, mod_ref, w_ref, wg_ref, bg_ref, o_ref, og_ref):
    h = (x_ref[...] * (1.0 + mod_ref[1:2, :]) + mod_ref[0:1, :]).astype(BF16)
    o_ref[...] = jnp.dot(h, w_ref[...], preferred_element_type=F32)
    og_ref[...] = jnp.dot(h, wg_ref[...], preferred_element_type=F32) + bg_ref[...]


def _inproj_c(x, mod, w_bf, wg_bf, bg, tm=512):
    g, s, d = x.shape
    n = w_bf.shape[1]
    ng = wg_bf.shape[1]
    return pl.pallas_call(
        _inproj_c_kernel,
        out_shape=(jax.ShapeDtypeStruct((g, s, n), F32), jax.ShapeDtypeStruct((g, s, ng), F32)),
        grid=(g, s // tm),
        in_specs=[
            pl.BlockSpec((None, tm, d), lambda gi, ti: (gi, ti, 0)),
            pl.BlockSpec((None, 6, d), lambda gi, ti: (gi, 0, 0)),
            _resident((d, n), lambda gi, ti: (0, 0)),
            _resident((d, ng), lambda gi, ti: (0, 0)),
            pl.BlockSpec((1, ng), lambda gi, ti: (0, 0)),
        ],
        out_specs=(pl.BlockSpec((None, tm, n), lambda gi, ti: (gi, ti, 0)),
                   pl.BlockSpec((None, tm, ng), lambda gi, ti: (gi, ti, 0))),
        compiler_params=_cparams(("arbitrary", "arbitrary")),
        name="inproj_c",
    )(x, mod, w_bf, wg_bf, bg)


def _log_sigmoid(x):
    return jnp.minimum(x, 0.0) - jnp.log(1.0 + jnp.exp(-jnp.abs(x)))


MLSTM_L = 128


def _split3_bf16(x):
    hi = x.astype(BF16)
    r1 = x - hi.astype(F32)
    mid = r1.astype(BF16)
    lo = (r1 - mid.astype(F32)).astype(BF16)
    return hi, mid, lo


def _mlstm_kernel(*refs, seq, hg, has_init, emit_state):
    q_ref, k_ref, v_ref, o_ref, gi_ref, gf_ref, hgain_ref = refs[:7]
    pos = 7
    if has_init:
        c0_ref, n0_ref, m0_ref = refs[pos:pos + 3]
        pos += 3
    out_ref = refs[pos]
    pos += 1
    if emit_state:
        co_ref, no_ref, mo_ref = refs[pos:pos + 3]
        pos += 3
    cext_ref, hf_ref, hb_ref, b_ref, g_ref, gmax_ref, mt_ref, wi_ref, en_ref, ws_ref, gt_ref, wc_ref = refs[pos:]

    L = MLSTM_L
    dh = MLSTM_DH
    nh = MLSTM_HEADS
    nc = seq // L
    head0 = pl.program_id(1) * hg
    neg = -jnp.inf

    lane = lax.broadcasted_iota(jnp.int32, (L, LANES), 1)
    lane1 = lax.broadcasted_iota(jnp.int32, (1, LANES), 1)
    row = lax.broadcasted_iota(jnp.int32, (L, L), 0)
    col = lax.broadcasted_iota(jnp.int32, (L, L), 1)
    lower = col <= row
    upper = col >= row
    tri_l = jnp.where(lower, 1.0, 0.0).astype(BF16)
    tri_u = jnp.where(upper, 1.0, 0.0).astype(BF16)
    fwd_lane = lane < nh
    fwd_lane1 = lane1 < nh
    trow = lax.broadcasted_iota(jnp.int32, (L, LANES), 0)

    btot, glast = [], []
    for c in range(nc):
        rows = slice(c * L, (c + 1) * L)
        f = _log_sigmoid(gf_ref[rows, :])
        parts = _split3_bf16(f)
        pre = sum(jnp.dot(tri_l, p, preferred_element_type=F32) for p in parts)
        suf = sum(jnp.dot(tri_u, p, preferred_element_type=F32) for p in parts)
        b = jnp.where(fwd_lane, pre, suf)
        g = gi_ref[rows, :] - b
        gp, gs = g, g
        k = 1
        while k < L:
            gp = jnp.where(trow >= k, jnp.maximum(gp, pltpu.roll(gp, k, 0)), gp)
            gs = jnp.where(trow < L - k, jnp.maximum(gs, pltpu.roll(gs, L - k, 0)), gs)
            k *= 2
        gmax = jnp.where(fwd_lane, gp, gs)
        b_ref[rows, :] = b
        g_ref[rows, :] = g
        gmax_ref[rows, :] = gmax
        btot.append(jnp.where(fwd_lane1, b[L - 1:L, :], b[0:1, :]))
        glast.append(jnp.where(fwd_lane1, gmax[L - 1:L, :], gmax[0:1, :]))

    m_init = m0_ref[...] if has_init else jnp.zeros((1, LANES), F32)
    mf, mb = m_init, m_init
    ms_f, mn_f, ms_b, mn_b = [None] * nc, [None] * nc, [None] * nc, [None] * nc
    for c in range(nc):
        ms_f[c] = mf
        mf = btot[c] + jnp.maximum(mf, glast[c])
        mn_f[c] = mf
        cb = nc - 1 - c
        ms_b[cb] = mb
        mb = btot[cb] + jnp.maximum(mb, glast[cb])
        mn_b[cb] = mb
    m_final = jnp.where(fwd_lane1, mf, mb)

    for c in range(nc):
        rows = slice(c * L, (c + 1) * L)
        m_start = jnp.where(fwd_lane1, ms_f[c], ms_b[c])
        m_next = jnp.where(fwd_lane1, mn_f[c], mn_b[c])
        g = g_ref[rows, :]
        mt = jnp.maximum(m_start, gmax_ref[rows, :])
        mt_ref[rows, :] = mt
        wi_ref[rows, :] = jnp.exp(m_start - mt)
        en_ref[rows, :] = jnp.exp(-(b_ref[rows, :] + mt))
        ws_ref[rows, :] = jnp.exp(btot[c] + g - m_next)
        gt_ref[c] = g.T
        wc_ref[c:c + 1, :] = jnp.exp(btot[c] + m_start - m_next)

    lane_d = lax.broadcasted_iota(jnp.int32, (dh, dh), 1)
    for d in range(2):
        for hh in range(hg):
            idx = d * hg + hh
            if has_init:
                cext_ref[idx, :, 0:dh] = c0_ref[d, hh]
                cext_ref[idx, :, dh:2 * dh] = jnp.where(lane_d == 0, n0_ref[d, hh], 0.0)
            else:
                cext_ref[idx] = jnp.zeros((dh, 2 * dh), F32)

    ones_col = jnp.where(lane == 0, 1.0, 0.0).astype(BF16)
    nt = (((1,), (1,)), ((), ()))
    tn = (((0,), (0,)), ((), ()))

    def column(x, j):
        return jnp.sum(jnp.where(lane == j, x, 0.0), axis=-1, keepdims=True)

    def one_direction(d, hh, c, s_qk, q_bf, k_s, v_ext, v_bf):
        idx = d * hg + hh
        j = d * nh + head0 + hh
        rows = pl.ds(pl.multiple_of(c * L, L), L)
        mt = column(mt_ref[rows, :], j)
        wi = column(wi_ref[rows, :], j)
        en = column(en_ref[rows, :], j)
        ws = column(ws_ref[rows, :], j)
        g_r = gt_ref[c, pl.ds(j, 1), :]
        w_c = jnp.sum(jnp.where(lane1 == j, wc_ref[pl.ds(c, 1), :], 0.0), axis=-1, keepdims=True)
        causal = lower if d == 0 else upper
        p = s_qk * jnp.exp(jnp.where(causal, g_r - mt, neg))
        qc = jnp.dot(q_bf, cext_ref[idx].astype(BF16), preferred_element_type=F32)
        num = wi * qc[:, 0:dh] + jnp.dot(p.astype(BF16), v_bf, preferred_element_type=F32)
        den = wi * qc[:, dh:dh + 1] + jnp.sum(p, axis=-1, keepdims=True)
        h = num / jnp.maximum(jnp.abs(den), en)
        upd = lax.dot_general((ws * k_s).astype(BF16), v_ext, tn, preferred_element_type=F32)
        cext_ref[idx] = w_c * cext_ref[idx] + upd
        return h

    def load_chunk(hh, c):
        sl = (pl.ds(pl.multiple_of(c * L, L), L), slice(hh * dh, (hh + 1) * dh))
        q_bf = q_ref[sl].astype(BF16)
        k_s = k_ref[sl] * (dh ** -0.5)
        v_bf = v_ref[sl].astype(BF16)
        v_ext = jnp.concatenate([v_bf, ones_col], axis=-1)
        s_qk = lax.dot_general(q_bf, k_s.astype(BF16), nt, preferred_element_type=F32)
        return s_qk, q_bf, k_s, v_ext, v_bf

    def step(c, carry):
        cb = nc - 1 - c
        for hh in range(hg):
            h = one_direction(0, hh, c, *load_chunk(hh, c))
            hf_ref[pl.ds(pl.multiple_of(c * L, L), L), hh * dh:(hh + 1) * dh] = h
        for hh in range(hg):
            h = one_direction(1, hh, cb, *load_chunk(hh, cb))
            hb_ref[pl.ds(pl.multiple_of(cb * L, L), L), hh * dh:(hh + 1) * dh] = h
        return carry

    lax.fori_loop(0, nc, step, 0)

    for hh in range(hg):
        cs = slice(hh * dh, (hh + 1) * dh)
        hs = hf_ref[:, cs] + hb_ref[:, cs]
        mu = jnp.mean(hs, axis=-1, keepdims=True)
        hc = hs - mu
        var = jnp.mean(hc * hc, axis=-1, keepdims=True)
        hn = hc * lax.rsqrt(var + LN_EPS) * hgain_ref[:, cs]
        out_ref[:, cs] = (_sigmoid(o_ref[:, cs]) * hn).astype(out_ref.dtype)

    if emit_state:
        for d in range(2):
            for hh in range(hg):
                idx = d * hg + hh
                co_ref[d, hh] = cext_ref[idx, :, 0:dh]
                no_ref[d, hh] = cext_ref[idx, :, dh:dh + 1]
        mo_ref[...] = m_final


def _mlstm(proj, gates, head_g, g0, n_seq, seq, hg, init=None, emit_state=False):
    g, s, _ = proj.shape
    per_group = s // seq
    n_hg = MLSTM_HEADS // hg
    w = hg * MLSTM_DH
    nc = seq // MLSTM_L
    n_blocks = D_MODEL // w

    def tok_map(colblock):
        return lambda b, hi: (g0 + b // per_group, b % per_group, colblock * n_blocks + hi)

    def gate_map(half):
        return lambda b, hi: (g0 + b // per_group, b % per_group, half)

    args = [proj, proj, proj, proj, gates, gates, head_g.reshape(1, D_MODEL)]
    in_specs = [
        pl.BlockSpec((None, seq, w), tok_map(0)),
        pl.BlockSpec((None, seq, w), tok_map(1)),
        pl.BlockSpec((None, seq, w), tok_map(2)),
        pl.BlockSpec((None, seq, w), tok_map(3)),
        pl.BlockSpec((None, seq, LANES), gate_map(0)),
        pl.BlockSpec((None, seq, LANES), gate_map(1)),
        pl.BlockSpec((1, w), lambda b, hi: (0, hi)),
    ]
    if init is not None:
        c0, n0, m0 = init
        m0_lanes = jnp.pad(m0.reshape(n_seq, 1, 2 * MLSTM_HEADS), ((0, 0), (0, 0), (0, LANES - 2 * MLSTM_HEADS)))
        args += [c0, n0.reshape(n0.shape + (1,)), m0_lanes]
        in_specs += [
            pl.BlockSpec((None, 2, hg, MLSTM_DH, MLSTM_DH), lambda b, hi: (b, 0, hi, 0, 0)),
            pl.BlockSpec((None, 2, hg, MLSTM_DH, 1), lambda b, hi: (b, 0, hi, 0, 0)),
            pl.BlockSpec((None, 1, LANES), lambda b, hi: (b, 0, 0)),
        ]

    out_shape = [jax.ShapeDtypeStruct((n_seq // per_group, s, D_MODEL), BF16)]
    out_specs = [pl.BlockSpec((None, seq, w), lambda b, hi: (b // per_group, b % per_group, hi))]
    if emit_state:
        out_shape += [
            jax.ShapeDtypeStruct((n_seq, 2, MLSTM_HEADS, MLSTM_DH, MLSTM_DH), F32),
            jax.ShapeDtypeStruct((n_seq, 2, MLSTM_HEADS, MLSTM_DH, 1), F32),
            jax.ShapeDtypeStruct((n_seq, n_hg, 1, LANES), F32),
        ]
        out_specs += [
            pl.BlockSpec((None, 2, hg, MLSTM_DH, MLSTM_DH), lambda b, hi: (b, 0, hi, 0, 0)),
            pl.BlockSpec((None, 2, hg, MLSTM_DH, 1), lambda b, hi: (b, 0, hi, 0, 0)),
            pl.BlockSpec((None, None, 1, LANES), lambda b, hi: (b, hi, 0, 0)),
        ]

    tok_scratch = pltpu.VMEM((seq, LANES), F32)
    return pl.pallas_call(
        functools.partial(_mlstm_kernel, seq=seq, hg=hg, has_init=init is not None, emit_state=emit_state),
        out_shape=tuple(out_shape),
        grid=(n_seq, n_hg),
        in_specs=in_specs,
        out_specs=tuple(out_specs),
        scratch_shapes=[
            pltpu.VMEM((2 * hg, MLSTM_DH, 2 * MLSTM_DH), F32),
            pltpu.VMEM((seq, w), F32),
            pltpu.VMEM((seq, w), F32),
        ] + [tok_scratch] * 7 + [
            pltpu.VMEM((nc, LANES, MLSTM_L), F32),
            pltpu.VMEM((max(nc, 8), LANES), F32),
        ],
        compiler_params=_cparams(("arbitrary", "arbitrary")),
        name="mlstm_%d" % seq,
    )(*args)


def _mlstm_kernel_old(*refs, seq, hg, has_init, emit_state):
    q_ref, k_ref, v_ref, o_ref, gc_ref, gr_ref, hgain_ref = refs[:7]
    pos = 7
    if has_init:
        c0_ref, n0_ref, m0_ref = refs[pos:pos + 3]
        pos += 3
    out_ref = refs[pos]
    pos += 1
    if emit_state:
        co_ref, no_ref, mo_ref = refs[pos:pos + 3]
        pos += 3
    cext_ref, hf_ref, hb_ref = refs[pos:pos + 3]

    L = MLSTM_CHUNK
    dh = MLSTM_DH
    nc = seq // L
    head0 = pl.program_id(1) * hg
    neg = -jnp.inf

    lane_d = lax.broadcasted_iota(jnp.int32, (dh, dh), 1)
    for d in range(2):
        for hh in range(hg):
            idx = d * hg + hh
            if has_init:
                cext_ref[idx, :, 0:dh] = c0_ref[d, hh]
                cext_ref[idx, :, dh:2 * dh] = jnp.where(lane_d == 0, n0_ref[d, hh], 0.0)
            else:
                cext_ref[idx] = jnp.zeros((dh, 2 * dh), F32)

    row = lax.broadcasted_iota(jnp.int32, (L, L), 0)
    col = lax.broadcasted_iota(jnp.int32, (L, L), 1)
    lower = col <= row
    upper = col >= row
    lane_g = lax.broadcasted_iota(jnp.int32, (L, LANES), 1)
    ones_col = jnp.where(lane_g == 0, 1.0, 0.0).astype(BF16)
    nt = (((1,), (1,)), ((), ()))
    tn = (((0,), (0,)), ((), ()))

    def one_direction(d, hh, c, s_qk, q_bf, k_s, v_ext, v_bf, m_prev):
        idx = d * hg + hh
        head = head0 + hh
        causal, anti = (lower, upper) if d == 0 else (upper, lower)
        gates_c = gc_ref[pl.ds(c * L, L), :]

        def col_of(j):
            return jnp.sum(jnp.where(lane_g == j, gates_c, 0.0), axis=-1, keepdims=True)

        i_c = col_of((2 * d) * MLSTM_HEADS + head)
        f_c = _log_sigmoid(col_of((2 * d + 1) * MLSTM_HEADS + head))
        i_r = gr_ref[2 * d, hh, pl.ds(c, 1), :]
        f_r = _log_sigmoid(gr_ref[2 * d + 1, hh, pl.ds(c, 1), :])

        b_c = jnp.sum(jnp.where(causal, f_r, 0.0), axis=1, keepdims=True)
        b_r = jnp.sum(jnp.where(anti, f_c, 0.0), axis=0, keepdims=True)
        b_tot = jnp.sum(f_r, axis=1, keepdims=True)
        dmat = jnp.where(causal, b_c - b_r + i_r, neg)
        m_inter = b_c + m_prev
        m_t = jnp.maximum(m_inter, jnp.max(dmat, axis=-1, keepdims=True))
        w_inter = jnp.exp(m_inter - m_t)
        p = s_qk * jnp.exp(dmat - m_t)
        qc = jnp.dot(q_bf, cext_ref[idx].astype(BF16), preferred_element_type=F32)
        num = w_inter * qc[:, 0:dh] + jnp.dot(p.astype(BF16), v_bf, preferred_element_type=F32)
        den = w_inter * qc[:, dh:dh + 1] + jnp.sum(p, axis=-1, keepdims=True)
        h = num / jnp.maximum(jnp.abs(den), jnp.exp(-m_t))
        last = L - 1 if d == 0 else 0
        m_new = m_t[last:last + 1, :]
        w_c = jnp.exp(b_tot + m_prev - m_new)
        w_s = jnp.exp(b_tot - b_c + i_c - m_new)
        upd = lax.dot_general((w_s * k_s).astype(BF16), v_ext, tn, preferred_element_type=F32)
        cext_ref[idx] = w_c * cext_ref[idx] + upd
        return h, m_new

    def load_chunk(hh, c):
        sl = (pl.ds(c * L, L), slice(hh * dh, (hh + 1) * dh))
        q_bf = q_ref[sl].astype(BF16)
        k_s = k_ref[sl] * (dh ** -0.5)
        v_bf = v_ref[sl].astype(BF16)
        v_ext = jnp.concatenate([v_bf, ones_col], axis=-1)
        s_qk = lax.dot_general(q_bf, k_s.astype(BF16), nt, preferred_element_type=F32)
        return s_qk, q_bf, k_s, v_ext, v_bf

    def step(c, ms):
        cb = nc - 1 - c
        new_ms = []
        for hh in range(hg):
            h, m_new = one_direction(0, hh, c, *load_chunk(hh, c), ms[hh])
            hf_ref[pl.ds(c * L, L), hh * dh:(hh + 1) * dh] = h
            new_ms.append(m_new)
        for hh in range(hg):
            h, m_new = one_direction(1, hh, cb, *load_chunk(hh, cb), ms[hg + hh])
            hb_ref[pl.ds(cb * L, L), hh * dh:(hh + 1) * dh] = h
            new_ms.append(m_new)
        return tuple(new_ms)

    if has_init:
        ms0 = tuple(m0_ref[d, hh] for d in range(2) for hh in range(hg))
    else:
        ms0 = tuple(jnp.zeros((1, 1), F32) for _ in range(2 * hg))
    ms = lax.fori_loop(0, nc, step, ms0)

    for hh in range(hg):
        cs = slice(hh * dh, (hh + 1) * dh)
        hs = hf_ref[:, cs] + hb_ref[:, cs]
        mu = jnp.mean(hs, axis=-1, keepdims=True)
        hc = hs - mu
        var = jnp.mean(hc * hc, axis=-1, keepdims=True)
        hn = hc * lax.rsqrt(var + LN_EPS) * hgain_ref[:, cs]
        out_ref[:, cs] = (_sigmoid(o_ref[:, cs]) * hn).astype(out_ref.dtype)

    if emit_state:
        for d in range(2):
            for hh in range(hg):
                idx = d * hg + hh
                co_ref[d, hh] = cext_ref[idx, :, 0:dh]
                no_ref[d, hh] = cext_ref[idx, :, dh:dh + 1]
                mo_ref[d, hh] = jnp.broadcast_to(ms[idx], (1, LANES))


def _mlstm_old(proj, gates, head_g, g0, n_seq, seq, hg, init=None, emit_state=False):
    g, s, _ = proj.shape
    per_group = s // seq
    n_hg = MLSTM_HEADS // hg
    w = hg * MLSTM_DH
    nc = seq // MLSTM_CHUNK
    n_blocks = D_MODEL // w

    g_seq = gates[g0:g0 + n_seq // per_group].reshape(n_seq, seq, LANES)
    g_row = g_seq[:, :, :N_GATES * MLSTM_HEADS].transpose(0, 2, 1).reshape(
        n_seq, N_GATES, MLSTM_HEADS, nc, MLSTM_CHUNK)

    def tok_map(colblock):
        return lambda b, hi: (g0 + b // per_group, b % per_group, colblock * n_blocks + hi)

    args = [proj, proj, proj, proj, g_seq, g_row, head_g.reshape(1, D_MODEL)]
    in_specs = [
        pl.BlockSpec((None, seq, w), tok_map(0)),
        pl.BlockSpec((None, seq, w), tok_map(1)),
        pl.BlockSpec((None, seq, w), tok_map(2)),
        pl.BlockSpec((None, seq, w), tok_map(3)),
        pl.BlockSpec((None, seq, LANES), lambda b, hi: (b, 0, 0)),
        pl.BlockSpec((None, N_GATES, hg, nc, MLSTM_CHUNK), lambda b, hi: (b, 0, hi, 0, 0)),
        pl.BlockSpec((1, w), lambda b, hi: (0, hi)),
    ]
    if init is not None:
        c0, n0, m0 = init
        args += [c0, n0.reshape(n0.shape + (1,)), m0.reshape(m0.shape + (1, 1))]
        in_specs += [
            pl.BlockSpec((None, 2, hg, MLSTM_DH, MLSTM_DH), lambda b, hi: (b, 0, hi, 0, 0)),
            pl.BlockSpec((None, 2, hg, MLSTM_DH, 1), lambda b, hi: (b, 0, hi, 0, 0)),
            pl.BlockSpec((None, 2, hg, 1, 1), lambda b, hi: (b, 0, hi, 0, 0)),
        ]
    out_shape = [jax.ShapeDtypeStruct((n_seq // per_group, s, D_MODEL), BF16)]
    out_specs = [pl.BlockSpec((None, seq, w), lambda b, hi: (b // per_group, b % per_group, hi))]
    if emit_state:
        out_shape += [
            jax.ShapeDtypeStruct((n_seq, 2, MLSTM_HEADS, MLSTM_DH, MLSTM_DH), F32),
            jax.ShapeDtypeStruct((n_seq, 2, MLSTM_HEADS, MLSTM_DH, 1), F32),
            jax.ShapeDtypeStruct((n_seq, 2, MLSTM_HEADS, 1, LANES), F32),
        ]
        out_specs += [
            pl.BlockSpec((None, 2, hg, MLSTM_DH, MLSTM_DH), lambda b, hi: (b, 0, hi, 0, 0)),
            pl.BlockSpec((None, 2, hg, MLSTM_DH, 1), lambda b, hi: (b, 0, hi, 0, 0)),
            pl.BlockSpec((None, 2, hg, 1, LANES), lambda b, hi: (b, 0, hi, 0, 0)),
        ]

    return pl.pallas_call(
        functools.partial(_mlstm_kernel, seq=seq, hg=hg, has_init=init is not None, emit_state=emit_state),
        out_shape=tuple(out_shape),
        grid=(n_seq, n_hg),
        in_specs=in_specs,
        out_specs=tuple(out_specs),
        scratch_shapes=[
            pltpu.VMEM((2 * hg, MLSTM_DH, 2 * MLSTM_DH), F32),
            pltpu.VMEM((seq, w), F32),
            pltpu.VMEM((seq, w), F32),
        ],
        compiler_params=_cparams(("arbitrary", "arbitrary")),
        name="mlstm_%d" % seq,
    )(*args)


def kernel(x_prompt, x_sample, c, cache_k, cache_v, state_C, state_n, state_m, c_ctx, ada_w, ada_b, ln_g, ln_b, w_in_a, diff_lambda, diff_norm_g, pool_w, pool_scale, w_out_a, ffn_w_gu, ffn_w_down, w_in_c, b_gates_c, mlstm_norm_g, w_out_c, router_w, router_b, moe_w_gu, moe_w_down):
    n_ctx, seq_ctx, d = x_prompt.shape
    n_lat, seq_lat, _ = x_sample.shape
    assert d == D_MODEL and (n_ctx * seq_ctx) % seq_lat == 0 and seq_lat % seq_ctx == 0
    gl = n_lat
    gc = n_ctx * seq_ctx // seq_lat
    s = seq_lat

    x = jnp.concatenate([x_sample, x_prompt.reshape(gc, s, d)], axis=0)
    cvec = jnp.concatenate([c, jnp.broadcast_to(c_ctx[None, :], (gc, d))], axis=0)
    mod_all = _modulation(cvec, ada_w, ada_b).reshape(DEPTH, gl + gc, 6, d)

    mod = mod_all[0]
    lam_init = 0.8 - 0.6 * math.exp(-0.3 * 0)
    cos_t, sin_t = _rope_tables(s)
    proj = _inproj_a(x, mod, w_in_a[0].astype(BF16), cos_t, sin_t, gl)
    norm_g = diff_norm_g[0].reshape(1, LANES)
    attn_c, new_k, new_v = _attn_context(proj, diff_lambda[0], norm_g, gl, n_ctx, seq_ctx, lam_init)
    attn_l = _attn_latent(proj, cache_k, cache_v, diff_lambda[0], norm_g, gl, lam_init)
    pool_c = _pool(proj, pool_w[0], pool_scale[0], gl, gc, seq_ctx)
    pool_l = _pool(proj, pool_w[0], pool_scale[0], 0, gl, seq_lat)
    w_out = w_out_a[0].astype(BF16)
    x = _outproj([(attn_l, attn_c), (pool_l, pool_c)], [w_out[:DIFF_WIDTH], w_out[DIFF_WIDTH:]],
                 x, mod, ln_g[0, 0], ln_b[0, 0], 2)
    x = _ffn(x, mod, ffn_w_gu[0].astype(BF16), ffn_w_down[0].astype(BF16), ln_g[0, 1], ln_b[0, 1])

    mod = mod_all[1]
    n_main = 4 * D_MODEL
    w_main = w_in_c[0][:, :n_main].astype(BF16)
    nh = MLSTM_HEADS
    wg4 = w_in_c[0][:, n_main:].reshape(d, N_GATES, nh)
    bg4 = b_gates_c[0].reshape(1, N_GATES, nh)
    lane_pad = ((0, 0), (0, LANES - 2 * nh))

    def gate_lanes(a):
        return jnp.concatenate([jnp.pad(jnp.concatenate([a[:, 0], a[:, 2]], axis=-1), lane_pad),
                                jnp.pad(jnp.concatenate([a[:, 1], a[:, 3]], axis=-1), lane_pad)], axis=-1)

    proj, gates = _inproj_c(x, mod, w_main, gate_lanes(wg4).astype(BF16), gate_lanes(bg4))
    mix_c, new_c, new_n, new_m = _mlstm(proj, gates, mlstm_norm_g[0], gl, n_ctx, seq_ctx, 4, emit_state=True)
    (mix_l,) = _mlstm(proj, gates, mlstm_norm_g[0], 0, n_lat, seq_lat, 4,
                      init=(state_C[:, 0], state_n[:, 0], state_m[:, 0]))
    x = _outproj([(mix_l, mix_c)], [w_out_c[0].astype(BF16)], x, mod, ln_g[1, 0], ln_b[1, 0], 2)
    x = _moe(x, mod, router_w[0], router_b[0], moe_w_gu[0].astype(BF16), moe_w_down[0].astype(BF16),
             ln_g[1, 1], ln_b[1, 1])

    y_sample = x[:gl]
    y_prompt = x[gl:].reshape(n_ctx, seq_ctx, d)
    new_m = new_m[:, 0, 0, :2 * MLSTM_HEADS].reshape(n_ctx, 2, MLSTM_HEADS)
    return (y_prompt, y_sample, new_k, new_v, new_c[:, None], new_n[..., 0][:, None], new_m[:, None])
```

```python
import functools
import math

import jax
import jax.numpy as jnp
from jax import lax
from jax.experimental import pallas as pl
from jax.experimental.pallas import tpu as pltpu

F32 = jnp.float32
BF16 = jnp.bfloat16

D_MODEL = 1024
GRID_W = 64
ROPE_BASE = 10000.0
DIFF_HEADS = 4
DIFF_DH = 64
DIFF_WIDTH = DIFF_HEADS * 2 * DIFF_DH
POOL_GROUPS = 4
POOL_GC = 128
POOL_WIDTH = POOL_GROUPS * POOL_GC
POOL_WINDOWS = (2, 4, 8, 16)
W_IN_A = 3 * DIFF_WIDTH + POOL_WIDTH
MLSTM_HEADS = 8
MLSTM_DH = 128
MLSTM_CHUNK = 64
N_GATES = 4
D_FF = 2816
N_EXPERTS = 8
D_FF_EXPERT = 1792
LN_EPS = 1e-5
DEPTH = 2
ALPHA = (2.0 * DEPTH) ** 0.25

LANES = 128
FF_CHUNK = 256
VMEM_LIMIT = 56 * 1024 * 1024


def _cparams(sem):
    return pltpu.CompilerParams(dimension_semantics=sem, vmem_limit_bytes=VMEM_LIMIT)


def _resident(shape, index_map):
    return pl.BlockSpec(shape, index_map, pipeline_mode=pl.Buffered(1))


def _layer_norm_rows(z, g, b):
    mu = jnp.mean(z, axis=-1, keepdims=True)
    zc = z - mu
    var = jnp.mean(zc * zc, axis=-1, keepdims=True)
    return zc * lax.rsqrt(var + LN_EPS) * g + b


def _sigmoid(x):
    return 1.0 / (1.0 + jnp.exp(-x))


def _split_bf16(x):
    hi = x.astype(BF16)
    lo = (x - hi.astype(F32)).astype(BF16)
    return hi, lo


def _mod_kernel(c_ref, w_ref, b_ref, o_ref):
    c = c_ref[...]
    h = (c * _sigmoid(c)).astype(BF16)
    o_ref[...] = jnp.dot(h, w_ref[...].astype(BF16), preferred_element_type=F32) + b_ref[...]


def _modulation(cvec, ada_w, ada_b):
    depth, d, n = ada_w.shape
    g = cvec.shape[0]
    tn = 1536
    return pl.pallas_call(
        _mod_kernel,
        out_shape=jax.ShapeDtypeStruct((depth, g, n), F32),
        grid=(depth, n // tn),
        in_specs=[
            pl.BlockSpec((g, d), lambda l, j: (0, 0)),
            pl.BlockSpec((None, d, tn), lambda l, j: (l, 0, j)),
            pl.BlockSpec((None, 1, tn), lambda l, j: (l, 0, j)),
        ],
        out_specs=pl.BlockSpec((None, g, tn), lambda l, j: (l, 0, j)),
        compiler_params=_cparams(("arbitrary", "arbitrary")),
        name="modulation",
    )(cvec, ada_w, ada_b.reshape(depth, 1, n))


def _rot_half16(x):
    lane = lax.broadcasted_iota(jnp.int32, x.shape, 1)
    return jnp.where((lane % 32) < 16, pltpu.roll(x, LANES - 16, 1), pltpu.roll(x, 16, 1))


def _inproj_a_kernel(x_ref, mod_ref, w_ref, cos_ref, sin_ref, o_ref):
    h = x_ref[...] * (1.0 + mod_ref[1:2, :]) + mod_ref[0:1, :]
    p = jnp.dot(h.astype(BF16), w_ref[...], preferred_element_type=F32)
    cos = cos_ref[...]
    sin = sin_ref[...]
    n_rope = 2 * DIFF_WIDTH // LANES
    for j in range(n_rope):
        blk = p[:, j * LANES:(j + 1) * LANES]
        o_ref[:, j * LANES:(j + 1) * LANES] = blk * cos + _rot_half16(blk) * sin
    o_ref[:, n_rope * LANES:] = p[:, n_rope * LANES:]


def _inproj_a(x, mod, w_bf, cos_t, sin_t, n_latent_groups, tm=512):
    g, s, d = x.shape
    n = w_bf.shape[1]

    def table_map(gi, ti):
        return (jnp.where(gi >= n_latent_groups, 1, 0), ti, 0)

    return pl.pallas_call(
        _inproj_a_kernel,
        out_shape=jax.ShapeDtypeStruct((g, s, n), F32),
        grid=(g, s // tm),
        in_specs=[
            pl.BlockSpec((None, tm, d), lambda gi, ti: (gi, ti, 0)),
            pl.BlockSpec((None, 6, d), lambda gi, ti: (gi, 0, 0)),
            _resident((d, n), lambda gi, ti: (0, 0)),
            pl.BlockSpec((None, tm, LANES), table_map),
            pl.BlockSpec((None, tm, LANES), table_map),
        ],
        out_specs=pl.BlockSpec((None, tm, n), lambda gi, ti: (gi, ti, 0)),
        compiler_params=_cparams(("arbitrary", "arbitrary")),
        name="inproj_a",
    )(x, mod, w_bf, cos_t, sin_t)


def _rope_tables(n_tokens):
    rows = n_tokens // GRID_W
    row_pos = jnp.repeat(jnp.arange(rows), GRID_W).astype(F32)
    col_pos = jnp.tile(jnp.arange(GRID_W), rows).astype(F32)
    n_freq = DIFF_DH // 4
    inv_freq = jnp.power(ROPE_BASE, -jnp.arange(n_freq, dtype=F32) / n_freq)
    ang = jnp.stack([row_pos[:, None] * inv_freq, col_pos[:, None] * inv_freq], axis=1)
    cos, sin = jnp.cos(ang), jnp.sin(ang)
    cos64 = jnp.concatenate([cos[:, 0], cos[:, 0], cos[:, 1], cos[:, 1]], axis=-1)
    sin64 = jnp.concatenate([-sin[:, 0], sin[:, 0], -sin[:, 1], sin[:, 1]], axis=-1)
    cos_l = jnp.tile(cos64, (1, LANES // DIFF_DH))
    sin_l = jnp.tile(sin64, (1, LANES // DIFF_DH))
    cos_t = jnp.stack([cos_l, jnp.ones_like(cos_l)])
    sin_t = jnp.stack([sin_l, jnp.zeros_like(sin_l)])
    return cos_t, sin_t


def _diff_attn_kernel(*refs, n_pieces, lam_init, emit_kv):
    lam_ref, ng_ref, q_ref = refs[:3]
    kv_refs = refs[3:3 + 2 * n_pieces]
    o_ref = refs[3 + 2 * n_pieces]

    lp = lam_ref[...]
    lam = (jnp.exp(jnp.sum(lp[0:1] * lp[1:2], axis=-1, keepdims=True))
           - jnp.exp(jnp.sum(lp[2:3] * lp[3:4], axis=-1, keepdims=True)) + lam_init)

    q = q_ref[...] * (DIFF_DH ** -0.5)
    lane = lax.broadcasted_iota(jnp.int32, q.shape, 1)
    q1 = jnp.where(lane < DIFF_DH, q, 0.0).astype(BF16)
    q2 = jnp.where(lane >= DIFF_DH, q, 0.0).astype(BF16)

    nt = (((1,), (1,)), ((), ()))
    s1, s2, vs = [], [], []
    for i in range(n_pieces):
        kb = kv_refs[2 * i][...].astype(BF16)
        vs.append(kv_refs[2 * i + 1][...].astype(BF16))
        s1.append(lax.dot_general(q1, kb, nt, preferred_element_type=F32))
        s2.append(lax.dot_general(q2, kb, nt, preferred_element_type=F32))

    def softmax_pieces(ss):
        m = functools.reduce(jnp.maximum, [jnp.max(s, axis=-1, keepdims=True) for s in ss])
        es = [jnp.exp(s - m) for s in ss]
        l = functools.reduce(jnp.add, [jnp.sum(e, axis=-1, keepdims=True) for e in es])
        return [e / l for e in es]

    p1 = softmax_pieces(s1)
    p2 = softmax_pieces(s2)
    o = None
    for i in range(n_pieces):
        a = (p1[i] - lam * p2[i]).astype(BF16)
        t = jnp.dot(a, vs[i], preferred_element_type=F32)
        o = t if o is None else o + t
    o = o * lax.rsqrt(jnp.mean(o * o, axis=-1, keepdims=True) + LN_EPS)
    o_ref[...] = (o * ng_ref[...] * (1.0 - lam_init)).astype(o_ref.dtype)
    if emit_kv:
        ko_ref, vo_ref = refs[4 + 2 * n_pieces:]
        ko_ref[...] = kv_refs[0][...]
        vo_ref[...] = kv_refs[1][...]


def _attn_context(proj, lam_p, norm_g, n_latent_groups, n_seq, seq, lam_init):
    g, s, _ = proj.shape
    per_group = s // seq
    blk = (None, seq, LANES)

    def tok_map(col0):
        return lambda b, h: (n_latent_groups + b // per_group, b % per_group, col0 + h)

    cache_shape = jax.ShapeDtypeStruct((n_seq, 1, DIFF_HEADS, seq, LANES), F32)
    cache_spec = pl.BlockSpec((None, None, None, seq, LANES), lambda b, h: (b, 0, h, 0, 0))
    out_spec = pl.BlockSpec(blk, lambda b, h: (b // per_group, b % per_group, h))
    return pl.pallas_call(
        functools.partial(_diff_attn_kernel, n_pieces=1, lam_init=lam_init, emit_kv=True),
        out_shape=(jax.ShapeDtypeStruct((g - n_latent_groups, s, DIFF_WIDTH), BF16), cache_shape, cache_shape),
        grid=(n_seq, DIFF_HEADS),
        in_specs=[
            pl.BlockSpec((4, DIFF_DH), lambda b, h: (0, 0)),
            pl.BlockSpec((1, LANES), lambda b, h: (0, 0)),
            pl.BlockSpec(blk, tok_map(0)),
            pl.BlockSpec(blk, tok_map(DIFF_HEADS)),
            pl.BlockSpec(blk, tok_map(2 * DIFF_HEADS)),
        ],
        out_specs=(out_spec, cache_spec, cache_spec),
        compiler_params=_cparams(("arbitrary", "arbitrary")),
        name="attn_context",
    )(lam_p, norm_g, proj, proj, proj)


def _attn_latent(proj, cache_k, cache_v, lam_p, norm_g, n_latent_groups, lam_init, tq=256):
    g, s, _ = proj.shape
    past = cache_k.shape[3]
    cache_spec = pl.BlockSpec((None, None, None, past, LANES), lambda b, h, qi: (b, 0, h, 0, 0))
    return pl.pallas_call(
        functools.partial(_diff_attn_kernel, n_pieces=2, lam_init=lam_init, emit_kv=False),
        out_shape=jax.ShapeDtypeStruct((n_latent_groups, s, DIFF_WIDTH), BF16),
        grid=(n_latent_groups, DIFF_HEADS, s // tq),
        in_specs=[
            pl.BlockSpec((4, DIFF_DH), lambda b, h, qi: (0, 0)),
            pl.BlockSpec((1, LANES), lambda b, h, qi: (0, 0)),
            pl.BlockSpec((None, tq, LANES), lambda b, h, qi: (b, qi, h)),
            cache_spec,
            cache_spec,
            pl.BlockSpec((None, s, LANES), lambda b, h, qi: (b, 0, DIFF_HEADS + h)),
            pl.BlockSpec((None, s, LANES), lambda b, h, qi: (b, 0, 2 * DIFF_HEADS + h)),
        ],
        out_specs=pl.BlockSpec((None, tq, LANES), lambda b, h, qi: (b, qi, h)),
        compiler_params=_cparams(("arbitrary", "arbitrary", "arbitrary")),
        name="attn_latent",
    )(lam_p, norm_g, proj, cache_k, cache_v, proj, proj)


def _pool_kernel(p_ref, w_ref, sc_ref, o_ref, band_ref, *, seq):
    @pl.when((pl.program_id(0) == 0) & (pl.program_id(1) == 0))
    def _():
        t = lax.broadcasted_iota(jnp.int32, (seq, seq), 0)
        s_ = lax.broadcasted_iota(jnp.int32, (seq, seq), 1)
        for gi, w in enumerate(POOL_WINDOWS):
            inside = (s_ >= t - w // 2) & (s_ <= t + w // 2 - 1)
            band_ref[gi] = jnp.where(inside, 1.0, 0.0).astype(BF16)

    tcol = lax.broadcasted_iota(jnp.int32, (seq, 1), 0)
    for gi, w in enumerate(POOL_WINDOWS):
        u = p_ref[:, gi * POOL_GC:(gi + 1) * POOL_GC]
        hi, lo = _split_bf16(u)
        band = band_ref[gi]
        win = (jnp.dot(band, hi, preferred_element_type=F32)
               + jnp.dot(band, lo, preferred_element_type=F32))
        cnt = (jnp.minimum(tcol + (w // 2 - 1), seq - 1) - jnp.maximum(tcol - w // 2, 0) + 1).astype(F32)
        pooled = win / cnt - u
        mixed = jnp.dot(pooled.astype(BF16), w_ref[gi].astype(BF16), preferred_element_type=F32)
        o_ref[:, gi * POOL_GC:(gi + 1) * POOL_GC] = (
            mixed * sc_ref[:, gi * POOL_GC:(gi + 1) * POOL_GC]).astype(o_ref.dtype)


def _pool(proj, pool_w, pool_scale, g0, n_groups, seq):
    g, s, _ = proj.shape
    col = 3 * DIFF_WIDTH // POOL_WIDTH
    return pl.pallas_call(
        functools.partial(_pool_kernel, seq=seq),
        out_shape=jax.ShapeDtypeStruct((n_groups, s, POOL_WIDTH), BF16),
        grid=(n_groups, s // seq),
        in_specs=[
            pl.BlockSpec((None, seq, POOL_WIDTH), lambda gi, ti: (g0 + gi, ti, col)),
            pl.BlockSpec((POOL_GROUPS, POOL_GC, POOL_GC), lambda gi, ti: (0, 0, 0)),
            pl.BlockSpec((1, POOL_WIDTH), lambda gi, ti: (0, 0)),
        ],
        out_specs=pl.BlockSpec((None, seq, POOL_WIDTH), lambda gi, ti: (gi, ti, 0)),
        scratch_shapes=[pltpu.VMEM((POOL_GROUPS, seq, seq), BF16)],
        compiler_params=_cparams(("arbitrary", "arbitrary")),
        name="pool_%d" % seq,
    )(proj, pool_w, pool_scale.reshape(1, POOL_WIDTH))


def _outproj_kernel(*refs, n_in, gate_row, n_latent_groups):
    a_refs = refs[:2 * n_in]
    w_refs = refs[2 * n_in:3 * n_in]
    x_ref, mod_ref, g_ref, b_ref, o_ref = refs[3 * n_in:]
    is_latent = pl.program_id(0) < n_latent_groups
    acc = None
    for i, w_ref in enumerate(w_refs):
        a = jnp.where(is_latent, a_refs[2 * i][...], a_refs[2 * i + 1][...])
        t = jnp.dot(a, w_ref[...], preferred_element_type=F32)
        acc = t if acc is None else acc + t
    z = ALPHA * x_ref[...] + mod_ref[gate_row:gate_row + 1, :] * acc
    o_ref[...] = _layer_norm_rows(z, g_ref[...], b_ref[...])


def _outproj(acts, weights, x, mod, ln_g, ln_b, gate_row, tm=512):
    g, s, d = x.shape
    n_in = len(acts)
    gl = acts[0][0].shape[0]
    in_specs = []
    flat_acts = []
    for a_lat, a_ctx in acts:
        k = a_lat.shape[-1]
        in_specs.append(pl.BlockSpec((None, tm, k), lambda gi, ti: (jnp.minimum(gi, gl - 1), jnp.where(gi < gl, ti, 0), 0)))
        in_specs.append(pl.BlockSpec((None, tm, k), lambda gi, ti: (jnp.maximum(gi - gl, 0), jnp.where(gi < gl, 0, ti), 0)))
        flat_acts += [a_lat, a_ctx]
    in_specs += [_resident(w.shape, lambda gi, ti: (0, 0)) for w in weights]
    in_specs += [
        pl.BlockSpec((None, tm, d), lambda gi, ti: (gi, ti, 0)),
        pl.BlockSpec((None, 6, d), lambda gi, ti: (gi, 0, 0)),
        pl.BlockSpec((1, d), lambda gi, ti: (0, 0)),
        pl.BlockSpec((1, d), lambda gi, ti: (0, 0)),
    ]
    return pl.pallas_call(
        functools.partial(_outproj_kernel, n_in=n_in, gate_row=gate_row, n_latent_groups=gl),
        out_shape=jax.ShapeDtypeStruct((g, s, d), F32),
        grid=(g, s // tm),
        in_specs=in_specs,
        out_specs=pl.BlockSpec((None, tm, d), lambda gi, ti: (gi, ti, 0)),
        compiler_params=_cparams(("arbitrary", "arbitrary")),
        name="outproj",
    )(*flat_acts, *weights, x, mod, ln_g.reshape(1, d), ln_b.reshape(1, d))


def _swiglu_chunks(h_bf, wgu_ref, wd_ref, d_ff):
    acc = None
    for j in range(d_ff // FF_CHUNK):
        lo = j * FF_CHUNK
        gate = jnp.dot(h_bf, wgu_ref[:, lo:lo + FF_CHUNK], preferred_element_type=F32)
        up = jnp.dot(h_bf, wgu_ref[:, d_ff + lo:d_ff + lo + FF_CHUNK], preferred_element_type=F32)
        act = (gate * _sigmoid(gate) * up).astype(BF16)
        t = jnp.dot(act, wd_ref[lo:lo + FF_CHUNK, :], preferred_element_type=F32)
        acc = t if acc is None else acc + t
    return acc


def _ffn_kernel(x_ref, mod_ref, wgu_ref, wd_ref, g_ref, b_ref, o_ref):
    x = x_ref[...]
    h = (x * (1.0 + mod_ref[4:5, :]) + mod_ref[3:4, :]).astype(BF16)
    acc = _swiglu_chunks(h, wgu_ref, wd_ref, D_FF)
    z = ALPHA * x + mod_ref[5:6, :] * acc
    o_ref[...] = _layer_norm_rows(z, g_ref[...], b_ref[...])


def _ffn(x, mod, wgu_bf, wd_bf, ln_g, ln_b, tm=256):
    g, s, d = x.shape
    return pl.pallas_call(
        _ffn_kernel,
        out_shape=jax.ShapeDtypeStruct((g, s, d), F32),
        grid=(g, s // tm),
        in_specs=[
            pl.BlockSpec((None, tm, d), lambda gi, ti: (gi, ti, 0)),
            pl.BlockSpec((None, 6, d), lambda gi, ti: (gi, 0, 0)),
            _resident(wgu_bf.shape, lambda gi, ti: (0, 0)),
            _resident(wd_bf.shape, lambda gi, ti: (0, 0)),
            pl.BlockSpec((1, d), lambda gi, ti: (0, 0)),
            pl.BlockSpec((1, d), lambda gi, ti: (0, 0)),
        ],
        out_specs=pl.BlockSpec((None, tm, d), lambda gi, ti: (gi, ti, 0)),
        compiler_params=_cparams(("arbitrary", "arbitrary")),
        name="ffn",
    )(x, mod, wgu_bf, wd_bf, ln_g.reshape(1, d), ln_b.reshape(1, d))


def _router_combine(h, rw_ref, rb_ref):
    h_hi, h_lo = _split_bf16(h)
    w_hi, w_lo = _split_bf16(rw_ref[...])
    logits = (jnp.dot(h_hi, w_hi, preferred_element_type=F32)
              + jnp.dot(h_lo, w_hi, preferred_element_type=F32)
              + jnp.dot(h_hi, w_lo, preferred_element_type=F32)) + rb_ref[...]
    lane = lax.broadcasted_iota(jnp.int32, logits.shape, 1).astype(F32)
    neg = -jnp.inf
    logits = jnp.where(lane < N_EXPERTS, logits, neg)
    m1 = jnp.max(logits, axis=-1, keepdims=True)
    i1 = jnp.min(jnp.where(logits == m1, lane, float(LANES)), axis=-1, keepdims=True)
    rest = jnp.where(lane == i1, neg, logits)
    m2 = jnp.max(rest, axis=-1, keepdims=True)
    i2 = jnp.min(jnp.where(rest == m2, lane, float(LANES)), axis=-1, keepdims=True)
    e2 = jnp.exp(m2 - m1)
    w1 = 1.0 / (1.0 + e2)
    w2 = e2 / (1.0 + e2)
    member = (lane == i1) | (lane == i2)
    return jnp.where(lane == i1, w1, 0.0) + jnp.where(lane == i2, w2, 0.0), member


MOE_BM = 256
MOE_TC = 256
MOE_TMC = 512


def _route_kernel(x_ref, mod_ref, rw_ref, rb_ref, h_ref, cw_ref, srank_ref, cnt_ref, tri_ref, run_ref):
    tm = x_ref.shape[0]

    @pl.when(pl.program_id(0) == 0)
    def _():
        r = lax.broadcasted_iota(jnp.int32, (tm, tm), 0)
        c = lax.broadcasted_iota(jnp.int32, (tm, tm), 1)
        tri_ref[...] = jnp.where(c <= r, 1.0, 0.0).astype(BF16)
        run_ref[...] = jnp.zeros_like(run_ref)

    h = x_ref[...] * (1.0 + mod_ref[4:5, :]) + mod_ref[3:4, :]
    h_ref[...] = h.astype(BF16)
    cw, member = _router_combine(h, rw_ref, rb_ref)
    cw_ref[...] = cw
    mem = jnp.where(member, 1.0, 0.0)
    rank = jnp.dot(tri_ref[...], mem.astype(BF16), preferred_element_type=F32) + run_ref[...]
    srank_ref[...] = jnp.where(member, rank, -rank)
    run_ref[...] = rank[tm - 1:tm, :]
    cnt_ref[...] = rank[tm - 1:tm, :]


def _route(x, mod, router_w, router_b, tm=512):
    g, s, d = x.shape
    n = g * s
    per = s // tm
    rw = jnp.pad(router_w, ((0, 0), (0, LANES - N_EXPERTS)))
    rb = jnp.pad(router_b, (0, LANES - N_EXPERTS)).reshape(1, LANES)
    return pl.pallas_call(
        _route_kernel,
        out_shape=(jax.ShapeDtypeStruct((n, d), BF16), jax.ShapeDtypeStruct((n, LANES), F32),
                   jax.ShapeDtypeStruct((n, LANES), F32), jax.ShapeDtypeStruct((1, LANES), F32)),
        grid=(n // tm,),
        in_specs=[
            pl.BlockSpec((None, tm, d), lambda i: (i // per, i % per, 0)),
            pl.BlockSpec((None, 6, d), lambda i: (i // per, 0, 0)),
            pl.BlockSpec((d, LANES), lambda i: (0, 0)),
            pl.BlockSpec((1, LANES), lambda i: (0, 0)),
        ],
        out_specs=(pl.BlockSpec((tm, d), lambda i: (i, 0)), pl.BlockSpec((tm, LANES), lambda i: (i, 0)),
                   pl.BlockSpec((tm, LANES), lambda i: (i, 0)), pl.BlockSpec((1, LANES), lambda i: (0, 0))),
        scratch_shapes=[pltpu.VMEM((tm, tm), BF16), pltpu.VMEM((1, LANES), F32)],
        compiler_params=_cparams(("arbitrary",)),
        name="moe_route",
    )(x, mod, rw, rb)


def _moe_plan(srank, counts, n_blocks, n_items):
    e_n = N_EXPERTS
    n = srank.shape[0]
    i32 = jnp.int32
    cnt = counts[0, :e_n].astype(i32)
    nb = (cnt + MOE_BM - 1) // MOE_BM
    nb_incl = jnp.cumsum(nb)
    gstart = nb_incl - nb
    n_used = nb_incl[-1]
    sr = srank[:, :e_n]
    rank = jnp.abs(sr).astype(i32)
    pos_tok = jnp.where(sr > 0, rank - 1 + MOE_BM * gstart[None, :], -1)
    rank_t = rank.T

    r = jnp.arange(n_blocks, dtype=i32)
    used = r < n_used
    rc = jnp.minimum(r, n_used - 1)
    e_r = jnp.minimum(jnp.sum(nb_incl[None, :] <= rc[:, None], axis=1, dtype=i32), e_n - 1)
    b = rc - gstart[e_r]
    lo = b * MOE_BM + 1
    hi = jnp.minimum((b + 1) * MOE_BM, cnt[e_r])
    def find(e, v):
        return jnp.sum(rank_t[e] < v[:, None], axis=1, dtype=i32)

    jlo = jnp.where(used, find(e_r, lo) // MOE_TC, 0)
    jhi = jnp.where(used, find(e_r, hi) // MOE_TC, -1)

    n_tiles = n // MOE_TMC
    ends = rank[MOE_TMC - 1::MOE_TMC]
    starts = jnp.concatenate([jnp.zeros((1, e_n), i32), ends[:-1]], axis=0)
    fb = gstart[None, :] + starts // MOE_BM
    lb = gstart[None, :] + (ends - 1) // MOE_BM
    n_pe = jnp.where(ends > starts, lb - fb + 1, 0).reshape(-1)
    incl = jnp.cumsum(n_pe)
    off = incl - n_pe
    total = incl[-1]
    w = jnp.arange(n_items, dtype=i32)
    valid = w < total
    wc = jnp.minimum(w, total - 1)
    p = jnp.sum(incl[None, :] <= wc[:, None], axis=1, dtype=i32)
    it_tile = p // e_n
    it_e = p % e_n
    it_blk = fb.reshape(-1)[p] + (wc - off[p])
    tile_off = jnp.concatenate([off[::e_n], total[None]])
    it_first = (wc == tile_off[it_tile]).astype(i32)
    it_last = (wc == tile_off[it_tile + 1] - 1).astype(i32)
    return (pos_tok, e_r, jlo, jhi, n_used.reshape(1),
            it_tile, it_blk, it_e, it_first, it_last, valid.astype(i32))


def _experts_kernel(be_ref, jlo_ref, jhi_ref, nused_ref, h_ref, pos_ref, wgu_ref, wd_ref, y_ref, xg_ref):
    r = pl.program_id(0)

    @pl.when(r < nused_ref[0])
    def _():
        e = be_ref[r]
        slot = r * MOE_BM + lax.broadcasted_iota(jnp.int32, (MOE_BM, MOE_TC), 0)
        xg_ref[...] = jnp.zeros_like(xg_ref)

        def chunk(j, carry):
            onehot = jnp.where(pos_ref[e, pl.ds(j, 1), :] == slot, 1.0, 0.0).astype(BF16)
            rows = h_ref[pl.ds(pl.multiple_of(j * MOE_TC, MOE_TC), MOE_TC), :]
            xg_ref[...] += jnp.dot(onehot, rows, preferred_element_type=F32)
            return carry

        lax.fori_loop(jlo_ref[r], jhi_ref[r] + 1, chunk, 0)
        y_ref[...] = _swiglu_chunks(xg_ref[...].astype(BF16), wgu_ref, wd_ref, D_FF_EXPERT)

    @pl.when(r >= nused_ref[0])
    def _():
        y_ref[...] = jnp.zeros_like(y_ref)


def _experts(h, pos_row, plan, wgu_bf, wd_bf, n_blocks):
    n, d = h.shape
    e_r, jlo, jhi, n_used = plan
    grid_spec = pltpu.PrefetchScalarGridSpec(
        num_scalar_prefetch=4,
        grid=(n_blocks,),
        in_specs=[
            _resident((n, d), lambda r, be, lo, hi, nu: (0, 0)),
            _resident(pos_row.shape, lambda r, be, lo, hi, nu: (0, 0, 0)),
            pl.BlockSpec((None, d, 2 * D_FF_EXPERT), lambda r, be, lo, hi, nu: (be[r], 0, 0),
                         pipeline_mode=pl.Buffered(1)),
            pl.BlockSpec((None, D_FF_EXPERT, d), lambda r, be, lo, hi, nu: (be[r], 0, 0)),
        ],
        out_specs=pl.BlockSpec((MOE_BM, d), lambda r, be, lo, hi, nu: (r, 0)),
        scratch_shapes=[pltpu.VMEM((MOE_BM, d), F32)],
    )
    return pl.pallas_call(
        _experts_kernel,
        out_shape=jax.ShapeDtypeStruct((n_blocks * MOE_BM, d), F32),
        grid_spec=grid_spec,
        compiler_params=_cparams(("arbitrary",)),
        name="moe_experts",
    )(e_r, jlo, jhi, n_used, h, pos_row, wgu_bf, wd_bf)


def _combine_kernel(tile_ref, blk_ref, e_ref, first_ref, last_ref, valid_ref,
                    y_ref, pos_ref, cw_ref, x_ref, mod_ref, g_ref, b_ref, o_ref, acc_ref):
    w = pl.program_id(0)

    @pl.when(valid_ref[w] == 1)
    def _():
        @pl.when(first_ref[w] == 1)
        def _():
            acc_ref[...] = jnp.zeros_like(acc_ref)

        e = e_ref[w]
        lane = lax.broadcasted_iota(jnp.int32, (MOE_TMC, LANES), 1)
        pos_e = jnp.sum(jnp.where(lane == e, pos_ref[...], 0.0), axis=-1, keepdims=True)
        cw_e = jnp.sum(jnp.where(lane == e, cw_ref[...], 0.0), axis=-1, keepdims=True)
        slot = (blk_ref[w] * MOE_BM + lax.broadcasted_iota(jnp.int32, (MOE_TMC, MOE_BM), 1)).astype(F32)
        onehot = jnp.where(pos_e == slot, 1.0, 0.0).astype(BF16)
        y_hi, y_lo = _split_bf16(y_ref[...])
        part = (jnp.dot(onehot, y_hi, preferred_element_type=F32)
                + jnp.dot(onehot, y_lo, preferred_element_type=F32))
        acc_ref[...] += cw_e * part

        @pl.when(last_ref[w] == 1)
        def _():
            z = ALPHA * x_ref[...] + mod_ref[5:6, :] * acc_ref[...]
            o_ref[...] = _layer_norm_rows(z, g_ref[...], b_ref[...])


def _combine(y, pos_tok_f, cw, x, mod, ln_g, ln_b, items, n_items):
    g, s, d = x.shape
    per = s // MOE_TMC

    def tok2(w, tile, *_):
        return (tile[w], 0)

    def tok3(w, tile, *_):
        return (tile[w] // per, tile[w] % per, 0)

    grid_spec = pltpu.PrefetchScalarGridSpec(
        num_scalar_prefetch=6,
        grid=(n_items,),
        in_specs=[
            pl.BlockSpec((MOE_BM, d), lambda w, tile, blk, *_: (blk[w], 0)),
            pl.BlockSpec((MOE_TMC, LANES), tok2),
            pl.BlockSpec((MOE_TMC, LANES), tok2),
            pl.BlockSpec((None, MOE_TMC, d), tok3),
            pl.BlockSpec((None, 6, d), lambda w, tile, *_: (tile[w] // per, 0, 0)),
            pl.BlockSpec((1, d), lambda w, *_: (0, 0)),
            pl.BlockSpec((1, d), lambda w, *_: (0, 0)),
        ],
        out_specs=pl.BlockSpec((None, MOE_TMC, d), tok3),
        scratch_shapes=[pltpu.VMEM((MOE_TMC, d), F32)],
    )
    return pl.pallas_call(
        _combine_kernel,
        out_shape=jax.ShapeDtypeStruct((g, s, d), F32),
        grid_spec=grid_spec,
        compiler_params=_cparams(("arbitrary",)),
        name="moe_combine",
    )(*items, y, pos_tok_f, cw, x, mod, ln_g.reshape(1, d), ln_b.reshape(1, d))


def _moe(x, mod, router_w, router_b, wgu_bf, wd_bf, ln_g, ln_b):
    g, s, d = x.shape
    n = g * s
    n_slots = 2 * n
    n_blocks = n_slots // MOE_BM + N_EXPERTS
    n_items = n_blocks + N_EXPERTS * (n // MOE_TMC)
    h, cw, srank, counts = _route(x, mod, router_w, router_b)
    plan = _moe_plan(srank, counts, n_blocks, n_items)
    pos_tok = plan[0]
    pos_row = pos_tok.T.reshape(N_EXPERTS, n // MOE_TC, MOE_TC)
    y = _experts(h, pos_row, plan[1:5], wgu_bf, wd_bf, n_blocks)
    pos_tok_f = jnp.pad(pos_tok.astype(F32), ((0, 0), (0, LANES - N_EXPERTS)), constant_values=-1.0)
    return _combine(y, pos_tok_f, cw, x, mod, ln_g, ln_b, plan[5:], n_items)


META_LANES = LANES


def _router_top2(h, rw_ref, rb_ref):
    h_hi, h_lo = _split_bf16(h)
    w_hi, w_lo = _split_bf16(rw_ref[...])
    logits = (jnp.dot(h_hi, w_hi, preferred_element_type=F32)
              + jnp.dot(h_lo, w_hi, preferred_element_type=F32)
              + jnp.dot(h_hi, w_lo, preferred_element_type=F32)) + rb_ref[...]
    lane = lax.broadcasted_iota(jnp.int32, logits.shape, 1).astype(F32)
    neg = -jnp.inf
    logits = jnp.where(lane < N_EXPERTS, logits, neg)
    m1 = jnp.max(logits, axis=-1, keepdims=True)
    i1 = jnp.min(jnp.where(logits == m1, lane, float(LANES)), axis=-1, keepdims=True)
    rest = jnp.where(lane == i1, neg, logits)
    m2 = jnp.max(rest, axis=-1, keepdims=True)
    i2 = jnp.min(jnp.where(rest == m2, lane, float(LANES)), axis=-1, keepdims=True)
    e2 = jnp.exp(m2 - m1)
    return lane, i1, i2, 1.0 / (1.0 + e2), e2 / (1.0 + e2)


def _route2_kernel(x_ref, mod_ref, rw_ref, rb_ref, h_ref, wk_ref, srank_ref, cnt_ref, tri_ref, run_ref):
    tm, d = x_ref.shape

    @pl.when(pl.program_id(0) == 0)
    def _():
        r = lax.broadcasted_iota(jnp.int32, (tm, tm), 0)
        c = lax.broadcasted_iota(jnp.int32, (tm, tm), 1)
        tri_ref[...] = jnp.where(c <= r, 1.0, 0.0).astype(BF16)
        run_ref[...] = jnp.zeros_like(run_ref)

    h = x_ref[...] * (1.0 + mod_ref[4:5, :]) + mod_ref[3:4, :]
    h_ref[:, 0:d] = h.astype(BF16)
    lane, i1, i2, w1, w2 = _router_top2(h, rw_ref, rb_ref)
    first_is_low = i1 < i2
    e_hi = jnp.where(first_is_low, i2, i1)
    wk_ref[...] = jnp.where(lane == 0.0, jnp.where(first_is_low, w1, w2),
                            jnp.where(lane == 1.0, jnp.where(first_is_low, w2, w1), 0.0))
    tok = (pl.program_id(0) * tm + lax.broadcasted_iota(jnp.int32, (tm, META_LANES), 0))
    meta = jnp.where(lane == 0.0, (tok // 128).astype(F32),
                     jnp.where(lane == 1.0, (tok % 128).astype(F32),
                               jnp.where(lane == 2.0, e_hi, jnp.where(lane == 3.0, 1.0, 0.0))))
    h_ref[:, d:d + META_LANES] = meta.astype(BF16)

    member = (lane == i1) | (lane == i2)
    mem = jnp.where(member, 1.0, 0.0)
    rank = jnp.dot(tri_ref[...], mem.astype(BF16), preferred_element_type=F32) + run_ref[...]
    srank_ref[...] = jnp.where(member, rank, -rank)
    run_ref[...] = rank[tm - 1:tm, :]
    cnt_ref[...] = rank[tm - 1:tm, :]


def _route2(x, mod, router_w, router_b, tm=512):
    g, s, d = x.shape
    n = g * s
    assert n <= 128 * 256
    per = s // tm
    rw = jnp.pad(router_w, ((0, 0), (0, LANES - N_EXPERTS)))
    rb = jnp.pad(router_b, (0, LANES - N_EXPERTS)).reshape(1, LANES)
    return pl.pallas_call(
        _route2_kernel,
        out_shape=(jax.ShapeDtypeStruct((n, d + META_LANES), BF16), jax.ShapeDtypeStruct((n, LANES), F32),
                   jax.ShapeDtypeStruct((n, LANES), F32), jax.ShapeDtypeStruct((1, LANES), F32)),
        grid=(n // tm,),
        in_specs=[
            pl.BlockSpec((None, tm, d), lambda i: (i // per, i % per, 0)),
            pl.BlockSpec((None, 6, d), lambda i: (i // per, 0, 0)),
            pl.BlockSpec((d, LANES), lambda i: (0, 0)),
            pl.BlockSpec((1, LANES), lambda i: (0, 0)),
        ],
        out_specs=(pl.BlockSpec((tm, d + META_LANES), lambda i: (i, 0)), pl.BlockSpec((tm, LANES), lambda i: (i, 0)),
                   pl.BlockSpec((tm, LANES), lambda i: (i, 0)), pl.BlockSpec((1, LANES), lambda i: (0, 0))),
        scratch_shapes=[pltpu.VMEM((tm, tm), BF16), pltpu.VMEM((1, LANES), F32)],
        compiler_params=_cparams(("arbitrary",)),
        name="moe_route",
    )(x, mod, rw, rb)


def _moe_plan2(srank, counts, n_blocks):
    e_n = N_EXPERTS
    i32 = jnp.int32
    cnt = counts[0, :e_n].astype(i32)
    nb = (cnt + MOE_BM - 1) // MOE_BM
    nb_incl = jnp.cumsum(nb)
    gstart = nb_incl - nb
    n_used = nb_incl[-1]
    sr = srank[:, :e_n]
    rank = jnp.abs(sr).astype(i32)
    pos_tok = jnp.where(sr > 0, rank - 1 + MOE_BM * gstart[None, :], -1)
    chunk_end = rank[MOE_TC - 1::MOE_TC].T

    r = jnp.arange(n_blocks, dtype=i32)
    used = r < n_used
    rc = jnp.minimum(r, n_used - 1)
    e_r = jnp.minimum(jnp.sum(nb_incl[None, :] <= rc[:, None], axis=1, dtype=i32), e_n - 1)
    b = rc - gstart[e_r]
    lo = b * MOE_BM + 1
    hi = jnp.minimum((b + 1) * MOE_BM, cnt[e_r])
    ends_r = chunk_end[e_r]
    jlo = jnp.where(used, jnp.sum(ends_r < lo[:, None], axis=1, dtype=i32), 0)
    jhi = jnp.where(used, jnp.sum(ends_r < hi[:, None], axis=1, dtype=i32), -1)
    return pos_tok, e_r, jlo, jhi, n_used.reshape(1)


def _experts2_kernel(be_ref, jlo_ref, jhi_ref, nused_ref, h_ref, pos_ref, wgu_ref, wd_ref, y_ref, meta_ref, xg_ref):
    r = pl.program_id(0)
    d = y_ref.shape[1]

    @pl.when(r < nused_ref[0])
    def _():
        e = be_ref[r]
        slot = r * MOE_BM + lax.broadcasted_iota(jnp.int32, (MOE_BM, MOE_TC), 0)
        xg_ref[...] = jnp.zeros_like(xg_ref)

        def chunk(j, carry):
            onehot = jnp.where(pos_ref[e, pl.ds(j, 1), :] == slot, 1.0, 0.0).astype(BF16)
            rows = h_ref[pl.ds(pl.multiple_of(j * MOE_TC, MOE_TC), MOE_TC), :]
            xg_ref[...] += jnp.dot(onehot, rows, preferred_element_type=F32)
            return carry

        lax.fori_loop(jlo_ref[r], jhi_ref[r] + 1, chunk, 0)
        meta_ref[...] = xg_ref[:, d:d + META_LANES]
        y_ref[...] = _swiglu_chunks(xg_ref[:, 0:d].astype(BF16), wgu_ref, wd_ref, D_FF_EXPERT)

    @pl.when(r >= nused_ref[0])
    def _():
        y_ref[...] = jnp.zeros_like(y_ref)
        meta_ref[...] = jnp.zeros_like(meta_ref)


def _experts2(h, pos_row, plan, wgu_bf, wd_bf, n_blocks):
    n, dx = h.shape
    d = dx - META_LANES
    e_r, jlo, jhi, n_used = plan
    grid_spec = pltpu.PrefetchScalarGridSpec(
        num_scalar_prefetch=4,
        grid=(n_blocks,),
        in_specs=[
            _resident((n, dx), lambda r, be, lo, hi, nu: (0, 0)),
            _resident(pos_row.shape, lambda r, be, lo, hi, nu: (0, 0, 0)),
            pl.BlockSpec((None, d, 2 * D_FF_EXPERT), lambda r, be, lo, hi, nu: (be[r], 0, 0),
                         pipeline_mode=pl.Buffered(1)),
            pl.BlockSpec((None, D_FF_EXPERT, d), lambda r, be, lo, hi, nu: (be[r], 0, 0),
                         pipeline_mode=pl.Buffered(1)),
        ],
        out_specs=(pl.BlockSpec((MOE_BM, d), lambda r, be, lo, hi, nu: (r, 0)),
                   pl.BlockSpec((MOE_BM, META_LANES), lambda r, be, lo, hi, nu: (r, 0))),
        scratch_shapes=[pltpu.VMEM((MOE_BM, dx), F32)],
    )
    return pl.pallas_call(
        _experts2_kernel,
        out_shape=(jax.ShapeDtypeStruct((n_blocks * MOE_BM, d), F32),
                   jax.ShapeDtypeStruct((n_blocks * MOE_BM, META_LANES), F32)),
        grid_spec=grid_spec,
        compiler_params=_cparams(("arbitrary",)),
        name="moe_experts",
    )(e_r, jlo, jhi, n_used, h, pos_row, wgu_bf, wd_bf)


def _scatter_kernel(dest_ref, y_ref, o_ref, ybuf_ref, sem):
    r = pl.program_id(0)
    n_steps = pl.num_programs(0)
    buf = r % 2

    def wait_block(b):
        pltpu.make_async_copy(ybuf_ref.at[b], o_ref.at[pl.ds(0, MOE_BM), :], sem.at[b]).wait()

    @pl.when(r >= 2)
    def _():
        wait_block(buf)

    ybuf_ref[buf] = y_ref[...]

    def issue(i, carry):
        pltpu.make_async_copy(ybuf_ref.at[buf, pl.ds(i, 1), :], o_ref.at[pl.ds(dest_ref[0, i], 1), :],
                              sem.at[buf]).start()
        return carry

    lax.fori_loop(0, MOE_BM, issue, 0, unroll=8)

    @pl.when(r == n_steps - 1)
    def _():
        wait_block(1 - buf)
        wait_block(buf)


def _scatter_rows(y, dest, n_rows_out):
    n_slots, d = y.shape
    n_blocks = n_slots // MOE_BM
    assert n_blocks >= 2
    return pl.pallas_call(
        _scatter_kernel,
        out_shape=jax.ShapeDtypeStruct((n_rows_out, d), F32),
        grid=(n_blocks,),
        in_specs=[
            pl.BlockSpec((None, 1, MOE_BM), lambda r: (r, 0, 0), memory_space=pltpu.SMEM),
            pl.BlockSpec((MOE_BM, d), lambda r: (r, 0)),
        ],
        out_specs=pl.BlockSpec(memory_space=pl.ANY),
        scratch_shapes=[pltpu.VMEM((2, MOE_BM, d), F32), pltpu.SemaphoreType.DMA((2,))],
        compiler_params=_cparams(("arbitrary",)),
        name="moe_scatter",
    )(dest.reshape(n_blocks, 1, MOE_BM), y)


def _finish_kernel(a_ref, b_ref, wk_ref, x_ref, mod_ref, g_ref, bb_ref, ol_ref, oc_ref, *, n_latent_tiles):
    lane = lax.broadcasted_iota(jnp.int32, wk_ref.shape, 1)
    wk = wk_ref[...]
    w0 = jnp.sum(jnp.where(lane == 0, wk, 0.0), axis=-1, keepdims=True)
    w1 = jnp.sum(jnp.where(lane == 1, wk, 0.0), axis=-1, keepdims=True)
    acc = w0 * a_ref[...] + w1 * b_ref[...]
    z = ALPHA * x_ref[...] + mod_ref[5:6, :] * acc
    out = _layer_norm_rows(z, g_ref[...], bb_ref[...])
    is_latent = pl.program_id(0) < n_latent_tiles

    @pl.when(is_latent)
    def _():
        ol_ref[...] = out

    @pl.when(jnp.logical_not(is_latent))
    def _():
        oc_ref[...] = out


def _finish(rows, wk, x, mod, ln_g, ln_b, n_latent_groups, tm=512):
    g, s, d = x.shape
    n = g * s
    per = s // tm
    nt = n // tm
    nlt = n_latent_groups * per

    def lat_map(i):
        j = jnp.minimum(i, nlt - 1)
        return (j // per, j % per, 0)

    def ctx_map(i):
        j = jnp.maximum(i - nlt, 0)
        return (j // per, j % per, 0)

    return pl.pallas_call(
        functools.partial(_finish_kernel, n_latent_tiles=nlt),
        out_shape=(jax.ShapeDtypeStruct((n_latent_groups, s, d), F32),
                   jax.ShapeDtypeStruct((g - n_latent_groups, s, d), F32)),
        grid=(nt,),
        in_specs=[
            pl.BlockSpec((tm, d), lambda i: (i, 0)),
            pl.BlockSpec((tm, d), lambda i: (nt + i, 0)),
            pl.BlockSpec((tm, LANES), lambda i: (i, 0)),
            pl.BlockSpec((None, tm, d), lambda i: (i // per, i % per, 0)),
            pl.BlockSpec((None, 6, d), lambda i: (i // per, 0, 0)),
            pl.BlockSpec((1, d), lambda i: (0, 0)),
            pl.BlockSpec((1, d), lambda i: (0, 0)),
        ],
        out_specs=(pl.BlockSpec((None, tm, d), lat_map), pl.BlockSpec((None, tm, d), ctx_map)),
        compiler_params=_cparams(("arbitrary",)),
        name="moe_finish",
    )(rows, rows, wk, x, mod, ln_g.reshape(1, d), ln_b.reshape(1, d))


def _moe2(x, mod, router_w, router_b, wgu_bf, wd_bf, ln_g, ln_b, n_latent_groups):
    g, s, d = x.shape
    n = g * s
    n_blocks = 2 * n // MOE_BM + N_EXPERTS
    n_slots = n_blocks * MOE_BM
    h, wk, srank, counts = _route2(x, mod, router_w, router_b)
    pos_tok, e_r, jlo, jhi, n_used = _moe_plan2(srank, counts, n_blocks)
    pos_row = pos_tok.T.reshape(N_EXPERTS, n // MOE_TC, MOE_TC)
    y, meta = _experts2(h, pos_row, (e_r, jlo, jhi, n_used), wgu_bf, wd_bf, n_blocks)
    tok = (meta[:, 0] * 128.0 + meta[:, 1]).astype(jnp.int32)
    choice = (meta[:, 2] == jnp.repeat(e_r, MOE_BM).astype(F32)).astype(jnp.int32)
    unused = meta[:, 3] < 0.5
    spare = 2 * n + jnp.cumsum(unused.astype(jnp.int32)) - 1
    dest = jnp.where(unused, spare, choice * n + tok)
    rows = _scatter_rows(y, dest, n_slots)
    return _finish(rows, wk, x, mod, ln_g, ln_b, n_latent_groups)


def _inproj_c_kernel(x_ref, mod_ref, w_ref, wg_ref, bg_ref, o_ref, og_ref):
    h = (x_ref[...] * (1.0 + mod_ref[1:2, :]) + mod_ref[0:1, :]).astype(BF16)
    o_ref[...] = jnp.dot(h, w_ref[...], preferred_element_type=F32)
    og_ref[...] = jnp.dot(h, wg_ref[...], preferred_element_type=F32) + bg_ref[...]


def _inproj_c(x, mod, w_bf, wg_bf, bg, tm=512):
    g, s, d = x.shape
    n = w_bf.shape[1]
    ng = wg_bf.shape[1]
    return pl.pallas_call(
        _inproj_c_kernel,
        out_shape=(jax.ShapeDtypeStruct((g, s, n), F32), jax.ShapeDtypeStruct((g, s, ng), F32)),
        grid=(g, s // tm),
        in_specs=[
            pl.BlockSpec((None, tm, d), lambda gi, ti: (gi, ti, 0)),
            pl.BlockSpec((None, 6, d), lambda gi, ti: (gi, 0, 0)),
            _resident((d, n), lambda gi, ti: (0, 0)),
            _resident((d, ng), lambda gi, ti: (0, 0)),
            pl.BlockSpec((1, ng), lambda gi, ti: (0, 0)),
        ],
        out_specs=(pl.BlockSpec((None, tm, n), lambda gi, ti: (gi, ti, 0)),
                   pl.BlockSpec((None, tm, ng), lambda gi, ti: (gi, ti, 0))),
        compiler_params=_cparams(("arbitrary", "arbitrary")),
        name="inproj_c",
    )(x, mod, w_bf, wg_bf, bg)


def _log_sigmoid(x):
    return jnp.minimum(x, 0.0) - jnp.log(1.0 + jnp.exp(-jnp.abs(x)))


MLSTM_L = 128


def _split3_bf16(x):
    hi = x.astype(BF16)
    r1 = x - hi.astype(F32)
    mid = r1.astype(BF16)
    lo = (r1 - mid.astype(F32)).astype(BF16)
    return hi, mid, lo


def _mlstm_kernel(*refs, seq, hg, has_init, emit_state):
    q_ref, k_ref, v_ref, o_ref, gi_ref, gf_ref, hgain_ref = refs[:7]
    pos = 7
    if has_init:
        c0_ref, n0_ref, m0_ref = refs[pos:pos + 3]
        pos += 3
    out_ref = refs[pos]
    pos += 1
    if emit_state:
        co_ref, no_ref, mo_ref = refs[pos:pos + 3]
        pos += 3
    cext_ref, hf_ref, hb_ref, b_ref, g_ref, gmax_ref, mt_ref, wi_ref, en_ref, ws_ref, gt_ref, wc_ref = refs[pos:]

    L = MLSTM_L
    dh = MLSTM_DH
    nh = MLSTM_HEADS
    nc = seq // L
    head0 = pl.program_id(1) * hg
    neg = -jnp.inf

    lane = lax.broadcasted_iota(jnp.int32, (L, LANES), 1)
    lane1 = lax.broadcasted_iota(jnp.int32, (1, LANES), 1)
    row = lax.broadcasted_iota(jnp.int32, (L, L), 0)
    col = lax.broadcasted_iota(jnp.int32, (L, L), 1)
    lower = col <= row
    upper = col >= row
    tri_l = jnp.where(lower, 1.0, 0.0).astype(BF16)
    tri_u = jnp.where(upper, 1.0, 0.0).astype(BF16)
    fwd_lane = lane < nh
    fwd_lane1 = lane1 < nh
    trow = lax.broadcasted_iota(jnp.int32, (L, LANES), 0)

    btot, glast = [], []
    for c in range(nc):
        rows = slice(c * L, (c + 1) * L)
        f = _log_sigmoid(gf_ref[rows, :])
        parts = _split3_bf16(f)
        pre = sum(jnp.dot(tri_l, p, preferred_element_type=F32) for p in parts)
        suf = sum(jnp.dot(tri_u, p, preferred_element_type=F32) for p in parts)
        b = jnp.where(fwd_lane, pre, suf)
        g = gi_ref[rows, :] - b
        gp, gs = g, g
        k = 1
        while k < L:
            gp = jnp.where(trow >= k, jnp.maximum(gp, pltpu.roll(gp, k, 0)), gp)
            gs = jnp.where(trow < L - k, jnp.maximum(gs, pltpu.roll(gs, L - k, 0)), gs)
            k *= 2
        gmax = jnp.where(fwd_lane, gp, gs)
        b_ref[rows, :] = b
        g_ref[rows, :] = g
        gmax_ref[rows, :] = gmax
        btot.append(jnp.where(fwd_lane1, b[L - 1:L, :], b[0:1, :]))
        glast.append(jnp.where(fwd_lane1, gmax[L - 1:L, :], gmax[0:1, :]))

    m_init = m0_ref[...] if has_init else jnp.zeros((1, LANES), F32)
    mf, mb = m_init, m_init
    ms_f, mn_f, ms_b, mn_b = [None] * nc, [None] * nc, [None] * nc, [None] * nc
    for c in range(nc):
        ms_f[c] = mf
        mf = btot[c] + jnp.maximum(mf, glast[c])
        mn_f[c] = mf
        cb = nc - 1 - c
        ms_b[cb] = mb
        mb = btot[cb] + jnp.maximum(mb, glast[cb])
        mn_b[cb] = mb
    m_final = jnp.where(fwd_lane1, mf, mb)

    for c in range(nc):
        rows = slice(c * L, (c + 1) * L)
        m_start = jnp.where(fwd_lane1, ms_f[c], ms_b[c])
        m_next = jnp.where(fwd_lane1, mn_f[c], mn_b[c])
        g = g_ref[rows, :]
        mt = jnp.maximum(m_start, gmax_ref[rows, :])
        mt_ref[rows, :] = mt
        wi_ref[rows, :] = jnp.exp(m_start - mt)
        en_ref[rows, :] = jnp.exp(-(b_ref[rows, :] + mt))
        ws_ref[rows, :] = jnp.exp(btot[c] + g - m_next)
        gt_ref[c] = g.T
        wc_ref[c:c + 1, :] = jnp.exp(btot[c] + m_start - m_next)

    lane_d = lax.broadcasted_iota(jnp.int32, (dh, dh), 1)
    for d in range(2):
        for hh in range(hg):
            idx = d * hg + hh
            if has_init:
                cext_ref[idx, :, 0:dh] = c0_ref[d, hh]
                cext_ref[idx, :, dh:2 * dh] = jnp.where(lane_d == 0, n0_ref[d, hh], 0.0)
            else:
                cext_ref[idx] = jnp.zeros((dh, 2 * dh), F32)

    ones_col = jnp.where(lane == 0, 1.0, 0.0).astype(BF16)
    nt = (((1,), (1,)), ((), ()))
    tn = (((0,), (0,)), ((), ()))

    def column(x, j):
        return jnp.sum(jnp.where(lane == j, x, 0.0), axis=-1, keepdims=True)

    def one_direction(d, hh, c, s_qk, q_bf, k_s, v_ext, v_bf):
        idx = d * hg + hh
        j = d * nh + head0 + hh
        rows = pl.ds(pl.multiple_of(c * L, L), L)
        mt = column(mt_ref[rows, :], j)
        wi = column(wi_ref[rows, :], j)
        en = column(en_ref[rows, :], j)
        ws = column(ws_ref[rows, :], j)
        g_r = gt_ref[c, pl.ds(j, 1), :]
        w_c = jnp.sum(jnp.where(lane1 == j, wc_ref[pl.ds(c, 1), :], 0.0), axis=-1, keepdims=True)
        causal = lower if d == 0 else upper
        p = s_qk * jnp.exp(jnp.where(causal, g_r - mt, neg))
        qc = jnp.dot(q_bf, cext_ref[idx].astype(BF16), preferred_element_type=F32)
        num = wi * qc[:, 0:dh] + jnp.dot(p.astype(BF16), v_bf, preferred_element_type=F32)
        den = wi * qc[:, dh:dh + 1] + jnp.sum(p, axis=-1, keepdims=True)
        h = num / jnp.maximum(jnp.abs(den), en)
        upd = lax.dot_general((ws * k_s).astype(BF16), v_ext, tn, preferred_element_type=F32)
        cext_ref[idx] = w_c * cext_ref[idx] + upd
        return h

    def load_chunk(hh, c):
        sl = (pl.ds(pl.multiple_of(c * L, L), L), slice(hh * dh, (hh + 1) * dh))
        q_bf = q_ref[sl].astype(BF16)
        k_s = k_ref[sl] * (dh ** -0.5)
        v_bf = v_ref[sl].astype(BF16)
        v_ext = jnp.concatenate([v_bf, ones_col], axis=-1)
        s_qk = lax.dot_general(q_bf, k_s.astype(BF16), nt, preferred_element_type=F32)
        return s_qk, q_bf, k_s, v_ext, v_bf

    def step(c, carry):
        cb = nc - 1 - c
        for hh in range(hg):
            h = one_direction(0, hh, c, *load_chunk(hh, c))
            hf_ref[pl.ds(pl.multiple_of(c * L, L), L), hh * dh:(hh + 1) * dh] = h
        for hh in range(hg):
            h = one_direction(1, hh, cb, *load_chunk(hh, cb))
            hb_ref[pl.ds(pl.multiple_of(cb * L, L), L), hh * dh:(hh + 1) * dh] = h
        return carry

    lax.fori_loop(0, nc, step, 0)

    for hh in range(hg):
        cs = slice(hh * dh, (hh + 1) * dh)
        hs = hf_ref[:, cs] + hb_ref[:, cs]
        mu = jnp.mean(hs, axis=-1, keepdims=True)
        hc = hs - mu
        var = jnp.mean(hc * hc, axis=-1, keepdims=True)
        hn = hc * lax.rsqrt(var + LN_EPS) * hgain_ref[:, cs]
        out_ref[:, cs] = (_sigmoid(o_ref[:, cs]) * hn).astype(out_ref.dtype)

    if emit_state:
        for d in range(2):
            for hh in range(hg):
                idx = d * hg + hh
                co_ref[d, hh] = cext_ref[idx, :, 0:dh]
                no_ref[d, hh] = cext_ref[idx, :, dh:dh + 1]
        mo_ref[...] = m_final


def _mlstm(proj, gates, head_g, g0, n_seq, seq, hg, init=None, emit_state=False):
    g, s, _ = proj.shape
    per_group = s // seq
    n_hg = MLSTM_HEADS // hg
    w = hg * MLSTM_DH
    nc = seq // MLSTM_L
    n_blocks = D_MODEL // w

    def tok_map(colblock):
        return lambda b, hi: (g0 + b // per_group, b % per_group, colblock * n_blocks + hi)

    def gate_map(half):
        return lambda b, hi: (g0 + b // per_group, b % per_group, half)

    args = [proj, proj, proj, proj, gates, gates, head_g.reshape(1, D_MODEL)]
    in_specs = [
        pl.BlockSpec((None, seq, w), tok_map(0)),
        pl.BlockSpec((None, seq, w), tok_map(1)),
        pl.BlockSpec((None, seq, w), tok_map(2)),
        pl.BlockSpec((None, seq, w), tok_map(3)),
        pl.BlockSpec((None, seq, LANES), gate_map(0)),
        pl.BlockSpec((None, seq, LANES), gate_map(1)),
        pl.BlockSpec((1, w), lambda b, hi: (0, hi)),
    ]
    if init is not None:
        c0, n0, m0 = init
        m0_lanes = jnp.pad(m0.reshape(n_seq, 1, 2 * MLSTM_HEADS), ((0, 0), (0, 0), (0, LANES - 2 * MLSTM_HEADS)))
        args += [c0, n0.reshape(n0.shape + (1,)), m0_lanes]
        in_specs += [
            pl.BlockSpec((None, 2, hg, MLSTM_DH, MLSTM_DH), lambda b, hi: (b, 0, hi, 0, 0)),
            pl.BlockSpec((None, 2, hg, MLSTM_DH, 1), lambda b, hi: (b, 0, hi, 0, 0)),
            pl.BlockSpec((None, 1, LANES), lambda b, hi: (b, 0, 0)),
        ]

    out_shape = [jax.ShapeDtypeStruct((n_seq // per_group, s, D_MODEL), BF16)]
    out_specs = [pl.BlockSpec((None, seq, w), lambda b, hi: (b // per_group, b % per_group, hi))]
    if emit_state:
        out_shape += [
            jax.ShapeDtypeStruct((n_seq, 2, MLSTM_HEADS, MLSTM_DH, MLSTM_DH), F32),
            jax.ShapeDtypeStruct((n_seq, 2, MLSTM_HEADS, MLSTM_DH, 1), F32),
            jax.ShapeDtypeStruct((n_seq, n_hg, 1, LANES), F32),
        ]
        out_specs += [
            pl.BlockSpec((None, 2, hg, MLSTM_DH, MLSTM_DH), lambda b, hi: (b, 0, hi, 0, 0)),
            pl.BlockSpec((None, 2, hg, MLSTM_DH, 1), lambda b, hi: (b, 0, hi, 0, 0)),
            pl.BlockSpec((None, None, 1, LANES), lambda b, hi: (b, hi, 0, 0)),
        ]

    tok_scratch = pltpu.VMEM((seq, LANES), F32)
    return pl.pallas_call(
        functools.partial(_mlstm_kernel, seq=seq, hg=hg, has_init=init is not None, emit_state=emit_state),
        out_shape=tuple(out_shape),
        grid=(n_seq, n_hg),
        in_specs=in_specs,
        out_specs=tuple(out_specs),
        scratch_shapes=[
            pltpu.VMEM((2 * hg, MLSTM_DH, 2 * MLSTM_DH), F32),
            pltpu.VMEM((seq, w), F32),
            pltpu.VMEM((seq, w), F32),
        ] + [tok_scratch] * 7 + [
            pltpu.VMEM((nc, LANES, MLSTM_L), F32),
            pltpu.VMEM((max(nc, 8), LANES), F32),
        ],
        compiler_params=_cparams(("arbitrary", "arbitrary")),
        name="mlstm_%d" % seq,
    )(*args)


def _mlstm_kernel_old(*refs, seq, hg, has_init, emit_state):
    q_ref, k_ref, v_ref, o_ref, gc_ref, gr_ref, hgain_ref = refs[:7]
    pos = 7
    if has_init:
        c0_ref, n0_ref, m0_ref = refs[pos:pos + 3]
        pos += 3
    out_ref = refs[pos]
    pos += 1
    if emit_state:
        co_ref, no_ref, mo_ref = refs[pos:pos + 3]
        pos += 3
    cext_ref, hf_ref, hb_ref = refs[pos:pos + 3]

    L = MLSTM_CHUNK
    dh = MLSTM_DH
    nc = seq // L
    head0 = pl.program_id(1) * hg
    neg = -jnp.inf

    lane_d = lax.broadcasted_iota(jnp.int32, (dh, dh), 1)
    for d in range(2):
        for hh in range(hg):
            idx = d * hg + hh
            if has_init:
                cext_ref[idx, :, 0:dh] = c0_ref[d, hh]
                cext_ref[idx, :, dh:2 * dh] = jnp.where(lane_d == 0, n0_ref[d, hh], 0.0)
            else:
                cext_ref[idx] = jnp.zeros((dh, 2 * dh), F32)

    row = lax.broadcasted_iota(jnp.int32, (L, L), 0)
    col = lax.broadcasted_iota(jnp.int32, (L, L), 1)
    lower = col <= row
    upper = col >= row
    lane_g = lax.broadcasted_iota(jnp.int32, (L, LANES), 1)
    ones_col = jnp.where(lane_g == 0, 1.0, 0.0).astype(BF16)
    nt = (((1,), (1,)), ((), ()))
    tn = (((0,), (0,)), ((), ()))

    def one_direction(d, hh, c, s_qk, q_bf, k_s, v_ext, v_bf, m_prev):
        idx = d * hg + hh
        head = head0 + hh
        causal, anti = (lower, upper) if d == 0 else (upper, lower)
        gates_c = gc_ref[pl.ds(c * L, L), :]

        def col_of(j):
            return jnp.sum(jnp.where(lane_g == j, gates_c, 0.0), axis=-1, keepdims=True)

        i_c = col_of((2 * d) * MLSTM_HEADS + head)
        f_c = _log_sigmoid(col_of((2 * d + 1) * MLSTM_HEADS + head))
        i_r = gr_ref[2 * d, hh, pl.ds(c, 1), :]
        f_r = _log_sigmoid(gr_ref[2 * d + 1, hh, pl.ds(c, 1), :])

        b_c = jnp.sum(jnp.where(causal, f_r, 0.0), axis=1, keepdims=True)
        b_r = jnp.sum(jnp.where(anti, f_c, 0.0), axis=0, keepdims=True)
        b_tot = jnp.sum(f_r, axis=1, keepdims=True)
        dmat = jnp.where(causal, b_c - b_r + i_r, neg)
        m_inter = b_c + m_prev
        m_t = jnp.maximum(m_inter, jnp.max(dmat, axis=-1, keepdims=True))
        w_inter = jnp.exp(m_inter - m_t)
        p = s_qk * jnp.exp(dmat - m_t)
        qc = jnp.dot(q_bf, cext_ref[idx].astype(BF16), preferred_element_type=F32)
        num = w_inter * qc[:, 0:dh] + jnp.dot(p.astype(BF16), v_bf, preferred_element_type=F32)
        den = w_inter * qc[:, dh:dh + 1] + jnp.sum(p, axis=-1, keepdims=True)
        h = num / jnp.maximum(jnp.abs(den), jnp.exp(-m_t))
        last = L - 1 if d == 0 else 0
        m_new = m_t[last:last + 1, :]
        w_c = jnp.exp(b_tot + m_prev - m_new)
        w_s = jnp.exp(b_tot - b_c + i_c - m_new)
        upd = lax.dot_general((w_s * k_s).astype(BF16), v_ext, tn, preferred_element_type=F32)
        cext_ref[idx] = w_c * cext_ref[idx] + upd
        return h, m_new

    def load_chunk(hh, c):
        sl = (pl.ds(c * L, L), slice(hh * dh, (hh + 1) * dh))
        q_bf = q_ref[sl].astype(BF16)
        k_s = k_ref[sl] * (dh ** -0.5)
        v_bf = v_ref[sl].astype(BF16)
        v_ext = jnp.concatenate([v_bf, ones_col], axis=-1)
        s_qk = lax.dot_general(q_bf, k_s.astype(BF16), nt, preferred_element_type=F32)
        return s_qk, q_bf, k_s, v_ext, v_bf

    def step(c, ms):
        cb = nc - 1 - c
        new_ms = []
        for hh in range(hg):
            h, m_new = one_direction(0, hh, c, *load_chunk(hh, c), ms[hh])
            hf_ref[pl.ds(c * L, L), hh * dh:(hh + 1) * dh] = h
            new_ms.append(m_new)
        for hh in range(hg):
            h, m_new = one_direction(1, hh, cb, *load_chunk(hh, cb), ms[hg + hh])
            hb_ref[pl.ds(cb * L, L), hh * dh:(hh + 1) * dh] = h
            new_ms.append(m_new)
        return tuple(new_ms)

    if has_init:
        ms0 = tuple(m0_ref[d, hh] for d in range(2) for hh in range(hg))
    else:
        ms0 = tuple(jnp.zeros((1, 1), F32) for _ in range(2 * hg))
    ms = lax.fori_loop(0, nc, step, ms0)

    for hh in range(hg):
        cs = slice(hh * dh, (hh + 1) * dh)
        hs = hf_ref[:, cs] + hb_ref[:, cs]
        mu = jnp.mean(hs, axis=-1, keepdims=True)
        hc = hs - mu
        var = jnp.mean(hc * hc, axis=-1, keepdims=True)
        hn = hc * lax.rsqrt(var + LN_EPS) * hgain_ref[:, cs]
        out_ref[:, cs] = (_sigmoid(o_ref[:, cs]) * hn).astype(out_ref.dtype)

    if emit_state:
        for d in range(2):
            for hh in range(hg):
                idx = d * hg + hh
                co_ref[d, hh] = cext_ref[idx, :, 0:dh]
                no_ref[d, hh] = cext_ref[idx, :, dh:dh + 1]
                mo_ref[d, hh] = jnp.broadcast_to(ms[idx], (1, LANES))


def _mlstm_old(proj, gates, head_g, g0, n_seq, seq, hg, init=None, emit_state=False):
    g, s, _ = proj.shape
    per_group = s // seq
    n_hg = MLSTM_HEADS // hg
    w = hg * MLSTM_DH
    nc = seq // MLSTM_CHUNK
    n_blocks = D_MODEL // w

    g_seq = gates[g0:g0 + n_seq // per_group].reshape(n_seq, seq, LANES)
    g_row = g_seq[:, :, :N_GATES * MLSTM_HEADS].transpose(0, 2, 1).reshape(
        n_seq, N_GATES, MLSTM_HEADS, nc, MLSTM_CHUNK)

    def tok_map(colblock):
        return lambda b, hi: (g0 + b // per_group, b % per_group, colblock * n_blocks + hi)

    args = [proj, proj, proj, proj, g_seq, g_row, head_g.reshape(1, D_MODEL)]
    in_specs = [
        pl.BlockSpec((None, seq, w), tok_map(0)),
        pl.BlockSpec((None, seq, w), tok_map(1)),
        pl.BlockSpec((None, seq, w), tok_map(2)),
        pl.BlockSpec((None, seq, w), tok_map(3)),
        pl.BlockSpec((None, seq, LANES), lambda b, hi: (b, 0, 0)),
        pl.BlockSpec((None, N_GATES, hg, nc, MLSTM_CHUNK), lambda b, hi: (b, 0, hi, 0, 0)),
        pl.BlockSpec((1, w), lambda b, hi: (0, hi)),
    ]
    if init is not None:
        c0, n0, m0 = init
        args += [c0, n0.reshape(n0.shape + (1,)), m0.reshape(m0.shape + (1, 1))]
        in_specs += [
            pl.BlockSpec((None, 2, hg, MLSTM_DH, MLSTM_DH), lambda b, hi: (b, 0, hi, 0, 0)),
            pl.BlockSpec((None, 2, hg, MLSTM_DH, 1), lambda b, hi: (b, 0, hi, 0, 0)),
            pl.BlockSpec((None, 2, hg, 1, 1), lambda b, hi: (b, 0, hi, 0, 0)),
        ]
    out_shape = [jax.ShapeDtypeStruct((n_seq // per_group, s, D_MODEL), BF16)]
    out_specs = [pl.BlockSpec((None, seq, w), lambda b, hi: (b // per_group, b % per_group, hi))]
    if emit_state:
        out_shape += [
            jax.ShapeDtypeStruct((n_seq, 2, MLSTM_HEADS, MLSTM_DH, MLSTM_DH), F32),
            jax.ShapeDtypeStruct((n_seq, 2, MLSTM_HEADS, MLSTM_DH, 1), F32),
            jax.ShapeDtypeStruct((n_seq, 2, MLSTM_HEADS, 1, LANES), F32),
        ]
        out_specs += [
            pl.BlockSpec((None, 2, hg, MLSTM_DH, MLSTM_DH), lambda b, hi: (b, 0, hi, 0, 0)),
            pl.BlockSpec((None, 2, hg, MLSTM_DH, 1), lambda b, hi: (b, 0, hi, 0, 0)),
            pl.BlockSpec((None, 2, hg, 1, LANES), lambda b, hi: (b, 0, hi, 0, 0)),
        ]

    return pl.pallas_call(
        functools.partial(_mlstm_kernel, seq=seq, hg=hg, has_init=init is not None, emit_state=emit_state),
        out_shape=tuple(out_shape),
        grid=(n_seq, n_hg),
        in_specs=in_specs,
        out_specs=tuple(out_specs),
        scratch_shapes=[
            pltpu.VMEM((2 * hg, MLSTM_DH, 2 * MLSTM_DH), F32),
            pltpu.VMEM((seq, w), F32),
            pltpu.VMEM((seq, w), F32),
        ],
        compiler_params=_cparams(("arbitrary", "arbitrary")),
        name="mlstm_%d" % seq,
    )(*args)


def kernel(x_prompt, x_sample, c, cache_k, cache_v, state_C, state_n, state_m, c_ctx, ada_w, ada_b, ln_g, ln_b, w_in_a, diff_lambda, diff_norm_g, pool_w, pool_scale, w_out_a, ffn_w_gu, ffn_w_down, w_in_c, b_gates_c, mlstm_norm_g, w_out_c, router_w, router_b, moe_w_gu, moe_w_down):
    n_ctx, seq_ctx, d = x_prompt.shape
    n_lat, seq_lat, _ = x_sample.shape
    assert d == D_MODEL and (n_ctx * seq_ctx) % seq_lat == 0 and seq_lat % seq_ctx == 0
    gl = n_lat
    gc = n_ctx * seq_ctx // seq_lat
    s = seq_lat

    x = jnp.concatenate([x_sample, x_prompt.reshape(gc, s, d)], axis=0)
    cvec = jnp.concatenate([c, jnp.broadcast_to(c_ctx[None, :], (gc, d))], axis=0)
    mod_all = _modulation(cvec, ada_w, ada_b).reshape(DEPTH, gl + gc, 6, d)

    mod = mod_all[0]
    lam_init = 0.8 - 0.6 * math.exp(-0.3 * 0)
    cos_t, sin_t = _rope_tables(s)
    proj = _inproj_a(x, mod, w_in_a[0].astype(BF16), cos_t, sin_t, gl)
    norm_g = diff_norm_g[0].reshape(1, LANES)
    attn_c, new_k, new_v = _attn_context(proj, diff_lambda[0], norm_g, gl, n_ctx, seq_ctx, lam_init)
    attn_l = _attn_latent(proj, cache_k, cache_v, diff_lambda[0], norm_g, gl, lam_init)
    pool_c = _pool(proj, pool_w[0], pool_scale[0], gl, gc, seq_ctx)
    pool_l = _pool(proj, pool_w[0], pool_scale[0], 0, gl, seq_lat)
    w_out = w_out_a[0].astype(BF16)
    x = _outproj([(attn_l, attn_c), (pool_l, pool_c)], [w_out[:DIFF_WIDTH], w_out[DIFF_WIDTH:]],
                 x, mod, ln_g[0, 0], ln_b[0, 0], 2)
    x = _ffn(x, mod, ffn_w_gu[0].astype(BF16), ffn_w_down[0].astype(BF16), ln_g[0, 1], ln_b[0, 1])

    mod = mod_all[1]
    n_main = 4 * D_MODEL
    w_main = w_in_c[0][:, :n_main].astype(BF16)
    nh = MLSTM_HEADS
    wg4 = w_in_c[0][:, n_main:].reshape(d, N_GATES, nh)
    bg4 = b_gates_c[0].reshape(1, N_GATES, nh)
    lane_pad = ((0, 0), (0, LANES - 2 * nh))

    def gate_lanes(a):
        return jnp.concatenate([jnp.pad(jnp.concatenate([a[:, 0], a[:, 2]], axis=-1), lane_pad),
                                jnp.pad(jnp.concatenate([a[:, 1], a[:, 3]], axis=-1), lane_pad)], axis=-1)

    proj, gates = _inproj_c(x, mod, w_main, gate_lanes(wg4).astype(BF16), gate_lanes(bg4))
    mix_c, new_c, new_n, new_m = _mlstm(proj, gates, mlstm_norm_g[0], gl, n_ctx, seq_ctx, 4, emit_state=True)
    (mix_l,) = _mlstm(proj, gates, mlstm_norm_g[0], 0, n_lat, seq_lat, 4,
                      init=(state_C[:, 0], state_n[:, 0], state_m[:, 0]))
    x = _outproj([(mix_l, mix_c)], [w_out_c[0].astype(BF16)], x, mod, ln_g[1, 0], ln_b[1, 0], 2)
    y_sample, y_ctx = _moe2(x, mod, router_w[0], router_b[0], moe_w_gu[0].astype(BF16),
                            moe_w_down[0].astype(BF16), ln_g[1, 1], ln_b[1, 1], gl)
    y_prompt = y_ctx.reshape(n_ctx, seq_ctx, d)
    new_m = new_m[:, 0, 0, :2 * MLSTM_HEADS].reshape(n_ctx, 2, MLSTM_HEADS)
    return (y_prompt, y_sample, new_k, new_v, new_c[:, None], new_n[..., 0][:, None], new_m[:, None])
```

```python
import functools
import math

import jax
import jax.numpy as jnp
from jax import lax
from jax.experimental import pallas as pl
from jax.experimental.pallas import tpu as pltpu

F32 = jnp.float32
BF16 = jnp.bfloat16

D_MODEL = 1024
GRID_W = 64
ROPE_BASE = 10000.0
DIFF_HEADS = 4
DIFF_DH = 64
DIFF_WIDTH = DIFF_HEADS * 2 * DIFF_DH
POOL_GROUPS = 4
POOL_GC = 128
POOL_WIDTH = POOL_GROUPS * POOL_GC
POOL_WINDOWS = (2, 4, 8, 16)
W_IN_A = 3 * DIFF_WIDTH + POOL_WIDTH
MLSTM_HEADS = 8
MLSTM_DH = 128
MLSTM_CHUNK = 64
N_GATES = 4
D_FF = 2816
N_EXPERTS = 8
D_FF_EXPERT = 1792
LN_EPS = 1e-5
DEPTH = 2
ALPHA = (2.0 * DEPTH) ** 0.25

LANES = 128
FF_CHUNK = 256
VMEM_LIMIT = 56 * 1024 * 1024


def _cparams(sem):
    return pltpu.CompilerParams(dimension_semantics=sem, vmem_limit_bytes=VMEM_LIMIT)


def _resident(shape, index_map):
    return pl.BlockSpec(shape, index_map, pipeline_mode=pl.Buffered(1))


def _layer_norm_rows(z, g, b):
    mu = jnp.mean(z, axis=-1, keepdims=True)
    zc = z - mu
    var = jnp.mean(zc * zc, axis=-1, keepdims=True)
    return zc * lax.rsqrt(var + LN_EPS) * g + b


def _sigmoid(x):
    return 1.0 / (1.0 + jnp.exp(-x))


def _split_bf16(x):
    hi = x.astype(BF16)
    lo = (x - hi.astype(F32)).astype(BF16)
    return hi, lo


def _mod_kernel(c_ref, w_ref, b_ref, o_ref):
    c = c_ref[...]
    h = (c * _sigmoid(c)).astype(BF16)
    o_ref[...] = jnp.dot(h, w_ref[...].astype(BF16), preferred_element_type=F32) + b_ref[...]


def _modulation(cvec, ada_w, ada_b):
    depth, d, n = ada_w.shape
    g = cvec.shape[0]
    tn = 1536
    return pl.pallas_call(
        _mod_kernel,
        out_shape=jax.ShapeDtypeStruct((depth, g, n), F32),
        grid=(depth, n // tn),
        in_specs=[
            pl.BlockSpec((g, d), lambda l, j: (0, 0)),
            pl.BlockSpec((None, d, tn), lambda l, j: (l, 0, j)),
            pl.BlockSpec((None, 1, tn), lambda l, j: (l, 0, j)),
        ],
        out_specs=pl.BlockSpec((None, g, tn), lambda l, j: (l, 0, j)),
        compiler_params=_cparams(("arbitrary", "arbitrary")),
        name="modulation",
    )(cvec, ada_w, ada_b.reshape(depth, 1, n))


def _rot_half16(x):
    lane = lax.broadcasted_iota(jnp.int32, x.shape, 1)
    return jnp.where((lane % 32) < 16, pltpu.roll(x, LANES - 16, 1), pltpu.roll(x, 16, 1))


def _inproj_a_kernel(x_ref, mod_ref, w_ref, cos_ref, sin_ref, o_ref):
    h = x_ref[...] * (1.0 + mod_ref[1:2, :]) + mod_ref[0:1, :]
    p = jnp.dot(h.astype(BF16), w_ref[...], preferred_element_type=F32)
    cos = cos_ref[...]
    sin = sin_ref[...]
    n_rope = 2 * DIFF_WIDTH // LANES
    for j in range(n_rope):
        blk = p[:, j * LANES:(j + 1) * LANES]
        o_ref[:, j * LANES:(j + 1) * LANES] = blk * cos + _rot_half16(blk) * sin
    o_ref[:, n_rope * LANES:] = p[:, n_rope * LANES:]


def _inproj_a(x, mod, w_bf, cos_t, sin_t, n_latent_groups, tm=512):
    g, s, d = x.shape
    n = w_bf.shape[1]

    def table_map(gi, ti):
        return (jnp.where(gi >= n_latent_groups, 1, 0), ti, 0)

    return pl.pallas_call(
        _inproj_a_kernel,
        out_shape=jax.ShapeDtypeStruct((g, s, n), F32),
        grid=(g, s // tm),
        in_specs=[
            pl.BlockSpec((None, tm, d), lambda gi, ti: (gi, ti, 0)),
            pl.BlockSpec((None, 6, d), lambda gi, ti: (gi, 0, 0)),
            _resident((d, n), lambda gi, ti: (0, 0)),
            pl.BlockSpec((None, tm, LANES), table_map),
            pl.BlockSpec((None, tm, LANES), table_map),
        ],
        out_specs=pl.BlockSpec((None, tm, n), lambda gi, ti: (gi, ti, 0)),
        compiler_params=_cparams(("arbitrary", "arbitrary")),
        name="inproj_a",
    )(x, mod, w_bf, cos_t, sin_t)


def _rope_tables(n_tokens):
    rows = n_tokens // GRID_W
    row_pos = jnp.repeat(jnp.arange(rows), GRID_W).astype(F32)
    col_pos = jnp.tile(jnp.arange(GRID_W), rows).astype(F32)
    n_freq = DIFF_DH // 4
    inv_freq = jnp.power(ROPE_BASE, -jnp.arange(n_freq, dtype=F32) / n_freq)
    ang = jnp.stack([row_pos[:, None] * inv_freq, col_pos[:, None] * inv_freq], axis=1)
    cos, sin = jnp.cos(ang), jnp.sin(ang)
    cos64 = jnp.concatenate([cos[:, 0], cos[:, 0], cos[:, 1], cos[:, 1]], axis=-1)
    sin64 = jnp.concatenate([-sin[:, 0], sin[:, 0], -sin[:, 1], sin[:, 1]], axis=-1)
    cos_l = jnp.tile(cos64, (1, LANES // DIFF_DH))
    sin_l = jnp.tile(sin64, (1, LANES // DIFF_DH))
    cos_t = jnp.stack([cos_l, jnp.ones_like(cos_l)])
    sin_t = jnp.stack([sin_l, jnp.zeros_like(sin_l)])
    return cos_t, sin_t


def _diff_attn_kernel(*refs, n_pieces, lam_init, emit_kv):
    lam_ref, ng_ref, q_ref = refs[:3]
    kv_refs = refs[3:3 + 2 * n_pieces]
    o_ref = refs[3 + 2 * n_pieces]

    lp = lam_ref[...]
    lam = (jnp.exp(jnp.sum(lp[0:1] * lp[1:2], axis=-1, keepdims=True))
           - jnp.exp(jnp.sum(lp[2:3] * lp[3:4], axis=-1, keepdims=True)) + lam_init)

    q = q_ref[...] * (DIFF_DH ** -0.5)
    lane = lax.broadcasted_iota(jnp.int32, q.shape, 1)
    q1 = jnp.where(lane < DIFF_DH, q, 0.0).astype(BF16)
    q2 = jnp.where(lane >= DIFF_DH, q, 0.0).astype(BF16)

    nt = (((1,), (1,)), ((), ()))
    s1, s2, vs = [], [], []
    for i in range(n_pieces):
        kb = kv_refs[2 * i][...].astype(BF16)
        vs.append(kv_refs[2 * i + 1][...].astype(BF16))
        s1.append(lax.dot_general(q1, kb, nt, preferred_element_type=F32))
        s2.append(lax.dot_general(q2, kb, nt, preferred_element_type=F32))

    def softmax_pieces(ss):
        m = functools.reduce(jnp.maximum, [jnp.max(s, axis=-1, keepdims=True) for s in ss])
        es = [jnp.exp(s - m) for s in ss]
        l = functools.reduce(jnp.add, [jnp.sum(e, axis=-1, keepdims=True) for e in es])
        return [e / l for e in es]

    p1 = softmax_pieces(s1)
    p2 = softmax_pieces(s2)
    o = None
    for i in range(n_pieces):
        a = (p1[i] - lam * p2[i]).astype(BF16)
        t = jnp.dot(a, vs[i], preferred_element_type=F32)
        o = t if o is None else o + t
    o = o * lax.rsqrt(jnp.mean(o * o, axis=-1, keepdims=True) + LN_EPS)
    o_ref[...] = (o * ng_ref[...] * (1.0 - lam_init)).astype(o_ref.dtype)
    if emit_kv:
        ko_ref, vo_ref = refs[4 + 2 * n_pieces:]
        ko_ref[...] = kv_refs[0][...]
        vo_ref[...] = kv_refs[1][...]


def _attn_context(proj, lam_p, norm_g, n_latent_groups, n_seq, seq, lam_init):
    g, s, _ = proj.shape
    per_group = s // seq
    blk = (None, seq, LANES)

    def tok_map(col0):
        return lambda b, h: (n_latent_groups + b // per_group, b % per_group, col0 + h)

    cache_shape = jax.ShapeDtypeStruct((n_seq, 1, DIFF_HEADS, seq, LANES), F32)
    cache_spec = pl.BlockSpec((None, None, None, seq, LANES), lambda b, h: (b, 0, h, 0, 0))
    out_spec = pl.BlockSpec(blk, lambda b, h: (b // per_group, b % per_group, h))
    return pl.pallas_call(
        functools.partial(_diff_attn_kernel, n_pieces=1, lam_init=lam_init, emit_kv=True),
        out_shape=(jax.ShapeDtypeStruct((g - n_latent_groups, s, DIFF_WIDTH), BF16), cache_shape, cache_shape),
        grid=(n_seq, DIFF_HEADS),
        in_specs=[
            pl.BlockSpec((4, DIFF_DH), lambda b, h: (0, 0)),
            pl.BlockSpec((1, LANES), lambda b, h: (0, 0)),
            pl.BlockSpec(blk, tok_map(0)),
            pl.BlockSpec(blk, tok_map(DIFF_HEADS)),
            pl.BlockSpec(blk, tok_map(2 * DIFF_HEADS)),
        ],
        out_specs=(out_spec, cache_spec, cache_spec),
        compiler_params=_cparams(("arbitrary", "arbitrary")),
        name="attn_context",
    )(lam_p, norm_g, proj, proj, proj)


def _attn_latent(proj, cache_k, cache_v, lam_p, norm_g, n_latent_groups, lam_init, tq=256):
    g, s, _ = proj.shape
    past = cache_k.shape[3]
    cache_spec = pl.BlockSpec((None, None, None, past, LANES), lambda b, h, qi: (b, 0, h, 0, 0))
    return pl.pallas_call(
        functools.partial(_diff_attn_kernel, n_pieces=2, lam_init=lam_init, emit_kv=False),
        out_shape=jax.ShapeDtypeStruct((n_latent_groups, s, DIFF_WIDTH), BF16),
        grid=(n_latent_groups, DIFF_HEADS, s // tq),
        in_specs=[
            pl.BlockSpec((4, DIFF_DH), lambda b, h, qi: (0, 0)),
            pl.BlockSpec((1, LANES), lambda b, h, qi: (0, 0)),
            pl.BlockSpec((None, tq, LANES), lambda b, h, qi: (b, qi, h)),
            cache_spec,
            cache_spec,
            pl.BlockSpec((None, s, LANES), lambda b, h, qi: (b, 0, DIFF_HEADS + h)),
            pl.BlockSpec((None, s, LANES), lambda b, h, qi: (b, 0, 2 * DIFF_HEADS + h)),
        ],
        out_specs=pl.BlockSpec((None, tq, LANES), lambda b, h, qi: (b, qi, h)),
        compiler_params=_cparams(("arbitrary", "arbitrary", "arbitrary")),
        name="attn_latent",
    )(lam_p, norm_g, proj, cache_k, cache_v, proj, proj)


def _pool_kernel(p_ref, w_ref, sc_ref, o_ref, band_ref, *, seq):
    @pl.when((pl.program_id(0) == 0) & (pl.program_id(1) == 0))
    def _():
        t = lax.broadcasted_iota(jnp.int32, (seq, seq), 0)
        s_ = lax.broadcasted_iota(jnp.int32, (seq, seq), 1)
        for gi, w in enumerate(POOL_WINDOWS):
            inside = (s_ >= t - w // 2) & (s_ <= t + w // 2 - 1)
            band_ref[gi] = jnp.where(inside, 1.0, 0.0).astype(BF16)

    tcol = lax.broadcasted_iota(jnp.int32, (seq, 1), 0)
    for gi, w in enumerate(POOL_WINDOWS):
        u = p_ref[:, gi * POOL_GC:(gi + 1) * POOL_GC]
        hi, lo = _split_bf16(u)
        band = band_ref[gi]
        win = (jnp.dot(band, hi, preferred_element_type=F32)
               + jnp.dot(band, lo, preferred_element_type=F32))
        cnt = (jnp.minimum(tcol + (w // 2 - 1), seq - 1) - jnp.maximum(tcol - w // 2, 0) + 1).astype(F32)
        pooled = win / cnt - u
        mixed = jnp.dot(pooled.astype(BF16), w_ref[gi].astype(BF16), preferred_element_type=F32)
        o_ref[:, gi * POOL_GC:(gi + 1) * POOL_GC] = (
            mixed * sc_ref[:, gi * POOL_GC:(gi + 1) * POOL_GC]).astype(o_ref.dtype)


def _pool(proj, pool_w, pool_scale, g0, n_groups, seq):
    g, s, _ = proj.shape
    col = 3 * DIFF_WIDTH // POOL_WIDTH
    return pl.pallas_call(
        functools.partial(_pool_kernel, seq=seq),
        out_shape=jax.ShapeDtypeStruct((n_groups, s, POOL_WIDTH), BF16),
        grid=(n_groups, s // seq),
        in_specs=[
            pl.BlockSpec((None, seq, POOL_WIDTH), lambda gi, ti: (g0 + gi, ti, col)),
            pl.BlockSpec((POOL_GROUPS, POOL_GC, POOL_GC), lambda gi, ti: (0, 0, 0)),
            pl.BlockSpec((1, POOL_WIDTH), lambda gi, ti: (0, 0)),
        ],
        out_specs=pl.BlockSpec((None, seq, POOL_WIDTH), lambda gi, ti: (gi, ti, 0)),
        scratch_shapes=[pltpu.VMEM((POOL_GROUPS, seq, seq), BF16)],
        compiler_params=_cparams(("arbitrary", "arbitrary")),
        name="pool_%d" % seq,
    )(proj, pool_w, pool_scale.reshape(1, POOL_WIDTH))


def _outproj_kernel(*refs, n_in, gate_row, n_latent_groups):
    a_refs = refs[:2 * n_in]
    w_refs = refs[2 * n_in:3 * n_in]
    x_ref, mod_ref, g_ref, b_ref, o_ref = refs[3 * n_in:]
    is_latent = pl.program_id(0) < n_latent_groups
    acc = None
    for i, w_ref in enumerate(w_refs):
        a = jnp.where(is_latent, a_refs[2 * i][...], a_refs[2 * i + 1][...])
        t = jnp.dot(a, w_ref[...], preferred_element_type=F32)
        acc = t if acc is None else acc + t
    z = ALPHA * x_ref[...] + mod_ref[gate_row:gate_row + 1, :] * acc
    o_ref[...] = _layer_norm_rows(z, g_ref[...], b_ref[...])


def _outproj(acts, weights, x, mod, ln_g, ln_b, gate_row, tm=512):
    g, s, d = x.shape
    n_in = len(acts)
    gl = acts[0][0].shape[0]
    in_specs = []
    flat_acts = []
    for a_lat, a_ctx in acts:
        k = a_lat.shape[-1]
        in_specs.append(pl.BlockSpec((None, tm, k), lambda gi, ti: (jnp.minimum(gi, gl - 1), jnp.where(gi < gl, ti, 0), 0)))
        in_specs.append(pl.BlockSpec((None, tm, k), lambda gi, ti: (jnp.maximum(gi - gl, 0), jnp.where(gi < gl, 0, ti), 0)))
        flat_acts += [a_lat, a_ctx]
    in_specs += [_resident(w.shape, lambda gi, ti: (0, 0)) for w in weights]
    in_specs += [
        pl.BlockSpec((None, tm, d), lambda gi, ti: (gi, ti, 0)),
        pl.BlockSpec((None, 6, d), lambda gi, ti: (gi, 0, 0)),
        pl.BlockSpec((1, d), lambda gi, ti: (0, 0)),
        pl.BlockSpec((1, d), lambda gi, ti: (0, 0)),
    ]
    return pl.pallas_call(
        functools.partial(_outproj_kernel, n_in=n_in, gate_row=gate_row, n_latent_groups=gl),
        out_shape=jax.ShapeDtypeStruct((g, s, d), F32),
        grid=(g, s // tm),
        in_specs=in_specs,
        out_specs=pl.BlockSpec((None, tm, d), lambda gi, ti: (gi, ti, 0)),
        compiler_params=_cparams(("arbitrary", "arbitrary")),
        name="outproj",
    )(*flat_acts, *weights, x, mod, ln_g.reshape(1, d), ln_b.reshape(1, d))


def _swiglu_chunks(h_bf, wgu_ref, wd_ref, d_ff):
    acc = None
    for j in range(d_ff // FF_CHUNK):
        lo = j * FF_CHUNK
        gate = jnp.dot(h_bf, wgu_ref[:, lo:lo + FF_CHUNK], preferred_element_type=F32)
        up = jnp.dot(h_bf, wgu_ref[:, d_ff + lo:d_ff + lo + FF_CHUNK], preferred_element_type=F32)
        act = (gate * _sigmoid(gate) * up).astype(BF16)
        t = jnp.dot(act, wd_ref[lo:lo + FF_CHUNK, :], preferred_element_type=F32)
        acc = t if acc is None else acc + t
    return acc


def _ffn_kernel(x_ref, mod_ref, wgu_ref, wd_ref, g_ref, b_ref, o_ref):
    x = x_ref[...]
    h = (x * (1.0 + mod_ref[4:5, :]) + mod_ref[3:4, :]).astype(BF16)
    acc = _swiglu_chunks(h, wgu_ref, wd_ref, D_FF)
    z = ALPHA * x + mod_ref[5:6, :] * acc
    o_ref[...] = _layer_norm_rows(z, g_ref[...], b_ref[...])


def _ffn(x, mod, wgu_bf, wd_bf, ln_g, ln_b, tm=256):
    g, s, d = x.shape
    return pl.pallas_call(
        _ffn_kernel,
        out_shape=jax.ShapeDtypeStruct((g, s, d), F32),
        grid=(g, s // tm),
        in_specs=[
            pl.BlockSpec((None, tm, d), lambda gi, ti: (gi, ti, 0)),
            pl.BlockSpec((None, 6, d), lambda gi, ti: (gi, 0, 0)),
            _resident(wgu_bf.shape, lambda gi, ti: (0, 0)),
            _resident(wd_bf.shape, lambda gi, ti: (0, 0)),
            pl.BlockSpec((1, d), lambda gi, ti: (0, 0)),
            pl.BlockSpec((1, d), lambda gi, ti: (0, 0)),
        ],
        out_specs=pl.BlockSpec((None, tm, d), lambda gi, ti: (gi, ti, 0)),
        compiler_params=_cparams(("arbitrary", "arbitrary")),
        name="ffn",
    )(x, mod, wgu_bf, wd_bf, ln_g.reshape(1, d), ln_b.reshape(1, d))


def _router_combine(h, rw_ref, rb_ref):
    h_hi, h_lo = _split_bf16(h)
    w_hi, w_lo = _split_bf16(rw_ref[...])
    logits = (jnp.dot(h_hi, w_hi, preferred_element_type=F32)
              + jnp.dot(h_lo, w_hi, preferred_element_type=F32)
              + jnp.dot(h_hi, w_lo, preferred_element_type=F32)) + rb_ref[...]
    lane = lax.broadcasted_iota(jnp.int32, logits.shape, 1).astype(F32)
    neg = -jnp.inf
    logits = jnp.where(lane < N_EXPERTS, logits, neg)
    m1 = jnp.max(logits, axis=-1, keepdims=True)
    i1 = jnp.min(jnp.where(logits == m1, lane, float(LANES)), axis=-1, keepdims=True)
    rest = jnp.where(lane == i1, neg, logits)
    m2 = jnp.max(rest, axis=-1, keepdims=True)
    i2 = jnp.min(jnp.where(rest == m2, lane, float(LANES)), axis=-1, keepdims=True)
    e2 = jnp.exp(m2 - m1)
    w1 = 1.0 / (1.0 + e2)
    w2 = e2 / (1.0 + e2)
    member = (lane == i1) | (lane == i2)
    return jnp.where(lane == i1, w1, 0.0) + jnp.where(lane == i2, w2, 0.0), member


MOE_BM = 256
MOE_TC = 256
MOE_TMC = 512


def _route_kernel(x_ref, mod_ref, rw_ref, rb_ref, h_ref, cw_ref, srank_ref, cnt_ref, tri_ref, run_ref):
    tm = x_ref.shape[0]

    @pl.when(pl.program_id(0) == 0)
    def _():
        r = lax.broadcasted_iota(jnp.int32, (tm, tm), 0)
        c = lax.broadcasted_iota(jnp.int32, (tm, tm), 1)
        tri_ref[...] = jnp.where(c <= r, 1.0, 0.0).astype(BF16)
        run_ref[...] = jnp.zeros_like(run_ref)

    h = x_ref[...] * (1.0 + mod_ref[4:5, :]) + mod_ref[3:4, :]
    h_ref[...] = h.astype(BF16)
    cw, member = _router_combine(h, rw_ref, rb_ref)
    cw_ref[...] = cw
    mem = jnp.where(member, 1.0, 0.0)
    rank = jnp.dot(tri_ref[...], mem.astype(BF16), preferred_element_type=F32) + run_ref[...]
    srank_ref[...] = jnp.where(member, rank, -rank)
    run_ref[...] = rank[tm - 1:tm, :]
    cnt_ref[...] = rank[tm - 1:tm, :]


def _route(x, mod, router_w, router_b, tm=512):
    g, s, d = x.shape
    n = g * s
    per = s // tm
    rw = jnp.pad(router_w, ((0, 0), (0, LANES - N_EXPERTS)))
    rb = jnp.pad(router_b, (0, LANES - N_EXPERTS)).reshape(1, LANES)
    return pl.pallas_call(
        _route_kernel,
        out_shape=(jax.ShapeDtypeStruct((n, d), BF16), jax.ShapeDtypeStruct((n, LANES), F32),
                   jax.ShapeDtypeStruct((n, LANES), F32), jax.ShapeDtypeStruct((1, LANES), F32)),
        grid=(n // tm,),
        in_specs=[
            pl.BlockSpec((None, tm, d), lambda i: (i // per, i % per, 0)),
            pl.BlockSpec((None, 6, d), lambda i: (i // per, 0, 0)),
            pl.BlockSpec((d, LANES), lambda i: (0, 0)),
            pl.BlockSpec((1, LANES), lambda i: (0, 0)),
        ],
        out_specs=(pl.BlockSpec((tm, d), lambda i: (i, 0)), pl.BlockSpec((tm, LANES), lambda i: (i, 0)),
                   pl.BlockSpec((tm, LANES), lambda i: (i, 0)), pl.BlockSpec((1, LANES), lambda i: (0, 0))),
        scratch_shapes=[pltpu.VMEM((tm, tm), BF16), pltpu.VMEM((1, LANES), F32)],
        compiler_params=_cparams(("arbitrary",)),
        name="moe_route",
    )(x, mod, rw, rb)


def _moe_plan(srank, counts, n_blocks, n_items):
    e_n = N_EXPERTS
    n = srank.shape[0]
    i32 = jnp.int32
    cnt = counts[0, :e_n].astype(i32)
    nb = (cnt + MOE_BM - 1) // MOE_BM
    nb_incl = jnp.cumsum(nb)
    gstart = nb_incl - nb
    n_used = nb_incl[-1]
    sr = srank[:, :e_n]
    rank = jnp.abs(sr).astype(i32)
    pos_tok = jnp.where(sr > 0, rank - 1 + MOE_BM * gstart[None, :], -1)
    rank_t = rank.T

    r = jnp.arange(n_blocks, dtype=i32)
    used = r < n_used
    rc = jnp.minimum(r, n_used - 1)
    e_r = jnp.minimum(jnp.sum(nb_incl[None, :] <= rc[:, None], axis=1, dtype=i32), e_n - 1)
    b = rc - gstart[e_r]
    lo = b * MOE_BM + 1
    hi = jnp.minimum((b + 1) * MOE_BM, cnt[e_r])
    def find(e, v):
        return jnp.sum(rank_t[e] < v[:, None], axis=1, dtype=i32)

    jlo = jnp.where(used, find(e_r, lo) // MOE_TC, 0)
    jhi = jnp.where(used, find(e_r, hi) // MOE_TC, -1)

    n_tiles = n // MOE_TMC
    ends = rank[MOE_TMC - 1::MOE_TMC]
    starts = jnp.concatenate([jnp.zeros((1, e_n), i32), ends[:-1]], axis=0)
    fb = gstart[None, :] + starts // MOE_BM
    lb = gstart[None, :] + (ends - 1) // MOE_BM
    n_pe = jnp.where(ends > starts, lb - fb + 1, 0).reshape(-1)
    incl = jnp.cumsum(n_pe)
    off = incl - n_pe
    total = incl[-1]
    w = jnp.arange(n_items, dtype=i32)
    valid = w < total
    wc = jnp.minimum(w, total - 1)
    p = jnp.sum(incl[None, :] <= wc[:, None], axis=1, dtype=i32)
    it_tile = p // e_n
    it_e = p % e_n
    it_blk = fb.reshape(-1)[p] + (wc - off[p])
    tile_off = jnp.concatenate([off[::e_n], total[None]])
    it_first = (wc == tile_off[it_tile]).astype(i32)
    it_last = (wc == tile_off[it_tile + 1] - 1).astype(i32)
    return (pos_tok, e_r, jlo, jhi, n_used.reshape(1),
            it_tile, it_blk, it_e, it_first, it_last, valid.astype(i32))


def _experts_kernel(be_ref, jlo_ref, jhi_ref, nused_ref, h_ref, pos_ref, wgu_ref, wd_ref, y_ref, xg_ref):
    r = pl.program_id(0)

    @pl.when(r < nused_ref[0])
    def _():
        e = be_ref[r]
        slot = r * MOE_BM + lax.broadcasted_iota(jnp.int32, (MOE_BM, MOE_TC), 0)
        xg_ref[...] = jnp.zeros_like(xg_ref)

        def chunk(j, carry):
            onehot = jnp.where(pos_ref[e, pl.ds(j, 1), :] == slot, 1.0, 0.0).astype(BF16)
            rows = h_ref[pl.ds(pl.multiple_of(j * MOE_TC, MOE_TC), MOE_TC), :]
            xg_ref[...] += jnp.dot(onehot, rows, preferred_element_type=F32)
            return carry

        lax.fori_loop(jlo_ref[r], jhi_ref[r] + 1, chunk, 0)
        y_ref[...] = _swiglu_chunks(xg_ref[...].astype(BF16), wgu_ref, wd_ref, D_FF_EXPERT)

    @pl.when(r >= nused_ref[0])
    def _():
        y_ref[...] = jnp.zeros_like(y_ref)


def _experts(h, pos_row, plan, wgu_bf, wd_bf, n_blocks):
    n, d = h.shape
    e_r, jlo, jhi, n_used = plan
    grid_spec = pltpu.PrefetchScalarGridSpec(
        num_scalar_prefetch=4,
        grid=(n_blocks,),
        in_specs=[
            _resident((n, d), lambda r, be, lo, hi, nu: (0, 0)),
            _resident(pos_row.shape, lambda r, be, lo, hi, nu: (0, 0, 0)),
            pl.BlockSpec((None, d, 2 * D_FF_EXPERT), lambda r, be, lo, hi, nu: (be[r], 0, 0),
                         pipeline_mode=pl.Buffered(1)),
            pl.BlockSpec((None, D_FF_EXPERT, d), lambda r, be, lo, hi, nu: (be[r], 0, 0)),
        ],
        out_specs=pl.BlockSpec((MOE_BM, d), lambda r, be, lo, hi, nu: (r, 0)),
        scratch_shapes=[pltpu.VMEM((MOE_BM, d), F32)],
    )
    return pl.pallas_call(
        _experts_kernel,
        out_shape=jax.ShapeDtypeStruct((n_blocks * MOE_BM, d), F32),
        grid_spec=grid_spec,
        compiler_params=_cparams(("arbitrary",)),
        name="moe_experts",
    )(e_r, jlo, jhi, n_used, h, pos_row, wgu_bf, wd_bf)


def _combine_kernel(tile_ref, blk_ref, e_ref, first_ref, last_ref, valid_ref,
                    y_ref, pos_ref, cw_ref, x_ref, mod_ref, g_ref, b_ref, o_ref, acc_ref):
    w = pl.program_id(0)

    @pl.when(valid_ref[w] == 1)
    def _():
        @pl.when(first_ref[w] == 1)
        def _():
            acc_ref[...] = jnp.zeros_like(acc_ref)

        e = e_ref[w]
        lane = lax.broadcasted_iota(jnp.int32, (MOE_TMC, LANES), 1)
        pos_e = jnp.sum(jnp.where(lane == e, pos_ref[...], 0.0), axis=-1, keepdims=True)
        cw_e = jnp.sum(jnp.where(lane == e, cw_ref[...], 0.0), axis=-1, keepdims=True)
        slot = (blk_ref[w] * MOE_BM + lax.broadcasted_iota(jnp.int32, (MOE_TMC, MOE_BM), 1)).astype(F32)
        onehot = jnp.where(pos_e == slot, 1.0, 0.0).astype(BF16)
        y_hi, y_lo = _split_bf16(y_ref[...])
        part = (jnp.dot(onehot, y_hi, preferred_element_type=F32)
                + jnp.dot(onehot, y_lo, preferred_element_type=F32))
        acc_ref[...] += cw_e * part

        @pl.when(last_ref[w] == 1)
        def _():
            z = ALPHA * x_ref[...] + mod_ref[5:6, :] * acc_ref[...]
            o_ref[...] = _layer_norm_rows(z, g_ref[...], b_ref[...])


def _combine(y, pos_tok_f, cw, x, mod, ln_g, ln_b, items, n_items):
    g, s, d = x.shape
    per = s // MOE_TMC

    def tok2(w, tile, *_):
        return (tile[w], 0)

    def tok3(w, tile, *_):
        return (tile[w] // per, tile[w] % per, 0)

    grid_spec = pltpu.PrefetchScalarGridSpec(
        num_scalar_prefetch=6,
        grid=(n_items,),
        in_specs=[
            pl.BlockSpec((MOE_BM, d), lambda w, tile, blk, *_: (blk[w], 0)),
            pl.BlockSpec((MOE_TMC, LANES), tok2),
            pl.BlockSpec((MOE_TMC, LANES), tok2),
            pl.BlockSpec((None, MOE_TMC, d), tok3),
            pl.BlockSpec((None, 6, d), lambda w, tile, *_: (tile[w] // per, 0, 0)),
            pl.BlockSpec((1, d), lambda w, *_: (0, 0)),
            pl.BlockSpec((1, d), lambda w, *_: (0, 0)),
        ],
        out_specs=pl.BlockSpec((None, MOE_TMC, d), tok3),
        scratch_shapes=[pltpu.VMEM((MOE_TMC, d), F32)],
    )
    return pl.pallas_call(
        _combine_kernel,
        out_shape=jax.ShapeDtypeStruct((g, s, d), F32),
        grid_spec=grid_spec,
        compiler_params=_cparams(("arbitrary",)),
        name="moe_combine",
    )(*items, y, pos_tok_f, cw, x, mod, ln_g.reshape(1, d), ln_b.reshape(1, d))


def _moe(x, mod, router_w, router_b, wgu_bf, wd_bf, ln_g, ln_b):
    g, s, d = x.shape
    n = g * s
    n_slots = 2 * n
    n_blocks = n_slots // MOE_BM + N_EXPERTS
    n_items = n_blocks + N_EXPERTS * (n // MOE_TMC)
    h, cw, srank, counts = _route(x, mod, router_w, router_b)
    plan = _moe_plan(srank, counts, n_blocks, n_items)
    pos_tok = plan[0]
    pos_row = pos_tok.T.reshape(N_EXPERTS, n // MOE_TC, MOE_TC)
    y = _experts(h, pos_row, plan[1:5], wgu_bf, wd_bf, n_blocks)
    pos_tok_f = jnp.pad(pos_tok.astype(F32), ((0, 0), (0, LANES - N_EXPERTS)), constant_values=-1.0)
    return _combine(y, pos_tok_f, cw, x, mod, ln_g, ln_b, plan[5:], n_items)


META_LANES = LANES


def _router_top2(h, rw_ref, rb_ref):
    h_hi, h_lo = _split_bf16(h)
    w_hi, w_lo = _split_bf16(rw_ref[...])
    logits = (jnp.dot(h_hi, w_hi, preferred_element_type=F32)
              + jnp.dot(h_lo, w_hi, preferred_element_type=F32)
              + jnp.dot(h_hi, w_lo, preferred_element_type=F32)) + rb_ref[...]
    lane = lax.broadcasted_iota(jnp.int32, logits.shape, 1).astype(F32)
    neg = -jnp.inf
    logits = jnp.where(lane < N_EXPERTS, logits, neg)
    m1 = jnp.max(logits, axis=-1, keepdims=True)
    i1 = jnp.min(jnp.where(logits == m1, lane, float(LANES)), axis=-1, keepdims=True)
    rest = jnp.where(lane == i1, neg, logits)
    m2 = jnp.max(rest, axis=-1, keepdims=True)
    i2 = jnp.min(jnp.where(rest == m2, lane, float(LANES)), axis=-1, keepdims=True)
    e2 = jnp.exp(m2 - m1)
    return lane, i1, i2, 1.0 / (1.0 + e2), e2 / (1.0 + e2)


def _route2_kernel(x_ref, mod_ref, rw_ref, rb_ref, h_ref, wk_ref, srank_ref, cnt_ref, tri_ref, run_ref):
    tm, d = x_ref.shape

    @pl.when(pl.program_id(0) == 0)
    def _():
        r = lax.broadcasted_iota(jnp.int32, (tm, tm), 0)
        c = lax.broadcasted_iota(jnp.int32, (tm, tm), 1)
        tri_ref[...] = jnp.where(c <= r, 1.0, 0.0).astype(BF16)
        run_ref[...] = jnp.zeros_like(run_ref)

    h = x_ref[...] * (1.0 + mod_ref[4:5, :]) + mod_ref[3:4, :]
    h_ref[:, 0:d] = h.astype(BF16)
    lane, i1, i2, w1, w2 = _router_top2(h, rw_ref, rb_ref)
    first_is_low = i1 < i2
    e_hi = jnp.where(first_is_low, i2, i1)
    wk_ref[...] = jnp.where(lane == 0.0, jnp.where(first_is_low, w1, w2),
                            jnp.where(lane == 1.0, jnp.where(first_is_low, w2, w1), 0.0))
    tok = (pl.program_id(0) * tm + lax.broadcasted_iota(jnp.int32, (tm, META_LANES), 0))
    meta = jnp.where(lane == 0.0, (tok // 128).astype(F32),
                     jnp.where(lane == 1.0, (tok % 128).astype(F32),
                               jnp.where(lane == 2.0, e_hi, jnp.where(lane == 3.0, 1.0, 0.0))))
    h_ref[:, d:d + META_LANES] = meta.astype(BF16)

    member = (lane == i1) | (lane == i2)
    mem = jnp.where(member, 1.0, 0.0)
    rank = jnp.dot(tri_ref[...], mem.astype(BF16), preferred_element_type=F32) + run_ref[...]
    srank_ref[...] = jnp.where(member, rank, -rank)
    run_ref[...] = rank[tm - 1:tm, :]
    cnt_ref[...] = rank[tm - 1:tm, :]


def _route2(x, mod, router_w, router_b, tm=512):
    g, s, d = x.shape
    n = g * s
    assert n <= 128 * 256
    per = s // tm
    rw = jnp.pad(router_w, ((0, 0), (0, LANES - N_EXPERTS)))
    rb = jnp.pad(router_b, (0, LANES - N_EXPERTS)).reshape(1, LANES)
    return pl.pallas_call(
        _route2_kernel,
        out_shape=(jax.ShapeDtypeStruct((n, d + META_LANES), BF16), jax.ShapeDtypeStruct((n, LANES), F32),
                   jax.ShapeDtypeStruct((n, LANES), F32), jax.ShapeDtypeStruct((1, LANES), F32)),
        grid=(n // tm,),
        in_specs=[
            pl.BlockSpec((None, tm, d), lambda i: (i // per, i % per, 0)),
            pl.BlockSpec((None, 6, d), lambda i: (i // per, 0, 0)),
            pl.BlockSpec((d, LANES), lambda i: (0, 0)),
            pl.BlockSpec((1, LANES), lambda i: (0, 0)),
        ],
        out_specs=(pl.BlockSpec((tm, d + META_LANES), lambda i: (i, 0)), pl.BlockSpec((tm, LANES), lambda i: (i, 0)),
                   pl.BlockSpec((tm, LANES), lambda i: (i, 0)), pl.BlockSpec((1, LANES), lambda i: (0, 0))),
        scratch_shapes=[pltpu.VMEM((tm, tm), BF16), pltpu.VMEM((1, LANES), F32)],
        compiler_params=_cparams(("arbitrary",)),
        name="moe_route",
    )(x, mod, rw, rb)


def _moe_plan2(srank, counts, n_blocks):
    e_n = N_EXPERTS
    i32 = jnp.int32
    cnt = counts[0, :e_n].astype(i32)
    nb = (cnt + MOE_BM - 1) // MOE_BM
    nb_incl = jnp.cumsum(nb)
    gstart = nb_incl - nb
    n_used = nb_incl[-1]
    sr = srank[:, :e_n]
    rank = jnp.abs(sr).astype(i32)
    pos_tok = jnp.where(sr > 0, rank - 1 + MOE_BM * gstart[None, :], -1)
    chunk_end = rank[MOE_TC - 1::MOE_TC].T

    r = jnp.arange(n_blocks, dtype=i32)
    used = r < n_used
    rc = jnp.minimum(r, n_used - 1)
    e_r = jnp.minimum(jnp.sum(nb_incl[None, :] <= rc[:, None], axis=1, dtype=i32), e_n - 1)
    b = rc - gstart[e_r]
    lo = b * MOE_BM + 1
    hi = jnp.minimum((b + 1) * MOE_BM, cnt[e_r])
    ends_r = chunk_end[e_r]
    jlo = jnp.where(used, jnp.sum(ends_r < lo[:, None], axis=1, dtype=i32), 0)
    jhi = jnp.where(used, jnp.sum(ends_r < hi[:, None], axis=1, dtype=i32), -1)
    return pos_tok, e_r, jlo, jhi, n_used.reshape(1)


def _experts2_kernel(be_ref, jlo_ref, jhi_ref, nused_ref, h_ref, pos_ref, wgu_ref, wd_ref, y_ref, meta_ref, xg_ref):
    r = pl.program_id(0)
    d = y_ref.shape[1]

    @pl.when(r < nused_ref[0])
    def _():
        e = be_ref[r]
        slot = r * MOE_BM + lax.broadcasted_iota(jnp.int32, (MOE_BM, MOE_TC), 0)
        xg_ref[...] = jnp.zeros_like(xg_ref)

        def chunk(j, carry):
            onehot = jnp.where(pos_ref[e, pl.ds(j, 1), :] == slot, 1.0, 0.0).astype(BF16)
            rows = h_ref[pl.ds(pl.multiple_of(j * MOE_TC, MOE_TC), MOE_TC), :]
            xg_ref[...] += jnp.dot(onehot, rows, preferred_element_type=F32)
            return carry

        lax.fori_loop(jlo_ref[r], jhi_ref[r] + 1, chunk, 0)
        meta_ref[...] = xg_ref[:, d:d + META_LANES]
        y_ref[...] = _swiglu_chunks(xg_ref[:, 0:d].astype(BF16), wgu_ref, wd_ref, D_FF_EXPERT)

    @pl.when(r >= nused_ref[0])
    def _():
        y_ref[...] = jnp.zeros_like(y_ref)
        meta_ref[...] = jnp.zeros_like(meta_ref)


def _experts2(h, pos_row, plan, wgu_bf, wd_bf, n_blocks):
    n, dx = h.shape
    d = dx - META_LANES
    e_r, jlo, jhi, n_used = plan
    grid_spec = pltpu.PrefetchScalarGridSpec(
        num_scalar_prefetch=4,
        grid=(n_blocks,),
        in_specs=[
            _resident((n, dx), lambda r, be, lo, hi, nu: (0, 0)),
            _resident(pos_row.shape, lambda r, be, lo, hi, nu: (0, 0, 0)),
            pl.BlockSpec((None, d, 2 * D_FF_EXPERT), lambda r, be, lo, hi, nu: (be[r], 0, 0),
                         pipeline_mode=pl.Buffered(1)),
            pl.BlockSpec((None, D_FF_EXPERT, d), lambda r, be, lo, hi, nu: (be[r], 0, 0),
                         pipeline_mode=pl.Buffered(1)),
        ],
        out_specs=(pl.BlockSpec((MOE_BM, d), lambda r, be, lo, hi, nu: (r, 0)),
                   pl.BlockSpec((MOE_BM, META_LANES), lambda r, be, lo, hi, nu: (r, 0))),
        scratch_shapes=[pltpu.VMEM((MOE_BM, dx), F32)],
    )
    return pl.pallas_call(
        _experts2_kernel,
        out_shape=(jax.ShapeDtypeStruct((n_blocks * MOE_BM, d), F32),
                   jax.ShapeDtypeStruct((n_blocks * MOE_BM, META_LANES), F32)),
        grid_spec=grid_spec,
        compiler_params=_cparams(("arbitrary",)),
        name="moe_experts",
    )(e_r, jlo, jhi, n_used, h, pos_row, wgu_bf, wd_bf)


def _scatter_kernel(dest_ref, y_ref, o_ref, ybuf_ref, sem):
    r = pl.program_id(0)
    n_steps = pl.num_programs(0)
    buf = r % 2

    def wait_block(b):
        pltpu.make_async_copy(ybuf_ref.at[b], o_ref.at[pl.ds(0, MOE_BM), :], sem.at[b]).wait()

    @pl.when(r >= 2)
    def _():
        wait_block(buf)

    ybuf_ref[buf] = y_ref[...]

    def issue(i, carry):
        pltpu.make_async_copy(ybuf_ref.at[buf, pl.ds(i, 1), :], o_ref.at[pl.ds(dest_ref[0, i], 1), :],
                              sem.at[buf]).start()
        return carry

    lax.fori_loop(0, MOE_BM, issue, 0, unroll=8)

    @pl.when(r == n_steps - 1)
    def _():
        wait_block(1 - buf)
        wait_block(buf)


def _scatter_rows(y, dest, n_rows_out):
    n_slots, d = y.shape
    n_blocks = n_slots // MOE_BM
    assert n_blocks >= 2
    return pl.pallas_call(
        _scatter_kernel,
        out_shape=jax.ShapeDtypeStruct((n_rows_out, d), F32),
        grid=(n_blocks,),
        in_specs=[
            pl.BlockSpec((None, 1, MOE_BM), lambda r: (r, 0, 0), memory_space=pltpu.SMEM),
            pl.BlockSpec((MOE_BM, d), lambda r: (r, 0)),
        ],
        out_specs=pl.BlockSpec(memory_space=pl.ANY),
        scratch_shapes=[pltpu.VMEM((2, MOE_BM, d), F32), pltpu.SemaphoreType.DMA((2,))],
        compiler_params=_cparams(("arbitrary",)),
        name="moe_scatter",
    )(dest.reshape(n_blocks, 1, MOE_BM), y)


def _finish_kernel(a_ref, b_ref, wk_ref, x_ref, mod_ref, g_ref, bb_ref, ol_ref, oc_ref, *, n_latent_tiles):
    lane = lax.broadcasted_iota(jnp.int32, wk_ref.shape, 1)
    wk = wk_ref[...]
    w0 = jnp.sum(jnp.where(lane == 0, wk, 0.0), axis=-1, keepdims=True)
    w1 = jnp.sum(jnp.where(lane == 1, wk, 0.0), axis=-1, keepdims=True)
    acc = w0 * a_ref[...] + w1 * b_ref[...]
    z = ALPHA * x_ref[...] + mod_ref[5:6, :] * acc
    out = _layer_norm_rows(z, g_ref[...], bb_ref[...])
    is_latent = pl.program_id(0) < n_latent_tiles

    @pl.when(is_latent)
    def _():
        ol_ref[...] = out

    @pl.when(jnp.logical_not(is_latent))
    def _():
        oc_ref[...] = out


def _finish(rows, wk, x, mod, ln_g, ln_b, n_latent_groups, tm=512):
    g, s, d = x.shape
    n = g * s
    per = s // tm
    nt = n // tm
    nlt = n_latent_groups * per

    def lat_map(i):
        j = jnp.minimum(i, nlt - 1)
        return (j // per, j % per, 0)

    def ctx_map(i):
        j = jnp.maximum(i - nlt, 0)
        return (j // per, j % per, 0)

    return pl.pallas_call(
        functools.partial(_finish_kernel, n_latent_tiles=nlt),
        out_shape=(jax.ShapeDtypeStruct((n_latent_groups, s, d), F32),
                   jax.ShapeDtypeStruct((g - n_latent_groups, s, d), F32)),
        grid=(nt,),
        in_specs=[
            pl.BlockSpec((tm, d), lambda i: (i, 0)),
            pl.BlockSpec((tm, d), lambda i: (nt + i, 0)),
            pl.BlockSpec((tm, LANES), lambda i: (i, 0)),
            pl.BlockSpec((None, tm, d), lambda i: (i // per, i % per, 0)),
            pl.BlockSpec((None, 6, d), lambda i: (i // per, 0, 0)),
            pl.BlockSpec((1, d), lambda i: (0, 0)),
            pl.BlockSpec((1, d), lambda i: (0, 0)),
        ],
        out_specs=(pl.BlockSpec((None, tm, d), lat_map), pl.BlockSpec((None, tm, d), ctx_map)),
        compiler_params=_cparams(("arbitrary",)),
        name="moe_finish",
    )(rows, rows, wk, x, mod, ln_g.reshape(1, d), ln_b.reshape(1, d))


def _route3_kernel(x_ref, mod_ref, rw_ref, rb_ref, h_ref, meta_ref, wk_ref, srank_ref, cnt_ref, tri_ref, run_ref):
    tm, d = x_ref.shape

    @pl.when(pl.program_id(0) == 0)
    def _():
        r = lax.broadcasted_iota(jnp.int32, (tm, tm), 0)
        c = lax.broadcasted_iota(jnp.int32, (tm, tm), 1)
        tri_ref[...] = jnp.where(c <= r, 1.0, 0.0).astype(BF16)
        run_ref[...] = jnp.zeros_like(run_ref)

    h = x_ref[...] * (1.0 + mod_ref[4:5, :]) + mod_ref[3:4, :]
    h_ref[...] = h
    lane, i1, i2, w1, w2 = _router_top2(h, rw_ref, rb_ref)
    first_is_low = i1 < i2
    e_hi = jnp.where(first_is_low, i2, i1)
    wk_ref[...] = jnp.where(lane == 0.0, jnp.where(first_is_low, w1, w2),
                            jnp.where(lane == 1.0, jnp.where(first_is_low, w2, w1), 0.0))
    tok = (pl.program_id(0) * tm + lax.broadcasted_iota(jnp.int32, (tm, META_LANES), 0))
    meta = jnp.where(lane == 0.0, (tok // 128).astype(F32),
                     jnp.where(lane == 1.0, (tok % 128).astype(F32),
                               jnp.where(lane == 2.0, e_hi, jnp.where(lane == 3.0, 1.0, 0.0))))
    meta_ref[...] = meta.astype(BF16)

    member = (lane == i1) | (lane == i2)
    mem = jnp.where(member, 1.0, 0.0)
    rank = jnp.dot(tri_ref[...], mem.astype(BF16), preferred_element_type=F32) + run_ref[...]
    srank_ref[...] = jnp.where(member, rank, -rank)
    run_ref[...] = rank[tm - 1:tm, :]
    cnt_ref[...] = rank[tm - 1:tm, :]


def _route3(x, mod, router_w, router_b, tm=512):
    g, s, d = x.shape
    n = g * s
    assert n <= 128 * 256
    per = s // tm
    rw = jnp.pad(router_w, ((0, 0), (0, LANES - N_EXPERTS)))
    rb = jnp.pad(router_b, (0, LANES - N_EXPERTS)).reshape(1, LANES)
    tok_lanes = lambda i: (i, 0)
    return pl.pallas_call(
        _route3_kernel,
        out_shape=(jax.ShapeDtypeStruct((n, d), F32), jax.ShapeDtypeStruct((n, META_LANES), BF16),
                   jax.ShapeDtypeStruct((n, LANES), F32), jax.ShapeDtypeStruct((n, LANES), F32),
                   jax.ShapeDtypeStruct((1, LANES), F32)),
        grid=(n // tm,),
        in_specs=[
            pl.BlockSpec((None, tm, d), lambda i: (i // per, i % per, 0)),
            pl.BlockSpec((None, 6, d), lambda i: (i // per, 0, 0)),
            pl.BlockSpec((d, LANES), lambda i: (0, 0)),
            pl.BlockSpec((1, LANES), lambda i: (0, 0)),
        ],
        out_specs=(pl.BlockSpec((tm, d), tok_lanes), pl.BlockSpec((tm, META_LANES), tok_lanes),
                   pl.BlockSpec((tm, LANES), tok_lanes), pl.BlockSpec((tm, LANES), tok_lanes),
                   pl.BlockSpec((1, LANES), lambda i: (0, 0))),
        scratch_shapes=[pltpu.VMEM((tm, tm), BF16), pltpu.VMEM((1, LANES), F32)],
        compiler_params=_cparams(("arbitrary",)),
        name="moe_route",
    )(x, mod, rw, rb)


def _slot_meta_kernel(be_ref, jlo_ref, jhi_ref, nused_ref, meta_ref, pos_ref, o_ref):
    r = pl.program_id(0)
    o_ref[...] = jnp.zeros_like(o_ref)

    @pl.when(r < nused_ref[0])
    def _():
        e = be_ref[r]
        slot = r * MOE_BM + lax.broadcasted_iota(jnp.int32, (MOE_BM, MOE_TC), 0)

        def chunk(j, carry):
            onehot = jnp.where(pos_ref[e, pl.ds(j, 1), :] == slot, 1.0, 0.0).astype(BF16)
            rows = meta_ref[pl.ds(pl.multiple_of(j * MOE_TC, MOE_TC), MOE_TC), :]
            o_ref[...] += jnp.dot(onehot, rows, preferred_element_type=F32)
            return carry

        lax.fori_loop(jlo_ref[r], jhi_ref[r] + 1, chunk, 0)


def _slot_meta(meta, pos_row, plan, n_blocks):
    n = meta.shape[0]
    e_r, jlo, jhi, n_used = plan
    grid_spec = pltpu.PrefetchScalarGridSpec(
        num_scalar_prefetch=4,
        grid=(n_blocks,),
        in_specs=[
            _resident((n, META_LANES), lambda r, be, lo, hi, nu: (0, 0)),
            _resident(pos_row.shape, lambda r, be, lo, hi, nu: (0, 0, 0)),
        ],
        out_specs=pl.BlockSpec((MOE_BM, META_LANES), lambda r, be, lo, hi, nu: (r, 0)),
    )
    return pl.pallas_call(
        _slot_meta_kernel,
        out_shape=jax.ShapeDtypeStruct((n_blocks * MOE_BM, META_LANES), F32),
        grid_spec=grid_spec,
        compiler_params=_cparams(("arbitrary",)),
        name="moe_slot_meta",
    )(e_r, jlo, jhi, n_used, meta, pos_row)


def _experts3_kernel(be_ref, nused_ref, src_ref, nsrc_ref, dest_ref, ldest_ref, h_ref, wgu_ref, wd_ref, o_ref,
                     xg_ref, ybuf_ref, sem_g, sem_s):
    r = pl.program_id(0)
    n_steps = pl.num_programs(0)
    cur = r % 2
    nxt = 1 - cur
    d_ff = D_FF_EXPERT

    def gather_row(i, idx_ref, buf):
        return pltpu.make_async_copy(h_ref.at[pl.ds(idx_ref[0, i], 1), :], xg_ref.at[buf, pl.ds(i, 1), :],
                                     sem_g.at[buf])

    def scatter_row(i, buf):
        return pltpu.make_async_copy(ybuf_ref.at[buf, pl.ds(i, 1), :], o_ref.at[pl.ds(dest_ref[0, i], 1), :],
                                     sem_s.at[buf])

    def wait_gather(buf):
        pltpu.make_async_copy(h_ref.at[pl.ds(0, MOE_BM), :], xg_ref.at[buf], sem_g.at[buf]).wait()

    def wait_scatter(buf):
        pltpu.make_async_copy(ybuf_ref.at[buf], o_ref.at[pl.ds(0, MOE_BM), :], sem_s.at[buf]).wait()

    def issue_rows(lo, hi):
        for i in range(lo, hi):
            gather_row(i, nsrc_ref, nxt).start()
            scatter_row(i, nxt).start()

    @pl.when(r == 0)
    def _():
        ybuf_ref[1] = jnp.zeros((MOE_BM, ybuf_ref.shape[2]), F32)

        def first(i, carry):
            gather_row(i, src_ref, 0).start()
            return carry

        lax.fori_loop(0, MOE_BM, first, 0, unroll=8)

    wait_gather(cur)

    @pl.when(r < nused_ref[0])
    def _():
        x_bf = xg_ref[cur].astype(BF16)
        n_chunks = d_ff // FF_CHUNK
        per = -(-MOE_BM // n_chunks)
        acc = None
        for j in range(n_chunks):
            lo = j * FF_CHUNK
            gate = jnp.dot(x_bf, wgu_ref[:, lo:lo + FF_CHUNK], preferred_element_type=F32)
            up = jnp.dot(x_bf, wgu_ref[:, d_ff + lo:d_ff + lo + FF_CHUNK], preferred_element_type=F32)
            act = (gate * _sigmoid(gate) * up).astype(BF16)
            t = jnp.dot(act, wd_ref[lo:lo + FF_CHUNK, :], preferred_element_type=F32)
            acc = t if acc is None else acc + t
            issue_rows(min(j * per, MOE_BM), min((j + 1) * per, MOE_BM))

        @pl.when(r >= 1)
        def _():
            wait_scatter(cur)

        ybuf_ref[cur] = acc

    @pl.when(r >= nused_ref[0])
    def _():
        def both(i, carry):
            gather_row(i, nsrc_ref, nxt).start()
            scatter_row(i, nxt).start()
            return carry

        lax.fori_loop(0, MOE_BM, both, 0, unroll=8)

        @pl.when(r >= 1)
        def _():
            wait_scatter(cur)

        ybuf_ref[cur] = jnp.zeros((MOE_BM, ybuf_ref.shape[2]), F32)

    @pl.when(r == n_steps - 1)
    def _():
        def last(i, carry):
            pltpu.make_async_copy(ybuf_ref.at[cur, pl.ds(i, 1), :], o_ref.at[pl.ds(ldest_ref[0, i], 1), :],
                                  sem_s.at[cur]).start()
            return carry

        lax.fori_loop(0, MOE_BM, last, 0, unroll=8)
        wait_scatter(nxt)
        wait_scatter(cur)
        wait_gather(nxt)


def _experts3(h, src, dest, e_r, n_used, wgu_bf, wd_bf, n_blocks):
    n, d = h.shape
    last = n_blocks - 1
    smem_blk = lambda f: pl.BlockSpec((None, 1, MOE_BM), f, memory_space=pltpu.SMEM)
    grid_spec = pltpu.PrefetchScalarGridSpec(
        num_scalar_prefetch=2,
        grid=(n_blocks,),
        in_specs=[
            smem_blk(lambda r, be, nu: (0, 0, 0)),
            smem_blk(lambda r, be, nu: (jnp.minimum(r + 1, last), 0, 0)),
            smem_blk(lambda r, be, nu: (r, 0, 0)),
            smem_blk(lambda r, be, nu: (n_blocks, 0, 0)),
            pl.BlockSpec(memory_space=pl.ANY),
            pl.BlockSpec((None, d, 2 * D_FF_EXPERT), lambda r, be, nu: (be[r], 0, 0)),
            pl.BlockSpec((None, D_FF_EXPERT, d), lambda r, be, nu: (be[r], 0, 0)),
        ],
        out_specs=pl.BlockSpec(memory_space=pl.ANY),
        scratch_shapes=[pltpu.VMEM((2, MOE_BM, d), F32), pltpu.VMEM((2, MOE_BM, d), F32),
                        pltpu.SemaphoreType.DMA((2,)), pltpu.SemaphoreType.DMA((2,))],
    )
    return pl.pallas_call(
        _experts3_kernel,
        out_shape=jax.ShapeDtypeStruct((n_blocks * MOE_BM + MOE_BM, d), F32),
        grid_spec=grid_spec,
        compiler_params=_cparams(("arbitrary",)),
        name="moe_experts",
    )(e_r, n_used, src, src, dest, dest, h, wgu_bf, wd_bf)


def _moe3(x, mod, router_w, router_b, wgu_bf, wd_bf, ln_g, ln_b, n_latent_groups):
    g, s, d = x.shape
    n = g * s
    n_blocks = 2 * n // MOE_BM + N_EXPERTS
    n_slots = n_blocks * MOE_BM
    h, meta_tok, wk, srank, counts = _route3(x, mod, router_w, router_b)
    pos_tok, e_r, jlo, jhi, n_used = _moe_plan2(srank, counts, n_blocks)
    pos_row = pos_tok.T.reshape(N_EXPERTS, n // MOE_TC, MOE_TC)
    meta = _slot_meta(meta_tok, pos_row, (e_r, jlo, jhi, n_used), n_blocks)
    i32 = jnp.int32
    tok = (meta[:, 0] * 128.0 + meta[:, 1]).astype(i32)
    choice = (meta[:, 2] == jnp.repeat(e_r, MOE_BM).astype(F32)).astype(i32)
    unused = meta[:, 3] < 0.5
    spare = 2 * n + jnp.cumsum(unused.astype(i32)) - 1
    dest = jnp.where(unused, spare, choice * n + tok).reshape(n_blocks, 1, MOE_BM)
    placeholder = (n_slots + jnp.arange(MOE_BM, dtype=i32)).reshape(1, 1, MOE_BM)
    dest_tab = jnp.concatenate([placeholder, dest], axis=0)
    rows = _experts3(h, tok.reshape(n_blocks, 1, MOE_BM), dest_tab, e_r, n_used, wgu_bf, wd_bf, n_blocks)
    return _finish(rows, wk, x, mod, ln_g, ln_b, n_latent_groups)


def _moe2(x, mod, router_w, router_b, wgu_bf, wd_bf, ln_g, ln_b, n_latent_groups):
    g, s, d = x.shape
    n = g * s
    n_blocks = 2 * n // MOE_BM + N_EXPERTS
    n_slots = n_blocks * MOE_BM
    h, wk, srank, counts = _route2(x, mod, router_w, router_b)
    pos_tok, e_r, jlo, jhi, n_used = _moe_plan2(srank, counts, n_blocks)
    pos_row = pos_tok.T.reshape(N_EXPERTS, n // MOE_TC, MOE_TC)
    y, meta = _experts2(h, pos_row, (e_r, jlo, jhi, n_used), wgu_bf, wd_bf, n_blocks)
    tok = (meta[:, 0] * 128.0 + meta[:, 1]).astype(jnp.int32)
    choice = (meta[:, 2] == jnp.repeat(e_r, MOE_BM).astype(F32)).astype(jnp.int32)
    unused = meta[:, 3] < 0.5
    spare = 2 * n + jnp.cumsum(unused.astype(jnp.int32)) - 1
    dest = jnp.where(unused, spare, choice * n + tok)
    rows = _scatter_rows(y, dest, n_slots)
    return _finish(rows, wk, x, mod, ln_g, ln_b, n_latent_groups)


def _inproj_c_kernel(x_ref, mod_ref, w_ref, wg_ref, bg_ref, o_ref, og_ref):
    h = (x_ref[...] * (1.0 + mod_ref[1:2, :]) + mod_ref[0:1, :]).astype(BF16)
    o_ref[...] = jnp.dot(h, w_ref[...], preferred_element_type=F32)
    og_ref[...] = jnp.dot(h, wg_ref[...], preferred_element_type=F32) + bg_ref[...]


def _inproj_c(x, mod, w_bf, wg_bf, bg, tm=512):
    g, s, d = x.shape
    n = w_bf.shape[1]
    ng = wg_bf.shape[1]
    return pl.pallas_call(
        _inproj_c_kernel,
        out_shape=(jax.ShapeDtypeStruct((g, s, n), F32), jax.ShapeDtypeStruct((g, s, ng), F32)),
        grid=(g, s // tm),
        in_specs=[
            pl.BlockSpec((None, tm, d), lambda gi, ti: (gi, ti, 0)),
            pl.BlockSpec((None, 6, d), lambda gi, ti: (gi, 0, 0)),
            _resident((d, n), lambda gi, ti: (0, 0)),
            _resident((d, ng), lambda gi, ti: (0, 0)),
            pl.BlockSpec((1, ng), lambda gi, ti: (0, 0)),
        ],
        out_specs=(pl.BlockSpec((None, tm, n), lambda gi, ti: (gi, ti, 0)),
                   pl.BlockSpec((None, tm, ng), lambda gi, ti: (gi, ti, 0))),
        compiler_params=_cparams(("arbitrary", "arbitrary")),
        name="inproj_c",
    )(x, mod, w_bf, wg_bf, bg)


def _log_sigmoid(x):
    return jnp.minimum(x, 0.0) - jnp.log(1.0 + jnp.exp(-jnp.abs(x)))


MLSTM_L = 128


def _split3_bf16(x):
    hi = x.astype(BF16)
    r1 = x - hi.astype(F32)
    mid = r1.astype(BF16)
    lo = (r1 - mid.astype(F32)).astype(BF16)
    return hi, mid, lo


def _mlstm_kernel(*refs, seq, hg, has_init, emit_state):
    q_ref, k_ref, v_ref, o_ref, gi_ref, gf_ref, hgain_ref = refs[:7]
    pos = 7
    if has_init:
        c0_ref, n0_ref, m0_ref = refs[pos:pos + 3]
        pos += 3
    out_ref = refs[pos]
    pos += 1
    if emit_state:
        co_ref, no_ref, mo_ref = refs[pos:pos + 3]
        pos += 3
    cext_ref, hf_ref, hb_ref, b_ref, g_ref, gmax_ref, mt_ref, wi_ref, en_ref, ws_ref, gt_ref, wc_ref = refs[pos:]

    L = MLSTM_L
    dh = MLSTM_DH
    nh = MLSTM_HEADS
    nc = seq // L
    head0 = pl.program_id(1) * hg
    neg = -jnp.inf

    lane = lax.broadcasted_iota(jnp.int32, (L, LANES), 1)
    lane1 = lax.broadcasted_iota(jnp.int32, (1, LANES), 1)
    row = lax.broadcasted_iota(jnp.int32, (L, L), 0)
    col = lax.broadcasted_iota(jnp.int32, (L, L), 1)
    lower = col <= row
    upper = col >= row
    tri_l = jnp.where(lower, 1.0, 0.0).astype(BF16)
    tri_u = jnp.where(upper, 1.0, 0.0).astype(BF16)
    fwd_lane = lane < nh
    fwd_lane1 = lane1 < nh
    trow = lax.broadcasted_iota(jnp.int32, (L, LANES), 0)

    btot, glast = [], []
    for c in range(nc):
        rows = slice(c * L, (c + 1) * L)
        f = _log_sigmoid(gf_ref[rows, :])
        parts = _split3_bf16(f)
        pre = sum(jnp.dot(tri_l, p, preferred_element_type=F32) for p in parts)
        suf = sum(jnp.dot(tri_u, p, preferred_element_type=F32) for p in parts)
        b = jnp.where(fwd_lane, pre, suf)
        g = gi_ref[rows, :] - b
        gp, gs = g, g
        k = 1
        while k < L:
            gp = jnp.where(trow >= k, jnp.maximum(gp, pltpu.roll(gp, k, 0)), gp)
            gs = jnp.where(trow < L - k, jnp.maximum(gs, pltpu.roll(gs, L - k, 0)), gs)
            k *= 2
        gmax = jnp.where(fwd_lane, gp, gs)
        b_ref[rows, :] = b
        g_ref[rows, :] = g
        gmax_ref[rows, :] = gmax
        btot.append(jnp.where(fwd_lane1, b[L - 1:L, :], b[0:1, :]))
        glast.append(jnp.where(fwd_lane1, gmax[L - 1:L, :], gmax[0:1, :]))

    m_init = m0_ref[...] if has_init else jnp.zeros((1, LANES), F32)
    mf, mb = m_init, m_init
    ms_f, mn_f, ms_b, mn_b = [None] * nc, [None] * nc, [None] * nc, [None] * nc
    for c in range(nc):
        ms_f[c] = mf
        mf = btot[c] + jnp.maximum(mf, glast[c])
        mn_f[c] = mf
        cb = nc - 1 - c
        ms_b[cb] = mb
        mb = btot[cb] + jnp.maximum(mb, glast[cb])
        mn_b[cb] = mb
    m_final = jnp.where(fwd_lane1, mf, mb)

    for c in range(nc):
        rows = slice(c * L, (c + 1) * L)
        m_start = jnp.where(fwd_lane1, ms_f[c], ms_b[c])
        m_next = jnp.where(fwd_lane1, mn_f[c], mn_b[c])
        g = g_ref[rows, :]
        mt = jnp.maximum(m_start, gmax_ref[rows, :])
        mt_ref[rows, :] = mt
        wi_ref[rows, :] = jnp.exp(m_start - mt)
        en_ref[rows, :] = jnp.exp(-(b_ref[rows, :] + mt))
        ws_ref[rows, :] = jnp.exp(btot[c] + g - m_next)
        gt_ref[c] = g.T
        wc_ref[c:c + 1, :] = jnp.exp(btot[c] + m_start - m_next)

    lane_d = lax.broadcasted_iota(jnp.int32, (dh, dh), 1)
    for d in range(2):
        for hh in range(hg):
            idx = d * hg + hh
            if has_init:
                cext_ref[idx, :, 0:dh] = c0_ref[d, hh]
                cext_ref[idx, :, dh:2 * dh] = jnp.where(lane_d == 0, n0_ref[d, hh], 0.0)
            else:
                cext_ref[idx] = jnp.zeros((dh, 2 * dh), F32)

    ones_col = jnp.where(lane == 0, 1.0, 0.0).astype(BF16)
    nt = (((1,), (1,)), ((), ()))
    tn = (((0,), (0,)), ((), ()))

    def column(x, j):
        return jnp.sum(jnp.where(lane == j, x, 0.0), axis=-1, keepdims=True)

    def one_direction(d, hh, c, s_qk, q_bf, k_s, v_ext, v_bf):
        idx = d * hg + hh
        j = d * nh + head0 + hh
        rows = pl.ds(pl.multiple_of(c * L, L), L)
        mt = column(mt_ref[rows, :], j)
        wi = column(wi_ref[rows, :], j)
        en = column(en_ref[rows, :], j)
        ws = column(ws_ref[rows, :], j)
        g_r = gt_ref[c, pl.ds(j, 1), :]
        w_c = jnp.sum(jnp.where(lane1 == j, wc_ref[pl.ds(c, 1), :], 0.0), axis=-1, keepdims=True)
        causal = lower if d == 0 else upper
        p = s_qk * jnp.exp(jnp.where(causal, g_r - mt, neg))
        qc = jnp.dot(q_bf, cext_ref[idx].astype(BF16), preferred_element_type=F32)
        num = wi * qc[:, 0:dh] + jnp.dot(p.astype(BF16), v_bf, preferred_element_type=F32)
        den = wi * qc[:, dh:dh + 1] + jnp.sum(p, axis=-1, keepdims=True)
        h = num / jnp.maximum(jnp.abs(den), en)
        upd = lax.dot_general((ws * k_s).astype(BF16), v_ext, tn, preferred_element_type=F32)
        cext_ref[idx] = w_c * cext_ref[idx] + upd
        return h

    def load_chunk(hh, c):
        sl = (pl.ds(pl.multiple_of(c * L, L), L), slice(hh * dh, (hh + 1) * dh))
        q_bf = q_ref[sl].astype(BF16)
        k_s = k_ref[sl] * (dh ** -0.5)
        v_bf = v_ref[sl].astype(BF16)
        v_ext = jnp.concatenate([v_bf, ones_col], axis=-1)
        s_qk = lax.dot_general(q_bf, k_s.astype(BF16), nt, preferred_element_type=F32)
        return s_qk, q_bf, k_s, v_ext, v_bf

    def step(c, carry):
        cb = nc - 1 - c
        for hh in range(hg):
            h = one_direction(0, hh, c, *load_chunk(hh, c))
            hf_ref[pl.ds(pl.multiple_of(c * L, L), L), hh * dh:(hh + 1) * dh] = h
        for hh in range(hg):
            h = one_direction(1, hh, cb, *load_chunk(hh, cb))
            hb_ref[pl.ds(pl.multiple_of(cb * L, L), L), hh * dh:(hh + 1) * dh] = h
        return carry

    lax.fori_loop(0, nc, step, 0)

    for hh in range(hg):
        cs = slice(hh * dh, (hh + 1) * dh)
        hs = hf_ref[:, cs] + hb_ref[:, cs]
        mu = jnp.mean(hs, axis=-1, keepdims=True)
        hc = hs - mu
        var = jnp.mean(hc * hc, axis=-1, keepdims=True)
        hn = hc * lax.rsqrt(var + LN_EPS) * hgain_ref[:, cs]
        out_ref[:, cs] = (_sigmoid(o_ref[:, cs]) * hn).astype(out_ref.dtype)

    if emit_state:
        for d in range(2):
            for hh in range(hg):
                idx = d * hg + hh
                co_ref[d, hh] = cext_ref[idx, :, 0:dh]
                no_ref[d, hh] = cext_ref[idx, :, dh:dh + 1]
        mo_ref[...] = m_final


def _mlstm(proj, gates, head_g, g0, n_seq, seq, hg, init=None, emit_state=False):
    g, s, _ = proj.shape
    per_group = s // seq
    n_hg = MLSTM_HEADS // hg
    w = hg * MLSTM_DH
    nc = seq // MLSTM_L
    n_blocks = D_MODEL // w

    def tok_map(colblock):
        return lambda b, hi: (g0 + b // per_group, b % per_group, colblock * n_blocks + hi)

    def gate_map(half):
        return lambda b, hi: (g0 + b // per_group, b % per_group, half)

    args = [proj, proj, proj, proj, gates, gates, head_g.reshape(1, D_MODEL)]
    in_specs = [
        pl.BlockSpec((None, seq, w), tok_map(0)),
        pl.BlockSpec((None, seq, w), tok_map(1)),
        pl.BlockSpec((None, seq, w), tok_map(2)),
        pl.BlockSpec((None, seq, w), tok_map(3)),
        pl.BlockSpec((None, seq, LANES), gate_map(0)),
        pl.BlockSpec((None, seq, LANES), gate_map(1)),
        pl.BlockSpec((1, w), lambda b, hi: (0, hi)),
    ]
    if init is not None:
        c0, n0, m0 = init
        m0_lanes = jnp.pad(m0.reshape(n_seq, 1, 2 * MLSTM_HEADS), ((0, 0), (0, 0), (0, LANES - 2 * MLSTM_HEADS)))
        args += [c0, n0.reshape(n0.shape + (1,)), m0_lanes]
        in_specs += [
            pl.BlockSpec((None, 2, hg, MLSTM_DH, MLSTM_DH), lambda b, hi: (b, 0, hi, 0, 0)),
            pl.BlockSpec((None, 2, hg, MLSTM_DH, 1), lambda b, hi: (b, 0, hi, 0, 0)),
            pl.BlockSpec((None, 1, LANES), lambda b, hi: (b, 0, 0)),
        ]

    out_shape = [jax.ShapeDtypeStruct((n_seq // per_group, s, D_MODEL), BF16)]
    out_specs = [pl.BlockSpec((None, seq, w), lambda b, hi: (b // per_group, b % per_group, hi))]
    if emit_state:
        out_shape += [
            jax.ShapeDtypeStruct((n_seq, 2, MLSTM_HEADS, MLSTM_DH, MLSTM_DH), F32),
            jax.ShapeDtypeStruct((n_seq, 2, MLSTM_HEADS, MLSTM_DH, 1), F32),
            jax.ShapeDtypeStruct((n_seq, n_hg, 1, LANES), F32),
        ]
        out_specs += [
            pl.BlockSpec((None, 2, hg, MLSTM_DH, MLSTM_DH), lambda b, hi: (b, 0, hi, 0, 0)),
            pl.BlockSpec((None, 2, hg, MLSTM_DH, 1), lambda b, hi: (b, 0, hi, 0, 0)),
            pl.BlockSpec((None, None, 1, LANES), lambda b, hi: (b, hi, 0, 0)),
        ]

    tok_scratch = pltpu.VMEM((seq, LANES), F32)
    return pl.pallas_call(
        functools.partial(_mlstm_kernel, seq=seq, hg=hg, has_init=init is not None, emit_state=emit_state),
        out_shape=tuple(out_shape),
        grid=(n_seq, n_hg),
        in_specs=in_specs,
        out_specs=tuple(out_specs),
        scratch_shapes=[
            pltpu.VMEM((2 * hg, MLSTM_DH, 2 * MLSTM_DH), F32),
            pltpu.VMEM((seq, w), F32),
            pltpu.VMEM((seq, w), F32),
        ] + [tok_scratch] * 7 + [
            pltpu.VMEM((nc, LANES, MLSTM_L), F32),
            pltpu.VMEM((max(nc, 8), LANES), F32),
        ],
        compiler_params=_cparams(("arbitrary", "arbitrary")),
        name="mlstm_%d" % seq,
    )(*args)


def _mlstm_kernel_old(*refs, seq, hg, has_init, emit_state):
    q_ref, k_ref, v_ref, o_ref, gc_ref, gr_ref, hgain_ref = refs[:7]
    pos = 7
    if has_init:
        c0_ref, n0_ref, m0_ref = refs[pos:pos + 3]
        pos += 3
    out_ref = refs[pos]
    pos += 1
    if emit_state:
        co_ref, no_ref, mo_ref = refs[pos:pos + 3]
        pos += 3
    cext_ref, hf_ref, hb_ref = refs[pos:pos + 3]

    L = MLSTM_CHUNK
    dh = MLSTM_DH
    nc = seq // L
    head0 = pl.program_id(1) * hg
    neg = -jnp.inf

    lane_d = lax.broadcasted_iota(jnp.int32, (dh, dh), 1)
    for d in range(2):
        for hh in range(hg):
            idx = d * hg + hh
            if has_init:
                cext_ref[idx, :, 0:dh] = c0_ref[d, hh]
                cext_ref[idx, :, dh:2 * dh] = jnp.where(lane_d == 0, n0_ref[d, hh], 0.0)
            else:
                cext_ref[idx] = jnp.zeros((dh, 2 * dh), F32)

    row = lax.broadcasted_iota(jnp.int32, (L, L), 0)
    col = lax.broadcasted_iota(jnp.int32, (L, L), 1)
    lower = col <= row
    upper = col >= row
    lane_g = lax.broadcasted_iota(jnp.int32, (L, LANES), 1)
    ones_col = jnp.where(lane_g == 0, 1.0, 0.0).astype(BF16)
    nt = (((1,), (1,)), ((), ()))
    tn = (((0,), (0,)), ((), ()))

    def one_direction(d, hh, c, s_qk, q_bf, k_s, v_ext, v_bf, m_prev):
        idx = d * hg + hh
        head = head0 + hh
        causal, anti = (lower, upper) if d == 0 else (upper, lower)
        gates_c = gc_ref[pl.ds(c * L, L), :]

        def col_of(j):
            return jnp.sum(jnp.where(lane_g == j, gates_c, 0.0), axis=-1, keepdims=True)

        i_c = col_of((2 * d) * MLSTM_HEADS + head)
        f_c = _log_sigmoid(col_of((2 * d + 1) * MLSTM_HEADS + head))
        i_r = gr_ref[2 * d, hh, pl.ds(c, 1), :]
        f_r = _log_sigmoid(gr_ref[2 * d + 1, hh, pl.ds(c, 1), :])

        b_c = jnp.sum(jnp.where(causal, f_r, 0.0), axis=1, keepdims=True)
        b_r = jnp.sum(jnp.where(anti, f_c, 0.0), axis=0, keepdims=True)
        b_tot = jnp.sum(f_r, axis=1, keepdims=True)
        dmat = jnp.where(causal, b_c - b_r + i_r, neg)
        m_inter = b_c + m_prev
        m_t = jnp.maximum(m_inter, jnp.max(dmat, axis=-1, keepdims=True))
        w_inter = jnp.exp(m_inter - m_t)
        p = s_qk * jnp.exp(dmat - m_t)
        qc = jnp.dot(q_bf, cext_ref[idx].astype(BF16), preferred_element_type=F32)
        num = w_inter * qc[:, 0:dh] + jnp.dot(p.astype(BF16), v_bf, preferred_element_type=F32)
        den = w_inter * qc[:, dh:dh + 1] + jnp.sum(p, axis=-1, keepdims=True)
        h = num / jnp.maximum(jnp.abs(den), jnp.exp(-m_t))
        last = L - 1 if d == 0 else 0
        m_new = m_t[last:last + 1, :]
        w_c = jnp.exp(b_tot + m_prev - m_new)
        w_s = jnp.exp(b_tot - b_c + i_c - m_new)
        upd = lax.dot_general((w_s * k_s).astype(BF16), v_ext, tn, preferred_element_type=F32)
        cext_ref[idx] = w_c * cext_ref[idx] + upd
        return h, m_new

    def load_chunk(hh, c):
        sl = (pl.ds(c * L, L), slice(hh * dh, (hh + 1) * dh))
        q_bf = q_ref[sl].astype(BF16)
        k_s = k_ref[sl] * (dh ** -0.5)
        v_bf = v_ref[sl].astype(BF16)
        v_ext = jnp.concatenate([v_bf, ones_col], axis=-1)
        s_qk = lax.dot_general(q_bf, k_s.astype(BF16), nt, preferred_element_type=F32)
        return s_qk, q_bf, k_s, v_ext, v_bf

    def step(c, ms):
        cb = nc - 1 - c
        new_ms = []
        for hh in range(hg):
            h, m_new = one_direction(0, hh, c, *load_chunk(hh, c), ms[hh])
            hf_ref[pl.ds(c * L, L), hh * dh:(hh + 1) * dh] = h
            new_ms.append(m_new)
        for hh in range(hg):
            h, m_new = one_direction(1, hh, cb, *load_chunk(hh, cb), ms[hg + hh])
            hb_ref[pl.ds(cb * L, L), hh * dh:(hh + 1) * dh] = h
            new_ms.append(m_new)
        return tuple(new_ms)

    if has_init:
        ms0 = tuple(m0_ref[d, hh] for d in range(2) for hh in range(hg))
    else:
        ms0 = tuple(jnp.zeros((1, 1), F32) for _ in range(2 * hg))
    ms = lax.fori_loop(0, nc, step, ms0)

    for hh in range(hg):
        cs = slice(hh * dh, (hh + 1) * dh)
        hs = hf_ref[:, cs] + hb_ref[:, cs]
        mu = jnp.mean(hs, axis=-1, keepdims=True)
        hc = hs - mu
        var = jnp.mean(hc * hc, axis=-1, keepdims=True)
        hn = hc * lax.rsqrt(var + LN_EPS) * hgain_ref[:, cs]
        out_ref[:, cs] = (_sigmoid(o_ref[:, cs]) * hn).astype(out_ref.dtype)

    if emit_state:
        for d in range(2):
            for hh in range(hg):
                idx = d * hg + hh
                co_ref[d, hh] = cext_ref[idx, :, 0:dh]
                no_ref[d, hh] = cext_ref[idx, :, dh:dh + 1]
                mo_ref[d, hh] = jnp.broadcast_to(ms[idx], (1, LANES))


def _mlstm_old(proj, gates, head_g, g0, n_seq, seq, hg, init=None, emit_state=False):
    g, s, _ = proj.shape
    per_group = s // seq
    n_hg = MLSTM_HEADS // hg
    w = hg * MLSTM_DH
    nc = seq // MLSTM_CHUNK
    n_blocks = D_MODEL // w

    g_seq = gates[g0:g0 + n_seq // per_group].reshape(n_seq, seq, LANES)
    g_row = g_seq[:, :, :N_GATES * MLSTM_HEADS].transpose(0, 2, 1).reshape(
        n_seq, N_GATES, MLSTM_HEADS, nc, MLSTM_CHUNK)

    def tok_map(colblock):
        return lambda b, hi: (g0 + b // per_group, b % per_group, colblock * n_blocks + hi)

    args = [proj, proj, proj, proj, g_seq, g_row, head_g.reshape(1, D_MODEL)]
    in_specs = [
        pl.BlockSpec((None, seq, w), tok_map(0)),
        pl.BlockSpec((None, seq, w), tok_map(1)),
        pl.BlockSpec((None, seq, w), tok_map(2)),
        pl.BlockSpec((None, seq, w), tok_map(3)),
        pl.BlockSpec((None, seq, LANES), lambda b, hi: (b, 0, 0)),
        pl.BlockSpec((None, N_GATES, hg, nc, MLSTM_CHUNK), lambda b, hi: (b, 0, hi, 0, 0)),
        pl.BlockSpec((1, w), lambda b, hi: (0, hi)),
    ]
    if init is not None:
        c0, n0, m0 = init
        args += [c0, n0.reshape(n0.shape + (1,)), m0.reshape(m0.shape + (1, 1))]
        in_specs += [
            pl.BlockSpec((None, 2, hg, MLSTM_DH, MLSTM_DH), lambda b, hi: (b, 0, hi, 0, 0)),
            pl.BlockSpec((None, 2, hg, MLSTM_DH, 1), lambda b, hi: (b, 0, hi, 0, 0)),
            pl.BlockSpec((None, 2, hg, 1, 1), lambda b, hi: (b, 0, hi, 0, 0)),
        ]
    out_shape = [jax.ShapeDtypeStruct((n_seq // per_group, s, D_MODEL), BF16)]
    out_specs = [pl.BlockSpec((None, seq, w), lambda b, hi: (b // per_group, b % per_group, hi))]
    if emit_state:
        out_shape += [
            jax.ShapeDtypeStruct((n_seq, 2, MLSTM_HEADS, MLSTM_DH, MLSTM_DH), F32),
            jax.ShapeDtypeStruct((n_seq, 2, MLSTM_HEADS, MLSTM_DH, 1), F32),
            jax.ShapeDtypeStruct((n_seq, 2, MLSTM_HEADS, 1, LANES), F32),
        ]
        out_specs += [
            pl.BlockSpec((None, 2, hg, MLSTM_DH, MLSTM_DH), lambda b, hi: (b, 0, hi, 0, 0)),
            pl.BlockSpec((None, 2, hg, MLSTM_DH, 1), lambda b, hi: (b, 0, hi, 0, 0)),
            pl.BlockSpec((None, 2, hg, 1, LANES), lambda b, hi: (b, 0, hi, 0, 0)),
        ]

    return pl.pallas_call(
        functools.partial(_mlstm_kernel, seq=seq, hg=hg, has_init=init is not None, emit_state=emit_state),
        out_shape=tuple(out_shape),
        grid=(n_seq, n_hg),
        in_specs=in_specs,
        out_specs=tuple(out_specs),
        scratch_shapes=[
            pltpu.VMEM((2 * hg, MLSTM_DH, 2 * MLSTM_DH), F32),
            pltpu.VMEM((seq, w), F32),
            pltpu.VMEM((seq, w), F32),
        ],
        compiler_params=_cparams(("arbitrary", "arbitrary")),
        name="mlstm_%d" % seq,
    )(*args)


def kernel(x_prompt, x_sample, c, cache_k, cache_v, state_C, state_n, state_m, c_ctx, ada_w, ada_b, ln_g, ln_b, w_in_a, diff_lambda, diff_norm_g, pool_w, pool_scale, w_out_a, ffn_w_gu, ffn_w_down, w_in_c, b_gates_c, mlstm_norm_g, w_out_c, router_w, router_b, moe_w_gu, moe_w_down):
    n_ctx, seq_ctx, d = x_prompt.shape
    n_lat, seq_lat, _ = x_sample.shape
    assert d == D_MODEL and (n_ctx * seq_ctx) % seq_lat == 0 and seq_lat % seq_ctx == 0
    gl = n_lat
    gc = n_ctx * seq_ctx // seq_lat
    s = seq_lat

    x = jnp.concatenate([x_sample, x_prompt.reshape(gc, s, d)], axis=0)
    cvec = jnp.concatenate([c, jnp.broadcast_to(c_ctx[None, :], (gc, d))], axis=0)
    mod_all = _modulation(cvec, ada_w, ada_b).reshape(DEPTH, gl + gc, 6, d)

    mod = mod_all[0]
    lam_init = 0.8 - 0.6 * math.exp(-0.3 * 0)
    cos_t, sin_t = _rope_tables(s)
    proj = _inproj_a(x, mod, w_in_a[0].astype(BF16), cos_t, sin_t, gl)
    norm_g = diff_norm_g[0].reshape(1, LANES)
    attn_c, new_k, new_v = _attn_context(proj, diff_lambda[0], norm_g, gl, n_ctx, seq_ctx, lam_init)
    attn_l = _attn_latent(proj, cache_k, cache_v, diff_lambda[0], norm_g, gl, lam_init)
    pool_c = _pool(proj, pool_w[0], pool_scale[0], gl, gc, seq_ctx)
    pool_l = _pool(proj, pool_w[0], pool_scale[0], 0, gl, seq_lat)
    w_out = w_out_a[0].astype(BF16)
    x = _outproj([(attn_l, attn_c), (pool_l, pool_c)], [w_out[:DIFF_WIDTH], w_out[DIFF_WIDTH:]],
                 x, mod, ln_g[0, 0], ln_b[0, 0], 2)
    x = _ffn(x, mod, ffn_w_gu[0].astype(BF16), ffn_w_down[0].astype(BF16), ln_g[0, 1], ln_b[0, 1])

    mod = mod_all[1]
    n_main = 4 * D_MODEL
    w_main = w_in_c[0][:, :n_main].astype(BF16)
    nh = MLSTM_HEADS
    wg4 = w_in_c[0][:, n_main:].reshape(d, N_GATES, nh)
    bg4 = b_gates_c[0].reshape(1, N_GATES, nh)
    lane_pad = ((0, 0), (0, LANES - 2 * nh))

    def gate_lanes(a):
        return jnp.concatenate([jnp.pad(jnp.concatenate([a[:, 0], a[:, 2]], axis=-1), lane_pad),
                                jnp.pad(jnp.concatenate([a[:, 1], a[:, 3]], axis=-1), lane_pad)], axis=-1)

    proj, gates = _inproj_c(x, mod, w_main, gate_lanes(wg4).astype(BF16), gate_lanes(bg4))
    mix_c, new_c, new_n, new_m = _mlstm(proj, gates, mlstm_norm_g[0], gl, n_ctx, seq_ctx, 4, emit_state=True)
    (mix_l,) = _mlstm(proj, gates, mlstm_norm_g[0], 0, n_lat, seq_lat, 4,
                      init=(state_C[:, 0], state_n[:, 0], state_m[:, 0]))
    x = _outproj([(mix_l, mix_c)], [w_out_c[0].astype(BF16)], x, mod, ln_g[1, 0], ln_b[1, 0], 2)
    y_sample, y_ctx = _moe3(x, mod, router_w[0], router_b[0], moe_w_gu[0].astype(BF16),
                            moe_w_down[0].astype(BF16), ln_g[1, 1], ln_b[1, 1], gl)
    y_prompt = y_ctx.reshape(n_ctx, seq_ctx, d)
    new_m = new_m[:, 0, 0, :2 * MLSTM_HEADS].reshape(n_ctx, 2, MLSTM_HEADS)
    return (y_prompt, y_sample, new_k, new_v, new_c[:, None], new_n[..., 0][:, None], new_m[:, None])
```

```python
import functools
import math

import jax
import jax.numpy as jnp
from jax import lax
from jax.experimental import pallas as pl
from jax.experimental.pallas import tpu as pltpu

F32 = jnp.float32
BF16 = jnp.bfloat16

D_MODEL = 1024
GRID_W = 64
ROPE_BASE = 10000.0
DIFF_HEADS = 4
DIFF_DH = 64
DIFF_WIDTH = DIFF_HEADS * 2 * DIFF_DH
POOL_GROUPS = 4
POOL_GC = 128
POOL_WIDTH = POOL_GROUPS * POOL_GC
POOL_WINDOWS = (2, 4, 8, 16)
W_IN_A = 3 * DIFF_WIDTH + POOL_WIDTH
MLSTM_HEADS = 8
MLSTM_DH = 128
MLSTM_CHUNK = 64
N_GATES = 4
D_FF = 2816
N_EXPERTS = 8
D_FF_EXPERT = 1792
LN_EPS = 1e-5
DEPTH = 2
ALPHA = (2.0 * DEPTH) ** 0.25

LANES = 128
FF_CHUNK = 256
VMEM_LIMIT = 56 * 1024 * 1024


def _cparams(sem, **kw):
    return pltpu.CompilerParams(dimension_semantics=sem, vmem_limit_bytes=VMEM_LIMIT, **kw)


def _resident(shape, index_map):
    return pl.BlockSpec(shape, index_map, pipeline_mode=pl.Buffered(1))


def _layer_norm_rows(z, g, b):
    mu = jnp.mean(z, axis=-1, keepdims=True)
    zc = z - mu
    var = jnp.mean(zc * zc, axis=-1, keepdims=True)
    return zc * lax.rsqrt(var + LN_EPS) * g + b


def _sigmoid(x):
    return 1.0 / (1.0 + jnp.exp(-x))


SUBLANES = 8
assert D_MODEL == SUBLANES * LANES


def _store_token_tiles(ref, x, *lead):
    for j in range(SUBLANES):
        ref[lead + (slice(None), j, slice(None))] = x[:, j * LANES:(j + 1) * LANES]


def _load_token_tiles(ref, *lead):
    return jnp.concatenate([ref[lead + (slice(None), j, slice(None))] for j in range(SUBLANES)], axis=-1)


def _split_bf16(x):
    hi = x.astype(BF16)
    lo = (x - hi.astype(F32)).astype(BF16)
    return hi, lo


def _mod_kernel(c_ref, w_ref, b_ref, o_ref):
    c = c_ref[...]
    h = (c * _sigmoid(c)).astype(BF16)
    o_ref[...] = jnp.dot(h, w_ref[...].astype(BF16), preferred_element_type=F32) + b_ref[...]


def _modulation(cvec, ada_w, ada_b):
    depth, d, n = ada_w.shape
    g = cvec.shape[0]
    tn = 1536
    return pl.pallas_call(
        _mod_kernel,
        out_shape=jax.ShapeDtypeStruct((depth, g, n), F32),
        grid=(depth, n // tn),
        in_specs=[
            pl.BlockSpec((g, d), lambda l, j: (0, 0)),
            pl.BlockSpec((None, d, tn), lambda l, j: (l, 0, j)),
            pl.BlockSpec((None, 1, tn), lambda l, j: (l, 0, j)),
        ],
        out_specs=pl.BlockSpec((None, g, tn), lambda l, j: (l, 0, j)),
        compiler_params=_cparams(("arbitrary", "arbitrary")),
        name="modulation",
    )(cvec, ada_w, ada_b.reshape(depth, 1, n))


def _rot_half16(x):
    lane = lax.broadcasted_iota(jnp.int32, x.shape, 1)
    return jnp.where((lane % 32) < 16, pltpu.roll(x, LANES - 16, 1), pltpu.roll(x, 16, 1))


def _two_stream_specs(tm, d, gl, ctx_first_group=0):
    return [pl.BlockSpec((None, tm, d), lambda gi, ti: (jnp.minimum(gi, gl - 1), jnp.where(gi < gl, ti, 0), 0)),
            pl.BlockSpec((None, tm, d), lambda gi, ti: (ctx_first_group + jnp.maximum(gi - gl, 0),
                                                        jnp.where(gi < gl, 0, ti), 0))]


def _inproj_a_kernel(xl_ref, xc_ref, mod_ref, w_ref, cos_ref, sin_ref, o_ref, *, n_latent_groups):
    x = jnp.where(pl.program_id(0) < n_latent_groups, xl_ref[...], xc_ref[...])
    h = x * (1.0 + mod_ref[1:2, :]) + mod_ref[0:1, :]
    p = jnp.dot(h.astype(BF16), w_ref[...], preferred_element_type=F32)
    cos = cos_ref[...]
    sin = sin_ref[...]
    n_rope = 2 * DIFF_WIDTH // LANES
    for j in range(n_rope):
        blk = p[:, j * LANES:(j + 1) * LANES]
        o_ref[:, j * LANES:(j + 1) * LANES] = blk * cos + _rot_half16(blk) * sin
    o_ref[:, n_rope * LANES:] = p[:, n_rope * LANES:]


def _inproj_a(x_lat, x_ctx, mod, w_bf, cos_t, sin_t, tm=512):
    n_latent_groups, s, d = x_lat.shape
    g = n_latent_groups + x_ctx.shape[0]
    n = w_bf.shape[1]

    def table_map(gi, ti):
        return (jnp.where(gi >= n_latent_groups, 1, 0), ti, 0)

    return pl.pallas_call(
        functools.partial(_inproj_a_kernel, n_latent_groups=n_latent_groups),
        out_shape=jax.ShapeDtypeStruct((g, s, n), F32),
        grid=(g, s // tm),
        in_specs=_two_stream_specs(tm, d, n_latent_groups) + [
            pl.BlockSpec((None, 6, d), lambda gi, ti: (gi, 0, 0)),
            _resident((d, n), lambda gi, ti: (0, 0)),
            pl.BlockSpec((None, tm, LANES), table_map),
            pl.BlockSpec((None, tm, LANES), table_map),
        ],
        out_specs=pl.BlockSpec((None, tm, n), lambda gi, ti: (gi, ti, 0)),
        compiler_params=_cparams(("arbitrary", "arbitrary")),
        name="inproj_a",
    )(x_lat, x_ctx, mod, w_bf, cos_t, sin_t)


def _rope_tables(n_tokens):
    rows = n_tokens // GRID_W
    row_pos = jnp.repeat(jnp.arange(rows), GRID_W).astype(F32)
    col_pos = jnp.tile(jnp.arange(GRID_W), rows).astype(F32)
    n_freq = DIFF_DH // 4
    inv_freq = jnp.power(ROPE_BASE, -jnp.arange(n_freq, dtype=F32) / n_freq)
    ang = jnp.stack([row_pos[:, None] * inv_freq, col_pos[:, None] * inv_freq], axis=1)
    cos, sin = jnp.cos(ang), jnp.sin(ang)
    cos64 = jnp.concatenate([cos[:, 0], cos[:, 0], cos[:, 1], cos[:, 1]], axis=-1)
    sin64 = jnp.concatenate([-sin[:, 0], sin[:, 0], -sin[:, 1], sin[:, 1]], axis=-1)
    cos_l = jnp.tile(cos64, (1, LANES // DIFF_DH))
    sin_l = jnp.tile(sin64, (1, LANES // DIFF_DH))
    cos_t = jnp.stack([cos_l, jnp.ones_like(cos_l)])
    sin_t = jnp.stack([sin_l, jnp.zeros_like(sin_l)])
    return cos_t, sin_t


def _diff_attn_kernel(*refs, n_pieces, lam_init, emit_kv):
    lam_ref, ng_ref, q_ref = refs[:3]
    kv_refs = refs[3:3 + 2 * n_pieces]
    o_ref = refs[3 + 2 * n_pieces]

    lp = lam_ref[...]
    lam = (jnp.exp(jnp.sum(lp[0:1] * lp[1:2], axis=-1, keepdims=True))
           - jnp.exp(jnp.sum(lp[2:3] * lp[3:4], axis=-1, keepdims=True)) + lam_init)

    q = q_ref[...] * (DIFF_DH ** -0.5)
    lane = lax.broadcasted_iota(jnp.int32, q.shape, 1)
    q1 = jnp.where(lane < DIFF_DH, q, 0.0).astype(BF16)
    q2 = jnp.where(lane >= DIFF_DH, q, 0.0).astype(BF16)

    nt = (((1,), (1,)), ((), ()))
    s1, s2, vs = [], [], []
    for i in range(n_pieces):
        kb = kv_refs[2 * i][...].astype(BF16)
        vs.append(kv_refs[2 * i + 1][...].astype(BF16))
        s1.append(lax.dot_general(q1, kb, nt, preferred_element_type=F32))
        s2.append(lax.dot_general(q2, kb, nt, preferred_element_type=F32))

    def softmax_pieces(ss):
        m = functools.reduce(jnp.maximum, [jnp.max(s, axis=-1, keepdims=True) for s in ss])
        es = [jnp.exp(s - m) for s in ss]
        l = functools.reduce(jnp.add, [jnp.sum(e, axis=-1, keepdims=True) for e in es])
        return [e / l for e in es]

    p1 = softmax_pieces(s1)
    p2 = softmax_pieces(s2)
    o = None
    for i in range(n_pieces):
        a = (p1[i] - lam * p2[i]).astype(BF16)
        t = jnp.dot(a, vs[i], preferred_element_type=F32)
        o = t if o is None else o + t
    o = o * lax.rsqrt(jnp.mean(o * o, axis=-1, keepdims=True) + LN_EPS)
    o_ref[...] = (o * ng_ref[...] * (1.0 - lam_init)).astype(o_ref.dtype)
    if emit_kv:
        ko_ref, vo_ref = refs[4 + 2 * n_pieces:]
        ko_ref[...] = kv_refs[0][...]
        vo_ref[...] = kv_refs[1][...]


def _attn_context(proj, lam_p, norm_g, n_latent_groups, n_seq, seq, lam_init):
    g, s, _ = proj.shape
    per_group = s // seq
    blk = (None, seq, LANES)

    def tok_map(col0):
        return lambda b, h: (n_latent_groups + b // per_group, b % per_group, col0 + h)

    cache_shape = jax.ShapeDtypeStruct((n_seq, 1, DIFF_HEADS, seq, LANES), F32)
    cache_spec = pl.BlockSpec((None, None, None, seq, LANES), lambda b, h: (b, 0, h, 0, 0))
    out_spec = pl.BlockSpec(blk, lambda b, h: (b // per_group, b % per_group, h))
    return pl.pallas_call(
        functools.partial(_diff_attn_kernel, n_pieces=1, lam_init=lam_init, emit_kv=True),
        out_shape=(jax.ShapeDtypeStruct((g - n_latent_groups, s, DIFF_WIDTH), BF16), cache_shape, cache_shape),
        grid=(n_seq, DIFF_HEADS),
        in_specs=[
            pl.BlockSpec((4, DIFF_DH), lambda b, h: (0, 0)),
            pl.BlockSpec((1, LANES), lambda b, h: (0, 0)),
            pl.BlockSpec(blk, tok_map(0)),
            pl.BlockSpec(blk, tok_map(DIFF_HEADS)),
            pl.BlockSpec(blk, tok_map(2 * DIFF_HEADS)),
        ],
        out_specs=(out_spec, cache_spec, cache_spec),
        compiler_params=_cparams(("arbitrary", "arbitrary")),
        name="attn_context",
    )(lam_p, norm_g, proj, proj, proj)


def _attn_latent(proj, cache_k, cache_v, lam_p, norm_g, n_latent_groups, lam_init, tq=256):
    g, s, _ = proj.shape
    past = cache_k.shape[3]
    cache_spec = pl.BlockSpec((None, None, None, past, LANES), lambda b, h, qi: (b, 0, h, 0, 0))
    return pl.pallas_call(
        functools.partial(_diff_attn_kernel, n_pieces=2, lam_init=lam_init, emit_kv=False),
        out_shape=jax.ShapeDtypeStruct((n_latent_groups, s, DIFF_WIDTH), BF16),
        grid=(n_latent_groups, DIFF_HEADS, s // tq),
        in_specs=[
            pl.BlockSpec((4, DIFF_DH), lambda b, h, qi: (0, 0)),
            pl.BlockSpec((1, LANES), lambda b, h, qi: (0, 0)),
            pl.BlockSpec((None, tq, LANES), lambda b, h, qi: (b, qi, h)),
            cache_spec,
            cache_spec,
            pl.BlockSpec((None, s, LANES), lambda b, h, qi: (b, 0, DIFF_HEADS + h)),
            pl.BlockSpec((None, s, LANES), lambda b, h, qi: (b, 0, 2 * DIFF_HEADS + h)),
        ],
        out_specs=pl.BlockSpec((None, tq, LANES), lambda b, h, qi: (b, qi, h)),
        compiler_params=_cparams(("arbitrary", "arbitrary", "arbitrary")),
        name="attn_latent",
    )(lam_p, norm_g, proj, cache_k, cache_v, proj, proj)


def _pool_kernel(p_ref, w_ref, sc_ref, o_ref, band_ref, *, seq):
    @pl.when((pl.program_id(0) == 0) & (pl.program_id(1) == 0))
    def _():
        t = lax.broadcasted_iota(jnp.int32, (seq, seq), 0)
        s_ = lax.broadcasted_iota(jnp.int32, (seq, seq), 1)
        for gi, w in enumerate(POOL_WINDOWS):
            inside = (s_ >= t - w // 2) & (s_ <= t + w // 2 - 1)
            band_ref[gi] = jnp.where(inside, 1.0, 0.0).astype(BF16)

    tcol = lax.broadcasted_iota(jnp.int32, (seq, 1), 0)
    for gi, w in enumerate(POOL_WINDOWS):
        u = p_ref[:, gi * POOL_GC:(gi + 1) * POOL_GC]
        hi, lo = _split_bf16(u)
        band = band_ref[gi]
        win = (jnp.dot(band, hi, preferred_element_type=F32)
               + jnp.dot(band, lo, preferred_element_type=F32))
        cnt = (jnp.minimum(tcol + (w // 2 - 1), seq - 1) - jnp.maximum(tcol - w // 2, 0) + 1).astype(F32)
        pooled = win / cnt - u
        mixed = jnp.dot(pooled.astype(BF16), w_ref[gi].astype(BF16), preferred_element_type=F32)
        o_ref[:, gi * POOL_GC:(gi + 1) * POOL_GC] = (
            mixed * sc_ref[:, gi * POOL_GC:(gi + 1) * POOL_GC]).astype(o_ref.dtype)


def _pool(proj, pool_w, pool_scale, g0, n_groups, seq):
    g, s, _ = proj.shape
    col = 3 * DIFF_WIDTH // POOL_WIDTH
    return pl.pallas_call(
        functools.partial(_pool_kernel, seq=seq),
        out_shape=jax.ShapeDtypeStruct((n_groups, s, POOL_WIDTH), BF16),
        grid=(n_groups, s // seq),
        in_specs=[
            pl.BlockSpec((None, seq, POOL_WIDTH), lambda gi, ti: (g0 + gi, ti, col)),
            pl.BlockSpec((POOL_GROUPS, POOL_GC, POOL_GC), lambda gi, ti: (0, 0, 0)),
            pl.BlockSpec((1, POOL_WIDTH), lambda gi, ti: (0, 0)),
        ],
        out_specs=pl.BlockSpec((None, seq, POOL_WIDTH), lambda gi, ti: (gi, ti, 0)),
        scratch_shapes=[pltpu.VMEM((POOL_GROUPS, seq, seq), BF16)],
        compiler_params=_cparams(("arbitrary", "arbitrary")),
        name="pool_%d" % seq,
    )(proj, pool_w, pool_scale.reshape(1, POOL_WIDTH))


def _outproj_kernel(*refs, n_in, gate_row, n_latent_groups):
    a_refs = refs[:2 * n_in]
    w_refs = refs[2 * n_in:3 * n_in]
    xl_ref, xc_ref, mod_ref, g_ref, b_ref, o_ref = refs[3 * n_in:]
    is_latent = pl.program_id(0) < n_latent_groups
    acc = None
    for i, w_ref in enumerate(w_refs):
        a = jnp.where(is_latent, a_refs[2 * i][...], a_refs[2 * i + 1][...])
        t = jnp.dot(a, w_ref[...], preferred_element_type=F32)
        acc = t if acc is None else acc + t
    x = jnp.where(is_latent, xl_ref[...], xc_ref[...])
    z = ALPHA * x + mod_ref[gate_row:gate_row + 1, :] * acc
    o_ref[...] = _layer_norm_rows(z, g_ref[...], b_ref[...])


def _outproj(acts, weights, x_lat, x_ctx, mod, ln_g, ln_b, gate_row, tm=512):
    gl = acts[0][0].shape[0]
    _, s, d = x_lat.shape
    if x_ctx is None:
        g = x_lat.shape[0]
        x_ctx, x_specs = x_lat, _two_stream_specs(tm, d, gl, gl)
    else:
        g = gl + x_ctx.shape[0]
        x_specs = _two_stream_specs(tm, d, gl)
    n_in = len(acts)
    in_specs = []
    flat_acts = []
    for a_lat, a_ctx in acts:
        in_specs += _two_stream_specs(tm, a_lat.shape[-1], gl)
        flat_acts += [a_lat, a_ctx]
    in_specs += [_resident(w.shape, lambda gi, ti: (0, 0)) for w in weights]
    in_specs += x_specs
    in_specs += [
        pl.BlockSpec((None, 6, d), lambda gi, ti: (gi, 0, 0)),
        pl.BlockSpec((1, d), lambda gi, ti: (0, 0)),
        pl.BlockSpec((1, d), lambda gi, ti: (0, 0)),
    ]
    return pl.pallas_call(
        functools.partial(_outproj_kernel, n_in=n_in, gate_row=gate_row, n_latent_groups=gl),
        out_shape=jax.ShapeDtypeStruct((g, s, d), F32),
        grid=(g, s // tm),
        in_specs=in_specs,
        out_specs=pl.BlockSpec((None, tm, d), lambda gi, ti: (gi, ti, 0)),
        compiler_params=_cparams(("arbitrary", "arbitrary")),
        name="outproj",
    )(*flat_acts, *weights, x_lat, x_ctx, mod, ln_g.reshape(1, d), ln_b.reshape(1, d))


def _swiglu_chunks(h_bf, wgu_ref, wd_ref, d_ff):
    acc = None
    for j in range(d_ff // FF_CHUNK):
        lo = j * FF_CHUNK
        gate = jnp.dot(h_bf, wgu_ref[:, lo:lo + FF_CHUNK], preferred_element_type=F32)
        up = jnp.dot(h_bf, wgu_ref[:, d_ff + lo:d_ff + lo + FF_CHUNK], preferred_element_type=F32)
        act = (gate * _sigmoid(gate) * up).astype(BF16)
        t = jnp.dot(act, wd_ref[lo:lo + FF_CHUNK, :], preferred_element_type=F32)
        acc = t if acc is None else acc + t
    return acc


def _ffn_kernel(x_ref, mod_ref, wgu_ref, wd_ref, g_ref, b_ref, o_ref):
    x = x_ref[...]
    h = (x * (1.0 + mod_ref[4:5, :]) + mod_ref[3:4, :]).astype(BF16)
    acc = _swiglu_chunks(h, wgu_ref, wd_ref, D_FF)
    z = ALPHA * x + mod_ref[5:6, :] * acc
    o_ref[...] = _layer_norm_rows(z, g_ref[...], b_ref[...])


def _ffn(x, mod, wgu_bf, wd_bf, ln_g, ln_b, tm=512):
    g, s, d = x.shape
    return pl.pallas_call(
        _ffn_kernel,
        out_shape=jax.ShapeDtypeStruct((g, s, d), F32),
        grid=(g, s // tm),
        in_specs=[
            pl.BlockSpec((None, tm, d), lambda gi, ti: (gi, ti, 0)),
            pl.BlockSpec((None, 6, d), lambda gi, ti: (gi, 0, 0)),
            _resident(wgu_bf.shape, lambda gi, ti: (0, 0)),
            _resident(wd_bf.shape, lambda gi, ti: (0, 0)),
            pl.BlockSpec((1, d), lambda gi, ti: (0, 0)),
            pl.BlockSpec((1, d), lambda gi, ti: (0, 0)),
        ],
        out_specs=pl.BlockSpec((None, tm, d), lambda gi, ti: (gi, ti, 0)),
        compiler_params=_cparams(("arbitrary", "arbitrary")),
        name="ffn",
    )(x, mod, wgu_bf, wd_bf, ln_g.reshape(1, d), ln_b.reshape(1, d))


def _router_combine(h, rw_ref, rb_ref):
    h_hi, h_lo = _split_bf16(h)
    w_hi, w_lo = _split_bf16(rw_ref[...])
    logits = (jnp.dot(h_hi, w_hi, preferred_element_type=F32)
              + jnp.dot(h_lo, w_hi, preferred_element_type=F32)
              + jnp.dot(h_hi, w_lo, preferred_element_type=F32)) + rb_ref[...]
    lane = lax.broadcasted_iota(jnp.int32, logits.shape, 1).astype(F32)
    neg = -jnp.inf
    logits = jnp.where(lane < N_EXPERTS, logits, neg)
    m1 = jnp.max(logits, axis=-1, keepdims=True)
    i1 = jnp.min(jnp.where(logits == m1, lane, float(LANES)), axis=-1, keepdims=True)
    rest = jnp.where(lane == i1, neg, logits)
    m2 = jnp.max(rest, axis=-1, keepdims=True)
    i2 = jnp.min(jnp.where(rest == m2, lane, float(LANES)), axis=-1, keepdims=True)
    e2 = jnp.exp(m2 - m1)
    w1 = 1.0 / (1.0 + e2)
    w2 = e2 / (1.0 + e2)
    member = (lane == i1) | (lane == i2)
    return jnp.where(lane == i1, w1, 0.0) + jnp.where(lane == i2, w2, 0.0), member


MOE_BM = 256
MOE_TC = 256
MOE_TMC = 512


def _route_kernel(x_ref, mod_ref, rw_ref, rb_ref, h_ref, cw_ref, srank_ref, cnt_ref, tri_ref, run_ref):
    tm = x_ref.shape[0]

    @pl.when(pl.program_id(0) == 0)
    def _():
        r = lax.broadcasted_iota(jnp.int32, (tm, tm), 0)
        c = lax.broadcasted_iota(jnp.int32, (tm, tm), 1)
        tri_ref[...] = jnp.where(c <= r, 1.0, 0.0).astype(BF16)
        run_ref[...] = jnp.zeros_like(run_ref)

    h = x_ref[...] * (1.0 + mod_ref[4:5, :]) + mod_ref[3:4, :]
    h_ref[...] = h.astype(BF16)
    cw, member = _router_combine(h, rw_ref, rb_ref)
    cw_ref[...] = cw
    mem = jnp.where(member, 1.0, 0.0)
    rank = jnp.dot(tri_ref[...], mem.astype(BF16), preferred_element_type=F32) + run_ref[...]
    srank_ref[...] = jnp.where(member, rank, -rank)
    run_ref[...] = rank[tm - 1:tm, :]
    cnt_ref[...] = rank[tm - 1:tm, :]


def _route(x, mod, router_w, router_b, tm=512):
    g, s, d = x.shape
    n = g * s
    per = s // tm
    rw = jnp.pad(router_w, ((0, 0), (0, LANES - N_EXPERTS)))
    rb = jnp.pad(router_b, (0, LANES - N_EXPERTS)).reshape(1, LANES)
    return pl.pallas_call(
        _route_kernel,
        out_shape=(jax.ShapeDtypeStruct((n, d), BF16), jax.ShapeDtypeStruct((n, LANES), F32),
                   jax.ShapeDtypeStruct((n, LANES), F32), jax.ShapeDtypeStruct((1, LANES), F32)),
        grid=(n // tm,),
        in_specs=[
            pl.BlockSpec((None, tm, d), lambda i: (i // per, i % per, 0)),
            pl.BlockSpec((None, 6, d), lambda i: (i // per, 0, 0)),
            pl.BlockSpec((d, LANES), lambda i: (0, 0)),
            pl.BlockSpec((1, LANES), lambda i: (0, 0)),
        ],
        out_specs=(pl.BlockSpec((tm, d), lambda i: (i, 0)), pl.BlockSpec((tm, LANES), lambda i: (i, 0)),
                   pl.BlockSpec((tm, LANES), lambda i: (i, 0)), pl.BlockSpec((1, LANES), lambda i: (0, 0))),
        scratch_shapes=[pltpu.VMEM((tm, tm), BF16), pltpu.VMEM((1, LANES), F32)],
        compiler_params=_cparams(("arbitrary",)),
        name="moe_route",
    )(x, mod, rw, rb)


def _moe_plan(srank, counts, n_blocks, n_items):
    e_n = N_EXPERTS
    n = srank.shape[0]
    i32 = jnp.int32
    cnt = counts[0, :e_n].astype(i32)
    nb = (cnt + MOE_BM - 1) // MOE_BM
    nb_incl = jnp.cumsum(nb)
    gstart = nb_incl - nb
    n_used = nb_incl[-1]
    sr = srank[:, :e_n]
    rank = jnp.abs(sr).astype(i32)
    pos_tok = jnp.where(sr > 0, rank - 1 + MOE_BM * gstart[None, :], -1)
    rank_t = rank.T

    r = jnp.arange(n_blocks, dtype=i32)
    used = r < n_used
    rc = jnp.minimum(r, n_used - 1)
    e_r = jnp.minimum(jnp.sum(nb_incl[None, :] <= rc[:, None], axis=1, dtype=i32), e_n - 1)
    b = rc - gstart[e_r]
    lo = b * MOE_BM + 1
    hi = jnp.minimum((b + 1) * MOE_BM, cnt[e_r])
    def find(e, v):
        return jnp.sum(rank_t[e] < v[:, None], axis=1, dtype=i32)

    jlo = jnp.where(used, find(e_r, lo) // MOE_TC, 0)
    jhi = jnp.where(used, find(e_r, hi) // MOE_TC, -1)

    n_tiles = n // MOE_TMC
    ends = rank[MOE_TMC - 1::MOE_TMC]
    starts = jnp.concatenate([jnp.zeros((1, e_n), i32), ends[:-1]], axis=0)
    fb = gstart[None, :] + starts // MOE_BM
    lb = gstart[None, :] + (ends - 1) // MOE_BM
    n_pe = jnp.where(ends > starts, lb - fb + 1, 0).reshape(-1)
    incl = jnp.cumsum(n_pe)
    off = incl - n_pe
    total = incl[-1]
    w = jnp.arange(n_items, dtype=i32)
    valid = w < total
    wc = jnp.minimum(w, total - 1)
    p = jnp.sum(incl[None, :] <= wc[:, None], axis=1, dtype=i32)
    it_tile = p // e_n
    it_e = p % e_n
    it_blk = fb.reshape(-1)[p] + (wc - off[p])
    tile_off = jnp.concatenate([off[::e_n], total[None]])
    it_first = (wc == tile_off[it_tile]).astype(i32)
    it_last = (wc == tile_off[it_tile + 1] - 1).astype(i32)
    return (pos_tok, e_r, jlo, jhi, n_used.reshape(1),
            it_tile, it_blk, it_e, it_first, it_last, valid.astype(i32))


def _experts_kernel(be_ref, jlo_ref, jhi_ref, nused_ref, h_ref, pos_ref, wgu_ref, wd_ref, y_ref, xg_ref):
    r = pl.program_id(0)

    @pl.when(r < nused_ref[0])
    def _():
        e = be_ref[r]
        slot = r * MOE_BM + lax.broadcasted_iota(jnp.int32, (MOE_BM, MOE_TC), 0)
        xg_ref[...] = jnp.zeros_like(xg_ref)

        def chunk(j, carry):
            onehot = jnp.where(pos_ref[e, pl.ds(j, 1), :] == slot, 1.0, 0.0).astype(BF16)
            rows = h_ref[pl.ds(pl.multiple_of(j * MOE_TC, MOE_TC), MOE_TC), :]
            xg_ref[...] += jnp.dot(onehot, rows, preferred_element_type=F32)
            return carry

        lax.fori_loop(jlo_ref[r], jhi_ref[r] + 1, chunk, 0)
        y_ref[...] = _swiglu_chunks(xg_ref[...].astype(BF16), wgu_ref, wd_ref, D_FF_EXPERT)

    @pl.when(r >= nused_ref[0])
    def _():
        y_ref[...] = jnp.zeros_like(y_ref)


def _experts(h, pos_row, plan, wgu_bf, wd_bf, n_blocks):
    n, d = h.shape
    e_r, jlo, jhi, n_used = plan
    grid_spec = pltpu.PrefetchScalarGridSpec(
        num_scalar_prefetch=4,
        grid=(n_blocks,),
        in_specs=[
            _resident((n, d), lambda r, be, lo, hi, nu: (0, 0)),
            _resident(pos_row.shape, lambda r, be, lo, hi, nu: (0, 0, 0)),
            pl.BlockSpec((None, d, 2 * D_FF_EXPERT), lambda r, be, lo, hi, nu: (be[r], 0, 0),
                         pipeline_mode=pl.Buffered(1)),
            pl.BlockSpec((None, D_FF_EXPERT, d), lambda r, be, lo, hi, nu: (be[r], 0, 0)),
        ],
        out_specs=pl.BlockSpec((MOE_BM, d), lambda r, be, lo, hi, nu: (r, 0)),
        scratch_shapes=[pltpu.VMEM((MOE_BM, d), F32)],
    )
    return pl.pallas_call(
        _experts_kernel,
        out_shape=jax.ShapeDtypeStruct((n_blocks * MOE_BM, d), F32),
        grid_spec=grid_spec,
        compiler_params=_cparams(("arbitrary",)),
        name="moe_experts",
    )(e_r, jlo, jhi, n_used, h, pos_row, wgu_bf, wd_bf)


def _combine_kernel(tile_ref, blk_ref, e_ref, first_ref, last_ref, valid_ref,
                    y_ref, pos_ref, cw_ref, x_ref, mod_ref, g_ref, b_ref, o_ref, acc_ref):
    w = pl.program_id(0)

    @pl.when(valid_ref[w] == 1)
    def _():
        @pl.when(first_ref[w] == 1)
        def _():
            acc_ref[...] = jnp.zeros_like(acc_ref)

        e = e_ref[w]
        lane = lax.broadcasted_iota(jnp.int32, (MOE_TMC, LANES), 1)
        pos_e = jnp.sum(jnp.where(lane == e, pos_ref[...], 0.0), axis=-1, keepdims=True)
        cw_e = jnp.sum(jnp.where(lane == e, cw_ref[...], 0.0), axis=-1, keepdims=True)
        slot = (blk_ref[w] * MOE_BM + lax.broadcasted_iota(jnp.int32, (MOE_TMC, MOE_BM), 1)).astype(F32)
        onehot = jnp.where(pos_e == slot, 1.0, 0.0).astype(BF16)
        y_hi, y_lo = _split_bf16(y_ref[...])
        part = (jnp.dot(onehot, y_hi, preferred_element_type=F32)
                + jnp.dot(onehot, y_lo, preferred_element_type=F32))
        acc_ref[...] += cw_e * part

        @pl.when(last_ref[w] == 1)
        def _():
            z = ALPHA * x_ref[...] + mod_ref[5:6, :] * acc_ref[...]
            o_ref[...] = _layer_norm_rows(z, g_ref[...], b_ref[...])


def _combine(y, pos_tok_f, cw, x, mod, ln_g, ln_b, items, n_items):
    g, s, d = x.shape
    per = s // MOE_TMC

    def tok2(w, tile, *_):
        return (tile[w], 0)

    def tok3(w, tile, *_):
        return (tile[w] // per, tile[w] % per, 0)

    grid_spec = pltpu.PrefetchScalarGridSpec(
        num_scalar_prefetch=6,
        grid=(n_items,),
        in_specs=[
            pl.BlockSpec((MOE_BM, d), lambda w, tile, blk, *_: (blk[w], 0)),
            pl.BlockSpec((MOE_TMC, LANES), tok2),
            pl.BlockSpec((MOE_TMC, LANES), tok2),
            pl.BlockSpec((None, MOE_TMC, d), tok3),
            pl.BlockSpec((None, 6, d), lambda w, tile, *_: (tile[w] // per, 0, 0)),
            pl.BlockSpec((1, d), lambda w, *_: (0, 0)),
            pl.BlockSpec((1, d), lambda w, *_: (0, 0)),
        ],
        out_specs=pl.BlockSpec((None, MOE_TMC, d), tok3),
        scratch_shapes=[pltpu.VMEM((MOE_TMC, d), F32)],
    )
    return pl.pallas_call(
        _combine_kernel,
        out_shape=jax.ShapeDtypeStruct((g, s, d), F32),
        grid_spec=grid_spec,
        compiler_params=_cparams(("arbitrary",)),
        name="moe_combine",
    )(*items, y, pos_tok_f, cw, x, mod, ln_g.reshape(1, d), ln_b.reshape(1, d))


def _moe(x, mod, router_w, router_b, wgu_bf, wd_bf, ln_g, ln_b):
    g, s, d = x.shape
    n = g * s
    n_slots = 2 * n
    n_blocks = n_slots // MOE_BM + N_EXPERTS
    n_items = n_blocks + N_EXPERTS * (n // MOE_TMC)
    h, cw, srank, counts = _route(x, mod, router_w, router_b)
    plan = _moe_plan(srank, counts, n_blocks, n_items)
    pos_tok = plan[0]
    pos_row = pos_tok.T.reshape(N_EXPERTS, n // MOE_TC, MOE_TC)
    y = _experts(h, pos_row, plan[1:5], wgu_bf, wd_bf, n_blocks)
    pos_tok_f = jnp.pad(pos_tok.astype(F32), ((0, 0), (0, LANES - N_EXPERTS)), constant_values=-1.0)
    return _combine(y, pos_tok_f, cw, x, mod, ln_g, ln_b, plan[5:], n_items)


META_LANES = LANES


def _router_top2(h, rw_ref, rb_ref):
    h_hi, h_lo = _split_bf16(h)
    w_hi, w_lo = _split_bf16(rw_ref[...])
    logits = (jnp.dot(h_hi, w_hi, preferred_element_type=F32)
              + jnp.dot(h_lo, w_hi, preferred_element_type=F32)
              + jnp.dot(h_hi, w_lo, preferred_element_type=F32)) + rb_ref[...]
    lane = lax.broadcasted_iota(jnp.int32, logits.shape, 1).astype(F32)
    neg = -jnp.inf
    logits = jnp.where(lane < N_EXPERTS, logits, neg)
    m1 = jnp.max(logits, axis=-1, keepdims=True)
    i1 = jnp.min(jnp.where(logits == m1, lane, float(LANES)), axis=-1, keepdims=True)
    rest = jnp.where(lane == i1, neg, logits)
    m2 = jnp.max(rest, axis=-1, keepdims=True)
    i2 = jnp.min(jnp.where(rest == m2, lane, float(LANES)), axis=-1, keepdims=True)
    e2 = jnp.exp(m2 - m1)
    return lane, i1, i2, 1.0 / (1.0 + e2), e2 / (1.0 + e2)


def _route2_kernel(x_ref, mod_ref, rw_ref, rb_ref, h_ref, wk_ref, srank_ref, cnt_ref, tri_ref, run_ref):
    tm, d = x_ref.shape

    @pl.when(pl.program_id(0) == 0)
    def _():
        r = lax.broadcasted_iota(jnp.int32, (tm, tm), 0)
        c = lax.broadcasted_iota(jnp.int32, (tm, tm), 1)
        tri_ref[...] = jnp.where(c <= r, 1.0, 0.0).astype(BF16)
        run_ref[...] = jnp.zeros_like(run_ref)

    h = x_ref[...] * (1.0 + mod_ref[4:5, :]) + mod_ref[3:4, :]
    h_ref[:, 0:d] = h.astype(BF16)
    lane, i1, i2, w1, w2 = _router_top2(h, rw_ref, rb_ref)
    first_is_low = i1 < i2
    e_hi = jnp.where(first_is_low, i2, i1)
    wk_ref[...] = jnp.where(lane == 0.0, jnp.where(first_is_low, w1, w2),
                            jnp.where(lane == 1.0, jnp.where(first_is_low, w2, w1), 0.0))
    tok = (pl.program_id(0) * tm + lax.broadcasted_iota(jnp.int32, (tm, META_LANES), 0))
    meta = jnp.where(lane == 0.0, (tok // 128).astype(F32),
                     jnp.where(lane == 1.0, (tok % 128).astype(F32),
                               jnp.where(lane == 2.0, e_hi, jnp.where(lane == 3.0, 1.0, 0.0))))
    h_ref[:, d:d + META_LANES] = meta.astype(BF16)

    member = (lane == i1) | (lane == i2)
    mem = jnp.where(member, 1.0, 0.0)
    rank = jnp.dot(tri_ref[...], mem.astype(BF16), preferred_element_type=F32) + run_ref[...]
    srank_ref[...] = jnp.where(member, rank, -rank)
    run_ref[...] = rank[tm - 1:tm, :]
    cnt_ref[...] = rank[tm - 1:tm, :]


def _route2(x, mod, router_w, router_b, tm=512):
    g, s, d = x.shape
    n = g * s
    assert n <= 128 * 256
    per = s // tm
    rw = jnp.pad(router_w, ((0, 0), (0, LANES - N_EXPERTS)))
    rb = jnp.pad(router_b, (0, LANES - N_EXPERTS)).reshape(1, LANES)
    return pl.pallas_call(
        _route2_kernel,
        out_shape=(jax.ShapeDtypeStruct((n, d + META_LANES), BF16), jax.ShapeDtypeStruct((n, LANES), F32),
                   jax.ShapeDtypeStruct((n, LANES), F32), jax.ShapeDtypeStruct((1, LANES), F32)),
        grid=(n // tm,),
        in_specs=[
            pl.BlockSpec((None, tm, d), lambda i: (i // per, i % per, 0)),
            pl.BlockSpec((None, 6, d), lambda i: (i // per, 0, 0)),
            pl.BlockSpec((d, LANES), lambda i: (0, 0)),
            pl.BlockSpec((1, LANES), lambda i: (0, 0)),
        ],
        out_specs=(pl.BlockSpec((tm, d + META_LANES), lambda i: (i, 0)), pl.BlockSpec((tm, LANES), lambda i: (i, 0)),
                   pl.BlockSpec((tm, LANES), lambda i: (i, 0)), pl.BlockSpec((1, LANES), lambda i: (0, 0))),
        scratch_shapes=[pltpu.VMEM((tm, tm), BF16), pltpu.VMEM((1, LANES), F32)],
        compiler_params=_cparams(("arbitrary",)),
        name="moe_route",
    )(x, mod, rw, rb)


def _moe_plan2(srank, counts, n_blocks):
    e_n = N_EXPERTS
    i32 = jnp.int32
    cnt = counts[0, :e_n].astype(i32)
    nb = (cnt + MOE_BM - 1) // MOE_BM
    nb_incl = jnp.cumsum(nb)
    gstart = nb_incl - nb
    n_used = nb_incl[-1]
    sr = srank[:, :e_n]
    rank = jnp.abs(sr).astype(i32)
    pos_tok = jnp.where(sr > 0, rank - 1 + MOE_BM * gstart[None, :], -1)
    chunk_end = rank[MOE_TC - 1::MOE_TC].T

    r = jnp.arange(n_blocks, dtype=i32)
    used = r < n_used
    rc = jnp.minimum(r, n_used - 1)
    e_r = jnp.minimum(jnp.sum(nb_incl[None, :] <= rc[:, None], axis=1, dtype=i32), e_n - 1)
    b = rc - gstart[e_r]
    lo = b * MOE_BM + 1
    hi = jnp.minimum((b + 1) * MOE_BM, cnt[e_r])
    ends_r = chunk_end[e_r]
    jlo = jnp.where(used, jnp.sum(ends_r < lo[:, None], axis=1, dtype=i32), 0)
    jhi = jnp.where(used, jnp.sum(ends_r < hi[:, None], axis=1, dtype=i32), -1)
    return pos_tok, e_r, jlo, jhi, n_used.reshape(1)


def _experts2_kernel(be_ref, jlo_ref, jhi_ref, nused_ref, h_ref, pos_ref, wgu_ref, wd_ref, y_ref, meta_ref, xg_ref):
    r = pl.program_id(0)
    d = y_ref.shape[1]

    @pl.when(r < nused_ref[0])
    def _():
        e = be_ref[r]
        slot = r * MOE_BM + lax.broadcasted_iota(jnp.int32, (MOE_BM, MOE_TC), 0)
        xg_ref[...] = jnp.zeros_like(xg_ref)

        def chunk(j, carry):
            onehot = jnp.where(pos_ref[e, pl.ds(j, 1), :] == slot, 1.0, 0.0).astype(BF16)
            rows = h_ref[pl.ds(pl.multiple_of(j * MOE_TC, MOE_TC), MOE_TC), :]
            xg_ref[...] += jnp.dot(onehot, rows, preferred_element_type=F32)
            return carry

        lax.fori_loop(jlo_ref[r], jhi_ref[r] + 1, chunk, 0)
        meta_ref[...] = xg_ref[:, d:d + META_LANES]
        y_ref[...] = _swiglu_chunks(xg_ref[:, 0:d].astype(BF16), wgu_ref, wd_ref, D_FF_EXPERT)

    @pl.when(r >= nused_ref[0])
    def _():
        y_ref[...] = jnp.zeros_like(y_ref)
        meta_ref[...] = jnp.zeros_like(meta_ref)


def _experts2(h, pos_row, plan, wgu_bf, wd_bf, n_blocks):
    n, dx = h.shape
    d = dx - META_LANES
    e_r, jlo, jhi, n_used = plan
    grid_spec = pltpu.PrefetchScalarGridSpec(
        num_scalar_prefetch=4,
        grid=(n_blocks,),
        in_specs=[
            _resident((n, dx), lambda r, be, lo, hi, nu: (0, 0)),
            _resident(pos_row.shape, lambda r, be, lo, hi, nu: (0, 0, 0)),
            pl.BlockSpec((None, d, 2 * D_FF_EXPERT), lambda r, be, lo, hi, nu: (be[r], 0, 0),
                         pipeline_mode=pl.Buffered(1)),
            pl.BlockSpec((None, D_FF_EXPERT, d), lambda r, be, lo, hi, nu: (be[r], 0, 0),
                         pipeline_mode=pl.Buffered(1)),
        ],
        out_specs=(pl.BlockSpec((MOE_BM, d), lambda r, be, lo, hi, nu: (r, 0)),
                   pl.BlockSpec((MOE_BM, META_LANES), lambda r, be, lo, hi, nu: (r, 0))),
        scratch_shapes=[pltpu.VMEM((MOE_BM, dx), F32)],
    )
    return pl.pallas_call(
        _experts2_kernel,
        out_shape=(jax.ShapeDtypeStruct((n_blocks * MOE_BM, d), F32),
                   jax.ShapeDtypeStruct((n_blocks * MOE_BM, META_LANES), F32)),
        grid_spec=grid_spec,
        compiler_params=_cparams(("arbitrary",)),
        name="moe_experts",
    )(e_r, jlo, jhi, n_used, h, pos_row, wgu_bf, wd_bf)


def _scatter_kernel(dest_ref, y_ref, o_ref, ybuf_ref, sem):
    r = pl.program_id(0)
    n_steps = pl.num_programs(0)
    buf = r % 2

    def wait_block(b):
        pltpu.make_async_copy(ybuf_ref.at[b], o_ref.at[pl.ds(0, MOE_BM), :], sem.at[b]).wait()

    @pl.when(r >= 2)
    def _():
        wait_block(buf)

    ybuf_ref[buf] = y_ref[...]

    def issue(i, carry):
        pltpu.make_async_copy(ybuf_ref.at[buf, pl.ds(i, 1), :], o_ref.at[pl.ds(dest_ref[0, i], 1), :],
                              sem.at[buf]).start()
        return carry

    lax.fori_loop(0, MOE_BM, issue, 0, unroll=8)

    @pl.when(r == n_steps - 1)
    def _():
        wait_block(1 - buf)
        wait_block(buf)


def _scatter_rows(y, dest, n_rows_out):
    n_slots, d = y.shape
    n_blocks = n_slots // MOE_BM
    assert n_blocks >= 2
    return pl.pallas_call(
        _scatter_kernel,
        out_shape=jax.ShapeDtypeStruct((n_rows_out, d), F32),
        grid=(n_blocks,),
        in_specs=[
            pl.BlockSpec((None, 1, MOE_BM), lambda r: (r, 0, 0), memory_space=pltpu.SMEM),
            pl.BlockSpec((MOE_BM, d), lambda r: (r, 0)),
        ],
        out_specs=pl.BlockSpec(memory_space=pl.ANY),
        scratch_shapes=[pltpu.VMEM((2, MOE_BM, d), F32), pltpu.SemaphoreType.DMA((2,))],
        compiler_params=_cparams(("arbitrary",)),
        name="moe_scatter",
    )(dest.reshape(n_blocks, 1, MOE_BM), y)


def _finish_kernel(a_ref, b_ref, wk_ref, x_ref, mod_ref, g_ref, bb_ref, ol_ref, oc_ref, *, n_latent_tiles):
    lane = lax.broadcasted_iota(jnp.int32, wk_ref.shape, 1)
    wk = wk_ref[...]
    w0 = jnp.sum(jnp.where(lane == 0, wk, 0.0), axis=-1, keepdims=True)
    w1 = jnp.sum(jnp.where(lane == 1, wk, 0.0), axis=-1, keepdims=True)
    acc = w0 * a_ref[...] + w1 * b_ref[...]
    z = ALPHA * x_ref[...] + mod_ref[5:6, :] * acc
    out = _layer_norm_rows(z, g_ref[...], bb_ref[...])
    is_latent = pl.program_id(0) < n_latent_tiles

    @pl.when(is_latent)
    def _():
        ol_ref[...] = out

    @pl.when(jnp.logical_not(is_latent))
    def _():
        oc_ref[...] = out


def _finish(rows, wk, x, mod, ln_g, ln_b, n_latent_groups, tm=512):
    g, s, d = x.shape
    n = g * s
    per = s // tm
    nt = n // tm
    nlt = n_latent_groups * per

    def lat_map(i):
        j = jnp.minimum(i, nlt - 1)
        return (j // per, j % per, 0)

    def ctx_map(i):
        j = jnp.maximum(i - nlt, 0)
        return (j // per, j % per, 0)

    return pl.pallas_call(
        functools.partial(_finish_kernel, n_latent_tiles=nlt),
        out_shape=(jax.ShapeDtypeStruct((n_latent_groups, s, d), F32),
                   jax.ShapeDtypeStruct((g - n_latent_groups, s, d), F32)),
        grid=(nt,),
        in_specs=[
            pl.BlockSpec((tm, d), lambda i: (i, 0)),
            pl.BlockSpec((tm, d), lambda i: (nt + i, 0)),
            pl.BlockSpec((tm, LANES), lambda i: (i, 0)),
            pl.BlockSpec((None, tm, d), lambda i: (i // per, i % per, 0)),
            pl.BlockSpec((None, 6, d), lambda i: (i // per, 0, 0)),
            pl.BlockSpec((1, d), lambda i: (0, 0)),
            pl.BlockSpec((1, d), lambda i: (0, 0)),
        ],
        out_specs=(pl.BlockSpec((None, tm, d), lat_map), pl.BlockSpec((None, tm, d), ctx_map)),
        compiler_params=_cparams(("arbitrary",)),
        name="moe_finish",
    )(rows, rows, wk, x, mod, ln_g.reshape(1, d), ln_b.reshape(1, d))


def _route3_kernel(x_ref, mod_ref, rw_ref, rb_ref, h_ref, meta_ref, wk_ref, srank_ref, cnt_ref, tri_ref, run_ref):
    tm, d = x_ref.shape

    @pl.when(pl.program_id(0) == 0)
    def _():
        r = lax.broadcasted_iota(jnp.int32, (tm, tm), 0)
        c = lax.broadcasted_iota(jnp.int32, (tm, tm), 1)
        tri_ref[...] = jnp.where(c <= r, 1.0, 0.0).astype(BF16)
        run_ref[...] = jnp.zeros_like(run_ref)

    h = x_ref[...] * (1.0 + mod_ref[4:5, :]) + mod_ref[3:4, :]
    _store_token_tiles(h_ref, h)
    lane, i1, i2, w1, w2 = _router_top2(h, rw_ref, rb_ref)
    first_is_low = i1 < i2
    e_hi = jnp.where(first_is_low, i2, i1)
    wk_ref[...] = jnp.where(lane == 0.0, jnp.where(first_is_low, w1, w2),
                            jnp.where(lane == 1.0, jnp.where(first_is_low, w2, w1), 0.0))
    tok = (pl.program_id(0) * tm + lax.broadcasted_iota(jnp.int32, (tm, META_LANES), 0))
    meta = jnp.where(lane == 0.0, (tok // 128).astype(F32),
                     jnp.where(lane == 1.0, (tok % 128).astype(F32),
                               jnp.where(lane == 2.0, e_hi, jnp.where(lane == 3.0, 1.0, 0.0))))
    meta_ref[...] = meta.astype(BF16)

    member = (lane == i1) | (lane == i2)
    mem = jnp.where(member, 1.0, 0.0)
    rank = jnp.dot(tri_ref[...], mem.astype(BF16), preferred_element_type=F32) + run_ref[...]
    srank_ref[...] = jnp.where(member, rank, -rank)
    run_ref[...] = rank[tm - 1:tm, :]
    cnt_ref[...] = rank[tm - 1:tm, :]


def _route3(x, mod, router_w, router_b, tm=512):
    g, s, d = x.shape
    n = g * s
    assert n <= 128 * 256
    per = s // tm
    rw = jnp.pad(router_w, ((0, 0), (0, LANES - N_EXPERTS)))
    rb = jnp.pad(router_b, (0, LANES - N_EXPERTS)).reshape(1, LANES)
    tok_lanes = lambda i: (i, 0)
    return pl.pallas_call(
        _route3_kernel,
        out_shape=(jax.ShapeDtypeStruct((n, SUBLANES, LANES), F32), jax.ShapeDtypeStruct((n, META_LANES), BF16),
                   jax.ShapeDtypeStruct((n, LANES), F32), jax.ShapeDtypeStruct((n, LANES), F32),
                   jax.ShapeDtypeStruct((1, LANES), F32)),
        grid=(n // tm,),
        in_specs=[
            pl.BlockSpec((None, tm, d), lambda i: (i // per, i % per, 0)),
            pl.BlockSpec((None, 6, d), lambda i: (i // per, 0, 0)),
            pl.BlockSpec((d, LANES), lambda i: (0, 0)),
            pl.BlockSpec((1, LANES), lambda i: (0, 0)),
        ],
        out_specs=(pl.BlockSpec((tm, SUBLANES, LANES), lambda i: (i, 0, 0)), pl.BlockSpec((tm, META_LANES), tok_lanes),
                   pl.BlockSpec((tm, LANES), tok_lanes), pl.BlockSpec((tm, LANES), tok_lanes),
                   pl.BlockSpec((1, LANES), lambda i: (0, 0))),
        scratch_shapes=[pltpu.VMEM((tm, tm), BF16), pltpu.VMEM((1, LANES), F32)],
        compiler_params=_cparams(("arbitrary",)),
        name="moe_route",
    )(x, mod, rw, rb)


def _slot_meta_kernel(be_ref, jlo_ref, jhi_ref, nused_ref, meta_ref, pos_ref, o_ref):
    r = pl.program_id(0)
    o_ref[...] = jnp.zeros_like(o_ref)

    @pl.when(r < nused_ref[0])
    def _():
        e = be_ref[r]
        slot = r * MOE_BM + lax.broadcasted_iota(jnp.int32, (MOE_BM, MOE_TC), 0)

        def chunk(j, carry):
            onehot = jnp.where(pos_ref[e, pl.ds(j, 1), :] == slot, 1.0, 0.0).astype(BF16)
            rows = meta_ref[pl.ds(pl.multiple_of(j * MOE_TC, MOE_TC), MOE_TC), :]
            o_ref[...] += jnp.dot(onehot, rows, preferred_element_type=F32)
            return carry

        lax.fori_loop(jlo_ref[r], jhi_ref[r] + 1, chunk, 0)


def _slot_meta(meta, pos_row, plan, n_blocks):
    n = meta.shape[0]
    e_r, jlo, jhi, n_used = plan
    grid_spec = pltpu.PrefetchScalarGridSpec(
        num_scalar_prefetch=4,
        grid=(n_blocks,),
        in_specs=[
            _resident((n, META_LANES), lambda r, be, lo, hi, nu: (0, 0)),
            _resident(pos_row.shape, lambda r, be, lo, hi, nu: (0, 0, 0)),
        ],
        out_specs=pl.BlockSpec((MOE_BM, META_LANES), lambda r, be, lo, hi, nu: (r, 0)),
    )
    return pl.pallas_call(
        _slot_meta_kernel,
        out_shape=jax.ShapeDtypeStruct((n_blocks * MOE_BM, META_LANES), F32),
        grid_spec=grid_spec,
        compiler_params=_cparams(("arbitrary",)),
        name="moe_slot_meta",
    )(e_r, jlo, jhi, n_used, meta, pos_row)


def _experts3_kernel(be_ref, nused_ref, src_ref, nsrc_ref, dest_ref, ldest_ref, h_ref, wgu_ref, wd_ref, o_ref,
                     xg_ref, ybuf_ref, sem_g, sem_s):
    r = pl.program_id(0)
    n_steps = pl.num_programs(0)
    cur = r % 2
    nxt = 1 - cur
    d_ff = D_FF_EXPERT

    def gather_row(i, idx_ref, buf):
        return pltpu.make_async_copy(h_ref.at[idx_ref[0, i]], xg_ref.at[buf, i], sem_g.at[buf])

    def scatter_row(i, buf):
        return pltpu.make_async_copy(ybuf_ref.at[buf, i], o_ref.at[dest_ref[0, i]], sem_s.at[buf])

    def wait_gather(buf):
        pltpu.make_async_copy(h_ref.at[pl.ds(0, MOE_BM)], xg_ref.at[buf], sem_g.at[buf]).wait()

    def wait_scatter(buf):
        pltpu.make_async_copy(ybuf_ref.at[buf], o_ref.at[pl.ds(0, MOE_BM)], sem_s.at[buf]).wait()

    zero_rows = jnp.zeros((MOE_BM, D_MODEL), F32)

    def issue_rows(lo, hi):
        for i in range(lo, hi):
            gather_row(i, nsrc_ref, nxt).start()
            scatter_row(i, nxt).start()

    @pl.when(r == 0)
    def _():
        _store_token_tiles(ybuf_ref, zero_rows, 1)

        def first(i, carry):
            gather_row(i, src_ref, 0).start()
            return carry

        lax.fori_loop(0, MOE_BM, first, 0, unroll=8)

    wait_gather(cur)

    @pl.when(r < nused_ref[0])
    def _():
        x_bf = _load_token_tiles(xg_ref, cur).astype(BF16)
        n_chunks = d_ff // FF_CHUNK
        per = -(-MOE_BM // n_chunks)
        acc = None
        for j in range(n_chunks):
            lo = j * FF_CHUNK
            gate = jnp.dot(x_bf, wgu_ref[:, lo:lo + FF_CHUNK], preferred_element_type=F32)
            up = jnp.dot(x_bf, wgu_ref[:, d_ff + lo:d_ff + lo + FF_CHUNK], preferred_element_type=F32)
            act = (gate * _sigmoid(gate) * up).astype(BF16)
            t = jnp.dot(act, wd_ref[lo:lo + FF_CHUNK, :], preferred_element_type=F32)
            acc = t if acc is None else acc + t
            issue_rows(min(j * per, MOE_BM), min((j + 1) * per, MOE_BM))

        @pl.when(r >= 1)
        def _():
            wait_scatter(cur)

        _store_token_tiles(ybuf_ref, acc, cur)

    @pl.when(r >= nused_ref[0])
    def _():
        def both(i, carry):
            gather_row(i, nsrc_ref, nxt).start()
            scatter_row(i, nxt).start()
            return carry

        lax.fori_loop(0, MOE_BM, both, 0, unroll=8)

        @pl.when(r >= 1)
        def _():
            wait_scatter(cur)

        _store_token_tiles(ybuf_ref, zero_rows, cur)

    @pl.when(r == n_steps - 1)
    def _():
        def last(i, carry):
            pltpu.make_async_copy(ybuf_ref.at[cur, i], o_ref.at[ldest_ref[0, i]], sem_s.at[cur]).start()
            return carry

        lax.fori_loop(0, MOE_BM, last, 0, unroll=8)
        wait_scatter(nxt)
        wait_scatter(cur)
        wait_gather(nxt)


def _experts3(h, src, dest, e_r, n_used, wgu_bf, wd_bf, n_blocks):
    d = D_MODEL
    last = n_blocks - 1
    tile_rows = (MOE_BM, SUBLANES, LANES)
    smem_blk = lambda f: pl.BlockSpec((None, 1, MOE_BM), f, memory_space=pltpu.SMEM)
    grid_spec = pltpu.PrefetchScalarGridSpec(
        num_scalar_prefetch=2,
        grid=(n_blocks,),
        in_specs=[
            smem_blk(lambda r, be, nu: (0, 0, 0)),
            smem_blk(lambda r, be, nu: (jnp.minimum(r + 1, last), 0, 0)),
            smem_blk(lambda r, be, nu: (r, 0, 0)),
            smem_blk(lambda r, be, nu: (n_blocks, 0, 0)),
            pl.BlockSpec(memory_space=pl.ANY),
            pl.BlockSpec((None, d, 2 * D_FF_EXPERT), lambda r, be, nu: (be[r], 0, 0)),
            pl.BlockSpec((None, D_FF_EXPERT, d), lambda r, be, nu: (be[r], 0, 0)),
        ],
        out_specs=pl.BlockSpec(memory_space=pl.ANY),
        scratch_shapes=[pltpu.VMEM((2,) + tile_rows, F32), pltpu.VMEM((2,) + tile_rows, F32),
                        pltpu.SemaphoreType.DMA((2,)), pltpu.SemaphoreType.DMA((2,))],
    )
    return pl.pallas_call(
        _experts3_kernel,
        out_shape=jax.ShapeDtypeStruct((n_blocks * MOE_BM + MOE_BM, SUBLANES, LANES), F32),
        grid_spec=grid_spec,
        compiler_params=_cparams(("arbitrary",)),
        name="moe_experts",
    )(e_r, n_used, src, src, dest, dest, h, wgu_bf, wd_bf)


def _moe3(x, mod, router_w, router_b, wgu_bf, wd_bf, ln_g, ln_b, n_latent_groups):
    g, s, d = x.shape
    n = g * s
    n_blocks = 2 * n // MOE_BM + N_EXPERTS
    n_slots = n_blocks * MOE_BM
    h, meta_tok, wk, srank, counts = _route3(x, mod, router_w, router_b)
    pos_tok, e_r, jlo, jhi, n_used = _moe_plan2(srank, counts, n_blocks)
    pos_row = pos_tok.T.reshape(N_EXPERTS, n // MOE_TC, MOE_TC)
    meta = _slot_meta(meta_tok, pos_row, (e_r, jlo, jhi, n_used), n_blocks)
    i32 = jnp.int32
    tok = (meta[:, 0] * 128.0 + meta[:, 1]).astype(i32)
    choice = (meta[:, 2] == jnp.repeat(e_r, MOE_BM).astype(F32)).astype(i32)
    unused = meta[:, 3] < 0.5
    spare = 2 * n + jnp.cumsum(unused.astype(i32)) - 1
    dest = jnp.where(unused, spare, choice * n + tok).reshape(n_blocks, 1, MOE_BM)
    placeholder = (n_slots + jnp.arange(MOE_BM, dtype=i32)).reshape(1, 1, MOE_BM)
    dest_tab = jnp.concatenate([placeholder, dest], axis=0)
    rows = _experts3(h, tok.reshape(n_blocks, 1, MOE_BM), dest_tab, e_r, n_used, wgu_bf, wd_bf, n_blocks)
    return _finish(rows, wk, x, mod, ln_g, ln_b, n_latent_groups)


def _moe2(x, mod, router_w, router_b, wgu_bf, wd_bf, ln_g, ln_b, n_latent_groups):
    g, s, d = x.shape
    n = g * s
    n_blocks = 2 * n // MOE_BM + N_EXPERTS
    n_slots = n_blocks * MOE_BM
    h, wk, srank, counts = _route2(x, mod, router_w, router_b)
    pos_tok, e_r, jlo, jhi, n_used = _moe_plan2(srank, counts, n_blocks)
    pos_row = pos_tok.T.reshape(N_EXPERTS, n // MOE_TC, MOE_TC)
    y, meta = _experts2(h, pos_row, (e_r, jlo, jhi, n_used), wgu_bf, wd_bf, n_blocks)
    tok = (meta[:, 0] * 128.0 + meta[:, 1]).astype(jnp.int32)
    choice = (meta[:, 2] == jnp.repeat(e_r, MOE_BM).astype(F32)).astype(jnp.int32)
    unused = meta[:, 3] < 0.5
    spare = 2 * n + jnp.cumsum(unused.astype(jnp.int32)) - 1
    dest = jnp.where(unused, spare, choice * n + tok)
    rows = _scatter_rows(y, dest, n_slots)
    return _finish(rows, wk, x, mod, ln_g, ln_b, n_latent_groups)


def _inproj_c_kernel(x_ref, mod_ref, w_ref, wg_ref, bg_ref, o_ref, og_ref):
    h = (x_ref[...] * (1.0 + mod_ref[1:2, :]) + mod_ref[0:1, :]).astype(BF16)
    o_ref[...] = jnp.dot(h, w_ref[...], preferred_element_type=F32)
    og_ref[...] = jnp.dot(h, wg_ref[...], preferred_element_type=F32) + bg_ref[...]


def _inproj_c(x, mod, w_bf, wg_bf, bg, tm=512):
    g, s, d = x.shape
    n = w_bf.shape[1]
    ng = wg_bf.shape[1]
    return pl.pallas_call(
        _inproj_c_kernel,
        out_shape=(jax.ShapeDtypeStruct((g, s, n), F32), jax.ShapeDtypeStruct((g, s, ng), F32)),
        grid=(g, s // tm),
        in_specs=[
            pl.BlockSpec((None, tm, d), lambda gi, ti: (gi, ti, 0)),
            pl.BlockSpec((None, 6, d), lambda gi, ti: (gi, 0, 0)),
            _resident((d, n), lambda gi, ti: (0, 0)),
            _resident((d, ng), lambda gi, ti: (0, 0)),
            pl.BlockSpec((1, ng), lambda gi, ti: (0, 0)),
        ],
        out_specs=(pl.BlockSpec((None, tm, n), lambda gi, ti: (gi, ti, 0)),
                   pl.BlockSpec((None, tm, ng), lambda gi, ti: (gi, ti, 0))),
        compiler_params=_cparams(("arbitrary", "arbitrary")),
        name="inproj_c",
    )(x, mod, w_bf, wg_bf, bg)


def _log_sigmoid(x):
    return jnp.minimum(x, 0.0) - jnp.log(1.0 + jnp.exp(-jnp.abs(x)))


MLSTM_L = 128


def _split3_bf16(x):
    hi = x.astype(BF16)
    r1 = x - hi.astype(F32)
    mid = r1.astype(BF16)
    lo = (r1 - mid.astype(F32)).astype(BF16)
    return hi, mid, lo


def _mlstm_kernel(*refs, seq, hg, has_init, emit_state):
    q_ref, k_ref, v_ref, o_ref, gi_ref, gf_ref, hgain_ref = refs[:7]
    pos = 7
    if has_init:
        c0_ref, n0_ref, m0_ref = refs[pos:pos + 3]
        pos += 3
    out_ref = refs[pos]
    pos += 1
    if emit_state:
        co_ref, no_ref, mo_ref = refs[pos:pos + 3]
        pos += 3
    cext_ref, hf_ref, hb_ref, b_ref, g_ref, gmax_ref, mt_ref, wi_ref, en_ref, ws_ref, gt_ref, wc_ref = refs[pos:]

    L = MLSTM_L
    dh = MLSTM_DH
    nh = MLSTM_HEADS
    nc = seq // L
    head0 = pl.program_id(1) * hg
    neg = -jnp.inf

    lane = lax.broadcasted_iota(jnp.int32, (L, LANES), 1)
    lane1 = lax.broadcasted_iota(jnp.int32, (1, LANES), 1)
    row = lax.broadcasted_iota(jnp.int32, (L, L), 0)
    col = lax.broadcasted_iota(jnp.int32, (L, L), 1)
    lower = col <= row
    upper = col >= row
    tri_l = jnp.where(lower, 1.0, 0.0).astype(BF16)
    tri_u = jnp.where(upper, 1.0, 0.0).astype(BF16)
    fwd_lane = lane < nh
    fwd_lane1 = lane1 < nh
    trow = lax.broadcasted_iota(jnp.int32, (L, LANES), 0)

    btot, glast = [], []
    for c in range(nc):
        rows = slice(c * L, (c + 1) * L)
        f = _log_sigmoid(gf_ref[rows, :])
        parts = _split3_bf16(f)
        pre = sum(jnp.dot(tri_l, p, preferred_element_type=F32) for p in parts)
        suf = sum(jnp.dot(tri_u, p, preferred_element_type=F32) for p in parts)
        b = jnp.where(fwd_lane, pre, suf)
        g = gi_ref[rows, :] - b
        gp, gs = g, g
        k = 1
        while k < L:
            gp = jnp.where(trow >= k, jnp.maximum(gp, pltpu.roll(gp, k, 0)), gp)
            gs = jnp.where(trow < L - k, jnp.maximum(gs, pltpu.roll(gs, L - k, 0)), gs)
            k *= 2
        gmax = jnp.where(fwd_lane, gp, gs)
        b_ref[rows, :] = b
        g_ref[rows, :] = g
        gmax_ref[rows, :] = gmax
        btot.append(jnp.where(fwd_lane1, b[L - 1:L, :], b[0:1, :]))
        glast.append(jnp.where(fwd_lane1, gmax[L - 1:L, :], gmax[0:1, :]))

    m_init = m0_ref[...] if has_init else jnp.zeros((1, LANES), F32)
    mf, mb = m_init, m_init
    ms_f, mn_f, ms_b, mn_b = [None] * nc, [None] * nc, [None] * nc, [None] * nc
    for c in range(nc):
        ms_f[c] = mf
        mf = btot[c] + jnp.maximum(mf, glast[c])
        mn_f[c] = mf
        cb = nc - 1 - c
        ms_b[cb] = mb
        mb = btot[cb] + jnp.maximum(mb, glast[cb])
        mn_b[cb] = mb
    m_final = jnp.where(fwd_lane1, mf, mb)

    for c in range(nc):
        rows = slice(c * L, (c + 1) * L)
        m_start = jnp.where(fwd_lane1, ms_f[c], ms_b[c])
        m_next = jnp.where(fwd_lane1, mn_f[c], mn_b[c])
        g = g_ref[rows, :]
        mt = jnp.maximum(m_start, gmax_ref[rows, :])
        mt_ref[rows, :] = mt
        wi_ref[rows, :] = jnp.exp(m_start - mt)
        en_ref[rows, :] = jnp.exp(-(b_ref[rows, :] + mt))
        ws_ref[rows, :] = jnp.exp(btot[c] + g - m_next)
        gt_ref[c] = g.T
        wc_ref[c:c + 1, :] = jnp.exp(btot[c] + m_start - m_next)

    lane_d = lax.broadcasted_iota(jnp.int32, (dh, dh), 1)
    for d in range(2):
        for hh in range(hg):
            idx = d * hg + hh
            if has_init:
                cext_ref[idx, :, 0:dh] = c0_ref[d, hh]
                n0_tile = jnp.where(lax.broadcasted_iota(jnp.int32, (dh, dh), 0) == 0, n0_ref[d, hh], 0.0)
                cext_ref[idx, :, dh:2 * dh] = n0_tile.T
            else:
                cext_ref[idx] = jnp.zeros((dh, 2 * dh), F32)

    ones_col = jnp.where(lane == 0, 1.0, 0.0).astype(BF16)
    nt = (((1,), (1,)), ((), ()))
    tn = (((0,), (0,)), ((), ()))

    def column(x, j):
        return jnp.sum(jnp.where(lane == j, x, 0.0), axis=-1, keepdims=True)

    def one_direction(d, hh, c, s_qk, q_bf, k_s, v_ext, v_bf):
        idx = d * hg + hh
        j = d * nh + head0 + hh
        rows = pl.ds(pl.multiple_of(c * L, L), L)
        mt = column(mt_ref[rows, :], j)
        wi = column(wi_ref[rows, :], j)
        en = column(en_ref[rows, :], j)
        ws = column(ws_ref[rows, :], j)
        g_r = gt_ref[c, pl.ds(j, 1), :]
        w_c = jnp.sum(jnp.where(lane1 == j, wc_ref[pl.ds(c, 1), :], 0.0), axis=-1, keepdims=True)
        causal = lower if d == 0 else upper
        p = s_qk * jnp.exp(jnp.where(causal, g_r - mt, neg))
        qc = jnp.dot(q_bf, cext_ref[idx].astype(BF16), preferred_element_type=F32)
        num = wi * qc[:, 0:dh] + jnp.dot(p.astype(BF16), v_bf, preferred_element_type=F32)
        den = wi * qc[:, dh:dh + 1] + jnp.sum(p, axis=-1, keepdims=True)
        h = num / jnp.maximum(jnp.abs(den), en)
        upd = lax.dot_general((ws * k_s).astype(BF16), v_ext, tn, preferred_element_type=F32)
        cext_ref[idx] = w_c * cext_ref[idx] + upd
        return h

    def load_chunk(hh, c):
        sl = (pl.ds(pl.multiple_of(c * L, L), L), slice(hh * dh, (hh + 1) * dh))
        q_bf = q_ref[sl].astype(BF16)
        k_s = k_ref[sl] * (dh ** -0.5)
        v_bf = v_ref[sl].astype(BF16)
        v_ext = jnp.concatenate([v_bf, ones_col], axis=-1)
        s_qk = lax.dot_general(q_bf, k_s.astype(BF16), nt, preferred_element_type=F32)
        return s_qk, q_bf, k_s, v_ext, v_bf

    def step(c, carry):
        cb = nc - 1 - c
        for hh in range(hg):
            h = one_direction(0, hh, c, *load_chunk(hh, c))
            hf_ref[pl.ds(pl.multiple_of(c * L, L), L), hh * dh:(hh + 1) * dh] = h
        for hh in range(hg):
            h = one_direction(1, hh, cb, *load_chunk(hh, cb))
            hb_ref[pl.ds(pl.multiple_of(cb * L, L), L), hh * dh:(hh + 1) * dh] = h
        return carry

    lax.fori_loop(0, nc, step, 0)

    for hh in range(hg):
        cs = slice(hh * dh, (hh + 1) * dh)
        hs = hf_ref[:, cs] + hb_ref[:, cs]
        mu = jnp.mean(hs, axis=-1, keepdims=True)
        hc = hs - mu
        var = jnp.mean(hc * hc, axis=-1, keepdims=True)
        hn = hc * lax.rsqrt(var + LN_EPS) * hgain_ref[:, cs]
        out_ref[:, cs] = (_sigmoid(o_ref[:, cs]) * hn).astype(out_ref.dtype)

    if emit_state:
        for d in range(2):
            for hh in range(hg):
                idx = d * hg + hh
                co_ref[d, hh] = cext_ref[idx, :, 0:dh]
                no_ref[d, hh] = cext_ref[idx, :, dh:2 * dh].T[0:1, :]
        mo_ref[...] = m_final


def _mlstm(proj, gates, head_g, g0, n_seq, seq, hg, init=None, emit_state=False):
    g, s, _ = proj.shape
    per_group = s // seq
    n_hg = MLSTM_HEADS // hg
    w = hg * MLSTM_DH
    nc = seq // MLSTM_L
    n_blocks = D_MODEL // w

    def tok_map(colblock):
        return lambda b, hi: (g0 + b // per_group, b % per_group, colblock * n_blocks + hi)

    def gate_map(half):
        return lambda b, hi: (g0 + b // per_group, b % per_group, half)

    args = [proj, proj, proj, proj, gates, gates, head_g.reshape(1, D_MODEL)]
    in_specs = [
        pl.BlockSpec((None, seq, w), tok_map(0)),
        pl.BlockSpec((None, seq, w), tok_map(1)),
        pl.BlockSpec((None, seq, w), tok_map(2)),
        pl.BlockSpec((None, seq, w), tok_map(3)),
        pl.BlockSpec((None, seq, LANES), gate_map(0)),
        pl.BlockSpec((None, seq, LANES), gate_map(1)),
        pl.BlockSpec((1, w), lambda b, hi: (0, hi)),
    ]
    if init is not None:
        c0, n0, m0 = init
        m0_lanes = jnp.pad(m0.reshape(n_seq, 1, 2 * MLSTM_HEADS), ((0, 0), (0, 0), (0, LANES - 2 * MLSTM_HEADS)))
        args += [c0, n0.reshape(n0.shape[:-1] + (1, MLSTM_DH)), m0_lanes]
        in_specs += [
            pl.BlockSpec((None, 2, hg, MLSTM_DH, MLSTM_DH), lambda b, hi: (b, 0, hi, 0, 0)),
            pl.BlockSpec((None, 2, hg, 1, MLSTM_DH), lambda b, hi: (b, 0, hi, 0, 0)),
            pl.BlockSpec((None, 1, LANES), lambda b, hi: (b, 0, 0)),
        ]

    out_shape = [jax.ShapeDtypeStruct((n_seq // per_group, s, D_MODEL), BF16)]
    out_specs = [pl.BlockSpec((None, seq, w), lambda b, hi: (b // per_group, b % per_group, hi))]
    if emit_state:
        out_shape += [
            jax.ShapeDtypeStruct((n_seq, 2, MLSTM_HEADS, MLSTM_DH, MLSTM_DH), F32),
            jax.ShapeDtypeStruct((n_seq, 2, MLSTM_HEADS, 1, MLSTM_DH), F32),
            jax.ShapeDtypeStruct((n_seq, n_hg, 1, LANES), F32),
        ]
        out_specs += [
            pl.BlockSpec((None, 2, hg, MLSTM_DH, MLSTM_DH), lambda b, hi: (b, 0, hi, 0, 0)),
            pl.BlockSpec((None, 2, hg, 1, MLSTM_DH), lambda b, hi: (b, 0, hi, 0, 0)),
            pl.BlockSpec((None, None, 1, LANES), lambda b, hi: (b, hi, 0, 0)),
        ]

    tok_scratch = pltpu.VMEM((seq, LANES), F32)
    return pl.pallas_call(
        functools.partial(_mlstm_kernel, seq=seq, hg=hg, has_init=init is not None, emit_state=emit_state),
        out_shape=tuple(out_shape),
        grid=(n_seq, n_hg),
        in_specs=in_specs,
        out_specs=tuple(out_specs),
        scratch_shapes=[
            pltpu.VMEM((2 * hg, MLSTM_DH, 2 * MLSTM_DH), F32),
            pltpu.VMEM((seq, w), F32),
            pltpu.VMEM((seq, w), F32),
        ] + [tok_scratch] * 7 + [
            pltpu.VMEM((nc, LANES, MLSTM_L), F32),
            pltpu.VMEM((max(nc, 8), LANES), F32),
        ],
        compiler_params=_cparams(("arbitrary", "arbitrary")),
        name="mlstm_%d" % seq,
    )(*args)


def _mlstm_kernel_old(*refs, seq, hg, has_init, emit_state):
    q_ref, k_ref, v_ref, o_ref, gc_ref, gr_ref, hgain_ref = refs[:7]
    pos = 7
    if has_init:
        c0_ref, n0_ref, m0_ref = refs[pos:pos + 3]
        pos += 3
    out_ref = refs[pos]
    pos += 1
    if emit_state:
        co_ref, no_ref, mo_ref = refs[pos:pos + 3]
        pos += 3
    cext_ref, hf_ref, hb_ref = refs[pos:pos + 3]

    L = MLSTM_CHUNK
    dh = MLSTM_DH
    nc = seq // L
    head0 = pl.program_id(1) * hg
    neg = -jnp.inf

    lane_d = lax.broadcasted_iota(jnp.int32, (dh, dh), 1)
    for d in range(2):
        for hh in range(hg):
            idx = d * hg + hh
            if has_init:
                cext_ref[idx, :, 0:dh] = c0_ref[d, hh]
                n0_tile = jnp.where(lax.broadcasted_iota(jnp.int32, (dh, dh), 0) == 0, n0_ref[d, hh], 0.0)
                cext_ref[idx, :, dh:2 * dh] = n0_tile.T
            else:
                cext_ref[idx] = jnp.zeros((dh, 2 * dh), F32)

    row = lax.broadcasted_iota(jnp.int32, (L, L), 0)
    col = lax.broadcasted_iota(jnp.int32, (L, L), 1)
    lower = col <= row
    upper = col >= row
    lane_g = lax.broadcasted_iota(jnp.int32, (L, LANES), 1)
    ones_col = jnp.where(lane_g == 0, 1.0, 0.0).astype(BF16)
    nt = (((1,), (1,)), ((), ()))
    tn = (((0,), (0,)), ((), ()))

    def one_direction(d, hh, c, s_qk, q_bf, k_s, v_ext, v_bf, m_prev):
        idx = d * hg + hh
        head = head0 + hh
        causal, anti = (lower, upper) if d == 0 else (upper, lower)
        gates_c = gc_ref[pl.ds(c * L, L), :]

        def col_of(j):
            return jnp.sum(jnp.where(lane_g == j, gates_c, 0.0), axis=-1, keepdims=True)

        i_c = col_of((2 * d) * MLSTM_HEADS + head)
        f_c = _log_sigmoid(col_of((2 * d + 1) * MLSTM_HEADS + head))
        i_r = gr_ref[2 * d, hh, pl.ds(c, 1), :]
        f_r = _log_sigmoid(gr_ref[2 * d + 1, hh, pl.ds(c, 1), :])

        b_c = jnp.sum(jnp.where(causal, f_r, 0.0), axis=1, keepdims=True)
        b_r = jnp.sum(jnp.where(anti, f_c, 0.0), axis=0, keepdims=True)
        b_tot = jnp.sum(f_r, axis=1, keepdims=True)
        dmat = jnp.where(causal, b_c - b_r + i_r, neg)
        m_inter = b_c + m_prev
        m_t = jnp.maximum(m_inter, jnp.max(dmat, axis=-1, keepdims=True))
        w_inter = jnp.exp(m_inter - m_t)
        p = s_qk * jnp.exp(dmat - m_t)
        qc = jnp.dot(q_bf, cext_ref[idx].astype(BF16), preferred_element_type=F32)
        num = w_inter * qc[:, 0:dh] + jnp.dot(p.astype(BF16), v_bf, preferred_element_type=F32)
        den = w_inter * qc[:, dh:dh + 1] + jnp.sum(p, axis=-1, keepdims=True)
        h = num / jnp.maximum(jnp.abs(den), jnp.exp(-m_t))
        last = L - 1 if d == 0 else 0
        m_new = m_t[last:last + 1, :]
        w_c = jnp.exp(b_tot + m_prev - m_new)
        w_s = jnp.exp(b_tot - b_c + i_c - m_new)
        upd = lax.dot_general((w_s * k_s).astype(BF16), v_ext, tn, preferred_element_type=F32)
        cext_ref[idx] = w_c * cext_ref[idx] + upd
        return h, m_new

    def load_chunk(hh, c):
        sl = (pl.ds(c * L, L), slice(hh * dh, (hh + 1) * dh))
        q_bf = q_ref[sl].astype(BF16)
        k_s = k_ref[sl] * (dh ** -0.5)
        v_bf = v_ref[sl].astype(BF16)
        v_ext = jnp.concatenate([v_bf, ones_col], axis=-1)
        s_qk = lax.dot_general(q_bf, k_s.astype(BF16), nt, preferred_element_type=F32)
        return s_qk, q_bf, k_s, v_ext, v_bf

    def step(c, ms):
        cb = nc - 1 - c
        new_ms = []
        for hh in range(hg):
            h, m_new = one_direction(0, hh, c, *load_chunk(hh, c), ms[hh])
            hf_ref[pl.ds(c * L, L), hh * dh:(hh + 1) * dh] = h
            new_ms.append(m_new)
        for hh in range(hg):
            h, m_new = one_direction(1, hh, cb, *load_chunk(hh, cb), ms[hg + hh])
            hb_ref[pl.ds(cb * L, L), hh * dh:(hh + 1) * dh] = h
            new_ms.append(m_new)
        return tuple(new_ms)

    if has_init:
        ms0 = tuple(m0_ref[d, hh] for d in range(2) for hh in range(hg))
    else:
        ms0 = tuple(jnp.zeros((1, 1), F32) for _ in range(2 * hg))
    ms = lax.fori_loop(0, nc, step, ms0)

    for hh in range(hg):
        cs = slice(hh * dh, (hh + 1) * dh)
        hs = hf_ref[:, cs] + hb_ref[:, cs]
        mu = jnp.mean(hs, axis=-1, keepdims=True)
        hc = hs - mu
        var = jnp.mean(hc * hc, axis=-1, keepdims=True)
        hn = hc * lax.rsqrt(var + LN_EPS) * hgain_ref[:, cs]
        out_ref[:, cs] = (_sigmoid(o_ref[:, cs]) * hn).astype(out_ref.dtype)

    if emit_state:
        for d in range(2):
            for hh in range(hg):
                idx = d * hg + hh
                co_ref[d, hh] = cext_ref[idx, :, 0:dh]
                no_ref[d, hh] = cext_ref[idx, :, dh:2 * dh].T[0:1, :]
                mo_ref[d, hh] = jnp.broadcast_to(ms[idx], (1, LANES))


def _mlstm_old(proj, gates, head_g, g0, n_seq, seq, hg, init=None, emit_state=False):
    g, s, _ = proj.shape
    per_group = s // seq
    n_hg = MLSTM_HEADS // hg
    w = hg * MLSTM_DH
    nc = seq // MLSTM_CHUNK
    n_blocks = D_MODEL // w

    g_seq = gates[g0:g0 + n_seq // per_group].reshape(n_seq, seq, LANES)
    g_row = g_seq[:, :, :N_GATES * MLSTM_HEADS].transpose(0, 2, 1).reshape(
        n_seq, N_GATES, MLSTM_HEADS, nc, MLSTM_CHUNK)

    def tok_map(colblock):
        return lambda b, hi: (g0 + b // per_group, b % per_group, colblock * n_blocks + hi)

    args = [proj, proj, proj, proj, g_seq, g_row, head_g.reshape(1, D_MODEL)]
    in_specs = [
        pl.BlockSpec((None, seq, w), tok_map(0)),
        pl.BlockSpec((None, seq, w), tok_map(1)),
        pl.BlockSpec((None, seq, w), tok_map(2)),
        pl.BlockSpec((None, seq, w), tok_map(3)),
        pl.BlockSpec((None, seq, LANES), lambda b, hi: (b, 0, 0)),
        pl.BlockSpec((None, N_GATES, hg, nc, MLSTM_CHUNK), lambda b, hi: (b, 0, hi, 0, 0)),
        pl.BlockSpec((1, w), lambda b, hi: (0, hi)),
    ]
    if init is not None:
        c0, n0, m0 = init
        args += [c0, n0.reshape(n0.shape + (1,)), m0.reshape(m0.shape + (1, 1))]
        in_specs += [
            pl.BlockSpec((None, 2, hg, MLSTM_DH, MLSTM_DH), lambda b, hi: (b, 0, hi, 0, 0)),
            pl.BlockSpec((None, 2, hg, 1, MLSTM_DH), lambda b, hi: (b, 0, hi, 0, 0)),
            pl.BlockSpec((None, 2, hg, 1, 1), lambda b, hi: (b, 0, hi, 0, 0)),
        ]
    out_shape = [jax.ShapeDtypeStruct((n_seq // per_group, s, D_MODEL), BF16)]
    out_specs = [pl.BlockSpec((None, seq, w), lambda b, hi: (b // per_group, b % per_group, hi))]
    if emit_state:
        out_shape += [
            jax.ShapeDtypeStruct((n_seq, 2, MLSTM_HEADS, MLSTM_DH, MLSTM_DH), F32),
            jax.ShapeDtypeStruct((n_seq, 2, MLSTM_HEADS, 1, MLSTM_DH), F32),
            jax.ShapeDtypeStruct((n_seq, 2, MLSTM_HEADS, 1, LANES), F32),
        ]
        out_specs += [
            pl.BlockSpec((None, 2, hg, MLSTM_DH, MLSTM_DH), lambda b, hi: (b, 0, hi, 0, 0)),
            pl.BlockSpec((None, 2, hg, 1, MLSTM_DH), lambda b, hi: (b, 0, hi, 0, 0)),
            pl.BlockSpec((None, 2, hg, 1, LANES), lambda b, hi: (b, 0, hi, 0, 0)),
        ]

    return pl.pallas_call(
        functools.partial(_mlstm_kernel, seq=seq, hg=hg, has_init=init is not None, emit_state=emit_state),
        out_shape=tuple(out_shape),
        grid=(n_seq, n_hg),
        in_specs=in_specs,
        out_specs=tuple(out_specs),
        scratch_shapes=[
            pltpu.VMEM((2 * hg, MLSTM_DH, 2 * MLSTM_DH), F32),
            pltpu.VMEM((seq, w), F32),
            pltpu.VMEM((seq, w), F32),
        ],
        compiler_params=_cparams(("arbitrary", "arbitrary")),
        name="mlstm_%d" % seq,
    )(*args)


def kernel(x_prompt, x_sample, c, cache_k, cache_v, state_C, state_n, state_m, c_ctx, ada_w, ada_b, ln_g, ln_b, w_in_a, diff_lambda, diff_norm_g, pool_w, pool_scale, w_out_a, ffn_w_gu, ffn_w_down, w_in_c, b_gates_c, mlstm_norm_g, w_out_c, router_w, router_b, moe_w_gu, moe_w_down):
    n_ctx, seq_ctx, d = x_prompt.shape
    n_lat, seq_lat, _ = x_sample.shape
    assert d == D_MODEL and (n_ctx * seq_ctx) % seq_lat == 0 and seq_lat % seq_ctx == 0
    gl = n_lat
    gc = n_ctx * seq_ctx // seq_lat
    s = seq_lat

    x_ctx = x_prompt.reshape(gc, s, d)
    cvec = jnp.concatenate([c, jnp.broadcast_to(c_ctx[None, :], (gc, d))], axis=0)
    mod_all = _modulation(cvec, ada_w, ada_b).reshape(DEPTH, gl + gc, 6, d)

    mod = mod_all[0]
    lam_init = 0.8 - 0.6 * math.exp(-0.3 * 0)
    cos_t, sin_t = _rope_tables(s)
    proj = _inproj_a(x_sample, x_ctx, mod, w_in_a[0].astype(BF16), cos_t, sin_t)
    norm_g = diff_norm_g[0].reshape(1, LANES)
    attn_c, new_k, new_v = _attn_context(proj, diff_lambda[0], norm_g, gl, n_ctx, seq_ctx, lam_init)
    attn_l = _attn_latent(proj, cache_k, cache_v, diff_lambda[0], norm_g, gl, lam_init)
    pool_c = _pool(proj, pool_w[0], pool_scale[0], gl, gc, seq_ctx)
    pool_l = _pool(proj, pool_w[0], pool_scale[0], 0, gl, seq_lat)
    w_out = w_out_a[0].astype(BF16)
    x = _outproj([(attn_l, attn_c), (pool_l, pool_c)], [w_out[:DIFF_WIDTH], w_out[DIFF_WIDTH:]],
                 x_sample, x_ctx, mod, ln_g[0, 0], ln_b[0, 0], 2)
    x = _ffn(x, mod, ffn_w_gu[0].astype(BF16), ffn_w_down[0].astype(BF16), ln_g[0, 1], ln_b[0, 1])

    mod = mod_all[1]
    n_main = 4 * D_MODEL
    w_main = w_in_c[0][:, :n_main].astype(BF16)
    nh = MLSTM_HEADS
    wg4 = w_in_c[0][:, n_main:].reshape(d, N_GATES, nh)
    bg4 = b_gates_c[0].reshape(1, N_GATES, nh)
    lane_pad = ((0, 0), (0, LANES - 2 * nh))

    def gate_lanes(a):
        return jnp.concatenate([jnp.pad(jnp.concatenate([a[:, 0], a[:, 2]], axis=-1), lane_pad),
                                jnp.pad(jnp.concatenate([a[:, 1], a[:, 3]], axis=-1), lane_pad)], axis=-1)

    proj, gates = _inproj_c(x, mod, w_main, gate_lanes(wg4).astype(BF16), gate_lanes(bg4))
    mix_c, new_c, new_n, new_m = _mlstm(proj, gates, mlstm_norm_g[0], gl, n_ctx, seq_ctx, 4, emit_state=True)
    (mix_l,) = _mlstm(proj, gates, mlstm_norm_g[0], 0, n_lat, seq_lat, 4,
                      init=(state_C[:, 0], state_n[:, 0], state_m[:, 0]))
    x = _outproj([(mix_l, mix_c)], [w_out_c[0].astype(BF16)], x, None, mod, ln_g[1, 0], ln_b[1, 0], 2)
    y_sample, y_ctx = _moe2(x, mod, router_w[0], router_b[0], moe_w_gu[0].astype(BF16),
                            moe_w_down[0].astype(BF16), ln_g[1, 1], ln_b[1, 1], gl)
    y_prompt = y_ctx.reshape(n_ctx, seq_ctx, d)
    new_m = new_m[:, 0, 0, :2 * MLSTM_HEADS].reshape(n_ctx, 2, MLSTM_HEADS)
    return (y_prompt, y_sample, new_k, new_v, new_c[:, None], new_n[..., 0, :][:, None], new_m[:, None])
```

```python
import functools
import math

import jax
import jax.numpy as jnp
from jax import lax
from jax.experimental import pallas as pl
from jax.experimental.pallas import tpu as pltpu

F32 = jnp.float32
BF16 = jnp.bfloat16

D_MODEL = 1024
GRID_W = 64
ROPE_BASE = 10000.0
DIFF_HEADS = 4
DIFF_DH = 64
DIFF_WIDTH = DIFF_HEADS * 2 * DIFF_DH
POOL_GROUPS = 4
POOL_GC = 128
POOL_WIDTH = POOL_GROUPS * POOL_GC
POOL_WINDOWS = (2, 4, 8, 16)
W_IN_A = 3 * DIFF_WIDTH + POOL_WIDTH
MLSTM_HEADS = 8
MLSTM_DH = 128
MLSTM_CHUNK = 64
N_GATES = 4
D_FF = 2816
N_EXPERTS = 8
D_FF_EXPERT = 1792
LN_EPS = 1e-5
DEPTH = 2
ALPHA = (2.0 * DEPTH) ** 0.25

LANES = 128
FF_CHUNK = 256
VMEM_LIMIT = 56 * 1024 * 1024


def _cparams(sem, **kw):
    return pltpu.CompilerParams(dimension_semantics=sem, vmem_limit_bytes=VMEM_LIMIT, **kw)


def _resident(shape, index_map):
    return pl.BlockSpec(shape, index_map, pipeline_mode=pl.Buffered(1))


def _layer_norm_rows(z, g, b):
    mu = jnp.mean(z, axis=-1, keepdims=True)
    zc = z - mu
    var = jnp.mean(zc * zc, axis=-1, keepdims=True)
    return zc * lax.rsqrt(var + LN_EPS) * g + b


def _sigmoid(x):
    return 1.0 / (1.0 + jnp.exp(-x))


SUBLANES = 8
assert D_MODEL == SUBLANES * LANES


def _store_token_tiles(ref, x, *lead):
    for j in range(SUBLANES):
        ref[lead + (slice(None), j, slice(None))] = x[:, j * LANES:(j + 1) * LANES]


def _load_token_tiles(ref, *lead):
    return jnp.concatenate([ref[lead + (slice(None), j, slice(None))] for j in range(SUBLANES)], axis=-1)


def _split_bf16(x):
    hi = x.astype(BF16)
    lo = (x - hi.astype(F32)).astype(BF16)
    return hi, lo


def _mod_kernel(c_ref, w_ref, b_ref, o_ref):
    c = c_ref[...]
    h = (c * _sigmoid(c)).astype(BF16)
    o_ref[...] = jnp.dot(h, w_ref[...].astype(BF16), preferred_element_type=F32) + b_ref[...]


def _modulation(cvec, ada_w, ada_b):
    depth, d, n = ada_w.shape
    g = cvec.shape[0]
    tn = 1536
    return pl.pallas_call(
        _mod_kernel,
        out_shape=jax.ShapeDtypeStruct((depth, g, n), F32),
        grid=(depth, n // tn),
        in_specs=[
            pl.BlockSpec((g, d), lambda l, j: (0, 0)),
            pl.BlockSpec((None, d, tn), lambda l, j: (l, 0, j)),
            pl.BlockSpec((None, 1, tn), lambda l, j: (l, 0, j)),
        ],
        out_specs=pl.BlockSpec((None, g, tn), lambda l, j: (l, 0, j)),
        compiler_params=_cparams(("arbitrary", "arbitrary")),
        name="modulation",
    )(cvec, ada_w, ada_b.reshape(depth, 1, n))


def _rot_half16(x):
    lane = lax.broadcasted_iota(jnp.int32, x.shape, 1)
    return jnp.where((lane % 32) < 16, pltpu.roll(x, LANES - 16, 1), pltpu.roll(x, 16, 1))


def _two_stream_specs(tm, d, gl, ctx_first_group=0):
    return [pl.BlockSpec((None, tm, d), lambda gi, ti: (jnp.minimum(gi, gl - 1), jnp.where(gi < gl, ti, 0), 0)),
            pl.BlockSpec((None, tm, d), lambda gi, ti: (ctx_first_group + jnp.maximum(gi - gl, 0),
                                                        jnp.where(gi < gl, 0, ti), 0))]


def _inproj_a_kernel(xl_ref, xc_ref, mod_ref, w_ref, cos_ref, sin_ref, o_ref, *, n_latent_groups):
    x = jnp.where(pl.program_id(0) < n_latent_groups, xl_ref[...], xc_ref[...])
    h = x * (1.0 + mod_ref[1:2, :]) + mod_ref[0:1, :]
    p = jnp.dot(h.astype(BF16), w_ref[...], preferred_element_type=F32)
    cos = cos_ref[...]
    sin = sin_ref[...]
    n_rope = 2 * DIFF_WIDTH // LANES
    for j in range(n_rope):
        blk = p[:, j * LANES:(j + 1) * LANES]
        o_ref[:, j * LANES:(j + 1) * LANES] = blk * cos + _rot_half16(blk) * sin
    o_ref[:, n_rope * LANES:] = p[:, n_rope * LANES:]


def _inproj_a(x_lat, x_ctx, mod, w_bf, cos_t, sin_t, tm=512):
    n_latent_groups, s, d = x_lat.shape
    g = n_latent_groups + x_ctx.shape[0]
    n = w_bf.shape[1]

    def table_map(gi, ti):
        return (jnp.where(gi >= n_latent_groups, 1, 0), ti, 0)

    return pl.pallas_call(
        functools.partial(_inproj_a_kernel, n_latent_groups=n_latent_groups),
        out_shape=jax.ShapeDtypeStruct((g, s, n), F32),
        grid=(g, s // tm),
        in_specs=_two_stream_specs(tm, d, n_latent_groups) + [
            pl.BlockSpec((None, 6, d), lambda gi, ti: (gi, 0, 0)),
            _resident((d, n), lambda gi, ti: (0, 0)),
            pl.BlockSpec((None, tm, LANES), table_map),
            pl.BlockSpec((None, tm, LANES), table_map),
        ],
        out_specs=pl.BlockSpec((None, tm, n), lambda gi, ti: (gi, ti, 0)),
        compiler_params=_cparams(("arbitrary", "arbitrary")),
        name="inproj_a",
    )(x_lat, x_ctx, mod, w_bf, cos_t, sin_t)


def _rope_tables(n_tokens):
    rows = n_tokens // GRID_W
    row_pos = jnp.repeat(jnp.arange(rows), GRID_W).astype(F32)
    col_pos = jnp.tile(jnp.arange(GRID_W), rows).astype(F32)
    n_freq = DIFF_DH // 4
    inv_freq = jnp.power(ROPE_BASE, -jnp.arange(n_freq, dtype=F32) / n_freq)
    ang = jnp.stack([row_pos[:, None] * inv_freq, col_pos[:, None] * inv_freq], axis=1)
    cos, sin = jnp.cos(ang), jnp.sin(ang)
    cos64 = jnp.concatenate([cos[:, 0], cos[:, 0], cos[:, 1], cos[:, 1]], axis=-1)
    sin64 = jnp.concatenate([-sin[:, 0], sin[:, 0], -sin[:, 1], sin[:, 1]], axis=-1)
    cos_l = jnp.tile(cos64, (1, LANES // DIFF_DH))
    sin_l = jnp.tile(sin64, (1, LANES // DIFF_DH))
    cos_t = jnp.stack([cos_l, jnp.ones_like(cos_l)])
    sin_t = jnp.stack([sin_l, jnp.zeros_like(sin_l)])
    return cos_t, sin_t


def _diff_attn_kernel(*refs, n_pieces, n_heads, lam_init, emit_kv):
    lam_ref, ng_ref, q_ref = refs[:3]
    kv_refs = refs[3:3 + 2 * n_pieces]
    o_ref = refs[3 + 2 * n_pieces]

    lp = lam_ref[...]
    lam = (jnp.exp(jnp.sum(lp[0:1] * lp[1:2], axis=-1, keepdims=True))
           - jnp.exp(jnp.sum(lp[2:3] * lp[3:4], axis=-1, keepdims=True)) + lam_init)
    nt = (((1,), (1,)), ((), ()))

    def softmax_pieces(ss):
        m = functools.reduce(jnp.maximum, [jnp.max(s, axis=-1, keepdims=True) for s in ss])
        es = [jnp.exp(s - m) for s in ss]
        l = functools.reduce(jnp.add, [jnp.sum(e, axis=-1, keepdims=True) for e in es])
        return [e / l for e in es]

    for h in range(n_heads):
        hs = slice(h * LANES, (h + 1) * LANES)
        q = q_ref[:, hs] * (DIFF_DH ** -0.5)
        lane = lax.broadcasted_iota(jnp.int32, q.shape, 1)
        q1 = jnp.where(lane < DIFF_DH, q, 0.0).astype(BF16)
        q2 = jnp.where(lane >= DIFF_DH, q, 0.0).astype(BF16)
        s1, s2, vs = [], [], []
        for i in range(n_pieces):
            kb = kv_refs[2 * i][:, hs].astype(BF16)
            vs.append(kv_refs[2 * i + 1][:, hs].astype(BF16))
            s1.append(lax.dot_general(q1, kb, nt, preferred_element_type=F32))
            s2.append(lax.dot_general(q2, kb, nt, preferred_element_type=F32))
        p1 = softmax_pieces(s1)
        p2 = softmax_pieces(s2)
        o = None
        for i in range(n_pieces):
            a = (p1[i] - lam * p2[i]).astype(BF16)
            t = jnp.dot(a, vs[i], preferred_element_type=F32)
            o = t if o is None else o + t
        o = o * lax.rsqrt(jnp.mean(o * o, axis=-1, keepdims=True) + LN_EPS)
        o_ref[:, hs] = (o * ng_ref[...] * (1.0 - lam_init)).astype(o_ref.dtype)
        if emit_kv:
            ko_ref, vo_ref = refs[4 + 2 * n_pieces:]
            ko_ref[h] = kv_refs[0][:, hs]
            vo_ref[h] = kv_refs[1][:, hs]


def _attn_context(proj, lam_p, norm_g, n_latent_groups, n_seq, seq, lam_init):
    g, s, _ = proj.shape
    per_group = s // seq
    blk = (None, seq, DIFF_WIDTH)

    def tok_map(colblock):
        return lambda b: (n_latent_groups + b // per_group, b % per_group, colblock)

    cache_shape = jax.ShapeDtypeStruct((n_seq, 1, DIFF_HEADS, seq, LANES), F32)
    cache_spec = pl.BlockSpec((None, None, DIFF_HEADS, seq, LANES), lambda b: (b, 0, 0, 0, 0))
    out_spec = pl.BlockSpec(blk, lambda b: (b // per_group, b % per_group, 0))
    return pl.pallas_call(
        functools.partial(_diff_attn_kernel, n_pieces=1, n_heads=DIFF_HEADS, lam_init=lam_init, emit_kv=True),
        out_shape=(jax.ShapeDtypeStruct((g - n_latent_groups, s, DIFF_WIDTH), BF16), cache_shape, cache_shape),
        grid=(n_seq,),
        in_specs=[
            pl.BlockSpec((4, DIFF_DH), lambda b: (0, 0)),
            pl.BlockSpec((1, LANES), lambda b: (0, 0)),
            pl.BlockSpec(blk, tok_map(0)),
            pl.BlockSpec(blk, tok_map(1)),
            pl.BlockSpec(blk, tok_map(2)),
        ],
        out_specs=(out_spec, cache_spec, cache_spec),
        compiler_params=_cparams(("arbitrary",)),
        name="attn_context",
    )(lam_p, norm_g, proj, proj, proj)


def _attn_latent(proj, cache_k, cache_v, lam_p, norm_g, n_latent_groups, lam_init, tq=256):
    g, s, _ = proj.shape
    past = cache_k.shape[3]
    cache_spec = pl.BlockSpec((None, None, None, past, LANES), lambda b, h, qi: (b, 0, h, 0, 0))
    return pl.pallas_call(
        functools.partial(_diff_attn_kernel, n_pieces=2, n_heads=1, lam_init=lam_init, emit_kv=False),
        out_shape=jax.ShapeDtypeStruct((n_latent_groups, s, DIFF_WIDTH), BF16),
        grid=(n_latent_groups, DIFF_HEADS, s // tq),
        in_specs=[
            pl.BlockSpec((4, DIFF_DH), lambda b, h, qi: (0, 0)),
            pl.BlockSpec((1, LANES), lambda b, h, qi: (0, 0)),
            pl.BlockSpec((None, tq, LANES), lambda b, h, qi: (b, qi, h)),
            cache_spec,
            cache_spec,
            pl.BlockSpec((None, s, LANES), lambda b, h, qi: (b, 0, DIFF_HEADS + h)),
            pl.BlockSpec((None, s, LANES), lambda b, h, qi: (b, 0, 2 * DIFF_HEADS + h)),
        ],
        out_specs=pl.BlockSpec((None, tq, LANES), lambda b, h, qi: (b, qi, h)),
        compiler_params=_cparams(("arbitrary", "arbitrary", "arbitrary")),
        name="attn_latent",
    )(lam_p, norm_g, proj, cache_k, cache_v, proj, proj)


POOL_ROW_BLOCK = 256
POOL_COL_WINDOW = 512
assert (POOL_COL_WINDOW - POOL_ROW_BLOCK) // 2 >= max(POOL_WINDOWS) // 2


def _pool_kernel(p_ref, w_ref, sc_ref, o_ref, band_ref, *, seq):
    @pl.when((pl.program_id(0) == 0) & (pl.program_id(1) == 0))
    def _():
        t = lax.broadcasted_iota(jnp.int32, (seq, seq), 0)
        s_ = lax.broadcasted_iota(jnp.int32, (seq, seq), 1)
        for gi, w in enumerate(POOL_WINDOWS):
            inside = (s_ >= t - w // 2) & (s_ <= t + w // 2 - 1)
            band_ref[gi] = jnp.where(inside, 1.0, 0.0).astype(BF16)

    tcol = lax.broadcasted_iota(jnp.int32, (seq, 1), 0)
    for gi, w in enumerate(POOL_WINDOWS):
        u = p_ref[:, gi * POOL_GC:(gi + 1) * POOL_GC]
        hi, lo = _split_bf16(u)
        rb = min(seq, POOL_ROW_BLOCK)
        cw = min(seq, POOL_COL_WINDOW)
        blocks = []
        for i in range(seq // rb):
            c0 = min(max(i * rb - (cw - rb) // 2, 0), seq - cw)
            band = band_ref[gi, i * rb:(i + 1) * rb, c0:c0 + cw]
            blocks.append(jnp.dot(band, hi[c0:c0 + cw], preferred_element_type=F32)
                          + jnp.dot(band, lo[c0:c0 + cw], preferred_element_type=F32))
        win = blocks[0] if len(blocks) == 1 else jnp.concatenate(blocks, axis=0)
        cnt = (jnp.minimum(tcol + (w // 2 - 1), seq - 1) - jnp.maximum(tcol - w // 2, 0) + 1).astype(F32)
        pooled = win / cnt - u
        mixed = jnp.dot(pooled.astype(BF16), w_ref[gi].astype(BF16), preferred_element_type=F32)
        o_ref[:, gi * POOL_GC:(gi + 1) * POOL_GC] = (
            mixed * sc_ref[:, gi * POOL_GC:(gi + 1) * POOL_GC]).astype(o_ref.dtype)


def _pool(proj, pool_w, pool_scale, g0, n_groups, seq):
    g, s, _ = proj.shape
    col = 3 * DIFF_WIDTH // POOL_WIDTH
    return pl.pallas_call(
        functools.partial(_pool_kernel, seq=seq),
        out_shape=jax.ShapeDtypeStruct((n_groups, s, POOL_WIDTH), BF16),
        grid=(n_groups, s // seq),
        in_specs=[
            pl.BlockSpec((None, seq, POOL_WIDTH), lambda gi, ti: (g0 + gi, ti, col)),
            pl.BlockSpec((POOL_GROUPS, POOL_GC, POOL_GC), lambda gi, ti: (0, 0, 0)),
            pl.BlockSpec((1, POOL_WIDTH), lambda gi, ti: (0, 0)),
        ],
        out_specs=pl.BlockSpec((None, seq, POOL_WIDTH), lambda gi, ti: (gi, ti, 0)),
        scratch_shapes=[pltpu.VMEM((POOL_GROUPS, seq, seq), BF16)],
        compiler_params=_cparams(("arbitrary", "arbitrary")),
        name="pool_%d" % seq,
    )(proj, pool_w, pool_scale.reshape(1, POOL_WIDTH))


def _outproj_kernel(*refs, n_in, gate_row, n_latent_groups):
    a_refs = refs[:2 * n_in]
    w_refs = refs[2 * n_in:3 * n_in]
    xl_ref, xc_ref, mod_ref, g_ref, b_ref, o_ref = refs[3 * n_in:]
    is_latent = pl.program_id(0) < n_latent_groups
    acc = None
    for i, w_ref in enumerate(w_refs):
        a = jnp.where(is_latent, a_refs[2 * i][...], a_refs[2 * i + 1][...])
        t = jnp.dot(a, w_ref[...], preferred_element_type=F32)
        acc = t if acc is None else acc + t
    x = jnp.where(is_latent, xl_ref[...], xc_ref[...])
    z = ALPHA * x + mod_ref[gate_row:gate_row + 1, :] * acc
    o_ref[...] = _layer_norm_rows(z, g_ref[...], b_ref[...])


def _outproj(acts, weights, x_lat, x_ctx, mod, ln_g, ln_b, gate_row, tm=512):
    gl = acts[0][0].shape[0]
    _, s, d = x_lat.shape
    if x_ctx is None:
        g = x_lat.shape[0]
        x_ctx, x_specs = x_lat, _two_stream_specs(tm, d, gl, gl)
    else:
        g = gl + x_ctx.shape[0]
        x_specs = _two_stream_specs(tm, d, gl)
    n_in = len(acts)
    in_specs = []
    flat_acts = []
    for a_lat, a_ctx in acts:
        in_specs += _two_stream_specs(tm, a_lat.shape[-1], gl)
        flat_acts += [a_lat, a_ctx]
    in_specs += [_resident(w.shape, lambda gi, ti: (0, 0)) for w in weights]
    in_specs += x_specs
    in_specs += [
        pl.BlockSpec((None, 6, d), lambda gi, ti: (gi, 0, 0)),
        pl.BlockSpec((1, d), lambda gi, ti: (0, 0)),
        pl.BlockSpec((1, d), lambda gi, ti: (0, 0)),
    ]
    return pl.pallas_call(
        functools.partial(_outproj_kernel, n_in=n_in, gate_row=gate_row, n_latent_groups=gl),
        out_shape=jax.ShapeDtypeStruct((g, s, d), F32),
        grid=(g, s // tm),
        in_specs=in_specs,
        out_specs=pl.BlockSpec((None, tm, d), lambda gi, ti: (gi, ti, 0)),
        compiler_params=_cparams(("arbitrary", "arbitrary")),
        name="outproj",
    )(*flat_acts, *weights, x_lat, x_ctx, mod, ln_g.reshape(1, d), ln_b.reshape(1, d))


def _swiglu_chunks(h_bf, wgu_ref, wd_ref, d_ff):
    acc = None
    for j in range(d_ff // FF_CHUNK):
        lo = j * FF_CHUNK
        gate = jnp.dot(h_bf, wgu_ref[:, lo:lo + FF_CHUNK], preferred_element_type=F32)
        up = jnp.dot(h_bf, wgu_ref[:, d_ff + lo:d_ff + lo + FF_CHUNK], preferred_element_type=F32)
        act = (gate * _sigmoid(gate) * up).astype(BF16)
        t = jnp.dot(act, wd_ref[lo:lo + FF_CHUNK, :], preferred_element_type=F32)
        acc = t if acc is None else acc + t
    return acc


def _ffn_kernel(x_ref, mod_ref, wgu_ref, wd_ref, g_ref, b_ref, o_ref):
    x = x_ref[...]
    h = (x * (1.0 + mod_ref[4:5, :]) + mod_ref[3:4, :]).astype(BF16)
    acc = _swiglu_chunks(h, wgu_ref, wd_ref, D_FF)
    z = ALPHA * x + mod_ref[5:6, :] * acc
    o_ref[...] = _layer_norm_rows(z, g_ref[...], b_ref[...])


def _ffn(x, mod, wgu_bf, wd_bf, ln_g, ln_b, tm=512):
    g, s, d = x.shape
    return pl.pallas_call(
        _ffn_kernel,
        out_shape=jax.ShapeDtypeStruct((g, s, d), F32),
        grid=(g, s // tm),
        in_specs=[
            pl.BlockSpec((None, tm, d), lambda gi, ti: (gi, ti, 0)),
            pl.BlockSpec((None, 6, d), lambda gi, ti: (gi, 0, 0)),
            _resident(wgu_bf.shape, lambda gi, ti: (0, 0)),
            _resident(wd_bf.shape, lambda gi, ti: (0, 0)),
            pl.BlockSpec((1, d), lambda gi, ti: (0, 0)),
            pl.BlockSpec((1, d), lambda gi, ti: (0, 0)),
        ],
        out_specs=pl.BlockSpec((None, tm, d), lambda gi, ti: (gi, ti, 0)),
        compiler_params=_cparams(("arbitrary", "arbitrary")),
        name="ffn",
    )(x, mod, wgu_bf, wd_bf, ln_g.reshape(1, d), ln_b.reshape(1, d))


def _router_combine(h, rw_ref, rb_ref):
    h_hi, h_lo = _split_bf16(h)
    w_hi, w_lo = _split_bf16(rw_ref[...])
    logits = (jnp.dot(h_hi, w_hi, preferred_element_type=F32)
              + jnp.dot(h_lo, w_hi, preferred_element_type=F32)
              + jnp.dot(h_hi, w_lo, preferred_element_type=F32)) + rb_ref[...]
    lane = lax.broadcasted_iota(jnp.int32, logits.shape, 1).astype(F32)
    neg = -jnp.inf
    logits = jnp.where(lane < N_EXPERTS, logits, neg)
    m1 = jnp.max(logits, axis=-1, keepdims=True)
    i1 = jnp.min(jnp.where(logits == m1, lane, float(LANES)), axis=-1, keepdims=True)
    rest = jnp.where(lane == i1, neg, logits)
    m2 = jnp.max(rest, axis=-1, keepdims=True)
    i2 = jnp.min(jnp.where(rest == m2, lane, float(LANES)), axis=-1, keepdims=True)
    e2 = jnp.exp(m2 - m1)
    w1 = 1.0 / (1.0 + e2)
    w2 = e2 / (1.0 + e2)
    member = (lane == i1) | (lane == i2)
    return jnp.where(lane == i1, w1, 0.0) + jnp.where(lane == i2, w2, 0.0), member


MOE_BM = 256
MOE_TC = 256
MOE_TMC = 512


def _route_kernel(x_ref, mod_ref, rw_ref, rb_ref, h_ref, cw_ref, srank_ref, cnt_ref, tri_ref, run_ref):
    tm = x_ref.shape[0]

    @pl.when(pl.program_id(0) == 0)
    def _():
        r = lax.broadcasted_iota(jnp.int32, (tm, tm), 0)
        c = lax.broadcasted_iota(jnp.int32, (tm, tm), 1)
        tri_ref[...] = jnp.where(c <= r, 1.0, 0.0).astype(BF16)
        run_ref[...] = jnp.zeros_like(run_ref)

    h = x_ref[...] * (1.0 + mod_ref[4:5, :]) + mod_ref[3:4, :]
    h_ref[...] = h.astype(BF16)
    cw, member = _router_combine(h, rw_ref, rb_ref)
    cw_ref[...] = cw
    mem = jnp.where(member, 1.0, 0.0)
    rank = jnp.dot(tri_ref[...], mem.astype(BF16), preferred_element_type=F32) + run_ref[...]
    srank_ref[...] = jnp.where(member, rank, -rank)
    run_ref[...] = rank[tm - 1:tm, :]
    cnt_ref[...] = rank[tm - 1:tm, :]


def _route(x, mod, router_w, router_b, tm=512):
    g, s, d = x.shape
    n = g * s
    per = s // tm
    rw = jnp.pad(router_w, ((0, 0), (0, LANES - N_EXPERTS)))
    rb = jnp.pad(router_b, (0, LANES - N_EXPERTS)).reshape(1, LANES)
    return pl.pallas_call(
        _route_kernel,
        out_shape=(jax.ShapeDtypeStruct((n, d), BF16), jax.ShapeDtypeStruct((n, LANES), F32),
                   jax.ShapeDtypeStruct((n, LANES), F32), jax.ShapeDtypeStruct((1, LANES), F32)),
        grid=(n // tm,),
        in_specs=[
            pl.BlockSpec((None, tm, d), lambda i: (i // per, i % per, 0)),
            pl.BlockSpec((None, 6, d), lambda i: (i // per, 0, 0)),
            pl.BlockSpec((d, LANES), lambda i: (0, 0)),
            pl.BlockSpec((1, LANES), lambda i: (0, 0)),
        ],
        out_specs=(pl.BlockSpec((tm, d), lambda i: (i, 0)), pl.BlockSpec((tm, LANES), lambda i: (i, 0)),
                   pl.BlockSpec((tm, LANES), lambda i: (i, 0)), pl.BlockSpec((1, LANES), lambda i: (0, 0))),
        scratch_shapes=[pltpu.VMEM((tm, tm), BF16), pltpu.VMEM((1, LANES), F32)],
        compiler_params=_cparams(("arbitrary",)),
        name="moe_route",
    )(x, mod, rw, rb)


def _moe_plan(srank, counts, n_blocks, n_items):
    e_n = N_EXPERTS
    n = srank.shape[0]
    i32 = jnp.int32
    cnt = counts[0, :e_n].astype(i32)
    nb = (cnt + MOE_BM - 1) // MOE_BM
    nb_incl = jnp.cumsum(nb)
    gstart = nb_incl - nb
    n_used = nb_incl[-1]
    sr = srank[:, :e_n]
    rank = jnp.abs(sr).astype(i32)
    pos_tok = jnp.where(sr > 0, rank - 1 + MOE_BM * gstart[None, :], -1)
    rank_t = rank.T

    r = jnp.arange(n_blocks, dtype=i32)
    used = r < n_used
    rc = jnp.minimum(r, n_used - 1)
    e_r = jnp.minimum(jnp.sum(nb_incl[None, :] <= rc[:, None], axis=1, dtype=i32), e_n - 1)
    b = rc - gstart[e_r]
    lo = b * MOE_BM + 1
    hi = jnp.minimum((b + 1) * MOE_BM, cnt[e_r])
    def find(e, v):
        return jnp.sum(rank_t[e] < v[:, None], axis=1, dtype=i32)

    jlo = jnp.where(used, find(e_r, lo) // MOE_TC, 0)
    jhi = jnp.where(used, find(e_r, hi) // MOE_TC, -1)

    n_tiles = n // MOE_TMC
    ends = rank[MOE_TMC - 1::MOE_TMC]
    starts = jnp.concatenate([jnp.zeros((1, e_n), i32), ends[:-1]], axis=0)
    fb = gstart[None, :] + starts // MOE_BM
    lb = gstart[None, :] + (ends - 1) // MOE_BM
    n_pe = jnp.where(ends > starts, lb - fb + 1, 0).reshape(-1)
    incl = jnp.cumsum(n_pe)
    off = incl - n_pe
    total = incl[-1]
    w = jnp.arange(n_items, dtype=i32)
    valid = w < total
    wc = jnp.minimum(w, total - 1)
    p = jnp.sum(incl[None, :] <= wc[:, None], axis=1, dtype=i32)
    it_tile = p // e_n
    it_e = p % e_n
    it_blk = fb.reshape(-1)[p] + (wc - off[p])
    tile_off = jnp.concatenate([off[::e_n], total[None]])
    it_first = (wc == tile_off[it_tile]).astype(i32)
    it_last = (wc == tile_off[it_tile + 1] - 1).astype(i32)
    return (pos_tok, e_r, jlo, jhi, n_used.reshape(1),
            it_tile, it_blk, it_e, it_first, it_last, valid.astype(i32))


def _experts_kernel(be_ref, jlo_ref, jhi_ref, nused_ref, h_ref, pos_ref, wgu_ref, wd_ref, y_ref, xg_ref):
    r = pl.program_id(0)

    @pl.when(r < nused_ref[0])
    def _():
        e = be_ref[r]
        slot = r * MOE_BM + lax.broadcasted_iota(jnp.int32, (MOE_BM, MOE_TC), 0)
        xg_ref[...] = jnp.zeros_like(xg_ref)

        def chunk(j, carry):
            onehot = jnp.where(pos_ref[e, pl.ds(j, 1), :] == slot, 1.0, 0.0).astype(BF16)
            rows = h_ref[pl.ds(pl.multiple_of(j * MOE_TC, MOE_TC), MOE_TC), :]
            xg_ref[...] += jnp.dot(onehot, rows, preferred_element_type=F32)
            return carry

        lax.fori_loop(jlo_ref[r], jhi_ref[r] + 1, chunk, 0)
        y_ref[...] = _swiglu_chunks(xg_ref[...].astype(BF16), wgu_ref, wd_ref, D_FF_EXPERT)

    @pl.when(r >= nused_ref[0])
    def _():
        y_ref[...] = jnp.zeros_like(y_ref)


def _experts(h, pos_row, plan, wgu_bf, wd_bf, n_blocks):
    n, d = h.shape
    e_r, jlo, jhi, n_used = plan
    grid_spec = pltpu.PrefetchScalarGridSpec(
        num_scalar_prefetch=4,
        grid=(n_blocks,),
        in_specs=[
            _resident((n, d), lambda r, be, lo, hi, nu: (0, 0)),
            _resident(pos_row.shape, lambda r, be, lo, hi, nu: (0, 0, 0)),
            pl.BlockSpec((None, d, 2 * D_FF_EXPERT), lambda r, be, lo, hi, nu: (be[r], 0, 0),
                         pipeline_mode=pl.Buffered(1)),
            pl.BlockSpec((None, D_FF_EXPERT, d), lambda r, be, lo, hi, nu: (be[r], 0, 0)),
        ],
        out_specs=pl.BlockSpec((MOE_BM, d), lambda r, be, lo, hi, nu: (r, 0)),
        scratch_shapes=[pltpu.VMEM((MOE_BM, d), F32)],
    )
    return pl.pallas_call(
        _experts_kernel,
        out_shape=jax.ShapeDtypeStruct((n_blocks * MOE_BM, d), F32),
        grid_spec=grid_spec,
        compiler_params=_cparams(("arbitrary",)),
        name="moe_experts",
    )(e_r, jlo, jhi, n_used, h, pos_row, wgu_bf, wd_bf)


def _combine_kernel(tile_ref, blk_ref, e_ref, first_ref, last_ref, valid_ref,
                    y_ref, pos_ref, cw_ref, x_ref, mod_ref, g_ref, b_ref, o_ref, acc_ref):
    w = pl.program_id(0)

    @pl.when(valid_ref[w] == 1)
    def _():
        @pl.when(first_ref[w] == 1)
        def _():
            acc_ref[...] = jnp.zeros_like(acc_ref)

        e = e_ref[w]
        lane = lax.broadcasted_iota(jnp.int32, (MOE_TMC, LANES), 1)
        pos_e = jnp.sum(jnp.where(lane == e, pos_ref[...], 0.0), axis=-1, keepdims=True)
        cw_e = jnp.sum(jnp.where(lane == e, cw_ref[...], 0.0), axis=-1, keepdims=True)
        slot = (blk_ref[w] * MOE_BM + lax.broadcasted_iota(jnp.int32, (MOE_TMC, MOE_BM), 1)).astype(F32)
        onehot = jnp.where(pos_e == slot, 1.0, 0.0).astype(BF16)
        y_hi, y_lo = _split_bf16(y_ref[...])
        part = (jnp.dot(onehot, y_hi, preferred_element_type=F32)
                + jnp.dot(onehot, y_lo, preferred_element_type=F32))
        acc_ref[...] += cw_e * part

        @pl.when(last_ref[w] == 1)
        def _():
            z = ALPHA * x_ref[...] + mod_ref[5:6, :] * acc_ref[...]
            o_ref[...] = _layer_norm_rows(z, g_ref[...], b_ref[...])


def _combine(y, pos_tok_f, cw, x, mod, ln_g, ln_b, items, n_items):
    g, s, d = x.shape
    per = s // MOE_TMC

    def tok2(w, tile, *_):
        return (tile[w], 0)

    def tok3(w, tile, *_):
        return (tile[w] // per, tile[w] % per, 0)

    grid_spec = pltpu.PrefetchScalarGridSpec(
        num_scalar_prefetch=6,
        grid=(n_items,),
        in_specs=[
            pl.BlockSpec((MOE_BM, d), lambda w, tile, blk, *_: (blk[w], 0)),
            pl.BlockSpec((MOE_TMC, LANES), tok2),
            pl.BlockSpec((MOE_TMC, LANES), tok2),
            pl.BlockSpec((None, MOE_TMC, d), tok3),
            pl.BlockSpec((None, 6, d), lambda w, tile, *_: (tile[w] // per, 0, 0)),
            pl.BlockSpec((1, d), lambda w, *_: (0, 0)),
            pl.BlockSpec((1, d), lambda w, *_: (0, 0)),
        ],
        out_specs=pl.BlockSpec((None, MOE_TMC, d), tok3),
        scratch_shapes=[pltpu.VMEM((MOE_TMC, d), F32)],
    )
    return pl.pallas_call(
        _combine_kernel,
        out_shape=jax.ShapeDtypeStruct((g, s, d), F32),
        grid_spec=grid_spec,
        compiler_params=_cparams(("arbitrary",)),
        name="moe_combine",
    )(*items, y, pos_tok_f, cw, x, mod, ln_g.reshape(1, d), ln_b.reshape(1, d))


def _moe(x, mod, router_w, router_b, wgu_bf, wd_bf, ln_g, ln_b):
    g, s, d = x.shape
    n = g * s
    n_slots = 2 * n
    n_blocks = n_slots // MOE_BM + N_EXPERTS
    n_items = n_blocks + N_EXPERTS * (n // MOE_TMC)
    h, cw, srank, counts = _route(x, mod, router_w, router_b)
    plan = _moe_plan(srank, counts, n_blocks, n_items)
    pos_tok = plan[0]
    pos_row = pos_tok.T.reshape(N_EXPERTS, n // MOE_TC, MOE_TC)
    y = _experts(h, pos_row, plan[1:5], wgu_bf, wd_bf, n_blocks)
    pos_tok_f = jnp.pad(pos_tok.astype(F32), ((0, 0), (0, LANES - N_EXPERTS)), constant_values=-1.0)
    return _combine(y, pos_tok_f, cw, x, mod, ln_g, ln_b, plan[5:], n_items)


META_LANES = LANES


def _router_top2(h, rw_ref, rb_ref):
    h_hi, h_lo = _split_bf16(h)
    w_hi, w_lo = _split_bf16(rw_ref[...])
    logits = (jnp.dot(h_hi, w_hi, preferred_element_type=F32)
              + jnp.dot(h_lo, w_hi, preferred_element_type=F32)
              + jnp.dot(h_hi, w_lo, preferred_element_type=F32)) + rb_ref[...]
    lane = lax.broadcasted_iota(jnp.int32, logits.shape, 1).astype(F32)
    neg = -jnp.inf
    logits = jnp.where(lane < N_EXPERTS, logits, neg)
    m1 = jnp.max(logits, axis=-1, keepdims=True)
    i1 = jnp.min(jnp.where(logits == m1, lane, float(LANES)), axis=-1, keepdims=True)
    rest = jnp.where(lane == i1, neg, logits)
    m2 = jnp.max(rest, axis=-1, keepdims=True)
    i2 = jnp.min(jnp.where(rest == m2, lane, float(LANES)), axis=-1, keepdims=True)
    e2 = jnp.exp(m2 - m1)
    return lane, i1, i2, 1.0 / (1.0 + e2), e2 / (1.0 + e2)


def _route2_kernel(x_ref, mod_ref, rw_ref, rb_ref, h_ref, wk_ref, srank_ref, cnt_ref, tri_ref, run_ref):
    tm, d = x_ref.shape

    @pl.when(pl.program_id(0) == 0)
    def _():
        r = lax.broadcasted_iota(jnp.int32, (tm, tm), 0)
        c = lax.broadcasted_iota(jnp.int32, (tm, tm), 1)
        tri_ref[...] = jnp.where(c <= r, 1.0, 0.0).astype(BF16)
        run_ref[...] = jnp.zeros_like(run_ref)

    h = x_ref[...] * (1.0 + mod_ref[4:5, :]) + mod_ref[3:4, :]
    h_ref[:, 0:d] = h.astype(BF16)
    lane, i1, i2, w1, w2 = _router_top2(h, rw_ref, rb_ref)
    first_is_low = i1 < i2
    e_hi = jnp.where(first_is_low, i2, i1)
    wk_ref[...] = jnp.where(lane == 0.0, jnp.where(first_is_low, w1, w2),
                            jnp.where(lane == 1.0, jnp.where(first_is_low, w2, w1), 0.0))
    tok = (pl.program_id(0) * tm + lax.broadcasted_iota(jnp.int32, (tm, META_LANES), 0))
    meta = jnp.where(lane == 0.0, (tok // 128).astype(F32),
                     jnp.where(lane == 1.0, (tok % 128).astype(F32),
                               jnp.where(lane == 2.0, e_hi, jnp.where(lane == 3.0, 1.0, 0.0))))
    h_ref[:, d:d + META_LANES] = meta.astype(BF16)

    member = (lane == i1) | (lane == i2)
    mem = jnp.where(member, 1.0, 0.0)
    rank = jnp.dot(tri_ref[...], mem.astype(BF16), preferred_element_type=F32) + run_ref[...]
    srank_ref[...] = jnp.where(member, rank, -rank)
    run_ref[...] = rank[tm - 1:tm, :]
    cnt_ref[...] = rank[tm - 1:tm, :]


def _route2(x, mod, router_w, router_b, tm=512):
    g, s, d = x.shape
    n = g * s
    assert n <= 128 * 256
    per = s // tm
    rw = jnp.pad(router_w, ((0, 0), (0, LANES - N_EXPERTS)))
    rb = jnp.pad(router_b, (0, LANES - N_EXPERTS)).reshape(1, LANES)
    return pl.pallas_call(
        _route2_kernel,
        out_shape=(jax.ShapeDtypeStruct((n, d + META_LANES), BF16), jax.ShapeDtypeStruct((n, LANES), F32),
                   jax.ShapeDtypeStruct((n, LANES), F32), jax.ShapeDtypeStruct((1, LANES), F32)),
        grid=(n // tm,),
        in_specs=[
            pl.BlockSpec((None, tm, d), lambda i: (i // per, i % per, 0)),
            pl.BlockSpec((None, 6, d), lambda i: (i // per, 0, 0)),
            pl.BlockSpec((d, LANES), lambda i: (0, 0)),
            pl.BlockSpec((1, LANES), lambda i: (0, 0)),
        ],
        out_specs=(pl.BlockSpec((tm, d + META_LANES), lambda i: (i, 0)), pl.BlockSpec((tm, LANES), lambda i: (i, 0)),
                   pl.BlockSpec((tm, LANES), lambda i: (i, 0)), pl.BlockSpec((1, LANES), lambda i: (0, 0))),
        scratch_shapes=[pltpu.VMEM((tm, tm), BF16), pltpu.VMEM((1, LANES), F32)],
        compiler_params=_cparams(("arbitrary",)),
        name="moe_route",
    )(x, mod, rw, rb)


def _moe_plan2(srank, counts, n_blocks):
    e_n = N_EXPERTS
    i32 = jnp.int32
    cnt = counts[0, :e_n].astype(i32)
    nb = (cnt + MOE_BM - 1) // MOE_BM
    nb_incl = jnp.cumsum(nb)
    gstart = nb_incl - nb
    n_used = nb_incl[-1]
    sr = srank[:, :e_n]
    rank = jnp.abs(sr).astype(i32)
    pos_tok = jnp.where(sr > 0, rank - 1 + MOE_BM * gstart[None, :], -1)
    chunk_end = rank[MOE_TC - 1::MOE_TC].T

    r = jnp.arange(n_blocks, dtype=i32)
    used = r < n_used
    rc = jnp.minimum(r, n_used - 1)
    e_r = jnp.minimum(jnp.sum(nb_incl[None, :] <= rc[:, None], axis=1, dtype=i32), e_n - 1)
    b = rc - gstart[e_r]
    lo = b * MOE_BM + 1
    hi = jnp.minimum((b + 1) * MOE_BM, cnt[e_r])
    ends_r = chunk_end[e_r]
    jlo = jnp.where(used, jnp.sum(ends_r < lo[:, None], axis=1, dtype=i32), 0)
    jhi = jnp.where(used, jnp.sum(ends_r < hi[:, None], axis=1, dtype=i32), -1)
    return pos_tok, e_r, jlo, jhi, n_used.reshape(1)


def _experts2_kernel(be_ref, jlo_ref, jhi_ref, nused_ref, h_ref, pos_ref, wgu_ref, wd_ref, y_ref, meta_ref, xg_ref):
    r = pl.program_id(0)
    d = y_ref.shape[1]

    @pl.when(r < nused_ref[0])
    def _():
        e = be_ref[r]
        slot = r * MOE_BM + lax.broadcasted_iota(jnp.int32, (MOE_BM, MOE_TC), 0)
        xg_ref[...] = jnp.zeros_like(xg_ref)

        def chunk(j, carry):
            onehot = jnp.where(pos_ref[e, pl.ds(j, 1), :] == slot, 1.0, 0.0).astype(BF16)
            rows = h_ref[pl.ds(pl.multiple_of(j * MOE_TC, MOE_TC), MOE_TC), :]
            xg_ref[...] += jnp.dot(onehot, rows, preferred_element_type=F32)
            return carry

        lax.fori_loop(jlo_ref[r], jhi_ref[r] + 1, chunk, 0)
        meta_ref[...] = xg_ref[:, d:d + META_LANES]
        y_ref[...] = _swiglu_chunks(xg_ref[:, 0:d].astype(BF16), wgu_ref, wd_ref, D_FF_EXPERT)

    @pl.when(r >= nused_ref[0])
    def _():
        y_ref[...] = jnp.zeros_like(y_ref)
        meta_ref[...] = jnp.zeros_like(meta_ref)


def _experts2(h, pos_row, plan, wgu_bf, wd_bf, n_blocks):
    n, dx = h.shape
    d = dx - META_LANES
    e_r, jlo, jhi, n_used = plan
    grid_spec = pltpu.PrefetchScalarGridSpec(
        num_scalar_prefetch=4,
        grid=(n_blocks,),
        in_specs=[
            _resident((n, dx), lambda r, be, lo, hi, nu: (0, 0)),
            _resident(pos_row.shape, lambda r, be, lo, hi, nu: (0, 0, 0)),
            pl.BlockSpec((None, d, 2 * D_FF_EXPERT), lambda r, be, lo, hi, nu: (be[r], 0, 0),
                         pipeline_mode=pl.Buffered(1)),
            pl.BlockSpec((None, D_FF_EXPERT, d), lambda r, be, lo, hi, nu: (be[r], 0, 0),
                         pipeline_mode=pl.Buffered(1)),
        ],
        out_specs=(pl.BlockSpec((MOE_BM, d), lambda r, be, lo, hi, nu: (r, 0)),
                   pl.BlockSpec((MOE_BM, META_LANES), lambda r, be, lo, hi, nu: (r, 0))),
        scratch_shapes=[pltpu.VMEM((MOE_BM, dx), F32)],
    )
    return pl.pallas_call(
        _experts2_kernel,
        out_shape=(jax.ShapeDtypeStruct((n_blocks * MOE_BM, d), F32),
                   jax.ShapeDtypeStruct((n_blocks * MOE_BM, META_LANES), F32)),
        grid_spec=grid_spec,
        compiler_params=_cparams(("arbitrary",)),
        name="moe_experts",
    )(e_r, jlo, jhi, n_used, h, pos_row, wgu_bf, wd_bf)


def _scatter_kernel(dest_ref, y_ref, o_ref, ybuf_ref, sem):
    r = pl.program_id(0)
    n_steps = pl.num_programs(0)
    buf = r % 2

    def wait_block(b):
        pltpu.make_async_copy(ybuf_ref.at[b], o_ref.at[pl.ds(0, MOE_BM), :], sem.at[b]).wait()

    @pl.when(r >= 2)
    def _():
        wait_block(buf)

    ybuf_ref[buf] = y_ref[...]

    for i in range(MOE_BM):
        pltpu.make_async_copy(ybuf_ref.at[buf, pl.ds(i, 1), :], o_ref.at[pl.ds(dest_ref[0, i], 1), :],
                              sem.at[buf]).start()

    @pl.when(r == n_steps - 1)
    def _():
        wait_block(1 - buf)
        wait_block(buf)


def _scatter_rows(y, dest, n_rows_out):
    n_slots, d = y.shape
    n_blocks = n_slots // MOE_BM
    assert n_blocks >= 2
    return pl.pallas_call(
        _scatter_kernel,
        out_shape=jax.ShapeDtypeStruct((n_rows_out, d), F32),
        grid=(n_blocks,),
        in_specs=[
            pl.BlockSpec((None, 1, MOE_BM), lambda r: (r, 0, 0), memory_space=pltpu.SMEM),
            pl.BlockSpec((MOE_BM, d), lambda r: (r, 0)),
        ],
        out_specs=pl.BlockSpec(memory_space=pl.ANY),
        scratch_shapes=[pltpu.VMEM((2, MOE_BM, d), F32), pltpu.SemaphoreType.DMA((2,))],
        compiler_params=_cparams(("arbitrary",)),
        name="moe_scatter",
    )(dest.reshape(n_blocks, 1, MOE_BM), y)


def _finish_kernel(a_ref, b_ref, wk_ref, x_ref, mod_ref, g_ref, bb_ref, ol_ref, oc_ref, *, n_latent_tiles):
    lane = lax.broadcasted_iota(jnp.int32, wk_ref.shape, 1)
    wk = wk_ref[...]
    w0 = jnp.sum(jnp.where(lane == 0, wk, 0.0), axis=-1, keepdims=True)
    w1 = jnp.sum(jnp.where(lane == 1, wk, 0.0), axis=-1, keepdims=True)
    acc = w0 * a_ref[...] + w1 * b_ref[...]
    z = ALPHA * x_ref[...] + mod_ref[5:6, :] * acc
    out = _layer_norm_rows(z, g_ref[...], bb_ref[...])
    is_latent = pl.program_id(0) < n_latent_tiles

    @pl.when(is_latent)
    def _():
        ol_ref[...] = out

    @pl.when(jnp.logical_not(is_latent))
    def _():
        oc_ref[...] = out


def _finish(rows, wk, x, mod, ln_g, ln_b, n_latent_groups, tm=512):
    g, s, d = x.shape
    n = g * s
    per = s // tm
    nt = n // tm
    nlt = n_latent_groups * per

    def lat_map(i):
        j = jnp.minimum(i, nlt - 1)
        return (j // per, j % per, 0)

    def ctx_map(i):
        j = jnp.maximum(i - nlt, 0)
        return (j // per, j % per, 0)

    return pl.pallas_call(
        functools.partial(_finish_kernel, n_latent_tiles=nlt),
        out_shape=(jax.ShapeDtypeStruct((n_latent_groups, s, d), F32),
                   jax.ShapeDtypeStruct((g - n_latent_groups, s, d), F32)),
        grid=(nt,),
        in_specs=[
            pl.BlockSpec((tm, d), lambda i: (i, 0)),
            pl.BlockSpec((tm, d), lambda i: (nt + i, 0)),
            pl.BlockSpec((tm, LANES), lambda i: (i, 0)),
            pl.BlockSpec((None, tm, d), lambda i: (i // per, i % per, 0)),
            pl.BlockSpec((None, 6, d), lambda i: (i // per, 0, 0)),
            pl.BlockSpec((1, d), lambda i: (0, 0)),
            pl.BlockSpec((1, d), lambda i: (0, 0)),
        ],
        out_specs=(pl.BlockSpec((None, tm, d), lat_map), pl.BlockSpec((None, tm, d), ctx_map)),
        compiler_params=_cparams(("arbitrary",)),
        name="moe_finish",
    )(rows, rows, wk, x, mod, ln_g.reshape(1, d), ln_b.reshape(1, d))


def _route3_kernel(x_ref, mod_ref, rw_ref, rb_ref, h_ref, meta_ref, wk_ref, srank_ref, cnt_ref, tri_ref, run_ref):
    tm, d = x_ref.shape

    @pl.when(pl.program_id(0) == 0)
    def _():
        r = lax.broadcasted_iota(jnp.int32, (tm, tm), 0)
        c = lax.broadcasted_iota(jnp.int32, (tm, tm), 1)
        tri_ref[...] = jnp.where(c <= r, 1.0, 0.0).astype(BF16)
        run_ref[...] = jnp.zeros_like(run_ref)

    h = x_ref[...] * (1.0 + mod_ref[4:5, :]) + mod_ref[3:4, :]
    _store_token_tiles(h_ref, h)
    lane, i1, i2, w1, w2 = _router_top2(h, rw_ref, rb_ref)
    first_is_low = i1 < i2
    e_hi = jnp.where(first_is_low, i2, i1)
    wk_ref[...] = jnp.where(lane == 0.0, jnp.where(first_is_low, w1, w2),
                            jnp.where(lane == 1.0, jnp.where(first_is_low, w2, w1), 0.0))
    tok = (pl.program_id(0) * tm + lax.broadcasted_iota(jnp.int32, (tm, META_LANES), 0))
    meta = jnp.where(lane == 0.0, (tok // 128).astype(F32),
                     jnp.where(lane == 1.0, (tok % 128).astype(F32),
                               jnp.where(lane == 2.0, e_hi, jnp.where(lane == 3.0, 1.0, 0.0))))
    meta_ref[...] = meta.astype(BF16)

    member = (lane == i1) | (lane == i2)
    mem = jnp.where(member, 1.0, 0.0)
    rank = jnp.dot(tri_ref[...], mem.astype(BF16), preferred_element_type=F32) + run_ref[...]
    srank_ref[...] = jnp.where(member, rank, -rank)
    run_ref[...] = rank[tm - 1:tm, :]
    cnt_ref[...] = rank[tm - 1:tm, :]


def _route3(x, mod, router_w, router_b, tm=512):
    g, s, d = x.shape
    n = g * s
    assert n <= 128 * 256
    per = s // tm
    rw = jnp.pad(router_w, ((0, 0), (0, LANES - N_EXPERTS)))
    rb = jnp.pad(router_b, (0, LANES - N_EXPERTS)).reshape(1, LANES)
    tok_lanes = lambda i: (i, 0)
    return pl.pallas_call(
        _route3_kernel,
        out_shape=(jax.ShapeDtypeStruct((n, SUBLANES, LANES), F32), jax.ShapeDtypeStruct((n, META_LANES), BF16),
                   jax.ShapeDtypeStruct((n, LANES), F32), jax.ShapeDtypeStruct((n, LANES), F32),
                   jax.ShapeDtypeStruct((1, LANES), F32)),
        grid=(n // tm,),
        in_specs=[
            pl.BlockSpec((None, tm, d), lambda i: (i // per, i % per, 0)),
            pl.BlockSpec((None, 6, d), lambda i: (i // per, 0, 0)),
            pl.BlockSpec((d, LANES), lambda i: (0, 0)),
            pl.BlockSpec((1, LANES), lambda i: (0, 0)),
        ],
        out_specs=(pl.BlockSpec((tm, SUBLANES, LANES), lambda i: (i, 0, 0)), pl.BlockSpec((tm, META_LANES), tok_lanes),
                   pl.BlockSpec((tm, LANES), tok_lanes), pl.BlockSpec((tm, LANES), tok_lanes),
                   pl.BlockSpec((1, LANES), lambda i: (0, 0))),
        scratch_shapes=[pltpu.VMEM((tm, tm), BF16), pltpu.VMEM((1, LANES), F32)],
        compiler_params=_cparams(("arbitrary",)),
        name="moe_route",
    )(x, mod, rw, rb)


def _slot_meta_kernel(be_ref, jlo_ref, jhi_ref, nused_ref, meta_ref, pos_ref, o_ref):
    r = pl.program_id(0)
    o_ref[...] = jnp.zeros_like(o_ref)

    @pl.when(r < nused_ref[0])
    def _():
        e = be_ref[r]
        slot = r * MOE_BM + lax.broadcasted_iota(jnp.int32, (MOE_BM, MOE_TC), 0)

        def chunk(j, carry):
            onehot = jnp.where(pos_ref[e, pl.ds(j, 1), :] == slot, 1.0, 0.0).astype(BF16)
            rows = meta_ref[pl.ds(pl.multiple_of(j * MOE_TC, MOE_TC), MOE_TC), :]
            o_ref[...] += jnp.dot(onehot, rows, preferred_element_type=F32)
            return carry

        lax.fori_loop(jlo_ref[r], jhi_ref[r] + 1, chunk, 0)


def _slot_meta(meta, pos_row, plan, n_blocks):
    n = meta.shape[0]
    e_r, jlo, jhi, n_used = plan
    grid_spec = pltpu.PrefetchScalarGridSpec(
        num_scalar_prefetch=4,
        grid=(n_blocks,),
        in_specs=[
            _resident((n, META_LANES), lambda r, be, lo, hi, nu: (0, 0)),
            _resident(pos_row.shape, lambda r, be, lo, hi, nu: (0, 0, 0)),
        ],
        out_specs=pl.BlockSpec((MOE_BM, META_LANES), lambda r, be, lo, hi, nu: (r, 0)),
    )
    return pl.pallas_call(
        _slot_meta_kernel,
        out_shape=jax.ShapeDtypeStruct((n_blocks * MOE_BM, META_LANES), F32),
        grid_spec=grid_spec,
        compiler_params=_cparams(("arbitrary",)),
        name="moe_slot_meta",
    )(e_r, jlo, jhi, n_used, meta, pos_row)


def _experts3_kernel(be_ref, nused_ref, src_ref, nsrc_ref, dest_ref, ldest_ref, h_ref, wgu_ref, wd_ref, o_ref,
                     xg_ref, ybuf_ref, sem_g, sem_s):
    r = pl.program_id(0)
    n_steps = pl.num_programs(0)
    cur = r % 2
    nxt = 1 - cur
    d_ff = D_FF_EXPERT

    def gather_row(i, idx_ref, buf):
        return pltpu.make_async_copy(h_ref.at[idx_ref[0, i]], xg_ref.at[buf, i], sem_g.at[buf])

    def scatter_row(i, buf):
        return pltpu.make_async_copy(ybuf_ref.at[buf, i], o_ref.at[dest_ref[0, i]], sem_s.at[buf])

    def wait_gather(buf):
        pltpu.make_async_copy(h_ref.at[pl.ds(0, MOE_BM)], xg_ref.at[buf], sem_g.at[buf]).wait()

    def wait_scatter(buf):
        pltpu.make_async_copy(ybuf_ref.at[buf], o_ref.at[pl.ds(0, MOE_BM)], sem_s.at[buf]).wait()

    zero_rows = jnp.zeros((MOE_BM, D_MODEL), F32)

    def issue_rows(lo, hi):
        for i in range(lo, hi):
            gather_row(i, nsrc_ref, nxt).start()
            scatter_row(i, nxt).start()

    @pl.when(r == 0)
    def _():
        _store_token_tiles(ybuf_ref, zero_rows, 1)

        def first(i, carry):
            gather_row(i, src_ref, 0).start()
            return carry

        lax.fori_loop(0, MOE_BM, first, 0, unroll=8)

    wait_gather(cur)

    @pl.when(r < nused_ref[0])
    def _():
        x_bf = _load_token_tiles(xg_ref, cur).astype(BF16)
        n_chunks = d_ff // FF_CHUNK
        per = -(-MOE_BM // n_chunks)
        acc = None
        for j in range(n_chunks):
            lo = j * FF_CHUNK
            gate = jnp.dot(x_bf, wgu_ref[:, lo:lo + FF_CHUNK], preferred_element_type=F32)
            up = jnp.dot(x_bf, wgu_ref[:, d_ff + lo:d_ff + lo + FF_CHUNK], preferred_element_type=F32)
            act = (gate * _sigmoid(gate) * up).astype(BF16)
            t = jnp.dot(act, wd_ref[lo:lo + FF_CHUNK, :], preferred_element_type=F32)
            acc = t if acc is None else acc + t
            issue_rows(min(j * per, MOE_BM), min((j + 1) * per, MOE_BM))

        @pl.when(r >= 1)
        def _():
            wait_scatter(cur)

        _store_token_tiles(ybuf_ref, acc, cur)

    @pl.when(r >= nused_ref[0])
    def _():
        def both(i, carry):
            gather_row(i, nsrc_ref, nxt).start()
            scatter_row(i, nxt).start()
            return carry

        lax.fori_loop(0, MOE_BM, both, 0, unroll=8)

        @pl.when(r >= 1)
        def _():
            wait_scatter(cur)

        _store_token_tiles(ybuf_ref, zero_rows, cur)

    @pl.when(r == n_steps - 1)
    def _():
        def last(i, carry):
            pltpu.make_async_copy(ybuf_ref.at[cur, i], o_ref.at[ldest_ref[0, i]], sem_s.at[cur]).start()
            return carry

        lax.fori_loop(0, MOE_BM, last, 0, unroll=8)
        wait_scatter(nxt)
        wait_scatter(cur)
        wait_gather(nxt)


def _experts3(h, src, dest, e_r, n_used, wgu_bf, wd_bf, n_blocks):
    d = D_MODEL
    last = n_blocks - 1
    tile_rows = (MOE_BM, SUBLANES, LANES)
    smem_blk = lambda f: pl.BlockSpec((None, 1, MOE_BM), f, memory_space=pltpu.SMEM)
    grid_spec = pltpu.PrefetchScalarGridSpec(
        num_scalar_prefetch=2,
        grid=(n_blocks,),
        in_specs=[
            smem_blk(lambda r, be, nu: (0, 0, 0)),
            smem_blk(lambda r, be, nu: (jnp.minimum(r + 1, last), 0, 0)),
            smem_blk(lambda r, be, nu: (r, 0, 0)),
            smem_blk(lambda r, be, nu: (n_blocks, 0, 0)),
            pl.BlockSpec(memory_space=pl.ANY),
            pl.BlockSpec((None, d, 2 * D_FF_EXPERT), lambda r, be, nu: (be[r], 0, 0)),
            pl.BlockSpec((None, D_FF_EXPERT, d), lambda r, be, nu: (be[r], 0, 0)),
        ],
        out_specs=pl.BlockSpec(memory_space=pl.ANY),
        scratch_shapes=[pltpu.VMEM((2,) + tile_rows, F32), pltpu.VMEM((2,) + tile_rows, F32),
                        pltpu.SemaphoreType.DMA((2,)), pltpu.SemaphoreType.DMA((2,))],
    )
    return pl.pallas_call(
        _experts3_kernel,
        out_shape=jax.ShapeDtypeStruct((n_blocks * MOE_BM + MOE_BM, SUBLANES, LANES), F32),
        grid_spec=grid_spec,
        compiler_params=_cparams(("arbitrary",)),
        name="moe_experts",
    )(e_r, n_used, src, src, dest, dest, h, wgu_bf, wd_bf)


def _moe3(x, mod, router_w, router_b, wgu_bf, wd_bf, ln_g, ln_b, n_latent_groups):
    g, s, d = x.shape
    n = g * s
    n_blocks = 2 * n // MOE_BM + N_EXPERTS
    n_slots = n_blocks * MOE_BM
    h, meta_tok, wk, srank, counts = _route3(x, mod, router_w, router_b)
    pos_tok, e_r, jlo, jhi, n_used = _moe_plan2(srank, counts, n_blocks)
    pos_row = pos_tok.T.reshape(N_EXPERTS, n // MOE_TC, MOE_TC)
    meta = _slot_meta(meta_tok, pos_row, (e_r, jlo, jhi, n_used), n_blocks)
    i32 = jnp.int32
    tok = (meta[:, 0] * 128.0 + meta[:, 1]).astype(i32)
    choice = (meta[:, 2] == jnp.repeat(e_r, MOE_BM).astype(F32)).astype(i32)
    unused = meta[:, 3] < 0.5
    spare = 2 * n + jnp.cumsum(unused.astype(i32)) - 1
    dest = jnp.where(unused, spare, choice * n + tok).reshape(n_blocks, 1, MOE_BM)
    placeholder = (n_slots + jnp.arange(MOE_BM, dtype=i32)).reshape(1, 1, MOE_BM)
    dest_tab = jnp.concatenate([placeholder, dest], axis=0)
    rows = _experts3(h, tok.reshape(n_blocks, 1, MOE_BM), dest_tab, e_r, n_used, wgu_bf, wd_bf, n_blocks)
    return _finish(rows, wk, x, mod, ln_g, ln_b, n_latent_groups)


def _moe2(x, mod, router_w, router_b, wgu_bf, wd_bf, ln_g, ln_b, n_latent_groups):
    g, s, d = x.shape
    n = g * s
    n_blocks = 2 * n // MOE_BM + N_EXPERTS
    n_slots = n_blocks * MOE_BM
    h, wk, srank, counts = _route2(x, mod, router_w, router_b)
    pos_tok, e_r, jlo, jhi, n_used = _moe_plan2(srank, counts, n_blocks)
    pos_row = pos_tok.T.reshape(N_EXPERTS, n // MOE_TC, MOE_TC)
    y, meta = _experts2(h, pos_row, (e_r, jlo, jhi, n_used), wgu_bf, wd_bf, n_blocks)
    tok = (meta[:, 0] * 128.0 + meta[:, 1]).astype(jnp.int32)
    choice = (meta[:, 2] == jnp.repeat(e_r, MOE_BM).astype(F32)).astype(jnp.int32)
    unused = meta[:, 3] < 0.5
    spare = 2 * n + jnp.cumsum(unused.astype(jnp.int32)) - 1
    dest = jnp.where(unused, spare, choice * n + tok)
    rows = _scatter_rows(y, dest, n_slots)
    return _finish(rows, wk, x, mod, ln_g, ln_b, n_latent_groups)


def _inproj_c_kernel(x_ref, mod_ref, w_ref, wg_ref, bg_ref, o_ref, og_ref):
    h = (x_ref[...] * (1.0 + mod_ref[1:2, :]) + mod_ref[0:1, :]).astype(BF16)
    o_ref[...] = jnp.dot(h, w_ref[...], preferred_element_type=F32)
    og_ref[...] = jnp.dot(h, wg_ref[...], preferred_element_type=F32) + bg_ref[...]


def _inproj_c(x, mod, w_bf, wg_bf, bg, tm=512):
    g, s, d = x.shape
    n = w_bf.shape[1]
    ng = wg_bf.shape[1]
    return pl.pallas_call(
        _inproj_c_kernel,
        out_shape=(jax.ShapeDtypeStruct((g, s, n), F32), jax.ShapeDtypeStruct((g, s, ng), F32)),
        grid=(g, s // tm),
        in_specs=[
            pl.BlockSpec((None, tm, d), lambda gi, ti: (gi, ti, 0)),
            pl.BlockSpec((None, 6, d), lambda gi, ti: (gi, 0, 0)),
            _resident((d, n), lambda gi, ti: (0, 0)),
            _resident((d, ng), lambda gi, ti: (0, 0)),
            pl.BlockSpec((1, ng), lambda gi, ti: (0, 0)),
        ],
        out_specs=(pl.BlockSpec((None, tm, n), lambda gi, ti: (gi, ti, 0)),
                   pl.BlockSpec((None, tm, ng), lambda gi, ti: (gi, ti, 0))),
        compiler_params=_cparams(("arbitrary", "arbitrary")),
        name="inproj_c",
    )(x, mod, w_bf, wg_bf, bg)


def _log_sigmoid(x):
    return jnp.minimum(x, 0.0) - jnp.log(1.0 + jnp.exp(-jnp.abs(x)))


MLSTM_L = 128


def _split3_bf16(x):
    hi = x.astype(BF16)
    r1 = x - hi.astype(F32)
    mid = r1.astype(BF16)
    lo = (r1 - mid.astype(F32)).astype(BF16)
    return hi, mid, lo


def _mlstm_kernel(*refs, seq, hg, has_init, emit_state):
    q_ref, k_ref, v_ref, o_ref, gi_ref, gf_ref, hgain_ref = refs[:7]
    pos = 7
    if has_init:
        c0_ref, n0_ref, m0_ref = refs[pos:pos + 3]
        pos += 3
    out_ref = refs[pos]
    pos += 1
    if emit_state:
        co_ref, no_ref, mo_ref = refs[pos:pos + 3]
        pos += 3
    cext_ref, hf_ref, hb_ref, b_ref, g_ref, gmax_ref, mt_ref, wi_ref, en_ref, ws_ref, gt_ref, wc_ref = refs[pos:]

    L = MLSTM_L
    dh = MLSTM_DH
    nh = MLSTM_HEADS
    nc = seq // L
    head0 = pl.program_id(1) * hg
    neg = -jnp.inf

    lane = lax.broadcasted_iota(jnp.int32, (L, LANES), 1)
    lane1 = lax.broadcasted_iota(jnp.int32, (1, LANES), 1)
    row = lax.broadcasted_iota(jnp.int32, (L, L), 0)
    col = lax.broadcasted_iota(jnp.int32, (L, L), 1)
    lower = col <= row
    upper = col >= row
    tri_l = jnp.where(lower, 1.0, 0.0).astype(BF16)
    tri_u = jnp.where(upper, 1.0, 0.0).astype(BF16)
    fwd_lane = lane < nh
    fwd_lane1 = lane1 < nh
    trow = lax.broadcasted_iota(jnp.int32, (L, LANES), 0)

    btot, glast = [], []
    for c in range(nc):
        rows = slice(c * L, (c + 1) * L)
        f = _log_sigmoid(gf_ref[rows, :])
        parts = _split3_bf16(f)
        pre = sum(jnp.dot(tri_l, p, preferred_element_type=F32) for p in parts)
        suf = sum(jnp.dot(tri_u, p, preferred_element_type=F32) for p in parts)
        b = jnp.where(fwd_lane, pre, suf)
        g = gi_ref[rows, :] - b
        gp, gs = g, g
        k = 1
        while k < L:
            gp = jnp.where(trow >= k, jnp.maximum(gp, pltpu.roll(gp, k, 0)), gp)
            gs = jnp.where(trow < L - k, jnp.maximum(gs, pltpu.roll(gs, L - k, 0)), gs)
            k *= 2
        gmax = jnp.where(fwd_lane, gp, gs)
        b_ref[rows, :] = b
        g_ref[rows, :] = g
        gmax_ref[rows, :] = gmax
        btot.append(jnp.where(fwd_lane1, b[L - 1:L, :], b[0:1, :]))
        glast.append(jnp.where(fwd_lane1, gmax[L - 1:L, :], gmax[0:1, :]))

    m_init = m0_ref[...] if has_init else jnp.zeros((1, LANES), F32)
    mf, mb = m_init, m_init
    ms_f, mn_f, ms_b, mn_b = [None] * nc, [None] * nc, [None] * nc, [None] * nc
    for c in range(nc):
        ms_f[c] = mf
        mf = btot[c] + jnp.maximum(mf, glast[c])
        mn_f[c] = mf
        cb = nc - 1 - c
        ms_b[cb] = mb
        mb = btot[cb] + jnp.maximum(mb, glast[cb])
        mn_b[cb] = mb
    m_final = jnp.where(fwd_lane1, mf, mb)

    for c in range(nc):
        rows = slice(c * L, (c + 1) * L)
        m_start = jnp.where(fwd_lane1, ms_f[c], ms_b[c])
        m_next = jnp.where(fwd_lane1, mn_f[c], mn_b[c])
        g = g_ref[rows, :]
        mt = jnp.maximum(m_start, gmax_ref[rows, :])
        mt_ref[rows, :] = mt
        wi_ref[rows, :] = jnp.exp(m_start - mt)
        en_ref[rows, :] = jnp.exp(-(b_ref[rows, :] + mt))
        ws_ref[rows, :] = jnp.exp(btot[c] + g - m_next)
        gt_ref[c] = g.T
        wc_ref[c:c + 1, :] = jnp.exp(btot[c] + m_start - m_next)

    lane_d = lax.broadcasted_iota(jnp.int32, (dh, dh), 1)
    for d in range(2):
        for hh in range(hg):
            idx = d * hg + hh
            if has_init:
                cext_ref[idx, :, 0:dh] = c0_ref[d, hh]
                n0_tile = jnp.where(lax.broadcasted_iota(jnp.int32, (dh, dh), 0) == 0, n0_ref[d, hh], 0.0)
                cext_ref[idx, :, dh:2 * dh] = n0_tile.T
            else:
                cext_ref[idx] = jnp.zeros((dh, 2 * dh), F32)

    ones_col = jnp.where(lane == 0, 1.0, 0.0).astype(BF16)
    nt = (((1,), (1,)), ((), ()))
    tn = (((0,), (0,)), ((), ()))

    def column(x, j):
        return jnp.sum(jnp.where(lane == j, x, 0.0), axis=-1, keepdims=True)

    def one_direction(d, hh, c, s_qk, q_bf, k_s, v_ext, v_bf):
        idx = d * hg + hh
        j = d * nh + head0 + hh
        rows = pl.ds(pl.multiple_of(c * L, L), L)
        mt = column(mt_ref[rows, :], j)
        wi = column(wi_ref[rows, :], j)
        en = column(en_ref[rows, :], j)
        ws = column(ws_ref[rows, :], j)
        g_r = gt_ref[c, pl.ds(j, 1), :]
        w_c = jnp.sum(jnp.where(lane1 == j, wc_ref[pl.ds(c, 1), :], 0.0), axis=-1, keepdims=True)
        causal = lower if d == 0 else upper
        p = s_qk * jnp.exp(jnp.where(causal, g_r - mt, neg))
        qc = jnp.dot(q_bf, cext_ref[idx].astype(BF16), preferred_element_type=F32)
        num = wi * qc[:, 0:dh] + jnp.dot(p.astype(BF16), v_bf, preferred_element_type=F32)
        den = wi * qc[:, dh:dh + 1] + jnp.sum(p, axis=-1, keepdims=True)
        h = num / jnp.maximum(jnp.abs(den), en)
        upd = lax.dot_general((ws * k_s).astype(BF16), v_ext, tn, preferred_element_type=F32)
        cext_ref[idx] = w_c * cext_ref[idx] + upd
        return h

    def load_chunk(hh, c):
        sl = (pl.ds(pl.multiple_of(c * L, L), L), slice(hh * dh, (hh + 1) * dh))
        q_bf = q_ref[sl].astype(BF16)
        k_s = k_ref[sl] * (dh ** -0.5)
        v_bf = v_ref[sl].astype(BF16)
        v_ext = jnp.concatenate([v_bf, ones_col], axis=-1)
        s_qk = lax.dot_general(q_bf, k_s.astype(BF16), nt, preferred_element_type=F32)
        return s_qk, q_bf, k_s, v_ext, v_bf

    def step(c, carry):
        cb = nc - 1 - c
        for hh in range(hg):
            h = one_direction(0, hh, c, *load_chunk(hh, c))
            hf_ref[pl.ds(pl.multiple_of(c * L, L), L), hh * dh:(hh + 1) * dh] = h
        for hh in range(hg):
            h = one_direction(1, hh, cb, *load_chunk(hh, cb))
            hb_ref[pl.ds(pl.multiple_of(cb * L, L), L), hh * dh:(hh + 1) * dh] = h
        return carry

    lax.fori_loop(0, nc, step, 0)

    for hh in range(hg):
        cs = slice(hh * dh, (hh + 1) * dh)
        hs = hf_ref[:, cs] + hb_ref[:, cs]
        mu = jnp.mean(hs, axis=-1, keepdims=True)
        hc = hs - mu
        var = jnp.mean(hc * hc, axis=-1, keepdims=True)
        hn = hc * lax.rsqrt(var + LN_EPS) * hgain_ref[:, cs]
        out_ref[:, cs] = (_sigmoid(o_ref[:, cs]) * hn).astype(out_ref.dtype)

    if emit_state:
        for d in range(2):
            for hh in range(hg):
                idx = d * hg + hh
                co_ref[d, hh] = cext_ref[idx, :, 0:dh]
                no_ref[d, hh] = cext_ref[idx, :, dh:2 * dh].T[0:1, :]
        mo_ref[...] = m_final


def _mlstm(proj, gates, head_g, g0, n_seq, seq, hg, init=None, emit_state=False):
    g, s, _ = proj.shape
    per_group = s // seq
    n_hg = MLSTM_HEADS // hg
    w = hg * MLSTM_DH
    nc = seq // MLSTM_L
    n_blocks = D_MODEL // w

    def tok_map(colblock):
        return lambda b, hi: (g0 + b // per_group, b % per_group, colblock * n_blocks + hi)

    def gate_map(half):
        return lambda b, hi: (g0 + b // per_group, b % per_group, half)

    args = [proj, proj, proj, proj, gates, gates, head_g.reshape(1, D_MODEL)]
    in_specs = [
        pl.BlockSpec((None, seq, w), tok_map(0)),
        pl.BlockSpec((None, seq, w), tok_map(1)),
        pl.BlockSpec((None, seq, w), tok_map(2)),
        pl.BlockSpec((None, seq, w), tok_map(3)),
        pl.BlockSpec((None, seq, LANES), gate_map(0)),
        pl.BlockSpec((None, seq, LANES), gate_map(1)),
        pl.BlockSpec((1, w), lambda b, hi: (0, hi)),
    ]
    if init is not None:
        c0, n0, m0 = init
        m0_lanes = jnp.pad(m0.reshape(n_seq, 1, 2 * MLSTM_HEADS), ((0, 0), (0, 0), (0, LANES - 2 * MLSTM_HEADS)))
        args += [c0, n0.reshape(n0.shape[:-1] + (1, MLSTM_DH)), m0_lanes]
        in_specs += [
            pl.BlockSpec((None, 2, hg, MLSTM_DH, MLSTM_DH), lambda b, hi: (b, 0, hi, 0, 0)),
            pl.BlockSpec((None, 2, hg, 1, MLSTM_DH), lambda b, hi: (b, 0, hi, 0, 0)),
            pl.BlockSpec((None, 1, LANES), lambda b, hi: (b, 0, 0)),
        ]

    out_shape = [jax.ShapeDtypeStruct((n_seq // per_group, s, D_MODEL), BF16)]
    out_specs = [pl.BlockSpec((None, seq, w), lambda b, hi: (b // per_group, b % per_group, hi))]
    if emit_state:
        out_shape += [
            jax.ShapeDtypeStruct((n_seq, 2, MLSTM_HEADS, MLSTM_DH, MLSTM_DH), F32),
            jax.ShapeDtypeStruct((n_seq, 2, MLSTM_HEADS, 1, MLSTM_DH), F32),
            jax.ShapeDtypeStruct((n_seq, n_hg, 1, LANES), F32),
        ]
        out_specs += [
            pl.BlockSpec((None, 2, hg, MLSTM_DH, MLSTM_DH), lambda b, hi: (b, 0, hi, 0, 0)),
            pl.BlockSpec((None, 2, hg, 1, MLSTM_DH), lambda b, hi: (b, 0, hi, 0, 0)),
            pl.BlockSpec((None, None, 1, LANES), lambda b, hi: (b, hi, 0, 0)),
        ]

    tok_scratch = pltpu.VMEM((seq, LANES), F32)
    return pl.pallas_call(
        functools.partial(_mlstm_kernel, seq=seq, hg=hg, has_init=init is not None, emit_state=emit_state),
        out_shape=tuple(out_shape),
        grid=(n_seq, n_hg),
        in_specs=in_specs,
        out_specs=tuple(out_specs),
        scratch_shapes=[
            pltpu.VMEM((2 * hg, MLSTM_DH, 2 * MLSTM_DH), F32),
            pltpu.VMEM((seq, w), F32),
            pltpu.VMEM((seq, w), F32),
        ] + [tok_scratch] * 7 + [
            pltpu.VMEM((nc, LANES, MLSTM_L), F32),
            pltpu.VMEM((max(nc, 8), LANES), F32),
        ],
        compiler_params=_cparams(("arbitrary", "arbitrary")),
        name="mlstm_%d" % seq,
    )(*args)


def _mlstm_kernel_old(*refs, seq, hg, has_init, emit_state):
    q_ref, k_ref, v_ref, o_ref, gc_ref, gr_ref, hgain_ref = refs[:7]
    pos = 7
    if has_init:
        c0_ref, n0_ref, m0_ref = refs[pos:pos + 3]
        pos += 3
    out_ref = refs[pos]
    pos += 1
    if emit_state:
        co_ref, no_ref, mo_ref = refs[pos:pos + 3]
        pos += 3
    cext_ref, hf_ref, hb_ref = refs[pos:pos + 3]

    L = MLSTM_CHUNK
    dh = MLSTM_DH
    nc = seq // L
    head0 = pl.program_id(1) * hg
    neg = -jnp.inf

    lane_d = lax.broadcasted_iota(jnp.int32, (dh, dh), 1)
    for d in range(2):
        for hh in range(hg):
            idx = d * hg + hh
            if has_init:
                cext_ref[idx, :, 0:dh] = c0_ref[d, hh]
                n0_tile = jnp.where(lax.broadcasted_iota(jnp.int32, (dh, dh), 0) == 0, n0_ref[d, hh], 0.0)
                cext_ref[idx, :, dh:2 * dh] = n0_tile.T
            else:
                cext_ref[idx] = jnp.zeros((dh, 2 * dh), F32)

    row = lax.broadcasted_iota(jnp.int32, (L, L), 0)
    col = lax.broadcasted_iota(jnp.int32, (L, L), 1)
    lower = col <= row
    upper = col >= row
    lane_g = lax.broadcasted_iota(jnp.int32, (L, LANES), 1)
    ones_col = jnp.where(lane_g == 0, 1.0, 0.0).astype(BF16)
    nt = (((1,), (1,)), ((), ()))
    tn = (((0,), (0,)), ((), ()))

    def one_direction(d, hh, c, s_qk, q_bf, k_s, v_ext, v_bf, m_prev):
        idx = d * hg + hh
        head = head0 + hh
        causal, anti = (lower, upper) if d == 0 else (upper, lower)
        gates_c = gc_ref[pl.ds(c * L, L), :]

        def col_of(j):
            return jnp.sum(jnp.where(lane_g == j, gates_c, 0.0), axis=-1, keepdims=True)

        i_c = col_of((2 * d) * MLSTM_HEADS + head)
        f_c = _log_sigmoid(col_of((2 * d + 1) * MLSTM_HEADS + head))
        i_r = gr_ref[2 * d, hh, pl.ds(c, 1), :]
        f_r = _log_sigmoid(gr_ref[2 * d + 1, hh, pl.ds(c, 1), :])

        b_c = jnp.sum(jnp.where(causal, f_r, 0.0), axis=1, keepdims=True)
        b_r = jnp.sum(jnp.where(anti, f_c, 0.0), axis=0, keepdims=True)
        b_tot = jnp.sum(f_r, axis=1, keepdims=True)
        dmat = jnp.where(causal, b_c - b_r + i_r, neg)
        m_inter = b_c + m_prev
        m_t = jnp.maximum(m_inter, jnp.max(dmat, axis=-1, keepdims=True))
        w_inter = jnp.exp(m_inter - m_t)
        p = s_qk * jnp.exp(dmat - m_t)
        qc = jnp.dot(q_bf, cext_ref[idx].astype(BF16), preferred_element_type=F32)
        num = w_inter * qc[:, 0:dh] + jnp.dot(p.astype(BF16), v_bf, preferred_element_type=F32)
        den = w_inter * qc[:, dh:dh + 1] + jnp.sum(p, axis=-1, keepdims=True)
        h = num / jnp.maximum(jnp.abs(den), jnp.exp(-m_t))
        last = L - 1 if d == 0 else 0
        m_new = m_t[last:last + 1, :]
        w_c = jnp.exp(b_tot + m_prev - m_new)
        w_s = jnp.exp(b_tot - b_c + i_c - m_new)
        upd = lax.dot_general((w_s * k_s).astype(BF16), v_ext, tn, preferred_element_type=F32)
        cext_ref[idx] = w_c * cext_ref[idx] + upd
        return h, m_new

    def load_chunk(hh, c):
        sl = (pl.ds(c * L, L), slice(hh * dh, (hh + 1) * dh))
        q_bf = q_ref[sl].astype(BF16)
        k_s = k_ref[sl] * (dh ** -0.5)
        v_bf = v_ref[sl].astype(BF16)
        v_ext = jnp.concatenate([v_bf, ones_col], axis=-1)
        s_qk = lax.dot_general(q_bf, k_s.astype(BF16), nt, preferred_element_type=F32)
        return s_qk, q_bf, k_s, v_ext, v_bf

    def step(c, ms):
        cb = nc - 1 - c
        new_ms = []
        for hh in range(hg):
            h, m_new = one_direction(0, hh, c, *load_chunk(hh, c), ms[hh])
            hf_ref[pl.ds(c * L, L), hh * dh:(hh + 1) * dh] = h
            new_ms.append(m_new)
        for hh in range(hg):
            h, m_new = one_direction(1, hh, cb, *load_chunk(hh, cb), ms[hg + hh])
            hb_ref[pl.ds(cb * L, L), hh * dh:(hh + 1) * dh] = h
            new_ms.append(m_new)
        return tuple(new_ms)

    if has_init:
        ms0 = tuple(m0_ref[d, hh] for d in range(2) for hh in range(hg))
    else:
        ms0 = tuple(jnp.zeros((1, 1), F32) for _ in range(2 * hg))
    ms = lax.fori_loop(0, nc, step, ms0)

    for hh in range(hg):
        cs = slice(hh * dh, (hh + 1) * dh)
        hs = hf_ref[:, cs] + hb_ref[:, cs]
        mu = jnp.mean(hs, axis=-1, keepdims=True)
        hc = hs - mu
        var = jnp.mean(hc * hc, axis=-1, keepdims=True)
        hn = hc * lax.rsqrt(var + LN_EPS) * hgain_ref[:, cs]
        out_ref[:, cs] = (_sigmoid(o_ref[:, cs]) * hn).astype(out_ref.dtype)

    if emit_state:
        for d in range(2):
            for hh in range(hg):
                idx = d * hg + hh
                co_ref[d, hh] = cext_ref[idx, :, 0:dh]
                no_ref[d, hh] = cext_ref[idx, :, dh:2 * dh].T[0:1, :]
                mo_ref[d, hh] = jnp.broadcast_to(ms[idx], (1, LANES))


def _mlstm_old(proj, gates, head_g, g0, n_seq, seq, hg, init=None, emit_state=False):
    g, s, _ = proj.shape
    per_group = s // seq
    n_hg = MLSTM_HEADS // hg
    w = hg * MLSTM_DH
    nc = seq // MLSTM_CHUNK
    n_blocks = D_MODEL // w

    g_seq = gates[g0:g0 + n_seq // per_group].reshape(n_seq, seq, LANES)
    g_row = g_seq[:, :, :N_GATES * MLSTM_HEADS].transpose(0, 2, 1).reshape(
        n_seq, N_GATES, MLSTM_HEADS, nc, MLSTM_CHUNK)

    def tok_map(colblock):
        return lambda b, hi: (g0 + b // per_group, b % per_group, colblock * n_blocks + hi)

    args = [proj, proj, proj, proj, g_seq, g_row, head_g.reshape(1, D_MODEL)]
    in_specs = [
        pl.BlockSpec((None, seq, w), tok_map(0)),
        pl.BlockSpec((None, seq, w), tok_map(1)),
        pl.BlockSpec((None, seq, w), tok_map(2)),
        pl.BlockSpec((None, seq, w), tok_map(3)),
        pl.BlockSpec((None, seq, LANES), lambda b, hi: (b, 0, 0)),
        pl.BlockSpec((None, N_GATES, hg, nc, MLSTM_CHUNK), lambda b, hi: (b, 0, hi, 0, 0)),
        pl.BlockSpec((1, w), lambda b, hi: (0, hi)),
    ]
    if init is not None:
        c0, n0, m0 = init
        args += [c0, n0.reshape(n0.shape + (1,)), m0.reshape(m0.shape + (1, 1))]
        in_specs += [
            pl.BlockSpec((None, 2, hg, MLSTM_DH, MLSTM_DH), lambda b, hi: (b, 0, hi, 0, 0)),
            pl.BlockSpec((None, 2, hg, 1, MLSTM_DH), lambda b, hi: (b, 0, hi, 0, 0)),
            pl.BlockSpec((None, 2, hg, 1, 1), lambda b, hi: (b, 0, hi, 0, 0)),
        ]
    out_shape = [jax.ShapeDtypeStruct((n_seq // per_group, s, D_MODEL), BF16)]
    out_specs = [pl.BlockSpec((None, seq, w), lambda b, hi: (b // per_group, b % per_group, hi))]
    if emit_state:
        out_shape += [
            jax.ShapeDtypeStruct((n_seq, 2, MLSTM_HEADS, MLSTM_DH, MLSTM_DH), F32),
            jax.ShapeDtypeStruct((n_seq, 2, MLSTM_HEADS, 1, MLSTM_DH), F32),
            jax.ShapeDtypeStruct((n_seq, 2, MLSTM_HEADS, 1, LANES), F32),
        ]
        out_specs += [
            pl.BlockSpec((None, 2, hg, MLSTM_DH, MLSTM_DH), lambda b, hi: (b, 0, hi, 0, 0)),
            pl.BlockSpec((None, 2, hg, 1, MLSTM_DH), lambda b, hi: (b, 0, hi, 0, 0)),
            pl.BlockSpec((None, 2, hg, 1, LANES), lambda b, hi: (b, 0, hi, 0, 0)),
        ]

    return pl.pallas_call(
        functools.partial(_mlstm_kernel, seq=seq, hg=hg, has_init=init is not None, emit_state=emit_state),
        out_shape=tuple(out_shape),
        grid=(n_seq, n_hg),
        in_specs=in_specs,
        out_specs=tuple(out_specs),
        scratch_shapes=[
            pltpu.VMEM((2 * hg, MLSTM_DH, 2 * MLSTM_DH), F32),
            pltpu.VMEM((seq, w), F32),
            pltpu.VMEM((seq, w), F32),
        ],
        compiler_params=_cparams(("arbitrary", "arbitrary")),
        name="mlstm_%d" % seq,
    )(*args)


def kernel(x_prompt, x_sample, c, cache_k, cache_v, state_C, state_n, state_m, c_ctx, ada_w, ada_b, ln_g, ln_b, w_in_a, diff_lambda, diff_norm_g, pool_w, pool_scale, w_out_a, ffn_w_gu, ffn_w_down, w_in_c, b_gates_c, mlstm_norm_g, w_out_c, router_w, router_b, moe_w_gu, moe_w_down):
    n_ctx, seq_ctx, d = x_prompt.shape
    n_lat, seq_lat, _ = x_sample.shape
    assert d == D_MODEL and (n_ctx * seq_ctx) % seq_lat == 0 and seq_lat % seq_ctx == 0
    gl = n_lat
    gc = n_ctx * seq_ctx // seq_lat
    s = seq_lat

    x_ctx = x_prompt.reshape(gc, s, d)
    cvec = jnp.concatenate([c, jnp.broadcast_to(c_ctx[None, :], (gc, d))], axis=0)
    mod_all = _modulation(cvec, ada_w, ada_b).reshape(DEPTH, gl + gc, 6, d)

    mod = mod_all[0]
    lam_init = 0.8 - 0.6 * math.exp(-0.3 * 0)
    cos_t, sin_t = _rope_tables(s)
    proj = _inproj_a(x_sample, x_ctx, mod, w_in_a[0].astype(BF16), cos_t, sin_t)
    norm_g = diff_norm_g[0].reshape(1, LANES)
    attn_c, new_k, new_v = _attn_context(proj, diff_lambda[0], norm_g, gl, n_ctx, seq_ctx, lam_init)
    attn_l = _attn_latent(proj, cache_k, cache_v, diff_lambda[0], norm_g, gl, lam_init)
    pool_c = _pool(proj, pool_w[0], pool_scale[0], gl, gc, seq_ctx)
    pool_l = _pool(proj, pool_w[0], pool_scale[0], 0, gl, seq_lat)
    w_out = w_out_a[0].astype(BF16)
    x = _outproj([(attn_l, attn_c), (pool_l, pool_c)], [w_out[:DIFF_WIDTH], w_out[DIFF_WIDTH:]],
                 x_sample, x_ctx, mod, ln_g[0, 0], ln_b[0, 0], 2)
    x = _ffn(x, mod, ffn_w_gu[0].astype(BF16), ffn_w_down[0].astype(BF16), ln_g[0, 1], ln_b[0, 1])

    mod = mod_all[1]
    n_main = 4 * D_MODEL
    w_main = w_in_c[0][:, :n_main].astype(BF16)
    nh = MLSTM_HEADS
    wg4 = w_in_c[0][:, n_main:].reshape(d, N_GATES, nh)
    bg4 = b_gates_c[0].reshape(1, N_GATES, nh)
    lane_pad = ((0, 0), (0, LANES - 2 * nh))

    def gate_lanes(a):
        return jnp.concatenate([jnp.pad(jnp.concatenate([a[:, 0], a[:, 2]], axis=-1), lane_pad),
                                jnp.pad(jnp.concatenate([a[:, 1], a[:, 3]], axis=-1), lane_pad)], axis=-1)

    proj, gates = _inproj_c(x, mod, w_main, gate_lanes(wg4).astype(BF16), gate_lanes(bg4))
    mix_c, new_c, new_n, new_m = _mlstm(proj, gates, mlstm_norm_g[0], gl, n_ctx, seq_ctx, MLSTM_HEADS,
                                        emit_state=True)
    (mix_l,) = _mlstm(proj, gates, mlstm_norm_g[0], 0, n_lat, seq_lat, 4,
                      init=(state_C[:, 0], state_n[:, 0], state_m[:, 0]))
    x = _outproj([(mix_l, mix_c)], [w_out_c[0].astype(BF16)], x, None, mod, ln_g[1, 0], ln_b[1, 0], 2)
    y_sample, y_ctx = _moe2(x, mod, router_w[0], router_b[0], moe_w_gu[0].astype(BF16),
                            moe_w_down[0].astype(BF16), ln_g[1, 1], ln_b[1, 1], gl)
    y_prompt = y_ctx.reshape(n_ctx, seq_ctx, d)
    new_m = new_m[:, 0, 0, :2 * MLSTM_HEADS].reshape(n_ctx, 2, MLSTM_HEADS)
    return (y_prompt, y_sample, new_k, new_v, new_c[:, None], new_n[..., 0, :][:, None], new_m[:, None])
```

```python
import functools
import math

import jax
import jax.numpy as jnp
from jax import lax
from jax.experimental import pallas as pl
from jax.experimental.pallas import tpu as pltpu

F32 = jnp.float32
BF16 = jnp.bfloat16

D_MODEL = 1024
GRID_W = 64
ROPE_BASE = 10000.0
DIFF_HEADS = 4
DIFF_DH = 64
DIFF_WIDTH = DIFF_HEADS * 2 * DIFF_DH
POOL_GROUPS = 4
POOL_GC = 128
POOL_WIDTH = POOL_GROUPS * POOL_GC
POOL_WINDOWS = (2, 4, 8, 16)
W_IN_A = 3 * DIFF_WIDTH + POOL_WIDTH
MLSTM_HEADS = 8
MLSTM_DH = 128
MLSTM_CHUNK = 64
N_GATES = 4
D_FF = 2816
N_EXPERTS = 8
D_FF_EXPERT = 1792
LN_EPS = 1e-5
DEPTH = 2
ALPHA = (2.0 * DEPTH) ** 0.25

LANES = 128
FF_CHUNK = 256
VMEM_LIMIT = 56 * 1024 * 1024


def _cparams(sem, **kw):
    return pltpu.CompilerParams(dimension_semantics=sem, vmem_limit_bytes=VMEM_LIMIT, **kw)


def _resident(shape, index_map):
    return pl.BlockSpec(shape, index_map, pipeline_mode=pl.Buffered(1))


def _layer_norm_rows(z, g, b):
    mu = jnp.mean(z, axis=-1, keepdims=True)
    zc = z - mu
    var = jnp.mean(zc * zc, axis=-1, keepdims=True)
    return zc * lax.rsqrt(var + LN_EPS) * g + b


def _sigmoid(x):
    return 1.0 / (1.0 + jnp.exp(-x))


SUBLANES = 8
assert D_MODEL == SUBLANES * LANES


def _store_token_tiles(ref, x, *lead):
    for j in range(SUBLANES):
        ref[lead + (slice(None), j, slice(None))] = x[:, j * LANES:(j + 1) * LANES]


def _load_token_tiles(ref, *lead):
    return jnp.concatenate([ref[lead + (slice(None), j, slice(None))] for j in range(SUBLANES)], axis=-1)


def _split_bf16(x):
    hi = x.astype(BF16)
    lo = (x - hi.astype(F32)).astype(BF16)
    return hi, lo


def _mod_kernel(c_ref, w_ref, b_ref, o_ref):
    c = c_ref[...]
    h = (c * _sigmoid(c)).astype(BF16)
    o_ref[...] = jnp.dot(h, w_ref[...].astype(BF16), preferred_element_type=F32) + b_ref[...]


def _modulation(cvec, ada_w, ada_b):
    depth, d, n = ada_w.shape
    g = cvec.shape[0]
    tn = 1536
    return pl.pallas_call(
        _mod_kernel,
        out_shape=jax.ShapeDtypeStruct((depth, g, n), F32),
        grid=(depth, n // tn),
        in_specs=[
            pl.BlockSpec((g, d), lambda l, j: (0, 0)),
            pl.BlockSpec((None, d, tn), lambda l, j: (l, 0, j)),
            pl.BlockSpec((None, 1, tn), lambda l, j: (l, 0, j)),
        ],
        out_specs=pl.BlockSpec((None, g, tn), lambda l, j: (l, 0, j)),
        compiler_params=_cparams(("arbitrary", "arbitrary")),
        name="modulation",
    )(cvec, ada_w, ada_b.reshape(depth, 1, n))


def _rot_half16(x):
    lane = lax.broadcasted_iota(jnp.int32, x.shape, 1)
    return jnp.where((lane % 32) < 16, pltpu.roll(x, LANES - 16, 1), pltpu.roll(x, 16, 1))


def _two_stream_specs(tm, d, gl, ctx_first_group=0):
    return [pl.BlockSpec((None, tm, d), lambda gi, ti: (jnp.minimum(gi, gl - 1), jnp.where(gi < gl, ti, 0), 0)),
            pl.BlockSpec((None, tm, d), lambda gi, ti: (ctx_first_group + jnp.maximum(gi - gl, 0),
                                                        jnp.where(gi < gl, 0, ti), 0))]


def _inproj_a_kernel(xl_ref, xc_ref, mod_ref, w_ref, cos_ref, sin_ref, o_ref, *, n_latent_groups):
    x = jnp.where(pl.program_id(0) < n_latent_groups, xl_ref[...], xc_ref[...])
    h = x * (1.0 + mod_ref[1:2, :]) + mod_ref[0:1, :]
    p = jnp.dot(h.astype(BF16), w_ref[...], preferred_element_type=F32)
    cos = cos_ref[...]
    sin = sin_ref[...]
    n_rope = 2 * DIFF_WIDTH // LANES
    for j in range(n_rope):
        blk = p[:, j * LANES:(j + 1) * LANES]
        o_ref[:, j * LANES:(j + 1) * LANES] = blk * cos + _rot_half16(blk) * sin
    o_ref[:, n_rope * LANES:] = p[:, n_rope * LANES:]


def _inproj_a(x_lat, x_ctx, mod, w_bf, cos_t, sin_t, tm=512):
    n_latent_groups, s, d = x_lat.shape
    g = n_latent_groups + x_ctx.shape[0]
    n = w_bf.shape[1]

    def table_map(gi, ti):
        return (jnp.where(gi >= n_latent_groups, 1, 0), ti, 0)

    return pl.pallas_call(
        functools.partial(_inproj_a_kernel, n_latent_groups=n_latent_groups),
        out_shape=jax.ShapeDtypeStruct((g, s, n), F32),
        grid=(g, s // tm),
        in_specs=_two_stream_specs(tm, d, n_latent_groups) + [
            pl.BlockSpec((None, 6, d), lambda gi, ti: (gi, 0, 0)),
            _resident((d, n), lambda gi, ti: (0, 0)),
            pl.BlockSpec((None, tm, LANES), table_map),
            pl.BlockSpec((None, tm, LANES), table_map),
        ],
        out_specs=pl.BlockSpec((None, tm, n), lambda gi, ti: (gi, ti, 0)),
        compiler_params=_cparams(("arbitrary", "arbitrary")),
        name="inproj_a",
    )(x_lat, x_ctx, mod, w_bf, cos_t, sin_t)


def _rope_tables(n_tokens):
    rows = n_tokens // GRID_W
    row_pos = jnp.repeat(jnp.arange(rows), GRID_W).astype(F32)
    col_pos = jnp.tile(jnp.arange(GRID_W), rows).astype(F32)
    n_freq = DIFF_DH // 4
    inv_freq = jnp.power(ROPE_BASE, -jnp.arange(n_freq, dtype=F32) / n_freq)
    ang = jnp.stack([row_pos[:, None] * inv_freq, col_pos[:, None] * inv_freq], axis=1)
    cos, sin = jnp.cos(ang), jnp.sin(ang)
    cos64 = jnp.concatenate([cos[:, 0], cos[:, 0], cos[:, 1], cos[:, 1]], axis=-1)
    sin64 = jnp.concatenate([-sin[:, 0], sin[:, 0], -sin[:, 1], sin[:, 1]], axis=-1)
    cos_l = jnp.tile(cos64, (1, LANES // DIFF_DH))
    sin_l = jnp.tile(sin64, (1, LANES // DIFF_DH))
    cos_t = jnp.stack([cos_l, jnp.ones_like(cos_l)])
    sin_t = jnp.stack([sin_l, jnp.zeros_like(sin_l)])
    return cos_t, sin_t


def _diff_attn_kernel(*refs, n_pieces, n_heads, lam_init, emit_kv):
    lam_ref, ng_ref, q_ref = refs[:3]
    kv_refs = refs[3:3 + 2 * n_pieces]
    o_ref = refs[3 + 2 * n_pieces]

    lp = lam_ref[...]
    lam = (jnp.exp(jnp.sum(lp[0:1] * lp[1:2], axis=-1, keepdims=True))
           - jnp.exp(jnp.sum(lp[2:3] * lp[3:4], axis=-1, keepdims=True)) + lam_init)
    nt = (((1,), (1,)), ((), ()))

    def softmax_pieces(ss):
        m = functools.reduce(jnp.maximum, [jnp.max(s, axis=-1, keepdims=True) for s in ss])
        es = [jnp.exp(s - m) for s in ss]
        l = functools.reduce(jnp.add, [jnp.sum(e, axis=-1, keepdims=True) for e in es])
        return [e / l for e in es]

    for h in range(n_heads):
        hs = slice(h * LANES, (h + 1) * LANES)
        q = q_ref[:, hs] * (DIFF_DH ** -0.5)
        lane = lax.broadcasted_iota(jnp.int32, q.shape, 1)
        q1 = jnp.where(lane < DIFF_DH, q, 0.0).astype(BF16)
        q2 = jnp.where(lane >= DIFF_DH, q, 0.0).astype(BF16)
        s1, s2, vs = [], [], []
        for i in range(n_pieces):
            kb = kv_refs[2 * i][:, hs].astype(BF16)
            vs.append(kv_refs[2 * i + 1][:, hs].astype(BF16))
            s1.append(lax.dot_general(q1, kb, nt, preferred_element_type=F32))
            s2.append(lax.dot_general(q2, kb, nt, preferred_element_type=F32))
        p1 = softmax_pieces(s1)
        p2 = softmax_pieces(s2)
        o = None
        for i in range(n_pieces):
            a = (p1[i] - lam * p2[i]).astype(BF16)
            t = jnp.dot(a, vs[i], preferred_element_type=F32)
            o = t if o is None else o + t
        o = o * lax.rsqrt(jnp.mean(o * o, axis=-1, keepdims=True) + LN_EPS)
        o_ref[:, hs] = (o * ng_ref[...] * (1.0 - lam_init)).astype(o_ref.dtype)
        if emit_kv:
            ko_ref, vo_ref = refs[4 + 2 * n_pieces:]
            ko_ref[h] = kv_refs[0][:, hs]
            vo_ref[h] = kv_refs[1][:, hs]


def _attn_context(proj, lam_p, norm_g, n_latent_groups, n_seq, seq, lam_init):
    g, s, _ = proj.shape
    per_group = s // seq
    blk = (None, seq, DIFF_WIDTH)

    def tok_map(colblock):
        return lambda b: (n_latent_groups + b // per_group, b % per_group, colblock)

    cache_shape = jax.ShapeDtypeStruct((n_seq, 1, DIFF_HEADS, seq, LANES), F32)
    cache_spec = pl.BlockSpec((None, None, DIFF_HEADS, seq, LANES), lambda b: (b, 0, 0, 0, 0))
    out_spec = pl.BlockSpec(blk, lambda b: (b // per_group, b % per_group, 0))
    return pl.pallas_call(
        functools.partial(_diff_attn_kernel, n_pieces=1, n_heads=DIFF_HEADS, lam_init=lam_init, emit_kv=True),
        out_shape=(jax.ShapeDtypeStruct((g - n_latent_groups, s, DIFF_WIDTH), BF16), cache_shape, cache_shape),
        grid=(n_seq,),
        in_specs=[
            pl.BlockSpec((4, DIFF_DH), lambda b: (0, 0)),
            pl.BlockSpec((1, LANES), lambda b: (0, 0)),
            pl.BlockSpec(blk, tok_map(0)),
            pl.BlockSpec(blk, tok_map(1)),
            pl.BlockSpec(blk, tok_map(2)),
        ],
        out_specs=(out_spec, cache_spec, cache_spec),
        compiler_params=_cparams(("arbitrary",)),
        name="attn_context",
    )(lam_p, norm_g, proj, proj, proj)


def _attn_latent(proj, cache_k, cache_v, lam_p, norm_g, n_latent_groups, lam_init, tq=256):
    g, s, _ = proj.shape
    past = cache_k.shape[3]
    cache_spec = pl.BlockSpec((None, None, None, past, LANES), lambda b, h, qi: (b, 0, h, 0, 0))
    return pl.pallas_call(
        functools.partial(_diff_attn_kernel, n_pieces=2, n_heads=1, lam_init=lam_init, emit_kv=False),
        out_shape=jax.ShapeDtypeStruct((n_latent_groups, s, DIFF_WIDTH), BF16),
        grid=(n_latent_groups, DIFF_HEADS, s // tq),
        in_specs=[
            pl.BlockSpec((4, DIFF_DH), lambda b, h, qi: (0, 0)),
            pl.BlockSpec((1, LANES), lambda b, h, qi: (0, 0)),
            pl.BlockSpec((None, tq, LANES), lambda b, h, qi: (b, qi, h)),
            cache_spec,
            cache_spec,
            pl.BlockSpec((None, s, LANES), lambda b, h, qi: (b, 0, DIFF_HEADS + h)),
            pl.BlockSpec((None, s, LANES), lambda b, h, qi: (b, 0, 2 * DIFF_HEADS + h)),
        ],
        out_specs=pl.BlockSpec((None, tq, LANES), lambda b, h, qi: (b, qi, h)),
        compiler_params=_cparams(("arbitrary", "arbitrary", "arbitrary")),
        name="attn_latent",
    )(lam_p, norm_g, proj, cache_k, cache_v, proj, proj)


POOL_ROW_BLOCK = 256
POOL_COL_WINDOW = 512
assert (POOL_COL_WINDOW - POOL_ROW_BLOCK) // 2 >= max(POOL_WINDOWS) // 2


def _pool_kernel(p_ref, w_ref, sc_ref, o_ref, band_ref, *, seq):
    @pl.when((pl.program_id(0) == 0) & (pl.program_id(1) == 0))
    def _():
        t = lax.broadcasted_iota(jnp.int32, (seq, seq), 0)
        s_ = lax.broadcasted_iota(jnp.int32, (seq, seq), 1)
        for gi, w in enumerate(POOL_WINDOWS):
            inside = (s_ >= t - w // 2) & (s_ <= t + w // 2 - 1)
            band_ref[gi] = jnp.where(inside, 1.0, 0.0).astype(BF16)

    tcol = lax.broadcasted_iota(jnp.int32, (seq, 1), 0)
    for gi, w in enumerate(POOL_WINDOWS):
        u = p_ref[:, gi * POOL_GC:(gi + 1) * POOL_GC]
        hi, lo = _split_bf16(u)
        rb = min(seq, POOL_ROW_BLOCK)
        cw = min(seq, POOL_COL_WINDOW)
        blocks = []
        for i in range(seq // rb):
            c0 = min(max(i * rb - (cw - rb) // 2, 0), seq - cw)
            band = band_ref[gi, i * rb:(i + 1) * rb, c0:c0 + cw]
            blocks.append(jnp.dot(band, hi[c0:c0 + cw], preferred_element_type=F32)
                          + jnp.dot(band, lo[c0:c0 + cw], preferred_element_type=F32))
        win = blocks[0] if len(blocks) == 1 else jnp.concatenate(blocks, axis=0)
        cnt = (jnp.minimum(tcol + (w // 2 - 1), seq - 1) - jnp.maximum(tcol - w // 2, 0) + 1).astype(F32)
        pooled = win / cnt - u
        mixed = jnp.dot(pooled.astype(BF16), w_ref[gi].astype(BF16), preferred_element_type=F32)
        o_ref[:, gi * POOL_GC:(gi + 1) * POOL_GC] = (
            mixed * sc_ref[:, gi * POOL_GC:(gi + 1) * POOL_GC]).astype(o_ref.dtype)


def _pool(proj, pool_w, pool_scale, g0, n_groups, seq):
    g, s, _ = proj.shape
    col = 3 * DIFF_WIDTH // POOL_WIDTH
    return pl.pallas_call(
        functools.partial(_pool_kernel, seq=seq),
        out_shape=jax.ShapeDtypeStruct((n_groups, s, POOL_WIDTH), BF16),
        grid=(n_groups, s // seq),
        in_specs=[
            pl.BlockSpec((None, seq, POOL_WIDTH), lambda gi, ti: (g0 + gi, ti, col)),
            pl.BlockSpec((POOL_GROUPS, POOL_GC, POOL_GC), lambda gi, ti: (0, 0, 0)),
            pl.BlockSpec((1, POOL_WIDTH), lambda gi, ti: (0, 0)),
        ],
        out_specs=pl.BlockSpec((None, seq, POOL_WIDTH), lambda gi, ti: (gi, ti, 0)),
        scratch_shapes=[pltpu.VMEM((POOL_GROUPS, seq, seq), BF16)],
        compiler_params=_cparams(("arbitrary", "arbitrary")),
        name="pool_%d" % seq,
    )(proj, pool_w, pool_scale.reshape(1, POOL_WIDTH))


def _outproj_kernel(*refs, n_in, gate_row, n_latent_groups):
    a_refs = refs[:2 * n_in]
    w_refs = refs[2 * n_in:3 * n_in]
    xl_ref, xc_ref, mod_ref, g_ref, b_ref, o_ref = refs[3 * n_in:]
    is_latent = pl.program_id(0) < n_latent_groups
    acc = None
    for i, w_ref in enumerate(w_refs):
        a = jnp.where(is_latent, a_refs[2 * i][...], a_refs[2 * i + 1][...])
        t = jnp.dot(a, w_ref[...], preferred_element_type=F32)
        acc = t if acc is None else acc + t
    x = jnp.where(is_latent, xl_ref[...], xc_ref[...])
    z = ALPHA * x + mod_ref[gate_row:gate_row + 1, :] * acc
    o_ref[...] = _layer_norm_rows(z, g_ref[...], b_ref[...])


def _outproj(acts, weights, x_lat, x_ctx, mod, ln_g, ln_b, gate_row, tm=512):
    gl = acts[0][0].shape[0]
    _, s, d = x_lat.shape
    if x_ctx is None:
        g = x_lat.shape[0]
        x_ctx, x_specs = x_lat, _two_stream_specs(tm, d, gl, gl)
    else:
        g = gl + x_ctx.shape[0]
        x_specs = _two_stream_specs(tm, d, gl)
    n_in = len(acts)
    in_specs = []
    flat_acts = []
    for a_lat, a_ctx in acts:
        in_specs += _two_stream_specs(tm, a_lat.shape[-1], gl)
        flat_acts += [a_lat, a_ctx]
    in_specs += [_resident(w.shape, lambda gi, ti: (0, 0)) for w in weights]
    in_specs += x_specs
    in_specs += [
        pl.BlockSpec((None, 6, d), lambda gi, ti: (gi, 0, 0)),
        pl.BlockSpec((1, d), lambda gi, ti: (0, 0)),
        pl.BlockSpec((1, d), lambda gi, ti: (0, 0)),
    ]
    return pl.pallas_call(
        functools.partial(_outproj_kernel, n_in=n_in, gate_row=gate_row, n_latent_groups=gl),
        out_shape=jax.ShapeDtypeStruct((g, s, d), F32),
        grid=(g, s // tm),
        in_specs=in_specs,
        out_specs=pl.BlockSpec((None, tm, d), lambda gi, ti: (gi, ti, 0)),
        compiler_params=_cparams(("arbitrary", "arbitrary")),
        name="outproj",
    )(*flat_acts, *weights, x_lat, x_ctx, mod, ln_g.reshape(1, d), ln_b.reshape(1, d))


def _swiglu_chunks(h_bf, wgu_ref, wd_ref, d_ff):
    acc = None
    for j in range(d_ff // FF_CHUNK):
        lo = j * FF_CHUNK
        gate = jnp.dot(h_bf, wgu_ref[:, lo:lo + FF_CHUNK], preferred_element_type=F32)
        up = jnp.dot(h_bf, wgu_ref[:, d_ff + lo:d_ff + lo + FF_CHUNK], preferred_element_type=F32)
        act = (gate * _sigmoid(gate) * up).astype(BF16)
        t = jnp.dot(act, wd_ref[lo:lo + FF_CHUNK, :], preferred_element_type=F32)
        acc = t if acc is None else acc + t
    return acc


def _ffn_kernel(x_ref, mod_ref, wgu_ref, wd_ref, g_ref, b_ref, *rest, n_cast):
    cast_in, o_ref, cast_out = rest[:n_cast], rest[n_cast], rest[n_cast + 1:]
    x = x_ref[...]
    h = (x * (1.0 + mod_ref[4:5, :]) + mod_ref[3:4, :]).astype(BF16)
    acc = _swiglu_chunks(h, wgu_ref, wd_ref, D_FF)
    z = ALPHA * x + mod_ref[5:6, :] * acc
    o_ref[...] = _layer_norm_rows(z, g_ref[...], b_ref[...])
    for src, dst in zip(cast_in, cast_out):
        dst[...] = src[...].astype(BF16)


def _ffn(x, mod, wgu_bf, wd_bf, ln_g, ln_b, cast_weights=(), tm=512):
    g, s, d = x.shape
    per = s // tm
    n_steps = g * per
    if any(n_steps % w.shape[0] or w.shape[1] % (16 * (n_steps // w.shape[0])) for w in cast_weights):
        out, _ = _ffn(x, mod, wgu_bf, wd_bf, ln_g, ln_b, (), tm)
        return out, tuple(w.astype(BF16) for w in cast_weights)
    cast_specs = []
    for w in cast_weights:
        e, rows, cols = w.shape
        slabs = n_steps // e
        cast_specs.append(pl.BlockSpec((None, rows // slabs, cols),
                                       lambda gi, ti, slabs=slabs: ((gi * per + ti) // slabs, (gi * per + ti) % slabs, 0)))
    outs = pl.pallas_call(
        functools.partial(_ffn_kernel, n_cast=len(cast_weights)),
        out_shape=(jax.ShapeDtypeStruct((g, s, d), F32),) + tuple(
            jax.ShapeDtypeStruct(w.shape, BF16) for w in cast_weights),
        grid=(g, per),
        in_specs=[
            pl.BlockSpec((None, tm, d), lambda gi, ti: (gi, ti, 0)),
            pl.BlockSpec((None, 6, d), lambda gi, ti: (gi, 0, 0)),
            _resident(wgu_bf.shape, lambda gi, ti: (0, 0)),
            _resident(wd_bf.shape, lambda gi, ti: (0, 0)),
            pl.BlockSpec((1, d), lambda gi, ti: (0, 0)),
            pl.BlockSpec((1, d), lambda gi, ti: (0, 0)),
        ] + cast_specs,
        out_specs=(pl.BlockSpec((None, tm, d), lambda gi, ti: (gi, ti, 0)),) + tuple(cast_specs),
        compiler_params=_cparams(("arbitrary", "arbitrary")),
        name="ffn",
    )(x, mod, wgu_bf, wd_bf, ln_g.reshape(1, d), ln_b.reshape(1, d), *cast_weights)
    return outs[0], outs[1:]


def _router_combine(h, rw_ref, rb_ref):
    h_hi, h_lo = _split_bf16(h)
    w_hi, w_lo = _split_bf16(rw_ref[...])
    logits = (jnp.dot(h_hi, w_hi, preferred_element_type=F32)
              + jnp.dot(h_lo, w_hi, preferred_element_type=F32)
              + jnp.dot(h_hi, w_lo, preferred_element_type=F32)) + rb_ref[...]
    lane = lax.broadcasted_iota(jnp.int32, logits.shape, 1).astype(F32)
    neg = -jnp.inf
    logits = jnp.where(lane < N_EXPERTS, logits, neg)
    m1 = jnp.max(logits, axis=-1, keepdims=True)
    i1 = jnp.min(jnp.where(logits == m1, lane, float(LANES)), axis=-1, keepdims=True)
    rest = jnp.where(lane == i1, neg, logits)
    m2 = jnp.max(rest, axis=-1, keepdims=True)
    i2 = jnp.min(jnp.where(rest == m2, lane, float(LANES)), axis=-1, keepdims=True)
    e2 = jnp.exp(m2 - m1)
    w1 = 1.0 / (1.0 + e2)
    w2 = e2 / (1.0 + e2)
    member = (lane == i1) | (lane == i2)
    return jnp.where(lane == i1, w1, 0.0) + jnp.where(lane == i2, w2, 0.0), member


MOE_BM = 256
MOE_TC = 256
MOE_TMC = 512


def _route_kernel(x_ref, mod_ref, rw_ref, rb_ref, h_ref, cw_ref, srank_ref, cnt_ref, tri_ref, run_ref):
    tm = x_ref.shape[0]

    @pl.when(pl.program_id(0) == 0)
    def _():
        r = lax.broadcasted_iota(jnp.int32, (tm, tm), 0)
        c = lax.broadcasted_iota(jnp.int32, (tm, tm), 1)
        tri_ref[...] = jnp.where(c <= r, 1.0, 0.0).astype(BF16)
        run_ref[...] = jnp.zeros_like(run_ref)

    h = x_ref[...] * (1.0 + mod_ref[4:5, :]) + mod_ref[3:4, :]
    h_ref[...] = h.astype(BF16)
    cw, member = _router_combine(h, rw_ref, rb_ref)
    cw_ref[...] = cw
    mem = jnp.where(member, 1.0, 0.0)
    rank = jnp.dot(tri_ref[...], mem.astype(BF16), preferred_element_type=F32) + run_ref[...]
    srank_ref[...] = jnp.where(member, rank, -rank)
    run_ref[...] = rank[tm - 1:tm, :]
    cnt_ref[...] = rank[tm - 1:tm, :]


def _route(x, mod, router_w, router_b, tm=512):
    g, s, d = x.shape
    n = g * s
    per = s // tm
    rw = jnp.pad(router_w, ((0, 0), (0, LANES - N_EXPERTS)))
    rb = jnp.pad(router_b, (0, LANES - N_EXPERTS)).reshape(1, LANES)
    return pl.pallas_call(
        _route_kernel,
        out_shape=(jax.ShapeDtypeStruct((n, d), BF16), jax.ShapeDtypeStruct((n, LANES), F32),
                   jax.ShapeDtypeStruct((n, LANES), F32), jax.ShapeDtypeStruct((1, LANES), F32)),
        grid=(n // tm,),
        in_specs=[
            pl.BlockSpec((None, tm, d), lambda i: (i // per, i % per, 0)),
            pl.BlockSpec((None, 6, d), lambda i: (i // per, 0, 0)),
            pl.BlockSpec((d, LANES), lambda i: (0, 0)),
            pl.BlockSpec((1, LANES), lambda i: (0, 0)),
        ],
        out_specs=(pl.BlockSpec((tm, d), lambda i: (i, 0)), pl.BlockSpec((tm, LANES), lambda i: (i, 0)),
                   pl.BlockSpec((tm, LANES), lambda i: (i, 0)), pl.BlockSpec((1, LANES), lambda i: (0, 0))),
        scratch_shapes=[pltpu.VMEM((tm, tm), BF16), pltpu.VMEM((1, LANES), F32)],
        compiler_params=_cparams(("arbitrary",)),
        name="moe_route",
    )(x, mod, rw, rb)


def _moe_plan(srank, counts, n_blocks, n_items):
    e_n = N_EXPERTS
    n = srank.shape[0]
    i32 = jnp.int32
    cnt = counts[0, :e_n].astype(i32)
    nb = (cnt + MOE_BM - 1) // MOE_BM
    nb_incl = jnp.cumsum(nb)
    gstart = nb_incl - nb
    n_used = nb_incl[-1]
    sr = srank[:, :e_n]
    rank = jnp.abs(sr).astype(i32)
    pos_tok = jnp.where(sr > 0, rank - 1 + MOE_BM * gstart[None, :], -1)
    rank_t = rank.T

    r = jnp.arange(n_blocks, dtype=i32)
    used = r < n_used
    rc = jnp.minimum(r, n_used - 1)
    e_r = jnp.minimum(jnp.sum(nb_incl[None, :] <= rc[:, None], axis=1, dtype=i32), e_n - 1)
    b = rc - gstart[e_r]
    lo = b * MOE_BM + 1
    hi = jnp.minimum((b + 1) * MOE_BM, cnt[e_r])
    def find(e, v):
        return jnp.sum(rank_t[e] < v[:, None], axis=1, dtype=i32)

    jlo = jnp.where(used, find(e_r, lo) // MOE_TC, 0)
    jhi = jnp.where(used, find(e_r, hi) // MOE_TC, -1)

    n_tiles = n // MOE_TMC
    ends = rank[MOE_TMC - 1::MOE_TMC]
    starts = jnp.concatenate([jnp.zeros((1, e_n), i32), ends[:-1]], axis=0)
    fb = gstart[None, :] + starts // MOE_BM
    lb = gstart[None, :] + (ends - 1) // MOE_BM
    n_pe = jnp.where(ends > starts, lb - fb + 1, 0).reshape(-1)
    incl = jnp.cumsum(n_pe)
    off = incl - n_pe
    total = incl[-1]
    w = jnp.arange(n_items, dtype=i32)
    valid = w < total
    wc = jnp.minimum(w, total - 1)
    p = jnp.sum(incl[None, :] <= wc[:, None], axis=1, dtype=i32)
    it_tile = p // e_n
    it_e = p % e_n
    it_blk = fb.reshape(-1)[p] + (wc - off[p])
    tile_off = jnp.concatenate([off[::e_n], total[None]])
    it_first = (wc == tile_off[it_tile]).astype(i32)
    it_last = (wc == tile_off[it_tile + 1] - 1).astype(i32)
    return (pos_tok, e_r, jlo, jhi, n_used.reshape(1),
            it_tile, it_blk, it_e, it_first, it_last, valid.astype(i32))


def _experts_kernel(be_ref, jlo_ref, jhi_ref, nused_ref, h_ref, pos_ref, wgu_ref, wd_ref, y_ref, xg_ref):
    r = pl.program_id(0)

    @pl.when(r < nused_ref[0])
    def _():
        e = be_ref[r]
        slot = r * MOE_BM + lax.broadcasted_iota(jnp.int32, (MOE_BM, MOE_TC), 0)
        xg_ref[...] = jnp.zeros_like(xg_ref)

        def chunk(j, carry):
            onehot = jnp.where(pos_ref[e, pl.ds(j, 1), :] == slot, 1.0, 0.0).astype(BF16)
            rows = h_ref[pl.ds(pl.multiple_of(j * MOE_TC, MOE_TC), MOE_TC), :]
            xg_ref[...] += jnp.dot(onehot, rows, preferred_element_type=F32)
            return carry

        lax.fori_loop(jlo_ref[r], jhi_ref[r] + 1, chunk, 0)
        y_ref[...] = _swiglu_chunks(xg_ref[...].astype(BF16), wgu_ref, wd_ref, D_FF_EXPERT)

    @pl.when(r >= nused_ref[0])
    def _():
        y_ref[...] = jnp.zeros_like(y_ref)


def _experts(h, pos_row, plan, wgu_bf, wd_bf, n_blocks):
    n, d = h.shape
    e_r, jlo, jhi, n_used = plan
    grid_spec = pltpu.PrefetchScalarGridSpec(
        num_scalar_prefetch=4,
        grid=(n_blocks,),
        in_specs=[
            _resident((n, d), lambda r, be, lo, hi, nu: (0, 0)),
            _resident(pos_row.shape, lambda r, be, lo, hi, nu: (0, 0, 0)),
            pl.BlockSpec((None, d, 2 * D_FF_EXPERT), lambda r, be, lo, hi, nu: (be[r], 0, 0),
                         pipeline_mode=pl.Buffered(1)),
            pl.BlockSpec((None, D_FF_EXPERT, d), lambda r, be, lo, hi, nu: (be[r], 0, 0)),
        ],
        out_specs=pl.BlockSpec((MOE_BM, d), lambda r, be, lo, hi, nu: (r, 0)),
        scratch_shapes=[pltpu.VMEM((MOE_BM, d), F32)],
    )
    return pl.pallas_call(
        _experts_kernel,
        out_shape=jax.ShapeDtypeStruct((n_blocks * MOE_BM, d), F32),
        grid_spec=grid_spec,
        compiler_params=_cparams(("arbitrary",)),
        name="moe_experts",
    )(e_r, jlo, jhi, n_used, h, pos_row, wgu_bf, wd_bf)


def _combine_kernel(tile_ref, blk_ref, e_ref, first_ref, last_ref, valid_ref,
                    y_ref, pos_ref, cw_ref, x_ref, mod_ref, g_ref, b_ref, o_ref, acc_ref):
    w = pl.program_id(0)

    @pl.when(valid_ref[w] == 1)
    def _():
        @pl.when(first_ref[w] == 1)
        def _():
            acc_ref[...] = jnp.zeros_like(acc_ref)

        e = e_ref[w]
        lane = lax.broadcasted_iota(jnp.int32, (MOE_TMC, LANES), 1)
        pos_e = jnp.sum(jnp.where(lane == e, pos_ref[...], 0.0), axis=-1, keepdims=True)
        cw_e = jnp.sum(jnp.where(lane == e, cw_ref[...], 0.0), axis=-1, keepdims=True)
        slot = (blk_ref[w] * MOE_BM + lax.broadcasted_iota(jnp.int32, (MOE_TMC, MOE_BM), 1)).astype(F32)
        onehot = jnp.where(pos_e == slot, 1.0, 0.0).astype(BF16)
        y_hi, y_lo = _split_bf16(y_ref[...])
        part = (jnp.dot(onehot, y_hi, preferred_element_type=F32)
                + jnp.dot(onehot, y_lo, preferred_element_type=F32))
        acc_ref[...] += cw_e * part

        @pl.when(last_ref[w] == 1)
        def _():
            z = ALPHA * x_ref[...] + mod_ref[5:6, :] * acc_ref[...]
            o_ref[...] = _layer_norm_rows(z, g_ref[...], b_ref[...])


def _combine(y, pos_tok_f, cw, x, mod, ln_g, ln_b, items, n_items):
    g, s, d = x.shape
    per = s // MOE_TMC

    def tok2(w, tile, *_):
        return (tile[w], 0)

    def tok3(w, tile, *_):
        return (tile[w] // per, tile[w] % per, 0)

    grid_spec = pltpu.PrefetchScalarGridSpec(
        num_scalar_prefetch=6,
        grid=(n_items,),
        in_specs=[
            pl.BlockSpec((MOE_BM, d), lambda w, tile, blk, *_: (blk[w], 0)),
            pl.BlockSpec((MOE_TMC, LANES), tok2),
            pl.BlockSpec((MOE_TMC, LANES), tok2),
            pl.BlockSpec((None, MOE_TMC, d), tok3),
            pl.BlockSpec((None, 6, d), lambda w, tile, *_: (tile[w] // per, 0, 0)),
            pl.BlockSpec((1, d), lambda w, *_: (0, 0)),
            pl.BlockSpec((1, d), lambda w, *_: (0, 0)),
        ],
        out_specs=pl.BlockSpec((None, MOE_TMC, d), tok3),
        scratch_shapes=[pltpu.VMEM((MOE_TMC, d), F32)],
    )
    return pl.pallas_call(
        _combine_kernel,
        out_shape=jax.ShapeDtypeStruct((g, s, d), F32),
        grid_spec=grid_spec,
        compiler_params=_cparams(("arbitrary",)),
        name="moe_combine",
    )(*items, y, pos_tok_f, cw, x, mod, ln_g.reshape(1, d), ln_b.reshape(1, d))


def _moe(x, mod, router_w, router_b, wgu_bf, wd_bf, ln_g, ln_b):
    g, s, d = x.shape
    n = g * s
    n_slots = 2 * n
    n_blocks = n_slots // MOE_BM + N_EXPERTS
    n_items = n_blocks + N_EXPERTS * (n // MOE_TMC)
    h, cw, srank, counts = _route(x, mod, router_w, router_b)
    plan = _moe_plan(srank, counts, n_blocks, n_items)
    pos_tok = plan[0]
    pos_row = pos_tok.T.reshape(N_EXPERTS, n // MOE_TC, MOE_TC)
    y = _experts(h, pos_row, plan[1:5], wgu_bf, wd_bf, n_blocks)
    pos_tok_f = jnp.pad(pos_tok.astype(F32), ((0, 0), (0, LANES - N_EXPERTS)), constant_values=-1.0)
    return _combine(y, pos_tok_f, cw, x, mod, ln_g, ln_b, plan[5:], n_items)


META_LANES = LANES


def _router_top2(h, rw_ref, rb_ref):
    h_hi, h_lo = _split_bf16(h)
    w_hi, w_lo = _split_bf16(rw_ref[...])
    logits = (jnp.dot(h_hi, w_hi, preferred_element_type=F32)
              + jnp.dot(h_lo, w_hi, preferred_element_type=F32)
              + jnp.dot(h_hi, w_lo, preferred_element_type=F32)) + rb_ref[...]
    lane = lax.broadcasted_iota(jnp.int32, logits.shape, 1).astype(F32)
    neg = -jnp.inf
    logits = jnp.where(lane < N_EXPERTS, logits, neg)
    m1 = jnp.max(logits, axis=-1, keepdims=True)
    i1 = jnp.min(jnp.where(logits == m1, lane, float(LANES)), axis=-1, keepdims=True)
    rest = jnp.where(lane == i1, neg, logits)
    m2 = jnp.max(rest, axis=-1, keepdims=True)
    i2 = jnp.min(jnp.where(rest == m2, lane, float(LANES)), axis=-1, keepdims=True)
    e2 = jnp.exp(m2 - m1)
    return lane, i1, i2, 1.0 / (1.0 + e2), e2 / (1.0 + e2)


def _route2_kernel(x_ref, mod_ref, rw_ref, rb_ref, h_ref, wk_ref, srank_ref, cnt_ref, tri_ref, run_ref):
    tm, d = x_ref.shape

    @pl.when(pl.program_id(0) == 0)
    def _():
        r = lax.broadcasted_iota(jnp.int32, (tm, tm), 0)
        c = lax.broadcasted_iota(jnp.int32, (tm, tm), 1)
        tri_ref[...] = jnp.where(c <= r, 1.0, 0.0).astype(BF16)
        run_ref[...] = jnp.zeros_like(run_ref)

    h = x_ref[...] * (1.0 + mod_ref[4:5, :]) + mod_ref[3:4, :]
    h_ref[:, 0:d] = h.astype(BF16)
    lane, i1, i2, w1, w2 = _router_top2(h, rw_ref, rb_ref)
    first_is_low = i1 < i2
    e_hi = jnp.where(first_is_low, i2, i1)
    wk_ref[...] = jnp.where(lane == 0.0, jnp.where(first_is_low, w1, w2),
                            jnp.where(lane == 1.0, jnp.where(first_is_low, w2, w1), 0.0))
    tok = (pl.program_id(0) * tm + lax.broadcasted_iota(jnp.int32, (tm, META_LANES), 0))
    meta = jnp.where(lane == 0.0, (tok // 128).astype(F32),
                     jnp.where(lane == 1.0, (tok % 128).astype(F32),
                               jnp.where(lane == 2.0, e_hi, jnp.where(lane == 3.0, 1.0, 0.0))))
    h_ref[:, d:d + META_LANES] = meta.astype(BF16)

    member = (lane == i1) | (lane == i2)
    mem = jnp.where(member, 1.0, 0.0)
    rank = jnp.dot(tri_ref[...], mem.astype(BF16), preferred_element_type=F32) + run_ref[...]
    srank_ref[...] = jnp.where(member, rank, -rank)
    run_ref[...] = rank[tm - 1:tm, :]
    cnt_ref[...] = rank[tm - 1:tm, :]


def _route2(x, mod, router_w, router_b, tm=512):
    g, s, d = x.shape
    n = g * s
    assert n <= 128 * 256
    per = s // tm
    rw = jnp.pad(router_w, ((0, 0), (0, LANES - N_EXPERTS)))
    rb = jnp.pad(router_b, (0, LANES - N_EXPERTS)).reshape(1, LANES)
    return pl.pallas_call(
        _route2_kernel,
        out_shape=(jax.ShapeDtypeStruct((n, d + META_LANES), BF16), jax.ShapeDtypeStruct((n, LANES), F32),
                   jax.ShapeDtypeStruct((n, LANES), F32), jax.ShapeDtypeStruct((1, LANES), F32)),
        grid=(n // tm,),
        in_specs=[
            pl.BlockSpec((None, tm, d), lambda i: (i // per, i % per, 0)),
            pl.BlockSpec((None, 6, d), lambda i: (i // per, 0, 0)),
            pl.BlockSpec((d, LANES), lambda i: (0, 0)),
            pl.BlockSpec((1, LANES), lambda i: (0, 0)),
        ],
        out_specs=(pl.BlockSpec((tm, d + META_LANES), lambda i: (i, 0)), pl.BlockSpec((tm, LANES), lambda i: (i, 0)),
                   pl.BlockSpec((tm, LANES), lambda i: (i, 0)), pl.BlockSpec((1, LANES), lambda i: (0, 0))),
        scratch_shapes=[pltpu.VMEM((tm, tm), BF16), pltpu.VMEM((1, LANES), F32)],
        compiler_params=_cparams(("arbitrary",)),
        name="moe_route",
    )(x, mod, rw, rb)


def _moe_plan2(srank, counts, n_blocks):
    e_n = N_EXPERTS
    i32 = jnp.int32
    cnt = counts[0, :e_n].astype(i32)
    nb = (cnt + MOE_BM - 1) // MOE_BM
    nb_incl = jnp.cumsum(nb)
    gstart = nb_incl - nb
    n_used = nb_incl[-1]
    sr = srank[:, :e_n]
    rank = jnp.abs(sr).astype(i32)
    pos_tok = jnp.where(sr > 0, rank - 1 + MOE_BM * gstart[None, :], -1)
    chunk_end = rank[MOE_TC - 1::MOE_TC].T

    r = jnp.arange(n_blocks, dtype=i32)
    used = r < n_used
    rc = jnp.minimum(r, n_used - 1)
    e_r = jnp.minimum(jnp.sum(nb_incl[None, :] <= rc[:, None], axis=1, dtype=i32), e_n - 1)
    b = rc - gstart[e_r]
    lo = b * MOE_BM + 1
    hi = jnp.minimum((b + 1) * MOE_BM, cnt[e_r])
    ends_r = chunk_end[e_r]
    jlo = jnp.where(used, jnp.sum(ends_r < lo[:, None], axis=1, dtype=i32), 0)
    jhi = jnp.where(used, jnp.sum(ends_r < hi[:, None], axis=1, dtype=i32), -1)
    return pos_tok, e_r, jlo, jhi, n_used.reshape(1)


def _experts2_kernel(be_ref, jlo_ref, jhi_ref, nused_ref, h_ref, pos_ref, wgu_ref, wd_ref, y_ref, meta_ref, xg_ref):
    r = pl.program_id(0)
    d = y_ref.shape[1]

    @pl.when(r < nused_ref[0])
    def _():
        e = be_ref[r]
        slot = r * MOE_BM + lax.broadcasted_iota(jnp.int32, (MOE_BM, MOE_TC), 0)
        xg_ref[...] = jnp.zeros_like(xg_ref)

        def chunk(j, carry):
            onehot = jnp.where(pos_ref[e, pl.ds(j, 1), :] == slot, 1.0, 0.0).astype(BF16)
            rows = h_ref[pl.ds(pl.multiple_of(j * MOE_TC, MOE_TC), MOE_TC), :]
            xg_ref[...] += jnp.dot(onehot, rows, preferred_element_type=F32)
            return carry

        lax.fori_loop(jlo_ref[r], jhi_ref[r] + 1, chunk, 0)
        meta_ref[...] = xg_ref[:, d:d + META_LANES]
        y_ref[...] = _swiglu_chunks(xg_ref[:, 0:d].astype(BF16), wgu_ref, wd_ref, D_FF_EXPERT)

    @pl.when(r >= nused_ref[0])
    def _():
        y_ref[...] = jnp.zeros_like(y_ref)
        meta_ref[...] = jnp.zeros_like(meta_ref)


def _experts2(h, pos_row, plan, wgu_bf, wd_bf, n_blocks):
    n, dx = h.shape
    d = dx - META_LANES
    e_r, jlo, jhi, n_used = plan
    grid_spec = pltpu.PrefetchScalarGridSpec(
        num_scalar_prefetch=4,
        grid=(n_blocks,),
        in_specs=[
            _resident((n, dx), lambda r, be, lo, hi, nu: (0, 0)),
            _resident(pos_row.shape, lambda r, be, lo, hi, nu: (0, 0, 0)),
            pl.BlockSpec((None, d, 2 * D_FF_EXPERT), lambda r, be, lo, hi, nu: (be[r], 0, 0),
                         pipeline_mode=pl.Buffered(1)),
            pl.BlockSpec((None, D_FF_EXPERT, d), lambda r, be, lo, hi, nu: (be[r], 0, 0),
                         pipeline_mode=pl.Buffered(1)),
        ],
        out_specs=(pl.BlockSpec((MOE_BM, d), lambda r, be, lo, hi, nu: (r, 0)),
                   pl.BlockSpec((MOE_BM, META_LANES), lambda r, be, lo, hi, nu: (r, 0))),
        scratch_shapes=[pltpu.VMEM((MOE_BM, dx), F32)],
    )
    return pl.pallas_call(
        _experts2_kernel,
        out_shape=(jax.ShapeDtypeStruct((n_blocks * MOE_BM, d), F32),
                   jax.ShapeDtypeStruct((n_blocks * MOE_BM, META_LANES), F32)),
        grid_spec=grid_spec,
        compiler_params=_cparams(("arbitrary",)),
        name="moe_experts",
    )(e_r, jlo, jhi, n_used, h, pos_row, wgu_bf, wd_bf)


def _scatter_kernel(dest_ref, y_ref, o_ref, ybuf_ref, sem):
    r = pl.program_id(0)
    n_steps = pl.num_programs(0)
    buf = r % 2

    def wait_block(b):
        pltpu.make_async_copy(ybuf_ref.at[b], o_ref.at[pl.ds(0, MOE_BM), :], sem.at[b]).wait()

    @pl.when(r >= 2)
    def _():
        wait_block(buf)

    ybuf_ref[buf] = y_ref[...]

    for i in range(MOE_BM):
        pltpu.make_async_copy(ybuf_ref.at[buf, pl.ds(i, 1), :], o_ref.at[pl.ds(dest_ref[0, i], 1), :],
                              sem.at[buf]).start()

    @pl.when(r == n_steps - 1)
    def _():
        wait_block(1 - buf)
        wait_block(buf)


def _scatter_rows(y, dest, n_rows_out):
    n_slots, d = y.shape
    n_blocks = n_slots // MOE_BM
    assert n_blocks >= 2
    return pl.pallas_call(
        _scatter_kernel,
        out_shape=jax.ShapeDtypeStruct((n_rows_out, d), F32),
        grid=(n_blocks,),
        in_specs=[
            pl.BlockSpec((None, 1, MOE_BM), lambda r: (r, 0, 0), memory_space=pltpu.SMEM),
            pl.BlockSpec((MOE_BM, d), lambda r: (r, 0)),
        ],
        out_specs=pl.BlockSpec(memory_space=pl.ANY),
        scratch_shapes=[pltpu.VMEM((2, MOE_BM, d), F32), pltpu.SemaphoreType.DMA((2,))],
        compiler_params=_cparams(("arbitrary",)),
        name="moe_scatter",
    )(dest.reshape(n_blocks, 1, MOE_BM), y)


def _finish_kernel(a_ref, b_ref, wk_ref, x_ref, mod_ref, g_ref, bb_ref, ol_ref, oc_ref, *, n_latent_tiles):
    lane = lax.broadcasted_iota(jnp.int32, wk_ref.shape, 1)
    wk = wk_ref[...]
    w0 = jnp.sum(jnp.where(lane == 0, wk, 0.0), axis=-1, keepdims=True)
    w1 = jnp.sum(jnp.where(lane == 1, wk, 0.0), axis=-1, keepdims=True)
    acc = w0 * a_ref[...] + w1 * b_ref[...]
    z = ALPHA * x_ref[...] + mod_ref[5:6, :] * acc
    out = _layer_norm_rows(z, g_ref[...], bb_ref[...])
    is_latent = pl.program_id(0) < n_latent_tiles

    @pl.when(is_latent)
    def _():
        ol_ref[...] = out

    @pl.when(jnp.logical_not(is_latent))
    def _():
        oc_ref[...] = out


def _finish(rows, wk, x, mod, ln_g, ln_b, n_latent_groups, tm=512):
    g, s, d = x.shape
    n = g * s
    per = s // tm
    nt = n // tm
    nlt = n_latent_groups * per

    def lat_map(i):
        j = jnp.minimum(i, nlt - 1)
        return (j // per, j % per, 0)

    def ctx_map(i):
        j = jnp.maximum(i - nlt, 0)
        return (j // per, j % per, 0)

    return pl.pallas_call(
        functools.partial(_finish_kernel, n_latent_tiles=nlt),
        out_shape=(jax.ShapeDtypeStruct((n_latent_groups, s, d), F32),
                   jax.ShapeDtypeStruct((g - n_latent_groups, s, d), F32)),
        grid=(nt,),
        in_specs=[
            pl.BlockSpec((tm, d), lambda i: (i, 0)),
            pl.BlockSpec((tm, d), lambda i: (nt + i, 0)),
            pl.BlockSpec((tm, LANES), lambda i: (i, 0)),
            pl.BlockSpec((None, tm, d), lambda i: (i // per, i % per, 0)),
            pl.BlockSpec((None, 6, d), lambda i: (i // per, 0, 0)),
            pl.BlockSpec((1, d), lambda i: (0, 0)),
            pl.BlockSpec((1, d), lambda i: (0, 0)),
        ],
        out_specs=(pl.BlockSpec((None, tm, d), lat_map), pl.BlockSpec((None, tm, d), ctx_map)),
        compiler_params=_cparams(("arbitrary",)),
        name="moe_finish",
    )(rows, rows, wk, x, mod, ln_g.reshape(1, d), ln_b.reshape(1, d))


def _route3_kernel(x_ref, mod_ref, rw_ref, rb_ref, h_ref, meta_ref, wk_ref, srank_ref, cnt_ref, tri_ref, run_ref):
    tm, d = x_ref.shape

    @pl.when(pl.program_id(0) == 0)
    def _():
        r = lax.broadcasted_iota(jnp.int32, (tm, tm), 0)
        c = lax.broadcasted_iota(jnp.int32, (tm, tm), 1)
        tri_ref[...] = jnp.where(c <= r, 1.0, 0.0).astype(BF16)
        run_ref[...] = jnp.zeros_like(run_ref)

    h = x_ref[...] * (1.0 + mod_ref[4:5, :]) + mod_ref[3:4, :]
    _store_token_tiles(h_ref, h)
    lane, i1, i2, w1, w2 = _router_top2(h, rw_ref, rb_ref)
    first_is_low = i1 < i2
    e_hi = jnp.where(first_is_low, i2, i1)
    wk_ref[...] = jnp.where(lane == 0.0, jnp.where(first_is_low, w1, w2),
                            jnp.where(lane == 1.0, jnp.where(first_is_low, w2, w1), 0.0))
    tok = (pl.program_id(0) * tm + lax.broadcasted_iota(jnp.int32, (tm, META_LANES), 0))
    meta = jnp.where(lane == 0.0, (tok // 128).astype(F32),
                     jnp.where(lane == 1.0, (tok % 128).astype(F32),
                               jnp.where(lane == 2.0, e_hi, jnp.where(lane == 3.0, 1.0, 0.0))))
    meta_ref[...] = meta.astype(BF16)

    member = (lane == i1) | (lane == i2)
    mem = jnp.where(member, 1.0, 0.0)
    rank = jnp.dot(tri_ref[...], mem.astype(BF16), preferred_element_type=F32) + run_ref[...]
    srank_ref[...] = jnp.where(member, rank, -rank)
    run_ref[...] = rank[tm - 1:tm, :]
    cnt_ref[...] = rank[tm - 1:tm, :]


def _route3(x, mod, router_w, router_b, tm=512):
    g, s, d = x.shape
    n = g * s
    assert n <= 128 * 256
    per = s // tm
    rw = jnp.pad(router_w, ((0, 0), (0, LANES - N_EXPERTS)))
    rb = jnp.pad(router_b, (0, LANES - N_EXPERTS)).reshape(1, LANES)
    tok_lanes = lambda i: (i, 0)
    return pl.pallas_call(
        _route3_kernel,
        out_shape=(jax.ShapeDtypeStruct((n, SUBLANES, LANES), F32), jax.ShapeDtypeStruct((n, META_LANES), BF16),
                   jax.ShapeDtypeStruct((n, LANES), F32), jax.ShapeDtypeStruct((n, LANES), F32),
                   jax.ShapeDtypeStruct((1, LANES), F32)),
        grid=(n // tm,),
        in_specs=[
            pl.BlockSpec((None, tm, d), lambda i: (i // per, i % per, 0)),
            pl.BlockSpec((None, 6, d), lambda i: (i // per, 0, 0)),
            pl.BlockSpec((d, LANES), lambda i: (0, 0)),
            pl.BlockSpec((1, LANES), lambda i: (0, 0)),
        ],
        out_specs=(pl.BlockSpec((tm, SUBLANES, LANES), lambda i: (i, 0, 0)), pl.BlockSpec((tm, META_LANES), tok_lanes),
                   pl.BlockSpec((tm, LANES), tok_lanes), pl.BlockSpec((tm, LANES), tok_lanes),
                   pl.BlockSpec((1, LANES), lambda i: (0, 0))),
        scratch_shapes=[pltpu.VMEM((tm, tm), BF16), pltpu.VMEM((1, LANES), F32)],
        compiler_params=_cparams(("arbitrary",)),
        name="moe_route",
    )(x, mod, rw, rb)


def _slot_meta_kernel(be_ref, jlo_ref, jhi_ref, nused_ref, meta_ref, pos_ref, o_ref):
    r = pl.program_id(0)
    o_ref[...] = jnp.zeros_like(o_ref)

    @pl.when(r < nused_ref[0])
    def _():
        e = be_ref[r]
        slot = r * MOE_BM + lax.broadcasted_iota(jnp.int32, (MOE_BM, MOE_TC), 0)

        def chunk(j, carry):
            onehot = jnp.where(pos_ref[e, pl.ds(j, 1), :] == slot, 1.0, 0.0).astype(BF16)
            rows = meta_ref[pl.ds(pl.multiple_of(j * MOE_TC, MOE_TC), MOE_TC), :]
            o_ref[...] += jnp.dot(onehot, rows, preferred_element_type=F32)
            return carry

        lax.fori_loop(jlo_ref[r], jhi_ref[r] + 1, chunk, 0)


def _slot_meta(meta, pos_row, plan, n_blocks):
    n = meta.shape[0]
    e_r, jlo, jhi, n_used = plan
    grid_spec = pltpu.PrefetchScalarGridSpec(
        num_scalar_prefetch=4,
        grid=(n_blocks,),
        in_specs=[
            _resident((n, META_LANES), lambda r, be, lo, hi, nu: (0, 0)),
            _resident(pos_row.shape, lambda r, be, lo, hi, nu: (0, 0, 0)),
        ],
        out_specs=pl.BlockSpec((MOE_BM, META_LANES), lambda r, be, lo, hi, nu: (r, 0)),
    )
    return pl.pallas_call(
        _slot_meta_kernel,
        out_shape=jax.ShapeDtypeStruct((n_blocks * MOE_BM, META_LANES), F32),
        grid_spec=grid_spec,
        compiler_params=_cparams(("arbitrary",)),
        name="moe_slot_meta",
    )(e_r, jlo, jhi, n_used, meta, pos_row)


def _experts3_kernel(be_ref, nused_ref, src_ref, nsrc_ref, dest_ref, ldest_ref, h_ref, wgu_ref, wd_ref, o_ref,
                     xg_ref, ybuf_ref, sem_g, sem_s):
    r = pl.program_id(0)
    n_steps = pl.num_programs(0)
    cur = r % 2
    nxt = 1 - cur
    d_ff = D_FF_EXPERT

    def gather_row(i, idx_ref, buf):
        return pltpu.make_async_copy(h_ref.at[idx_ref[0, i]], xg_ref.at[buf, i], sem_g.at[buf])

    def scatter_row(i, buf):
        return pltpu.make_async_copy(ybuf_ref.at[buf, i], o_ref.at[dest_ref[0, i]], sem_s.at[buf])

    def wait_gather(buf):
        pltpu.make_async_copy(h_ref.at[pl.ds(0, MOE_BM)], xg_ref.at[buf], sem_g.at[buf]).wait()

    def wait_scatter(buf):
        pltpu.make_async_copy(ybuf_ref.at[buf], o_ref.at[pl.ds(0, MOE_BM)], sem_s.at[buf]).wait()

    zero_rows = jnp.zeros((MOE_BM, D_MODEL), F32)

    def issue_rows(lo, hi):
        for i in range(lo, hi):
            gather_row(i, nsrc_ref, nxt).start()
            scatter_row(i, nxt).start()

    @pl.when(r == 0)
    def _():
        _store_token_tiles(ybuf_ref, zero_rows, 1)

        def first(i, carry):
            gather_row(i, src_ref, 0).start()
            return carry

        lax.fori_loop(0, MOE_BM, first, 0, unroll=8)

    wait_gather(cur)

    @pl.when(r < nused_ref[0])
    def _():
        x_bf = _load_token_tiles(xg_ref, cur).astype(BF16)
        n_chunks = d_ff // FF_CHUNK
        per = -(-MOE_BM // n_chunks)
        acc = None
        for j in range(n_chunks):
            lo = j * FF_CHUNK
            gate = jnp.dot(x_bf, wgu_ref[:, lo:lo + FF_CHUNK], preferred_element_type=F32)
            up = jnp.dot(x_bf, wgu_ref[:, d_ff + lo:d_ff + lo + FF_CHUNK], preferred_element_type=F32)
            act = (gate * _sigmoid(gate) * up).astype(BF16)
            t = jnp.dot(act, wd_ref[lo:lo + FF_CHUNK, :], preferred_element_type=F32)
            acc = t if acc is None else acc + t
            issue_rows(min(j * per, MOE_BM), min((j + 1) * per, MOE_BM))

        @pl.when(r >= 1)
        def _():
            wait_scatter(cur)

        _store_token_tiles(ybuf_ref, acc, cur)

    @pl.when(r >= nused_ref[0])
    def _():
        def both(i, carry):
            gather_row(i, nsrc_ref, nxt).start()
            scatter_row(i, nxt).start()
            return carry

        lax.fori_loop(0, MOE_BM, both, 0, unroll=8)

        @pl.when(r >= 1)
        def _():
            wait_scatter(cur)

        _store_token_tiles(ybuf_ref, zero_rows, cur)

    @pl.when(r == n_steps - 1)
    def _():
        def last(i, carry):
            pltpu.make_async_copy(ybuf_ref.at[cur, i], o_ref.at[ldest_ref[0, i]], sem_s.at[cur]).start()
            return carry

        lax.fori_loop(0, MOE_BM, last, 0, unroll=8)
        wait_scatter(nxt)
        wait_scatter(cur)
        wait_gather(nxt)


def _experts3(h, src, dest, e_r, n_used, wgu_bf, wd_bf, n_blocks):
    d = D_MODEL
    last = n_blocks - 1
    tile_rows = (MOE_BM, SUBLANES, LANES)
    smem_blk = lambda f: pl.BlockSpec((None, 1, MOE_BM), f, memory_space=pltpu.SMEM)
    grid_spec = pltpu.PrefetchScalarGridSpec(
        num_scalar_prefetch=2,
        grid=(n_blocks,),
        in_specs=[
            smem_blk(lambda r, be, nu: (0, 0, 0)),
            smem_blk(lambda r, be, nu: (jnp.minimum(r + 1, last), 0, 0)),
            smem_blk(lambda r, be, nu: (r, 0, 0)),
            smem_blk(lambda r, be, nu: (n_blocks, 0, 0)),
            pl.BlockSpec(memory_space=pl.ANY),
            pl.BlockSpec((None, d, 2 * D_FF_EXPERT), lambda r, be, nu: (be[r], 0, 0)),
            pl.BlockSpec((None, D_FF_EXPERT, d), lambda r, be, nu: (be[r], 0, 0)),
        ],
        out_specs=pl.BlockSpec(memory_space=pl.ANY),
        scratch_shapes=[pltpu.VMEM((2,) + tile_rows, F32), pltpu.VMEM((2,) + tile_rows, F32),
                        pltpu.SemaphoreType.DMA((2,)), pltpu.SemaphoreType.DMA((2,))],
    )
    return pl.pallas_call(
        _experts3_kernel,
        out_shape=jax.ShapeDtypeStruct((n_blocks * MOE_BM + MOE_BM, SUBLANES, LANES), F32),
        grid_spec=grid_spec,
        compiler_params=_cparams(("arbitrary",)),
        name="moe_experts",
    )(e_r, n_used, src, src, dest, dest, h, wgu_bf, wd_bf)


def _moe3(x, mod, router_w, router_b, wgu_bf, wd_bf, ln_g, ln_b, n_latent_groups):
    g, s, d = x.shape
    n = g * s
    n_blocks = 2 * n // MOE_BM + N_EXPERTS
    n_slots = n_blocks * MOE_BM
    h, meta_tok, wk, srank, counts = _route3(x, mod, router_w, router_b)
    pos_tok, e_r, jlo, jhi, n_used = _moe_plan2(srank, counts, n_blocks)
    pos_row = pos_tok.T.reshape(N_EXPERTS, n // MOE_TC, MOE_TC)
    meta = _slot_meta(meta_tok, pos_row, (e_r, jlo, jhi, n_used), n_blocks)
    i32 = jnp.int32
    tok = (meta[:, 0] * 128.0 + meta[:, 1]).astype(i32)
    choice = (meta[:, 2] == jnp.repeat(e_r, MOE_BM).astype(F32)).astype(i32)
    unused = meta[:, 3] < 0.5
    spare = 2 * n + jnp.cumsum(unused.astype(i32)) - 1
    dest = jnp.where(unused, spare, choice * n + tok).reshape(n_blocks, 1, MOE_BM)
    placeholder = (n_slots + jnp.arange(MOE_BM, dtype=i32)).reshape(1, 1, MOE_BM)
    dest_tab = jnp.concatenate([placeholder, dest], axis=0)
    rows = _experts3(h, tok.reshape(n_blocks, 1, MOE_BM), dest_tab, e_r, n_used, wgu_bf, wd_bf, n_blocks)
    return _finish(rows, wk, x, mod, ln_g, ln_b, n_latent_groups)


def _moe2(x, mod, router_w, router_b, wgu_bf, wd_bf, ln_g, ln_b, n_latent_groups):
    g, s, d = x.shape
    n = g * s
    n_blocks = 2 * n // MOE_BM + N_EXPERTS
    n_slots = n_blocks * MOE_BM
    h, wk, srank, counts = _route2(x, mod, router_w, router_b)
    pos_tok, e_r, jlo, jhi, n_used = _moe_plan2(srank, counts, n_blocks)
    pos_row = pos_tok.T.reshape(N_EXPERTS, n // MOE_TC, MOE_TC)
    y, meta = _experts2(h, pos_row, (e_r, jlo, jhi, n_used), wgu_bf, wd_bf, n_blocks)
    tok = (meta[:, 0] * 128.0 + meta[:, 1]).astype(jnp.int32)
    choice = (meta[:, 2] == jnp.repeat(e_r, MOE_BM).astype(F32)).astype(jnp.int32)
    unused = meta[:, 3] < 0.5
    spare = 2 * n + jnp.cumsum(unused.astype(jnp.int32)) - 1
    dest = jnp.where(unused, spare, choice * n + tok)
    rows = _scatter_rows(y, dest, n_slots)
    return _finish(rows, wk, x, mod, ln_g, ln_b, n_latent_groups)


def _inproj_c_kernel(x_ref, mod_ref, w_ref, wg_ref, bg_ref, o_ref, og_ref):
    h = (x_ref[...] * (1.0 + mod_ref[1:2, :]) + mod_ref[0:1, :]).astype(BF16)
    o_ref[...] = jnp.dot(h, w_ref[...], preferred_element_type=F32)
    og_ref[...] = jnp.dot(h, wg_ref[...], preferred_element_type=F32) + bg_ref[...]


def _inproj_c(x, mod, w_bf, wg_bf, bg, tm=512):
    g, s, d = x.shape
    n = w_bf.shape[1]
    ng = wg_bf.shape[1]
    return pl.pallas_call(
        _inproj_c_kernel,
        out_shape=(jax.ShapeDtypeStruct((g, s, n), F32), jax.ShapeDtypeStruct((g, s, ng), F32)),
        grid=(g, s // tm),
        in_specs=[
            pl.BlockSpec((None, tm, d), lambda gi, ti: (gi, ti, 0)),
            pl.BlockSpec((None, 6, d), lambda gi, ti: (gi, 0, 0)),
            _resident((d, n), lambda gi, ti: (0, 0)),
            _resident((d, ng), lambda gi, ti: (0, 0)),
            pl.BlockSpec((1, ng), lambda gi, ti: (0, 0)),
        ],
        out_specs=(pl.BlockSpec((None, tm, n), lambda gi, ti: (gi, ti, 0)),
                   pl.BlockSpec((None, tm, ng), lambda gi, ti: (gi, ti, 0))),
        compiler_params=_cparams(("arbitrary", "arbitrary")),
        name="inproj_c",
    )(x, mod, w_bf, wg_bf, bg)


def _log_sigmoid(x):
    return jnp.minimum(x, 0.0) - jnp.log(1.0 + jnp.exp(-jnp.abs(x)))


MLSTM_L = 128


def _split3_bf16(x):
    hi = x.astype(BF16)
    r1 = x - hi.astype(F32)
    mid = r1.astype(BF16)
    lo = (r1 - mid.astype(F32)).astype(BF16)
    return hi, mid, lo


def _mlstm_kernel(*refs, seq, hg, has_init, emit_state):
    q_ref, k_ref, v_ref, o_ref, gi_ref, gf_ref, hgain_ref = refs[:7]
    pos = 7
    if has_init:
        c0_ref, n0_ref, m0_ref = refs[pos:pos + 3]
        pos += 3
    out_ref = refs[pos]
    pos += 1
    if emit_state:
        co_ref, no_ref, mo_ref = refs[pos:pos + 3]
        pos += 3
    cext_ref, hf_ref, hb_ref, b_ref, g_ref, gmax_ref, mt_ref, wi_ref, en_ref, ws_ref, gt_ref, wc_ref = refs[pos:]

    L = MLSTM_L
    dh = MLSTM_DH
    nh = MLSTM_HEADS
    nc = seq // L
    head0 = pl.program_id(1) * hg
    neg = -jnp.inf

    lane = lax.broadcasted_iota(jnp.int32, (L, LANES), 1)
    lane1 = lax.broadcasted_iota(jnp.int32, (1, LANES), 1)
    row = lax.broadcasted_iota(jnp.int32, (L, L), 0)
    col = lax.broadcasted_iota(jnp.int32, (L, L), 1)
    lower = col <= row
    upper = col >= row
    tri_l = jnp.where(lower, 1.0, 0.0).astype(BF16)
    tri_u = jnp.where(upper, 1.0, 0.0).astype(BF16)
    fwd_lane = lane < nh
    fwd_lane1 = lane1 < nh
    trow = lax.broadcasted_iota(jnp.int32, (L, LANES), 0)

    btot, glast = [], []
    for c in range(nc):
        rows = slice(c * L, (c + 1) * L)
        f = _log_sigmoid(gf_ref[rows, :])
        parts = _split3_bf16(f)
        pre = sum(jnp.dot(tri_l, p, preferred_element_type=F32) for p in parts)
        suf = sum(jnp.dot(tri_u, p, preferred_element_type=F32) for p in parts)
        b = jnp.where(fwd_lane, pre, suf)
        g = gi_ref[rows, :] - b
        gp, gs = g, g
        k = 1
        while k < L:
            gp = jnp.where(trow >= k, jnp.maximum(gp, pltpu.roll(gp, k, 0)), gp)
            gs = jnp.where(trow < L - k, jnp.maximum(gs, pltpu.roll(gs, L - k, 0)), gs)
            k *= 2
        gmax = jnp.where(fwd_lane, gp, gs)
        b_ref[rows, :] = b
        g_ref[rows, :] = g
        gmax_ref[rows, :] = gmax
        btot.append(jnp.where(fwd_lane1, b[L - 1:L, :], b[0:1, :]))
        glast.append(jnp.where(fwd_lane1, gmax[L - 1:L, :], gmax[0:1, :]))

    m_init = m0_ref[...] if has_init else jnp.zeros((1, LANES), F32)
    mf, mb = m_init, m_init
    ms_f, mn_f, ms_b, mn_b = [None] * nc, [None] * nc, [None] * nc, [None] * nc
    for c in range(nc):
        ms_f[c] = mf
        mf = btot[c] + jnp.maximum(mf, glast[c])
        mn_f[c] = mf
        cb = nc - 1 - c
        ms_b[cb] = mb
        mb = btot[cb] + jnp.maximum(mb, glast[cb])
        mn_b[cb] = mb
    m_final = jnp.where(fwd_lane1, mf, mb)

    for c in range(nc):
        rows = slice(c * L, (c + 1) * L)
        m_start = jnp.where(fwd_lane1, ms_f[c], ms_b[c])
        m_next = jnp.where(fwd_lane1, mn_f[c], mn_b[c])
        g = g_ref[rows, :]
        mt = jnp.maximum(m_start, gmax_ref[rows, :])
        mt_ref[rows, :] = mt
        wi_ref[rows, :] = jnp.exp(m_start - mt)
        en_ref[rows, :] = jnp.exp(-(b_ref[rows, :] + mt))
        ws_ref[rows, :] = jnp.exp(btot[c] + g - m_next)
        gt_ref[c] = g.T
        wc_ref[c:c + 1, :] = jnp.exp(btot[c] + m_start - m_next)

    lane_d = lax.broadcasted_iota(jnp.int32, (dh, dh), 1)
    for d in range(2):
        for hh in range(hg):
            idx = d * hg + hh
            if has_init:
                cext_ref[idx, :, 0:dh] = c0_ref[d, hh]
                n0_tile = jnp.where(lax.broadcasted_iota(jnp.int32, (dh, dh), 0) == 0, n0_ref[d, hh], 0.0)
                cext_ref[idx, :, dh:2 * dh] = n0_tile.T
            else:
                cext_ref[idx] = jnp.zeros((dh, 2 * dh), F32)

    ones_col = jnp.where(lane == 0, 1.0, 0.0).astype(BF16)
    nt = (((1,), (1,)), ((), ()))
    tn = (((0,), (0,)), ((), ()))

    def column(x, j):
        return jnp.sum(jnp.where(lane == j, x, 0.0), axis=-1, keepdims=True)

    def one_direction(d, hh, c, s_qk, q_bf, k_s, v_ext, v_bf):
        idx = d * hg + hh
        j = d * nh + head0 + hh
        rows = pl.ds(pl.multiple_of(c * L, L), L)
        mt = column(mt_ref[rows, :], j)
        wi = column(wi_ref[rows, :], j)
        en = column(en_ref[rows, :], j)
        ws = column(ws_ref[rows, :], j)
        g_r = gt_ref[c, pl.ds(j, 1), :]
        w_c = jnp.sum(jnp.where(lane1 == j, wc_ref[pl.ds(c, 1), :], 0.0), axis=-1, keepdims=True)
        causal = lower if d == 0 else upper
        p = s_qk * jnp.exp(jnp.where(causal, g_r - mt, neg))
        qc = jnp.dot(q_bf, cext_ref[idx].astype(BF16), preferred_element_type=F32)
        num = wi * qc[:, 0:dh] + jnp.dot(p.astype(BF16), v_bf, preferred_element_type=F32)
        den = wi * qc[:, dh:dh + 1] + jnp.sum(p, axis=-1, keepdims=True)
        h = num / jnp.maximum(jnp.abs(den), en)
        upd = lax.dot_general((ws * k_s).astype(BF16), v_ext, tn, preferred_element_type=F32)
        cext_ref[idx] = w_c * cext_ref[idx] + upd
        return h

    def load_chunk(hh, c):
        sl = (pl.ds(pl.multiple_of(c * L, L), L), slice(hh * dh, (hh + 1) * dh))
        q_bf = q_ref[sl].astype(BF16)
        k_s = k_ref[sl] * (dh ** -0.5)
        v_bf = v_ref[sl].astype(BF16)
        v_ext = jnp.concatenate([v_bf, ones_col], axis=-1)
        s_qk = lax.dot_general(q_bf, k_s.astype(BF16), nt, preferred_element_type=F32)
        return s_qk, q_bf, k_s, v_ext, v_bf

    def step(c, carry):
        cb = nc - 1 - c
        for hh in range(hg):
            h = one_direction(0, hh, c, *load_chunk(hh, c))
            hf_ref[pl.ds(pl.multiple_of(c * L, L), L), hh * dh:(hh + 1) * dh] = h
        for hh in range(hg):
            h = one_direction(1, hh, cb, *load_chunk(hh, cb))
            hb_ref[pl.ds(pl.multiple_of(cb * L, L), L), hh * dh:(hh + 1) * dh] = h
        return carry

    lax.fori_loop(0, nc, step, 0)

    for hh in range(hg):
        cs = slice(hh * dh, (hh + 1) * dh)
        hs = hf_ref[:, cs] + hb_ref[:, cs]
        mu = jnp.mean(hs, axis=-1, keepdims=True)
        hc = hs - mu
        var = jnp.mean(hc * hc, axis=-1, keepdims=True)
        hn = hc * lax.rsqrt(var + LN_EPS) * hgain_ref[:, cs]
        out_ref[:, cs] = (_sigmoid(o_ref[:, cs]) * hn).astype(out_ref.dtype)

    if emit_state:
        for d in range(2):
            for hh in range(hg):
                idx = d * hg + hh
                co_ref[d, hh] = cext_ref[idx, :, 0:dh]
                no_ref[d, hh] = cext_ref[idx, :, dh:2 * dh].T[0:1, :]
        mo_ref[...] = m_final


def _mlstm(proj, gates, head_g, g0, n_seq, seq, hg, init=None, emit_state=False):
    g, s, _ = proj.shape
    per_group = s // seq
    n_hg = MLSTM_HEADS // hg
    w = hg * MLSTM_DH
    nc = seq // MLSTM_L
    n_blocks = D_MODEL // w

    def tok_map(colblock):
        return lambda b, hi: (g0 + b // per_group, b % per_group, colblock * n_blocks + hi)

    def gate_map(half):
        return lambda b, hi: (g0 + b // per_group, b % per_group, half)

    args = [proj, proj, proj, proj, gates, gates, head_g.reshape(1, D_MODEL)]
    in_specs = [
        pl.BlockSpec((None, seq, w), tok_map(0)),
        pl.BlockSpec((None, seq, w), tok_map(1)),
        pl.BlockSpec((None, seq, w), tok_map(2)),
        pl.BlockSpec((None, seq, w), tok_map(3)),
        pl.BlockSpec((None, seq, LANES), gate_map(0)),
        pl.BlockSpec((None, seq, LANES), gate_map(1)),
        pl.BlockSpec((1, w), lambda b, hi: (0, hi)),
    ]
    if init is not None:
        c0, n0, m0 = init
        m0_lanes = jnp.pad(m0.reshape(n_seq, 1, 2 * MLSTM_HEADS), ((0, 0), (0, 0), (0, LANES - 2 * MLSTM_HEADS)))
        args += [c0, n0.reshape(n0.shape[:-1] + (1, MLSTM_DH)), m0_lanes]
        in_specs += [
            pl.BlockSpec((None, 2, hg, MLSTM_DH, MLSTM_DH), lambda b, hi: (b, 0, hi, 0, 0)),
            pl.BlockSpec((None, 2, hg, 1, MLSTM_DH), lambda b, hi: (b, 0, hi, 0, 0)),
            pl.BlockSpec((None, 1, LANES), lambda b, hi: (b, 0, 0)),
        ]

    out_shape = [jax.ShapeDtypeStruct((n_seq // per_group, s, D_MODEL), BF16)]
    out_specs = [pl.BlockSpec((None, seq, w), lambda b, hi: (b // per_group, b % per_group, hi))]
    if emit_state:
        out_shape += [
            jax.ShapeDtypeStruct((n_seq, 2, MLSTM_HEADS, MLSTM_DH, MLSTM_DH), F32),
            jax.ShapeDtypeStruct((n_seq, 2, MLSTM_HEADS, 1, MLSTM_DH), F32),
            jax.ShapeDtypeStruct((n_seq, n_hg, 1, LANES), F32),
        ]
        out_specs += [
            pl.BlockSpec((None, 2, hg, MLSTM_DH, MLSTM_DH), lambda b, hi: (b, 0, hi, 0, 0)),
            pl.BlockSpec((None, 2, hg, 1, MLSTM_DH), lambda b, hi: (b, 0, hi, 0, 0)),
            pl.BlockSpec((None, None, 1, LANES), lambda b, hi: (b, hi, 0, 0)),
        ]

    tok_scratch = pltpu.VMEM((seq, LANES), F32)
    return pl.pallas_call(
        functools.partial(_mlstm_kernel, seq=seq, hg=hg, has_init=init is not None, emit_state=emit_state),
        out_shape=tuple(out_shape),
        grid=(n_seq, n_hg),
        in_specs=in_specs,
        out_specs=tuple(out_specs),
        scratch_shapes=[
            pltpu.VMEM((2 * hg, MLSTM_DH, 2 * MLSTM_DH), F32),
            pltpu.VMEM((seq, w), F32),
            pltpu.VMEM((seq, w), F32),
        ] + [tok_scratch] * 7 + [
            pltpu.VMEM((nc, LANES, MLSTM_L), F32),
            pltpu.VMEM((max(nc, 8), LANES), F32),
        ],
        compiler_params=_cparams(("arbitrary", "arbitrary")),
        name="mlstm_%d" % seq,
    )(*args)


def _mlstm_kernel_old(*refs, seq, hg, has_init, emit_state):
    q_ref, k_ref, v_ref, o_ref, gc_ref, gr_ref, hgain_ref = refs[:7]
    pos = 7
    if has_init:
        c0_ref, n0_ref, m0_ref = refs[pos:pos + 3]
        pos += 3
    out_ref = refs[pos]
    pos += 1
    if emit_state:
        co_ref, no_ref, mo_ref = refs[pos:pos + 3]
        pos += 3
    cext_ref, hf_ref, hb_ref = refs[pos:pos + 3]

    L = MLSTM_CHUNK
    dh = MLSTM_DH
    nc = seq // L
    head0 = pl.program_id(1) * hg
    neg = -jnp.inf

    lane_d = lax.broadcasted_iota(jnp.int32, (dh, dh), 1)
    for d in range(2):
        for hh in range(hg):
            idx = d * hg + hh
            if has_init:
                cext_ref[idx, :, 0:dh] = c0_ref[d, hh]
                n0_tile = jnp.where(lax.broadcasted_iota(jnp.int32, (dh, dh), 0) == 0, n0_ref[d, hh], 0.0)
                cext_ref[idx, :, dh:2 * dh] = n0_tile.T
            else:
                cext_ref[idx] = jnp.zeros((dh, 2 * dh), F32)

    row = lax.broadcasted_iota(jnp.int32, (L, L), 0)
    col = lax.broadcasted_iota(jnp.int32, (L, L), 1)
    lower = col <= row
    upper = col >= row
    lane_g = lax.broadcasted_iota(jnp.int32, (L, LANES), 1)
    ones_col = jnp.where(lane_g == 0, 1.0, 0.0).astype(BF16)
    nt = (((1,), (1,)), ((), ()))
    tn = (((0,), (0,)), ((), ()))

    def one_direction(d, hh, c, s_qk, q_bf, k_s, v_ext, v_bf, m_prev):
        idx = d * hg + hh
        head = head0 + hh
        causal, anti = (lower, upper) if d == 0 else (upper, lower)
        gates_c = gc_ref[pl.ds(c * L, L), :]

        def col_of(j):
            return jnp.sum(jnp.where(lane_g == j, gates_c, 0.0), axis=-1, keepdims=True)

        i_c = col_of((2 * d) * MLSTM_HEADS + head)
        f_c = _log_sigmoid(col_of((2 * d + 1) * MLSTM_HEADS + head))
        i_r = gr_ref[2 * d, hh, pl.ds(c, 1), :]
        f_r = _log_sigmoid(gr_ref[2 * d + 1, hh, pl.ds(c, 1), :])

        b_c = jnp.sum(jnp.where(causal, f_r, 0.0), axis=1, keepdims=True)
        b_r = jnp.sum(jnp.where(anti, f_c, 0.0), axis=0, keepdims=True)
        b_tot = jnp.sum(f_r, axis=1, keepdims=True)
        dmat = jnp.where(causal, b_c - b_r + i_r, neg)
        m_inter = b_c + m_prev
        m_t = jnp.maximum(m_inter, jnp.max(dmat, axis=-1, keepdims=True))
        w_inter = jnp.exp(m_inter - m_t)
        p = s_qk * jnp.exp(dmat - m_t)
        qc = jnp.dot(q_bf, cext_ref[idx].astype(BF16), preferred_element_type=F32)
        num = w_inter * qc[:, 0:dh] + jnp.dot(p.astype(BF16), v_bf, preferred_element_type=F32)
        den = w_inter * qc[:, dh:dh + 1] + jnp.sum(p, axis=-1, keepdims=True)
        h = num / jnp.maximum(jnp.abs(den), jnp.exp(-m_t))
        last = L - 1 if d == 0 else 0
        m_new = m_t[last:last + 1, :]
        w_c = jnp.exp(b_tot + m_prev - m_new)
        w_s = jnp.exp(b_tot - b_c + i_c - m_new)
        upd = lax.dot_general((w_s * k_s).astype(BF16), v_ext, tn, preferred_element_type=F32)
        cext_ref[idx] = w_c * cext_ref[idx] + upd
        return h, m_new

    def load_chunk(hh, c):
        sl = (pl.ds(c * L, L), slice(hh * dh, (hh + 1) * dh))
        q_bf = q_ref[sl].astype(BF16)
        k_s = k_ref[sl] * (dh ** -0.5)
        v_bf = v_ref[sl].astype(BF16)
        v_ext = jnp.concatenate([v_bf, ones_col], axis=-1)
        s_qk = lax.dot_general(q_bf, k_s.astype(BF16), nt, preferred_element_type=F32)
        return s_qk, q_bf, k_s, v_ext, v_bf

    def step(c, ms):
        cb = nc - 1 - c
        new_ms = []
        for hh in range(hg):
            h, m_new = one_direction(0, hh, c, *load_chunk(hh, c), ms[hh])
            hf_ref[pl.ds(c * L, L), hh * dh:(hh + 1) * dh] = h
            new_ms.append(m_new)
        for hh in range(hg):
            h, m_new = one_direction(1, hh, cb, *load_chunk(hh, cb), ms[hg + hh])
            hb_ref[pl.ds(cb * L, L), hh * dh:(hh + 1) * dh] = h
            new_ms.append(m_new)
        return tuple(new_ms)

    if has_init:
        ms0 = tuple(m0_ref[d, hh] for d in range(2) for hh in range(hg))
    else:
        ms0 = tuple(jnp.zeros((1, 1), F32) for _ in range(2 * hg))
    ms = lax.fori_loop(0, nc, step, ms0)

    for hh in range(hg):
        cs = slice(hh * dh, (hh + 1) * dh)
        hs = hf_ref[:, cs] + hb_ref[:, cs]
        mu = jnp.mean(hs, axis=-1, keepdims=True)
        hc = hs - mu
        var = jnp.mean(hc * hc, axis=-1, keepdims=True)
        hn = hc * lax.rsqrt(var + LN_EPS) * hgain_ref[:, cs]
        out_ref[:, cs] = (_sigmoid(o_ref[:, cs]) * hn).astype(out_ref.dtype)

    if emit_state:
        for d in range(2):
            for hh in range(hg):
                idx = d * hg + hh
                co_ref[d, hh] = cext_ref[idx, :, 0:dh]
                no_ref[d, hh] = cext_ref[idx, :, dh:2 * dh].T[0:1, :]
                mo_ref[d, hh] = jnp.broadcast_to(ms[idx], (1, LANES))


def _mlstm_old(proj, gates, head_g, g0, n_seq, seq, hg, init=None, emit_state=False):
    g, s, _ = proj.shape
    per_group = s // seq
    n_hg = MLSTM_HEADS // hg
    w = hg * MLSTM_DH
    nc = seq // MLSTM_CHUNK
    n_blocks = D_MODEL // w

    g_seq = gates[g0:g0 + n_seq // per_group].reshape(n_seq, seq, LANES)
    g_row = g_seq[:, :, :N_GATES * MLSTM_HEADS].transpose(0, 2, 1).reshape(
        n_seq, N_GATES, MLSTM_HEADS, nc, MLSTM_CHUNK)

    def tok_map(colblock):
        return lambda b, hi: (g0 + b // per_group, b % per_group, colblock * n_blocks + hi)

    args = [proj, proj, proj, proj, g_seq, g_row, head_g.reshape(1, D_MODEL)]
    in_specs = [
        pl.BlockSpec((None, seq, w), tok_map(0)),
        pl.BlockSpec((None, seq, w), tok_map(1)),
        pl.BlockSpec((None, seq, w), tok_map(2)),
        pl.BlockSpec((None, seq, w), tok_map(3)),
        pl.BlockSpec((None, seq, LANES), lambda b, hi: (b, 0, 0)),
        pl.BlockSpec((None, N_GATES, hg, nc, MLSTM_CHUNK), lambda b, hi: (b, 0, hi, 0, 0)),
        pl.BlockSpec((1, w), lambda b, hi: (0, hi)),
    ]
    if init is not None:
        c0, n0, m0 = init
        args += [c0, n0.reshape(n0.shape + (1,)), m0.reshape(m0.shape + (1, 1))]
        in_specs += [
            pl.BlockSpec((None, 2, hg, MLSTM_DH, MLSTM_DH), lambda b, hi: (b, 0, hi, 0, 0)),
            pl.BlockSpec((None, 2, hg, 1, MLSTM_DH), lambda b, hi: (b, 0, hi, 0, 0)),
            pl.BlockSpec((None, 2, hg, 1, 1), lambda b, hi: (b, 0, hi, 0, 0)),
        ]
    out_shape = [jax.ShapeDtypeStruct((n_seq // per_group, s, D_MODEL), BF16)]
    out_specs = [pl.BlockSpec((None, seq, w), lambda b, hi: (b // per_group, b % per_group, hi))]
    if emit_state:
        out_shape += [
            jax.ShapeDtypeStruct((n_seq, 2, MLSTM_HEADS, MLSTM_DH, MLSTM_DH), F32),
            jax.ShapeDtypeStruct((n_seq, 2, MLSTM_HEADS, 1, MLSTM_DH), F32),
            jax.ShapeDtypeStruct((n_seq, 2, MLSTM_HEADS, 1, LANES), F32),
        ]
        out_specs += [
            pl.BlockSpec((None, 2, hg, MLSTM_DH, MLSTM_DH), lambda b, hi: (b, 0, hi, 0, 0)),
            pl.BlockSpec((None, 2, hg, 1, MLSTM_DH), lambda b, hi: (b, 0, hi, 0, 0)),
            pl.BlockSpec((None, 2, hg, 1, LANES), lambda b, hi: (b, 0, hi, 0, 0)),
        ]

    return pl.pallas_call(
        functools.partial(_mlstm_kernel, seq=seq, hg=hg, has_init=init is not None, emit_state=emit_state),
        out_shape=tuple(out_shape),
        grid=(n_seq, n_hg),
        in_specs=in_specs,
        out_specs=tuple(out_specs),
        scratch_shapes=[
            pltpu.VMEM((2 * hg, MLSTM_DH, 2 * MLSTM_DH), F32),
            pltpu.VMEM((seq, w), F32),
            pltpu.VMEM((seq, w), F32),
        ],
        compiler_params=_cparams(("arbitrary", "arbitrary")),
        name="mlstm_%d" % seq,
    )(*args)


def kernel(x_prompt, x_sample, c, cache_k, cache_v, state_C, state_n, state_m, c_ctx, ada_w, ada_b, ln_g, ln_b, w_in_a, diff_lambda, diff_norm_g, pool_w, pool_scale, w_out_a, ffn_w_gu, ffn_w_down, w_in_c, b_gates_c, mlstm_norm_g, w_out_c, router_w, router_b, moe_w_gu, moe_w_down):
    n_ctx, seq_ctx, d = x_prompt.shape
    n_lat, seq_lat, _ = x_sample.shape
    assert d == D_MODEL and (n_ctx * seq_ctx) % seq_lat == 0 and seq_lat % seq_ctx == 0
    gl = n_lat
    gc = n_ctx * seq_ctx // seq_lat
    s = seq_lat

    x_ctx = x_prompt.reshape(gc, s, d)
    cvec = jnp.concatenate([c, jnp.broadcast_to(c_ctx[None, :], (gc, d))], axis=0)
    mod_all = _modulation(cvec, ada_w, ada_b).reshape(DEPTH, gl + gc, 6, d)

    mod = mod_all[0]
    lam_init = 0.8 - 0.6 * math.exp(-0.3 * 0)
    cos_t, sin_t = _rope_tables(s)
    proj = _inproj_a(x_sample, x_ctx, mod, w_in_a[0].astype(BF16), cos_t, sin_t)
    norm_g = diff_norm_g[0].reshape(1, LANES)
    attn_c, new_k, new_v = _attn_context(proj, diff_lambda[0], norm_g, gl, n_ctx, seq_ctx, lam_init)
    attn_l = _attn_latent(proj, cache_k, cache_v, diff_lambda[0], norm_g, gl, lam_init)
    pool_c = _pool(proj, pool_w[0], pool_scale[0], gl, gc, seq_ctx)
    pool_l = _pool(proj, pool_w[0], pool_scale[0], 0, gl, seq_lat)
    w_out = w_out_a[0].astype(BF16)
    x = _outproj([(attn_l, attn_c), (pool_l, pool_c)], [w_out[:DIFF_WIDTH], w_out[DIFF_WIDTH:]],
                 x_sample, x_ctx, mod, ln_g[0, 0], ln_b[0, 0], 2)
    x, (moe_wgu_bf, moe_wd_bf) = _ffn(x, mod, ffn_w_gu[0].astype(BF16), ffn_w_down[0].astype(BF16),
                                       ln_g[0, 1], ln_b[0, 1], (moe_w_gu[0], moe_w_down[0]))

    mod = mod_all[1]
    n_main = 4 * D_MODEL
    w_main = w_in_c[0][:, :n_main].astype(BF16)
    nh = MLSTM_HEADS
    wg4 = w_in_c[0][:, n_main:].reshape(d, N_GATES, nh)
    bg4 = b_gates_c[0].reshape(1, N_GATES, nh)
    lane_pad = ((0, 0), (0, LANES - 2 * nh))

    def gate_lanes(a):
        return jnp.concatenate([jnp.pad(jnp.concatenate([a[:, 0], a[:, 2]], axis=-1), lane_pad),
                                jnp.pad(jnp.concatenate([a[:, 1], a[:, 3]], axis=-1), lane_pad)], axis=-1)

    proj, gates = _inproj_c(x, mod, w_main, gate_lanes(wg4).astype(BF16), gate_lanes(bg4))
    mix_c, new_c, new_n, new_m = _mlstm(proj, gates, mlstm_norm_g[0], gl, n_ctx, seq_ctx, MLSTM_HEADS,
                                        emit_state=True)
    (mix_l,) = _mlstm(proj, gates, mlstm_norm_g[0], 0, n_lat, seq_lat, 4,
                      init=(state_C[:, 0], state_n[:, 0], state_m[:, 0]))
    x = _outproj([(mix_l, mix_c)], [w_out_c[0].astype(BF16)], x, None, mod, ln_g[1, 0], ln_b[1, 0], 2)
    y_sample, y_ctx = _moe2(x, mod, router_w[0], router_b[0], moe_wgu_bf, moe_wd_bf,
                            ln_g[1, 1], ln_b[1, 1], gl)
    y_prompt = y_ctx.reshape(n_ctx, seq_ctx, d)
    new_m = new_m[:, 0, 0, :2 * MLSTM_HEADS].reshape(n_ctx, 2, MLSTM_HEADS)
    return (y_prompt, y_sample, new_k, new_v, new_c[:, None], new_n[..., 0, :][:, None], new_m[:, None])
```

```python
import functools
import math

import jax
import jax.numpy as jnp
from jax import lax
from jax.experimental import pallas as pl
from jax.experimental.pallas import tpu as pltpu

F32 = jnp.float32
BF16 = jnp.bfloat16

D_MODEL = 1024
GRID_W = 64
ROPE_BASE = 10000.0
DIFF_HEADS = 4
DIFF_DH = 64
DIFF_WIDTH = DIFF_HEADS * 2 * DIFF_DH
POOL_GROUPS = 4
POOL_GC = 128
POOL_WIDTH = POOL_GROUPS * POOL_GC
POOL_WINDOWS = (2, 4, 8, 16)
MLSTM_HEADS = 8
MLSTM_DH = 128
N_GATES = 4
D_FF = 2816
N_EXPERTS = 8
D_FF_EXPERT = 1792
LN_EPS = 1e-5
DEPTH = 2
ALPHA = (2.0 * DEPTH) ** 0.25

LANES = 128
SUBLANES = 8
FF_CHUNK = 256
VMEM_LIMIT = 56 * 1024 * 1024


def _cparams(sem, **kw):
    return pltpu.CompilerParams(dimension_semantics=sem, vmem_limit_bytes=VMEM_LIMIT, **kw)


def _resident(shape, index_map):
    return pl.BlockSpec(shape, index_map, pipeline_mode=pl.Buffered(1))


def _layer_norm_rows(z, g, b):
    mu = jnp.mean(z, axis=-1, keepdims=True)
    zc = z - mu
    var = jnp.mean(zc * zc, axis=-1, keepdims=True)
    return zc * lax.rsqrt(var + LN_EPS) * g + b


def _sigmoid(x):
    return 1.0 / (1.0 + jnp.exp(-x))


def _split_bf16(x):
    hi = x.astype(BF16)
    lo = (x - hi.astype(F32)).astype(BF16)
    return hi, lo


def _cast_rider_specs(weights, n_steps, per):
    specs = []
    for w in weights:
        e, rows, cols = w.shape
        per_e = next((k for k in range(n_steps // e, 0, -1)
                      if n_steps % (e * k) == 0 and rows % (2 * SUBLANES * k) == 0), None)
        if per_e is None:
            return None
        hold = n_steps // (e * per_e)

        def slab_map(gi, ti, per_e=per_e, hold=hold):
            slab = (gi * per + ti) // hold
            return (slab // per_e, slab % per_e, 0)

        specs.append(pl.BlockSpec((None, rows // per_e, cols), slab_map))
    return specs


def _cast_slabs(cast_in, cast_out):
    for src, dst in zip(cast_in, cast_out):
        dst[...] = src[...].astype(BF16)


def _mod_kernel(c_ref, w_ref, b_ref, o_ref):
    c = c_ref[...]
    h = (c * _sigmoid(c)).astype(BF16)
    o_ref[...] = jnp.dot(h, w_ref[...].astype(BF16), preferred_element_type=F32) + b_ref[...]


def _modulation(cvec, ada_w, ada_b):
    depth, d, n = ada_w.shape
    g = cvec.shape[0]
    tn = 1536
    return pl.pallas_call(
        _mod_kernel,
        out_shape=jax.ShapeDtypeStruct((depth, g, n), F32),
        grid=(depth, n // tn),
        in_specs=[
            pl.BlockSpec((g, d), lambda l, j: (0, 0)),
            pl.BlockSpec((None, d, tn), lambda l, j: (l, 0, j)),
            pl.BlockSpec((None, 1, tn), lambda l, j: (l, 0, j)),
        ],
        out_specs=pl.BlockSpec((None, g, tn), lambda l, j: (l, 0, j)),
        compiler_params=_cparams(("arbitrary", "arbitrary")),
        name="modulation",
    )(cvec, ada_w, ada_b.reshape(depth, 1, n))


def _rot_half16(x):
    lane = lax.broadcasted_iota(jnp.int32, x.shape, 1)
    return jnp.where((lane % 32) < 16, pltpu.roll(x, LANES - 16, 1), pltpu.roll(x, 16, 1))


def _two_stream_specs(tm, d, gl, ctx_first_group=0):
    return [pl.BlockSpec((None, tm, d), lambda gi, ti: (jnp.minimum(gi, gl - 1), jnp.where(gi < gl, ti, 0), 0)),
            pl.BlockSpec((None, tm, d), lambda gi, ti: (ctx_first_group + jnp.maximum(gi - gl, 0),
                                                        jnp.where(gi < gl, 0, ti), 0))]


def _inproj_a_kernel(xl_ref, xc_ref, mod_ref, w_ref, cos_ref, sin_ref, *rest, n_latent_groups, n_cast):
    cast_in, o_ref, cast_out = rest[:n_cast], rest[n_cast], rest[n_cast + 1:]
    _cast_slabs(cast_in, cast_out)
    x = jnp.where(pl.program_id(0) < n_latent_groups, xl_ref[...], xc_ref[...])
    h = x * (1.0 + mod_ref[1:2, :]) + mod_ref[0:1, :]
    p = jnp.dot(h.astype(BF16), w_ref[...], preferred_element_type=F32)
    cos = cos_ref[...]
    sin = sin_ref[...]
    n_rope = 2 * DIFF_WIDTH // LANES
    for j in range(n_rope):
        blk = p[:, j * LANES:(j + 1) * LANES]
        o_ref[:, j * LANES:(j + 1) * LANES] = blk * cos + _rot_half16(blk) * sin
    o_ref[:, n_rope * LANES:] = p[:, n_rope * LANES:]


def _inproj_a(x_lat, x_ctx, mod, w_bf, cos_t, sin_t, cast_weights=(), tm=512):
    n_latent_groups, s, d = x_lat.shape
    g = n_latent_groups + x_ctx.shape[0]
    n = w_bf.shape[1]
    per = s // tm
    cast_specs = _cast_rider_specs(cast_weights, g * per, per)
    if cast_specs is None:
        proj, _ = _inproj_a(x_lat, x_ctx, mod, w_bf, cos_t, sin_t, (), tm)
        return proj, tuple(w.astype(BF16) for w in cast_weights)

    def table_map(gi, ti):
        return (jnp.where(gi >= n_latent_groups, 1, 0), ti, 0)

    outs = pl.pallas_call(
        functools.partial(_inproj_a_kernel, n_latent_groups=n_latent_groups, n_cast=len(cast_weights)),
        out_shape=(jax.ShapeDtypeStruct((g, s, n), F32),) + tuple(
            jax.ShapeDtypeStruct(w.shape, BF16) for w in cast_weights),
        grid=(g, per),
        in_specs=_two_stream_specs(tm, d, n_latent_groups) + [
            pl.BlockSpec((None, 6, d), lambda gi, ti: (gi, 0, 0)),
            _resident((d, n), lambda gi, ti: (0, 0)),
            pl.BlockSpec((None, tm, LANES), table_map),
            pl.BlockSpec((None, tm, LANES), table_map),
        ] + cast_specs,
        out_specs=(pl.BlockSpec((None, tm, n), lambda gi, ti: (gi, ti, 0)),) + tuple(cast_specs),
        compiler_params=_cparams(("arbitrary", "arbitrary")),
        name="inproj_a",
    )(x_lat, x_ctx, mod, w_bf, cos_t, sin_t, *cast_weights)
    return outs[0], outs[1:]


def _rope_tables(n_tokens):
    rows = n_tokens // GRID_W
    row_pos = jnp.repeat(jnp.arange(rows), GRID_W).astype(F32)
    col_pos = jnp.tile(jnp.arange(GRID_W), rows).astype(F32)
    n_freq = DIFF_DH // 4
    inv_freq = jnp.power(ROPE_BASE, -jnp.arange(n_freq, dtype=F32) / n_freq)
    ang = jnp.stack([row_pos[:, None] * inv_freq, col_pos[:, None] * inv_freq], axis=1)
    cos, sin = jnp.cos(ang), jnp.sin(ang)
    cos64 = jnp.concatenate([cos[:, 0], cos[:, 0], cos[:, 1], cos[:, 1]], axis=-1)
    sin64 = jnp.concatenate([-sin[:, 0], sin[:, 0], -sin[:, 1], sin[:, 1]], axis=-1)
    cos_l = jnp.tile(cos64, (1, LANES // DIFF_DH))
    sin_l = jnp.tile(sin64, (1, LANES // DIFF_DH))
    cos_t = jnp.stack([cos_l, jnp.ones_like(cos_l)])
    sin_t = jnp.stack([sin_l, jnp.zeros_like(sin_l)])
    return cos_t, sin_t


def _diff_attn_kernel(*refs, n_pieces, n_heads, lam_init, emit_kv):
    lam_ref, ng_ref, q_ref = refs[:3]
    kv_refs = refs[3:3 + 2 * n_pieces]
    o_ref = refs[3 + 2 * n_pieces]

    lp = lam_ref[...]
    lam = (jnp.exp(jnp.sum(lp[0:1] * lp[1:2], axis=-1, keepdims=True))
           - jnp.exp(jnp.sum(lp[2:3] * lp[3:4], axis=-1, keepdims=True)) + lam_init)
    nt = (((1,), (1,)), ((), ()))

    def softmax_pieces(ss):
        m = functools.reduce(jnp.maximum, [jnp.max(s, axis=-1, keepdims=True) for s in ss])
        es = [jnp.exp(s - m) for s in ss]
        l = functools.reduce(jnp.add, [jnp.sum(e, axis=-1, keepdims=True) for e in es])
        return [e / l for e in es]

    for h in range(n_heads):
        hs = slice(h * LANES, (h + 1) * LANES)
        q = q_ref[:, hs] * (DIFF_DH ** -0.5)
        lane = lax.broadcasted_iota(jnp.int32, q.shape, 1)
        q1 = jnp.where(lane < DIFF_DH, q, 0.0).astype(BF16)
        q2 = jnp.where(lane >= DIFF_DH, q, 0.0).astype(BF16)
        s1, s2, vs = [], [], []
        for i in range(n_pieces):
            kb = kv_refs[2 * i][:, hs].astype(BF16)
            vs.append(kv_refs[2 * i + 1][:, hs].astype(BF16))
            s1.append(lax.dot_general(q1, kb, nt, preferred_element_type=F32))
            s2.append(lax.dot_general(q2, kb, nt, preferred_element_type=F32))
        p1 = softmax_pieces(s1)
        p2 = softmax_pieces(s2)
        o = None
        for i in range(n_pieces):
            a = (p1[i] - lam * p2[i]).astype(BF16)
            t = jnp.dot(a, vs[i], preferred_element_type=F32)
            o = t if o is None else o + t
        o = o * lax.rsqrt(jnp.mean(o * o, axis=-1, keepdims=True) + LN_EPS)
        o_ref[:, hs] = (o * ng_ref[...] * (1.0 - lam_init)).astype(o_ref.dtype)
        if emit_kv:
            ko_ref, vo_ref = refs[4 + 2 * n_pieces:]
            ko_ref[h] = kv_refs[0][:, hs]
            vo_ref[h] = kv_refs[1][:, hs]


def _attn_context(proj, lam_p, norm_g, n_latent_groups, n_seq, seq, lam_init):
    g, s, _ = proj.shape
    per_group = s // seq
    blk = (None, seq, DIFF_WIDTH)

    def tok_map(colblock):
        return lambda b: (n_latent_groups + b // per_group, b % per_group, colblock)

    cache_shape = jax.ShapeDtypeStruct((n_seq, 1, DIFF_HEADS, seq, LANES), F32)
    cache_spec = pl.BlockSpec((None, None, DIFF_HEADS, seq, LANES), lambda b: (b, 0, 0, 0, 0))
    out_spec = pl.BlockSpec(blk, lambda b: (b // per_group, b % per_group, 0))
    return pl.pallas_call(
        functools.partial(_diff_attn_kernel, n_pieces=1, n_heads=DIFF_HEADS, lam_init=lam_init, emit_kv=True),
        out_shape=(jax.ShapeDtypeStruct((g - n_latent_groups, s, DIFF_WIDTH), BF16), cache_shape, cache_shape),
        grid=(n_seq,),
        in_specs=[
            pl.BlockSpec((4, DIFF_DH), lambda b: (0, 0)),
            pl.BlockSpec((1, LANES), lambda b: (0, 0)),
            pl.BlockSpec(blk, tok_map(0)),
            pl.BlockSpec(blk, tok_map(1)),
            pl.BlockSpec(blk, tok_map(2)),
        ],
        out_specs=(out_spec, cache_spec, cache_spec),
        compiler_params=_cparams(("arbitrary",)),
        name="attn_context",
    )(lam_p, norm_g, proj, proj, proj)


def _attn_latent(proj, cache_k, cache_v, lam_p, norm_g, n_latent_groups, lam_init, tq=256):
    g, s, _ = proj.shape
    past = cache_k.shape[3]
    cache_spec = pl.BlockSpec((None, None, None, past, LANES), lambda b, h, qi: (b, 0, h, 0, 0))
    return pl.pallas_call(
        functools.partial(_diff_attn_kernel, n_pieces=2, n_heads=1, lam_init=lam_init, emit_kv=False),
        out_shape=jax.ShapeDtypeStruct((n_latent_groups, s, DIFF_WIDTH), BF16),
        grid=(n_latent_groups, DIFF_HEADS, s // tq),
        in_specs=[
            pl.BlockSpec((4, DIFF_DH), lambda b, h, qi: (0, 0)),
            pl.BlockSpec((1, LANES), lambda b, h, qi: (0, 0)),
            pl.BlockSpec((None, tq, LANES), lambda b, h, qi: (b, qi, h)),
            cache_spec,
            cache_spec,
            pl.BlockSpec((None, s, LANES), lambda b, h, qi: (b, 0, DIFF_HEADS + h)),
            pl.BlockSpec((None, s, LANES), lambda b, h, qi: (b, 0, 2 * DIFF_HEADS + h)),
        ],
        out_specs=pl.BlockSpec((None, tq, LANES), lambda b, h, qi: (b, qi, h)),
        compiler_params=_cparams(("arbitrary", "arbitrary", "arbitrary")),
        name="attn_latent",
    )(lam_p, norm_g, proj, cache_k, cache_v, proj, proj)


POOL_ROW_BLOCK = 256
POOL_COL_WINDOW = 512
assert (POOL_COL_WINDOW - POOL_ROW_BLOCK) // 2 >= max(POOL_WINDOWS) // 2


def _pool_kernel(p_ref, w_ref, sc_ref, o_ref, band_ref, *, seq):
    @pl.when((pl.program_id(0) == 0) & (pl.program_id(1) == 0))
    def _():
        t = lax.broadcasted_iota(jnp.int32, (seq, seq), 0)
        s_ = lax.broadcasted_iota(jnp.int32, (seq, seq), 1)
        for gi, w in enumerate(POOL_WINDOWS):
            inside = (s_ >= t - w // 2) & (s_ <= t + w // 2 - 1)
            band_ref[gi] = jnp.where(inside, 1.0, 0.0).astype(BF16)

    tcol = lax.broadcasted_iota(jnp.int32, (seq, 1), 0)
    for gi, w in enumerate(POOL_WINDOWS):
        u = p_ref[:, gi * POOL_GC:(gi + 1) * POOL_GC]
        hi, lo = _split_bf16(u)
        rb = min(seq, POOL_ROW_BLOCK)
        cw = min(seq, POOL_COL_WINDOW)
        blocks = []
        for i in range(seq // rb):
            c0 = min(max(i * rb - (cw - rb) // 2, 0), seq - cw)
            band = band_ref[gi, i * rb:(i + 1) * rb, c0:c0 + cw]
            blocks.append(jnp.dot(band, hi[c0:c0 + cw], preferred_element_type=F32)
                          + jnp.dot(band, lo[c0:c0 + cw], preferred_element_type=F32))
        win = blocks[0] if len(blocks) == 1 else jnp.concatenate(blocks, axis=0)
        cnt = (jnp.minimum(tcol + (w // 2 - 1), seq - 1) - jnp.maximum(tcol - w // 2, 0) + 1).astype(F32)
        pooled = win / cnt - u
        mixed = jnp.dot(pooled.astype(BF16), w_ref[gi].astype(BF16), preferred_element_type=F32)
        o_ref[:, gi * POOL_GC:(gi + 1) * POOL_GC] = (
            mixed * sc_ref[:, gi * POOL_GC:(gi + 1) * POOL_GC]).astype(o_ref.dtype)


def _pool(proj, pool_w, pool_scale, g0, n_groups, seq):
    g, s, _ = proj.shape
    col = 3 * DIFF_WIDTH // POOL_WIDTH
    return pl.pallas_call(
        functools.partial(_pool_kernel, seq=seq),
        out_shape=jax.ShapeDtypeStruct((n_groups, s, POOL_WIDTH), BF16),
        grid=(n_groups, s // seq),
        in_specs=[
            pl.BlockSpec((None, seq, POOL_WIDTH), lambda gi, ti: (g0 + gi, ti, col)),
            pl.BlockSpec((POOL_GROUPS, POOL_GC, POOL_GC), lambda gi, ti: (0, 0, 0)),
            pl.BlockSpec((1, POOL_WIDTH), lambda gi, ti: (0, 0)),
        ],
        out_specs=pl.BlockSpec((None, seq, POOL_WIDTH), lambda gi, ti: (gi, ti, 0)),
        scratch_shapes=[pltpu.VMEM((POOL_GROUPS, seq, seq), BF16)],
        compiler_params=_cparams(("arbitrary", "arbitrary")),
        name="pool_%d" % seq,
    )(proj, pool_w, pool_scale.reshape(1, POOL_WIDTH))


def _outproj_kernel(*refs, n_in, gate_row, n_latent_groups):
    a_refs = refs[:2 * n_in]
    w_refs = refs[2 * n_in:3 * n_in]
    xl_ref, xc_ref, mod_ref, g_ref, b_ref, o_ref = refs[3 * n_in:]
    is_latent = pl.program_id(0) < n_latent_groups
    acc = None
    for i, w_ref in enumerate(w_refs):
        a = jnp.where(is_latent, a_refs[2 * i][...], a_refs[2 * i + 1][...])
        t = jnp.dot(a, w_ref[...], preferred_element_type=F32)
        acc = t if acc is None else acc + t
    x = jnp.where(is_latent, xl_ref[...], xc_ref[...])
    z = ALPHA * x + mod_ref[gate_row:gate_row + 1, :] * acc
    o_ref[...] = _layer_norm_rows(z, g_ref[...], b_ref[...])


def _outproj(acts, weights, x_lat, x_ctx, mod, ln_g, ln_b, gate_row, tm=512):
    gl = acts[0][0].shape[0]
    _, s, d = x_lat.shape
    if x_ctx is None:
        g = x_lat.shape[0]
        x_ctx, x_specs = x_lat, _two_stream_specs(tm, d, gl, gl)
    else:
        g = gl + x_ctx.shape[0]
        x_specs = _two_stream_specs(tm, d, gl)
    n_in = len(acts)
    in_specs = []
    flat_acts = []
    for a_lat, a_ctx in acts:
        in_specs += _two_stream_specs(tm, a_lat.shape[-1], gl)
        flat_acts += [a_lat, a_ctx]
    in_specs += [_resident(w.shape, lambda gi, ti: (0, 0)) for w in weights]
    in_specs += x_specs
    in_specs += [
        pl.BlockSpec((None, 6, d), lambda gi, ti: (gi, 0, 0)),
        pl.BlockSpec((1, d), lambda gi, ti: (0, 0)),
        pl.BlockSpec((1, d), lambda gi, ti: (0, 0)),
    ]
    return pl.pallas_call(
        functools.partial(_outproj_kernel, n_in=n_in, gate_row=gate_row, n_latent_groups=gl),
        out_shape=jax.ShapeDtypeStruct((g, s, d), F32),
        grid=(g, s // tm),
        in_specs=in_specs,
        out_specs=pl.BlockSpec((None, tm, d), lambda gi, ti: (gi, ti, 0)),
        compiler_params=_cparams(("arbitrary", "arbitrary")),
        name="outproj",
    )(*flat_acts, *weights, x_lat, x_ctx, mod, ln_g.reshape(1, d), ln_b.reshape(1, d))


def _swiglu_chunks(h_bf, wgu_ref, wd_ref, d_ff):
    acc = None
    for j in range(d_ff // FF_CHUNK):
        lo = j * FF_CHUNK
        gate = jnp.dot(h_bf, wgu_ref[:, lo:lo + FF_CHUNK], preferred_element_type=F32)
        up = jnp.dot(h_bf, wgu_ref[:, d_ff + lo:d_ff + lo + FF_CHUNK], preferred_element_type=F32)
        act = (gate * _sigmoid(gate) * up).astype(BF16)
        t = jnp.dot(act, wd_ref[lo:lo + FF_CHUNK, :], preferred_element_type=F32)
        acc = t if acc is None else acc + t
    return acc


def _ffn_kernel(x_ref, mod_ref, wgu_ref, wd_ref, g_ref, b_ref, *rest, n_cast):
    cast_in, o_ref, cast_out = rest[:n_cast], rest[n_cast], rest[n_cast + 1:]
    _cast_slabs(cast_in, cast_out)
    x = x_ref[...]
    h = (x * (1.0 + mod_ref[4:5, :]) + mod_ref[3:4, :]).astype(BF16)
    acc = _swiglu_chunks(h, wgu_ref, wd_ref, D_FF)
    z = ALPHA * x + mod_ref[5:6, :] * acc
    o_ref[...] = _layer_norm_rows(z, g_ref[...], b_ref[...])


def _ffn(x, mod, wgu_bf, wd_bf, ln_g, ln_b, cast_weights=(), tm=512):
    g, s, d = x.shape
    per = s // tm
    cast_specs = _cast_rider_specs(cast_weights, g * per, per)
    if cast_specs is None:
        out, _ = _ffn(x, mod, wgu_bf, wd_bf, ln_g, ln_b, (), tm)
        return out, tuple(w.astype(BF16) for w in cast_weights)
    outs = pl.pallas_call(
        functools.partial(_ffn_kernel, n_cast=len(cast_weights)),
        out_shape=(jax.ShapeDtypeStruct((g, s, d), F32),) + tuple(
            jax.ShapeDtypeStruct(w.shape, BF16) for w in cast_weights),
        grid=(g, per),
        in_specs=[
            pl.BlockSpec((None, tm, d), lambda gi, ti: (gi, ti, 0)),
            pl.BlockSpec((None, 6, d), lambda gi, ti: (gi, 0, 0)),
            _resident(wgu_bf.shape, lambda gi, ti: (0, 0)),
            _resident(wd_bf.shape, lambda gi, ti: (0, 0)),
            pl.BlockSpec((1, d), lambda gi, ti: (0, 0)),
            pl.BlockSpec((1, d), lambda gi, ti: (0, 0)),
        ] + cast_specs,
        out_specs=(pl.BlockSpec((None, tm, d), lambda gi, ti: (gi, ti, 0)),) + tuple(cast_specs),
        compiler_params=_cparams(("arbitrary", "arbitrary")),
        name="ffn",
    )(x, mod, wgu_bf, wd_bf, ln_g.reshape(1, d), ln_b.reshape(1, d), *cast_weights)
    return outs[0], outs[1:]


MOE_BM = 256
MOE_TC = 256
META_LANES = LANES


def _router_top2(h, rw_ref, rb_ref):
    h_hi, h_lo = _split_bf16(h)
    w_hi, w_lo = _split_bf16(rw_ref[...])
    logits = (jnp.dot(h_hi, w_hi, preferred_element_type=F32)
              + jnp.dot(h_lo, w_hi, preferred_element_type=F32)
              + jnp.dot(h_hi, w_lo, preferred_element_type=F32)) + rb_ref[...]
    lane = lax.broadcasted_iota(jnp.int32, logits.shape, 1).astype(F32)
    neg = -jnp.inf
    logits = jnp.where(lane < N_EXPERTS, logits, neg)
    m1 = jnp.max(logits, axis=-1, keepdims=True)
    i1 = jnp.min(jnp.where(logits == m1, lane, float(LANES)), axis=-1, keepdims=True)
    rest = jnp.where(lane == i1, neg, logits)
    m2 = jnp.max(rest, axis=-1, keepdims=True)
    i2 = jnp.min(jnp.where(rest == m2, lane, float(LANES)), axis=-1, keepdims=True)
    e2 = jnp.exp(m2 - m1)
    return lane, i1, i2, 1.0 / (1.0 + e2), e2 / (1.0 + e2)


def _route_kernel(x_ref, mod_ref, rw_ref, rb_ref, h_ref, wk_ref, srank_ref, cnt_ref, tri_ref, run_ref):
    tm, d = x_ref.shape

    @pl.when(pl.program_id(0) == 0)
    def _():
        r = lax.broadcasted_iota(jnp.int32, (tm, tm), 0)
        c = lax.broadcasted_iota(jnp.int32, (tm, tm), 1)
        tri_ref[...] = jnp.where(c <= r, 1.0, 0.0).astype(BF16)
        run_ref[...] = jnp.zeros_like(run_ref)

    h = x_ref[...] * (1.0 + mod_ref[4:5, :]) + mod_ref[3:4, :]
    h_ref[:, 0:d] = h.astype(BF16)
    lane, i1, i2, w1, w2 = _router_top2(h, rw_ref, rb_ref)
    first_is_low = i1 < i2
    e_hi = jnp.where(first_is_low, i2, i1)
    wk_ref[...] = jnp.where(lane == 0.0, jnp.where(first_is_low, w1, w2),
                            jnp.where(lane == 1.0, jnp.where(first_is_low, w2, w1), 0.0))
    tok = (pl.program_id(0) * tm + lax.broadcasted_iota(jnp.int32, (tm, META_LANES), 0))
    meta = jnp.where(lane == 0.0, (tok // 128).astype(F32),
                     jnp.where(lane == 1.0, (tok % 128).astype(F32),
                               jnp.where(lane == 2.0, e_hi, jnp.where(lane == 3.0, 1.0, 0.0))))
    h_ref[:, d:d + META_LANES] = meta.astype(BF16)

    member = (lane == i1) | (lane == i2)
    mem = jnp.where(member, 1.0, 0.0)
    rank = jnp.dot(tri_ref[...], mem.astype(BF16), preferred_element_type=F32) + run_ref[...]
    srank_ref[...] = jnp.where(member, rank, -rank).T[0:SUBLANES, :]
    run_ref[...] = rank[tm - 1:tm, :]
    cnt_ref[...] = rank[tm - 1:tm, :]


def _route(x, mod, router_w, router_b, tm=512):
    g, s, d = x.shape
    n = g * s
    assert n <= 128 * 256
    per = s // tm
    rw = jnp.pad(router_w, ((0, 0), (0, LANES - N_EXPERTS)))
    rb = jnp.pad(router_b, (0, LANES - N_EXPERTS)).reshape(1, LANES)
    return pl.pallas_call(
        _route_kernel,
        out_shape=(jax.ShapeDtypeStruct((n, d + META_LANES), BF16), jax.ShapeDtypeStruct((n, LANES), F32),
                   jax.ShapeDtypeStruct((SUBLANES, n), F32), jax.ShapeDtypeStruct((1, LANES), F32)),
        grid=(n // tm,),
        in_specs=[
            pl.BlockSpec((None, tm, d), lambda i: (i // per, i % per, 0)),
            pl.BlockSpec((None, 6, d), lambda i: (i // per, 0, 0)),
            pl.BlockSpec((d, LANES), lambda i: (0, 0)),
            pl.BlockSpec((1, LANES), lambda i: (0, 0)),
        ],
        out_specs=(pl.BlockSpec((tm, d + META_LANES), lambda i: (i, 0)), pl.BlockSpec((tm, LANES), lambda i: (i, 0)),
                   pl.BlockSpec((SUBLANES, tm), lambda i: (0, i)), pl.BlockSpec((1, LANES), lambda i: (0, 0))),
        scratch_shapes=[pltpu.VMEM((tm, tm), BF16), pltpu.VMEM((1, LANES), F32)],
        compiler_params=_cparams(("arbitrary",)),
        name="moe_route",
    )(x, mod, rw, rb)


def _moe_plan(srank, counts, n_blocks):
    e_n = N_EXPERTS
    i32 = jnp.int32
    cnt = counts[0, :e_n].astype(i32)
    nb = (cnt + MOE_BM - 1) // MOE_BM
    nb_incl = jnp.cumsum(nb)
    gstart = nb_incl - nb
    n_used = nb_incl[-1]
    sr = srank[:e_n]
    rank = jnp.abs(sr).astype(i32)
    pos_row = jnp.where(sr > 0, rank - 1 + MOE_BM * gstart[:, None], -1)
    chunk_end = rank[:, MOE_TC - 1::MOE_TC]

    r = jnp.arange(n_blocks, dtype=i32)
    used = r < n_used
    rc = jnp.minimum(r, n_used - 1)
    e_r = jnp.minimum(jnp.sum(nb_incl[None, :] <= rc[:, None], axis=1, dtype=i32), e_n - 1)
    b = rc - gstart[e_r]
    lo = b * MOE_BM + 1
    hi = jnp.minimum((b + 1) * MOE_BM, cnt[e_r])
    ends_r = chunk_end[e_r]
    jlo = jnp.where(used, jnp.sum(ends_r < lo[:, None], axis=1, dtype=i32), 0)
    jhi = jnp.where(used, jnp.sum(ends_r < hi[:, None], axis=1, dtype=i32), -1)
    return pos_row, e_r, jlo, jhi, n_used.reshape(1)


def _experts_kernel(be_ref, jlo_ref, jhi_ref, nused_ref, h_ref, pos_ref, wgu_ref, wd_ref, y_ref, meta_ref, xg_ref):
    r = pl.program_id(0)
    d = y_ref.shape[1]

    @pl.when(r < nused_ref[0])
    def _():
        e = be_ref[r]
        slot = r * MOE_BM + lax.broadcasted_iota(jnp.int32, (MOE_BM, MOE_TC), 0)
        xg_ref[...] = jnp.zeros_like(xg_ref)

        def chunk(j, carry):
            onehot = jnp.where(pos_ref[e, pl.ds(j, 1), :] == slot, 1.0, 0.0).astype(BF16)
            rows = h_ref[pl.ds(pl.multiple_of(j * MOE_TC, MOE_TC), MOE_TC), :]
            xg_ref[...] += jnp.dot(onehot, rows, preferred_element_type=F32)
            return carry

        lax.fori_loop(jlo_ref[r], jhi_ref[r] + 1, chunk, 0)
        meta_ref[...] = xg_ref[:, d:d + META_LANES].T[0:SUBLANES, :]
        y_ref[...] = _swiglu_chunks(xg_ref[:, 0:d].astype(BF16), wgu_ref, wd_ref, D_FF_EXPERT)

    @pl.when(r >= nused_ref[0])
    def _():
        y_ref[...] = jnp.zeros_like(y_ref)
        meta_ref[...] = jnp.zeros_like(meta_ref)


def _experts(h, pos_row, plan, wgu_bf, wd_bf, n_blocks):
    n, dx = h.shape
    d = dx - META_LANES
    e_r, jlo, jhi, n_used = plan
    grid_spec = pltpu.PrefetchScalarGridSpec(
        num_scalar_prefetch=4,
        grid=(n_blocks,),
        in_specs=[
            _resident((n, dx), lambda r, be, lo, hi, nu: (0, 0)),
            _resident(pos_row.shape, lambda r, be, lo, hi, nu: (0, 0, 0)),
            pl.BlockSpec((None, d, 2 * D_FF_EXPERT), lambda r, be, lo, hi, nu: (be[r], 0, 0),
                         pipeline_mode=pl.Buffered(1)),
            pl.BlockSpec((None, D_FF_EXPERT, d), lambda r, be, lo, hi, nu: (be[r], 0, 0),
                         pipeline_mode=pl.Buffered(1)),
        ],
        out_specs=(pl.BlockSpec((MOE_BM, d), lambda r, be, lo, hi, nu: (r, 0)),
                   pl.BlockSpec((None, SUBLANES, MOE_BM), lambda r, be, lo, hi, nu: (r, 0, 0))),
        scratch_shapes=[pltpu.VMEM((MOE_BM, dx), F32)],
    )
    return pl.pallas_call(
        _experts_kernel,
        out_shape=(jax.ShapeDtypeStruct((n_blocks * MOE_BM, d), F32),
                   jax.ShapeDtypeStruct((n_blocks, SUBLANES, MOE_BM), F32)),
        grid_spec=grid_spec,
        compiler_params=_cparams(("arbitrary",)),
        name="moe_experts",
    )(e_r, jlo, jhi, n_used, h, pos_row, wgu_bf, wd_bf)


def _scatter_kernel(dest_ref, y_ref, o_ref, ybuf_ref, sem):
    r = pl.program_id(0)
    n_steps = pl.num_programs(0)
    buf = r % 2

    def wait_block(b):
        pltpu.make_async_copy(ybuf_ref.at[b], o_ref.at[pl.ds(0, MOE_BM), :], sem.at[b]).wait()

    @pl.when(r >= 2)
    def _():
        wait_block(buf)

    ybuf_ref[buf] = y_ref[...]

    for i in range(MOE_BM):
        pltpu.make_async_copy(ybuf_ref.at[buf, pl.ds(i, 1), :], o_ref.at[pl.ds(dest_ref[0, i], 1), :],
                              sem.at[buf]).start()

    @pl.when(r == n_steps - 1)
    def _():
        wait_block(1 - buf)
        wait_block(buf)


def _scatter_rows(y, dest, n_rows_out):
    n_slots, d = y.shape
    n_blocks = n_slots // MOE_BM
    assert n_blocks >= 2
    return pl.pallas_call(
        _scatter_kernel,
        out_shape=jax.ShapeDtypeStruct((n_rows_out, d), F32),
        grid=(n_blocks,),
        in_specs=[
            pl.BlockSpec((None, 1, MOE_BM), lambda r: (r, 0, 0), memory_space=pltpu.SMEM),
            pl.BlockSpec((MOE_BM, d), lambda r: (r, 0)),
        ],
        out_specs=pl.BlockSpec(memory_space=pl.ANY),
        scratch_shapes=[pltpu.VMEM((2, MOE_BM, d), F32), pltpu.SemaphoreType.DMA((2,))],
        compiler_params=_cparams(("arbitrary",)),
        name="moe_scatter",
    )(dest.reshape(n_blocks, 1, MOE_BM), y)


def _finish_kernel(a_ref, b_ref, wk_ref, x_ref, mod_ref, g_ref, bb_ref, ol_ref, oc_ref, *, n_latent_tiles):
    lane = lax.broadcasted_iota(jnp.int32, wk_ref.shape, 1)
    wk = wk_ref[...]
    w0 = jnp.sum(jnp.where(lane == 0, wk, 0.0), axis=-1, keepdims=True)
    w1 = jnp.sum(jnp.where(lane == 1, wk, 0.0), axis=-1, keepdims=True)
    acc = w0 * a_ref[...] + w1 * b_ref[...]
    z = ALPHA * x_ref[...] + mod_ref[5:6, :] * acc
    out = _layer_norm_rows(z, g_ref[...], bb_ref[...])
    is_latent = pl.program_id(0) < n_latent_tiles

    @pl.when(is_latent)
    def _():
        ol_ref[...] = out

    @pl.when(jnp.logical_not(is_latent))
    def _():
        oc_ref[...] = out


def _finish(rows, wk, x, mod, ln_g, ln_b, n_latent_groups, tm=512):
    g, s, d = x.shape
    n = g * s
    per = s // tm
    nt = n // tm
    nlt = n_latent_groups * per

    def lat_map(i):
        j = jnp.minimum(i, nlt - 1)
        return (j // per, j % per, 0)

    def ctx_map(i):
        j = jnp.maximum(i - nlt, 0)
        return (j // per, j % per, 0)

    return pl.pallas_call(
        functools.partial(_finish_kernel, n_latent_tiles=nlt),
        out_shape=(jax.ShapeDtypeStruct((n_latent_groups, s, d), F32),
                   jax.ShapeDtypeStruct((g - n_latent_groups, s, d), F32)),
        grid=(nt,),
        in_specs=[
            pl.BlockSpec((tm, d), lambda i: (i, 0)),
            pl.BlockSpec((tm, d), lambda i: (nt + i, 0)),
            pl.BlockSpec((tm, LANES), lambda i: (i, 0)),
            pl.BlockSpec((None, tm, d), lambda i: (i // per, i % per, 0)),
            pl.BlockSpec((None, 6, d), lambda i: (i // per, 0, 0)),
            pl.BlockSpec((1, d), lambda i: (0, 0)),
            pl.BlockSpec((1, d), lambda i: (0, 0)),
        ],
        out_specs=(pl.BlockSpec((None, tm, d), lat_map), pl.BlockSpec((None, tm, d), ctx_map)),
        compiler_params=_cparams(("arbitrary",)),
        name="moe_finish",
    )(rows, rows, wk, x, mod, ln_g.reshape(1, d), ln_b.reshape(1, d))


def _moe(x, mod, router_w, router_b, wgu_bf, wd_bf, ln_g, ln_b, n_latent_groups):
    g, s, d = x.shape
    n = g * s
    n_blocks = 2 * n // MOE_BM + N_EXPERTS
    n_slots = n_blocks * MOE_BM
    h, wk, srank, counts = _route(x, mod, router_w, router_b)
    pos_row, e_r, jlo, jhi, n_used = _moe_plan(srank, counts, n_blocks)
    pos_row = pos_row.reshape(N_EXPERTS, n // MOE_TC, MOE_TC)
    y, meta = _experts(h, pos_row, (e_r, jlo, jhi, n_used), wgu_bf, wd_bf, n_blocks)
    tok = (meta[:, 0] * 128.0 + meta[:, 1]).astype(jnp.int32)
    choice = (meta[:, 2] == e_r[:, None].astype(F32)).astype(jnp.int32)
    unused = meta[:, 3] < 0.5
    spare = (2 * n + jnp.cumsum(unused.reshape(-1).astype(jnp.int32)) - 1).reshape(n_blocks, MOE_BM)
    dest = jnp.where(unused, spare, choice * n + tok)
    rows = _scatter_rows(y, dest, n_slots)
    return _finish(rows, wk, x, mod, ln_g, ln_b, n_latent_groups)


def _inproj_c_kernel(x_ref, mod_ref, w_ref, wg_ref, bg_ref, o_ref, og_ref):
    h = (x_ref[...] * (1.0 + mod_ref[1:2, :]) + mod_ref[0:1, :]).astype(BF16)
    o_ref[...] = jnp.dot(h, w_ref[:, 0:o_ref.shape[1]], preferred_element_type=F32)
    og_ref[...] = jnp.dot(h, wg_ref[...], preferred_element_type=F32) + bg_ref[...]


def _inproj_c(x, mod, w_bf, n, wg_bf, bg, tm=512):
    g, s, d = x.shape
    ng = wg_bf.shape[1]
    return pl.pallas_call(
        _inproj_c_kernel,
        out_shape=(jax.ShapeDtypeStruct((g, s, n), F32), jax.ShapeDtypeStruct((g, s, ng), F32)),
        grid=(g, s // tm),
        in_specs=[
            pl.BlockSpec((None, tm, d), lambda gi, ti: (gi, ti, 0)),
            pl.BlockSpec((None, 6, d), lambda gi, ti: (gi, 0, 0)),
            _resident(w_bf.shape, lambda gi, ti: (0, 0)),
            _resident((d, ng), lambda gi, ti: (0, 0)),
            pl.BlockSpec((1, ng), lambda gi, ti: (0, 0)),
        ],
        out_specs=(pl.BlockSpec((None, tm, n), lambda gi, ti: (gi, ti, 0)),
                   pl.BlockSpec((None, tm, ng), lambda gi, ti: (gi, ti, 0))),
        compiler_params=_cparams(("arbitrary", "arbitrary")),
        name="inproj_c",
    )(x, mod, w_bf, wg_bf, bg)


def _log_sigmoid(x):
    return jnp.minimum(x, 0.0) - jnp.log(1.0 + jnp.exp(-jnp.abs(x)))


MLSTM_L = 128


def _split3_bf16(x):
    hi = x.astype(BF16)
    r1 = x - hi.astype(F32)
    mid = r1.astype(BF16)
    lo = (r1 - mid.astype(F32)).astype(BF16)
    return hi, mid, lo


def _mlstm_kernel(*refs, seq, hg, has_init, emit_state):
    q_ref, k_ref, v_ref, o_ref, gi_ref, gf_ref, hgain_ref = refs[:7]
    pos = 7
    if has_init:
        c0_ref, n0_ref, m0_ref = refs[pos:pos + 3]
        pos += 3
    out_ref = refs[pos]
    pos += 1
    if emit_state:
        co_ref, no_ref, mo_ref = refs[pos:pos + 3]
        pos += 3
    cext_ref, hf_ref, hb_ref, b_ref, g_ref, gmax_ref, mt_ref, wi_ref, en_ref, ws_ref, gt_ref, wc_ref = refs[pos:]

    L = MLSTM_L
    dh = MLSTM_DH
    nh = MLSTM_HEADS
    nc = seq // L
    head0 = pl.program_id(1) * hg
    neg = -jnp.inf

    lane = lax.broadcasted_iota(jnp.int32, (L, LANES), 1)
    lane1 = lax.broadcasted_iota(jnp.int32, (1, LANES), 1)
    row = lax.broadcasted_iota(jnp.int32, (L, L), 0)
    col = lax.broadcasted_iota(jnp.int32, (L, L), 1)
    lower = col <= row
    upper = col >= row
    tri_l = jnp.where(lower, 1.0, 0.0).astype(BF16)
    tri_u = jnp.where(upper, 1.0, 0.0).astype(BF16)
    fwd_lane = lane < nh
    fwd_lane1 = lane1 < nh
    trow = lax.broadcasted_iota(jnp.int32, (L, LANES), 0)

    btot, glast = [], []
    for c in range(nc):
        rows = slice(c * L, (c + 1) * L)
        f = _log_sigmoid(gf_ref[rows, :])
        parts = _split3_bf16(f)
        pre = sum(jnp.dot(tri_l, p, preferred_element_type=F32) for p in parts)
        suf = sum(jnp.dot(tri_u, p, preferred_element_type=F32) for p in parts)
        b = jnp.where(fwd_lane, pre, suf)
        g = gi_ref[rows, :] - b
        gp, gs = g, g
        k = 1
        while k < L:
            gp = jnp.where(trow >= k, jnp.maximum(gp, pltpu.roll(gp, k, 0)), gp)
            gs = jnp.where(trow < L - k, jnp.maximum(gs, pltpu.roll(gs, L - k, 0)), gs)
            k *= 2
        gmax = jnp.where(fwd_lane, gp, gs)
        b_ref[rows, :] = b
        g_ref[rows, :] = g
        gmax_ref[rows, :] = gmax
        btot.append(jnp.where(fwd_lane1, b[L - 1:L, :], b[0:1, :]))
        glast.append(jnp.where(fwd_lane1, gmax[L - 1:L, :], gmax[0:1, :]))

    m_init = m0_ref[...] if has_init else jnp.zeros((1, LANES), F32)
    mf, mb = m_init, m_init
    ms_f, mn_f, ms_b, mn_b = [None] * nc, [None] * nc, [None] * nc, [None] * nc
    for c in range(nc):
        ms_f[c] = mf
        mf = btot[c] + jnp.maximum(mf, glast[c])
        mn_f[c] = mf
        cb = nc - 1 - c
        ms_b[cb] = mb
        mb = btot[cb] + jnp.maximum(mb, glast[cb])
        mn_b[cb] = mb
    m_final = jnp.where(fwd_lane1, mf, mb)

    for c in range(nc):
        rows = slice(c * L, (c + 1) * L)
        m_start = jnp.where(fwd_lane1, ms_f[c], ms_b[c])
        m_next = jnp.where(fwd_lane1, mn_f[c], mn_b[c])
        g = g_ref[rows, :]
        mt = jnp.maximum(m_start, gmax_ref[rows, :])
        mt_ref[rows, :] = mt
        wi_ref[rows, :] = jnp.exp(m_start - mt)
        en_ref[rows, :] = jnp.exp(-(b_ref[rows, :] + mt))
        ws_ref[rows, :] = jnp.exp(btot[c] + g - m_next)
        gt_ref[c] = g.T
        wc_ref[c:c + 1, :] = jnp.exp(btot[c] + m_start - m_next)

    for d in range(2):
        for hh in range(hg):
            idx = d * hg + hh
            if has_init:
                cext_ref[idx, :, 0:dh] = c0_ref[d, hh]
                n0_tile = jnp.where(lax.broadcasted_iota(jnp.int32, (dh, dh), 0) == 0, n0_ref[d, hh], 0.0)
                cext_ref[idx, :, dh:2 * dh] = n0_tile.T
            else:
                cext_ref[idx] = jnp.zeros((dh, 2 * dh), F32)

    ones_col = jnp.where(lane == 0, 1.0, 0.0).astype(BF16)
    nt = (((1,), (1,)), ((), ()))
    tn = (((0,), (0,)), ((), ()))

    def column(x, j):
        return jnp.sum(jnp.where(lane == j, x, 0.0), axis=-1, keepdims=True)

    def one_direction(d, hh, c, s_qk, q_bf, k_s, v_ext, v_bf):
        idx = d * hg + hh
        j = d * nh + head0 + hh
        rows = pl.ds(pl.multiple_of(c * L, L), L)
        mt = column(mt_ref[rows, :], j)
        wi = column(wi_ref[rows, :], j)
        en = column(en_ref[rows, :], j)
        ws = column(ws_ref[rows, :], j)
        g_r = gt_ref[c, pl.ds(j, 1), :]
        w_c = jnp.sum(jnp.where(lane1 == j, wc_ref[pl.ds(c, 1), :], 0.0), axis=-1, keepdims=True)
        causal = lower if d == 0 else upper
        p = s_qk * jnp.exp(jnp.where(causal, g_r - mt, neg))
        qc = jnp.dot(q_bf, cext_ref[idx].astype(BF16), preferred_element_type=F32)
        num = wi * qc[:, 0:dh] + jnp.dot(p.astype(BF16), v_bf, preferred_element_type=F32)
        den = wi * qc[:, dh:dh + 1] + jnp.sum(p, axis=-1, keepdims=True)
        h = num / jnp.maximum(jnp.abs(den), en)
        upd = lax.dot_general((ws * k_s).astype(BF16), v_ext, tn, preferred_element_type=F32)
        cext_ref[idx] = w_c * cext_ref[idx] + upd
        return h

    def load_chunk(hh, c):
        sl = (pl.ds(pl.multiple_of(c * L, L), L), slice(hh * dh, (hh + 1) * dh))
        q_bf = q_ref[sl].astype(BF16)
        k_s = k_ref[sl] * (dh ** -0.5)
        v_bf = v_ref[sl].astype(BF16)
        v_ext = jnp.concatenate([v_bf, ones_col], axis=-1)
        s_qk = lax.dot_general(q_bf, k_s.astype(BF16), nt, preferred_element_type=F32)
        return s_qk, q_bf, k_s, v_ext, v_bf

    def step(c, carry):
        cb = nc - 1 - c
        for hh in range(hg):
            h = one_direction(0, hh, c, *load_chunk(hh, c))
            hf_ref[pl.ds(pl.multiple_of(c * L, L), L), hh * dh:(hh + 1) * dh] = h
        for hh in range(hg):
            h = one_direction(1, hh, cb, *load_chunk(hh, cb))
            hb_ref[pl.ds(pl.multiple_of(cb * L, L), L), hh * dh:(hh + 1) * dh] = h
        return carry

    lax.fori_loop(0, nc, step, 0)

    for hh in range(hg):
        cs = slice(hh * dh, (hh + 1) * dh)
        hs = hf_ref[:, cs] + hb_ref[:, cs]
        mu = jnp.mean(hs, axis=-1, keepdims=True)
        hc = hs - mu
        var = jnp.mean(hc * hc, axis=-1, keepdims=True)
        hn = hc * lax.rsqrt(var + LN_EPS) * hgain_ref[:, cs]
        out_ref[:, cs] = (_sigmoid(o_ref[:, cs]) * hn).astype(out_ref.dtype)

    if emit_state:
        for d in range(2):
            for hh in range(hg):
                idx = d * hg + hh
                co_ref[d, hh] = cext_ref[idx, :, 0:dh]
                no_ref[d, hh] = cext_ref[idx, :, dh:2 * dh].T[0:1, :]
        mo_ref[...] = m_final


def _mlstm(proj, gates, head_g, g0, n_seq, seq, hg, init=None, emit_state=False):
    g, s, _ = proj.shape
    per_group = s // seq
    n_hg = MLSTM_HEADS // hg
    w = hg * MLSTM_DH
    nc = seq // MLSTM_L
    n_blocks = D_MODEL // w

    def tok_map(colblock):
        return lambda b, hi: (g0 + b // per_group, b % per_group, colblock * n_blocks + hi)

    def gate_map(half):
        return lambda b, hi: (g0 + b // per_group, b % per_group, half)

    args = [proj, proj, proj, proj, gates, gates, head_g.reshape(1, D_MODEL)]
    in_specs = [
        pl.BlockSpec((None, seq, w), tok_map(0)),
        pl.BlockSpec((None, seq, w), tok_map(1)),
        pl.BlockSpec((None, seq, w), tok_map(2)),
        pl.BlockSpec((None, seq, w), tok_map(3)),
        pl.BlockSpec((None, seq, LANES), gate_map(0)),
        pl.BlockSpec((None, seq, LANES), gate_map(1)),
        pl.BlockSpec((1, w), lambda b, hi: (0, hi)),
    ]
    if init is not None:
        c0, n0, m0 = init
        m0_lanes = jnp.pad(m0.reshape(n_seq, 1, 2 * MLSTM_HEADS), ((0, 0), (0, 0), (0, LANES - 2 * MLSTM_HEADS)))
        args += [c0, n0.reshape(n0.shape[:-1] + (1, MLSTM_DH)), m0_lanes]
        in_specs += [
            pl.BlockSpec((None, None, 2, hg, MLSTM_DH, MLSTM_DH), lambda b, hi: (b, 0, 0, hi, 0, 0)),
            pl.BlockSpec((None, 2, hg, 1, MLSTM_DH), lambda b, hi: (b, 0, hi, 0, 0)),
            pl.BlockSpec((None, 1, LANES), lambda b, hi: (b, 0, 0)),
        ]

    out_shape = [jax.ShapeDtypeStruct((n_seq // per_group, s, D_MODEL), BF16)]
    out_specs = [pl.BlockSpec((None, seq, w), lambda b, hi: (b // per_group, b % per_group, hi))]
    if emit_state:
        out_shape += [
            jax.ShapeDtypeStruct((n_seq, 2, MLSTM_HEADS, MLSTM_DH, MLSTM_DH), F32),
            jax.ShapeDtypeStruct((n_seq, 2, MLSTM_HEADS, 1, MLSTM_DH), F32),
            jax.ShapeDtypeStruct((n_seq, n_hg, 1, LANES), F32),
        ]
        out_specs += [
            pl.BlockSpec((None, 2, hg, MLSTM_DH, MLSTM_DH), lambda b, hi: (b, 0, hi, 0, 0)),
            pl.BlockSpec((None, 2, hg, 1, MLSTM_DH), lambda b, hi: (b, 0, hi, 0, 0)),
            pl.BlockSpec((None, None, 1, LANES), lambda b, hi: (b, hi, 0, 0)),
        ]

    tok_scratch = pltpu.VMEM((seq, LANES), F32)
    return pl.pallas_call(
        functools.partial(_mlstm_kernel, seq=seq, hg=hg, has_init=init is not None, emit_state=emit_state),
        out_shape=tuple(out_shape),
        grid=(n_seq, n_hg),
        in_specs=in_specs,
        out_specs=tuple(out_specs),
        scratch_shapes=[
            pltpu.VMEM((2 * hg, MLSTM_DH, 2 * MLSTM_DH), F32),
            pltpu.VMEM((seq, w), F32),
            pltpu.VMEM((seq, w), F32),
        ] + [tok_scratch] * 7 + [
            pltpu.VMEM((nc, LANES, MLSTM_L), F32),
            pltpu.VMEM((max(nc, 8), LANES), F32),
        ],
        compiler_params=_cparams(("arbitrary", "arbitrary")),
        name="mlstm_%d" % seq,
    )(*args)


def kernel(x_prompt, x_sample, c, cache_k, cache_v, state_C, state_n, state_m, c_ctx, ada_w, ada_b, ln_g, ln_b, w_in_a, diff_lambda, diff_norm_g, pool_w, pool_scale, w_out_a, ffn_w_gu, ffn_w_down, w_in_c, b_gates_c, mlstm_norm_g, w_out_c, router_w, router_b, moe_w_gu, moe_w_down):
    n_ctx, seq_ctx, d = x_prompt.shape
    n_lat, seq_lat, _ = x_sample.shape
    assert d == D_MODEL and (n_ctx * seq_ctx) % seq_lat == 0 and seq_lat % seq_ctx == 0
    gl = n_lat
    gc = n_ctx * seq_ctx // seq_lat
    s = seq_lat

    x_ctx = x_prompt.reshape(gc, s, d)
    cvec = jnp.concatenate([c, jnp.broadcast_to(c_ctx[None, :], (gc, d))], axis=0)
    mod_all = _modulation(cvec, ada_w, ada_b).reshape(DEPTH, gl + gc, 6, d)

    mod = mod_all[0]
    lam_init = 0.8 - 0.6 * math.exp(-0.3 * 0)
    cos_t, sin_t = _rope_tables(s)
    proj, (ffn_wgu_bf, ffn_wd_bf) = _inproj_a(x_sample, x_ctx, mod, w_in_a[0].astype(BF16), cos_t, sin_t,
                                              (ffn_w_gu, ffn_w_down))
    norm_g = diff_norm_g[0].reshape(1, LANES)
    attn_c, new_k, new_v = _attn_context(proj, diff_lambda[0], norm_g, gl, n_ctx, seq_ctx, lam_init)
    attn_l = _attn_latent(proj, cache_k, cache_v, diff_lambda[0], norm_g, gl, lam_init)
    pool_c = _pool(proj, pool_w[0], pool_scale[0], gl, gc, seq_ctx)
    pool_l = _pool(proj, pool_w[0], pool_scale[0], 0, gl, seq_lat)
    w_out = w_out_a[0].astype(BF16)
    x = _outproj([(attn_l, attn_c), (pool_l, pool_c)], [w_out[:DIFF_WIDTH], w_out[DIFF_WIDTH:]],
                 x_sample, x_ctx, mod, ln_g[0, 0], ln_b[0, 0], 2)
    x, (moe_wgu_bf, moe_wd_bf, w_in_c_bf) = _ffn(x, mod, ffn_wgu_bf[0], ffn_wd_bf[0], ln_g[0, 1], ln_b[0, 1],
                                                  (moe_w_gu[0], moe_w_down[0], w_in_c))

    mod = mod_all[1]
    n_main = 4 * D_MODEL
    nh = MLSTM_HEADS
    wg4 = w_in_c[0][:, n_main:].reshape(d, N_GATES, nh)
    bg4 = b_gates_c[0].reshape(1, N_GATES, nh)
    lane_pad = ((0, 0), (0, LANES - 2 * nh))

    def gate_lanes(a):
        return jnp.concatenate([jnp.pad(jnp.concatenate([a[:, 0], a[:, 2]], axis=-1), lane_pad),
                                jnp.pad(jnp.concatenate([a[:, 1], a[:, 3]], axis=-1), lane_pad)], axis=-1)

    proj, gates = _inproj_c(x, mod, w_in_c_bf[0], n_main, gate_lanes(wg4).astype(BF16), gate_lanes(bg4))
    mix_c, new_c, new_n, new_m = _mlstm(proj, gates, mlstm_norm_g[0], gl, n_ctx, seq_ctx, MLSTM_HEADS,
                                        emit_state=True)
    (mix_l,) = _mlstm(proj, gates, mlstm_norm_g[0], 0, n_lat, seq_lat, 4,
                      init=(state_C, state_n[:, 0], state_m[:, 0]))
    x = _outproj([(mix_l, mix_c)], [w_out_c[0].astype(BF16)], x, None, mod, ln_g[1, 0], ln_b[1, 0], 2)
    y_sample, y_ctx = _moe(x, mod, router_w[0], router_b[0], moe_wgu_bf, moe_wd_bf,
                           ln_g[1, 1], ln_b[1, 1], gl)
    y_prompt = y_ctx.reshape(n_ctx, seq_ctx, d)
    new_m = new_m[:, 0, 0, :2 * MLSTM_HEADS].reshape(n_ctx, 2, MLSTM_HEADS)
    return (y_prompt, y_sample, new_k, new_v, new_c[:, None], new_n[..., 0, :][:, None], new_m[:, None])
```

```python
import functools
import math

import jax
import jax.numpy as jnp
from jax import lax
from jax.experimental import pallas as pl
from jax.experimental.pallas import tpu as pltpu

F32 = jnp.float32
BF16 = jnp.bfloat16

D_MODEL = 1024
GRID_W = 64
ROPE_BASE = 10000.0
DIFF_HEADS = 4
DIFF_DH = 64
DIFF_WIDTH = DIFF_HEADS * 2 * DIFF_DH
POOL_GROUPS = 4
POOL_GC = 128
POOL_WIDTH = POOL_GROUPS * POOL_GC
POOL_WINDOWS = (2, 4, 8, 16)
MLSTM_HEADS = 8
MLSTM_DH = 128
N_GATES = 4
D_FF = 2816
N_EXPERTS = 8
D_FF_EXPERT = 1792
LN_EPS = 1e-5
DEPTH = 2
ALPHA = (2.0 * DEPTH) ** 0.25

LANES = 128
SUBLANES = 8
FF_CHUNK = 256
VMEM_LIMIT = 56 * 1024 * 1024


def _cparams(sem, vmem=VMEM_LIMIT):
    return pltpu.CompilerParams(dimension_semantics=sem, vmem_limit_bytes=vmem)


def _resident(shape, index_map):
    return pl.BlockSpec(shape, index_map, pipeline_mode=pl.Buffered(1))


def _layer_norm_rows(z, g, b):
    mu = jnp.mean(z, axis=-1, keepdims=True)
    zc = z - mu
    var = jnp.mean(zc * zc, axis=-1, keepdims=True)
    return zc * lax.rsqrt(var + LN_EPS) * g + b


def _sigmoid(x):
    return 1.0 / (1.0 + jnp.exp(-x))


def _split_bf16(x):
    hi = x.astype(BF16)
    lo = (x - hi.astype(F32)).astype(BF16)
    return hi, lo


def _cast_rider_specs(weights, n_steps, per):
    specs = []
    for w in weights:
        e, rows, cols = w.shape
        per_e = next((k for k in range(n_steps // e, 0, -1)
                      if n_steps % (e * k) == 0 and rows % (2 * SUBLANES * k) == 0), None)
        if per_e is None:
            return None
        hold = n_steps // (e * per_e)

        def slab_map(gi, ti, per_e=per_e, hold=hold):
            slab = (gi * per + ti) // hold
            return (slab // per_e, slab % per_e, 0)

        specs.append(pl.BlockSpec((None, rows // per_e, cols), slab_map))
    return specs


def _cast_slabs(cast_in, cast_out):
    for src, dst in zip(cast_in, cast_out):
        dst[...] = src[...].astype(BF16)


def _mod_kernel(c_ref, w_ref, b_ref, o_ref):
    c = c_ref[...]
    h = (c * _sigmoid(c)).astype(BF16)
    o_ref[...] = jnp.dot(h, w_ref[...].astype(BF16), preferred_element_type=F32) + b_ref[...]


def _modulation(cvec, ada_w, ada_b):
    depth, d, n = ada_w.shape
    g = cvec.shape[0]
    tn = 1536
    return pl.pallas_call(
        _mod_kernel,
        out_shape=jax.ShapeDtypeStruct((depth, g, n), F32),
        grid=(depth, n // tn),
        in_specs=[
            pl.BlockSpec((g, d), lambda l, j: (0, 0)),
            pl.BlockSpec((None, d, tn), lambda l, j: (l, 0, j)),
            pl.BlockSpec((None, 1, tn), lambda l, j: (l, 0, j)),
        ],
        out_specs=pl.BlockSpec((None, g, tn), lambda l, j: (l, 0, j)),
        compiler_params=_cparams(("arbitrary", "arbitrary")),
        name="modulation",
    )(cvec, ada_w, ada_b.reshape(depth, 1, n))


def _rot_half16(x):
    lane = lax.broadcasted_iota(jnp.int32, x.shape, 1)
    return jnp.where((lane % 32) < 16, pltpu.roll(x, LANES - 16, 1), pltpu.roll(x, 16, 1))


def _two_stream_specs(tm, d, gl, ctx_first_group=0):
    return [pl.BlockSpec((None, tm, d), lambda gi, ti: (jnp.minimum(gi, gl - 1), jnp.where(gi < gl, ti, 0), 0)),
            pl.BlockSpec((None, tm, d), lambda gi, ti: (ctx_first_group + jnp.maximum(gi - gl, 0),
                                                        jnp.where(gi < gl, 0, ti), 0))]


def _inproj_a_kernel(xl_ref, xc_ref, mod_ref, w_ref, cos_ref, sin_ref, *rest, n_latent_groups, n_cast):
    cast_in, o_ref, cast_out = rest[:n_cast], rest[n_cast], rest[n_cast + 1:]
    _cast_slabs(cast_in, cast_out)
    x = jnp.where(pl.program_id(0) < n_latent_groups, xl_ref[...], xc_ref[...])
    h = x * (1.0 + mod_ref[1:2, :]) + mod_ref[0:1, :]
    p = jnp.dot(h.astype(BF16), w_ref[...], preferred_element_type=F32)
    cos = cos_ref[...]
    sin = sin_ref[...]
    n_rope = 2 * DIFF_WIDTH // LANES
    for j in range(n_rope):
        blk = p[:, j * LANES:(j + 1) * LANES]
        o_ref[:, j * LANES:(j + 1) * LANES] = blk * cos + _rot_half16(blk) * sin
    o_ref[:, n_rope * LANES:] = p[:, n_rope * LANES:]


def _inproj_a(x_lat, x_ctx, mod, w_bf, cos_t, sin_t, cast_weights=(), tm=512):
    n_latent_groups, s, d = x_lat.shape
    g = n_latent_groups + x_ctx.shape[0]
    n = w_bf.shape[1]
    per = s // tm
    cast_specs = _cast_rider_specs(cast_weights, g * per, per)
    if cast_specs is None:
        proj, _ = _inproj_a(x_lat, x_ctx, mod, w_bf, cos_t, sin_t, (), tm)
        return proj, tuple(w.astype(BF16) for w in cast_weights)

    def table_map(gi, ti):
        return (jnp.where(gi >= n_latent_groups, 1, 0), ti, 0)

    outs = pl.pallas_call(
        functools.partial(_inproj_a_kernel, n_latent_groups=n_latent_groups, n_cast=len(cast_weights)),
        out_shape=(jax.ShapeDtypeStruct((g, s, n), F32),) + tuple(
            jax.ShapeDtypeStruct(w.shape, BF16) for w in cast_weights),
        grid=(g, per),
        in_specs=_two_stream_specs(tm, d, n_latent_groups) + [
            pl.BlockSpec((None, 6, d), lambda gi, ti: (gi, 0, 0)),
            _resident((d, n), lambda gi, ti: (0, 0)),
            pl.BlockSpec((None, tm, LANES), table_map),
            pl.BlockSpec((None, tm, LANES), table_map),
        ] + cast_specs,
        out_specs=(pl.BlockSpec((None, tm, n), lambda gi, ti: (gi, ti, 0)),) + tuple(cast_specs),
        compiler_params=_cparams(("arbitrary", "arbitrary")),
        name="inproj_a",
    )(x_lat, x_ctx, mod, w_bf, cos_t, sin_t, *cast_weights)
    return outs[0], outs[1:]


def _rope_tables(n_tokens):
    rows = n_tokens // GRID_W
    row_pos = jnp.repeat(jnp.arange(rows), GRID_W).astype(F32)
    col_pos = jnp.tile(jnp.arange(GRID_W), rows).astype(F32)
    n_freq = DIFF_DH // 4
    inv_freq = jnp.power(ROPE_BASE, -jnp.arange(n_freq, dtype=F32) / n_freq)
    ang = jnp.stack([row_pos[:, None] * inv_freq, col_pos[:, None] * inv_freq], axis=1)
    cos, sin = jnp.cos(ang), jnp.sin(ang)
    cos64 = jnp.concatenate([cos[:, 0], cos[:, 0], cos[:, 1], cos[:, 1]], axis=-1)
    sin64 = jnp.concatenate([-sin[:, 0], sin[:, 0], -sin[:, 1], sin[:, 1]], axis=-1)
    cos_l = jnp.tile(cos64, (1, LANES // DIFF_DH))
    sin_l = jnp.tile(sin64, (1, LANES // DIFF_DH))
    cos_t = jnp.stack([cos_l, jnp.ones_like(cos_l)])
    sin_t = jnp.stack([sin_l, jnp.zeros_like(sin_l)])
    return cos_t, sin_t


def _diff_attn_kernel(*refs, n_pieces, n_heads, lam_init, emit_kv):
    lam_ref, ng_ref, q_ref = refs[:3]
    kv_refs = refs[3:3 + 2 * n_pieces]
    o_ref = refs[3 + 2 * n_pieces]

    lp = lam_ref[...]
    lam = (jnp.exp(jnp.sum(lp[0:1] * lp[1:2], axis=-1, keepdims=True))
           - jnp.exp(jnp.sum(lp[2:3] * lp[3:4], axis=-1, keepdims=True)) + lam_init)
    nt = (((1,), (1,)), ((), ()))

    def softmax_pieces(ss):
        m = functools.reduce(jnp.maximum, [jnp.max(s, axis=-1, keepdims=True) for s in ss])
        es = [jnp.exp(s - m) for s in ss]
        l = functools.reduce(jnp.add, [jnp.sum(e, axis=-1, keepdims=True) for e in es])
        return [e / l for e in es]

    for h in range(n_heads):
        hs = slice(h * LANES, (h + 1) * LANES)
        q = q_ref[:, hs] * (DIFF_DH ** -0.5)
        lane = lax.broadcasted_iota(jnp.int32, q.shape, 1)
        q1 = jnp.where(lane < DIFF_DH, q, 0.0).astype(BF16)
        q2 = jnp.where(lane >= DIFF_DH, q, 0.0).astype(BF16)
        s1, s2, vs = [], [], []
        for i in range(n_pieces):
            kb = kv_refs[2 * i][:, hs].astype(BF16)
            vs.append(kv_refs[2 * i + 1][:, hs].astype(BF16))
            s1.append(lax.dot_general(q1, kb, nt, preferred_element_type=F32))
            s2.append(lax.dot_general(q2, kb, nt, preferred_element_type=F32))
        p1 = softmax_pieces(s1)
        p2 = softmax_pieces(s2)
        o = None
        for i in range(n_pieces):
            a = (p1[i] - lam * p2[i]).astype(BF16)
            t = jnp.dot(a, vs[i], preferred_element_type=F32)
            o = t if o is None else o + t
        o = o * lax.rsqrt(jnp.mean(o * o, axis=-1, keepdims=True) + LN_EPS)
        o_ref[:, hs] = (o * ng_ref[...] * (1.0 - lam_init)).astype(o_ref.dtype)
        if emit_kv:
            ko_ref, vo_ref = refs[4 + 2 * n_pieces:]
            ko_ref[h] = kv_refs[0][:, hs]
            vo_ref[h] = kv_refs[1][:, hs]


def _attn_context(proj, lam_p, norm_g, n_latent_groups, n_seq, seq, lam_init):
    g, s, _ = proj.shape
    per_group = s // seq
    blk = (None, seq, DIFF_WIDTH)

    def tok_map(colblock):
        return lambda b: (n_latent_groups + b // per_group, b % per_group, colblock)

    cache_shape = jax.ShapeDtypeStruct((n_seq, 1, DIFF_HEADS, seq, LANES), F32)
    cache_spec = pl.BlockSpec((None, None, DIFF_HEADS, seq, LANES), lambda b: (b, 0, 0, 0, 0))
    out_spec = pl.BlockSpec(blk, lambda b: (b // per_group, b % per_group, 0))
    return pl.pallas_call(
        functools.partial(_diff_attn_kernel, n_pieces=1, n_heads=DIFF_HEADS, lam_init=lam_init, emit_kv=True),
        out_shape=(jax.ShapeDtypeStruct((g - n_latent_groups, s, DIFF_WIDTH), BF16), cache_shape, cache_shape),
        grid=(n_seq,),
        in_specs=[
            pl.BlockSpec((4, DIFF_DH), lambda b: (0, 0)),
            pl.BlockSpec((1, LANES), lambda b: (0, 0)),
            pl.BlockSpec(blk, tok_map(0)),
            pl.BlockSpec(blk, tok_map(1)),
            pl.BlockSpec(blk, tok_map(2)),
        ],
        out_specs=(out_spec, cache_spec, cache_spec),
        compiler_params=_cparams(("arbitrary",)),
        name="attn_context",
    )(lam_p, norm_g, proj, proj, proj)


def _attn_latent(proj, cache_k, cache_v, lam_p, norm_g, n_latent_groups, lam_init, tq=256):
    g, s, _ = proj.shape
    past = cache_k.shape[3]
    cache_spec = pl.BlockSpec((None, None, None, past, LANES), lambda b, h, qi: (b, 0, h, 0, 0))
    return pl.pallas_call(
        functools.partial(_diff_attn_kernel, n_pieces=2, n_heads=1, lam_init=lam_init, emit_kv=False),
        out_shape=jax.ShapeDtypeStruct((n_latent_groups, s, DIFF_WIDTH), BF16),
        grid=(n_latent_groups, DIFF_HEADS, s // tq),
        in_specs=[
            pl.BlockSpec((4, DIFF_DH), lambda b, h, qi: (0, 0)),
            pl.BlockSpec((1, LANES), lambda b, h, qi: (0, 0)),
            pl.BlockSpec((None, tq, LANES), lambda b, h, qi: (b, qi, h)),
            cache_spec,
            cache_spec,
            pl.BlockSpec((None, s, LANES), lambda b, h, qi: (b, 0, DIFF_HEADS + h)),
            pl.BlockSpec((None, s, LANES), lambda b, h, qi: (b, 0, 2 * DIFF_HEADS + h)),
        ],
        out_specs=pl.BlockSpec((None, tq, LANES), lambda b, h, qi: (b, qi, h)),
        compiler_params=_cparams(("arbitrary", "arbitrary", "arbitrary")),
        name="attn_latent",
    )(lam_p, norm_g, proj, cache_k, cache_v, proj, proj)


POOL_ROW_BLOCK = 256
POOL_COL_WINDOW = 512
assert (POOL_COL_WINDOW - POOL_ROW_BLOCK) // 2 >= max(POOL_WINDOWS) // 2


def _pool_kernel(p_ref, w_ref, sc_ref, o_ref, band_ref, *, seq):
    @pl.when((pl.program_id(0) == 0) & (pl.program_id(1) == 0))
    def _():
        t = lax.broadcasted_iota(jnp.int32, (seq, seq), 0)
        s_ = lax.broadcasted_iota(jnp.int32, (seq, seq), 1)
        for gi, w in enumerate(POOL_WINDOWS):
            inside = (s_ >= t - w // 2) & (s_ <= t + w // 2 - 1)
            band_ref[gi] = jnp.where(inside, 1.0, 0.0).astype(BF16)

    tcol = lax.broadcasted_iota(jnp.int32, (seq, 1), 0)
    for gi, w in enumerate(POOL_WINDOWS):
        u = p_ref[:, gi * POOL_GC:(gi + 1) * POOL_GC]
        hi, lo = _split_bf16(u)
        rb = min(seq, POOL_ROW_BLOCK)
        cw = min(seq, POOL_COL_WINDOW)
        blocks = []
        for i in range(seq // rb):
            c0 = min(max(i * rb - (cw - rb) // 2, 0), seq - cw)
            band = band_ref[gi, i * rb:(i + 1) * rb, c0:c0 + cw]
            blocks.append(jnp.dot(band, hi[c0:c0 + cw], preferred_element_type=F32)
                          + jnp.dot(band, lo[c0:c0 + cw], preferred_element_type=F32))
        win = blocks[0] if len(blocks) == 1 else jnp.concatenate(blocks, axis=0)
        cnt = (jnp.minimum(tcol + (w // 2 - 1), seq - 1) - jnp.maximum(tcol - w // 2, 0) + 1).astype(F32)
        pooled = win / cnt - u
        mixed = jnp.dot(pooled.astype(BF16), w_ref[gi].astype(BF16), preferred_element_type=F32)
        o_ref[:, gi * POOL_GC:(gi + 1) * POOL_GC] = (
            mixed * sc_ref[:, gi * POOL_GC:(gi + 1) * POOL_GC]).astype(o_ref.dtype)


def _pool(proj, pool_w, pool_scale, g0, n_groups, seq):
    g, s, _ = proj.shape
    col = 3 * DIFF_WIDTH // POOL_WIDTH
    return pl.pallas_call(
        functools.partial(_pool_kernel, seq=seq),
        out_shape=jax.ShapeDtypeStruct((n_groups, s, POOL_WIDTH), BF16),
        grid=(n_groups, s // seq),
        in_specs=[
            pl.BlockSpec((None, seq, POOL_WIDTH), lambda gi, ti: (g0 + gi, ti, col)),
            pl.BlockSpec((POOL_GROUPS, POOL_GC, POOL_GC), lambda gi, ti: (0, 0, 0)),
            pl.BlockSpec((1, POOL_WIDTH), lambda gi, ti: (0, 0)),
        ],
        out_specs=pl.BlockSpec((None, seq, POOL_WIDTH), lambda gi, ti: (gi, ti, 0)),
        scratch_shapes=[pltpu.VMEM((POOL_GROUPS, seq, seq), BF16)],
        compiler_params=_cparams(("arbitrary", "arbitrary")),
        name="pool_%d" % seq,
    )(proj, pool_w, pool_scale.reshape(1, POOL_WIDTH))


def _outproj_kernel(*refs, n_in, gate_row, n_latent_groups):
    a_refs = refs[:2 * n_in]
    w_refs = refs[2 * n_in:3 * n_in]
    xl_ref, xc_ref, mod_ref, g_ref, b_ref, o_ref = refs[3 * n_in:]
    is_latent = pl.program_id(0) < n_latent_groups
    acc = None
    for i, w_ref in enumerate(w_refs):
        a = jnp.where(is_latent, a_refs[2 * i][...], a_refs[2 * i + 1][...])
        t = jnp.dot(a, w_ref[...], preferred_element_type=F32)
        acc = t if acc is None else acc + t
    x = jnp.where(is_latent, xl_ref[...], xc_ref[...])
    z = ALPHA * x + mod_ref[gate_row:gate_row + 1, :] * acc
    o_ref[...] = _layer_norm_rows(z, g_ref[...], b_ref[...])


def _outproj(acts, weights, x_lat, x_ctx, mod, ln_g, ln_b, gate_row, tm=512):
    gl = acts[0][0].shape[0]
    _, s, d = x_lat.shape
    if x_ctx is None:
        g = x_lat.shape[0]
        x_ctx, x_specs = x_lat, _two_stream_specs(tm, d, gl, gl)
    else:
        g = gl + x_ctx.shape[0]
        x_specs = _two_stream_specs(tm, d, gl)
    n_in = len(acts)
    in_specs = []
    flat_acts = []
    for a_lat, a_ctx in acts:
        in_specs += _two_stream_specs(tm, a_lat.shape[-1], gl)
        flat_acts += [a_lat, a_ctx]
    in_specs += [_resident(w.shape, lambda gi, ti: (0, 0)) for w in weights]
    in_specs += x_specs
    in_specs += [
        pl.BlockSpec((None, 6, d), lambda gi, ti: (gi, 0, 0)),
        pl.BlockSpec((1, d), lambda gi, ti: (0, 0)),
        pl.BlockSpec((1, d), lambda gi, ti: (0, 0)),
    ]
    return pl.pallas_call(
        functools.partial(_outproj_kernel, n_in=n_in, gate_row=gate_row, n_latent_groups=gl),
        out_shape=jax.ShapeDtypeStruct((g, s, d), F32),
        grid=(g, s // tm),
        in_specs=in_specs,
        out_specs=pl.BlockSpec((None, tm, d), lambda gi, ti: (gi, ti, 0)),
        compiler_params=_cparams(("arbitrary", "arbitrary")),
        name="outproj",
    )(*flat_acts, *weights, x_lat, x_ctx, mod, ln_g.reshape(1, d), ln_b.reshape(1, d))


def _swiglu_chunks(h_bf, wgu_ref, wd_ref, d_ff):
    acc = None
    for j in range(d_ff // FF_CHUNK):
        lo = j * FF_CHUNK
        gate = jnp.dot(h_bf, wgu_ref[:, lo:lo + FF_CHUNK], preferred_element_type=F32)
        up = jnp.dot(h_bf, wgu_ref[:, d_ff + lo:d_ff + lo + FF_CHUNK], preferred_element_type=F32)
        act = (gate * _sigmoid(gate) * up).astype(BF16)
        t = jnp.dot(act, wd_ref[lo:lo + FF_CHUNK, :], preferred_element_type=F32)
        acc = t if acc is None else acc + t
    return acc


def _ffn_kernel(x_ref, mod_ref, wgu_ref, wd_ref, g_ref, b_ref, *rest, n_cast):
    cast_in, o_ref, cast_out = rest[:n_cast], rest[n_cast], rest[n_cast + 1:]
    _cast_slabs(cast_in, cast_out)
    x = x_ref[...]
    h = (x * (1.0 + mod_ref[4:5, :]) + mod_ref[3:4, :]).astype(BF16)
    acc = _swiglu_chunks(h, wgu_ref, wd_ref, D_FF)
    z = ALPHA * x + mod_ref[5:6, :] * acc
    o_ref[...] = _layer_norm_rows(z, g_ref[...], b_ref[...])


def _ffn(x, mod, wgu_bf, wd_bf, ln_g, ln_b, cast_weights=(), tm=512):
    g, s, d = x.shape
    per = s // tm
    cast_specs = _cast_rider_specs(cast_weights, g * per, per)
    if cast_specs is None:
        out, _ = _ffn(x, mod, wgu_bf, wd_bf, ln_g, ln_b, (), tm)
        return out, tuple(w.astype(BF16) for w in cast_weights)
    outs = pl.pallas_call(
        functools.partial(_ffn_kernel, n_cast=len(cast_weights)),
        out_shape=(jax.ShapeDtypeStruct((g, s, d), F32),) + tuple(
            jax.ShapeDtypeStruct(w.shape, BF16) for w in cast_weights),
        grid=(g, per),
        in_specs=[
            pl.BlockSpec((None, tm, d), lambda gi, ti: (gi, ti, 0)),
            pl.BlockSpec((None, 6, d), lambda gi, ti: (gi, 0, 0)),
            _resident(wgu_bf.shape, lambda gi, ti: (0, 0)),
            _resident(wd_bf.shape, lambda gi, ti: (0, 0)),
            pl.BlockSpec((1, d), lambda gi, ti: (0, 0)),
            pl.BlockSpec((1, d), lambda gi, ti: (0, 0)),
        ] + cast_specs,
        out_specs=(pl.BlockSpec((None, tm, d), lambda gi, ti: (gi, ti, 0)),) + tuple(cast_specs),
        compiler_params=_cparams(("arbitrary", "arbitrary")),
        name="ffn",
    )(x, mod, wgu_bf, wd_bf, ln_g.reshape(1, d), ln_b.reshape(1, d), *cast_weights)
    return outs[0], outs[1:]


MOE_BM = 512
MOE_SB = 256
MOE_TC = 256
MOE_VMEM_LIMIT = 60 * 1024 * 1024
META_LANES = LANES


def _router_top2(h, rw_ref, rb_ref):
    h_hi, h_lo = _split_bf16(h)
    w_hi, w_lo = _split_bf16(rw_ref[...])
    logits = (jnp.dot(h_hi, w_hi, preferred_element_type=F32)
              + jnp.dot(h_lo, w_hi, preferred_element_type=F32)
              + jnp.dot(h_hi, w_lo, preferred_element_type=F32)) + rb_ref[...]
    lane = lax.broadcasted_iota(jnp.int32, logits.shape, 1).astype(F32)
    neg = -jnp.inf
    logits = jnp.where(lane < N_EXPERTS, logits, neg)
    m1 = jnp.max(logits, axis=-1, keepdims=True)
    i1 = jnp.min(jnp.where(logits == m1, lane, float(LANES)), axis=-1, keepdims=True)
    rest = jnp.where(lane == i1, neg, logits)
    m2 = jnp.max(rest, axis=-1, keepdims=True)
    i2 = jnp.min(jnp.where(rest == m2, lane, float(LANES)), axis=-1, keepdims=True)
    e2 = jnp.exp(m2 - m1)
    return lane, i1, i2, 1.0 / (1.0 + e2), e2 / (1.0 + e2)


def _route_kernel(x_ref, mod_ref, rw_ref, rb_ref, h_ref, wk_ref, srank_ref, cnt_ref, tri_ref, run_ref):
    tm, d = x_ref.shape

    @pl.when(pl.program_id(0) == 0)
    def _():
        r = lax.broadcasted_iota(jnp.int32, (tm, tm), 0)
        c = lax.broadcasted_iota(jnp.int32, (tm, tm), 1)
        tri_ref[...] = jnp.where(c <= r, 1.0, 0.0).astype(BF16)
        run_ref[...] = jnp.zeros_like(run_ref)

    h = x_ref[...] * (1.0 + mod_ref[4:5, :]) + mod_ref[3:4, :]
    h_ref[:, 0:d] = h.astype(BF16)
    lane, i1, i2, w1, w2 = _router_top2(h, rw_ref, rb_ref)
    first_is_low = i1 < i2
    e_hi = jnp.where(first_is_low, i2, i1)
    wk_ref[...] = jnp.where(lane == 0.0, jnp.where(first_is_low, w1, w2),
                            jnp.where(lane == 1.0, jnp.where(first_is_low, w2, w1), 0.0))
    tok = (pl.program_id(0) * tm + lax.broadcasted_iota(jnp.int32, (tm, META_LANES), 0))
    meta = jnp.where(lane == 0.0, (tok // 128).astype(F32),
                     jnp.where(lane == 1.0, (tok % 128).astype(F32),
                               jnp.where(lane == 2.0, e_hi, jnp.where(lane == 3.0, 1.0, 0.0))))
    h_ref[:, d:d + META_LANES] = meta.astype(BF16)

    member = (lane == i1) | (lane == i2)
    mem = jnp.where(member, 1.0, 0.0)
    rank = jnp.dot(tri_ref[...], mem.astype(BF16), preferred_element_type=F32) + run_ref[...]
    srank_ref[...] = jnp.where(member, rank, -rank).T[0:SUBLANES, :]
    run_ref[...] = rank[tm - 1:tm, :]
    cnt_ref[...] = rank[tm - 1:tm, :]


def _route(x, mod, router_w, router_b, tm=512):
    g, s, d = x.shape
    n = g * s
    assert n <= 128 * 256
    per = s // tm
    rw = jnp.pad(router_w, ((0, 0), (0, LANES - N_EXPERTS)))
    rb = jnp.pad(router_b, (0, LANES - N_EXPERTS)).reshape(1, LANES)
    return pl.pallas_call(
        _route_kernel,
        out_shape=(jax.ShapeDtypeStruct((n, d + META_LANES), BF16), jax.ShapeDtypeStruct((n, LANES), F32),
                   jax.ShapeDtypeStruct((SUBLANES, n), F32), jax.ShapeDtypeStruct((1, LANES), F32)),
        grid=(n // tm,),
        in_specs=[
            pl.BlockSpec((None, tm, d), lambda i: (i // per, i % per, 0)),
            pl.BlockSpec((None, 6, d), lambda i: (i // per, 0, 0)),
            pl.BlockSpec((d, LANES), lambda i: (0, 0)),
            pl.BlockSpec((1, LANES), lambda i: (0, 0)),
        ],
        out_specs=(pl.BlockSpec((tm, d + META_LANES), lambda i: (i, 0)), pl.BlockSpec((tm, LANES), lambda i: (i, 0)),
                   pl.BlockSpec((SUBLANES, tm), lambda i: (0, i)), pl.BlockSpec((1, LANES), lambda i: (0, 0))),
        scratch_shapes=[pltpu.VMEM((tm, tm), BF16), pltpu.VMEM((1, LANES), F32)],
        compiler_params=_cparams(("arbitrary",)),
        name="moe_route",
    )(x, mod, rw, rb)


def _moe_plan(srank, counts, n_blocks):
    e_n = N_EXPERTS
    i32 = jnp.int32
    cnt = counts[0, :e_n].astype(i32)
    nb = (cnt + MOE_BM - 1) // MOE_BM
    nb_incl = jnp.cumsum(nb)
    gstart = nb_incl - nb
    n_used = nb_incl[-1]
    sr = srank[:e_n]
    rank = jnp.abs(sr).astype(i32)
    pos_row = jnp.where(sr > 0, rank - 1 + MOE_BM * gstart[:, None], -1)
    chunk_end = rank[:, MOE_TC - 1::MOE_TC]

    rc = jnp.minimum(jnp.arange(n_blocks, dtype=i32), n_used - 1)
    e_r = jnp.minimum(jnp.sum(nb_incl[None, :] <= rc[:, None], axis=1, dtype=i32), e_n - 1)

    sb_per = MOE_BM // MOE_SB
    q = jnp.arange(n_blocks * sb_per, dtype=i32)
    r_q = jnp.minimum(q // sb_per, n_used - 1)
    e_q = e_r[r_q]
    first = ((r_q - gstart[e_q]) * sb_per + q % sb_per) * MOE_SB
    used = (q // sb_per < n_used) & (first < cnt[e_q])
    hi = jnp.minimum(first + MOE_SB, cnt[e_q])
    ends_q = chunk_end[e_q]
    jlo = jnp.where(used, jnp.sum(ends_q < (first + 1)[:, None], axis=1, dtype=i32), 0)
    jhi = jnp.where(used, jnp.sum(ends_q < hi[:, None], axis=1, dtype=i32), -1)
    return pos_row, e_r, jlo, jhi, n_used.reshape(1)


def _experts_kernel(be_ref, jlo_ref, jhi_ref, nused_ref, h_ref, pos_ref, wgu_ref, wd_ref, y_ref, meta_ref, xg_ref):
    r = pl.program_id(0)
    d = y_ref.shape[1]

    @pl.when(r < nused_ref[0])
    def _():
        e = be_ref[r]
        xg_ref[...] = jnp.zeros_like(xg_ref)
        sb_per = MOE_BM // MOE_SB
        for u in range(sb_per):
            sb_rows = slice(u * MOE_SB, (u + 1) * MOE_SB)
            slot = r * MOE_BM + u * MOE_SB + lax.broadcasted_iota(jnp.int32, (MOE_SB, MOE_TC), 0)

            def chunk(j, carry, sb_rows=sb_rows, slot=slot):
                onehot = jnp.where(pos_ref[e, pl.ds(j, 1), :] == slot, 1.0, 0.0).astype(BF16)
                rows = h_ref[pl.ds(pl.multiple_of(j * MOE_TC, MOE_TC), MOE_TC), :]
                xg_ref[sb_rows, :] += jnp.dot(onehot, rows, preferred_element_type=F32)
                return carry

            lax.fori_loop(jlo_ref[r * sb_per + u], jhi_ref[r * sb_per + u] + 1, chunk, 0)
        meta_ref[...] = xg_ref[:, d:d + META_LANES].T[0:SUBLANES, :]
        y_ref[...] = _swiglu_chunks(xg_ref[:, 0:d].astype(BF16), wgu_ref, wd_ref, D_FF_EXPERT)

    @pl.when(r >= nused_ref[0])
    def _():
        y_ref[...] = jnp.zeros_like(y_ref)
        meta_ref[...] = jnp.zeros_like(meta_ref)


def _experts(h, pos_row, plan, wgu_bf, wd_bf, n_blocks):
    n, dx = h.shape
    d = dx - META_LANES
    e_r, jlo, jhi, n_used = plan
    grid_spec = pltpu.PrefetchScalarGridSpec(
        num_scalar_prefetch=4,
        grid=(n_blocks,),
        in_specs=[
            _resident((n, dx), lambda r, be, lo, hi, nu: (0, 0)),
            _resident(pos_row.shape, lambda r, be, lo, hi, nu: (0, 0, 0)),
            pl.BlockSpec((None, d, 2 * D_FF_EXPERT), lambda r, be, lo, hi, nu: (be[r], 0, 0),
                         pipeline_mode=pl.Buffered(1)),
            pl.BlockSpec((None, D_FF_EXPERT, d), lambda r, be, lo, hi, nu: (be[r], 0, 0),
                         pipeline_mode=pl.Buffered(1)),
        ],
        out_specs=(pl.BlockSpec((MOE_BM, d), lambda r, be, lo, hi, nu: (r, 0)),
                   pl.BlockSpec((None, SUBLANES, MOE_BM), lambda r, be, lo, hi, nu: (r, 0, 0))),
        scratch_shapes=[pltpu.VMEM((MOE_BM, dx), F32)],
    )
    return pl.pallas_call(
        _experts_kernel,
        out_shape=(jax.ShapeDtypeStruct((n_blocks * MOE_BM, d), F32),
                   jax.ShapeDtypeStruct((n_blocks, SUBLANES, MOE_BM), F32)),
        grid_spec=grid_spec,
        compiler_params=_cparams(("arbitrary",), MOE_VMEM_LIMIT),
        name="moe_experts",
    )(e_r, jlo, jhi, n_used, h, pos_row, wgu_bf, wd_bf)


def _scatter_kernel(dest_ref, y_ref, o_ref, ybuf_ref, sem):
    r = pl.program_id(0)
    n_steps = pl.num_programs(0)
    buf = r % 2

    def wait_block(b):
        pltpu.make_async_copy(ybuf_ref.at[b], o_ref.at[pl.ds(0, MOE_BM), :], sem.at[b]).wait()

    @pl.when(r >= 2)
    def _():
        wait_block(buf)

    ybuf_ref[buf] = y_ref[...]

    for i in range(MOE_BM):
        pltpu.make_async_copy(ybuf_ref.at[buf, pl.ds(i, 1), :], o_ref.at[pl.ds(dest_ref[0, i], 1), :],
                              sem.at[buf]).start()

    @pl.when(r == n_steps - 1)
    def _():
        wait_block(1 - buf)
        wait_block(buf)


def _scatter_rows(y, dest, n_rows_out):
    n_slots, d = y.shape
    n_blocks = n_slots // MOE_BM
    assert n_blocks >= 2
    return pl.pallas_call(
        _scatter_kernel,
        out_shape=jax.ShapeDtypeStruct((n_rows_out, d), F32),
        grid=(n_blocks,),
        in_specs=[
            pl.BlockSpec((None, 1, MOE_BM), lambda r: (r, 0, 0), memory_space=pltpu.SMEM),
            pl.BlockSpec((MOE_BM, d), lambda r: (r, 0)),
        ],
        out_specs=pl.BlockSpec(memory_space=pl.ANY),
        scratch_shapes=[pltpu.VMEM((2, MOE_BM, d), F32), pltpu.SemaphoreType.DMA((2,))],
        compiler_params=_cparams(("arbitrary",)),
        name="moe_scatter",
    )(dest.reshape(n_blocks, 1, MOE_BM), y)


def _finish_kernel(a_ref, b_ref, wk_ref, x_ref, mod_ref, g_ref, bb_ref, ol_ref, oc_ref, *, n_latent_tiles):
    lane = lax.broadcasted_iota(jnp.int32, wk_ref.shape, 1)
    wk = wk_ref[...]
    w0 = jnp.sum(jnp.where(lane == 0, wk, 0.0), axis=-1, keepdims=True)
    w1 = jnp.sum(jnp.where(lane == 1, wk, 0.0), axis=-1, keepdims=True)
    acc = w0 * a_ref[...] + w1 * b_ref[...]
    z = ALPHA * x_ref[...] + mod_ref[5:6, :] * acc
    out = _layer_norm_rows(z, g_ref[...], bb_ref[...])
    is_latent = pl.program_id(0) < n_latent_tiles

    @pl.when(is_latent)
    def _():
        ol_ref[...] = out

    @pl.when(jnp.logical_not(is_latent))
    def _():
        oc_ref[...] = out


def _finish(rows, wk, x, mod, ln_g, ln_b, n_latent_groups, tm=512):
    g, s, d = x.shape
    n = g * s
    per = s // tm
    nt = n // tm
    nlt = n_latent_groups * per

    def lat_map(i):
        j = jnp.minimum(i, nlt - 1)
        return (j // per, j % per, 0)

    def ctx_map(i):
        j = jnp.maximum(i - nlt, 0)
        return (j // per, j % per, 0)

    return pl.pallas_call(
        functools.partial(_finish_kernel, n_latent_tiles=nlt),
        out_shape=(jax.ShapeDtypeStruct((n_latent_groups, s, d), F32),
                   jax.ShapeDtypeStruct((g - n_latent_groups, s, d), F32)),
        grid=(nt,),
        in_specs=[
            pl.BlockSpec((tm, d), lambda i: (i, 0)),
            pl.BlockSpec((tm, d), lambda i: (nt + i, 0)),
            pl.BlockSpec((tm, LANES), lambda i: (i, 0)),
            pl.BlockSpec((None, tm, d), lambda i: (i // per, i % per, 0)),
            pl.BlockSpec((None, 6, d), lambda i: (i // per, 0, 0)),
            pl.BlockSpec((1, d), lambda i: (0, 0)),
            pl.BlockSpec((1, d), lambda i: (0, 0)),
        ],
        out_specs=(pl.BlockSpec((None, tm, d), lat_map), pl.BlockSpec((None, tm, d), ctx_map)),
        compiler_params=_cparams(("arbitrary",)),
        name="moe_finish",
    )(rows, rows, wk, x, mod, ln_g.reshape(1, d), ln_b.reshape(1, d))


def _moe(x, mod, router_w, router_b, wgu_bf, wd_bf, ln_g, ln_b, n_latent_groups):
    g, s, d = x.shape
    n = g * s
    n_blocks = 2 * n // MOE_BM + N_EXPERTS
    n_slots = n_blocks * MOE_BM
    h, wk, srank, counts = _route(x, mod, router_w, router_b)
    pos_row, e_r, jlo, jhi, n_used = _moe_plan(srank, counts, n_blocks)
    pos_row = pos_row.reshape(N_EXPERTS, n // MOE_TC, MOE_TC)
    y, meta = _experts(h, pos_row, (e_r, jlo, jhi, n_used), wgu_bf, wd_bf, n_blocks)
    tok = (meta[:, 0] * 128.0 + meta[:, 1]).astype(jnp.int32)
    choice = (meta[:, 2] == e_r[:, None].astype(F32)).astype(jnp.int32)
    unused = meta[:, 3] < 0.5
    spare = (2 * n + jnp.cumsum(unused.reshape(-1).astype(jnp.int32)) - 1).reshape(n_blocks, MOE_BM)
    dest = jnp.where(unused, spare, choice * n + tok)
    rows = _scatter_rows(y, dest, n_slots)
    return _finish(rows, wk, x, mod, ln_g, ln_b, n_latent_groups)


def _inproj_c_kernel(x_ref, mod_ref, w_ref, wg_ref, bg_ref, o_ref, og_ref):
    h = (x_ref[...] * (1.0 + mod_ref[1:2, :]) + mod_ref[0:1, :]).astype(BF16)
    o_ref[...] = jnp.dot(h, w_ref[:, 0:o_ref.shape[1]], preferred_element_type=F32)
    og_ref[...] = jnp.dot(h, wg_ref[...], preferred_element_type=F32) + bg_ref[...]


def _inproj_c(x, mod, w_bf, n, wg_bf, bg, tm=512):
    g, s, d = x.shape
    ng = wg_bf.shape[1]
    return pl.pallas_call(
        _inproj_c_kernel,
        out_shape=(jax.ShapeDtypeStruct((g, s, n), F32), jax.ShapeDtypeStruct((g, s, ng), F32)),
        grid=(g, s // tm),
        in_specs=[
            pl.BlockSpec((None, tm, d), lambda gi, ti: (gi, ti, 0)),
            pl.BlockSpec((None, 6, d), lambda gi, ti: (gi, 0, 0)),
            _resident(w_bf.shape, lambda gi, ti: (0, 0)),
            _resident((d, ng), lambda gi, ti: (0, 0)),
            pl.BlockSpec((1, ng), lambda gi, ti: (0, 0)),
        ],
        out_specs=(pl.BlockSpec((None, tm, n), lambda gi, ti: (gi, ti, 0)),
                   pl.BlockSpec((None, tm, ng), lambda gi, ti: (gi, ti, 0))),
        compiler_params=_cparams(("arbitrary", "arbitrary")),
        name="inproj_c",
    )(x, mod, w_bf, wg_bf, bg)


def _log_sigmoid(x):
    return jnp.minimum(x, 0.0) - jnp.log(1.0 + jnp.exp(-jnp.abs(x)))


MLSTM_L = 128


def _split3_bf16(x):
    hi = x.astype(BF16)
    r1 = x - hi.astype(F32)
    mid = r1.astype(BF16)
    lo = (r1 - mid.astype(F32)).astype(BF16)
    return hi, mid, lo


def _mlstm_kernel(*refs, seq, hg, has_init, emit_state):
    q_ref, k_ref, v_ref, o_ref, gi_ref, gf_ref, hgain_ref = refs[:7]
    pos = 7
    if has_init:
        c0_ref, n0_ref, m0_ref = refs[pos:pos + 3]
        pos += 3
    out_ref = refs[pos]
    pos += 1
    if emit_state:
        co_ref, no_ref, mo_ref = refs[pos:pos + 3]
        pos += 3
    cext_ref, hf_ref, hb_ref, b_ref, g_ref, gmax_ref, mt_ref, wi_ref, en_ref, ws_ref, gt_ref, wc_ref = refs[pos:]

    L = MLSTM_L
    dh = MLSTM_DH
    nh = MLSTM_HEADS
    nc = seq // L
    head0 = pl.program_id(1) * hg
    neg = -jnp.inf

    lane = lax.broadcasted_iota(jnp.int32, (L, LANES), 1)
    lane1 = lax.broadcasted_iota(jnp.int32, (1, LANES), 1)
    row = lax.broadcasted_iota(jnp.int32, (L, L), 0)
    col = lax.broadcasted_iota(jnp.int32, (L, L), 1)
    lower = col <= row
    upper = col >= row
    tri_l = jnp.where(lower, 1.0, 0.0).astype(BF16)
    tri_u = jnp.where(upper, 1.0, 0.0).astype(BF16)
    fwd_lane = lane < nh
    fwd_lane1 = lane1 < nh
    trow = lax.broadcasted_iota(jnp.int32, (L, LANES), 0)

    btot, glast = [], []
    for c in range(nc):
        rows = slice(c * L, (c + 1) * L)
        f = _log_sigmoid(gf_ref[rows, :])
        parts = _split3_bf16(f)
        pre = sum(jnp.dot(tri_l, p, preferred_element_type=F32) for p in parts)
        suf = sum(jnp.dot(tri_u, p, preferred_element_type=F32) for p in parts)
        b = jnp.where(fwd_lane, pre, suf)
        g = gi_ref[rows, :] - b
        gp, gs = g, g
        k = 1
        while k < L:
            gp = jnp.where(trow >= k, jnp.maximum(gp, pltpu.roll(gp, k, 0)), gp)
            gs = jnp.where(trow < L - k, jnp.maximum(gs, pltpu.roll(gs, L - k, 0)), gs)
            k *= 2
        gmax = jnp.where(fwd_lane, gp, gs)
        b_ref[rows, :] = b
        g_ref[rows, :] = g
        gmax_ref[rows, :] = gmax
        btot.append(jnp.where(fwd_lane1, b[L - 1:L, :], b[0:1, :]))
        glast.append(jnp.where(fwd_lane1, gmax[L - 1:L, :], gmax[0:1, :]))

    m_init = m0_ref[...] if has_init else jnp.zeros((1, LANES), F32)
    mf, mb = m_init, m_init
    ms_f, mn_f, ms_b, mn_b = [None] * nc, [None] * nc, [None] * nc, [None] * nc
    for c in range(nc):
        ms_f[c] = mf
        mf = btot[c] + jnp.maximum(mf, glast[c])
        mn_f[c] = mf
        cb = nc - 1 - c
        ms_b[cb] = mb
        mb = btot[cb] + jnp.maximum(mb, glast[cb])
        mn_b[cb] = mb
    m_final = jnp.where(fwd_lane1, mf, mb)

    for c in range(nc):
        rows = slice(c * L, (c + 1) * L)
        m_start = jnp.where(fwd_lane1, ms_f[c], ms_b[c])
        m_next = jnp.where(fwd_lane1, mn_f[c], mn_b[c])
        g = g_ref[rows, :]
        mt = jnp.maximum(m_start, gmax_ref[rows, :])
        mt_ref[rows, :] = mt
        wi_ref[rows, :] = jnp.exp(m_start - mt)
        en_ref[rows, :] = jnp.exp(-(b_ref[rows, :] + mt))
        ws_ref[rows, :] = jnp.exp(btot[c] + g - m_next)
        gt_ref[c] = g.T
        wc_ref[c:c + 1, :] = jnp.exp(btot[c] + m_start - m_next)

    for d in range(2):
        for hh in range(hg):
            idx = d * hg + hh
            if has_init:
                cext_ref[idx, :, 0:dh] = c0_ref[d, hh]
                n0_tile = jnp.where(lax.broadcasted_iota(jnp.int32, (dh, dh), 0) == 0, n0_ref[d, hh], 0.0)
                cext_ref[idx, :, dh:2 * dh] = n0_tile.T
            else:
                cext_ref[idx] = jnp.zeros((dh, 2 * dh), F32)

    ones_col = jnp.where(lane == 0, 1.0, 0.0).astype(BF16)
    nt = (((1,), (1,)), ((), ()))
    tn = (((0,), (0,)), ((), ()))

    def column(x, j):
        return jnp.sum(jnp.where(lane == j, x, 0.0), axis=-1, keepdims=True)

    def one_direction(d, hh, c, s_qk, q_bf, k_s, v_ext, v_bf):
        idx = d * hg + hh
        j = d * nh + head0 + hh
        rows = pl.ds(pl.multiple_of(c * L, L), L)
        mt = column(mt_ref[rows, :], j)
        wi = column(wi_ref[rows, :], j)
        en = column(en_ref[rows, :], j)
        ws = column(ws_ref[rows, :], j)
        g_r = gt_ref[c, pl.ds(j, 1), :]
        w_c = jnp.sum(jnp.where(lane1 == j, wc_ref[pl.ds(c, 1), :], 0.0), axis=-1, keepdims=True)
        causal = lower if d == 0 else upper
        p = s_qk * jnp.exp(jnp.where(causal, g_r - mt, neg))
        qc = jnp.dot(q_bf, cext_ref[idx].astype(BF16), preferred_element_type=F32)
        num = wi * qc[:, 0:dh] + jnp.dot(p.astype(BF16), v_bf, preferred_element_type=F32)
        den = wi * qc[:, dh:dh + 1] + jnp.sum(p, axis=-1, keepdims=True)
        h = num / jnp.maximum(jnp.abs(den), en)
        upd = lax.dot_general((ws * k_s).astype(BF16), v_ext, tn, preferred_element_type=F32)
        cext_ref[idx] = w_c * cext_ref[idx] + upd
        return h

    def load_chunk(hh, c):
        sl = (pl.ds(pl.multiple_of(c * L, L), L), slice(hh * dh, (hh + 1) * dh))
        q_bf = q_ref[sl].astype(BF16)
        k_s = k_ref[sl] * (dh ** -0.5)
        v_bf = v_ref[sl].astype(BF16)
        v_ext = jnp.concatenate([v_bf, ones_col], axis=-1)
        s_qk = lax.dot_general(q_bf, k_s.astype(BF16), nt, preferred_element_type=F32)
        return s_qk, q_bf, k_s, v_ext, v_bf

    def step(c, carry):
        cb = nc - 1 - c
        for hh in range(hg):
            h = one_direction(0, hh, c, *load_chunk(hh, c))
            hf_ref[pl.ds(pl.multiple_of(c * L, L), L), hh * dh:(hh + 1) * dh] = h
        for hh in range(hg):
            h = one_direction(1, hh, cb, *load_chunk(hh, cb))
            hb_ref[pl.ds(pl.multiple_of(cb * L, L), L), hh * dh:(hh + 1) * dh] = h
        return carry

    lax.fori_loop(0, nc, step, 0)

    for hh in range(hg):
        cs = slice(hh * dh, (hh + 1) * dh)
        hs = hf_ref[:, cs] + hb_ref[:, cs]
        mu = jnp.mean(hs, axis=-1, keepdims=True)
        hc = hs - mu
        var = jnp.mean(hc * hc, axis=-1, keepdims=True)
        hn = hc * lax.rsqrt(var + LN_EPS) * hgain_ref[:, cs]
        out_ref[:, cs] = (_sigmoid(o_ref[:, cs]) * hn).astype(out_ref.dtype)

    if emit_state:
        for d in range(2):
            for hh in range(hg):
                idx = d * hg + hh
                co_ref[d, hh] = cext_ref[idx, :, 0:dh]
                no_ref[d, hh] = cext_ref[idx, :, dh:2 * dh].T[0:1, :]
        mo_ref[...] = m_final


def _mlstm(proj, gates, head_g, g0, n_seq, seq, hg, init=None, emit_state=False):
    g, s, _ = proj.shape
    per_group = s // seq
    n_hg = MLSTM_HEADS // hg
    w = hg * MLSTM_DH
    nc = seq // MLSTM_L
    n_blocks = D_MODEL // w

    def tok_map(colblock):
        return lambda b, hi: (g0 + b // per_group, b % per_group, colblock * n_blocks + hi)

    def gate_map(half):
        return lambda b, hi: (g0 + b // per_group, b % per_group, half)

    args = [proj, proj, proj, proj, gates, gates, head_g.reshape(1, D_MODEL)]
    in_specs = [
        pl.BlockSpec((None, seq, w), tok_map(0)),
        pl.BlockSpec((None, seq, w), tok_map(1)),
        pl.BlockSpec((None, seq, w), tok_map(2)),
        pl.BlockSpec((None, seq, w), tok_map(3)),
        pl.BlockSpec((None, seq, LANES), gate_map(0)),
        pl.BlockSpec((None, seq, LANES), gate_map(1)),
        pl.BlockSpec((1, w), lambda b, hi: (0, hi)),
    ]
    if init is not None:
        c0, n0, m0 = init
        m0_lanes = jnp.pad(m0.reshape(n_seq, 1, 2 * MLSTM_HEADS), ((0, 0), (0, 0), (0, LANES - 2 * MLSTM_HEADS)))
        args += [c0, n0.reshape(n0.shape[:-1] + (1, MLSTM_DH)), m0_lanes]
        in_specs += [
            pl.BlockSpec((None, None, 2, hg, MLSTM_DH, MLSTM_DH), lambda b, hi: (b, 0, 0, hi, 0, 0)),
            pl.BlockSpec((None, 2, hg, 1, MLSTM_DH), lambda b, hi: (b, 0, hi, 0, 0)),
            pl.BlockSpec((None, 1, LANES), lambda b, hi: (b, 0, 0)),
        ]

    out_shape = [jax.ShapeDtypeStruct((n_seq // per_group, s, D_MODEL), BF16)]
    out_specs = [pl.BlockSpec((None, seq, w), lambda b, hi: (b // per_group, b % per_group, hi))]
    if emit_state:
        out_shape += [
            jax.ShapeDtypeStruct((n_seq, 2, MLSTM_HEADS, MLSTM_DH, MLSTM_DH), F32),
            jax.ShapeDtypeStruct((n_seq, 2, MLSTM_HEADS, 1, MLSTM_DH), F32),
            jax.ShapeDtypeStruct((n_seq, n_hg, 1, LANES), F32),
        ]
        out_specs += [
            pl.BlockSpec((None, 2, hg, MLSTM_DH, MLSTM_DH), lambda b, hi: (b, 0, hi, 0, 0)),
            pl.BlockSpec((None, 2, hg, 1, MLSTM_DH), lambda b, hi: (b, 0, hi, 0, 0)),
            pl.BlockSpec((None, None, 1, LANES), lambda b, hi: (b, hi, 0, 0)),
        ]

    tok_scratch = pltpu.VMEM((seq, LANES), F32)
    return pl.pallas_call(
        functools.partial(_mlstm_kernel, seq=seq, hg=hg, has_init=init is not None, emit_state=emit_state),
        out_shape=tuple(out_shape),
        grid=(n_seq, n_hg),
        in_specs=in_specs,
        out_specs=tuple(out_specs),
        scratch_shapes=[
            pltpu.VMEM((2 * hg, MLSTM_DH, 2 * MLSTM_DH), F32),
            pltpu.VMEM((seq, w), F32),
            pltpu.VMEM((seq, w), F32),
        ] + [tok_scratch] * 7 + [
            pltpu.VMEM((nc, LANES, MLSTM_L), F32),
            pltpu.VMEM((max(nc, 8), LANES), F32),
        ],
        compiler_params=_cparams(("arbitrary", "arbitrary")),
        name="mlstm_%d" % seq,
    )(*args)


def kernel(x_prompt, x_sample, c, cache_k, cache_v, state_C, state_n, state_m, c_ctx, ada_w, ada_b, ln_g, ln_b, w_in_a, diff_lambda, diff_norm_g, pool_w, pool_scale, w_out_a, ffn_w_gu, ffn_w_down, w_in_c, b_gates_c, mlstm_norm_g, w_out_c, router_w, router_b, moe_w_gu, moe_w_down):
    n_ctx, seq_ctx, d = x_prompt.shape
    n_lat, seq_lat, _ = x_sample.shape
    assert d == D_MODEL and (n_ctx * seq_ctx) % seq_lat == 0 and seq_lat % seq_ctx == 0
    gl = n_lat
    gc = n_ctx * seq_ctx // seq_lat
    s = seq_lat

    x_ctx = x_prompt.reshape(gc, s, d)
    cvec = jnp.concatenate([c, jnp.broadcast_to(c_ctx[None, :], (gc, d))], axis=0)
    mod_all = _modulation(cvec, ada_w, ada_b).reshape(DEPTH, gl + gc, 6, d)

    mod = mod_all[0]
    lam_init = 0.8 - 0.6 * math.exp(-0.3 * 0)
    cos_t, sin_t = _rope_tables(s)
    proj, (ffn_wgu_bf, ffn_wd_bf) = _inproj_a(x_sample, x_ctx, mod, w_in_a[0].astype(BF16), cos_t, sin_t,
                                              (ffn_w_gu, ffn_w_down))
    norm_g = diff_norm_g[0].reshape(1, LANES)
    attn_c, new_k, new_v = _attn_context(proj, diff_lambda[0], norm_g, gl, n_ctx, seq_ctx, lam_init)
    attn_l = _attn_latent(proj, cache_k, cache_v, diff_lambda[0], norm_g, gl, lam_init)
    pool_c = _pool(proj, pool_w[0], pool_scale[0], gl, gc, seq_ctx)
    pool_l = _pool(proj, pool_w[0], pool_scale[0], 0, gl, seq_lat)
    w_out = w_out_a[0].astype(BF16)
    x = _outproj([(attn_l, attn_c), (pool_l, pool_c)], [w_out[:DIFF_WIDTH], w_out[DIFF_WIDTH:]],
                 x_sample, x_ctx, mod, ln_g[0, 0], ln_b[0, 0], 2)
    x, (moe_wgu_bf, moe_wd_bf, w_in_c_bf) = _ffn(x, mod, ffn_wgu_bf[0], ffn_wd_bf[0], ln_g[0, 1], ln_b[0, 1],
                                                  (moe_w_gu[0], moe_w_down[0], w_in_c))

    mod = mod_all[1]
    n_main = 4 * D_MODEL
    nh = MLSTM_HEADS
    wg4 = w_in_c[0][:, n_main:].reshape(d, N_GATES, nh)
    bg4 = b_gates_c[0].reshape(1, N_GATES, nh)
    lane_pad = ((0, 0), (0, LANES - 2 * nh))

    def gate_lanes(a):
        return jnp.concatenate([jnp.pad(jnp.concatenate([a[:, 0], a[:, 2]], axis=-1), lane_pad),
                                jnp.pad(jnp.concatenate([a[:, 1], a[:, 3]], axis=-1), lane_pad)], axis=-1)

    proj, gates = _inproj_c(x, mod, w_in_c_bf[0], n_main, gate_lanes(wg4).astype(BF16), gate_lanes(bg4))
    mix_c, new_c, new_n, new_m = _mlstm(proj, gates, mlstm_norm_g[0], gl, n_ctx, seq_ctx, MLSTM_HEADS,
                                        emit_state=True)
    (mix_l,) = _mlstm(proj, gates, mlstm_norm_g[0], 0, n_lat, seq_lat, 4,
                      init=(state_C, state_n[:, 0], state_m[:, 0]))
    x = _outproj([(mix_l, mix_c)], [w_out_c[0].astype(BF16)], x, None, mod, ln_g[1, 0], ln_b[1, 0], 2)
    y_sample, y_ctx = _moe(x, mod, router_w[0], router_b[0], moe_wgu_bf, moe_wd_bf,
                           ln_g[1, 1], ln_b[1, 1], gl)
    y_prompt = y_ctx.reshape(n_ctx, seq_ctx, d)
    new_m = new_m[:, 0, 0, :2 * MLSTM_HEADS].reshape(n_ctx, 2, MLSTM_HEADS)
    return (y_prompt, y_sample, new_k, new_v, new_c[:, None], new_n[..., 0, :][:, None], new_m[:, None])
```

```python
import functools
import math

import jax
import jax.numpy as jnp
from jax import lax
from jax.experimental import pallas as pl
from jax.experimental.pallas import tpu as pltpu

F32 = jnp.float32
BF16 = jnp.bfloat16

D_MODEL = 1024
GRID_W = 64
ROPE_BASE = 10000.0
DIFF_HEADS = 4
DIFF_DH = 64
DIFF_WIDTH = DIFF_HEADS * 2 * DIFF_DH
POOL_GROUPS = 4
POOL_GC = 128
POOL_WIDTH = POOL_GROUPS * POOL_GC
POOL_WINDOWS = (2, 4, 8, 16)
MLSTM_HEADS = 8
MLSTM_DH = 128
N_GATES = 4
D_FF = 2816
N_EXPERTS = 8
D_FF_EXPERT = 1792
LN_EPS = 1e-5
DEPTH = 2
ALPHA = (2.0 * DEPTH) ** 0.25

LANES = 128
SUBLANES = 8
FF_CHUNK = 256
VMEM_LIMIT = 56 * 1024 * 1024


def _cparams(sem, vmem=VMEM_LIMIT):
    return pltpu.CompilerParams(dimension_semantics=sem, vmem_limit_bytes=vmem)


def _resident(shape, index_map):
    return pl.BlockSpec(shape, index_map, pipeline_mode=pl.Buffered(1))


def _layer_norm_rows(z, g, b):
    mu = jnp.mean(z, axis=-1, keepdims=True)
    zc = z - mu
    var = jnp.mean(zc * zc, axis=-1, keepdims=True)
    return zc * lax.rsqrt(var + LN_EPS) * g + b


def _sigmoid(x):
    return 1.0 / (1.0 + jnp.exp(-x))


def _split_bf16(x):
    hi = x.astype(BF16)
    lo = (x - hi.astype(F32)).astype(BF16)
    return hi, lo


def _cast_rider_specs(weights, n_steps, per):
    specs = []
    for w in weights:
        e, rows, cols = w.shape
        per_e = next((k for k in range(n_steps // e, 0, -1)
                      if n_steps % (e * k) == 0 and rows % (2 * SUBLANES * k) == 0), None)
        if per_e is None:
            return None
        hold = n_steps // (e * per_e)

        def slab_map(gi, ti, per_e=per_e, hold=hold):
            slab = (gi * per + ti) // hold
            return (slab // per_e, slab % per_e, 0)

        specs.append(pl.BlockSpec((None, rows // per_e, cols), slab_map))
    return specs


def _cast_slabs(cast_in, cast_out):
    for src, dst in zip(cast_in, cast_out):
        dst[...] = src[...].astype(BF16)


def _mod_kernel(c_ref, w_ref, b_ref, o_ref):
    c = c_ref[...]
    h = (c * _sigmoid(c)).astype(BF16)
    o_ref[...] = jnp.dot(h, w_ref[...].astype(BF16), preferred_element_type=F32) + b_ref[...]


def _modulation(cvec, ada_w, ada_b):
    depth, d, n = ada_w.shape
    g = cvec.shape[0]
    tn = 1536
    return pl.pallas_call(
        _mod_kernel,
        out_shape=jax.ShapeDtypeStruct((depth, g, n), F32),
        grid=(depth, n // tn),
        in_specs=[
            pl.BlockSpec((g, d), lambda l, j: (0, 0)),
            pl.BlockSpec((None, d, tn), lambda l, j: (l, 0, j)),
            pl.BlockSpec((None, 1, tn), lambda l, j: (l, 0, j)),
        ],
        out_specs=pl.BlockSpec((None, g, tn), lambda l, j: (l, 0, j)),
        compiler_params=_cparams(("arbitrary", "arbitrary")),
        name="modulation",
    )(cvec, ada_w, ada_b.reshape(depth, 1, n))


def _rot_half16(x):
    lane = lax.broadcasted_iota(jnp.int32, x.shape, 1)
    return jnp.where((lane % 32) < 16, pltpu.roll(x, LANES - 16, 1), pltpu.roll(x, 16, 1))


def _two_stream_specs(tm, d, gl, ctx_first_group=0):
    return [pl.BlockSpec((None, tm, d), lambda gi, ti: (jnp.minimum(gi, gl - 1), jnp.where(gi < gl, ti, 0), 0)),
            pl.BlockSpec((None, tm, d), lambda gi, ti: (ctx_first_group + jnp.maximum(gi - gl, 0),
                                                        jnp.where(gi < gl, 0, ti), 0))]


def _inproj_a_kernel(xl_ref, xc_ref, mod_ref, w_ref, cos_ref, sin_ref, *rest, n_latent_groups, n_cast):
    cast_in, o_ref, cast_out = rest[:n_cast], rest[n_cast], rest[n_cast + 1:]
    _cast_slabs(cast_in, cast_out)
    x = jnp.where(pl.program_id(0) < n_latent_groups, xl_ref[...], xc_ref[...])
    h = x * (1.0 + mod_ref[1:2, :]) + mod_ref[0:1, :]
    p = jnp.dot(h.astype(BF16), w_ref[...], preferred_element_type=F32)
    cos = cos_ref[...]
    sin = sin_ref[...]
    n_rope = 2 * DIFF_WIDTH // LANES
    for j in range(n_rope):
        blk = p[:, j * LANES:(j + 1) * LANES]
        o_ref[:, j * LANES:(j + 1) * LANES] = blk * cos + _rot_half16(blk) * sin
    o_ref[:, n_rope * LANES:] = p[:, n_rope * LANES:]


def _inproj_a(x_lat, x_ctx, mod, w_bf, cos_t, sin_t, cast_weights=(), tm=512):
    n_latent_groups, s, d = x_lat.shape
    g = n_latent_groups + x_ctx.shape[0]
    n = w_bf.shape[1]
    per = s // tm
    cast_specs = _cast_rider_specs(cast_weights, g * per, per)
    if cast_specs is None:
        proj, _ = _inproj_a(x_lat, x_ctx, mod, w_bf, cos_t, sin_t, (), tm)
        return proj, tuple(w.astype(BF16) for w in cast_weights)

    def table_map(gi, ti):
        return (jnp.where(gi >= n_latent_groups, 1, 0), ti, 0)

    outs = pl.pallas_call(
        functools.partial(_inproj_a_kernel, n_latent_groups=n_latent_groups, n_cast=len(cast_weights)),
        out_shape=(jax.ShapeDtypeStruct((g, s, n), F32),) + tuple(
            jax.ShapeDtypeStruct(w.shape, BF16) for w in cast_weights),
        grid=(g, per),
        in_specs=_two_stream_specs(tm, d, n_latent_groups) + [
            pl.BlockSpec((None, 6, d), lambda gi, ti: (gi, 0, 0)),
            _resident((d, n), lambda gi, ti: (0, 0)),
            pl.BlockSpec((None, tm, LANES), table_map),
            pl.BlockSpec((None, tm, LANES), table_map),
        ] + cast_specs,
        out_specs=(pl.BlockSpec((None, tm, n), lambda gi, ti: (gi, ti, 0)),) + tuple(cast_specs),
        compiler_params=_cparams(("arbitrary", "arbitrary")),
        name="inproj_a",
    )(x_lat, x_ctx, mod, w_bf, cos_t, sin_t, *cast_weights)
    return outs[0], outs[1:]


def _rope_tables(n_tokens):
    rows = n_tokens // GRID_W
    row_pos = jnp.repeat(jnp.arange(rows), GRID_W).astype(F32)
    col_pos = jnp.tile(jnp.arange(GRID_W), rows).astype(F32)
    n_freq = DIFF_DH // 4
    inv_freq = jnp.power(ROPE_BASE, -jnp.arange(n_freq, dtype=F32) / n_freq)
    ang = jnp.stack([row_pos[:, None] * inv_freq, col_pos[:, None] * inv_freq], axis=1)
    cos, sin = jnp.cos(ang), jnp.sin(ang)
    cos64 = jnp.concatenate([cos[:, 0], cos[:, 0], cos[:, 1], cos[:, 1]], axis=-1)
    sin64 = jnp.concatenate([-sin[:, 0], sin[:, 0], -sin[:, 1], sin[:, 1]], axis=-1)
    cos_l = jnp.tile(cos64, (1, LANES // DIFF_DH))
    sin_l = jnp.tile(sin64, (1, LANES // DIFF_DH))
    cos_t = jnp.stack([cos_l, jnp.ones_like(cos_l)])
    sin_t = jnp.stack([sin_l, jnp.zeros_like(sin_l)])
    return cos_t, sin_t


def _diff_attn_kernel(*refs, n_pieces, n_heads, lam_init, emit_kv):
    lam_ref, ng_ref, q_ref = refs[:3]
    kv_refs = refs[3:3 + 2 * n_pieces]
    o_ref = refs[3 + 2 * n_pieces]

    lp = lam_ref[...]
    lam = (jnp.exp(jnp.sum(lp[0:1] * lp[1:2], axis=-1, keepdims=True))
           - jnp.exp(jnp.sum(lp[2:3] * lp[3:4], axis=-1, keepdims=True)) + lam_init)
    nt = (((1,), (1,)), ((), ()))

    def softmax_pieces(ss):
        m = functools.reduce(jnp.maximum, [jnp.max(s, axis=-1, keepdims=True) for s in ss])
        es = [jnp.exp(s - m) for s in ss]
        l = functools.reduce(jnp.add, [jnp.sum(e, axis=-1, keepdims=True) for e in es])
        return [e / l for e in es]

    for h in range(n_heads):
        hs = slice(h * LANES, (h + 1) * LANES)
        q = q_ref[:, hs] * (DIFF_DH ** -0.5)
        lane = lax.broadcasted_iota(jnp.int32, q.shape, 1)
        q1 = jnp.where(lane < DIFF_DH, q, 0.0).astype(BF16)
        q2 = jnp.where(lane >= DIFF_DH, q, 0.0).astype(BF16)
        s1, s2, vs = [], [], []
        for i in range(n_pieces):
            kb = kv_refs[2 * i][:, hs].astype(BF16)
            vs.append(kv_refs[2 * i + 1][:, hs].astype(BF16))
            s1.append(lax.dot_general(q1, kb, nt, preferred_element_type=F32))
            s2.append(lax.dot_general(q2, kb, nt, preferred_element_type=F32))
        p1 = softmax_pieces(s1)
        p2 = softmax_pieces(s2)
        o = None
        for i in range(n_pieces):
            a = (p1[i] - lam * p2[i]).astype(BF16)
            t = jnp.dot(a, vs[i], preferred_element_type=F32)
            o = t if o is None else o + t
        o = o * lax.rsqrt(jnp.mean(o * o, axis=-1, keepdims=True) + LN_EPS)
        o_ref[:, hs] = (o * ng_ref[...] * (1.0 - lam_init)).astype(o_ref.dtype)
        if emit_kv:
            ko_ref, vo_ref = refs[4 + 2 * n_pieces:]
            ko_ref[h] = kv_refs[0][:, hs]
            vo_ref[h] = kv_refs[1][:, hs]


def _attn_context(proj, lam_p, norm_g, n_latent_groups, n_seq, seq, lam_init):
    g, s, _ = proj.shape
    per_group = s // seq
    blk = (None, seq, DIFF_WIDTH)

    def tok_map(colblock):
        return lambda b: (n_latent_groups + b // per_group, b % per_group, colblock)

    cache_shape = jax.ShapeDtypeStruct((n_seq, 1, DIFF_HEADS, seq, LANES), F32)
    cache_spec = pl.BlockSpec((None, None, DIFF_HEADS, seq, LANES), lambda b: (b, 0, 0, 0, 0))
    out_spec = pl.BlockSpec(blk, lambda b: (b // per_group, b % per_group, 0))
    return pl.pallas_call(
        functools.partial(_diff_attn_kernel, n_pieces=1, n_heads=DIFF_HEADS, lam_init=lam_init, emit_kv=True),
        out_shape=(jax.ShapeDtypeStruct((g - n_latent_groups, s, DIFF_WIDTH), BF16), cache_shape, cache_shape),
        grid=(n_seq,),
        in_specs=[
            pl.BlockSpec((4, DIFF_DH), lambda b: (0, 0)),
            pl.BlockSpec((1, LANES), lambda b: (0, 0)),
            pl.BlockSpec(blk, tok_map(0)),
            pl.BlockSpec(blk, tok_map(1)),
            pl.BlockSpec(blk, tok_map(2)),
        ],
        out_specs=(out_spec, cache_spec, cache_spec),
        compiler_params=_cparams(("arbitrary",)),
        name="attn_context",
    )(lam_p, norm_g, proj, proj, proj)


def _attn_latent(proj, cache_k, cache_v, lam_p, norm_g, n_latent_groups, lam_init, tq=256):
    g, s, _ = proj.shape
    past = cache_k.shape[3]
    cache_spec = pl.BlockSpec((None, None, None, past, LANES), lambda b, h, qi: (b, 0, h, 0, 0))
    return pl.pallas_call(
        functools.partial(_diff_attn_kernel, n_pieces=2, n_heads=1, lam_init=lam_init, emit_kv=False),
        out_shape=jax.ShapeDtypeStruct((n_latent_groups, s, DIFF_WIDTH), BF16),
        grid=(n_latent_groups, DIFF_HEADS, s // tq),
        in_specs=[
            pl.BlockSpec((4, DIFF_DH), lambda b, h, qi: (0, 0)),
            pl.BlockSpec((1, LANES), lambda b, h, qi: (0, 0)),
            pl.BlockSpec((None, tq, LANES), lambda b, h, qi: (b, qi, h)),
            cache_spec,
            cache_spec,
            pl.BlockSpec((None, s, LANES), lambda b, h, qi: (b, 0, DIFF_HEADS + h)),
            pl.BlockSpec((None, s, LANES), lambda b, h, qi: (b, 0, 2 * DIFF_HEADS + h)),
        ],
        out_specs=pl.BlockSpec((None, tq, LANES), lambda b, h, qi: (b, qi, h)),
        compiler_params=_cparams(("arbitrary", "arbitrary", "arbitrary")),
        name="attn_latent",
    )(lam_p, norm_g, proj, cache_k, cache_v, proj, proj)


POOL_ROW_BLOCK = 256
POOL_COL_WINDOW = 512
assert (POOL_COL_WINDOW - POOL_ROW_BLOCK) // 2 >= max(POOL_WINDOWS) // 2


def _pool_kernel(p_ref, w_ref, sc_ref, o_ref, band_ref, *, seq):
    @pl.when((pl.program_id(0) == 0) & (pl.program_id(1) == 0))
    def _():
        t = lax.broadcasted_iota(jnp.int32, (seq, seq), 0)
        s_ = lax.broadcasted_iota(jnp.int32, (seq, seq), 1)
        for gi, w in enumerate(POOL_WINDOWS):
            inside = (s_ >= t - w // 2) & (s_ <= t + w // 2 - 1)
            band_ref[gi] = jnp.where(inside, 1.0, 0.0).astype(BF16)

    tcol = lax.broadcasted_iota(jnp.int32, (seq, 1), 0)
    for gi, w in enumerate(POOL_WINDOWS):
        u = p_ref[:, gi * POOL_GC:(gi + 1) * POOL_GC]
        hi, lo = _split_bf16(u)
        rb = min(seq, POOL_ROW_BLOCK)
        cw = min(seq, POOL_COL_WINDOW)
        blocks = []
        for i in range(seq // rb):
            c0 = min(max(i * rb - (cw - rb) // 2, 0), seq - cw)
            band = band_ref[gi, i * rb:(i + 1) * rb, c0:c0 + cw]
            blocks.append(jnp.dot(band, hi[c0:c0 + cw], preferred_element_type=F32)
                          + jnp.dot(band, lo[c0:c0 + cw], preferred_element_type=F32))
        win = blocks[0] if len(blocks) == 1 else jnp.concatenate(blocks, axis=0)
        cnt = (jnp.minimum(tcol + (w // 2 - 1), seq - 1) - jnp.maximum(tcol - w // 2, 0) + 1).astype(F32)
        pooled = win / cnt - u
        mixed = jnp.dot(pooled.astype(BF16), w_ref[gi].astype(BF16), preferred_element_type=F32)
        o_ref[:, gi * POOL_GC:(gi + 1) * POOL_GC] = (
            mixed * sc_ref[:, gi * POOL_GC:(gi + 1) * POOL_GC]).astype(o_ref.dtype)


def _pool(proj, pool_w, pool_scale, g0, n_groups, seq):
    g, s, _ = proj.shape
    col = 3 * DIFF_WIDTH // POOL_WIDTH
    return pl.pallas_call(
        functools.partial(_pool_kernel, seq=seq),
        out_shape=jax.ShapeDtypeStruct((n_groups, s, POOL_WIDTH), BF16),
        grid=(n_groups, s // seq),
        in_specs=[
            pl.BlockSpec((None, seq, POOL_WIDTH), lambda gi, ti: (g0 + gi, ti, col)),
            pl.BlockSpec((POOL_GROUPS, POOL_GC, POOL_GC), lambda gi, ti: (0, 0, 0)),
            pl.BlockSpec((1, POOL_WIDTH), lambda gi, ti: (0, 0)),
        ],
        out_specs=pl.BlockSpec((None, seq, POOL_WIDTH), lambda gi, ti: (gi, ti, 0)),
        scratch_shapes=[pltpu.VMEM((POOL_GROUPS, seq, seq), BF16)],
        compiler_params=_cparams(("arbitrary", "arbitrary")),
        name="pool_%d" % seq,
    )(proj, pool_w, pool_scale.reshape(1, POOL_WIDTH))


def _outproj_kernel(*refs, n_in, gate_row, n_latent_groups):
    a_refs = refs[:2 * n_in]
    w_refs = refs[2 * n_in:3 * n_in]
    xl_ref, xc_ref, mod_ref, g_ref, b_ref, o_ref = refs[3 * n_in:]
    is_latent = pl.program_id(0) < n_latent_groups
    acc = None
    for i, w_ref in enumerate(w_refs):
        a = jnp.where(is_latent, a_refs[2 * i][...], a_refs[2 * i + 1][...])
        t = jnp.dot(a, w_ref[...], preferred_element_type=F32)
        acc = t if acc is None else acc + t
    x = jnp.where(is_latent, xl_ref[...], xc_ref[...])
    z = ALPHA * x + mod_ref[gate_row:gate_row + 1, :] * acc
    o_ref[...] = _layer_norm_rows(z, g_ref[...], b_ref[...])


def _outproj(acts, weights, x_lat, x_ctx, mod, ln_g, ln_b, gate_row, tm=512):
    gl = acts[0][0].shape[0]
    _, s, d = x_lat.shape
    if x_ctx is None:
        g = x_lat.shape[0]
        x_ctx, x_specs = x_lat, _two_stream_specs(tm, d, gl, gl)
    else:
        g = gl + x_ctx.shape[0]
        x_specs = _two_stream_specs(tm, d, gl)
    n_in = len(acts)
    in_specs = []
    flat_acts = []
    for a_lat, a_ctx in acts:
        in_specs += _two_stream_specs(tm, a_lat.shape[-1], gl)
        flat_acts += [a_lat, a_ctx]
    in_specs += [_resident(w.shape, lambda gi, ti: (0, 0)) for w in weights]
    in_specs += x_specs
    in_specs += [
        pl.BlockSpec((None, 6, d), lambda gi, ti: (gi, 0, 0)),
        pl.BlockSpec((1, d), lambda gi, ti: (0, 0)),
        pl.BlockSpec((1, d), lambda gi, ti: (0, 0)),
    ]
    return pl.pallas_call(
        functools.partial(_outproj_kernel, n_in=n_in, gate_row=gate_row, n_latent_groups=gl),
        out_shape=jax.ShapeDtypeStruct((g, s, d), F32),
        grid=(g, s // tm),
        in_specs=in_specs,
        out_specs=pl.BlockSpec((None, tm, d), lambda gi, ti: (gi, ti, 0)),
        compiler_params=_cparams(("arbitrary", "arbitrary")),
        name="outproj",
    )(*flat_acts, *weights, x_lat, x_ctx, mod, ln_g.reshape(1, d), ln_b.reshape(1, d))


def _swiglu_chunks(h_bf, wgu_ref, wd_ref, d_ff):
    acc = None
    for j in range(d_ff // FF_CHUNK):
        lo = j * FF_CHUNK
        gate = jnp.dot(h_bf, wgu_ref[:, lo:lo + FF_CHUNK], preferred_element_type=F32)
        up = jnp.dot(h_bf, wgu_ref[:, d_ff + lo:d_ff + lo + FF_CHUNK], preferred_element_type=F32)
        act = (gate * _sigmoid(gate) * up).astype(BF16)
        t = jnp.dot(act, wd_ref[lo:lo + FF_CHUNK, :], preferred_element_type=F32)
        acc = t if acc is None else acc + t
    return acc


def _ffn_kernel(x_ref, mod_ref, wgu_ref, wd_ref, g_ref, b_ref, *rest, n_cast):
    cast_in, o_ref, cast_out = rest[:n_cast], rest[n_cast], rest[n_cast + 1:]
    _cast_slabs(cast_in, cast_out)
    x = x_ref[...]
    h = (x * (1.0 + mod_ref[4:5, :]) + mod_ref[3:4, :]).astype(BF16)
    acc = _swiglu_chunks(h, wgu_ref, wd_ref, D_FF)
    z = ALPHA * x + mod_ref[5:6, :] * acc
    o_ref[...] = _layer_norm_rows(z, g_ref[...], b_ref[...])


def _ffn(x, mod, wgu_bf, wd_bf, ln_g, ln_b, cast_weights=(), tm=512):
    g, s, d = x.shape
    per = s // tm
    cast_specs = _cast_rider_specs(cast_weights, g * per, per)
    if cast_specs is None:
        out, _ = _ffn(x, mod, wgu_bf, wd_bf, ln_g, ln_b, (), tm)
        return out, tuple(w.astype(BF16) for w in cast_weights)
    outs = pl.pallas_call(
        functools.partial(_ffn_kernel, n_cast=len(cast_weights)),
        out_shape=(jax.ShapeDtypeStruct((g, s, d), F32),) + tuple(
            jax.ShapeDtypeStruct(w.shape, BF16) for w in cast_weights),
        grid=(g, per),
        in_specs=[
            pl.BlockSpec((None, tm, d), lambda gi, ti: (gi, ti, 0)),
            pl.BlockSpec((None, 6, d), lambda gi, ti: (gi, 0, 0)),
            _resident(wgu_bf.shape, lambda gi, ti: (0, 0)),
            _resident(wd_bf.shape, lambda gi, ti: (0, 0)),
            pl.BlockSpec((1, d), lambda gi, ti: (0, 0)),
            pl.BlockSpec((1, d), lambda gi, ti: (0, 0)),
        ] + cast_specs,
        out_specs=(pl.BlockSpec((None, tm, d), lambda gi, ti: (gi, ti, 0)),) + tuple(cast_specs),
        compiler_params=_cparams(("arbitrary", "arbitrary")),
        name="ffn",
    )(x, mod, wgu_bf, wd_bf, ln_g.reshape(1, d), ln_b.reshape(1, d), *cast_weights)
    return outs[0], outs[1:]


MOE_BM = 512
MOE_SB = 256
MOE_TC = 256
MOE_WINDOW = 6
MOE_VMEM_LIMIT = 60 * 1024 * 1024
META_LANES = LANES


def _router_top2(h, rw_ref, rb_ref):
    h_hi, h_lo = _split_bf16(h)
    w_hi, w_lo = _split_bf16(rw_ref[...])
    logits = (jnp.dot(h_hi, w_hi, preferred_element_type=F32)
              + jnp.dot(h_lo, w_hi, preferred_element_type=F32)
              + jnp.dot(h_hi, w_lo, preferred_element_type=F32)) + rb_ref[...]
    lane = lax.broadcasted_iota(jnp.int32, logits.shape, 1).astype(F32)
    neg = -jnp.inf
    logits = jnp.where(lane < N_EXPERTS, logits, neg)
    m1 = jnp.max(logits, axis=-1, keepdims=True)
    i1 = jnp.min(jnp.where(logits == m1, lane, float(LANES)), axis=-1, keepdims=True)
    rest = jnp.where(lane == i1, neg, logits)
    m2 = jnp.max(rest, axis=-1, keepdims=True)
    i2 = jnp.min(jnp.where(rest == m2, lane, float(LANES)), axis=-1, keepdims=True)
    e2 = jnp.exp(m2 - m1)
    return lane, i1, i2, 1.0 / (1.0 + e2), e2 / (1.0 + e2)


def _route_kernel(x_ref, mod_ref, rw_ref, rb_ref, h_ref, wk_ref, srank_ref, cnt_ref, tri_ref, run_ref):
    tm, d = x_ref.shape

    @pl.when(pl.program_id(0) == 0)
    def _():
        r = lax.broadcasted_iota(jnp.int32, (tm, tm), 0)
        c = lax.broadcasted_iota(jnp.int32, (tm, tm), 1)
        tri_ref[...] = jnp.where(c <= r, 1.0, 0.0).astype(BF16)
        run_ref[...] = jnp.zeros_like(run_ref)

    h = x_ref[...] * (1.0 + mod_ref[4:5, :]) + mod_ref[3:4, :]
    h_ref[:, 0:d] = h.astype(BF16)
    lane, i1, i2, w1, w2 = _router_top2(h, rw_ref, rb_ref)
    first_is_low = i1 < i2
    e_hi = jnp.where(first_is_low, i2, i1)
    wk_ref[...] = jnp.where(lane == 0.0, jnp.where(first_is_low, w1, w2),
                            jnp.where(lane == 1.0, jnp.where(first_is_low, w2, w1), 0.0))
    tok = (pl.program_id(0) * tm + lax.broadcasted_iota(jnp.int32, (tm, META_LANES), 0))
    meta = jnp.where(lane == 0.0, (tok // 128).astype(F32),
                     jnp.where(lane == 1.0, (tok % 128).astype(F32),
                               jnp.where(lane == 2.0, e_hi, jnp.where(lane == 3.0, 1.0, 0.0))))
    h_ref[:, d:d + META_LANES] = meta.astype(BF16)

    member = (lane == i1) | (lane == i2)
    mem = jnp.where(member, 1.0, 0.0)
    rank = jnp.dot(tri_ref[...], mem.astype(BF16), preferred_element_type=F32) + run_ref[...]
    srank_ref[...] = jnp.where(member, rank, -rank).T[0:SUBLANES, :]
    run_ref[...] = rank[tm - 1:tm, :]
    cnt_ref[...] = rank[tm - 1:tm, :]


def _route(x, mod, router_w, router_b, tm=512):
    g, s, d = x.shape
    n = g * s
    assert n <= 128 * 256
    assert n >= MOE_WINDOW * MOE_TC
    per = s // tm
    rw = jnp.pad(router_w, ((0, 0), (0, LANES - N_EXPERTS)))
    rb = jnp.pad(router_b, (0, LANES - N_EXPERTS)).reshape(1, LANES)
    return pl.pallas_call(
        _route_kernel,
        out_shape=(jax.ShapeDtypeStruct((n, d + META_LANES), BF16), jax.ShapeDtypeStruct((n, LANES), F32),
                   jax.ShapeDtypeStruct((SUBLANES, n), F32), jax.ShapeDtypeStruct((1, LANES), F32)),
        grid=(n // tm,),
        in_specs=[
            pl.BlockSpec((None, tm, d), lambda i: (i // per, i % per, 0)),
            pl.BlockSpec((None, 6, d), lambda i: (i // per, 0, 0)),
            pl.BlockSpec((d, LANES), lambda i: (0, 0)),
            pl.BlockSpec((1, LANES), lambda i: (0, 0)),
        ],
        out_specs=(pl.BlockSpec((tm, d + META_LANES), lambda i: (i, 0)), pl.BlockSpec((tm, LANES), lambda i: (i, 0)),
                   pl.BlockSpec((SUBLANES, tm), lambda i: (0, i)), pl.BlockSpec((1, LANES), lambda i: (0, 0))),
        scratch_shapes=[pltpu.VMEM((tm, tm), BF16), pltpu.VMEM((1, LANES), F32)],
        compiler_params=_cparams(("arbitrary",)),
        name="moe_route",
    )(x, mod, rw, rb)


def _moe_plan(srank, counts, n_blocks):
    e_n = N_EXPERTS
    i32 = jnp.int32
    cnt = counts[0, :e_n].astype(i32)
    nb = (cnt + MOE_BM - 1) // MOE_BM
    nb_incl = jnp.cumsum(nb)
    gstart = nb_incl - nb
    n_used = nb_incl[-1]
    sr = srank[:e_n]
    rank = jnp.abs(sr).astype(i32)
    pos_row = jnp.where(sr > 0, rank - 1 + MOE_BM * gstart[:, None], -1)
    chunk_end = rank[:, MOE_TC - 1::MOE_TC]

    rc = jnp.minimum(jnp.arange(n_blocks, dtype=i32), n_used - 1)
    e_r = jnp.minimum(jnp.sum(nb_incl[None, :] <= rc[:, None], axis=1, dtype=i32), e_n - 1)

    sb_per = MOE_BM // MOE_SB
    q = jnp.arange(n_blocks * sb_per, dtype=i32)
    r_q = jnp.minimum(q // sb_per, n_used - 1)
    e_q = e_r[r_q]
    first = ((r_q - gstart[e_q]) * sb_per + q % sb_per) * MOE_SB
    used = (q // sb_per < n_used) & (first < cnt[e_q])
    hi = jnp.minimum(first + MOE_SB, cnt[e_q])
    ends_q = chunk_end[e_q]
    jlo = jnp.where(used, jnp.sum(ends_q < (first + 1)[:, None], axis=1, dtype=i32), 0)
    jhi = jnp.where(used, jnp.sum(ends_q < hi[:, None], axis=1, dtype=i32), -1)
    return pos_row, e_r, jlo, jhi, n_used.reshape(1)


def _experts_kernel(be_ref, jlo_ref, jhi_ref, nused_ref, h_ref, pos_ref, wgu_ref, wd_ref, y_ref, meta_ref, xg_ref):
    r = pl.program_id(0)
    d = y_ref.shape[1]

    @pl.when(r < nused_ref[0])
    def _():
        e = be_ref[r]
        sb_per = MOE_BM // MOE_SB
        n_chunks = h_ref.shape[0] // MOE_TC
        for u in range(sb_per):
            sb_rows = slice(u * MOE_SB, (u + 1) * MOE_SB)
            slot = r * MOE_BM + u * MOE_SB + lax.broadcasted_iota(jnp.int32, (MOE_SB, MOE_TC), 0)

            def onehot(j, slot=slot):
                return jnp.where(pos_ref[e, pl.ds(j, 1), :] == slot, 1.0, 0.0).astype(BF16)

            j0 = jnp.minimum(jlo_ref[r * sb_per + u], n_chunks - MOE_WINDOW)
            window = jnp.concatenate([onehot(j0 + j) for j in range(MOE_WINDOW)], axis=-1)
            rows = h_ref[pl.ds(pl.multiple_of(j0 * MOE_TC, MOE_TC), MOE_WINDOW * MOE_TC), :]
            xg_ref[sb_rows, :] = jnp.dot(window, rows, preferred_element_type=F32)

            def chunk(j, carry, sb_rows=sb_rows, onehot=onehot):
                rows = h_ref[pl.ds(pl.multiple_of(j * MOE_TC, MOE_TC), MOE_TC), :]
                xg_ref[sb_rows, :] += jnp.dot(onehot(j), rows, preferred_element_type=F32)
                return carry

            lax.fori_loop(j0 + MOE_WINDOW, jhi_ref[r * sb_per + u] + 1, chunk, 0)
        meta_ref[...] = xg_ref[:, d:d + META_LANES].T[0:SUBLANES, :]
        y_ref[...] = _swiglu_chunks(xg_ref[:, 0:d].astype(BF16), wgu_ref, wd_ref, D_FF_EXPERT)

    @pl.when(r >= nused_ref[0])
    def _():
        y_ref[...] = jnp.zeros_like(y_ref)
        meta_ref[...] = jnp.zeros_like(meta_ref)


def _experts(h, pos_row, plan, wgu_bf, wd_bf, n_blocks):
    n, dx = h.shape
    d = dx - META_LANES
    e_r, jlo, jhi, n_used = plan
    grid_spec = pltpu.PrefetchScalarGridSpec(
        num_scalar_prefetch=4,
        grid=(n_blocks,),
        in_specs=[
            _resident((n, dx), lambda r, be, lo, hi, nu: (0, 0)),
            _resident(pos_row.shape, lambda r, be, lo, hi, nu: (0, 0, 0)),
            pl.BlockSpec((None, d, 2 * D_FF_EXPERT), lambda r, be, lo, hi, nu: (be[r], 0, 0),
                         pipeline_mode=pl.Buffered(1)),
            pl.BlockSpec((None, D_FF_EXPERT, d), lambda r, be, lo, hi, nu: (be[r], 0, 0),
                         pipeline_mode=pl.Buffered(1)),
        ],
        out_specs=(pl.BlockSpec((MOE_BM, d), lambda r, be, lo, hi, nu: (r, 0)),
                   pl.BlockSpec((None, SUBLANES, MOE_BM), lambda r, be, lo, hi, nu: (r, 0, 0))),
        scratch_shapes=[pltpu.VMEM((MOE_BM, dx), F32)],
    )
    return pl.pallas_call(
        _experts_kernel,
        out_shape=(jax.ShapeDtypeStruct((n_blocks * MOE_BM, d), F32),
                   jax.ShapeDtypeStruct((n_blocks, SUBLANES, MOE_BM), F32)),
        grid_spec=grid_spec,
        compiler_params=_cparams(("arbitrary",), MOE_VMEM_LIMIT),
        name="moe_experts",
    )(e_r, jlo, jhi, n_used, h, pos_row, wgu_bf, wd_bf)


def _scatter_kernel(dest_ref, y_ref, o_ref, ybuf_ref, sem):
    r = pl.program_id(0)
    n_steps = pl.num_programs(0)
    buf = r % 2

    def wait_block(b):
        pltpu.make_async_copy(ybuf_ref.at[b], o_ref.at[pl.ds(0, MOE_BM), :], sem.at[b]).wait()

    @pl.when(r >= 2)
    def _():
        wait_block(buf)

    ybuf_ref[buf] = y_ref[...]

    for i in range(MOE_BM):
        pltpu.make_async_copy(ybuf_ref.at[buf, pl.ds(i, 1), :], o_ref.at[pl.ds(dest_ref[0, i], 1), :],
                              sem.at[buf]).start()

    @pl.when(r == n_steps - 1)
    def _():
        wait_block(1 - buf)
        wait_block(buf)


def _scatter_rows(y, dest, n_rows_out):
    n_slots, d = y.shape
    n_blocks = n_slots // MOE_BM
    assert n_blocks >= 2
    return pl.pallas_call(
        _scatter_kernel,
        out_shape=jax.ShapeDtypeStruct((n_rows_out, d), F32),
        grid=(n_blocks,),
        in_specs=[
            pl.BlockSpec((None, 1, MOE_BM), lambda r: (r, 0, 0), memory_space=pltpu.SMEM),
            pl.BlockSpec((MOE_BM, d), lambda r: (r, 0)),
        ],
        out_specs=pl.BlockSpec(memory_space=pl.ANY),
        scratch_shapes=[pltpu.VMEM((2, MOE_BM, d), F32), pltpu.SemaphoreType.DMA((2,))],
        compiler_params=_cparams(("arbitrary",)),
        name="moe_scatter",
    )(dest.reshape(n_blocks, 1, MOE_BM), y)


def _finish_kernel(a_ref, b_ref, wk_ref, x_ref, mod_ref, g_ref, bb_ref, ol_ref, oc_ref, *, n_latent_tiles):
    lane = lax.broadcasted_iota(jnp.int32, wk_ref.shape, 1)
    wk = wk_ref[...]
    w0 = jnp.sum(jnp.where(lane == 0, wk, 0.0), axis=-1, keepdims=True)
    w1 = jnp.sum(jnp.where(lane == 1, wk, 0.0), axis=-1, keepdims=True)
    acc = w0 * a_ref[...] + w1 * b_ref[...]
    z = ALPHA * x_ref[...] + mod_ref[5:6, :] * acc
    out = _layer_norm_rows(z, g_ref[...], bb_ref[...])
    is_latent = pl.program_id(0) < n_latent_tiles

    @pl.when(is_latent)
    def _():
        ol_ref[...] = out

    @pl.when(jnp.logical_not(is_latent))
    def _():
        oc_ref[...] = out


def _finish(rows, wk, x, mod, ln_g, ln_b, n_latent_groups, tm=512):
    g, s, d = x.shape
    n = g * s
    per = s // tm
    nt = n // tm
    nlt = n_latent_groups * per

    def lat_map(i):
        j = jnp.minimum(i, nlt - 1)
        return (j // per, j % per, 0)

    def ctx_map(i):
        j = jnp.maximum(i - nlt, 0)
        return (j // per, j % per, 0)

    return pl.pallas_call(
        functools.partial(_finish_kernel, n_latent_tiles=nlt),
        out_shape=(jax.ShapeDtypeStruct((n_latent_groups, s, d), F32),
                   jax.ShapeDtypeStruct((g - n_latent_groups, s, d), F32)),
        grid=(nt,),
        in_specs=[
            pl.BlockSpec((tm, d), lambda i: (i, 0)),
            pl.BlockSpec((tm, d), lambda i: (nt + i, 0)),
            pl.BlockSpec((tm, LANES), lambda i: (i, 0)),
            pl.BlockSpec((None, tm, d), lambda i: (i // per, i % per, 0)),
            pl.BlockSpec((None, 6, d), lambda i: (i // per, 0, 0)),
            pl.BlockSpec((1, d), lambda i: (0, 0)),
            pl.BlockSpec((1, d), lambda i: (0, 0)),
        ],
        out_specs=(pl.BlockSpec((None, tm, d), lat_map), pl.BlockSpec((None, tm, d), ctx_map)),
        compiler_params=_cparams(("arbitrary",)),
        name="moe_finish",
    )(rows, rows, wk, x, mod, ln_g.reshape(1, d), ln_b.reshape(1, d))


def _moe(x, mod, router_w, router_b, wgu_bf, wd_bf, ln_g, ln_b, n_latent_groups):
    g, s, d = x.shape
    n = g * s
    n_blocks = 2 * n // MOE_BM + N_EXPERTS
    n_slots = n_blocks * MOE_BM
    h, wk, srank, counts = _route(x, mod, router_w, router_b)
    pos_row, e_r, jlo, jhi, n_used = _moe_plan(srank, counts, n_blocks)
    pos_row = pos_row.reshape(N_EXPERTS, n // MOE_TC, MOE_TC)
    y, meta = _experts(h, pos_row, (e_r, jlo, jhi, n_used), wgu_bf, wd_bf, n_blocks)
    tok = (meta[:, 0] * 128.0 + meta[:, 1]).astype(jnp.int32)
    choice = (meta[:, 2] == e_r[:, None].astype(F32)).astype(jnp.int32)
    unused = meta[:, 3] < 0.5
    spare = (2 * n + jnp.cumsum(unused.reshape(-1).astype(jnp.int32)) - 1).reshape(n_blocks, MOE_BM)
    dest = jnp.where(unused, spare, choice * n + tok)
    rows = _scatter_rows(y, dest, n_slots)
    return _finish(rows, wk, x, mod, ln_g, ln_b, n_latent_groups)


def _inproj_c_kernel(x_ref, mod_ref, w_ref, wg_ref, bg_ref, o_ref, og_ref):
    h = (x_ref[...] * (1.0 + mod_ref[1:2, :]) + mod_ref[0:1, :]).astype(BF16)
    o_ref[...] = jnp.dot(h, w_ref[:, 0:o_ref.shape[1]], preferred_element_type=F32)
    og_ref[...] = jnp.dot(h, wg_ref[...], preferred_element_type=F32) + bg_ref[...]


def _inproj_c(x, mod, w_bf, n, wg_bf, bg, tm=512):
    g, s, d = x.shape
    ng = wg_bf.shape[1]
    return pl.pallas_call(
        _inproj_c_kernel,
        out_shape=(jax.ShapeDtypeStruct((g, s, n), F32), jax.ShapeDtypeStruct((g, s, ng), F32)),
        grid=(g, s // tm),
        in_specs=[
            pl.BlockSpec((None, tm, d), lambda gi, ti: (gi, ti, 0)),
            pl.BlockSpec((None, 6, d), lambda gi, ti: (gi, 0, 0)),
            _resident(w_bf.shape, lambda gi, ti: (0, 0)),
            _resident((d, ng), lambda gi, ti: (0, 0)),
            pl.BlockSpec((1, ng), lambda gi, ti: (0, 0)),
        ],
        out_specs=(pl.BlockSpec((None, tm, n), lambda gi, ti: (gi, ti, 0)),
                   pl.BlockSpec((None, tm, ng), lambda gi, ti: (gi, ti, 0))),
        compiler_params=_cparams(("arbitrary", "arbitrary")),
        name="inproj_c",
    )(x, mod, w_bf, wg_bf, bg)


def _log_sigmoid(x):
    return jnp.minimum(x, 0.0) - jnp.log(1.0 + jnp.exp(-jnp.abs(x)))


MLSTM_L = 128


def _split3_bf16(x):
    hi = x.astype(BF16)
    r1 = x - hi.astype(F32)
    mid = r1.astype(BF16)
    lo = (r1 - mid.astype(F32)).astype(BF16)
    return hi, mid, lo


def _mlstm_kernel(*refs, seq, hg, has_init, emit_state):
    q_ref, k_ref, v_ref, o_ref, gi_ref, gf_ref, hgain_ref = refs[:7]
    pos = 7
    if has_init:
        c0_ref, n0_ref, m0_ref = refs[pos:pos + 3]
        pos += 3
    out_ref = refs[pos]
    pos += 1
    if emit_state:
        co_ref, no_ref, mo_ref = refs[pos:pos + 3]
        pos += 3
    cext_ref, hf_ref, hb_ref, b_ref, g_ref, gmax_ref, mt_ref, wi_ref, en_ref, ws_ref, gt_ref, wc_ref = refs[pos:]

    L = MLSTM_L
    dh = MLSTM_DH
    nh = MLSTM_HEADS
    nc = seq // L
    head0 = pl.program_id(1) * hg
    neg = -jnp.inf

    lane = lax.broadcasted_iota(jnp.int32, (L, LANES), 1)
    lane1 = lax.broadcasted_iota(jnp.int32, (1, LANES), 1)
    row = lax.broadcasted_iota(jnp.int32, (L, L), 0)
    col = lax.broadcasted_iota(jnp.int32, (L, L), 1)
    lower = col <= row
    upper = col >= row
    tri_l = jnp.where(lower, 1.0, 0.0).astype(BF16)
    tri_u = jnp.where(upper, 1.0, 0.0).astype(BF16)
    fwd_lane = lane < nh
    fwd_lane1 = lane1 < nh
    trow = lax.broadcasted_iota(jnp.int32, (L, LANES), 0)

    btot, glast = [], []
    for c in range(nc):
        rows = slice(c * L, (c + 1) * L)
        f = _log_sigmoid(gf_ref[rows, :])
        parts = _split3_bf16(f)
        pre = sum(jnp.dot(tri_l, p, preferred_element_type=F32) for p in parts)
        suf = sum(jnp.dot(tri_u, p, preferred_element_type=F32) for p in parts)
        b = jnp.where(fwd_lane, pre, suf)
        g = gi_ref[rows, :] - b
        gp, gs = g, g
        k = 1
        while k < L:
            gp = jnp.where(trow >= k, jnp.maximum(gp, pltpu.roll(gp, k, 0)), gp)
            gs = jnp.where(trow < L - k, jnp.maximum(gs, pltpu.roll(gs, L - k, 0)), gs)
            k *= 2
        gmax = jnp.where(fwd_lane, gp, gs)
        b_ref[rows, :] = b
        g_ref[rows, :] = g
        gmax_ref[rows, :] = gmax
        btot.append(jnp.where(fwd_lane1, b[L - 1:L, :], b[0:1, :]))
        glast.append(jnp.where(fwd_lane1, gmax[L - 1:L, :], gmax[0:1, :]))

    m_init = m0_ref[...] if has_init else jnp.zeros((1, LANES), F32)
    mf, mb = m_init, m_init
    ms_f, mn_f, ms_b, mn_b = [None] * nc, [None] * nc, [None] * nc, [None] * nc
    for c in range(nc):
        ms_f[c] = mf
        mf = btot[c] + jnp.maximum(mf, glast[c])
        mn_f[c] = mf
        cb = nc - 1 - c
        ms_b[cb] = mb
        mb = btot[cb] + jnp.maximum(mb, glast[cb])
        mn_b[cb] = mb
    m_final = jnp.where(fwd_lane1, mf, mb)

    for c in range(nc):
        rows = slice(c * L, (c + 1) * L)
        m_start = jnp.where(fwd_lane1, ms_f[c], ms_b[c])
        m_next = jnp.where(fwd_lane1, mn_f[c], mn_b[c])
        g = g_ref[rows, :]
        mt = jnp.maximum(m_start, gmax_ref[rows, :])
        mt_ref[rows, :] = mt
        wi_ref[rows, :] = jnp.exp(m_start - mt)
        en_ref[rows, :] = jnp.exp(-(b_ref[rows, :] + mt))
        ws_ref[rows, :] = jnp.exp(btot[c] + g - m_next)
        gt_ref[c] = g.T
        wc_ref[c:c + 1, :] = jnp.exp(btot[c] + m_start - m_next)

    for d in range(2):
        for hh in range(hg):
            idx = d * hg + hh
            if has_init:
                cext_ref[idx, :, 0:dh] = c0_ref[d, hh]
                n0_tile = jnp.where(lax.broadcasted_iota(jnp.int32, (dh, dh), 0) == 0, n0_ref[d, hh], 0.0)
                cext_ref[idx, :, dh:2 * dh] = n0_tile.T
            else:
                cext_ref[idx] = jnp.zeros((dh, 2 * dh), F32)

    ones_col = jnp.where(lane == 0, 1.0, 0.0).astype(BF16)
    nt = (((1,), (1,)), ((), ()))
    tn = (((0,), (0,)), ((), ()))

    def column(x, j):
        return jnp.sum(jnp.where(lane == j, x, 0.0), axis=-1, keepdims=True)

    def one_direction(d, hh, c, s_qk, q_bf, k_s, v_ext, v_bf):
        idx = d * hg + hh
        j = d * nh + head0 + hh
        rows = pl.ds(pl.multiple_of(c * L, L), L)
        mt = column(mt_ref[rows, :], j)
        wi = column(wi_ref[rows, :], j)
        en = column(en_ref[rows, :], j)
        ws = column(ws_ref[rows, :], j)
        g_r = gt_ref[c, pl.ds(j, 1), :]
        w_c = jnp.sum(jnp.where(lane1 == j, wc_ref[pl.ds(c, 1), :], 0.0), axis=-1, keepdims=True)
        causal = lower if d == 0 else upper
        p = s_qk * jnp.exp(jnp.where(causal, g_r - mt, neg))
        qc = jnp.dot(q_bf, cext_ref[idx].astype(BF16), preferred_element_type=F32)
        num = wi * qc[:, 0:dh] + jnp.dot(p.astype(BF16), v_bf, preferred_element_type=F32)
        den = wi * qc[:, dh:dh + 1] + jnp.sum(p, axis=-1, keepdims=True)
        h = num / jnp.maximum(jnp.abs(den), en)
        upd = lax.dot_general((ws * k_s).astype(BF16), v_ext, tn, preferred_element_type=F32)
        cext_ref[idx] = w_c * cext_ref[idx] + upd
        return h

    def load_chunk(hh, c):
        sl = (pl.ds(pl.multiple_of(c * L, L), L), slice(hh * dh, (hh + 1) * dh))
        q_bf = q_ref[sl].astype(BF16)
        k_s = k_ref[sl] * (dh ** -0.5)
        v_bf = v_ref[sl].astype(BF16)
        v_ext = jnp.concatenate([v_bf, ones_col], axis=-1)
        s_qk = lax.dot_general(q_bf, k_s.astype(BF16), nt, preferred_element_type=F32)
        return s_qk, q_bf, k_s, v_ext, v_bf

    def step(c, carry):
        cb = nc - 1 - c
        for hh in range(hg):
            h = one_direction(0, hh, c, *load_chunk(hh, c))
            hf_ref[pl.ds(pl.multiple_of(c * L, L), L), hh * dh:(hh + 1) * dh] = h
        for hh in range(hg):
            h = one_direction(1, hh, cb, *load_chunk(hh, cb))
            hb_ref[pl.ds(pl.multiple_of(cb * L, L), L), hh * dh:(hh + 1) * dh] = h
        return carry

    lax.fori_loop(0, nc, step, 0)

    for hh in range(hg):
        cs = slice(hh * dh, (hh + 1) * dh)
        hs = hf_ref[:, cs] + hb_ref[:, cs]
        mu = jnp.mean(hs, axis=-1, keepdims=True)
        hc = hs - mu
        var = jnp.mean(hc * hc, axis=-1, keepdims=True)
        hn = hc * lax.rsqrt(var + LN_EPS) * hgain_ref[:, cs]
        out_ref[:, cs] = (_sigmoid(o_ref[:, cs]) * hn).astype(out_ref.dtype)

    if emit_state:
        for d in range(2):
            for hh in range(hg):
                idx = d * hg + hh
                co_ref[d, hh] = cext_ref[idx, :, 0:dh]
                no_ref[d, hh] = cext_ref[idx, :, dh:2 * dh].T[0:1, :]
        mo_ref[...] = m_final


def _mlstm(proj, gates, head_g, g0, n_seq, seq, hg, init=None, emit_state=False):
    g, s, _ = proj.shape
    per_group = s // seq
    n_hg = MLSTM_HEADS // hg
    w = hg * MLSTM_DH
    nc = seq // MLSTM_L
    n_blocks = D_MODEL // w

    def tok_map(colblock):
        return lambda b, hi: (g0 + b // per_group, b % per_group, colblock * n_blocks + hi)

    def gate_map(half):
        return lambda b, hi: (g0 + b // per_group, b % per_group, half)

    args = [proj, proj, proj, proj, gates, gates, head_g.reshape(1, D_MODEL)]
    in_specs = [
        pl.BlockSpec((None, seq, w), tok_map(0)),
        pl.BlockSpec((None, seq, w), tok_map(1)),
        pl.BlockSpec((None, seq, w), tok_map(2)),
        pl.BlockSpec((None, seq, w), tok_map(3)),
        pl.BlockSpec((None, seq, LANES), gate_map(0)),
        pl.BlockSpec((None, seq, LANES), gate_map(1)),
        pl.BlockSpec((1, w), lambda b, hi: (0, hi)),
    ]
    if init is not None:
        c0, n0, m0 = init
        m0_lanes = jnp.pad(m0.reshape(n_seq, 1, 2 * MLSTM_HEADS), ((0, 0), (0, 0), (0, LANES - 2 * MLSTM_HEADS)))
        args += [c0, n0.reshape(n0.shape[:-1] + (1, MLSTM_DH)), m0_lanes]
        in_specs += [
            pl.BlockSpec((None, None, 2, hg, MLSTM_DH, MLSTM_DH), lambda b, hi: (b, 0, 0, hi, 0, 0)),
            pl.BlockSpec((None, 2, hg, 1, MLSTM_DH), lambda b, hi: (b, 0, hi, 0, 0)),
            pl.BlockSpec((None, 1, LANES), lambda b, hi: (b, 0, 0)),
        ]

    out_shape = [jax.ShapeDtypeStruct((n_seq // per_group, s, D_MODEL), BF16)]
    out_specs = [pl.BlockSpec((None, seq, w), lambda b, hi: (b // per_group, b % per_group, hi))]
    if emit_state:
        out_shape += [
            jax.ShapeDtypeStruct((n_seq, 2, MLSTM_HEADS, MLSTM_DH, MLSTM_DH), F32),
            jax.ShapeDtypeStruct((n_seq, 2, MLSTM_HEADS, 1, MLSTM_DH), F32),
            jax.ShapeDtypeStruct((n_seq, n_hg, 1, LANES), F32),
        ]
        out_specs += [
            pl.BlockSpec((None, 2, hg, MLSTM_DH, MLSTM_DH), lambda b, hi: (b, 0, hi, 0, 0)),
            pl.BlockSpec((None, 2, hg, 1, MLSTM_DH), lambda b, hi: (b, 0, hi, 0, 0)),
            pl.BlockSpec((None, None, 1, LANES), lambda b, hi: (b, hi, 0, 0)),
        ]

    tok_scratch = pltpu.VMEM((seq, LANES), F32)
    return pl.pallas_call(
        functools.partial(_mlstm_kernel, seq=seq, hg=hg, has_init=init is not None, emit_state=emit_state),
        out_shape=tuple(out_shape),
        grid=(n_seq, n_hg),
        in_specs=in_specs,
        out_specs=tuple(out_specs),
        scratch_shapes=[
            pltpu.VMEM((2 * hg, MLSTM_DH, 2 * MLSTM_DH), F32),
            pltpu.VMEM((seq, w), F32),
            pltpu.VMEM((seq, w), F32),
        ] + [tok_scratch] * 7 + [
            pltpu.VMEM((nc, LANES, MLSTM_L), F32),
            pltpu.VMEM((max(nc, 8), LANES), F32),
        ],
        compiler_params=_cparams(("arbitrary", "arbitrary")),
        name="mlstm_%d" % seq,
    )(*args)


def kernel(x_prompt, x_sample, c, cache_k, cache_v, state_C, state_n, state_m, c_ctx, ada_w, ada_b, ln_g, ln_b, w_in_a, diff_lambda, diff_norm_g, pool_w, pool_scale, w_out_a, ffn_w_gu, ffn_w_down, w_in_c, b_gates_c, mlstm_norm_g, w_out_c, router_w, router_b, moe_w_gu, moe_w_down):
    n_ctx, seq_ctx, d = x_prompt.shape
    n_lat, seq_lat, _ = x_sample.shape
    assert d == D_MODEL and (n_ctx * seq_ctx) % seq_lat == 0 and seq_lat % seq_ctx == 0
    gl = n_lat
    gc = n_ctx * seq_ctx // seq_lat
    s = seq_lat

    x_ctx = x_prompt.reshape(gc, s, d)
    cvec = jnp.concatenate([c, jnp.broadcast_to(c_ctx[None, :], (gc, d))], axis=0)
    mod_all = _modulation(cvec, ada_w, ada_b).reshape(DEPTH, gl + gc, 6, d)

    mod = mod_all[0]
    lam_init = 0.8 - 0.6 * math.exp(-0.3 * 0)
    cos_t, sin_t = _rope_tables(s)
    proj, (ffn_wgu_bf, ffn_wd_bf) = _inproj_a(x_sample, x_ctx, mod, w_in_a[0].astype(BF16), cos_t, sin_t,
                                              (ffn_w_gu, ffn_w_down))
    norm_g = diff_norm_g[0].reshape(1, LANES)
    attn_c, new_k, new_v = _attn_context(proj, diff_lambda[0], norm_g, gl, n_ctx, seq_ctx, lam_init)
    attn_l = _attn_latent(proj, cache_k, cache_v, diff_lambda[0], norm_g, gl, lam_init)
    pool_c = _pool(proj, pool_w[0], pool_scale[0], gl, gc, seq_ctx)
    pool_l = _pool(proj, pool_w[0], pool_scale[0], 0, gl, seq_lat)
    w_out = w_out_a[0].astype(BF16)
    x = _outproj([(attn_l, attn_c), (pool_l, pool_c)], [w_out[:DIFF_WIDTH], w_out[DIFF_WIDTH:]],
                 x_sample, x_ctx, mod, ln_g[0, 0], ln_b[0, 0], 2)
    x, (moe_wgu_bf, moe_wd_bf, w_in_c_bf) = _ffn(x, mod, ffn_wgu_bf[0], ffn_wd_bf[0], ln_g[0, 1], ln_b[0, 1],
                                                  (moe_w_gu[0], moe_w_down[0], w_in_c))

    mod = mod_all[1]
    n_main = 4 * D_MODEL
    nh = MLSTM_HEADS
    wg4 = w_in_c[0][:, n_main:].reshape(d, N_GATES, nh)
    bg4 = b_gates_c[0].reshape(1, N_GATES, nh)
    lane_pad = ((0, 0), (0, LANES - 2 * nh))

    def gate_lanes(a):
        return jnp.concatenate([jnp.pad(jnp.concatenate([a[:, 0], a[:, 2]], axis=-1), lane_pad),
                                jnp.pad(jnp.concatenate([a[:, 1], a[:, 3]], axis=-1), lane_pad)], axis=-1)

    proj, gates = _inproj_c(x, mod, w_in_c_bf[0], n_main, gate_lanes(wg4).astype(BF16), gate_lanes(bg4))
    mix_c, new_c, new_n, new_m = _mlstm(proj, gates, mlstm_norm_g[0], gl, n_ctx, seq_ctx, MLSTM_HEADS,
                                        emit_state=True)
    (mix_l,) = _mlstm(proj, gates, mlstm_norm_g[0], 0, n_lat, seq_lat, 4,
                      init=(state_C, state_n[:, 0], state_m[:, 0]))
    x = _outproj([(mix_l, mix_c)], [w_out_c[0].astype(BF16)], x, None, mod, ln_g[1, 0], ln_b[1, 0], 2)
    y_sample, y_ctx = _moe(x, mod, router_w[0], router_b[0], moe_wgu_bf, moe_wd_bf,
                           ln_g[1, 1], ln_b[1, 1], gl)
    y_prompt = y_ctx.reshape(n_ctx, seq_ctx, d)
    new_m = new_m[:, 0, 0, :2 * MLSTM_HEADS].reshape(n_ctx, 2, MLSTM_HEADS)
    return (y_prompt, y_sample, new_k, new_v, new_c[:, None], new_n[..., 0, :][:, None], new_m[:, None])
```

```python
import functools
import math

import jax
import jax.numpy as jnp
from jax import lax
from jax.experimental import pallas as pl
from jax.experimental.pallas import tpu as pltpu

F32 = jnp.float32
BF16 = jnp.bfloat16

D_MODEL = 1024
GRID_W = 64
ROPE_BASE = 10000.0
DIFF_HEADS = 4
DIFF_DH = 64
DIFF_WIDTH = DIFF_HEADS * 2 * DIFF_DH
POOL_GROUPS = 4
POOL_GC = 128
POOL_WIDTH = POOL_GROUPS * POOL_GC
POOL_WINDOWS = (2, 4, 8, 16)
MLSTM_HEADS = 8
MLSTM_DH = 128
N_GATES = 4
D_FF = 2816
N_EXPERTS = 8
D_FF_EXPERT = 1792
LN_EPS = 1e-5
DEPTH = 2
ALPHA = (2.0 * DEPTH) ** 0.25

LANES = 128
SUBLANES = 8
FF_CHUNK = 256
VMEM_LIMIT = 56 * 1024 * 1024
HIGH_VMEM_LIMIT = 60 * 1024 * 1024


def _cparams(sem, vmem=VMEM_LIMIT):
    return pltpu.CompilerParams(dimension_semantics=sem, vmem_limit_bytes=vmem)


def _resident(shape, index_map):
    return pl.BlockSpec(shape, index_map, pipeline_mode=pl.Buffered(1))


def _layer_norm_rows(z, g, b):
    mu = jnp.mean(z, axis=-1, keepdims=True)
    zc = z - mu
    var = jnp.mean(zc * zc, axis=-1, keepdims=True)
    return zc * lax.rsqrt(var + LN_EPS) * g + b


def _sigmoid(x):
    return 1.0 / (1.0 + jnp.exp(-x))


def _split_bf16(x):
    hi = x.astype(BF16)
    lo = (x - hi.astype(F32)).astype(BF16)
    return hi, lo


def _cast_rider_specs(weights, n_steps, per):
    specs = []
    for w in weights:
        e, rows, cols = w.shape
        per_e = next((k for k in range(n_steps // e, 0, -1)
                      if n_steps % (e * k) == 0 and rows % (2 * SUBLANES * k) == 0), None)
        if per_e is None:
            return None
        hold = n_steps // (e * per_e)

        def slab_map(gi, ti, per_e=per_e, hold=hold):
            slab = (gi * per + ti) // hold
            return (slab // per_e, slab % per_e, 0)

        specs.append(pl.BlockSpec((None, rows // per_e, cols), slab_map))
    return specs


def _cast_slabs(cast_in, cast_out):
    for src, dst in zip(cast_in, cast_out):
        dst[...] = src[...].astype(BF16)


def _mod_kernel(c_ref, w_ref, b_ref, o_ref):
    c = c_ref[...]
    h = (c * _sigmoid(c)).astype(BF16)
    o_ref[...] = jnp.dot(h, w_ref[...].astype(BF16), preferred_element_type=F32) + b_ref[...]


def _modulation(cvec, ada_w, ada_b):
    depth, d, n = ada_w.shape
    g = cvec.shape[0]
    tn = 1536
    return pl.pallas_call(
        _mod_kernel,
        out_shape=jax.ShapeDtypeStruct((depth, g, n), F32),
        grid=(depth, n // tn),
        in_specs=[
            pl.BlockSpec((g, d), lambda l, j: (0, 0)),
            pl.BlockSpec((None, d, tn), lambda l, j: (l, 0, j)),
            pl.BlockSpec((None, 1, tn), lambda l, j: (l, 0, j)),
        ],
        out_specs=pl.BlockSpec((None, g, tn), lambda l, j: (l, 0, j)),
        compiler_params=_cparams(("arbitrary", "arbitrary")),
        name="modulation",
    )(cvec, ada_w, ada_b.reshape(depth, 1, n))


def _rot_half16(x):
    lane = lax.broadcasted_iota(jnp.int32, x.shape, 1)
    return jnp.where((lane % 32) < 16, pltpu.roll(x, LANES - 16, 1), pltpu.roll(x, 16, 1))


def _two_stream_specs(tm, d, gl, ctx_first_group=0):
    return [pl.BlockSpec((None, tm, d), lambda gi, ti: (jnp.minimum(gi, gl - 1), jnp.where(gi < gl, ti, 0), 0)),
            pl.BlockSpec((None, tm, d), lambda gi, ti: (ctx_first_group + jnp.maximum(gi - gl, 0),
                                                        jnp.where(gi < gl, 0, ti), 0))]


def _inproj_a_kernel(xl_ref, xc_ref, mod_ref, w_ref, cos_ref, sin_ref, *rest, n_latent_groups, n_cast):
    cast_in, o_ref, cast_out = rest[:n_cast], rest[n_cast], rest[n_cast + 1:]
    _cast_slabs(cast_in, cast_out)
    x = jnp.where(pl.program_id(0) < n_latent_groups, xl_ref[...], xc_ref[...])
    h = x * (1.0 + mod_ref[1:2, :]) + mod_ref[0:1, :]
    p = jnp.dot(h.astype(BF16), w_ref[...], preferred_element_type=F32)
    cos = cos_ref[...]
    sin = sin_ref[...]
    n_rope = 2 * DIFF_WIDTH // LANES
    for j in range(n_rope):
        blk = p[:, j * LANES:(j + 1) * LANES]
        o_ref[:, j * LANES:(j + 1) * LANES] = blk * cos + _rot_half16(blk) * sin
    o_ref[:, n_rope * LANES:] = p[:, n_rope * LANES:]


def _inproj_a(x_lat, x_ctx, mod, w_bf, cos_t, sin_t, cast_weights=(), tm=512):
    n_latent_groups, s, d = x_lat.shape
    g = n_latent_groups + x_ctx.shape[0]
    n = w_bf.shape[1]
    per = s // tm
    cast_specs = _cast_rider_specs(cast_weights, g * per, per)
    if cast_specs is None:
        proj, _ = _inproj_a(x_lat, x_ctx, mod, w_bf, cos_t, sin_t, (), tm)
        return proj, tuple(w.astype(BF16) for w in cast_weights)

    def table_map(gi, ti):
        return (jnp.where(gi >= n_latent_groups, 1, 0), ti, 0)

    outs = pl.pallas_call(
        functools.partial(_inproj_a_kernel, n_latent_groups=n_latent_groups, n_cast=len(cast_weights)),
        out_shape=(jax.ShapeDtypeStruct((g, s, n), F32),) + tuple(
            jax.ShapeDtypeStruct(w.shape, BF16) for w in cast_weights),
        grid=(g, per),
        in_specs=_two_stream_specs(tm, d, n_latent_groups) + [
            pl.BlockSpec((None, 6, d), lambda gi, ti: (gi, 0, 0)),
            _resident((d, n), lambda gi, ti: (0, 0)),
            pl.BlockSpec((None, tm, LANES), table_map),
            pl.BlockSpec((None, tm, LANES), table_map),
        ] + cast_specs,
        out_specs=(pl.BlockSpec((None, tm, n), lambda gi, ti: (gi, ti, 0)),) + tuple(cast_specs),
        compiler_params=_cparams(("arbitrary", "arbitrary")),
        name="inproj_a",
    )(x_lat, x_ctx, mod, w_bf, cos_t, sin_t, *cast_weights)
    return outs[0], outs[1:]


def _rope_tables(n_tokens):
    rows = n_tokens // GRID_W
    row_pos = jnp.repeat(jnp.arange(rows), GRID_W).astype(F32)
    col_pos = jnp.tile(jnp.arange(GRID_W), rows).astype(F32)
    n_freq = DIFF_DH // 4
    inv_freq = jnp.power(ROPE_BASE, -jnp.arange(n_freq, dtype=F32) / n_freq)
    ang = jnp.stack([row_pos[:, None] * inv_freq, col_pos[:, None] * inv_freq], axis=1)
    cos, sin = jnp.cos(ang), jnp.sin(ang)
    cos64 = jnp.concatenate([cos[:, 0], cos[:, 0], cos[:, 1], cos[:, 1]], axis=-1)
    sin64 = jnp.concatenate([-sin[:, 0], sin[:, 0], -sin[:, 1], sin[:, 1]], axis=-1)
    cos_l = jnp.tile(cos64, (1, LANES // DIFF_DH))
    sin_l = jnp.tile(sin64, (1, LANES // DIFF_DH))
    cos_t = jnp.stack([cos_l, jnp.ones_like(cos_l)])
    sin_t = jnp.stack([sin_l, jnp.zeros_like(sin_l)])
    return cos_t, sin_t


def _diff_attn_kernel(*refs, n_pieces, n_heads, lam_init, emit_kv):
    lam_ref, ng_ref, q_ref = refs[:3]
    kv_refs = refs[3:3 + 2 * n_pieces]
    o_ref = refs[3 + 2 * n_pieces]

    lp = lam_ref[...]
    lam = (jnp.exp(jnp.sum(lp[0:1] * lp[1:2], axis=-1, keepdims=True))
           - jnp.exp(jnp.sum(lp[2:3] * lp[3:4], axis=-1, keepdims=True)) + lam_init)
    nt = (((1,), (1,)), ((), ()))

    def softmax_pieces(ss):
        m = functools.reduce(jnp.maximum, [jnp.max(s, axis=-1, keepdims=True) for s in ss])
        es = [jnp.exp(s - m) for s in ss]
        l = functools.reduce(jnp.add, [jnp.sum(e, axis=-1, keepdims=True) for e in es])
        return [e / l for e in es]

    for h in range(n_heads):
        hs = slice(h * LANES, (h + 1) * LANES)
        q = q_ref[:, hs] * (DIFF_DH ** -0.5)
        lane = lax.broadcasted_iota(jnp.int32, q.shape, 1)
        q1 = jnp.where(lane < DIFF_DH, q, 0.0).astype(BF16)
        q2 = jnp.where(lane >= DIFF_DH, q, 0.0).astype(BF16)
        s1, s2, vs = [], [], []
        for i in range(n_pieces):
            kb = kv_refs[2 * i][:, hs].astype(BF16)
            vs.append(kv_refs[2 * i + 1][:, hs].astype(BF16))
            s1.append(lax.dot_general(q1, kb, nt, preferred_element_type=F32))
            s2.append(lax.dot_general(q2, kb, nt, preferred_element_type=F32))
        p1 = softmax_pieces(s1)
        p2 = softmax_pieces(s2)
        o = None
        for i in range(n_pieces):
            a = (p1[i] - lam * p2[i]).astype(BF16)
            t = jnp.dot(a, vs[i], preferred_element_type=F32)
            o = t if o is None else o + t
        o = o * lax.rsqrt(jnp.mean(o * o, axis=-1, keepdims=True) + LN_EPS)
        o_ref[:, hs] = (o * ng_ref[...] * (1.0 - lam_init)).astype(o_ref.dtype)
        if emit_kv:
            ko_ref, vo_ref = refs[4 + 2 * n_pieces:]
            ko_ref[h] = kv_refs[0][:, hs]
            vo_ref[h] = kv_refs[1][:, hs]


def _attn_context(proj, lam_p, norm_g, n_latent_groups, n_seq, seq, lam_init):
    g, s, _ = proj.shape
    per_group = s // seq
    blk = (None, seq, DIFF_WIDTH)

    def tok_map(colblock):
        return lambda b: (n_latent_groups + b // per_group, b % per_group, colblock)

    cache_shape = jax.ShapeDtypeStruct((n_seq, 1, DIFF_HEADS, seq, LANES), F32)
    cache_spec = pl.BlockSpec((None, None, DIFF_HEADS, seq, LANES), lambda b: (b, 0, 0, 0, 0))
    out_spec = pl.BlockSpec(blk, lambda b: (b // per_group, b % per_group, 0))
    return pl.pallas_call(
        functools.partial(_diff_attn_kernel, n_pieces=1, n_heads=DIFF_HEADS, lam_init=lam_init, emit_kv=True),
        out_shape=(jax.ShapeDtypeStruct((g - n_latent_groups, s, DIFF_WIDTH), BF16), cache_shape, cache_shape),
        grid=(n_seq,),
        in_specs=[
            pl.BlockSpec((4, DIFF_DH), lambda b: (0, 0)),
            pl.BlockSpec((1, LANES), lambda b: (0, 0)),
            pl.BlockSpec(blk, tok_map(0)),
            pl.BlockSpec(blk, tok_map(1)),
            pl.BlockSpec(blk, tok_map(2)),
        ],
        out_specs=(out_spec, cache_spec, cache_spec),
        compiler_params=_cparams(("arbitrary",)),
        name="attn_context",
    )(lam_p, norm_g, proj, proj, proj)


def _attn_latent(proj, cache_k, cache_v, lam_p, norm_g, n_latent_groups, lam_init, tq=256):
    g, s, _ = proj.shape
    past = cache_k.shape[3]
    cache_spec = pl.BlockSpec((None, None, None, past, LANES), lambda b, h, qi: (b, 0, h, 0, 0))
    return pl.pallas_call(
        functools.partial(_diff_attn_kernel, n_pieces=2, n_heads=1, lam_init=lam_init, emit_kv=False),
        out_shape=jax.ShapeDtypeStruct((n_latent_groups, s, DIFF_WIDTH), BF16),
        grid=(n_latent_groups, DIFF_HEADS, s // tq),
        in_specs=[
            pl.BlockSpec((4, DIFF_DH), lambda b, h, qi: (0, 0)),
            pl.BlockSpec((1, LANES), lambda b, h, qi: (0, 0)),
            pl.BlockSpec((None, tq, LANES), lambda b, h, qi: (b, qi, h)),
            cache_spec,
            cache_spec,
            pl.BlockSpec((None, s, LANES), lambda b, h, qi: (b, 0, DIFF_HEADS + h)),
            pl.BlockSpec((None, s, LANES), lambda b, h, qi: (b, 0, 2 * DIFF_HEADS + h)),
        ],
        out_specs=pl.BlockSpec((None, tq, LANES), lambda b, h, qi: (b, qi, h)),
        compiler_params=_cparams(("arbitrary", "arbitrary", "arbitrary")),
        name="attn_latent",
    )(lam_p, norm_g, proj, cache_k, cache_v, proj, proj)


POOL_ROW_BLOCK = 256
POOL_COL_WINDOW = 512
assert (POOL_COL_WINDOW - POOL_ROW_BLOCK) // 2 >= max(POOL_WINDOWS) // 2


def _pool_kernel(p_ref, w_ref, sc_ref, o_ref, band_ref, *, seq):
    @pl.when((pl.program_id(0) == 0) & (pl.program_id(1) == 0))
    def _():
        t = lax.broadcasted_iota(jnp.int32, (seq, seq), 0)
        s_ = lax.broadcasted_iota(jnp.int32, (seq, seq), 1)
        for gi, w in enumerate(POOL_WINDOWS):
            inside = (s_ >= t - w // 2) & (s_ <= t + w // 2 - 1)
            band_ref[gi] = jnp.where(inside, 1.0, 0.0).astype(BF16)

    tcol = lax.broadcasted_iota(jnp.int32, (seq, 1), 0)
    for gi, w in enumerate(POOL_WINDOWS):
        u = p_ref[:, gi * POOL_GC:(gi + 1) * POOL_GC]
        hi, lo = _split_bf16(u)
        rb = min(seq, POOL_ROW_BLOCK)
        cw = min(seq, POOL_COL_WINDOW)
        blocks = []
        for i in range(seq // rb):
            c0 = min(max(i * rb - (cw - rb) // 2, 0), seq - cw)
            band = band_ref[gi, i * rb:(i + 1) * rb, c0:c0 + cw]
            blocks.append(jnp.dot(band, hi[c0:c0 + cw], preferred_element_type=F32)
                          + jnp.dot(band, lo[c0:c0 + cw], preferred_element_type=F32))
        win = blocks[0] if len(blocks) == 1 else jnp.concatenate(blocks, axis=0)
        cnt = (jnp.minimum(tcol + (w // 2 - 1), seq - 1) - jnp.maximum(tcol - w // 2, 0) + 1).astype(F32)
        pooled = win / cnt - u
        mixed = jnp.dot(pooled.astype(BF16), w_ref[gi].astype(BF16), preferred_element_type=F32)
        o_ref[:, gi * POOL_GC:(gi + 1) * POOL_GC] = (
            mixed * sc_ref[:, gi * POOL_GC:(gi + 1) * POOL_GC]).astype(o_ref.dtype)


def _pool(proj, pool_w, pool_scale, g0, n_groups, seq):
    g, s, _ = proj.shape
    col = 3 * DIFF_WIDTH // POOL_WIDTH
    return pl.pallas_call(
        functools.partial(_pool_kernel, seq=seq),
        out_shape=jax.ShapeDtypeStruct((n_groups, s, POOL_WIDTH), BF16),
        grid=(n_groups, s // seq),
        in_specs=[
            pl.BlockSpec((None, seq, POOL_WIDTH), lambda gi, ti: (g0 + gi, ti, col)),
            pl.BlockSpec((POOL_GROUPS, POOL_GC, POOL_GC), lambda gi, ti: (0, 0, 0)),
            pl.BlockSpec((1, POOL_WIDTH), lambda gi, ti: (0, 0)),
        ],
        out_specs=pl.BlockSpec((None, seq, POOL_WIDTH), lambda gi, ti: (gi, ti, 0)),
        scratch_shapes=[pltpu.VMEM((POOL_GROUPS, seq, seq), BF16)],
        compiler_params=_cparams(("arbitrary", "arbitrary")),
        name="pool_%d" % seq,
    )(proj, pool_w, pool_scale.reshape(1, POOL_WIDTH))


def _outproj_kernel(*refs, n_in, gate_row, n_latent_groups):
    a_refs = refs[:2 * n_in]
    w_refs = refs[2 * n_in:3 * n_in]
    xl_ref, xc_ref, mod_ref, g_ref, b_ref, o_ref = refs[3 * n_in:]
    is_latent = pl.program_id(0) < n_latent_groups
    acc = None
    for i, w_ref in enumerate(w_refs):
        a = jnp.where(is_latent, a_refs[2 * i][...], a_refs[2 * i + 1][...])
        t = jnp.dot(a, w_ref[...], preferred_element_type=F32)
        acc = t if acc is None else acc + t
    x = jnp.where(is_latent, xl_ref[...], xc_ref[...])
    z = ALPHA * x + mod_ref[gate_row:gate_row + 1, :] * acc
    o_ref[...] = _layer_norm_rows(z, g_ref[...], b_ref[...])


def _outproj(acts, weights, x_lat, x_ctx, mod, ln_g, ln_b, gate_row, tm=512):
    gl = acts[0][0].shape[0]
    _, s, d = x_lat.shape
    if x_ctx is None:
        g = x_lat.shape[0]
        x_ctx, x_specs = x_lat, _two_stream_specs(tm, d, gl, gl)
    else:
        g = gl + x_ctx.shape[0]
        x_specs = _two_stream_specs(tm, d, gl)
    n_in = len(acts)
    in_specs = []
    flat_acts = []
    for a_lat, a_ctx in acts:
        in_specs += _two_stream_specs(tm, a_lat.shape[-1], gl)
        flat_acts += [a_lat, a_ctx]
    in_specs += [_resident(w.shape, lambda gi, ti: (0, 0)) for w in weights]
    in_specs += x_specs
    in_specs += [
        pl.BlockSpec((None, 6, d), lambda gi, ti: (gi, 0, 0)),
        pl.BlockSpec((1, d), lambda gi, ti: (0, 0)),
        pl.BlockSpec((1, d), lambda gi, ti: (0, 0)),
    ]
    return pl.pallas_call(
        functools.partial(_outproj_kernel, n_in=n_in, gate_row=gate_row, n_latent_groups=gl),
        out_shape=jax.ShapeDtypeStruct((g, s, d), F32),
        grid=(g, s // tm),
        in_specs=in_specs,
        out_specs=pl.BlockSpec((None, tm, d), lambda gi, ti: (gi, ti, 0)),
        compiler_params=_cparams(("arbitrary", "arbitrary")),
        name="outproj",
    )(*flat_acts, *weights, x_lat, x_ctx, mod, ln_g.reshape(1, d), ln_b.reshape(1, d))


def _swiglu_chunks(h_bf, wgu_ref, wd_ref, d_ff):
    acc = None
    for j in range(d_ff // FF_CHUNK):
        lo = j * FF_CHUNK
        gate = jnp.dot(h_bf, wgu_ref[:, lo:lo + FF_CHUNK], preferred_element_type=F32)
        up = jnp.dot(h_bf, wgu_ref[:, d_ff + lo:d_ff + lo + FF_CHUNK], preferred_element_type=F32)
        act = (gate * _sigmoid(gate) * up).astype(BF16)
        t = jnp.dot(act, wd_ref[lo:lo + FF_CHUNK, :], preferred_element_type=F32)
        acc = t if acc is None else acc + t
    return acc


def _ffn_kernel(x_ref, mod_ref, wgu_ref, wd_ref, g_ref, b_ref, *rest, n_cast):
    cast_in, o_ref, cast_out = rest[:n_cast], rest[n_cast], rest[n_cast + 1:]
    _cast_slabs(cast_in, cast_out)
    x = x_ref[...]
    h = (x * (1.0 + mod_ref[4:5, :]) + mod_ref[3:4, :]).astype(BF16)
    acc = _swiglu_chunks(h, wgu_ref, wd_ref, D_FF)
    z = ALPHA * x + mod_ref[5:6, :] * acc
    o_ref[...] = _layer_norm_rows(z, g_ref[...], b_ref[...])


def _ffn(x, mod, wgu_bf, wd_bf, ln_g, ln_b, cast_weights=(), tm=512):
    g, s, d = x.shape
    per = s // tm
    cast_specs = _cast_rider_specs(cast_weights, g * per, per)
    if cast_specs is None:
        out, _ = _ffn(x, mod, wgu_bf, wd_bf, ln_g, ln_b, (), tm)
        return out, tuple(w.astype(BF16) for w in cast_weights)
    outs = pl.pallas_call(
        functools.partial(_ffn_kernel, n_cast=len(cast_weights)),
        out_shape=(jax.ShapeDtypeStruct((g, s, d), F32),) + tuple(
            jax.ShapeDtypeStruct(w.shape, BF16) for w in cast_weights),
        grid=(g, per),
        in_specs=[
            pl.BlockSpec((None, tm, d), lambda gi, ti: (gi, ti, 0)),
            pl.BlockSpec((None, 6, d), lambda gi, ti: (gi, 0, 0)),
            _resident(wgu_bf.shape, lambda gi, ti: (0, 0)),
            _resident(wd_bf.shape, lambda gi, ti: (0, 0)),
            pl.BlockSpec((1, d), lambda gi, ti: (0, 0)),
            pl.BlockSpec((1, d), lambda gi, ti: (0, 0)),
        ] + cast_specs,
        out_specs=(pl.BlockSpec((None, tm, d), lambda gi, ti: (gi, ti, 0)),) + tuple(cast_specs),
        compiler_params=_cparams(("arbitrary", "arbitrary")),
        name="ffn",
    )(x, mod, wgu_bf, wd_bf, ln_g.reshape(1, d), ln_b.reshape(1, d), *cast_weights)
    return outs[0], outs[1:]


MOE_BM = 512
MOE_SB = 256
MOE_TC = 256
MOE_WINDOW = 6
META_LANES = LANES


def _router_top2(h, rw_ref, rb_ref):
    h_hi, h_lo = _split_bf16(h)
    w_hi, w_lo = _split_bf16(rw_ref[...])
    logits = (jnp.dot(h_hi, w_hi, preferred_element_type=F32)
              + jnp.dot(h_lo, w_hi, preferred_element_type=F32)
              + jnp.dot(h_hi, w_lo, preferred_element_type=F32)) + rb_ref[...]
    lane = lax.broadcasted_iota(jnp.int32, logits.shape, 1).astype(F32)
    neg = -jnp.inf
    logits = jnp.where(lane < N_EXPERTS, logits, neg)
    m1 = jnp.max(logits, axis=-1, keepdims=True)
    i1 = jnp.min(jnp.where(logits == m1, lane, float(LANES)), axis=-1, keepdims=True)
    rest = jnp.where(lane == i1, neg, logits)
    m2 = jnp.max(rest, axis=-1, keepdims=True)
    i2 = jnp.min(jnp.where(rest == m2, lane, float(LANES)), axis=-1, keepdims=True)
    e2 = jnp.exp(m2 - m1)
    return lane, i1, i2, 1.0 / (1.0 + e2), e2 / (1.0 + e2)


def _route_kernel(x_ref, mod_ref, rw_ref, rb_ref, h_ref, wk_ref, srank_ref, cnt_ref, tri_ref, run_ref):
    tm, d = x_ref.shape

    @pl.when(pl.program_id(0) == 0)
    def _():
        r = lax.broadcasted_iota(jnp.int32, (tm, tm), 0)
        c = lax.broadcasted_iota(jnp.int32, (tm, tm), 1)
        tri_ref[...] = jnp.where(c <= r, 1.0, 0.0).astype(BF16)
        run_ref[...] = jnp.zeros_like(run_ref)

    h = x_ref[...] * (1.0 + mod_ref[4:5, :]) + mod_ref[3:4, :]
    h_ref[:, 0:d] = h.astype(BF16)
    lane, i1, i2, w1, w2 = _router_top2(h, rw_ref, rb_ref)
    first_is_low = i1 < i2
    e_hi = jnp.where(first_is_low, i2, i1)
    wk_ref[...] = jnp.where(lane == 0.0, jnp.where(first_is_low, w1, w2),
                            jnp.where(lane == 1.0, jnp.where(first_is_low, w2, w1), 0.0))
    tok = (pl.program_id(0) * tm + lax.broadcasted_iota(jnp.int32, (tm, META_LANES), 0))
    meta = jnp.where(lane == 0.0, (tok // 128).astype(F32),
                     jnp.where(lane == 1.0, (tok % 128).astype(F32),
                               jnp.where(lane == 2.0, e_hi, jnp.where(lane == 3.0, 1.0, 0.0))))
    h_ref[:, d:d + META_LANES] = meta.astype(BF16)

    member = (lane == i1) | (lane == i2)
    mem = jnp.where(member, 1.0, 0.0)
    rank = jnp.dot(tri_ref[...], mem.astype(BF16), preferred_element_type=F32) + run_ref[...]
    srank_ref[...] = jnp.where(member, rank, -rank).T[0:SUBLANES, :]
    run_ref[...] = rank[tm - 1:tm, :]
    cnt_ref[...] = rank[tm - 1:tm, :]


def _route(x, mod, router_w, router_b, tm=512):
    g, s, d = x.shape
    n = g * s
    assert n <= 128 * 256
    assert n >= MOE_WINDOW * MOE_TC
    per = s // tm
    rw = jnp.pad(router_w, ((0, 0), (0, LANES - N_EXPERTS)))
    rb = jnp.pad(router_b, (0, LANES - N_EXPERTS)).reshape(1, LANES)
    return pl.pallas_call(
        _route_kernel,
        out_shape=(jax.ShapeDtypeStruct((n, d + META_LANES), BF16), jax.ShapeDtypeStruct((n, LANES), F32),
                   jax.ShapeDtypeStruct((SUBLANES, n), F32), jax.ShapeDtypeStruct((1, LANES), F32)),
        grid=(n // tm,),
        in_specs=[
            pl.BlockSpec((None, tm, d), lambda i: (i // per, i % per, 0)),
            pl.BlockSpec((None, 6, d), lambda i: (i // per, 0, 0)),
            pl.BlockSpec((d, LANES), lambda i: (0, 0)),
            pl.BlockSpec((1, LANES), lambda i: (0, 0)),
        ],
        out_specs=(pl.BlockSpec((tm, d + META_LANES), lambda i: (i, 0)), pl.BlockSpec((tm, LANES), lambda i: (i, 0)),
                   pl.BlockSpec((SUBLANES, tm), lambda i: (0, i)), pl.BlockSpec((1, LANES), lambda i: (0, 0))),
        scratch_shapes=[pltpu.VMEM((tm, tm), BF16), pltpu.VMEM((1, LANES), F32)],
        compiler_params=_cparams(("arbitrary",)),
        name="moe_route",
    )(x, mod, rw, rb)


def _moe_plan(srank, counts, n_blocks):
    e_n = N_EXPERTS
    i32 = jnp.int32
    cnt = counts[0, :e_n].astype(i32)
    nb = (cnt + MOE_BM - 1) // MOE_BM
    nb_incl = jnp.cumsum(nb)
    gstart = nb_incl - nb
    n_used = nb_incl[-1]
    sr = srank[:e_n]
    rank = jnp.abs(sr).astype(i32)
    pos_row = jnp.where(sr > 0, rank - 1 + MOE_BM * gstart[:, None], -1)
    chunk_end = rank[:, MOE_TC - 1::MOE_TC]

    rc = jnp.minimum(jnp.arange(n_blocks, dtype=i32), n_used - 1)
    e_r = jnp.minimum(jnp.sum(nb_incl[None, :] <= rc[:, None], axis=1, dtype=i32), e_n - 1)

    sb_per = MOE_BM // MOE_SB
    q = jnp.arange(n_blocks * sb_per, dtype=i32)
    r_q = jnp.minimum(q // sb_per, n_used - 1)
    e_q = e_r[r_q]
    first = ((r_q - gstart[e_q]) * sb_per + q % sb_per) * MOE_SB
    used = (q // sb_per < n_used) & (first < cnt[e_q])
    hi = jnp.minimum(first + MOE_SB, cnt[e_q])
    ends_q = chunk_end[e_q]
    jlo = jnp.where(used, jnp.sum(ends_q < (first + 1)[:, None], axis=1, dtype=i32), 0)
    jhi = jnp.where(used, jnp.sum(ends_q < hi[:, None], axis=1, dtype=i32), -1)
    return pos_row, e_r, jlo, jhi, n_used.reshape(1)


def _experts_kernel(be_ref, jlo_ref, jhi_ref, nused_ref, h_ref, pos_ref, wgu_ref, wd_ref, y_ref, meta_ref, xg_ref):
    r = pl.program_id(0)
    d = y_ref.shape[1]

    @pl.when(r < nused_ref[0])
    def _():
        e = be_ref[r]
        sb_per = MOE_BM // MOE_SB
        n_chunks = h_ref.shape[0] // MOE_TC
        for u in range(sb_per):
            sb_rows = slice(u * MOE_SB, (u + 1) * MOE_SB)
            slot = r * MOE_BM + u * MOE_SB + lax.broadcasted_iota(jnp.int32, (MOE_SB, MOE_TC), 0)

            def onehot(j, slot=slot):
                return jnp.where(pos_ref[e, pl.ds(j, 1), :] == slot, 1.0, 0.0).astype(BF16)

            j0 = jnp.minimum(jlo_ref[r * sb_per + u], n_chunks - MOE_WINDOW)
            window = jnp.concatenate([onehot(j0 + j) for j in range(MOE_WINDOW)], axis=-1)
            rows = h_ref[pl.ds(pl.multiple_of(j0 * MOE_TC, MOE_TC), MOE_WINDOW * MOE_TC), :]
            xg_ref[sb_rows, :] = jnp.dot(window, rows, preferred_element_type=F32)

            def chunk(j, carry, sb_rows=sb_rows, onehot=onehot):
                rows = h_ref[pl.ds(pl.multiple_of(j * MOE_TC, MOE_TC), MOE_TC), :]
                xg_ref[sb_rows, :] += jnp.dot(onehot(j), rows, preferred_element_type=F32)
                return carry

            lax.fori_loop(j0 + MOE_WINDOW, jhi_ref[r * sb_per + u] + 1, chunk, 0)
        meta_ref[...] = xg_ref[:, d:d + META_LANES].T[0:SUBLANES, :]
        y_ref[...] = _swiglu_chunks(xg_ref[:, 0:d].astype(BF16), wgu_ref, wd_ref, D_FF_EXPERT)

    @pl.when(r >= nused_ref[0])
    def _():
        y_ref[...] = jnp.zeros_like(y_ref)
        meta_ref[...] = jnp.zeros_like(meta_ref)


def _experts(h, pos_row, plan, wgu_bf, wd_bf, n_blocks):
    n, dx = h.shape
    d = dx - META_LANES
    e_r, jlo, jhi, n_used = plan
    grid_spec = pltpu.PrefetchScalarGridSpec(
        num_scalar_prefetch=4,
        grid=(n_blocks,),
        in_specs=[
            _resident((n, dx), lambda r, be, lo, hi, nu: (0, 0)),
            _resident(pos_row.shape, lambda r, be, lo, hi, nu: (0, 0, 0)),
            pl.BlockSpec((None, d, 2 * D_FF_EXPERT), lambda r, be, lo, hi, nu: (be[r], 0, 0),
                         pipeline_mode=pl.Buffered(1)),
            pl.BlockSpec((None, D_FF_EXPERT, d), lambda r, be, lo, hi, nu: (be[r], 0, 0),
                         pipeline_mode=pl.Buffered(1)),
        ],
        out_specs=(pl.BlockSpec((MOE_BM, d), lambda r, be, lo, hi, nu: (r, 0)),
                   pl.BlockSpec((None, SUBLANES, MOE_BM), lambda r, be, lo, hi, nu: (r, 0, 0))),
        scratch_shapes=[pltpu.VMEM((MOE_BM, dx), F32)],
    )
    return pl.pallas_call(
        _experts_kernel,
        out_shape=(jax.ShapeDtypeStruct((n_blocks * MOE_BM, d), F32),
                   jax.ShapeDtypeStruct((n_blocks, SUBLANES, MOE_BM), F32)),
        grid_spec=grid_spec,
        compiler_params=_cparams(("arbitrary",), HIGH_VMEM_LIMIT),
        name="moe_experts",
    )(e_r, jlo, jhi, n_used, h, pos_row, wgu_bf, wd_bf)


def _scatter_kernel(dest_ref, y_ref, o_ref, ybuf_ref, sem):
    r = pl.program_id(0)
    n_steps = pl.num_programs(0)
    buf = r % 2

    def wait_block(b):
        pltpu.make_async_copy(ybuf_ref.at[b], o_ref.at[pl.ds(0, MOE_BM), :], sem.at[b]).wait()

    @pl.when(r >= 2)
    def _():
        wait_block(buf)

    ybuf_ref[buf] = y_ref[...]

    for i in range(MOE_BM):
        pltpu.make_async_copy(ybuf_ref.at[buf, pl.ds(i, 1), :], o_ref.at[pl.ds(dest_ref[0, i], 1), :],
                              sem.at[buf]).start()

    @pl.when(r == n_steps - 1)
    def _():
        wait_block(1 - buf)
        wait_block(buf)


def _scatter_rows(y, dest, n_rows_out):
    n_slots, d = y.shape
    n_blocks = n_slots // MOE_BM
    assert n_blocks >= 2
    return pl.pallas_call(
        _scatter_kernel,
        out_shape=jax.ShapeDtypeStruct((n_rows_out, d), F32),
        grid=(n_blocks,),
        in_specs=[
            pl.BlockSpec((None, 1, MOE_BM), lambda r: (r, 0, 0), memory_space=pltpu.SMEM),
            pl.BlockSpec((MOE_BM, d), lambda r: (r, 0)),
        ],
        out_specs=pl.BlockSpec(memory_space=pl.ANY),
        scratch_shapes=[pltpu.VMEM((2, MOE_BM, d), F32), pltpu.SemaphoreType.DMA((2,))],
        compiler_params=_cparams(("arbitrary",)),
        name="moe_scatter",
    )(dest.reshape(n_blocks, 1, MOE_BM), y)


def _finish_kernel(a_ref, b_ref, wk_ref, x_ref, mod_ref, g_ref, bb_ref, ol_ref, oc_ref, *, n_latent_tiles):
    lane = lax.broadcasted_iota(jnp.int32, wk_ref.shape, 1)
    wk = wk_ref[...]
    w0 = jnp.sum(jnp.where(lane == 0, wk, 0.0), axis=-1, keepdims=True)
    w1 = jnp.sum(jnp.where(lane == 1, wk, 0.0), axis=-1, keepdims=True)
    acc = w0 * a_ref[...] + w1 * b_ref[...]
    z = ALPHA * x_ref[...] + mod_ref[5:6, :] * acc
    out = _layer_norm_rows(z, g_ref[...], bb_ref[...])
    is_latent = pl.program_id(0) < n_latent_tiles

    @pl.when(is_latent)
    def _():
        ol_ref[...] = out

    @pl.when(jnp.logical_not(is_latent))
    def _():
        oc_ref[...] = out


def _finish(rows, wk, x, mod, ln_g, ln_b, n_latent_groups, tm=512):
    g, s, d = x.shape
    n = g * s
    per = s // tm
    nt = n // tm
    nlt = n_latent_groups * per

    def lat_map(i):
        j = jnp.minimum(i, nlt - 1)
        return (j // per, j % per, 0)

    def ctx_map(i):
        j = jnp.maximum(i - nlt, 0)
        return (j // per, j % per, 0)

    return pl.pallas_call(
        functools.partial(_finish_kernel, n_latent_tiles=nlt),
        out_shape=(jax.ShapeDtypeStruct((n_latent_groups, s, d), F32),
                   jax.ShapeDtypeStruct((g - n_latent_groups, s, d), F32)),
        grid=(nt,),
        in_specs=[
            pl.BlockSpec((tm, d), lambda i: (i, 0)),
            pl.BlockSpec((tm, d), lambda i: (nt + i, 0)),
            pl.BlockSpec((tm, LANES), lambda i: (i, 0)),
            pl.BlockSpec((None, tm, d), lambda i: (i // per, i % per, 0)),
            pl.BlockSpec((None, 6, d), lambda i: (i // per, 0, 0)),
            pl.BlockSpec((1, d), lambda i: (0, 0)),
            pl.BlockSpec((1, d), lambda i: (0, 0)),
        ],
        out_specs=(pl.BlockSpec((None, tm, d), lat_map), pl.BlockSpec((None, tm, d), ctx_map)),
        compiler_params=_cparams(("arbitrary",)),
        name="moe_finish",
    )(rows, rows, wk, x, mod, ln_g.reshape(1, d), ln_b.reshape(1, d))


def _moe(x, mod, router_w, router_b, wgu_bf, wd_bf, ln_g, ln_b, n_latent_groups):
    g, s, d = x.shape
    n = g * s
    n_blocks = 2 * n // MOE_BM + N_EXPERTS
    n_slots = n_blocks * MOE_BM
    h, wk, srank, counts = _route(x, mod, router_w, router_b)
    pos_row, e_r, jlo, jhi, n_used = _moe_plan(srank, counts, n_blocks)
    pos_row = pos_row.reshape(N_EXPERTS, n // MOE_TC, MOE_TC)
    y, meta = _experts(h, pos_row, (e_r, jlo, jhi, n_used), wgu_bf, wd_bf, n_blocks)
    tok = (meta[:, 0] * 128.0 + meta[:, 1]).astype(jnp.int32)
    choice = (meta[:, 2] == e_r[:, None].astype(F32)).astype(jnp.int32)
    unused = meta[:, 3] < 0.5
    spare = (2 * n + jnp.cumsum(unused.reshape(-1).astype(jnp.int32)) - 1).reshape(n_blocks, MOE_BM)
    dest = jnp.where(unused, spare, choice * n + tok)
    rows = _scatter_rows(y, dest, n_slots)
    return _finish(rows, wk, x, mod, ln_g, ln_b, n_latent_groups)


def _inproj_c_kernel(x_ref, mod_ref, w_ref, wg_ref, bg_ref, o_ref, og_ref):
    h = (x_ref[...] * (1.0 + mod_ref[1:2, :]) + mod_ref[0:1, :]).astype(BF16)
    o_ref[...] = jnp.dot(h, w_ref[:, 0:o_ref.shape[1]], preferred_element_type=F32)
    og_ref[...] = jnp.dot(h, wg_ref[...], preferred_element_type=F32) + bg_ref[...]


def _inproj_c(x, mod, w_bf, n, wg_bf, bg, tm=512):
    g, s, d = x.shape
    ng = wg_bf.shape[1]
    return pl.pallas_call(
        _inproj_c_kernel,
        out_shape=(jax.ShapeDtypeStruct((g, s, n), F32), jax.ShapeDtypeStruct((g, s, ng), F32)),
        grid=(g, s // tm),
        in_specs=[
            pl.BlockSpec((None, tm, d), lambda gi, ti: (gi, ti, 0)),
            pl.BlockSpec((None, 6, d), lambda gi, ti: (gi, 0, 0)),
            _resident(w_bf.shape, lambda gi, ti: (0, 0)),
            _resident((d, ng), lambda gi, ti: (0, 0)),
            pl.BlockSpec((1, ng), lambda gi, ti: (0, 0)),
        ],
        out_specs=(pl.BlockSpec((None, tm, n), lambda gi, ti: (gi, ti, 0)),
                   pl.BlockSpec((None, tm, ng), lambda gi, ti: (gi, ti, 0))),
        compiler_params=_cparams(("arbitrary", "arbitrary")),
        name="inproj_c",
    )(x, mod, w_bf, wg_bf, bg)


def _log_sigmoid(x):
    return jnp.minimum(x, 0.0) - jnp.log(1.0 + jnp.exp(-jnp.abs(x)))


MLSTM_L = 128


def _split3_bf16(x):
    hi = x.astype(BF16)
    r1 = x - hi.astype(F32)
    mid = r1.astype(BF16)
    lo = (r1 - mid.astype(F32)).astype(BF16)
    return hi, mid, lo


def _mlstm_kernel(*refs, seq, hg, has_init, emit_state):
    q_ref, k_ref, v_ref, o_ref, gi_ref, gf_ref, hgain_ref = refs[:7]
    pos = 7
    if has_init:
        c0_ref, n0_ref, m0_ref = refs[pos:pos + 3]
        pos += 3
    out_ref = refs[pos]
    pos += 1
    if emit_state:
        co_ref, no_ref, mo_ref = refs[pos:pos + 3]
        pos += 3
    cext_ref, hf_ref, hb_ref, b_ref, g_ref, gmax_ref, mt_ref, wi_ref, en_ref, ws_ref, gt_ref, wc_ref = refs[pos:]

    L = MLSTM_L
    dh = MLSTM_DH
    nh = MLSTM_HEADS
    nc = seq // L
    head0 = pl.program_id(1) * hg
    neg = -jnp.inf

    lane = lax.broadcasted_iota(jnp.int32, (L, LANES), 1)
    lane1 = lax.broadcasted_iota(jnp.int32, (1, LANES), 1)
    row = lax.broadcasted_iota(jnp.int32, (L, L), 0)
    col = lax.broadcasted_iota(jnp.int32, (L, L), 1)
    lower = col <= row
    upper = col >= row
    tri_l = jnp.where(lower, 1.0, 0.0).astype(BF16)
    tri_u = jnp.where(upper, 1.0, 0.0).astype(BF16)
    fwd_lane = lane < nh
    fwd_lane1 = lane1 < nh
    trow = lax.broadcasted_iota(jnp.int32, (L, LANES), 0)

    btot, glast = [], []
    for c in range(nc):
        rows = slice(c * L, (c + 1) * L)
        f = _log_sigmoid(gf_ref[rows, :])
        parts = _split3_bf16(f)
        pre = sum(jnp.dot(tri_l, p, preferred_element_type=F32) for p in parts)
        suf = sum(jnp.dot(tri_u, p, preferred_element_type=F32) for p in parts)
        b = jnp.where(fwd_lane, pre, suf)
        g = gi_ref[rows, :] - b
        gp, gs = g, g
        k = 1
        while k < L:
            gp = jnp.where(trow >= k, jnp.maximum(gp, pltpu.roll(gp, k, 0)), gp)
            gs = jnp.where(trow < L - k, jnp.maximum(gs, pltpu.roll(gs, L - k, 0)), gs)
            k *= 2
        gmax = jnp.where(fwd_lane, gp, gs)
        b_ref[rows, :] = b
        g_ref[rows, :] = g
        gmax_ref[rows, :] = gmax
        btot.append(jnp.where(fwd_lane1, b[L - 1:L, :], b[0:1, :]))
        glast.append(jnp.where(fwd_lane1, gmax[L - 1:L, :], gmax[0:1, :]))

    m_init = m0_ref[...] if has_init else jnp.zeros((1, LANES), F32)
    mf, mb = m_init, m_init
    ms_f, mn_f, ms_b, mn_b = [None] * nc, [None] * nc, [None] * nc, [None] * nc
    for c in range(nc):
        ms_f[c] = mf
        mf = btot[c] + jnp.maximum(mf, glast[c])
        mn_f[c] = mf
        cb = nc - 1 - c
        ms_b[cb] = mb
        mb = btot[cb] + jnp.maximum(mb, glast[cb])
        mn_b[cb] = mb
    m_final = jnp.where(fwd_lane1, mf, mb)

    for c in range(nc):
        rows = slice(c * L, (c + 1) * L)
        m_start = jnp.where(fwd_lane1, ms_f[c], ms_b[c])
        m_next = jnp.where(fwd_lane1, mn_f[c], mn_b[c])
        g = g_ref[rows, :]
        mt = jnp.maximum(m_start, gmax_ref[rows, :])
        mt_ref[rows, :] = mt
        wi_ref[rows, :] = jnp.exp(m_start - mt)
        en_ref[rows, :] = jnp.exp(-(b_ref[rows, :] + mt))
        ws_ref[rows, :] = jnp.exp(btot[c] + g - m_next)
        gt_ref[c] = g.T
        wc_ref[c:c + 1, :] = jnp.exp(btot[c] + m_start - m_next)

    for d in range(2):
        for hh in range(hg):
            idx = d * hg + hh
            if has_init:
                cext_ref[idx, :, 0:dh] = c0_ref[d, hh]
                n0_tile = jnp.where(lax.broadcasted_iota(jnp.int32, (dh, dh), 0) == 0, n0_ref[d, hh], 0.0)
                cext_ref[idx, :, dh:2 * dh] = n0_tile.T
            else:
                cext_ref[idx] = jnp.zeros((dh, 2 * dh), F32)

    ones_col = jnp.where(lane == 0, 1.0, 0.0).astype(BF16)
    nt = (((1,), (1,)), ((), ()))
    tn = (((0,), (0,)), ((), ()))

    def column(x, j):
        return jnp.sum(jnp.where(lane == j, x, 0.0), axis=-1, keepdims=True)

    def one_direction(d, hh, c, s_qk, q_bf, k_s, v_ext, v_bf):
        idx = d * hg + hh
        j = d * nh + head0 + hh
        rows = pl.ds(pl.multiple_of(c * L, L), L)
        mt = column(mt_ref[rows, :], j)
        wi = column(wi_ref[rows, :], j)
        en = column(en_ref[rows, :], j)
        ws = column(ws_ref[rows, :], j)
        g_r = gt_ref[c, pl.ds(j, 1), :]
        w_c = jnp.sum(jnp.where(lane1 == j, wc_ref[pl.ds(c, 1), :], 0.0), axis=-1, keepdims=True)
        causal = lower if d == 0 else upper
        p = s_qk * jnp.exp(jnp.where(causal, g_r - mt, neg))
        qc = jnp.dot(q_bf, cext_ref[idx].astype(BF16), preferred_element_type=F32)
        num = wi * qc[:, 0:dh] + jnp.dot(p.astype(BF16), v_bf, preferred_element_type=F32)
        den = wi * qc[:, dh:dh + 1] + jnp.sum(p, axis=-1, keepdims=True)
        h = num / jnp.maximum(jnp.abs(den), en)
        upd = lax.dot_general((ws * k_s).astype(BF16), v_ext, tn, preferred_element_type=F32)
        cext_ref[idx] = w_c * cext_ref[idx] + upd
        return h

    def load_chunk(hh, c):
        sl = (pl.ds(pl.multiple_of(c * L, L), L), slice(hh * dh, (hh + 1) * dh))
        q_bf = q_ref[sl].astype(BF16)
        k_s = k_ref[sl] * (dh ** -0.5)
        v_bf = v_ref[sl].astype(BF16)
        v_ext = jnp.concatenate([v_bf, ones_col], axis=-1)
        s_qk = lax.dot_general(q_bf, k_s.astype(BF16), nt, preferred_element_type=F32)
        return s_qk, q_bf, k_s, v_ext, v_bf

    def step(c, carry):
        cb = nc - 1 - c
        for hh in range(hg):
            h = one_direction(0, hh, c, *load_chunk(hh, c))
            hf_ref[pl.ds(pl.multiple_of(c * L, L), L), hh * dh:(hh + 1) * dh] = h
        for hh in range(hg):
            h = one_direction(1, hh, cb, *load_chunk(hh, cb))
            hb_ref[pl.ds(pl.multiple_of(cb * L, L), L), hh * dh:(hh + 1) * dh] = h
        return carry

    lax.fori_loop(0, nc, step, 0)

    for hh in range(hg):
        cs = slice(hh * dh, (hh + 1) * dh)
        hs = hf_ref[:, cs] + hb_ref[:, cs]
        mu = jnp.mean(hs, axis=-1, keepdims=True)
        hc = hs - mu
        var = jnp.mean(hc * hc, axis=-1, keepdims=True)
        hn = hc * lax.rsqrt(var + LN_EPS) * hgain_ref[:, cs]
        out_ref[:, cs] = (_sigmoid(o_ref[:, cs]) * hn).astype(out_ref.dtype)

    if emit_state:
        for d in range(2):
            for hh in range(hg):
                idx = d * hg + hh
                co_ref[d, hh] = cext_ref[idx, :, 0:dh]
                no_ref[d, hh] = cext_ref[idx, :, dh:2 * dh].T[0:1, :]
        mo_ref[...] = m_final


def _mlstm(proj, gates, head_g, g0, n_seq, seq, hg, init=None, emit_state=False):
    g, s, _ = proj.shape
    per_group = s // seq
    n_hg = MLSTM_HEADS // hg
    w = hg * MLSTM_DH
    nc = seq // MLSTM_L
    n_blocks = D_MODEL // w

    def tok_map(colblock):
        return lambda b, hi: (g0 + b // per_group, b % per_group, colblock * n_blocks + hi)

    def gate_map(half):
        return lambda b, hi: (g0 + b // per_group, b % per_group, half)

    args = [proj, proj, proj, proj, gates, gates, head_g.reshape(1, D_MODEL)]
    in_specs = [
        pl.BlockSpec((None, seq, w), tok_map(0)),
        pl.BlockSpec((None, seq, w), tok_map(1)),
        pl.BlockSpec((None, seq, w), tok_map(2)),
        pl.BlockSpec((None, seq, w), tok_map(3)),
        pl.BlockSpec((None, seq, LANES), gate_map(0)),
        pl.BlockSpec((None, seq, LANES), gate_map(1)),
        pl.BlockSpec((1, w), lambda b, hi: (0, hi)),
    ]
    if init is not None:
        c0, n0, m0 = init
        m0_lanes = jnp.pad(m0.reshape(n_seq, 1, 2 * MLSTM_HEADS), ((0, 0), (0, 0), (0, LANES - 2 * MLSTM_HEADS)))
        args += [c0, n0.reshape(n0.shape[:-1] + (1, MLSTM_DH)), m0_lanes]
        in_specs += [
            pl.BlockSpec((None, None, 2, hg, MLSTM_DH, MLSTM_DH), lambda b, hi: (b, 0, 0, hi, 0, 0)),
            pl.BlockSpec((None, 2, hg, 1, MLSTM_DH), lambda b, hi: (b, 0, hi, 0, 0)),
            pl.BlockSpec((None, 1, LANES), lambda b, hi: (b, 0, 0)),
        ]

    out_shape = [jax.ShapeDtypeStruct((n_seq // per_group, s, D_MODEL), BF16)]
    out_specs = [pl.BlockSpec((None, seq, w), lambda b, hi: (b // per_group, b % per_group, hi))]
    if emit_state:
        out_shape += [
            jax.ShapeDtypeStruct((n_seq, 2, MLSTM_HEADS, MLSTM_DH, MLSTM_DH), F32),
            jax.ShapeDtypeStruct((n_seq, 2, MLSTM_HEADS, 1, MLSTM_DH), F32),
            jax.ShapeDtypeStruct((n_seq, n_hg, 1, LANES), F32),
        ]
        out_specs += [
            pl.BlockSpec((None, 2, hg, MLSTM_DH, MLSTM_DH), lambda b, hi: (b, 0, hi, 0, 0)),
            pl.BlockSpec((None, 2, hg, 1, MLSTM_DH), lambda b, hi: (b, 0, hi, 0, 0)),
            pl.BlockSpec((None, None, 1, LANES), lambda b, hi: (b, hi, 0, 0)),
        ]

    tok_scratch = pltpu.VMEM((seq, LANES), F32)
    return pl.pallas_call(
        functools.partial(_mlstm_kernel, seq=seq, hg=hg, has_init=init is not None, emit_state=emit_state),
        out_shape=tuple(out_shape),
        grid=(n_seq, n_hg),
        in_specs=in_specs,
        out_specs=tuple(out_specs),
        scratch_shapes=[
            pltpu.VMEM((2 * hg, MLSTM_DH, 2 * MLSTM_DH), F32),
            pltpu.VMEM((seq, w), F32),
            pltpu.VMEM((seq, w), F32),
        ] + [tok_scratch] * 7 + [
            pltpu.VMEM((nc, LANES, MLSTM_L), F32),
            pltpu.VMEM((max(nc, 8), LANES), F32),
        ],
        compiler_params=_cparams(("arbitrary", "arbitrary"), HIGH_VMEM_LIMIT),
        name="mlstm_%d" % seq,
    )(*args)


def kernel(x_prompt, x_sample, c, cache_k, cache_v, state_C, state_n, state_m, c_ctx, ada_w, ada_b, ln_g, ln_b, w_in_a, diff_lambda, diff_norm_g, pool_w, pool_scale, w_out_a, ffn_w_gu, ffn_w_down, w_in_c, b_gates_c, mlstm_norm_g, w_out_c, router_w, router_b, moe_w_gu, moe_w_down):
    n_ctx, seq_ctx, d = x_prompt.shape
    n_lat, seq_lat, _ = x_sample.shape
    assert d == D_MODEL and (n_ctx * seq_ctx) % seq_lat == 0 and seq_lat % seq_ctx == 0
    gl = n_lat
    gc = n_ctx * seq_ctx // seq_lat
    s = seq_lat

    x_ctx = x_prompt.reshape(gc, s, d)
    cvec = jnp.concatenate([c, jnp.broadcast_to(c_ctx[None, :], (gc, d))], axis=0)
    mod_all = _modulation(cvec, ada_w, ada_b).reshape(DEPTH, gl + gc, 6, d)

    mod = mod_all[0]
    lam_init = 0.8 - 0.6 * math.exp(-0.3 * 0)
    cos_t, sin_t = _rope_tables(s)
    proj, (ffn_wgu_bf, ffn_wd_bf) = _inproj_a(x_sample, x_ctx, mod, w_in_a[0].astype(BF16), cos_t, sin_t,
                                              (ffn_w_gu, ffn_w_down))
    norm_g = diff_norm_g[0].reshape(1, LANES)
    attn_c, new_k, new_v = _attn_context(proj, diff_lambda[0], norm_g, gl, n_ctx, seq_ctx, lam_init)
    attn_l = _attn_latent(proj, cache_k, cache_v, diff_lambda[0], norm_g, gl, lam_init)
    pool_c = _pool(proj, pool_w[0], pool_scale[0], gl, gc, seq_ctx)
    pool_l = _pool(proj, pool_w[0], pool_scale[0], 0, gl, seq_lat)
    w_out = w_out_a[0].astype(BF16)
    x = _outproj([(attn_l, attn_c), (pool_l, pool_c)], [w_out[:DIFF_WIDTH], w_out[DIFF_WIDTH:]],
                 x_sample, x_ctx, mod, ln_g[0, 0], ln_b[0, 0], 2)
    x, (moe_wgu_bf, moe_wd_bf, w_in_c_bf) = _ffn(x, mod, ffn_wgu_bf[0], ffn_wd_bf[0], ln_g[0, 1], ln_b[0, 1],
                                                  (moe_w_gu[0], moe_w_down[0], w_in_c))

    mod = mod_all[1]
    n_main = 4 * D_MODEL
    nh = MLSTM_HEADS
    wg4 = w_in_c[0][:, n_main:].reshape(d, N_GATES, nh)
    bg4 = b_gates_c[0].reshape(1, N_GATES, nh)
    lane_pad = ((0, 0), (0, LANES - 2 * nh))

    def gate_lanes(a):
        return jnp.concatenate([jnp.pad(jnp.concatenate([a[:, 0], a[:, 2]], axis=-1), lane_pad),
                                jnp.pad(jnp.concatenate([a[:, 1], a[:, 3]], axis=-1), lane_pad)], axis=-1)

    proj, gates = _inproj_c(x, mod, w_in_c_bf[0], n_main, gate_lanes(wg4).astype(BF16), gate_lanes(bg4))
    mix_c, new_c, new_n, new_m = _mlstm(proj, gates, mlstm_norm_g[0], gl, n_ctx, seq_ctx, MLSTM_HEADS,
                                        emit_state=True)
    (mix_l,) = _mlstm(proj, gates, mlstm_norm_g[0], 0, n_lat, seq_lat, MLSTM_HEADS,
                      init=(state_C, state_n[:, 0], state_m[:, 0]))
    x = _outproj([(mix_l, mix_c)], [w_out_c[0].astype(BF16)], x, None, mod, ln_g[1, 0], ln_b[1, 0], 2)
    y_sample, y_ctx = _moe(x, mod, router_w[0], router_b[0], moe_wgu_bf, moe_wd_bf,
                           ln_g[1, 1], ln_b[1, 1], gl)
    y_prompt = y_ctx.reshape(n_ctx, seq_ctx, d)
    new_m = new_m[:, 0, 0, :2 * MLSTM_HEADS].reshape(n_ctx, 2, MLSTM_HEADS)
    return (y_prompt, y_sample, new_k, new_v, new_c[:, None], new_n[..., 0, :][:, None], new_m[:, None])
```

```python
import functools
import math

import jax
import jax.numpy as jnp
import numpy as np
from jax import lax
from jax.experimental import pallas as pl
from jax.experimental.pallas import tpu as pltpu

F32 = jnp.float32
BF16 = jnp.bfloat16

D_MODEL = 1024
GRID_W = 64
ROPE_BASE = 10000.0
DIFF_HEADS = 4
DIFF_DH = 64
DIFF_WIDTH = DIFF_HEADS * 2 * DIFF_DH
POOL_GROUPS = 4
POOL_GC = 128
POOL_WIDTH = POOL_GROUPS * POOL_GC
POOL_WINDOWS = (2, 4, 8, 16)
MLSTM_HEADS = 8
MLSTM_DH = 128
N_GATES = 4
D_FF = 2816
N_EXPERTS = 8
D_FF_EXPERT = 1792
LN_EPS = 1e-5
DEPTH = 2
ALPHA = (2.0 * DEPTH) ** 0.25

LANES = 128
SUBLANES = 8
FF_CHUNK = 256
VMEM_LIMIT = 56 * 1024 * 1024
HIGH_VMEM_LIMIT = 60 * 1024 * 1024


def _cparams(sem, vmem=VMEM_LIMIT):
    return pltpu.CompilerParams(dimension_semantics=sem, vmem_limit_bytes=vmem)


def _resident(shape, index_map):
    return pl.BlockSpec(shape, index_map, pipeline_mode=pl.Buffered(1))


def _layer_norm_rows(z, g, b):
    mu = jnp.mean(z, axis=-1, keepdims=True)
    zc = z - mu
    var = jnp.mean(zc * zc, axis=-1, keepdims=True)
    return zc * lax.rsqrt(var + LN_EPS) * g + b


def _sigmoid(x):
    return 1.0 / (1.0 + jnp.exp(-x))


def _split_bf16(x):
    hi = x.astype(BF16)
    lo = (x - hi.astype(F32)).astype(BF16)
    return hi, lo


def _cast_rider_specs(weights, n_steps, per):
    specs = []
    for w in weights:
        e, rows, cols = w.shape
        per_e = next((k for k in range(n_steps // e, 0, -1)
                      if n_steps % (e * k) == 0 and rows % (2 * SUBLANES * k) == 0), None)
        if per_e is None:
            return None
        hold = n_steps // (e * per_e)

        def slab_map(gi, ti, per_e=per_e, hold=hold):
            slab = (gi * per + ti) // hold
            return (slab // per_e, slab % per_e, 0)

        specs.append(pl.BlockSpec((None, rows // per_e, cols), slab_map))
    return specs


def _cast_slabs(cast_in, cast_out):
    for src, dst in zip(cast_in, cast_out):
        dst[...] = src[...].astype(BF16)


def _mod_kernel(c_ref, w_ref, b_ref, o_ref):
    c = c_ref[...]
    h = (c * _sigmoid(c)).astype(BF16)
    o_ref[...] = jnp.dot(h, w_ref[...].astype(BF16), preferred_element_type=F32) + b_ref[...]


def _modulation(cvec, ada_w, ada_b):
    depth, d, n = ada_w.shape
    g = cvec.shape[0]
    tn = 1536
    return pl.pallas_call(
        _mod_kernel,
        out_shape=jax.ShapeDtypeStruct((depth, g, n), F32),
        grid=(depth, n // tn),
        in_specs=[
            pl.BlockSpec((g, d), lambda l, j: (0, 0)),
            pl.BlockSpec((None, d, tn), lambda l, j: (l, 0, j)),
            pl.BlockSpec((None, 1, tn), lambda l, j: (l, 0, j)),
        ],
        out_specs=pl.BlockSpec((None, g, tn), lambda l, j: (l, 0, j)),
        compiler_params=_cparams(("arbitrary", "arbitrary")),
        name="modulation",
    )(cvec, ada_w, ada_b.reshape(depth, 1, n))


def _rot_half16(x):
    lane = lax.broadcasted_iota(jnp.int32, x.shape, 1)
    return jnp.where((lane % 32) < 16, pltpu.roll(x, LANES - 16, 1), pltpu.roll(x, 16, 1))


def _two_stream_specs(tm, d, gl, ctx_first_group=0):
    return [pl.BlockSpec((None, tm, d), lambda gi, ti: (jnp.minimum(gi, gl - 1), jnp.where(gi < gl, ti, 0), 0)),
            pl.BlockSpec((None, tm, d), lambda gi, ti: (ctx_first_group + jnp.maximum(gi - gl, 0),
                                                        jnp.where(gi < gl, 0, ti), 0))]


def _inproj_a_kernel(xl_ref, xc_ref, mod_ref, w_ref, cos_ref, sin_ref, *rest, n_latent_groups, n_cast):
    cast_in, o_ref, cast_out = rest[:n_cast], rest[n_cast], rest[n_cast + 1:]
    _cast_slabs(cast_in, cast_out)
    x = jnp.where(pl.program_id(0) < n_latent_groups, xl_ref[...], xc_ref[...])
    h = x * (1.0 + mod_ref[1:2, :]) + mod_ref[0:1, :]
    p = jnp.dot(h.astype(BF16), w_ref[...], preferred_element_type=F32)
    cos = cos_ref[...]
    sin = sin_ref[...]
    n_rope = 2 * DIFF_WIDTH // LANES
    for j in range(n_rope):
        blk = p[:, j * LANES:(j + 1) * LANES]
        o_ref[:, j * LANES:(j + 1) * LANES] = blk * cos + _rot_half16(blk) * sin
    o_ref[:, n_rope * LANES:] = p[:, n_rope * LANES:]


def _inproj_a(x_lat, x_ctx, mod, w_bf, cos_t, sin_t, cast_weights=(), tm=512):
    n_latent_groups, s, d = x_lat.shape
    g = n_latent_groups + x_ctx.shape[0]
    n = w_bf.shape[1]
    per = s // tm
    cast_specs = _cast_rider_specs(cast_weights, g * per, per)
    if cast_specs is None:
        proj, _ = _inproj_a(x_lat, x_ctx, mod, w_bf, cos_t, sin_t, (), tm)
        return proj, tuple(w.astype(BF16) for w in cast_weights)

    def table_map(gi, ti):
        return (jnp.where(gi >= n_latent_groups, 1, 0), ti, 0)

    outs = pl.pallas_call(
        functools.partial(_inproj_a_kernel, n_latent_groups=n_latent_groups, n_cast=len(cast_weights)),
        out_shape=(jax.ShapeDtypeStruct((g, s, n), F32),) + tuple(
            jax.ShapeDtypeStruct(w.shape, BF16) for w in cast_weights),
        grid=(g, per),
        in_specs=_two_stream_specs(tm, d, n_latent_groups) + [
            pl.BlockSpec((None, 6, d), lambda gi, ti: (gi, 0, 0)),
            _resident((d, n), lambda gi, ti: (0, 0)),
            pl.BlockSpec((None, tm, LANES), table_map),
            pl.BlockSpec((None, tm, LANES), table_map),
        ] + cast_specs,
        out_specs=(pl.BlockSpec((None, tm, n), lambda gi, ti: (gi, ti, 0)),) + tuple(cast_specs),
        compiler_params=_cparams(("arbitrary", "arbitrary")),
        name="inproj_a",
    )(x_lat, x_ctx, mod, w_bf, cos_t, sin_t, *cast_weights)
    return outs[0], outs[1:]


def _rope_tables(n_tokens):
    rows = n_tokens // GRID_W
    row_pos = np.repeat(np.arange(rows), GRID_W).astype(np.float64)
    col_pos = np.tile(np.arange(GRID_W), rows).astype(np.float64)
    n_freq = DIFF_DH // 4
    inv_freq = np.power(ROPE_BASE, -np.arange(n_freq, dtype=np.float64) / n_freq)
    ang = np.stack([row_pos[:, None] * inv_freq, col_pos[:, None] * inv_freq], axis=1)
    cos, sin = np.cos(ang), np.sin(ang)
    cos64 = np.concatenate([cos[:, 0], cos[:, 0], cos[:, 1], cos[:, 1]], axis=-1)
    sin64 = np.concatenate([-sin[:, 0], sin[:, 0], -sin[:, 1], sin[:, 1]], axis=-1)
    cos_l = np.tile(cos64, (1, LANES // DIFF_DH))
    sin_l = np.tile(sin64, (1, LANES // DIFF_DH))
    cos_t = np.stack([cos_l, np.ones_like(cos_l)])
    sin_t = np.stack([sin_l, np.zeros_like(sin_l)])
    return jnp.asarray(cos_t, F32), jnp.asarray(sin_t, F32)


def _diff_attn_kernel(*refs, n_pieces, n_heads, lam_init, emit_kv):
    lam_ref, ng_ref, q_ref = refs[:3]
    kv_refs = refs[3:3 + 2 * n_pieces]
    o_ref = refs[3 + 2 * n_pieces]

    lp = lam_ref[...]
    lam = (jnp.exp(jnp.sum(lp[0:1] * lp[1:2], axis=-1, keepdims=True))
           - jnp.exp(jnp.sum(lp[2:3] * lp[3:4], axis=-1, keepdims=True)) + lam_init)
    nt = (((1,), (1,)), ((), ()))

    def softmax_pieces(ss):
        m = functools.reduce(jnp.maximum, [jnp.max(s, axis=-1, keepdims=True) for s in ss])
        es = [jnp.exp(s - m) for s in ss]
        l = functools.reduce(jnp.add, [jnp.sum(e, axis=-1, keepdims=True) for e in es])
        return [e / l for e in es]

    for h in range(n_heads):
        hs = slice(h * LANES, (h + 1) * LANES)
        q = q_ref[:, hs] * (DIFF_DH ** -0.5)
        lane = lax.broadcasted_iota(jnp.int32, q.shape, 1)
        q1 = jnp.where(lane < DIFF_DH, q, 0.0).astype(BF16)
        q2 = jnp.where(lane >= DIFF_DH, q, 0.0).astype(BF16)
        s1, s2, vs = [], [], []
        for i in range(n_pieces):
            kb = kv_refs[2 * i][:, hs].astype(BF16)
            vs.append(kv_refs[2 * i + 1][:, hs].astype(BF16))
            s1.append(lax.dot_general(q1, kb, nt, preferred_element_type=F32))
            s2.append(lax.dot_general(q2, kb, nt, preferred_element_type=F32))
        p1 = softmax_pieces(s1)
        p2 = softmax_pieces(s2)
        o = None
        for i in range(n_pieces):
            a = (p1[i] - lam * p2[i]).astype(BF16)
            t = jnp.dot(a, vs[i], preferred_element_type=F32)
            o = t if o is None else o + t
        o = o * lax.rsqrt(jnp.mean(o * o, axis=-1, keepdims=True) + LN_EPS)
        o_ref[:, hs] = (o * ng_ref[...] * (1.0 - lam_init)).astype(o_ref.dtype)
        if emit_kv:
            ko_ref, vo_ref = refs[4 + 2 * n_pieces:]
            ko_ref[h] = kv_refs[0][:, hs]
            vo_ref[h] = kv_refs[1][:, hs]


def _attn_context(proj, lam_p, norm_g, n_latent_groups, n_seq, seq, lam_init):
    g, s, _ = proj.shape
    per_group = s // seq
    blk = (None, seq, DIFF_WIDTH)

    def tok_map(colblock):
        return lambda b: (n_latent_groups + b // per_group, b % per_group, colblock)

    cache_shape = jax.ShapeDtypeStruct((n_seq, 1, DIFF_HEADS, seq, LANES), F32)
    cache_spec = pl.BlockSpec((None, None, DIFF_HEADS, seq, LANES), lambda b: (b, 0, 0, 0, 0))
    out_spec = pl.BlockSpec(blk, lambda b: (b // per_group, b % per_group, 0))
    return pl.pallas_call(
        functools.partial(_diff_attn_kernel, n_pieces=1, n_heads=DIFF_HEADS, lam_init=lam_init, emit_kv=True),
        out_shape=(jax.ShapeDtypeStruct((g - n_latent_groups, s, DIFF_WIDTH), BF16), cache_shape, cache_shape),
        grid=(n_seq,),
        in_specs=[
            pl.BlockSpec((4, DIFF_DH), lambda b: (0, 0)),
            pl.BlockSpec((1, LANES), lambda b: (0, 0)),
            pl.BlockSpec(blk, tok_map(0)),
            pl.BlockSpec(blk, tok_map(1)),
            pl.BlockSpec(blk, tok_map(2)),
        ],
        out_specs=(out_spec, cache_spec, cache_spec),
        compiler_params=_cparams(("arbitrary",)),
        name="attn_context",
    )(lam_p, norm_g, proj, proj, proj)


def _attn_latent(proj, cache_k, cache_v, lam_p, norm_g, n_latent_groups, lam_init, tq=256):
    g, s, _ = proj.shape
    past = cache_k.shape[3]
    cache_spec = pl.BlockSpec((None, None, None, past, LANES), lambda b, h, qi: (b, 0, h, 0, 0))
    return pl.pallas_call(
        functools.partial(_diff_attn_kernel, n_pieces=2, n_heads=1, lam_init=lam_init, emit_kv=False),
        out_shape=jax.ShapeDtypeStruct((n_latent_groups, s, DIFF_WIDTH), BF16),
        grid=(n_latent_groups, DIFF_HEADS, s // tq),
        in_specs=[
            pl.BlockSpec((4, DIFF_DH), lambda b, h, qi: (0, 0)),
            pl.BlockSpec((1, LANES), lambda b, h, qi: (0, 0)),
            pl.BlockSpec((None, tq, LANES), lambda b, h, qi: (b, qi, h)),
            cache_spec,
            cache_spec,
            pl.BlockSpec((None, s, LANES), lambda b, h, qi: (b, 0, DIFF_HEADS + h)),
            pl.BlockSpec((None, s, LANES), lambda b, h, qi: (b, 0, 2 * DIFF_HEADS + h)),
        ],
        out_specs=pl.BlockSpec((None, tq, LANES), lambda b, h, qi: (b, qi, h)),
        compiler_params=_cparams(("arbitrary", "arbitrary", "arbitrary")),
        name="attn_latent",
    )(lam_p, norm_g, proj, cache_k, cache_v, proj, proj)


POOL_ROW_BLOCK = 256
POOL_COL_WINDOW = 512
assert (POOL_COL_WINDOW - POOL_ROW_BLOCK) // 2 >= max(POOL_WINDOWS) // 2


def _pool_kernel(p_ref, w_ref, sc_ref, o_ref, band_ref, *, seq):
    @pl.when((pl.program_id(0) == 0) & (pl.program_id(1) == 0))
    def _():
        t = lax.broadcasted_iota(jnp.int32, (seq, seq), 0)
        s_ = lax.broadcasted_iota(jnp.int32, (seq, seq), 1)
        for gi, w in enumerate(POOL_WINDOWS):
            inside = (s_ >= t - w // 2) & (s_ <= t + w // 2 - 1)
            band_ref[gi] = jnp.where(inside, 1.0, 0.0).astype(BF16)

    tcol = lax.broadcasted_iota(jnp.int32, (seq, 1), 0)
    for gi, w in enumerate(POOL_WINDOWS):
        u = p_ref[:, gi * POOL_GC:(gi + 1) * POOL_GC]
        hi, lo = _split_bf16(u)
        rb = min(seq, POOL_ROW_BLOCK)
        cw = min(seq, POOL_COL_WINDOW)
        blocks = []
        for i in range(seq // rb):
            c0 = min(max(i * rb - (cw - rb) // 2, 0), seq - cw)
            band = band_ref[gi, i * rb:(i + 1) * rb, c0:c0 + cw]
            blocks.append(jnp.dot(band, hi[c0:c0 + cw], preferred_element_type=F32)
                          + jnp.dot(band, lo[c0:c0 + cw], preferred_element_type=F32))
        win = blocks[0] if len(blocks) == 1 else jnp.concatenate(blocks, axis=0)
        cnt = (jnp.minimum(tcol + (w // 2 - 1), seq - 1) - jnp.maximum(tcol - w // 2, 0) + 1).astype(F32)
        pooled = win / cnt - u
        mixed = jnp.dot(pooled.astype(BF16), w_ref[gi].astype(BF16), preferred_element_type=F32)
        o_ref[:, gi * POOL_GC:(gi + 1) * POOL_GC] = (
            mixed * sc_ref[:, gi * POOL_GC:(gi + 1) * POOL_GC]).astype(o_ref.dtype)


def _pool(proj, pool_w, pool_scale, g0, n_groups, seq):
    g, s, _ = proj.shape
    col = 3 * DIFF_WIDTH // POOL_WIDTH
    return pl.pallas_call(
        functools.partial(_pool_kernel, seq=seq),
        out_shape=jax.ShapeDtypeStruct((n_groups, s, POOL_WIDTH), BF16),
        grid=(n_groups, s // seq),
        in_specs=[
            pl.BlockSpec((None, seq, POOL_WIDTH), lambda gi, ti: (g0 + gi, ti, col)),
            pl.BlockSpec((POOL_GROUPS, POOL_GC, POOL_GC), lambda gi, ti: (0, 0, 0)),
            pl.BlockSpec((1, POOL_WIDTH), lambda gi, ti: (0, 0)),
        ],
        out_specs=pl.BlockSpec((None, seq, POOL_WIDTH), lambda gi, ti: (gi, ti, 0)),
        scratch_shapes=[pltpu.VMEM((POOL_GROUPS, seq, seq), BF16)],
        compiler_params=_cparams(("arbitrary", "arbitrary")),
        name="pool_%d" % seq,
    )(proj, pool_w, pool_scale.reshape(1, POOL_WIDTH))


def _outproj_kernel(*refs, n_in, gate_row, n_latent_groups):
    a_refs = refs[:2 * n_in]
    w_refs = refs[2 * n_in:3 * n_in]
    xl_ref, xc_ref, mod_ref, g_ref, b_ref, o_ref = refs[3 * n_in:]
    is_latent = pl.program_id(0) < n_latent_groups
    acc = None
    for i, w_ref in enumerate(w_refs):
        a = jnp.where(is_latent, a_refs[2 * i][...], a_refs[2 * i + 1][...])
        t = jnp.dot(a, w_ref[...], preferred_element_type=F32)
        acc = t if acc is None else acc + t
    x = jnp.where(is_latent, xl_ref[...], xc_ref[...])
    z = ALPHA * x + mod_ref[gate_row:gate_row + 1, :] * acc
    o_ref[...] = _layer_norm_rows(z, g_ref[...], b_ref[...])


def _outproj(acts, weights, x_lat, x_ctx, mod, ln_g, ln_b, gate_row, tm=512):
    gl = acts[0][0].shape[0]
    _, s, d = x_lat.shape
    if x_ctx is None:
        g = x_lat.shape[0]
        x_ctx, x_specs = x_lat, _two_stream_specs(tm, d, gl, gl)
    else:
        g = gl + x_ctx.shape[0]
        x_specs = _two_stream_specs(tm, d, gl)
    n_in = len(acts)
    in_specs = []
    flat_acts = []
    for a_lat, a_ctx in acts:
        in_specs += _two_stream_specs(tm, a_lat.shape[-1], gl)
        flat_acts += [a_lat, a_ctx]
    in_specs += [_resident(w.shape, lambda gi, ti: (0, 0)) for w in weights]
    in_specs += x_specs
    in_specs += [
        pl.BlockSpec((None, 6, d), lambda gi, ti: (gi, 0, 0)),
        pl.BlockSpec((1, d), lambda gi, ti: (0, 0)),
        pl.BlockSpec((1, d), lambda gi, ti: (0, 0)),
    ]
    return pl.pallas_call(
        functools.partial(_outproj_kernel, n_in=n_in, gate_row=gate_row, n_latent_groups=gl),
        out_shape=jax.ShapeDtypeStruct((g, s, d), F32),
        grid=(g, s // tm),
        in_specs=in_specs,
        out_specs=pl.BlockSpec((None, tm, d), lambda gi, ti: (gi, ti, 0)),
        compiler_params=_cparams(("arbitrary", "arbitrary")),
        name="outproj",
    )(*flat_acts, *weights, x_lat, x_ctx, mod, ln_g.reshape(1, d), ln_b.reshape(1, d))


def _swiglu_chunks(h_bf, wgu_ref, wd_ref, d_ff):
    acc = None
    for j in range(d_ff // FF_CHUNK):
        lo = j * FF_CHUNK
        gate = jnp.dot(h_bf, wgu_ref[:, lo:lo + FF_CHUNK], preferred_element_type=F32)
        up = jnp.dot(h_bf, wgu_ref[:, d_ff + lo:d_ff + lo + FF_CHUNK], preferred_element_type=F32)
        act = (gate * _sigmoid(gate) * up).astype(BF16)
        t = jnp.dot(act, wd_ref[lo:lo + FF_CHUNK, :], preferred_element_type=F32)
        acc = t if acc is None else acc + t
    return acc


def _ffn_kernel(x_ref, mod_ref, wgu_ref, wd_ref, g_ref, b_ref, *rest, n_cast):
    cast_in, o_ref, cast_out = rest[:n_cast], rest[n_cast], rest[n_cast + 1:]
    _cast_slabs(cast_in, cast_out)
    x = x_ref[...]
    h = (x * (1.0 + mod_ref[4:5, :]) + mod_ref[3:4, :]).astype(BF16)
    acc = _swiglu_chunks(h, wgu_ref, wd_ref, D_FF)
    z = ALPHA * x + mod_ref[5:6, :] * acc
    o_ref[...] = _layer_norm_rows(z, g_ref[...], b_ref[...])


def _ffn(x, mod, wgu_bf, wd_bf, ln_g, ln_b, cast_weights=(), tm=512):
    g, s, d = x.shape
    per = s // tm
    cast_specs = _cast_rider_specs(cast_weights, g * per, per)
    if cast_specs is None:
        out, _ = _ffn(x, mod, wgu_bf, wd_bf, ln_g, ln_b, (), tm)
        return out, tuple(w.astype(BF16) for w in cast_weights)
    outs = pl.pallas_call(
        functools.partial(_ffn_kernel, n_cast=len(cast_weights)),
        out_shape=(jax.ShapeDtypeStruct((g, s, d), F32),) + tuple(
            jax.ShapeDtypeStruct(w.shape, BF16) for w in cast_weights),
        grid=(g, per),
        in_specs=[
            pl.BlockSpec((None, tm, d), lambda gi, ti: (gi, ti, 0)),
            pl.BlockSpec((None, 6, d), lambda gi, ti: (gi, 0, 0)),
            _resident(wgu_bf.shape, lambda gi, ti: (0, 0)),
            _resident(wd_bf.shape, lambda gi, ti: (0, 0)),
            pl.BlockSpec((1, d), lambda gi, ti: (0, 0)),
            pl.BlockSpec((1, d), lambda gi, ti: (0, 0)),
        ] + cast_specs,
        out_specs=(pl.BlockSpec((None, tm, d), lambda gi, ti: (gi, ti, 0)),) + tuple(cast_specs),
        compiler_params=_cparams(("arbitrary", "arbitrary")),
        name="ffn",
    )(x, mod, wgu_bf, wd_bf, ln_g.reshape(1, d), ln_b.reshape(1, d), *cast_weights)
    return outs[0], outs[1:]


MOE_BM = 512
MOE_SB = 256
MOE_TC = 256
MOE_WINDOW = 5
META_LANES = LANES


def _router_top2(h, rw_ref, rb_ref):
    h_hi, h_lo = _split_bf16(h)
    w_hi, w_lo = _split_bf16(rw_ref[...])
    logits = (jnp.dot(h_hi, w_hi, preferred_element_type=F32)
              + jnp.dot(h_lo, w_hi, preferred_element_type=F32)
              + jnp.dot(h_hi, w_lo, preferred_element_type=F32)) + rb_ref[...]
    lane = lax.broadcasted_iota(jnp.int32, logits.shape, 1).astype(F32)
    neg = -jnp.inf
    logits = jnp.where(lane < N_EXPERTS, logits, neg)
    m1 = jnp.max(logits, axis=-1, keepdims=True)
    i1 = jnp.min(jnp.where(logits == m1, lane, float(LANES)), axis=-1, keepdims=True)
    rest = jnp.where(lane == i1, neg, logits)
    m2 = jnp.max(rest, axis=-1, keepdims=True)
    i2 = jnp.min(jnp.where(rest == m2, lane, float(LANES)), axis=-1, keepdims=True)
    e2 = jnp.exp(m2 - m1)
    return lane, i1, i2, 1.0 / (1.0 + e2), e2 / (1.0 + e2)


def _route_kernel(x_ref, mod_ref, rw_ref, rb_ref, h_ref, wk_ref, srank_ref, cnt_ref, tri_ref, run_ref):
    tm, d = x_ref.shape

    @pl.when(pl.program_id(0) == 0)
    def _():
        r = lax.broadcasted_iota(jnp.int32, (tm, tm), 0)
        c = lax.broadcasted_iota(jnp.int32, (tm, tm), 1)
        tri_ref[...] = jnp.where(c <= r, 1.0, 0.0).astype(BF16)
        run_ref[...] = jnp.zeros_like(run_ref)

    h = x_ref[...] * (1.0 + mod_ref[4:5, :]) + mod_ref[3:4, :]
    h_ref[:, 0:d] = h.astype(BF16)
    lane, i1, i2, w1, w2 = _router_top2(h, rw_ref, rb_ref)
    first_is_low = i1 < i2
    e_hi = jnp.where(first_is_low, i2, i1)
    wk_ref[...] = jnp.where(lane == 0.0, jnp.where(first_is_low, w1, w2),
                            jnp.where(lane == 1.0, jnp.where(first_is_low, w2, w1), 0.0))
    tok = (pl.program_id(0) * tm + lax.broadcasted_iota(jnp.int32, (tm, META_LANES), 0))
    meta = jnp.where(lane == 0.0, (tok // 128).astype(F32),
                     jnp.where(lane == 1.0, (tok % 128).astype(F32),
                               jnp.where(lane == 2.0, e_hi, jnp.where(lane == 3.0, 1.0, 0.0))))
    h_ref[:, d:d + META_LANES] = meta.astype(BF16)

    member = (lane == i1) | (lane == i2)
    mem = jnp.where(member, 1.0, 0.0)
    rank = jnp.dot(tri_ref[...], mem.astype(BF16), preferred_element_type=F32) + run_ref[...]
    srank_ref[...] = jnp.where(member, rank, -rank).T[0:SUBLANES, :]
    run_ref[...] = rank[tm - 1:tm, :]
    cnt_ref[...] = rank[tm - 1:tm, :]


def _route(x, mod, router_w, router_b, tm=512):
    g, s, d = x.shape
    n = g * s
    assert n <= 128 * 256
    assert n >= MOE_WINDOW * MOE_TC
    per = s // tm
    rw = jnp.pad(router_w, ((0, 0), (0, LANES - N_EXPERTS)))
    rb = jnp.pad(router_b, (0, LANES - N_EXPERTS)).reshape(1, LANES)
    return pl.pallas_call(
        _route_kernel,
        out_shape=(jax.ShapeDtypeStruct((n, d + META_LANES), BF16), jax.ShapeDtypeStruct((n, LANES), F32),
                   jax.ShapeDtypeStruct((SUBLANES, n), F32), jax.ShapeDtypeStruct((1, LANES), F32)),
        grid=(n // tm,),
        in_specs=[
            pl.BlockSpec((None, tm, d), lambda i: (i // per, i % per, 0)),
            pl.BlockSpec((None, 6, d), lambda i: (i // per, 0, 0)),
            pl.BlockSpec((d, LANES), lambda i: (0, 0)),
            pl.BlockSpec((1, LANES), lambda i: (0, 0)),
        ],
        out_specs=(pl.BlockSpec((tm, d + META_LANES), lambda i: (i, 0)), pl.BlockSpec((tm, LANES), lambda i: (i, 0)),
                   pl.BlockSpec((SUBLANES, tm), lambda i: (0, i)), pl.BlockSpec((1, LANES), lambda i: (0, 0))),
        scratch_shapes=[pltpu.VMEM((tm, tm), BF16), pltpu.VMEM((1, LANES), F32)],
        compiler_params=_cparams(("arbitrary",)),
        name="moe_route",
    )(x, mod, rw, rb)


def _moe_plan(srank, counts, n_blocks):
    e_n = N_EXPERTS
    i32 = jnp.int32
    cnt = counts[0, :e_n].astype(i32)
    nb = (cnt + MOE_BM - 1) // MOE_BM
    nb_incl = jnp.cumsum(nb)
    gstart = nb_incl - nb
    n_used = nb_incl[-1]
    sr = srank[:e_n]
    rank = jnp.abs(sr).astype(i32)
    pos_row = jnp.where(sr > 0, rank - 1 + MOE_BM * gstart[:, None], -1)
    chunk_end = rank[:, MOE_TC - 1::MOE_TC]

    rc = jnp.minimum(jnp.arange(n_blocks, dtype=i32), n_used - 1)
    e_r = jnp.minimum(jnp.sum(nb_incl[None, :] <= rc[:, None], axis=1, dtype=i32), e_n - 1)

    sb_per = MOE_BM // MOE_SB
    q = jnp.arange(n_blocks * sb_per, dtype=i32)
    r_q = jnp.minimum(q // sb_per, n_used - 1)
    e_q = e_r[r_q]
    first = ((r_q - gstart[e_q]) * sb_per + q % sb_per) * MOE_SB
    used = (q // sb_per < n_used) & (first < cnt[e_q])
    hi = jnp.minimum(first + MOE_SB, cnt[e_q])
    ends_q = chunk_end[e_q]
    jlo = jnp.where(used, jnp.sum(ends_q < (first + 1)[:, None], axis=1, dtype=i32), 0)
    jhi = jnp.where(used, jnp.sum(ends_q < hi[:, None], axis=1, dtype=i32), -1)
    return pos_row, e_r, jlo, jhi, n_used.reshape(1)


def _experts_kernel(be_ref, jlo_ref, jhi_ref, nused_ref, h_ref, pos_ref, wgu_ref, wd_ref, y_ref, meta_ref, xg_ref):
    r = pl.program_id(0)
    d = y_ref.shape[1]

    @pl.when(r < nused_ref[0])
    def _():
        e = be_ref[r]
        sb_per = MOE_BM // MOE_SB
        n_chunks = h_ref.shape[0] // MOE_TC
        for u in range(sb_per):
            sb_rows = slice(u * MOE_SB, (u + 1) * MOE_SB)
            slot = r * MOE_BM + u * MOE_SB + lax.broadcasted_iota(jnp.int32, (MOE_SB, MOE_TC), 0)

            def onehot(j, slot=slot):
                return jnp.where(pos_ref[e, pl.ds(j, 1), :] == slot, 1.0, 0.0).astype(BF16)

            j0 = jnp.minimum(jlo_ref[r * sb_per + u], n_chunks - MOE_WINDOW)
            window = jnp.concatenate([onehot(j0 + j) for j in range(MOE_WINDOW)], axis=-1)
            rows = h_ref[pl.ds(pl.multiple_of(j0 * MOE_TC, MOE_TC), MOE_WINDOW * MOE_TC), :]
            xg_ref[sb_rows, :] = jnp.dot(window, rows, preferred_element_type=F32)

            def chunk(j, carry, sb_rows=sb_rows, onehot=onehot):
                rows = h_ref[pl.ds(pl.multiple_of(j * MOE_TC, MOE_TC), MOE_TC), :]
                xg_ref[sb_rows, :] += jnp.dot(onehot(j), rows, preferred_element_type=F32)
                return carry

            lax.fori_loop(j0 + MOE_WINDOW, jhi_ref[r * sb_per + u] + 1, chunk, 0)
        meta_ref[...] = xg_ref[:, d:d + META_LANES].T[0:SUBLANES, :]
        y_ref[...] = _swiglu_chunks(xg_ref[:, 0:d].astype(BF16), wgu_ref, wd_ref, D_FF_EXPERT)

    @pl.when(r >= nused_ref[0])
    def _():
        y_ref[...] = jnp.zeros_like(y_ref)
        meta_ref[...] = jnp.zeros_like(meta_ref)


def _experts(h, pos_row, plan, wgu_bf, wd_bf, n_blocks):
    n, dx = h.shape
    d = dx - META_LANES
    e_r, jlo, jhi, n_used = plan
    grid_spec = pltpu.PrefetchScalarGridSpec(
        num_scalar_prefetch=4,
        grid=(n_blocks,),
        in_specs=[
            _resident((n, dx), lambda r, be, lo, hi, nu: (0, 0)),
            _resident(pos_row.shape, lambda r, be, lo, hi, nu: (0, 0, 0)),
            pl.BlockSpec((None, d, 2 * D_FF_EXPERT), lambda r, be, lo, hi, nu: (be[r], 0, 0),
                         pipeline_mode=pl.Buffered(1)),
            pl.BlockSpec((None, D_FF_EXPERT, d), lambda r, be, lo, hi, nu: (be[r], 0, 0),
                         pipeline_mode=pl.Buffered(1)),
        ],
        out_specs=(pl.BlockSpec((MOE_BM, d), lambda r, be, lo, hi, nu: (r, 0)),
                   pl.BlockSpec((None, SUBLANES, MOE_BM), lambda r, be, lo, hi, nu: (r, 0, 0))),
        scratch_shapes=[pltpu.VMEM((MOE_BM, dx), F32)],
    )
    return pl.pallas_call(
        _experts_kernel,
        out_shape=(jax.ShapeDtypeStruct((n_blocks * MOE_BM, d), F32),
                   jax.ShapeDtypeStruct((n_blocks, SUBLANES, MOE_BM), F32)),
        grid_spec=grid_spec,
        compiler_params=_cparams(("arbitrary",), HIGH_VMEM_LIMIT),
        name="moe_experts",
    )(e_r, jlo, jhi, n_used, h, pos_row, wgu_bf, wd_bf)


def _scatter_kernel(dest_ref, y_ref, o_ref, ybuf_ref, sem):
    r = pl.program_id(0)
    n_steps = pl.num_programs(0)
    buf = r % 2

    def wait_block(b):
        pltpu.make_async_copy(ybuf_ref.at[b], o_ref.at[pl.ds(0, MOE_BM), :], sem.at[b]).wait()

    @pl.when(r >= 2)
    def _():
        wait_block(buf)

    def copy_out(b):
        ybuf_ref[b] = y_ref[...]
        for i in range(MOE_BM):
            pltpu.make_async_copy(ybuf_ref.at[b, pl.ds(i, 1), :], o_ref.at[pl.ds(dest_ref[0, i], 1), :],
                                  sem.at[b]).start()

    for b in range(2):
        pl.when(buf == b)(functools.partial(copy_out, b))

    @pl.when(r == n_steps - 1)
    def _():
        wait_block(1 - buf)
        wait_block(buf)


def _scatter_rows(y, dest, n_rows_out):
    n_slots, d = y.shape
    n_blocks = n_slots // MOE_BM
    assert n_blocks >= 2
    return pl.pallas_call(
        _scatter_kernel,
        out_shape=jax.ShapeDtypeStruct((n_rows_out, d), F32),
        grid=(n_blocks,),
        in_specs=[
            pl.BlockSpec((None, 1, MOE_BM), lambda r: (r, 0, 0), memory_space=pltpu.SMEM),
            pl.BlockSpec((MOE_BM, d), lambda r: (r, 0)),
        ],
        out_specs=pl.BlockSpec(memory_space=pl.ANY),
        scratch_shapes=[pltpu.VMEM((2, MOE_BM, d), F32), pltpu.SemaphoreType.DMA((2,))],
        compiler_params=_cparams(("arbitrary",)),
        name="moe_scatter",
    )(dest.reshape(n_blocks, 1, MOE_BM), y)


def _finish_kernel(a_ref, b_ref, wk_ref, x_ref, mod_ref, g_ref, bb_ref, ol_ref, oc_ref, *, n_latent_tiles):
    lane = lax.broadcasted_iota(jnp.int32, wk_ref.shape, 1)
    wk = wk_ref[...]
    w0 = jnp.sum(jnp.where(lane == 0, wk, 0.0), axis=-1, keepdims=True)
    w1 = jnp.sum(jnp.where(lane == 1, wk, 0.0), axis=-1, keepdims=True)
    acc = w0 * a_ref[...] + w1 * b_ref[...]
    z = ALPHA * x_ref[...] + mod_ref[5:6, :] * acc
    out = _layer_norm_rows(z, g_ref[...], bb_ref[...])
    is_latent = pl.program_id(0) < n_latent_tiles

    @pl.when(is_latent)
    def _():
        ol_ref[...] = out

    @pl.when(jnp.logical_not(is_latent))
    def _():
        oc_ref[...] = out


def _finish(rows, wk, x, mod, ln_g, ln_b, n_latent_groups, tm=512):
    g, s, d = x.shape
    n = g * s
    per = s // tm
    nt = n // tm
    nlt = n_latent_groups * per

    def lat_map(i):
        j = jnp.minimum(i, nlt - 1)
        return (j // per, j % per, 0)

    def ctx_map(i):
        j = jnp.maximum(i - nlt, 0)
        return (j // per, j % per, 0)

    return pl.pallas_call(
        functools.partial(_finish_kernel, n_latent_tiles=nlt),
        out_shape=(jax.ShapeDtypeStruct((n_latent_groups, s, d), F32),
                   jax.ShapeDtypeStruct((g - n_latent_groups, s, d), F32)),
        grid=(nt,),
        in_specs=[
            pl.BlockSpec((tm, d), lambda i: (i, 0)),
            pl.BlockSpec((tm, d), lambda i: (nt + i, 0)),
            pl.BlockSpec((tm, LANES), lambda i: (i, 0)),
            pl.BlockSpec((None, tm, d), lambda i: (i // per, i % per, 0)),
            pl.BlockSpec((None, 6, d), lambda i: (i // per, 0, 0)),
            pl.BlockSpec((1, d), lambda i: (0, 0)),
            pl.BlockSpec((1, d), lambda i: (0, 0)),
        ],
        out_specs=(pl.BlockSpec((None, tm, d), lat_map), pl.BlockSpec((None, tm, d), ctx_map)),
        compiler_params=_cparams(("arbitrary",)),
        name="moe_finish",
    )(rows, rows, wk, x, mod, ln_g.reshape(1, d), ln_b.reshape(1, d))


def _moe(x, mod, router_w, router_b, wgu_bf, wd_bf, ln_g, ln_b, n_latent_groups):
    g, s, d = x.shape
    n = g * s
    n_blocks = 2 * n // MOE_BM + N_EXPERTS
    n_slots = n_blocks * MOE_BM
    h, wk, srank, counts = _route(x, mod, router_w, router_b)
    pos_row, e_r, jlo, jhi, n_used = _moe_plan(srank, counts, n_blocks)
    pos_row = pos_row.reshape(N_EXPERTS, n // MOE_TC, MOE_TC)
    y, meta = _experts(h, pos_row, (e_r, jlo, jhi, n_used), wgu_bf, wd_bf, n_blocks)
    tok = (meta[:, 0] * 128.0 + meta[:, 1]).astype(jnp.int32)
    choice = (meta[:, 2] == e_r[:, None].astype(F32)).astype(jnp.int32)
    unused = meta[:, 3] < 0.5
    spare = (2 * n + jnp.cumsum(unused.reshape(-1).astype(jnp.int32)) - 1).reshape(n_blocks, MOE_BM)
    dest = jnp.where(unused, spare, choice * n + tok)
    rows = _scatter_rows(y, dest, n_slots)
    return _finish(rows, wk, x, mod, ln_g, ln_b, n_latent_groups)


def _inproj_c_kernel(x_ref, mod_ref, w_ref, wg_ref, bg_ref, o_ref, og_ref):
    h = (x_ref[...] * (1.0 + mod_ref[1:2, :]) + mod_ref[0:1, :]).astype(BF16)
    o_ref[...] = jnp.dot(h, w_ref[:, 0:o_ref.shape[1]], preferred_element_type=F32)
    og_ref[...] = jnp.dot(h, wg_ref[...], preferred_element_type=F32) + bg_ref[...]


def _inproj_c(x, mod, w_bf, n, wg_bf, bg, tm=512):
    g, s, d = x.shape
    ng = wg_bf.shape[1]
    return pl.pallas_call(
        _inproj_c_kernel,
        out_shape=(jax.ShapeDtypeStruct((g, s, n), F32), jax.ShapeDtypeStruct((g, s, ng), F32)),
        grid=(g, s // tm),
        in_specs=[
            pl.BlockSpec((None, tm, d), lambda gi, ti: (gi, ti, 0)),
            pl.BlockSpec((None, 6, d), lambda gi, ti: (gi, 0, 0)),
            _resident(w_bf.shape, lambda gi, ti: (0, 0)),
            _resident((d, ng), lambda gi, ti: (0, 0)),
            pl.BlockSpec((1, ng), lambda gi, ti: (0, 0)),
        ],
        out_specs=(pl.BlockSpec((None, tm, n), lambda gi, ti: (gi, ti, 0)),
                   pl.BlockSpec((None, tm, ng), lambda gi, ti: (gi, ti, 0))),
        compiler_params=_cparams(("arbitrary", "arbitrary")),
        name="inproj_c",
    )(x, mod, w_bf, wg_bf, bg)


def _log_sigmoid(x):
    return jnp.minimum(x, 0.0) - jnp.log(1.0 + jnp.exp(-jnp.abs(x)))


MLSTM_L = 128


def _split3_bf16(x):
    hi = x.astype(BF16)
    r1 = x - hi.astype(F32)
    mid = r1.astype(BF16)
    lo = (r1 - mid.astype(F32)).astype(BF16)
    return hi, mid, lo


def _mlstm_kernel(*refs, seq, hg, has_init, emit_state):
    q_ref, k_ref, v_ref, o_ref, gi_ref, gf_ref, hgain_ref = refs[:7]
    pos = 7
    if has_init:
        c0_ref, n0_ref, m0_ref = refs[pos:pos + 3]
        pos += 3
    out_ref = refs[pos]
    pos += 1
    if emit_state:
        co_ref, no_ref, mo_ref = refs[pos:pos + 3]
        pos += 3
    cext_ref, hf_ref, hb_ref, b_ref, g_ref, gmax_ref, mt_ref, wi_ref, en_ref, ws_ref, gt_ref, wc_ref = refs[pos:]

    L = MLSTM_L
    dh = MLSTM_DH
    nh = MLSTM_HEADS
    nc = seq // L
    head0 = pl.program_id(1) * hg
    neg = -jnp.inf

    lane = lax.broadcasted_iota(jnp.int32, (L, LANES), 1)
    lane1 = lax.broadcasted_iota(jnp.int32, (1, LANES), 1)
    row = lax.broadcasted_iota(jnp.int32, (L, L), 0)
    col = lax.broadcasted_iota(jnp.int32, (L, L), 1)
    lower = col <= row
    upper = col >= row
    tri_l = jnp.where(lower, 1.0, 0.0).astype(BF16)
    tri_u = jnp.where(upper, 1.0, 0.0).astype(BF16)
    fwd_lane = lane < nh
    fwd_lane1 = lane1 < nh
    trow = lax.broadcasted_iota(jnp.int32, (L, LANES), 0)

    btot, glast = [], []
    for c in range(nc):
        rows = slice(c * L, (c + 1) * L)
        f = _log_sigmoid(gf_ref[rows, :])
        parts = _split3_bf16(f)
        pre = sum(jnp.dot(tri_l, p, preferred_element_type=F32) for p in parts)
        suf = sum(jnp.dot(tri_u, p, preferred_element_type=F32) for p in parts)
        b = jnp.where(fwd_lane, pre, suf)
        g = gi_ref[rows, :] - b
        gp, gs = g, g
        k = 1
        while k < L:
            gp = jnp.where(trow >= k, jnp.maximum(gp, pltpu.roll(gp, k, 0)), gp)
            gs = jnp.where(trow < L - k, jnp.maximum(gs, pltpu.roll(gs, L - k, 0)), gs)
            k *= 2
        gmax = jnp.where(fwd_lane, gp, gs)
        b_ref[rows, :] = b
        g_ref[rows, :] = g
        gmax_ref[rows, :] = gmax
        btot.append(jnp.where(fwd_lane1, b[L - 1:L, :], b[0:1, :]))
        glast.append(jnp.where(fwd_lane1, gmax[L - 1:L, :], gmax[0:1, :]))

    m_init = m0_ref[...] if has_init else jnp.zeros((1, LANES), F32)
    mf, mb = m_init, m_init
    ms_f, mn_f, ms_b, mn_b = [None] * nc, [None] * nc, [None] * nc, [None] * nc
    for c in range(nc):
        ms_f[c] = mf
        mf = btot[c] + jnp.maximum(mf, glast[c])
        mn_f[c] = mf
        cb = nc - 1 - c
        ms_b[cb] = mb
        mb = btot[cb] + jnp.maximum(mb, glast[cb])
        mn_b[cb] = mb
    m_final = jnp.where(fwd_lane1, mf, mb)

    for c in range(nc):
        rows = slice(c * L, (c + 1) * L)
        m_start = jnp.where(fwd_lane1, ms_f[c], ms_b[c])
        m_next = jnp.where(fwd_lane1, mn_f[c], mn_b[c])
        g = g_ref[rows, :]
        mt = jnp.maximum(m_start, gmax_ref[rows, :])
        mt_ref[rows, :] = mt
        wi_ref[rows, :] = jnp.exp(m_start - mt)
        en_ref[rows, :] = jnp.exp(-(b_ref[rows, :] + mt))
        ws_ref[rows, :] = jnp.exp(btot[c] + g - m_next)
        gt_ref[c] = g.T
        wc_ref[c:c + 1, :] = jnp.exp(btot[c] + m_start - m_next)

    for d in range(2):
        for hh in range(hg):
            idx = d * hg + hh
            if has_init:
                cext_ref[idx, :, 0:dh] = c0_ref[d, hh]
                n0_tile = jnp.where(lax.broadcasted_iota(jnp.int32, (dh, dh), 0) == 0, n0_ref[d, hh], 0.0)
                cext_ref[idx, :, dh:2 * dh] = n0_tile.T
            else:
                cext_ref[idx] = jnp.zeros((dh, 2 * dh), F32)

    ones_col = jnp.where(lane == 0, 1.0, 0.0).astype(BF16)
    nt = (((1,), (1,)), ((), ()))
    tn = (((0,), (0,)), ((), ()))

    def column(x, j):
        return jnp.sum(jnp.where(lane == j, x, 0.0), axis=-1, keepdims=True)

    def one_direction(d, hh, c, s_qk, q_bf, k_s, v_ext, v_bf):
        idx = d * hg + hh
        j = d * nh + head0 + hh
        rows = pl.ds(pl.multiple_of(c * L, L), L)
        mt = column(mt_ref[rows, :], j)
        wi = column(wi_ref[rows, :], j)
        en = column(en_ref[rows, :], j)
        ws = column(ws_ref[rows, :], j)
        g_r = gt_ref[c, pl.ds(j, 1), :]
        w_c = jnp.sum(jnp.where(lane1 == j, wc_ref[pl.ds(c, 1), :], 0.0), axis=-1, keepdims=True)
        causal = lower if d == 0 else upper
        p = s_qk * jnp.exp(jnp.where(causal, g_r - mt, neg))
        qc = jnp.dot(q_bf, cext_ref[idx].astype(BF16), preferred_element_type=F32)
        num = wi * qc[:, 0:dh] + jnp.dot(p.astype(BF16), v_bf, preferred_element_type=F32)
        den = wi * qc[:, dh:dh + 1] + jnp.sum(p, axis=-1, keepdims=True)
        h = num / jnp.maximum(jnp.abs(den), en)
        upd = lax.dot_general((ws * k_s).astype(BF16), v_ext, tn, preferred_element_type=F32)
        cext_ref[idx] = w_c * cext_ref[idx] + upd
        return h

    def load_chunk(hh, c):
        sl = (pl.ds(pl.multiple_of(c * L, L), L), slice(hh * dh, (hh + 1) * dh))
        q_bf = q_ref[sl].astype(BF16)
        k_s = k_ref[sl] * (dh ** -0.5)
        v_bf = v_ref[sl].astype(BF16)
        v_ext = jnp.concatenate([v_bf, ones_col], axis=-1)
        s_qk = lax.dot_general(q_bf, k_s.astype(BF16), nt, preferred_element_type=F32)
        return s_qk, q_bf, k_s, v_ext, v_bf

    def step(c, carry):
        cb = nc - 1 - c
        for hh in range(hg):
            h = one_direction(0, hh, c, *load_chunk(hh, c))
            hf_ref[pl.ds(pl.multiple_of(c * L, L), L), hh * dh:(hh + 1) * dh] = h
        for hh in range(hg):
            h = one_direction(1, hh, cb, *load_chunk(hh, cb))
            hb_ref[pl.ds(pl.multiple_of(cb * L, L), L), hh * dh:(hh + 1) * dh] = h
        return carry

    if nc <= 2:
        for c in range(nc):
            step(c, 0)
    else:
        lax.fori_loop(0, nc, step, 0)

    for hh in range(hg):
        cs = slice(hh * dh, (hh + 1) * dh)
        hs = hf_ref[:, cs] + hb_ref[:, cs]
        mu = jnp.mean(hs, axis=-1, keepdims=True)
        hc = hs - mu
        var = jnp.mean(hc * hc, axis=-1, keepdims=True)
        hn = hc * lax.rsqrt(var + LN_EPS) * hgain_ref[:, cs]
        out_ref[:, cs] = (_sigmoid(o_ref[:, cs]) * hn).astype(out_ref.dtype)

    if emit_state:
        for d in range(2):
            for hh in range(hg):
                idx = d * hg + hh
                co_ref[d, hh] = cext_ref[idx, :, 0:dh]
                no_ref[d, hh] = cext_ref[idx, :, dh:2 * dh].T[0:1, :]
        mo_ref[...] = m_final


def _mlstm(proj, gates, head_g, g0, n_seq, seq, hg, init=None, emit_state=False):
    g, s, _ = proj.shape
    per_group = s // seq
    n_hg = MLSTM_HEADS // hg
    w = hg * MLSTM_DH
    nc = seq // MLSTM_L
    n_blocks = D_MODEL // w

    def tok_map(colblock):
        return lambda b, hi: (g0 + b // per_group, b % per_group, colblock * n_blocks + hi)

    def gate_map(half):
        return lambda b, hi: (g0 + b // per_group, b % per_group, half)

    args = [proj, proj, proj, proj, gates, gates, head_g.reshape(1, D_MODEL)]
    in_specs = [
        pl.BlockSpec((None, seq, w), tok_map(0)),
        pl.BlockSpec((None, seq, w), tok_map(1)),
        pl.BlockSpec((None, seq, w), tok_map(2)),
        pl.BlockSpec((None, seq, w), tok_map(3)),
        pl.BlockSpec((None, seq, LANES), gate_map(0)),
        pl.BlockSpec((None, seq, LANES), gate_map(1)),
        pl.BlockSpec((1, w), lambda b, hi: (0, hi)),
    ]
    if init is not None:
        c0, n0, m0 = init
        m0_lanes = jnp.pad(m0.reshape(n_seq, 1, 2 * MLSTM_HEADS), ((0, 0), (0, 0), (0, LANES - 2 * MLSTM_HEADS)))
        args += [c0, n0.reshape(n0.shape[:-1] + (1, MLSTM_DH)), m0_lanes]
        in_specs += [
            pl.BlockSpec((None, None, 2, hg, MLSTM_DH, MLSTM_DH), lambda b, hi: (b, 0, 0, hi, 0, 0)),
            pl.BlockSpec((None, 2, hg, 1, MLSTM_DH), lambda b, hi: (b, 0, hi, 0, 0)),
            pl.BlockSpec((None, 1, LANES), lambda b, hi: (b, 0, 0)),
        ]

    out_shape = [jax.ShapeDtypeStruct((n_seq // per_group, s, D_MODEL), BF16)]
    out_specs = [pl.BlockSpec((None, seq, w), lambda b, hi: (b // per_group, b % per_group, hi))]
    if emit_state:
        out_shape += [
            jax.ShapeDtypeStruct((n_seq, 2, MLSTM_HEADS, MLSTM_DH, MLSTM_DH), F32),
            jax.ShapeDtypeStruct((n_seq, 2, MLSTM_HEADS, 1, MLSTM_DH), F32),
            jax.ShapeDtypeStruct((n_seq, n_hg, 1, LANES), F32),
        ]
        out_specs += [
            pl.BlockSpec((None, 2, hg, MLSTM_DH, MLSTM_DH), lambda b, hi: (b, 0, hi, 0, 0)),
            pl.BlockSpec((None, 2, hg, 1, MLSTM_DH), lambda b, hi: (b, 0, hi, 0, 0)),
            pl.BlockSpec((None, None, 1, LANES), lambda b, hi: (b, hi, 0, 0)),
        ]

    tok_scratch = pltpu.VMEM((seq, LANES), F32)
    return pl.pallas_call(
        functools.partial(_mlstm_kernel, seq=seq, hg=hg, has_init=init is not None, emit_state=emit_state),
        out_shape=tuple(out_shape),
        grid=(n_seq, n_hg),
        in_specs=in_specs,
        out_specs=tuple(out_specs),
        scratch_shapes=[
            pltpu.VMEM((2 * hg, MLSTM_DH, 2 * MLSTM_DH), F32),
            pltpu.VMEM((seq, w), F32),
            pltpu.VMEM((seq, w), F32),
        ] + [tok_scratch] * 7 + [
            pltpu.VMEM((nc, LANES, MLSTM_L), F32),
            pltpu.VMEM((max(nc, 8), LANES), F32),
        ],
        compiler_params=_cparams(("arbitrary", "arbitrary"), HIGH_VMEM_LIMIT),
        name="mlstm_%d" % seq,
    )(*args)


def kernel(x_prompt, x_sample, c, cache_k, cache_v, state_C, state_n, state_m, c_ctx, ada_w, ada_b, ln_g, ln_b, w_in_a, diff_lambda, diff_norm_g, pool_w, pool_scale, w_out_a, ffn_w_gu, ffn_w_down, w_in_c, b_gates_c, mlstm_norm_g, w_out_c, router_w, router_b, moe_w_gu, moe_w_down):
    n_ctx, seq_ctx, d = x_prompt.shape
    n_lat, seq_lat, _ = x_sample.shape
    assert d == D_MODEL and (n_ctx * seq_ctx) % seq_lat == 0 and seq_lat % seq_ctx == 0
    gl = n_lat
    gc = n_ctx * seq_ctx // seq_lat
    s = seq_lat

    x_ctx = x_prompt.reshape(gc, s, d)
    cvec = jnp.concatenate([c, jnp.broadcast_to(c_ctx[None, :], (gc, d))], axis=0)
    mod_all = _modulation(cvec, ada_w, ada_b).reshape(DEPTH, gl + gc, 6, d)

    mod = mod_all[0]
    lam_init = 0.8 - 0.6 * math.exp(-0.3 * 0)
    cos_t, sin_t = _rope_tables(s)
    proj, (ffn_wgu_bf, ffn_wd_bf) = _inproj_a(x_sample, x_ctx, mod, w_in_a[0].astype(BF16), cos_t, sin_t,
                                              (ffn_w_gu, ffn_w_down))
    norm_g = diff_norm_g[0].reshape(1, LANES)
    attn_c, new_k, new_v = _attn_context(proj, diff_lambda[0], norm_g, gl, n_ctx, seq_ctx, lam_init)
    attn_l = _attn_latent(proj, cache_k, cache_v, diff_lambda[0], norm_g, gl, lam_init)
    pool_c = _pool(proj, pool_w[0], pool_scale[0], gl, gc, seq_ctx)
    pool_l = _pool(proj, pool_w[0], pool_scale[0], 0, gl, seq_lat)
    w_out = w_out_a[0].astype(BF16)
    x = _outproj([(attn_l, attn_c), (pool_l, pool_c)], [w_out[:DIFF_WIDTH], w_out[DIFF_WIDTH:]],
                 x_sample, x_ctx, mod, ln_g[0, 0], ln_b[0, 0], 2)
    x, (moe_wgu_bf, moe_wd_bf, w_in_c_bf) = _ffn(x, mod, ffn_wgu_bf[0], ffn_wd_bf[0], ln_g[0, 1], ln_b[0, 1],
                                                  (moe_w_gu[0], moe_w_down[0], w_in_c))

    mod = mod_all[1]
    n_main = 4 * D_MODEL
    nh = MLSTM_HEADS
    wg4 = w_in_c[0][:, n_main:].reshape(d, N_GATES, nh)
    bg4 = b_gates_c[0].reshape(1, N_GATES, nh)
    lane_pad = ((0, 0), (0, LANES - 2 * nh))

    def gate_lanes(a):
        return jnp.concatenate([jnp.pad(jnp.concatenate([a[:, 0], a[:, 2]], axis=-1), lane_pad),
                                jnp.pad(jnp.concatenate([a[:, 1], a[:, 3]], axis=-1), lane_pad)], axis=-1)

    proj, gates = _inproj_c(x, mod, w_in_c_bf[0], n_main, gate_lanes(wg4).astype(BF16), gate_lanes(bg4))
    mix_c, new_c, new_n, new_m = _mlstm(proj, gates, mlstm_norm_g[0], gl, n_ctx, seq_ctx, MLSTM_HEADS,
                                        emit_state=True)
    (mix_l,) = _mlstm(proj, gates, mlstm_norm_g[0], 0, n_lat, seq_lat, MLSTM_HEADS,
                      init=(state_C, state_n[:, 0], state_m[:, 0]))
    x = _outproj([(mix_l, mix_c)], [w_out_c[0].astype(BF16)], x, None, mod, ln_g[1, 0], ln_b[1, 0], 2)
    y_sample, y_ctx = _moe(x, mod, router_w[0], router_b[0], moe_wgu_bf, moe_wd_bf,
                           ln_g[1, 1], ln_b[1, 1], gl)
    y_prompt = y_ctx.reshape(n_ctx, seq_ctx, d)
    new_m = new_m[:, 0, 0, :2 * MLSTM_HEADS].reshape(n_ctx, 2, MLSTM_HEADS)
    return (y_prompt, y_sample, new_k, new_v, new_c[:, None], new_n[..., 0, :][:, None], new_m[:, None])
```

```python
import functools
import math

import jax
import jax.numpy as jnp
import numpy as np
from jax import lax
from jax.experimental import pallas as pl
from jax.experimental.pallas import tpu as pltpu

F32 = jnp.float32
BF16 = jnp.bfloat16

D_MODEL = 1024
GRID_W = 64
ROPE_BASE = 10000.0
DIFF_HEADS = 4
DIFF_DH = 64
DIFF_WIDTH = DIFF_HEADS * 2 * DIFF_DH
POOL_GROUPS = 4
POOL_GC = 128
POOL_WIDTH = POOL_GROUPS * POOL_GC
POOL_WINDOWS = (2, 4, 8, 16)
MLSTM_HEADS = 8
MLSTM_DH = 128
N_GATES = 4
D_FF = 2816
N_EXPERTS = 8
D_FF_EXPERT = 1792
LN_EPS = 1e-5
DEPTH = 2
ALPHA = (2.0 * DEPTH) ** 0.25

LANES = 128
SUBLANES = 8
FF_CHUNK = 256
VMEM_LIMIT = 56 * 1024 * 1024
HIGH_VMEM_LIMIT = 60 * 1024 * 1024


def _cparams(sem, vmem=VMEM_LIMIT):
    return pltpu.CompilerParams(dimension_semantics=sem, vmem_limit_bytes=vmem)


def _resident(shape, index_map):
    return pl.BlockSpec(shape, index_map, pipeline_mode=pl.Buffered(1))


def _layer_norm_rows(z, g, b):
    mu = jnp.mean(z, axis=-1, keepdims=True)
    zc = z - mu
    var = jnp.mean(zc * zc, axis=-1, keepdims=True)
    return zc * lax.rsqrt(var + LN_EPS) * g + b


def _sigmoid(x):
    return 1.0 / (1.0 + jnp.exp(-x))


def _split_bf16(x):
    hi = x.astype(BF16)
    lo = (x - hi.astype(F32)).astype(BF16)
    return hi, lo


def _cast_rider_specs(weights, n_steps, per):
    specs = []
    for w in weights:
        e, rows, cols = w.shape
        per_e = next((k for k in range(n_steps // e, 0, -1)
                      if n_steps % (e * k) == 0 and rows % (2 * SUBLANES * k) == 0), None)
        if per_e is None:
            return None
        hold = n_steps // (e * per_e)

        def slab_map(gi, ti, per_e=per_e, hold=hold):
            slab = (gi * per + ti) // hold
            return (slab // per_e, slab % per_e, 0)

        specs.append(pl.BlockSpec((None, rows // per_e, cols), slab_map))
    return specs


def _cast_slabs(cast_in, cast_out):
    for src, dst in zip(cast_in, cast_out):
        dst[...] = src[...].astype(BF16)


def _mod_kernel(c_ref, w_ref, b_ref, o_ref):
    c = c_ref[...]
    h = (c * _sigmoid(c)).astype(BF16)
    o_ref[...] = jnp.dot(h, w_ref[...].astype(BF16), preferred_element_type=F32) + b_ref[...]


def _modulation(cvec, ada_w, ada_b):
    depth, d, n = ada_w.shape
    g = cvec.shape[0]
    tn = 1536
    return pl.pallas_call(
        _mod_kernel,
        out_shape=jax.ShapeDtypeStruct((depth, g, n), F32),
        grid=(depth, n // tn),
        in_specs=[
            pl.BlockSpec((g, d), lambda l, j: (0, 0)),
            pl.BlockSpec((None, d, tn), lambda l, j: (l, 0, j)),
            pl.BlockSpec((None, 1, tn), lambda l, j: (l, 0, j)),
        ],
        out_specs=pl.BlockSpec((None, g, tn), lambda l, j: (l, 0, j)),
        compiler_params=_cparams(("arbitrary", "arbitrary")),
        name="modulation",
    )(cvec, ada_w, ada_b.reshape(depth, 1, n))


def _rot_half16(x):
    lane = lax.broadcasted_iota(jnp.int32, x.shape, 1)
    return jnp.where((lane % 32) < 16, pltpu.roll(x, LANES - 16, 1), pltpu.roll(x, 16, 1))


def _two_stream_specs(tm, d, gl, ctx_first_group=0):
    return [pl.BlockSpec((None, tm, d), lambda gi, ti: (jnp.minimum(gi, gl - 1), jnp.where(gi < gl, ti, 0), 0)),
            pl.BlockSpec((None, tm, d), lambda gi, ti: (ctx_first_group + jnp.maximum(gi - gl, 0),
                                                        jnp.where(gi < gl, 0, ti), 0))]


def _inproj_a_kernel(xl_ref, xc_ref, mod_ref, w_ref, cos_ref, sin_ref, *rest, n_latent_groups, n_cast):
    cast_in, o_ref, cast_out = rest[:n_cast], rest[n_cast], rest[n_cast + 1:]
    _cast_slabs(cast_in, cast_out)
    x = jnp.where(pl.program_id(0) < n_latent_groups, xl_ref[...], xc_ref[...])
    h = x * (1.0 + mod_ref[1:2, :]) + mod_ref[0:1, :]
    p = jnp.dot(h.astype(BF16), w_ref[...], preferred_element_type=F32)
    cos = cos_ref[...]
    sin = sin_ref[...]
    n_rope = 2 * DIFF_WIDTH // LANES
    for j in range(n_rope):
        blk = p[:, j * LANES:(j + 1) * LANES]
        o_ref[:, j * LANES:(j + 1) * LANES] = blk * cos + _rot_half16(blk) * sin
    o_ref[:, n_rope * LANES:] = p[:, n_rope * LANES:]


def _inproj_a(x_lat, x_ctx, mod, w_bf, cos_t, sin_t, cast_weights=(), tm=512):
    n_latent_groups, s, d = x_lat.shape
    g = n_latent_groups + x_ctx.shape[0]
    n = w_bf.shape[1]
    per = s // tm
    cast_specs = _cast_rider_specs(cast_weights, g * per, per)
    if cast_specs is None:
        proj, _ = _inproj_a(x_lat, x_ctx, mod, w_bf, cos_t, sin_t, (), tm)
        return proj, tuple(w.astype(BF16) for w in cast_weights)

    def table_map(gi, ti):
        return (jnp.where(gi >= n_latent_groups, 1, 0), ti, 0)

    outs = pl.pallas_call(
        functools.partial(_inproj_a_kernel, n_latent_groups=n_latent_groups, n_cast=len(cast_weights)),
        out_shape=(jax.ShapeDtypeStruct((g, s, n), F32),) + tuple(
            jax.ShapeDtypeStruct(w.shape, BF16) for w in cast_weights),
        grid=(g, per),
        in_specs=_two_stream_specs(tm, d, n_latent_groups) + [
            pl.BlockSpec((None, 6, d), lambda gi, ti: (gi, 0, 0)),
            _resident((d, n), lambda gi, ti: (0, 0)),
            pl.BlockSpec((None, tm, LANES), table_map),
            pl.BlockSpec((None, tm, LANES), table_map),
        ] + cast_specs,
        out_specs=(pl.BlockSpec((None, tm, n), lambda gi, ti: (gi, ti, 0)),) + tuple(cast_specs),
        compiler_params=_cparams(("arbitrary", "arbitrary")),
        name="inproj_a",
    )(x_lat, x_ctx, mod, w_bf, cos_t, sin_t, *cast_weights)
    return outs[0], outs[1:]


def _rope_tables(n_tokens):
    rows = n_tokens // GRID_W
    row_pos = np.repeat(np.arange(rows), GRID_W).astype(np.float64)
    col_pos = np.tile(np.arange(GRID_W), rows).astype(np.float64)
    n_freq = DIFF_DH // 4
    inv_freq = np.power(ROPE_BASE, -np.arange(n_freq, dtype=np.float64) / n_freq)
    ang = np.stack([row_pos[:, None] * inv_freq, col_pos[:, None] * inv_freq], axis=1)
    cos, sin = np.cos(ang), np.sin(ang)
    cos64 = np.concatenate([cos[:, 0], cos[:, 0], cos[:, 1], cos[:, 1]], axis=-1)
    sin64 = np.concatenate([-sin[:, 0], sin[:, 0], -sin[:, 1], sin[:, 1]], axis=-1)
    cos_l = np.tile(cos64, (1, LANES // DIFF_DH))
    sin_l = np.tile(sin64, (1, LANES // DIFF_DH))
    cos_t = np.stack([cos_l, np.ones_like(cos_l)])
    sin_t = np.stack([sin_l, np.zeros_like(sin_l)])
    return jnp.asarray(cos_t, F32), jnp.asarray(sin_t, F32)


def _diff_attn_kernel(*refs, n_pieces, n_heads, lam_init, emit_kv):
    lam_ref, ng_ref, q_ref = refs[:3]
    kv_refs = refs[3:3 + 2 * n_pieces]
    o_ref = refs[3 + 2 * n_pieces]

    lp = lam_ref[...]
    lam = (jnp.exp(jnp.sum(lp[0:1] * lp[1:2], axis=-1, keepdims=True))
           - jnp.exp(jnp.sum(lp[2:3] * lp[3:4], axis=-1, keepdims=True)) + lam_init)
    nt = (((1,), (1,)), ((), ()))

    def softmax_pieces(ss):
        m = functools.reduce(jnp.maximum, [jnp.max(s, axis=-1, keepdims=True) for s in ss])
        es = [jnp.exp(s - m) for s in ss]
        l = functools.reduce(jnp.add, [jnp.sum(e, axis=-1, keepdims=True) for e in es])
        return [e / l for e in es]

    for h in range(n_heads):
        hs = slice(h * LANES, (h + 1) * LANES)
        q = q_ref[:, hs] * (DIFF_DH ** -0.5)
        lane = lax.broadcasted_iota(jnp.int32, q.shape, 1)
        q1 = jnp.where(lane < DIFF_DH, q, 0.0).astype(BF16)
        q2 = jnp.where(lane >= DIFF_DH, q, 0.0).astype(BF16)
        s1, s2, vs = [], [], []
        for i in range(n_pieces):
            kb = kv_refs[2 * i][:, hs].astype(BF16)
            vs.append(kv_refs[2 * i + 1][:, hs].astype(BF16))
            s1.append(lax.dot_general(q1, kb, nt, preferred_element_type=F32))
            s2.append(lax.dot_general(q2, kb, nt, preferred_element_type=F32))
        p1 = softmax_pieces(s1)
        p2 = softmax_pieces(s2)
        o = None
        for i in range(n_pieces):
            a = (p1[i] - lam * p2[i]).astype(BF16)
            t = jnp.dot(a, vs[i], preferred_element_type=F32)
            o = t if o is None else o + t
        o = o * lax.rsqrt(jnp.mean(o * o, axis=-1, keepdims=True) + LN_EPS)
        o_ref[:, hs] = (o * ng_ref[...] * (1.0 - lam_init)).astype(o_ref.dtype)
        if emit_kv:
            ko_ref, vo_ref = refs[4 + 2 * n_pieces:]
            ko_ref[h] = kv_refs[0][:, hs]
            vo_ref[h] = kv_refs[1][:, hs]


def _attn_context(proj, lam_p, norm_g, n_latent_groups, n_seq, seq, lam_init):
    g, s, _ = proj.shape
    per_group = s // seq
    blk = (None, seq, DIFF_WIDTH)

    def tok_map(colblock):
        return lambda b: (n_latent_groups + b // per_group, b % per_group, colblock)

    cache_shape = jax.ShapeDtypeStruct((n_seq, 1, DIFF_HEADS, seq, LANES), F32)
    cache_spec = pl.BlockSpec((None, None, DIFF_HEADS, seq, LANES), lambda b: (b, 0, 0, 0, 0))
    out_spec = pl.BlockSpec(blk, lambda b: (b // per_group, b % per_group, 0))
    return pl.pallas_call(
        functools.partial(_diff_attn_kernel, n_pieces=1, n_heads=DIFF_HEADS, lam_init=lam_init, emit_kv=True),
        out_shape=(jax.ShapeDtypeStruct((g - n_latent_groups, s, DIFF_WIDTH), BF16), cache_shape, cache_shape),
        grid=(n_seq,),
        in_specs=[
            pl.BlockSpec((4, DIFF_DH), lambda b: (0, 0)),
            pl.BlockSpec((1, LANES), lambda b: (0, 0)),
            pl.BlockSpec(blk, tok_map(0)),
            pl.BlockSpec(blk, tok_map(1)),
            pl.BlockSpec(blk, tok_map(2)),
        ],
        out_specs=(out_spec, cache_spec, cache_spec),
        compiler_params=_cparams(("arbitrary",)),
        name="attn_context",
    )(lam_p, norm_g, proj, proj, proj)


def _attn_latent(proj, cache_k, cache_v, lam_p, norm_g, n_latent_groups, lam_init, tq=256):
    g, s, _ = proj.shape
    past = cache_k.shape[3]
    cache_spec = pl.BlockSpec((None, None, None, past, LANES), lambda b, h, qi: (b, 0, h, 0, 0))
    return pl.pallas_call(
        functools.partial(_diff_attn_kernel, n_pieces=2, n_heads=1, lam_init=lam_init, emit_kv=False),
        out_shape=jax.ShapeDtypeStruct((n_latent_groups, s, DIFF_WIDTH), BF16),
        grid=(n_latent_groups, DIFF_HEADS, s // tq),
        in_specs=[
            pl.BlockSpec((4, DIFF_DH), lambda b, h, qi: (0, 0)),
            pl.BlockSpec((1, LANES), lambda b, h, qi: (0, 0)),
            pl.BlockSpec((None, tq, LANES), lambda b, h, qi: (b, qi, h)),
            cache_spec,
            cache_spec,
            pl.BlockSpec((None, s, LANES), lambda b, h, qi: (b, 0, DIFF_HEADS + h)),
            pl.BlockSpec((None, s, LANES), lambda b, h, qi: (b, 0, 2 * DIFF_HEADS + h)),
        ],
        out_specs=pl.BlockSpec((None, tq, LANES), lambda b, h, qi: (b, qi, h)),
        compiler_params=_cparams(("arbitrary", "arbitrary", "arbitrary")),
        name="attn_latent",
    )(lam_p, norm_g, proj, cache_k, cache_v, proj, proj)


POOL_ROW_BLOCK = 256
POOL_COL_WINDOW = 512
assert (POOL_COL_WINDOW - POOL_ROW_BLOCK) // 2 >= max(POOL_WINDOWS) // 2


def _pool_kernel(p_ref, w_ref, sc_ref, o_ref, band_ref, *, seq):
    @pl.when((pl.program_id(0) == 0) & (pl.program_id(1) == 0))
    def _():
        t = lax.broadcasted_iota(jnp.int32, (seq, seq), 0)
        s_ = lax.broadcasted_iota(jnp.int32, (seq, seq), 1)
        for gi, w in enumerate(POOL_WINDOWS):
            inside = (s_ >= t - w // 2) & (s_ <= t + w // 2 - 1)
            band_ref[gi] = jnp.where(inside, 1.0, 0.0).astype(BF16)

    tcol = lax.broadcasted_iota(jnp.int32, (seq, 1), 0)
    for gi, w in enumerate(POOL_WINDOWS):
        u = p_ref[:, gi * POOL_GC:(gi + 1) * POOL_GC]
        hi, lo = _split_bf16(u)
        rb = min(seq, POOL_ROW_BLOCK)
        cw = min(seq, POOL_COL_WINDOW)
        blocks = []
        for i in range(seq // rb):
            c0 = min(max(i * rb - (cw - rb) // 2, 0), seq - cw)
            band = band_ref[gi, i * rb:(i + 1) * rb, c0:c0 + cw]
            blocks.append(jnp.dot(band, hi[c0:c0 + cw], preferred_element_type=F32)
                          + jnp.dot(band, lo[c0:c0 + cw], preferred_element_type=F32))
        win = blocks[0] if len(blocks) == 1 else jnp.concatenate(blocks, axis=0)
        cnt = (jnp.minimum(tcol + (w // 2 - 1), seq - 1) - jnp.maximum(tcol - w // 2, 0) + 1).astype(F32)
        pooled = win / cnt - u
        mixed = jnp.dot(pooled.astype(BF16), w_ref[gi].astype(BF16), preferred_element_type=F32)
        o_ref[:, gi * POOL_GC:(gi + 1) * POOL_GC] = (
            mixed * sc_ref[:, gi * POOL_GC:(gi + 1) * POOL_GC]).astype(o_ref.dtype)


def _pool(proj, pool_w, pool_scale, g0, n_groups, seq):
    g, s, _ = proj.shape
    col = 3 * DIFF_WIDTH // POOL_WIDTH
    return pl.pallas_call(
        functools.partial(_pool_kernel, seq=seq),
        out_shape=jax.ShapeDtypeStruct((n_groups, s, POOL_WIDTH), BF16),
        grid=(n_groups, s // seq),
        in_specs=[
            pl.BlockSpec((None, seq, POOL_WIDTH), lambda gi, ti: (g0 + gi, ti, col)),
            pl.BlockSpec((POOL_GROUPS, POOL_GC, POOL_GC), lambda gi, ti: (0, 0, 0)),
            pl.BlockSpec((1, POOL_WIDTH), lambda gi, ti: (0, 0)),
        ],
        out_specs=pl.BlockSpec((None, seq, POOL_WIDTH), lambda gi, ti: (gi, ti, 0)),
        scratch_shapes=[pltpu.VMEM((POOL_GROUPS, seq, seq), BF16)],
        compiler_params=_cparams(("arbitrary", "arbitrary")),
        name="pool_%d" % seq,
    )(proj, pool_w, pool_scale.reshape(1, POOL_WIDTH))


def _outproj_kernel(*refs, n_in, gate_row, n_latent_groups):
    a_refs = refs[:2 * n_in]
    w_refs = refs[2 * n_in:3 * n_in]
    xl_ref, xc_ref, mod_ref, g_ref, b_ref, o_ref = refs[3 * n_in:]
    is_latent = pl.program_id(0) < n_latent_groups
    acc = None
    for i, w_ref in enumerate(w_refs):
        a = jnp.where(is_latent, a_refs[2 * i][...], a_refs[2 * i + 1][...])
        t = jnp.dot(a, w_ref[...], preferred_element_type=F32)
        acc = t if acc is None else acc + t
    x = jnp.where(is_latent, xl_ref[...], xc_ref[...])
    z = ALPHA * x + mod_ref[gate_row:gate_row + 1, :] * acc
    o_ref[...] = _layer_norm_rows(z, g_ref[...], b_ref[...])


def _outproj(acts, weights, x_lat, x_ctx, mod, ln_g, ln_b, gate_row, tm=512):
    gl = acts[0][0].shape[0]
    _, s, d = x_lat.shape
    if x_ctx is None:
        g = x_lat.shape[0]
        x_ctx, x_specs = x_lat, _two_stream_specs(tm, d, gl, gl)
    else:
        g = gl + x_ctx.shape[0]
        x_specs = _two_stream_specs(tm, d, gl)
    n_in = len(acts)
    in_specs = []
    flat_acts = []
    for a_lat, a_ctx in acts:
        in_specs += _two_stream_specs(tm, a_lat.shape[-1], gl)
        flat_acts += [a_lat, a_ctx]
    in_specs += [_resident(w.shape, lambda gi, ti: (0, 0)) for w in weights]
    in_specs += x_specs
    in_specs += [
        pl.BlockSpec((None, 6, d), lambda gi, ti: (gi, 0, 0)),
        pl.BlockSpec((1, d), lambda gi, ti: (0, 0)),
        pl.BlockSpec((1, d), lambda gi, ti: (0, 0)),
    ]
    return pl.pallas_call(
        functools.partial(_outproj_kernel, n_in=n_in, gate_row=gate_row, n_latent_groups=gl),
        out_shape=jax.ShapeDtypeStruct((g, s, d), F32),
        grid=(g, s // tm),
        in_specs=in_specs,
        out_specs=pl.BlockSpec((None, tm, d), lambda gi, ti: (gi, ti, 0)),
        compiler_params=_cparams(("arbitrary", "arbitrary")),
        name="outproj",
    )(*flat_acts, *weights, x_lat, x_ctx, mod, ln_g.reshape(1, d), ln_b.reshape(1, d))


def _swiglu_chunks(h_bf, wgu_ref, wd_ref, d_ff):
    acc = None
    for j in range(d_ff // FF_CHUNK):
        lo = j * FF_CHUNK
        gate = jnp.dot(h_bf, wgu_ref[:, lo:lo + FF_CHUNK], preferred_element_type=F32)
        up = jnp.dot(h_bf, wgu_ref[:, d_ff + lo:d_ff + lo + FF_CHUNK], preferred_element_type=F32)
        act = (gate * _sigmoid(gate) * up).astype(BF16)
        t = jnp.dot(act, wd_ref[lo:lo + FF_CHUNK, :], preferred_element_type=F32)
        acc = t if acc is None else acc + t
    return acc


def _ffn_kernel(x_ref, mod_ref, wgu_ref, wd_ref, g_ref, b_ref, *rest, n_cast):
    cast_in, o_ref, cast_out = rest[:n_cast], rest[n_cast], rest[n_cast + 1:]
    _cast_slabs(cast_in, cast_out)
    x = x_ref[...]
    h = (x * (1.0 + mod_ref[4:5, :]) + mod_ref[3:4, :]).astype(BF16)
    acc = _swiglu_chunks(h, wgu_ref, wd_ref, D_FF)
    z = ALPHA * x + mod_ref[5:6, :] * acc
    o_ref[...] = _layer_norm_rows(z, g_ref[...], b_ref[...])


def _ffn(x, mod, wgu_bf, wd_bf, ln_g, ln_b, cast_weights=(), tm=512):
    g, s, d = x.shape
    per = s // tm
    cast_specs = _cast_rider_specs(cast_weights, g * per, per)
    if cast_specs is None:
        out, _ = _ffn(x, mod, wgu_bf, wd_bf, ln_g, ln_b, (), tm)
        return out, tuple(w.astype(BF16) for w in cast_weights)
    outs = pl.pallas_call(
        functools.partial(_ffn_kernel, n_cast=len(cast_weights)),
        out_shape=(jax.ShapeDtypeStruct((g, s, d), F32),) + tuple(
            jax.ShapeDtypeStruct(w.shape, BF16) for w in cast_weights),
        grid=(g, per),
        in_specs=[
            pl.BlockSpec((None, tm, d), lambda gi, ti: (gi, ti, 0)),
            pl.BlockSpec((None, 6, d), lambda gi, ti: (gi, 0, 0)),
            _resident(wgu_bf.shape, lambda gi, ti: (0, 0)),
            _resident(wd_bf.shape, lambda gi, ti: (0, 0)),
            pl.BlockSpec((1, d), lambda gi, ti: (0, 0)),
            pl.BlockSpec((1, d), lambda gi, ti: (0, 0)),
        ] + cast_specs,
        out_specs=(pl.BlockSpec((None, tm, d), lambda gi, ti: (gi, ti, 0)),) + tuple(cast_specs),
        compiler_params=_cparams(("arbitrary", "arbitrary")),
        name="ffn",
    )(x, mod, wgu_bf, wd_bf, ln_g.reshape(1, d), ln_b.reshape(1, d), *cast_weights)
    return outs[0], outs[1:]


MOE_BM = 512
MOE_SB = 256
MOE_TC = 256
MOE_WINDOW = 5
META_LANES = LANES


def _router_top2(h, rw_ref, rb_ref):
    h_hi, h_lo = _split_bf16(h)
    w_hi, w_lo = _split_bf16(rw_ref[...])
    logits = (jnp.dot(h_hi, w_hi, preferred_element_type=F32)
              + jnp.dot(h_lo, w_hi, preferred_element_type=F32)
              + jnp.dot(h_hi, w_lo, preferred_element_type=F32)) + rb_ref[...]
    lane = lax.broadcasted_iota(jnp.int32, logits.shape, 1).astype(F32)
    neg = -jnp.inf
    logits = jnp.where(lane < N_EXPERTS, logits, neg)
    m1 = jnp.max(logits, axis=-1, keepdims=True)
    i1 = jnp.min(jnp.where(logits == m1, lane, float(LANES)), axis=-1, keepdims=True)
    rest = jnp.where(lane == i1, neg, logits)
    m2 = jnp.max(rest, axis=-1, keepdims=True)
    i2 = jnp.min(jnp.where(rest == m2, lane, float(LANES)), axis=-1, keepdims=True)
    e2 = jnp.exp(m2 - m1)
    return lane, i1, i2, 1.0 / (1.0 + e2), e2 / (1.0 + e2)


def _route_kernel(x_ref, mod_ref, rw_ref, rb_ref, h_ref, wk_ref, srank_ref, cnt_ref, tri_ref, run_ref):
    tm, d = x_ref.shape

    @pl.when(pl.program_id(0) == 0)
    def _():
        r = lax.broadcasted_iota(jnp.int32, (tm, tm), 0)
        c = lax.broadcasted_iota(jnp.int32, (tm, tm), 1)
        tri_ref[...] = jnp.where(c <= r, 1.0, 0.0).astype(BF16)
        run_ref[...] = jnp.zeros_like(run_ref)

    h = x_ref[...] * (1.0 + mod_ref[4:5, :]) + mod_ref[3:4, :]
    h_ref[:, 0:d] = h.astype(BF16)
    lane, i1, i2, w1, w2 = _router_top2(h, rw_ref, rb_ref)
    first_is_low = i1 < i2
    e_hi = jnp.where(first_is_low, i2, i1)
    wk_ref[...] = jnp.where(lane == 0.0, jnp.where(first_is_low, w1, w2),
                            jnp.where(lane == 1.0, jnp.where(first_is_low, w2, w1), 0.0))
    tok = (pl.program_id(0) * tm + lax.broadcasted_iota(jnp.int32, (tm, META_LANES), 0))
    meta = jnp.where(lane == 0.0, (tok // 128).astype(F32),
                     jnp.where(lane == 1.0, (tok % 128).astype(F32),
                               jnp.where(lane == 2.0, e_hi, jnp.where(lane == 3.0, 1.0, 0.0))))
    h_ref[:, d:d + META_LANES] = meta.astype(BF16)

    member = (lane == i1) | (lane == i2)
    mem = jnp.where(member, 1.0, 0.0)
    rank = jnp.dot(tri_ref[...], mem.astype(BF16), preferred_element_type=F32) + run_ref[...]
    srank_ref[...] = jnp.where(member, rank, -rank).T[0:SUBLANES, :]
    run_ref[...] = rank[tm - 1:tm, :]
    cnt_ref[...] = rank[tm - 1:tm, :]


def _route(x, mod, router_w, router_b, tm=512):
    g, s, d = x.shape
    n = g * s
    assert n <= 128 * 256
    assert n >= MOE_WINDOW * MOE_TC
    per = s // tm
    rw = jnp.pad(router_w, ((0, 0), (0, LANES - N_EXPERTS)))
    rb = jnp.pad(router_b, (0, LANES - N_EXPERTS)).reshape(1, LANES)
    return pl.pallas_call(
        _route_kernel,
        out_shape=(jax.ShapeDtypeStruct((n, d + META_LANES), BF16), jax.ShapeDtypeStruct((n, LANES), F32),
                   jax.ShapeDtypeStruct((SUBLANES, n), F32), jax.ShapeDtypeStruct((1, LANES), F32)),
        grid=(n // tm,),
        in_specs=[
            pl.BlockSpec((None, tm, d), lambda i: (i // per, i % per, 0)),
            pl.BlockSpec((None, 6, d), lambda i: (i // per, 0, 0)),
            pl.BlockSpec((d, LANES), lambda i: (0, 0)),
            pl.BlockSpec((1, LANES), lambda i: (0, 0)),
        ],
        out_specs=(pl.BlockSpec((tm, d + META_LANES), lambda i: (i, 0)), pl.BlockSpec((tm, LANES), lambda i: (i, 0)),
                   pl.BlockSpec((SUBLANES, tm), lambda i: (0, i)), pl.BlockSpec((1, LANES), lambda i: (0, 0))),
        scratch_shapes=[pltpu.VMEM((tm, tm), BF16), pltpu.VMEM((1, LANES), F32)],
        compiler_params=_cparams(("arbitrary",)),
        name="moe_route",
    )(x, mod, rw, rb)


def _moe_plan(srank, counts, n_blocks):
    e_n = N_EXPERTS
    i32 = jnp.int32
    cnt = counts[0, :e_n].astype(i32)
    nb = (cnt + MOE_BM - 1) // MOE_BM
    nb_incl = jnp.cumsum(nb)
    gstart = nb_incl - nb
    n_used = nb_incl[-1]
    sr = srank[:e_n]
    rank = jnp.abs(sr).astype(i32)
    pos_row = jnp.where(sr > 0, rank - 1 + MOE_BM * gstart[:, None], -1)
    chunk_end = rank[:, MOE_TC - 1::MOE_TC]

    rc = jnp.minimum(jnp.arange(n_blocks, dtype=i32), n_used - 1)
    e_r = jnp.minimum(jnp.sum(nb_incl[None, :] <= rc[:, None], axis=1, dtype=i32), e_n - 1)

    sb_per = MOE_BM // MOE_SB
    q = jnp.arange(n_blocks * sb_per, dtype=i32)
    r_q = jnp.minimum(q // sb_per, n_used - 1)
    e_q = e_r[r_q]
    first = ((r_q - gstart[e_q]) * sb_per + q % sb_per) * MOE_SB
    used = (q // sb_per < n_used) & (first < cnt[e_q])
    hi = jnp.minimum(first + MOE_SB, cnt[e_q])
    ends_q = chunk_end[e_q]
    jlo = jnp.where(used, jnp.sum(ends_q < (first + 1)[:, None], axis=1, dtype=i32), 0)
    jhi = jnp.where(used, jnp.sum(ends_q < hi[:, None], axis=1, dtype=i32), -1)
    return pos_row, e_r, jlo, jhi, n_used.reshape(1)


def _experts_kernel(be_ref, jlo_ref, jhi_ref, nused_ref, h_ref, pos_ref, wgu_ref, wd_ref, y_ref, meta_ref, xg_ref):
    r = pl.program_id(0)
    d = y_ref.shape[1]

    @pl.when(r < nused_ref[0])
    def _():
        e = be_ref[r]
        sb_per = MOE_BM // MOE_SB
        n_chunks = h_ref.shape[0] // MOE_TC
        for u in range(sb_per):
            sb_rows = slice(u * MOE_SB, (u + 1) * MOE_SB)
            slot = r * MOE_BM + u * MOE_SB + lax.broadcasted_iota(jnp.int32, (MOE_SB, MOE_TC), 0)

            def onehot(j, slot=slot):
                return jnp.where(pos_ref[e, pl.ds(j, 1), :] == slot, 1.0, 0.0).astype(BF16)

            j0 = jnp.minimum(jlo_ref[r * sb_per + u], n_chunks - MOE_WINDOW)
            window = jnp.concatenate([onehot(j0 + j) for j in range(MOE_WINDOW)], axis=-1)
            rows = h_ref[pl.ds(pl.multiple_of(j0 * MOE_TC, MOE_TC), MOE_WINDOW * MOE_TC), :]
            xg_ref[sb_rows, :] = jnp.dot(window, rows, preferred_element_type=F32)

            def chunk(j, carry, sb_rows=sb_rows, onehot=onehot):
                rows = h_ref[pl.ds(pl.multiple_of(j * MOE_TC, MOE_TC), MOE_TC), :]
                xg_ref[sb_rows, :] += jnp.dot(onehot(j), rows, preferred_element_type=F32)
                return carry

            lax.fori_loop(j0 + MOE_WINDOW, jhi_ref[r * sb_per + u] + 1, chunk, 0)
        meta_ref[...] = xg_ref[:, d:d + META_LANES].T[0:SUBLANES, :]
        y_ref[...] = _swiglu_chunks(xg_ref[:, 0:d].astype(BF16), wgu_ref, wd_ref, D_FF_EXPERT)

    @pl.when(r >= nused_ref[0])
    def _():
        y_ref[...] = jnp.zeros_like(y_ref)
        meta_ref[...] = jnp.zeros_like(meta_ref)


def _experts(h, pos_row, plan, wgu_bf, wd_bf, n_blocks):
    n, dx = h.shape
    d = dx - META_LANES
    e_r, jlo, jhi, n_used = plan
    grid_spec = pltpu.PrefetchScalarGridSpec(
        num_scalar_prefetch=4,
        grid=(n_blocks,),
        in_specs=[
            _resident((n, dx), lambda r, be, lo, hi, nu: (0, 0)),
            _resident(pos_row.shape, lambda r, be, lo, hi, nu: (0, 0, 0)),
            pl.BlockSpec((None, d, 2 * D_FF_EXPERT), lambda r, be, lo, hi, nu: (be[r], 0, 0),
                         pipeline_mode=pl.Buffered(1)),
            pl.BlockSpec((None, D_FF_EXPERT, d), lambda r, be, lo, hi, nu: (be[r], 0, 0)),
        ],
        out_specs=(pl.BlockSpec((MOE_BM, d), lambda r, be, lo, hi, nu: (r, 0)),
                   pl.BlockSpec((None, SUBLANES, MOE_BM), lambda r, be, lo, hi, nu: (r, 0, 0))),
        scratch_shapes=[pltpu.VMEM((MOE_BM, dx), F32)],
    )
    return pl.pallas_call(
        _experts_kernel,
        out_shape=(jax.ShapeDtypeStruct((n_blocks * MOE_BM, d), F32),
                   jax.ShapeDtypeStruct((n_blocks, SUBLANES, MOE_BM), F32)),
        grid_spec=grid_spec,
        compiler_params=_cparams(("arbitrary",), HIGH_VMEM_LIMIT),
        name="moe_experts",
    )(e_r, jlo, jhi, n_used, h, pos_row, wgu_bf, wd_bf)


def _scatter_kernel(dest_ref, y_ref, o_ref, ybuf_ref, sem):
    r = pl.program_id(0)
    n_steps = pl.num_programs(0)
    buf = r % 2

    def wait_block(b):
        pltpu.make_async_copy(ybuf_ref.at[b], o_ref.at[pl.ds(0, MOE_BM), :], sem.at[b]).wait()

    @pl.when(r >= 2)
    def _():
        wait_block(buf)

    def copy_out(b):
        ybuf_ref[b] = y_ref[...]
        for i in range(MOE_BM):
            pltpu.make_async_copy(ybuf_ref.at[b, pl.ds(i, 1), :], o_ref.at[pl.ds(dest_ref[0, i], 1), :],
                                  sem.at[b]).start()

    for b in range(2):
        pl.when(buf == b)(functools.partial(copy_out, b))

    @pl.when(r == n_steps - 1)
    def _():
        wait_block(1 - buf)
        wait_block(buf)


def _scatter_rows(y, dest, n_rows_out):
    n_slots, d = y.shape
    n_blocks = n_slots // MOE_BM
    assert n_blocks >= 2
    return pl.pallas_call(
        _scatter_kernel,
        out_shape=jax.ShapeDtypeStruct((n_rows_out, d), F32),
        grid=(n_blocks,),
        in_specs=[
            pl.BlockSpec((None, 1, MOE_BM), lambda r: (r, 0, 0), memory_space=pltpu.SMEM),
            pl.BlockSpec((MOE_BM, d), lambda r: (r, 0)),
        ],
        out_specs=pl.BlockSpec(memory_space=pl.ANY),
        scratch_shapes=[pltpu.VMEM((2, MOE_BM, d), F32), pltpu.SemaphoreType.DMA((2,))],
        compiler_params=_cparams(("arbitrary",)),
        name="moe_scatter",
    )(dest.reshape(n_blocks, 1, MOE_BM), y)


def _finish_kernel(a_ref, b_ref, wk_ref, x_ref, mod_ref, g_ref, bb_ref, ol_ref, oc_ref, *, n_latent_tiles):
    lane = lax.broadcasted_iota(jnp.int32, wk_ref.shape, 1)
    wk = wk_ref[...]
    w0 = jnp.sum(jnp.where(lane == 0, wk, 0.0), axis=-1, keepdims=True)
    w1 = jnp.sum(jnp.where(lane == 1, wk, 0.0), axis=-1, keepdims=True)
    acc = w0 * a_ref[...] + w1 * b_ref[...]
    z = ALPHA * x_ref[...] + mod_ref[5:6, :] * acc
    out = _layer_norm_rows(z, g_ref[...], bb_ref[...])
    is_latent = pl.program_id(0) < n_latent_tiles

    @pl.when(is_latent)
    def _():
        ol_ref[...] = out

    @pl.when(jnp.logical_not(is_latent))
    def _():
        oc_ref[...] = out


def _finish(rows, wk, x, mod, ln_g, ln_b, n_latent_groups, tm=1024):
    g, s, d = x.shape
    n = g * s
    per = s // tm
    nt = n // tm
    nlt = n_latent_groups * per

    def lat_map(i):
        j = jnp.minimum(i, nlt - 1)
        return (j // per, j % per, 0)

    def ctx_map(i):
        j = jnp.maximum(i - nlt, 0)
        return (j // per, j % per, 0)

    return pl.pallas_call(
        functools.partial(_finish_kernel, n_latent_tiles=nlt),
        out_shape=(jax.ShapeDtypeStruct((n_latent_groups, s, d), F32),
                   jax.ShapeDtypeStruct((g - n_latent_groups, s, d), F32)),
        grid=(nt,),
        in_specs=[
            pl.BlockSpec((tm, d), lambda i: (i, 0)),
            pl.BlockSpec((tm, d), lambda i: (nt + i, 0)),
            pl.BlockSpec((tm, LANES), lambda i: (i, 0)),
            pl.BlockSpec((None, tm, d), lambda i: (i // per, i % per, 0)),
            pl.BlockSpec((None, 6, d), lambda i: (i // per, 0, 0)),
            pl.BlockSpec((1, d), lambda i: (0, 0)),
            pl.BlockSpec((1, d), lambda i: (0, 0)),
        ],
        out_specs=(pl.BlockSpec((None, tm, d), lat_map), pl.BlockSpec((None, tm, d), ctx_map)),
        compiler_params=_cparams(("arbitrary",)),
        name="moe_finish",
    )(rows, rows, wk, x, mod, ln_g.reshape(1, d), ln_b.reshape(1, d))


def _moe(x, mod, router_w, router_b, wgu_bf, wd_bf, ln_g, ln_b, n_latent_groups):
    g, s, d = x.shape
    n = g * s
    n_blocks = 2 * n // MOE_BM + N_EXPERTS
    n_slots = n_blocks * MOE_BM
    h, wk, srank, counts = _route(x, mod, router_w, router_b)
    pos_row, e_r, jlo, jhi, n_used = _moe_plan(srank, counts, n_blocks)
    pos_row = pos_row.reshape(N_EXPERTS, n // MOE_TC, MOE_TC)
    y, meta = _experts(h, pos_row, (e_r, jlo, jhi, n_used), wgu_bf, wd_bf, n_blocks)
    tok = (meta[:, 0] * 128.0 + meta[:, 1]).astype(jnp.int32)
    choice = (meta[:, 2] == e_r[:, None].astype(F32)).astype(jnp.int32)
    unused = meta[:, 3] < 0.5
    spare = (2 * n + jnp.cumsum(unused.reshape(-1).astype(jnp.int32)) - 1).reshape(n_blocks, MOE_BM)
    dest = jnp.where(unused, spare, choice * n + tok)
    rows = _scatter_rows(y, dest, n_slots)
    return _finish(rows, wk, x, mod, ln_g, ln_b, n_latent_groups)


def _inproj_c_kernel(x_ref, mod_ref, w_ref, wg_ref, bg_ref, o_ref, og_ref):
    h = (x_ref[...] * (1.0 + mod_ref[1:2, :]) + mod_ref[0:1, :]).astype(BF16)
    o_ref[...] = jnp.dot(h, w_ref[:, 0:o_ref.shape[1]], preferred_element_type=F32)
    og_ref[...] = jnp.dot(h, wg_ref[...], preferred_element_type=F32) + bg_ref[...]


def _inproj_c(x, mod, w_bf, n, wg_bf, bg, tm=512):
    g, s, d = x.shape
    ng = wg_bf.shape[1]
    return pl.pallas_call(
        _inproj_c_kernel,
        out_shape=(jax.ShapeDtypeStruct((g, s, n), F32), jax.ShapeDtypeStruct((g, s, ng), F32)),
        grid=(g, s // tm),
        in_specs=[
            pl.BlockSpec((None, tm, d), lambda gi, ti: (gi, ti, 0)),
            pl.BlockSpec((None, 6, d), lambda gi, ti: (gi, 0, 0)),
            _resident(w_bf.shape, lambda gi, ti: (0, 0)),
            _resident((d, ng), lambda gi, ti: (0, 0)),
            pl.BlockSpec((1, ng), lambda gi, ti: (0, 0)),
        ],
        out_specs=(pl.BlockSpec((None, tm, n), lambda gi, ti: (gi, ti, 0)),
                   pl.BlockSpec((None, tm, ng), lambda gi, ti: (gi, ti, 0))),
        compiler_params=_cparams(("arbitrary", "arbitrary")),
        name="inproj_c",
    )(x, mod, w_bf, wg_bf, bg)


def _log_sigmoid(x):
    return jnp.minimum(x, 0.0) - jnp.log(1.0 + jnp.exp(-jnp.abs(x)))


MLSTM_L = 128


def _split3_bf16(x):
    hi = x.astype(BF16)
    r1 = x - hi.astype(F32)
    mid = r1.astype(BF16)
    lo = (r1 - mid.astype(F32)).astype(BF16)
    return hi, mid, lo


def _mlstm_kernel(*refs, seq, hg, has_init, emit_state):
    q_ref, k_ref, v_ref, o_ref, gi_ref, gf_ref, hgain_ref = refs[:7]
    pos = 7
    if has_init:
        c0_ref, n0_ref, m0_ref = refs[pos:pos + 3]
        pos += 3
    out_ref = refs[pos]
    pos += 1
    if emit_state:
        co_ref, no_ref, mo_ref = refs[pos:pos + 3]
        pos += 3
    cext_ref, hf_ref, hb_ref, b_ref, g_ref, gmax_ref, mt_ref, wi_ref, en_ref, ws_ref, gt_ref, wc_ref = refs[pos:]

    L = MLSTM_L
    dh = MLSTM_DH
    nh = MLSTM_HEADS
    nc = seq // L
    head0 = pl.program_id(1) * hg
    neg = -jnp.inf

    lane = lax.broadcasted_iota(jnp.int32, (L, LANES), 1)
    lane1 = lax.broadcasted_iota(jnp.int32, (1, LANES), 1)
    row = lax.broadcasted_iota(jnp.int32, (L, L), 0)
    col = lax.broadcasted_iota(jnp.int32, (L, L), 1)
    lower = col <= row
    upper = col >= row
    tri_l = jnp.where(lower, 1.0, 0.0).astype(BF16)
    tri_u = jnp.where(upper, 1.0, 0.0).astype(BF16)
    fwd_lane = lane < nh
    fwd_lane1 = lane1 < nh
    trow = lax.broadcasted_iota(jnp.int32, (L, LANES), 0)

    btot, glast = [], []
    for c in range(nc):
        rows = slice(c * L, (c + 1) * L)
        f = _log_sigmoid(gf_ref[rows, :])
        parts = _split3_bf16(f)
        pre = sum(jnp.dot(tri_l, p, preferred_element_type=F32) for p in parts)
        suf = sum(jnp.dot(tri_u, p, preferred_element_type=F32) for p in parts)
        b = jnp.where(fwd_lane, pre, suf)
        g = gi_ref[rows, :] - b
        gp, gs = g, g
        k = 1
        while k < L:
            gp = jnp.where(trow >= k, jnp.maximum(gp, pltpu.roll(gp, k, 0)), gp)
            gs = jnp.where(trow < L - k, jnp.maximum(gs, pltpu.roll(gs, L - k, 0)), gs)
            k *= 2
        gmax = jnp.where(fwd_lane, gp, gs)
        b_ref[rows, :] = b
        g_ref[rows, :] = g
        gmax_ref[rows, :] = gmax
        btot.append(jnp.where(fwd_lane1, b[L - 1:L, :], b[0:1, :]))
        glast.append(jnp.where(fwd_lane1, gmax[L - 1:L, :], gmax[0:1, :]))

    m_init = m0_ref[...] if has_init else jnp.zeros((1, LANES), F32)
    mf, mb = m_init, m_init
    ms_f, mn_f, ms_b, mn_b = [None] * nc, [None] * nc, [None] * nc, [None] * nc
    for c in range(nc):
        ms_f[c] = mf
        mf = btot[c] + jnp.maximum(mf, glast[c])
        mn_f[c] = mf
        cb = nc - 1 - c
        ms_b[cb] = mb
        mb = btot[cb] + jnp.maximum(mb, glast[cb])
        mn_b[cb] = mb
    m_final = jnp.where(fwd_lane1, mf, mb)

    for c in range(nc):
        rows = slice(c * L, (c + 1) * L)
        m_start = jnp.where(fwd_lane1, ms_f[c], ms_b[c])
        m_next = jnp.where(fwd_lane1, mn_f[c], mn_b[c])
        g = g_ref[rows, :]
        mt = jnp.maximum(m_start, gmax_ref[rows, :])
        mt_ref[rows, :] = mt
        wi_ref[rows, :] = jnp.exp(m_start - mt)
        en_ref[rows, :] = jnp.exp(-(b_ref[rows, :] + mt))
        ws_ref[rows, :] = jnp.exp(btot[c] + g - m_next)
        gt_ref[c] = g.T
        wc_ref[c:c + 1, :] = jnp.exp(btot[c] + m_start - m_next)

    for d in range(2):
        for hh in range(hg):
            idx = d * hg + hh
            if has_init:
                cext_ref[idx, :, 0:dh] = c0_ref[d, hh]
                n0_tile = jnp.where(lax.broadcasted_iota(jnp.int32, (dh, dh), 0) == 0, n0_ref[d, hh], 0.0)
                cext_ref[idx, :, dh:2 * dh] = n0_tile.T
            else:
                cext_ref[idx] = jnp.zeros((dh, 2 * dh), F32)

    ones_col = jnp.where(lane == 0, 1.0, 0.0).astype(BF16)
    nt = (((1,), (1,)), ((), ()))
    tn = (((0,), (0,)), ((), ()))

    def column(x, j):
        return jnp.sum(jnp.where(lane == j, x, 0.0), axis=-1, keepdims=True)

    def one_direction(d, hh, c, s_qk, q_bf, k_s, v_ext, v_bf):
        idx = d * hg + hh
        j = d * nh + head0 + hh
        rows = pl.ds(pl.multiple_of(c * L, L), L)
        mt = column(mt_ref[rows, :], j)
        wi = column(wi_ref[rows, :], j)
        en = column(en_ref[rows, :], j)
        ws = column(ws_ref[rows, :], j)
        g_r = gt_ref[c, pl.ds(j, 1), :]
        w_c = jnp.sum(jnp.where(lane1 == j, wc_ref[pl.ds(c, 1), :], 0.0), axis=-1, keepdims=True)
        causal = lower if d == 0 else upper
        p = s_qk * jnp.exp(jnp.where(causal, g_r - mt, neg))
        qc = jnp.dot(q_bf, cext_ref[idx].astype(BF16), preferred_element_type=F32)
        num = wi * qc[:, 0:dh] + jnp.dot(p.astype(BF16), v_bf, preferred_element_type=F32)
        den = wi * qc[:, dh:dh + 1] + jnp.sum(p, axis=-1, keepdims=True)
        h = num / jnp.maximum(jnp.abs(den), en)
        upd = lax.dot_general((ws * k_s).astype(BF16), v_ext, tn, preferred_element_type=F32)
        cext_ref[idx] = w_c * cext_ref[idx] + upd
        return h

    def load_chunk(hh, c):
        sl = (pl.ds(pl.multiple_of(c * L, L), L), slice(hh * dh, (hh + 1) * dh))
        q_bf = q_ref[sl].astype(BF16)
        k_s = k_ref[sl] * (dh ** -0.5)
        v_bf = v_ref[sl].astype(BF16)
        v_ext = jnp.concatenate([v_bf, ones_col], axis=-1)
        s_qk = lax.dot_general(q_bf, k_s.astype(BF16), nt, preferred_element_type=F32)
        return s_qk, q_bf, k_s, v_ext, v_bf

    def step(c, carry):
        cb = nc - 1 - c
        for hh in range(hg):
            h = one_direction(0, hh, c, *load_chunk(hh, c))
            hf_ref[pl.ds(pl.multiple_of(c * L, L), L), hh * dh:(hh + 1) * dh] = h
        for hh in range(hg):
            h = one_direction(1, hh, cb, *load_chunk(hh, cb))
            hb_ref[pl.ds(pl.multiple_of(cb * L, L), L), hh * dh:(hh + 1) * dh] = h
        return carry

    if nc <= 2:
        for c in range(nc):
            step(c, 0)
    else:
        lax.fori_loop(0, nc, step, 0)

    for hh in range(hg):
        cs = slice(hh * dh, (hh + 1) * dh)
        hs = hf_ref[:, cs] + hb_ref[:, cs]
        mu = jnp.mean(hs, axis=-1, keepdims=True)
        hc = hs - mu
        var = jnp.mean(hc * hc, axis=-1, keepdims=True)
        hn = hc * lax.rsqrt(var + LN_EPS) * hgain_ref[:, cs]
        out_ref[:, cs] = (_sigmoid(o_ref[:, cs]) * hn).astype(out_ref.dtype)

    if emit_state:
        for d in range(2):
            for hh in range(hg):
                idx = d * hg + hh
                co_ref[d, hh] = cext_ref[idx, :, 0:dh]
                no_ref[d, hh] = cext_ref[idx, :, dh:2 * dh].T[0:1, :]
        mo_ref[...] = m_final


def _mlstm(proj, gates, head_g, g0, n_seq, seq, hg, init=None, emit_state=False):
    g, s, _ = proj.shape
    per_group = s // seq
    n_hg = MLSTM_HEADS // hg
    w = hg * MLSTM_DH
    nc = seq // MLSTM_L
    n_blocks = D_MODEL // w

    def tok_map(colblock):
        return lambda b, hi: (g0 + b // per_group, b % per_group, colblock * n_blocks + hi)

    def gate_map(half):
        return lambda b, hi: (g0 + b // per_group, b % per_group, half)

    args = [proj, proj, proj, proj, gates, gates, head_g.reshape(1, D_MODEL)]
    in_specs = [
        pl.BlockSpec((None, seq, w), tok_map(0)),
        pl.BlockSpec((None, seq, w), tok_map(1)),
        pl.BlockSpec((None, seq, w), tok_map(2)),
        pl.BlockSpec((None, seq, w), tok_map(3)),
        pl.BlockSpec((None, seq, LANES), gate_map(0)),
        pl.BlockSpec((None, seq, LANES), gate_map(1)),
        pl.BlockSpec((1, w), lambda b, hi: (0, hi)),
    ]
    if init is not None:
        c0, n0, m0 = init
        m0_lanes = jnp.pad(m0.reshape(n_seq, 1, 2 * MLSTM_HEADS), ((0, 0), (0, 0), (0, LANES - 2 * MLSTM_HEADS)))
        args += [c0, n0.reshape(n0.shape[:-1] + (1, MLSTM_DH)), m0_lanes]
        in_specs += [
            pl.BlockSpec((None, None, 2, hg, MLSTM_DH, MLSTM_DH), lambda b, hi: (b, 0, 0, hi, 0, 0)),
            pl.BlockSpec((None, 2, hg, 1, MLSTM_DH), lambda b, hi: (b, 0, hi, 0, 0)),
            pl.BlockSpec((None, 1, LANES), lambda b, hi: (b, 0, 0)),
        ]

    out_shape = [jax.ShapeDtypeStruct((n_seq // per_group, s, D_MODEL), BF16)]
    out_specs = [pl.BlockSpec((None, seq, w), lambda b, hi: (b // per_group, b % per_group, hi))]
    if emit_state:
        out_shape += [
            jax.ShapeDtypeStruct((n_seq, 2, MLSTM_HEADS, MLSTM_DH, MLSTM_DH), F32),
            jax.ShapeDtypeStruct((n_seq, 2, MLSTM_HEADS, 1, MLSTM_DH), F32),
            jax.ShapeDtypeStruct((n_seq, n_hg, 1, LANES), F32),
        ]
        out_specs += [
            pl.BlockSpec((None, 2, hg, MLSTM_DH, MLSTM_DH), lambda b, hi: (b, 0, hi, 0, 0)),
            pl.BlockSpec((None, 2, hg, 1, MLSTM_DH), lambda b, hi: (b, 0, hi, 0, 0)),
            pl.BlockSpec((None, None, 1, LANES), lambda b, hi: (b, hi, 0, 0)),
        ]

    tok_scratch = pltpu.VMEM((seq, LANES), F32)
    return pl.pallas_call(
        functools.partial(_mlstm_kernel, seq=seq, hg=hg, has_init=init is not None, emit_state=emit_state),
        out_shape=tuple(out_shape),
        grid=(n_seq, n_hg),
        in_specs=in_specs,
        out_specs=tuple(out_specs),
        scratch_shapes=[
            pltpu.VMEM((2 * hg, MLSTM_DH, 2 * MLSTM_DH), F32),
            pltpu.VMEM((seq, w), F32),
            pltpu.VMEM((seq, w), F32),
        ] + [tok_scratch] * 7 + [
            pltpu.VMEM((nc, LANES, MLSTM_L), F32),
            pltpu.VMEM((max(nc, 8), LANES), F32),
        ],
        compiler_params=_cparams(("arbitrary", "arbitrary"), HIGH_VMEM_LIMIT),
        name="mlstm_%d" % seq,
    )(*args)


def kernel(x_prompt, x_sample, c, cache_k, cache_v, state_C, state_n, state_m, c_ctx, ada_w, ada_b, ln_g, ln_b, w_in_a, diff_lambda, diff_norm_g, pool_w, pool_scale, w_out_a, ffn_w_gu, ffn_w_down, w_in_c, b_gates_c, mlstm_norm_g, w_out_c, router_w, router_b, moe_w_gu, moe_w_down):
    n_ctx, seq_ctx, d = x_prompt.shape
    n_lat, seq_lat, _ = x_sample.shape
    assert d == D_MODEL and (n_ctx * seq_ctx) % seq_lat == 0 and seq_lat % seq_ctx == 0
    gl = n_lat
    gc = n_ctx * seq_ctx // seq_lat
    s = seq_lat

    x_ctx = x_prompt.reshape(gc, s, d)
    cvec = jnp.concatenate([c, jnp.broadcast_to(c_ctx[None, :], (gc, d))], axis=0)
    mod_all = _modulation(cvec, ada_w, ada_b).reshape(DEPTH, gl + gc, 6, d)

    mod = mod_all[0]
    lam_init = 0.8 - 0.6 * math.exp(-0.3 * 0)
    cos_t, sin_t = _rope_tables(s)
    proj, (ffn_wgu_bf, ffn_wd_bf) = _inproj_a(x_sample, x_ctx, mod, w_in_a[0].astype(BF16), cos_t, sin_t,
                                              (ffn_w_gu, ffn_w_down))
    norm_g = diff_norm_g[0].reshape(1, LANES)
    attn_c, new_k, new_v = _attn_context(proj, diff_lambda[0], norm_g, gl, n_ctx, seq_ctx, lam_init)
    attn_l = _attn_latent(proj, cache_k, cache_v, diff_lambda[0], norm_g, gl, lam_init)
    pool_c = _pool(proj, pool_w[0], pool_scale[0], gl, gc, seq_ctx)
    pool_l = _pool(proj, pool_w[0], pool_scale[0], 0, gl, seq_lat)
    w_out = w_out_a[0].astype(BF16)
    x = _outproj([(attn_l, attn_c), (pool_l, pool_c)], [w_out[:DIFF_WIDTH], w_out[DIFF_WIDTH:]],
                 x_sample, x_ctx, mod, ln_g[0, 0], ln_b[0, 0], 2)
    x, (moe_wgu_bf, moe_wd_bf, w_in_c_bf) = _ffn(x, mod, ffn_wgu_bf[0], ffn_wd_bf[0], ln_g[0, 1], ln_b[0, 1],
                                                  (moe_w_gu[0], moe_w_down[0], w_in_c))

    mod = mod_all[1]
    n_main = 4 * D_MODEL
    nh = MLSTM_HEADS
    wg4 = w_in_c[0][:, n_main:].reshape(d, N_GATES, nh)
    bg4 = b_gates_c[0].reshape(1, N_GATES, nh)
    lane_pad = ((0, 0), (0, LANES - 2 * nh))

    def gate_lanes(a):
        return jnp.concatenate([jnp.pad(jnp.concatenate([a[:, 0], a[:, 2]], axis=-1), lane_pad),
                                jnp.pad(jnp.concatenate([a[:, 1], a[:, 3]], axis=-1), lane_pad)], axis=-1)

    proj, gates = _inproj_c(x, mod, w_in_c_bf[0], n_main, gate_lanes(wg4).astype(BF16), gate_lanes(bg4))
    mix_c, new_c, new_n, new_m = _mlstm(proj, gates, mlstm_norm_g[0], gl, n_ctx, seq_ctx, MLSTM_HEADS,
                                        emit_state=True)
    (mix_l,) = _mlstm(proj, gates, mlstm_norm_g[0], 0, n_lat, seq_lat, MLSTM_HEADS,
                      init=(state_C, state_n[:, 0], state_m[:, 0]))
    x = _outproj([(mix_l, mix_c)], [w_out_c[0].astype(BF16)], x, None, mod, ln_g[1, 0], ln_b[1, 0], 2)
    y_sample, y_ctx = _moe(x, mod, router_w[0], router_b[0], moe_wgu_bf, moe_wd_bf,
                           ln_g[1, 1], ln_b[1, 1], gl)
    y_prompt = y_ctx.reshape(n_ctx, seq_ctx, d)
    new_m = new_m[:, 0, 0, :2 * MLSTM_HEADS].reshape(n_ctx, 2, MLSTM_HEADS)
    return (y_prompt, y_sample, new_k, new_v, new_c[:, None], new_n[..., 0, :][:, None], new_m[:, None])
```

```python
import functools
import math

import jax
import jax.numpy as jnp
import numpy as np
from jax import lax
from jax.experimental import pallas as pl
from jax.experimental.pallas import tpu as pltpu

F32 = jnp.float32
BF16 = jnp.bfloat16

D_MODEL = 1024
GRID_W = 64
ROPE_BASE = 10000.0
DIFF_HEADS = 4
DIFF_DH = 64
DIFF_WIDTH = DIFF_HEADS * 2 * DIFF_DH
POOL_GROUPS = 4
POOL_GC = 128
POOL_WIDTH = POOL_GROUPS * POOL_GC
POOL_WINDOWS = (2, 4, 8, 16)
MLSTM_HEADS = 8
MLSTM_DH = 128
N_GATES = 4
D_FF = 2816
N_EXPERTS = 8
D_FF_EXPERT = 1792
LN_EPS = 1e-5
DEPTH = 2
ALPHA = (2.0 * DEPTH) ** 0.25

LANES = 128
SUBLANES = 8
FF_CHUNK = 256
VMEM_LIMIT = 56 * 1024 * 1024
HIGH_VMEM_LIMIT = 60 * 1024 * 1024


def _cparams(sem, vmem=VMEM_LIMIT):
    return pltpu.CompilerParams(dimension_semantics=sem, vmem_limit_bytes=vmem)


def _resident(shape, index_map):
    return pl.BlockSpec(shape, index_map, pipeline_mode=pl.Buffered(1))


def _layer_norm_rows(z, g, b):
    mu = jnp.mean(z, axis=-1, keepdims=True)
    zc = z - mu
    var = jnp.mean(zc * zc, axis=-1, keepdims=True)
    return zc * lax.rsqrt(var + LN_EPS) * g + b


def _sigmoid(x):
    return 1.0 / (1.0 + jnp.exp(-x))


def _split_bf16(x):
    hi = x.astype(BF16)
    lo = (x - hi.astype(F32)).astype(BF16)
    return hi, lo


def _cast_rider_specs(weights, n_steps, per):
    specs = []
    for w in weights:
        e, rows, cols = w.shape
        per_e = next((k for k in range(n_steps // e, 0, -1)
                      if n_steps % (e * k) == 0 and rows % (2 * SUBLANES * k) == 0), None)
        if per_e is None:
            return None
        hold = n_steps // (e * per_e)

        def slab_map(gi, ti, per_e=per_e, hold=hold):
            slab = (gi * per + ti) // hold
            return (slab // per_e, slab % per_e, 0)

        specs.append(pl.BlockSpec((None, rows // per_e, cols), slab_map))
    return specs


def _cast_slabs(cast_in, cast_out):
    for src, dst in zip(cast_in, cast_out):
        dst[...] = src[...].astype(BF16)


def _mod_kernel(c_ref, w_ref, b_ref, o_ref):
    c = c_ref[...]
    h = (c * _sigmoid(c)).astype(BF16)
    o_ref[...] = jnp.dot(h, w_ref[...].astype(BF16), preferred_element_type=F32) + b_ref[...]


def _modulation(cvec, ada_w, ada_b):
    depth, d, n = ada_w.shape
    g = cvec.shape[0]
    tn = 1536
    return pl.pallas_call(
        _mod_kernel,
        out_shape=jax.ShapeDtypeStruct((depth, g, n), F32),
        grid=(depth, n // tn),
        in_specs=[
            pl.BlockSpec((g, d), lambda l, j: (0, 0)),
            pl.BlockSpec((None, d, tn), lambda l, j: (l, 0, j)),
            pl.BlockSpec((None, 1, tn), lambda l, j: (l, 0, j)),
        ],
        out_specs=pl.BlockSpec((None, g, tn), lambda l, j: (l, 0, j)),
        compiler_params=_cparams(("arbitrary", "arbitrary")),
        name="modulation",
    )(cvec, ada_w, ada_b.reshape(depth, 1, n))


def _rot_half16(x):
    lane = lax.broadcasted_iota(jnp.int32, x.shape, 1)
    return jnp.where((lane % 32) < 16, pltpu.roll(x, LANES - 16, 1), pltpu.roll(x, 16, 1))


def _two_stream_specs(tm, d, gl, ctx_first_group=0):
    return [pl.BlockSpec((None, tm, d), lambda gi, ti: (jnp.minimum(gi, gl - 1), jnp.where(gi < gl, ti, 0), 0)),
            pl.BlockSpec((None, tm, d), lambda gi, ti: (ctx_first_group + jnp.maximum(gi - gl, 0),
                                                        jnp.where(gi < gl, 0, ti), 0))]


def _inproj_a_kernel(xl_ref, xc_ref, mod_ref, w_ref, cos_ref, sin_ref, *rest, n_latent_groups, n_cast):
    cast_in, o_ref, cast_out = rest[:n_cast], rest[n_cast], rest[n_cast + 1:]
    _cast_slabs(cast_in, cast_out)
    x = jnp.where(pl.program_id(0) < n_latent_groups, xl_ref[...], xc_ref[...])
    h = x * (1.0 + mod_ref[1:2, :]) + mod_ref[0:1, :]
    p = jnp.dot(h.astype(BF16), w_ref[...], preferred_element_type=F32)
    cos = cos_ref[...]
    sin = sin_ref[...]
    n_rope = 2 * DIFF_WIDTH // LANES
    for j in range(n_rope):
        blk = p[:, j * LANES:(j + 1) * LANES]
        o_ref[:, j * LANES:(j + 1) * LANES] = blk * cos + _rot_half16(blk) * sin
    o_ref[:, n_rope * LANES:] = p[:, n_rope * LANES:]


def _inproj_a(x_lat, x_ctx, mod, w_bf, cos_t, sin_t, cast_weights=(), tm=512):
    n_latent_groups, s, d = x_lat.shape
    g = n_latent_groups + x_ctx.shape[0]
    n = w_bf.shape[1]
    per = s // tm
    cast_specs = _cast_rider_specs(cast_weights, g * per, per)
    if cast_specs is None:
        proj, _ = _inproj_a(x_lat, x_ctx, mod, w_bf, cos_t, sin_t, (), tm)
        return proj, tuple(w.astype(BF16) for w in cast_weights)

    def table_map(gi, ti):
        return (jnp.where(gi >= n_latent_groups, 1, 0), ti, 0)

    outs = pl.pallas_call(
        functools.partial(_inproj_a_kernel, n_latent_groups=n_latent_groups, n_cast=len(cast_weights)),
        out_shape=(jax.ShapeDtypeStruct((g, s, n), F32),) + tuple(
            jax.ShapeDtypeStruct(w.shape, BF16) for w in cast_weights),
        grid=(g, per),
        in_specs=_two_stream_specs(tm, d, n_latent_groups) + [
            pl.BlockSpec((None, 6, d), lambda gi, ti: (gi, 0, 0)),
            _resident((d, n), lambda gi, ti: (0, 0)),
            pl.BlockSpec((None, tm, LANES), table_map),
            pl.BlockSpec((None, tm, LANES), table_map),
        ] + cast_specs,
        out_specs=(pl.BlockSpec((None, tm, n), lambda gi, ti: (gi, ti, 0)),) + tuple(cast_specs),
        compiler_params=_cparams(("arbitrary", "arbitrary")),
        name="inproj_a",
    )(x_lat, x_ctx, mod, w_bf, cos_t, sin_t, *cast_weights)
    return outs[0], outs[1:]


def _rope_tables(n_tokens):
    rows = n_tokens // GRID_W
    row_pos = np.repeat(np.arange(rows), GRID_W).astype(np.float64)
    col_pos = np.tile(np.arange(GRID_W), rows).astype(np.float64)
    n_freq = DIFF_DH // 4
    inv_freq = np.power(ROPE_BASE, -np.arange(n_freq, dtype=np.float64) / n_freq)
    ang = np.stack([row_pos[:, None] * inv_freq, col_pos[:, None] * inv_freq], axis=1)
    cos, sin = np.cos(ang), np.sin(ang)
    cos64 = np.concatenate([cos[:, 0], cos[:, 0], cos[:, 1], cos[:, 1]], axis=-1)
    sin64 = np.concatenate([-sin[:, 0], sin[:, 0], -sin[:, 1], sin[:, 1]], axis=-1)
    cos_l = np.tile(cos64, (1, LANES // DIFF_DH))
    sin_l = np.tile(sin64, (1, LANES // DIFF_DH))
    cos_t = np.stack([cos_l, np.ones_like(cos_l)])
    sin_t = np.stack([sin_l, np.zeros_like(sin_l)])
    return jnp.asarray(cos_t, F32), jnp.asarray(sin_t, F32)


def _diff_attn_kernel(*refs, n_pieces, n_heads, lam_init, emit_kv):
    lam_ref, ng_ref, q_ref = refs[:3]
    kv_refs = refs[3:3 + 2 * n_pieces]
    o_ref = refs[3 + 2 * n_pieces]

    lp = lam_ref[...]
    lam = (jnp.exp(jnp.sum(lp[0:1] * lp[1:2], axis=-1, keepdims=True))
           - jnp.exp(jnp.sum(lp[2:3] * lp[3:4], axis=-1, keepdims=True)) + lam_init)
    nt = (((1,), (1,)), ((), ()))

    def softmax_pieces(ss):
        m = functools.reduce(jnp.maximum, [jnp.max(s, axis=-1, keepdims=True) for s in ss])
        es = [jnp.exp(s - m) for s in ss]
        l = functools.reduce(jnp.add, [jnp.sum(e, axis=-1, keepdims=True) for e in es])
        return [e / l for e in es]

    for h in range(n_heads):
        hs = slice(h * LANES, (h + 1) * LANES)
        q = q_ref[:, hs] * (DIFF_DH ** -0.5)
        lane = lax.broadcasted_iota(jnp.int32, q.shape, 1)
        q1 = jnp.where(lane < DIFF_DH, q, 0.0).astype(BF16)
        q2 = jnp.where(lane >= DIFF_DH, q, 0.0).astype(BF16)
        s1, s2, vs = [], [], []
        for i in range(n_pieces):
            kb = kv_refs[2 * i][:, hs].astype(BF16)
            vs.append(kv_refs[2 * i + 1][:, hs].astype(BF16))
            s1.append(lax.dot_general(q1, kb, nt, preferred_element_type=F32))
            s2.append(lax.dot_general(q2, kb, nt, preferred_element_type=F32))
        p1 = softmax_pieces(s1)
        p2 = softmax_pieces(s2)
        o = None
        for i in range(n_pieces):
            a = (p1[i] - lam * p2[i]).astype(BF16)
            t = jnp.dot(a, vs[i], preferred_element_type=F32)
            o = t if o is None else o + t
        o = o * lax.rsqrt(jnp.mean(o * o, axis=-1, keepdims=True) + LN_EPS)
        o_ref[:, hs] = (o * ng_ref[...] * (1.0 - lam_init)).astype(o_ref.dtype)
        if emit_kv:
            ko_ref, vo_ref = refs[4 + 2 * n_pieces:]
            ko_ref[h] = kv_refs[0][:, hs]
            vo_ref[h] = kv_refs[1][:, hs]


def _attn_context(proj, lam_p, norm_g, n_latent_groups, n_seq, seq, lam_init):
    g, s, _ = proj.shape
    per_group = s // seq
    blk = (None, seq, DIFF_WIDTH)

    def tok_map(colblock):
        return lambda b: (n_latent_groups + b // per_group, b % per_group, colblock)

    cache_shape = jax.ShapeDtypeStruct((n_seq, 1, DIFF_HEADS, seq, LANES), F32)
    cache_spec = pl.BlockSpec((None, None, DIFF_HEADS, seq, LANES), lambda b: (b, 0, 0, 0, 0))
    out_spec = pl.BlockSpec(blk, lambda b: (b // per_group, b % per_group, 0))
    return pl.pallas_call(
        functools.partial(_diff_attn_kernel, n_pieces=1, n_heads=DIFF_HEADS, lam_init=lam_init, emit_kv=True),
        out_shape=(jax.ShapeDtypeStruct((g - n_latent_groups, s, DIFF_WIDTH), BF16), cache_shape, cache_shape),
        grid=(n_seq,),
        in_specs=[
            pl.BlockSpec((4, DIFF_DH), lambda b: (0, 0)),
            pl.BlockSpec((1, LANES), lambda b: (0, 0)),
            pl.BlockSpec(blk, tok_map(0)),
            pl.BlockSpec(blk, tok_map(1)),
            pl.BlockSpec(blk, tok_map(2)),
        ],
        out_specs=(out_spec, cache_spec, cache_spec),
        compiler_params=_cparams(("arbitrary",)),
        name="attn_context",
    )(lam_p, norm_g, proj, proj, proj)


def _attn_latent(proj, cache_k, cache_v, lam_p, norm_g, n_latent_groups, lam_init, tq=256):
    g, s, _ = proj.shape
    past = cache_k.shape[3]
    cache_spec = pl.BlockSpec((None, None, None, past, LANES), lambda b, h, qi: (b, 0, h, 0, 0))
    return pl.pallas_call(
        functools.partial(_diff_attn_kernel, n_pieces=2, n_heads=1, lam_init=lam_init, emit_kv=False),
        out_shape=jax.ShapeDtypeStruct((n_latent_groups, s, DIFF_WIDTH), BF16),
        grid=(n_latent_groups, DIFF_HEADS, s // tq),
        in_specs=[
            pl.BlockSpec((4, DIFF_DH), lambda b, h, qi: (0, 0)),
            pl.BlockSpec((1, LANES), lambda b, h, qi: (0, 0)),
            pl.BlockSpec((None, tq, LANES), lambda b, h, qi: (b, qi, h)),
            cache_spec,
            cache_spec,
            pl.BlockSpec((None, s, LANES), lambda b, h, qi: (b, 0, DIFF_HEADS + h)),
            pl.BlockSpec((None, s, LANES), lambda b, h, qi: (b, 0, 2 * DIFF_HEADS + h)),
        ],
        out_specs=pl.BlockSpec((None, tq, LANES), lambda b, h, qi: (b, qi, h)),
        compiler_params=_cparams(("arbitrary", "arbitrary", "arbitrary")),
        name="attn_latent",
    )(lam_p, norm_g, proj, cache_k, cache_v, proj, proj)


POOL_ROW_BLOCK = 256
POOL_COL_WINDOW = 512
assert (POOL_COL_WINDOW - POOL_ROW_BLOCK) // 2 >= max(POOL_WINDOWS) // 2


def _pool_kernel(p_ref, w_ref, sc_ref, o_ref, band_ref, *, seq):
    @pl.when((pl.program_id(0) == 0) & (pl.program_id(1) == 0))
    def _():
        t = lax.broadcasted_iota(jnp.int32, (seq, seq), 0)
        s_ = lax.broadcasted_iota(jnp.int32, (seq, seq), 1)
        for gi, w in enumerate(POOL_WINDOWS):
            inside = (s_ >= t - w // 2) & (s_ <= t + w // 2 - 1)
            band_ref[gi] = jnp.where(inside, 1.0, 0.0).astype(BF16)

    tcol = lax.broadcasted_iota(jnp.int32, (seq, 1), 0)
    for gi, w in enumerate(POOL_WINDOWS):
        u = p_ref[:, gi * POOL_GC:(gi + 1) * POOL_GC]
        hi, lo = _split_bf16(u)
        rb = min(seq, POOL_ROW_BLOCK)
        cw = min(seq, POOL_COL_WINDOW)
        blocks = []
        for i in range(seq // rb):
            c0 = min(max(i * rb - (cw - rb) // 2, 0), seq - cw)
            band = band_ref[gi, i * rb:(i + 1) * rb, c0:c0 + cw]
            blocks.append(jnp.dot(band, hi[c0:c0 + cw], preferred_element_type=F32)
                          + jnp.dot(band, lo[c0:c0 + cw], preferred_element_type=F32))
        win = blocks[0] if len(blocks) == 1 else jnp.concatenate(blocks, axis=0)
        cnt = (jnp.minimum(tcol + (w // 2 - 1), seq - 1) - jnp.maximum(tcol - w // 2, 0) + 1).astype(F32)
        pooled = win / cnt - u
        mixed = jnp.dot(pooled.astype(BF16), w_ref[gi].astype(BF16), preferred_element_type=F32)
        o_ref[:, gi * POOL_GC:(gi + 1) * POOL_GC] = (
            mixed * sc_ref[:, gi * POOL_GC:(gi + 1) * POOL_GC]).astype(o_ref.dtype)


def _pool(proj, pool_w, pool_scale, g0, n_groups, seq):
    g, s, _ = proj.shape
    col = 3 * DIFF_WIDTH // POOL_WIDTH
    return pl.pallas_call(
        functools.partial(_pool_kernel, seq=seq),
        out_shape=jax.ShapeDtypeStruct((n_groups, s, POOL_WIDTH), BF16),
        grid=(n_groups, s // seq),
        in_specs=[
            pl.BlockSpec((None, seq, POOL_WIDTH), lambda gi, ti: (g0 + gi, ti, col)),
            pl.BlockSpec((POOL_GROUPS, POOL_GC, POOL_GC), lambda gi, ti: (0, 0, 0)),
            pl.BlockSpec((1, POOL_WIDTH), lambda gi, ti: (0, 0)),
        ],
        out_specs=pl.BlockSpec((None, seq, POOL_WIDTH), lambda gi, ti: (gi, ti, 0)),
        scratch_shapes=[pltpu.VMEM((POOL_GROUPS, seq, seq), BF16)],
        compiler_params=_cparams(("arbitrary", "arbitrary")),
        name="pool_%d" % seq,
    )(proj, pool_w, pool_scale.reshape(1, POOL_WIDTH))


def _outproj_kernel(*refs, n_in, gate_row, n_latent_groups):
    a_refs = refs[:2 * n_in]
    w_refs = refs[2 * n_in:3 * n_in]
    xl_ref, xc_ref, mod_ref, g_ref, b_ref, o_ref = refs[3 * n_in:]
    is_latent = pl.program_id(0) < n_latent_groups
    acc = None
    for i, w_ref in enumerate(w_refs):
        a = jnp.where(is_latent, a_refs[2 * i][...], a_refs[2 * i + 1][...])
        t = jnp.dot(a, w_ref[...], preferred_element_type=F32)
        acc = t if acc is None else acc + t
    x = jnp.where(is_latent, xl_ref[...], xc_ref[...])
    z = ALPHA * x + mod_ref[gate_row:gate_row + 1, :] * acc
    o_ref[...] = _layer_norm_rows(z, g_ref[...], b_ref[...])


def _outproj(acts, weights, x_lat, x_ctx, mod, ln_g, ln_b, gate_row, tm=512):
    gl = acts[0][0].shape[0]
    _, s, d = x_lat.shape
    if x_ctx is None:
        g = x_lat.shape[0]
        x_ctx, x_specs = x_lat, _two_stream_specs(tm, d, gl, gl)
    else:
        g = gl + x_ctx.shape[0]
        x_specs = _two_stream_specs(tm, d, gl)
    n_in = len(acts)
    in_specs = []
    flat_acts = []
    for a_lat, a_ctx in acts:
        in_specs += _two_stream_specs(tm, a_lat.shape[-1], gl)
        flat_acts += [a_lat, a_ctx]
    in_specs += [_resident(w.shape, lambda gi, ti: (0, 0)) for w in weights]
    in_specs += x_specs
    in_specs += [
        pl.BlockSpec((None, 6, d), lambda gi, ti: (gi, 0, 0)),
        pl.BlockSpec((1, d), lambda gi, ti: (0, 0)),
        pl.BlockSpec((1, d), lambda gi, ti: (0, 0)),
    ]
    return pl.pallas_call(
        functools.partial(_outproj_kernel, n_in=n_in, gate_row=gate_row, n_latent_groups=gl),
        out_shape=jax.ShapeDtypeStruct((g, s, d), F32),
        grid=(g, s // tm),
        in_specs=in_specs,
        out_specs=pl.BlockSpec((None, tm, d), lambda gi, ti: (gi, ti, 0)),
        compiler_params=_cparams(("arbitrary", "arbitrary")),
        name="outproj",
    )(*flat_acts, *weights, x_lat, x_ctx, mod, ln_g.reshape(1, d), ln_b.reshape(1, d))


def _swiglu_chunks(h_bf, wgu_ref, wd_ref, d_ff):
    acc = None
    for j in range(d_ff // FF_CHUNK):
        lo = j * FF_CHUNK
        gate = jnp.dot(h_bf, wgu_ref[:, lo:lo + FF_CHUNK], preferred_element_type=F32)
        up = jnp.dot(h_bf, wgu_ref[:, d_ff + lo:d_ff + lo + FF_CHUNK], preferred_element_type=F32)
        act = (gate * _sigmoid(gate) * up).astype(BF16)
        t = jnp.dot(act, wd_ref[lo:lo + FF_CHUNK, :], preferred_element_type=F32)
        acc = t if acc is None else acc + t
    return acc


def _ffn_kernel(x_ref, mod_ref, wgu_ref, wd_ref, g_ref, b_ref, *rest, n_cast):
    cast_in, o_ref, cast_out = rest[:n_cast], rest[n_cast], rest[n_cast + 1:]
    _cast_slabs(cast_in, cast_out)
    x = x_ref[...]
    h = (x * (1.0 + mod_ref[4:5, :]) + mod_ref[3:4, :]).astype(BF16)
    acc = _swiglu_chunks(h, wgu_ref, wd_ref, D_FF)
    z = ALPHA * x + mod_ref[5:6, :] * acc
    o_ref[...] = _layer_norm_rows(z, g_ref[...], b_ref[...])


def _ffn(x, mod, wgu_bf, wd_bf, ln_g, ln_b, cast_weights=(), tm=512):
    g, s, d = x.shape
    per = s // tm
    cast_specs = _cast_rider_specs(cast_weights, g * per, per)
    if cast_specs is None:
        out, _ = _ffn(x, mod, wgu_bf, wd_bf, ln_g, ln_b, (), tm)
        return out, tuple(w.astype(BF16) for w in cast_weights)
    outs = pl.pallas_call(
        functools.partial(_ffn_kernel, n_cast=len(cast_weights)),
        out_shape=(jax.ShapeDtypeStruct((g, s, d), F32),) + tuple(
            jax.ShapeDtypeStruct(w.shape, BF16) for w in cast_weights),
        grid=(g, per),
        in_specs=[
            pl.BlockSpec((None, tm, d), lambda gi, ti: (gi, ti, 0)),
            pl.BlockSpec((None, 6, d), lambda gi, ti: (gi, 0, 0)),
            _resident(wgu_bf.shape, lambda gi, ti: (0, 0)),
            _resident(wd_bf.shape, lambda gi, ti: (0, 0)),
            pl.BlockSpec((1, d), lambda gi, ti: (0, 0)),
            pl.BlockSpec((1, d), lambda gi, ti: (0, 0)),
        ] + cast_specs,
        out_specs=(pl.BlockSpec((None, tm, d), lambda gi, ti: (gi, ti, 0)),) + tuple(cast_specs),
        compiler_params=_cparams(("arbitrary", "arbitrary")),
        name="ffn",
    )(x, mod, wgu_bf, wd_bf, ln_g.reshape(1, d), ln_b.reshape(1, d), *cast_weights)
    return outs[0], outs[1:]


MOE_BM = 512
MOE_SB = 256
MOE_TC = 256
MOE_WINDOW = 5
META_LANES = LANES


def _router_top2(h, rw_ref, rb_ref):
    h_hi, h_lo = _split_bf16(h)
    w_hi, w_lo = _split_bf16(rw_ref[...])
    logits = (jnp.dot(h_hi, w_hi, preferred_element_type=F32)
              + jnp.dot(h_lo, w_hi, preferred_element_type=F32)
              + jnp.dot(h_hi, w_lo, preferred_element_type=F32)) + rb_ref[...]
    lane = lax.broadcasted_iota(jnp.int32, logits.shape, 1).astype(F32)
    neg = -jnp.inf
    logits = jnp.where(lane < N_EXPERTS, logits, neg)
    m1 = jnp.max(logits, axis=-1, keepdims=True)
    i1 = jnp.min(jnp.where(logits == m1, lane, float(LANES)), axis=-1, keepdims=True)
    rest = jnp.where(lane == i1, neg, logits)
    m2 = jnp.max(rest, axis=-1, keepdims=True)
    i2 = jnp.min(jnp.where(rest == m2, lane, float(LANES)), axis=-1, keepdims=True)
    e2 = jnp.exp(m2 - m1)
    return lane, i1, i2, 1.0 / (1.0 + e2), e2 / (1.0 + e2)


def _route_kernel(x_ref, mod_ref, rw_ref, rb_ref, h_ref, wk_ref, srank_ref, cnt_ref, tri_ref, run_ref):
    tm, d = x_ref.shape

    @pl.when(pl.program_id(0) == 0)
    def _():
        r = lax.broadcasted_iota(jnp.int32, (tm, tm), 0)
        c = lax.broadcasted_iota(jnp.int32, (tm, tm), 1)
        tri_ref[...] = jnp.where(c <= r, 1.0, 0.0).astype(BF16)
        run_ref[...] = jnp.zeros_like(run_ref)

    h = x_ref[...] * (1.0 + mod_ref[4:5, :]) + mod_ref[3:4, :]
    h_ref[:, 0:d] = h.astype(BF16)
    lane, i1, i2, w1, w2 = _router_top2(h, rw_ref, rb_ref)
    first_is_low = i1 < i2
    e_hi = jnp.where(first_is_low, i2, i1)
    wk_ref[...] = jnp.where(lane == 0.0, jnp.where(first_is_low, w1, w2),
                            jnp.where(lane == 1.0, jnp.where(first_is_low, w2, w1), 0.0))
    tok = (pl.program_id(0) * tm + lax.broadcasted_iota(jnp.int32, (tm, META_LANES), 0))
    meta = jnp.where(lane == 0.0, (tok // 128).astype(F32),
                     jnp.where(lane == 1.0, (tok % 128).astype(F32),
                               jnp.where(lane == 2.0, e_hi, jnp.where(lane == 3.0, 1.0, 0.0))))
    h_ref[:, d:d + META_LANES] = meta.astype(BF16)

    member = (lane == i1) | (lane == i2)
    mem = jnp.where(member, 1.0, 0.0)
    rank = jnp.dot(tri_ref[...], mem.astype(BF16), preferred_element_type=F32) + run_ref[...]
    srank_ref[...] = jnp.where(member, rank, -rank).T[0:SUBLANES, :]
    run_ref[...] = rank[tm - 1:tm, :]
    cnt_ref[...] = rank[tm - 1:tm, :]


def _route(x, mod, router_w, router_b, tm=512):
    g, s, d = x.shape
    n = g * s
    assert n <= 128 * 256
    assert n >= MOE_WINDOW * MOE_TC
    per = s // tm
    rw = jnp.pad(router_w, ((0, 0), (0, LANES - N_EXPERTS)))
    rb = jnp.pad(router_b, (0, LANES - N_EXPERTS)).reshape(1, LANES)
    return pl.pallas_call(
        _route_kernel,
        out_shape=(jax.ShapeDtypeStruct((n, d + META_LANES), BF16), jax.ShapeDtypeStruct((n, LANES), F32),
                   jax.ShapeDtypeStruct((SUBLANES, n), F32), jax.ShapeDtypeStruct((1, LANES), F32)),
        grid=(n // tm,),
        in_specs=[
            pl.BlockSpec((None, tm, d), lambda i: (i // per, i % per, 0)),
            pl.BlockSpec((None, 6, d), lambda i: (i // per, 0, 0)),
            pl.BlockSpec((d, LANES), lambda i: (0, 0)),
            pl.BlockSpec((1, LANES), lambda i: (0, 0)),
        ],
        out_specs=(pl.BlockSpec((tm, d + META_LANES), lambda i: (i, 0)), pl.BlockSpec((tm, LANES), lambda i: (i, 0)),
                   pl.BlockSpec((SUBLANES, tm), lambda i: (0, i)), pl.BlockSpec((1, LANES), lambda i: (0, 0))),
        scratch_shapes=[pltpu.VMEM((tm, tm), BF16), pltpu.VMEM((1, LANES), F32)],
        compiler_params=_cparams(("arbitrary",)),
        name="moe_route",
    )(x, mod, rw, rb)


def _moe_plan(srank, counts, n_blocks):
    e_n = N_EXPERTS
    i32 = jnp.int32
    cnt = counts[0, :e_n].astype(i32)
    nb = (cnt + MOE_BM - 1) // MOE_BM
    nb_incl = jnp.cumsum(nb)
    gstart = nb_incl - nb
    n_used = nb_incl[-1]
    sr = srank[:e_n]
    rank = jnp.abs(sr).astype(i32)
    pos_row = jnp.where(sr > 0, rank - 1 + MOE_BM * gstart[:, None], -1)
    chunk_end = rank[:, MOE_TC - 1::MOE_TC]

    rc = jnp.minimum(jnp.arange(n_blocks, dtype=i32), n_used - 1)
    e_r = jnp.minimum(jnp.sum(nb_incl[None, :] <= rc[:, None], axis=1, dtype=i32), e_n - 1)

    sb_per = MOE_BM // MOE_SB
    q = jnp.arange(n_blocks * sb_per, dtype=i32)
    r_q = jnp.minimum(q // sb_per, n_used - 1)
    e_q = e_r[r_q]
    first = ((r_q - gstart[e_q]) * sb_per + q % sb_per) * MOE_SB
    used = (q // sb_per < n_used) & (first < cnt[e_q])
    hi = jnp.minimum(first + MOE_SB, cnt[e_q])
    ends_q = chunk_end[e_q]
    jlo = jnp.where(used, jnp.sum(ends_q < (first + 1)[:, None], axis=1, dtype=i32), 0)
    jhi = jnp.where(used, jnp.sum(ends_q < hi[:, None], axis=1, dtype=i32), -1)
    return pos_row, e_r, jlo, jhi, n_used.reshape(1)


def _experts_kernel(be_ref, jlo_ref, jhi_ref, nused_ref, h_ref, pos_ref, wgu_ref, wd_ref, y_ref, meta_ref, xg_ref):
    r = pl.program_id(0)
    d = y_ref.shape[1]

    @pl.when(r < nused_ref[0])
    def _():
        e = be_ref[r]
        sb_per = MOE_BM // MOE_SB
        n_chunks = h_ref.shape[0] // MOE_TC
        for u in range(sb_per):
            sb_rows = slice(u * MOE_SB, (u + 1) * MOE_SB)
            slot = r * MOE_BM + u * MOE_SB + lax.broadcasted_iota(jnp.int32, (MOE_SB, MOE_TC), 0)

            def onehot(j, slot=slot):
                return jnp.where(pos_ref[e, pl.ds(j, 1), :] == slot, 1.0, 0.0).astype(BF16)

            j0 = jnp.minimum(jlo_ref[r * sb_per + u], n_chunks - MOE_WINDOW)
            window = jnp.concatenate([onehot(j0 + j) for j in range(MOE_WINDOW)], axis=-1)
            rows = h_ref[pl.ds(pl.multiple_of(j0 * MOE_TC, MOE_TC), MOE_WINDOW * MOE_TC), :]
            xg_ref[sb_rows, :] = jnp.dot(window, rows, preferred_element_type=F32)

            def chunk(j, carry, sb_rows=sb_rows, onehot=onehot):
                rows = h_ref[pl.ds(pl.multiple_of(j * MOE_TC, MOE_TC), MOE_TC), :]
                xg_ref[sb_rows, :] += jnp.dot(onehot(j), rows, preferred_element_type=F32)
                return carry

            lax.fori_loop(j0 + MOE_WINDOW, jhi_ref[r * sb_per + u] + 1, chunk, 0)
        meta_ref[...] = xg_ref[:, d:d + META_LANES].T[0:SUBLANES, :]
        y_ref[...] = _swiglu_chunks(xg_ref[:, 0:d].astype(BF16), wgu_ref, wd_ref, D_FF_EXPERT)

    @pl.when(r >= nused_ref[0])
    def _():
        y_ref[...] = jnp.zeros_like(y_ref)
        meta_ref[...] = jnp.zeros_like(meta_ref)


def _experts(h, pos_row, plan, wgu_bf, wd_bf, n_blocks):
    n, dx = h.shape
    d = dx - META_LANES
    e_r, jlo, jhi, n_used = plan
    grid_spec = pltpu.PrefetchScalarGridSpec(
        num_scalar_prefetch=4,
        grid=(n_blocks,),
        in_specs=[
            _resident((n, dx), lambda r, be, lo, hi, nu: (0, 0)),
            _resident(pos_row.shape, lambda r, be, lo, hi, nu: (0, 0, 0)),
            pl.BlockSpec((None, d, 2 * D_FF_EXPERT), lambda r, be, lo, hi, nu: (be[r], 0, 0),
                         pipeline_mode=pl.Buffered(1)),
            pl.BlockSpec((None, D_FF_EXPERT, d), lambda r, be, lo, hi, nu: (be[r], 0, 0)),
        ],
        out_specs=(pl.BlockSpec((MOE_BM, d), lambda r, be, lo, hi, nu: (r, 0)),
                   pl.BlockSpec((None, SUBLANES, MOE_BM), lambda r, be, lo, hi, nu: (r, 0, 0))),
        scratch_shapes=[pltpu.VMEM((MOE_BM, dx), F32)],
    )
    return pl.pallas_call(
        _experts_kernel,
        out_shape=(jax.ShapeDtypeStruct((n_blocks * MOE_BM, d), F32),
                   jax.ShapeDtypeStruct((n_blocks, SUBLANES, MOE_BM), F32)),
        grid_spec=grid_spec,
        compiler_params=_cparams(("arbitrary",), HIGH_VMEM_LIMIT),
        name="moe_experts",
    )(e_r, jlo, jhi, n_used, h, pos_row, wgu_bf, wd_bf)


def _scatter_kernel(dest_ref, y_ref, o_ref, ybuf_ref, sem):
    r = pl.program_id(0)
    n_steps = pl.num_programs(0)
    buf = r % 2

    def wait_block(b):
        pltpu.make_async_copy(ybuf_ref.at[b], o_ref.at[pl.ds(0, MOE_BM), :], sem.at[b]).wait()

    @pl.when(r >= 2)
    def _():
        wait_block(buf)

    def copy_out(b):
        ybuf_ref[b] = y_ref[...]
        for i in range(MOE_BM):
            pltpu.make_async_copy(ybuf_ref.at[b, pl.ds(i, 1), :], o_ref.at[pl.ds(dest_ref[0, i], 1), :],
                                  sem.at[b]).start(priority=i % 2)

    for b in range(2):
        pl.when(buf == b)(functools.partial(copy_out, b))

    @pl.when(r == n_steps - 1)
    def _():
        wait_block(1 - buf)
        wait_block(buf)


def _scatter_rows(y, dest, n_rows_out):
    n_slots, d = y.shape
    n_blocks = n_slots // MOE_BM
    assert n_blocks >= 2
    return pl.pallas_call(
        _scatter_kernel,
        out_shape=jax.ShapeDtypeStruct((n_rows_out, d), F32),
        grid=(n_blocks,),
        in_specs=[
            pl.BlockSpec((None, 1, MOE_BM), lambda r: (r, 0, 0), memory_space=pltpu.SMEM),
            pl.BlockSpec((MOE_BM, d), lambda r: (r, 0)),
        ],
        out_specs=pl.BlockSpec(memory_space=pl.ANY),
        scratch_shapes=[pltpu.VMEM((2, MOE_BM, d), F32), pltpu.SemaphoreType.DMA((2,))],
        compiler_params=_cparams(("arbitrary",)),
        name="moe_scatter",
    )(dest.reshape(n_blocks, 1, MOE_BM), y)


def _finish_kernel(a_ref, b_ref, wk_ref, x_ref, mod_ref, g_ref, bb_ref, ol_ref, oc_ref, *, n_latent_tiles):
    lane = lax.broadcasted_iota(jnp.int32, wk_ref.shape, 1)
    wk = wk_ref[...]
    w0 = jnp.sum(jnp.where(lane == 0, wk, 0.0), axis=-1, keepdims=True)
    w1 = jnp.sum(jnp.where(lane == 1, wk, 0.0), axis=-1, keepdims=True)
    acc = w0 * a_ref[...] + w1 * b_ref[...]
    z = ALPHA * x_ref[...] + mod_ref[5:6, :] * acc
    out = _layer_norm_rows(z, g_ref[...], bb_ref[...])
    is_latent = pl.program_id(0) < n_latent_tiles

    @pl.when(is_latent)
    def _():
        ol_ref[...] = out

    @pl.when(jnp.logical_not(is_latent))
    def _():
        oc_ref[...] = out


def _finish(rows, wk, x, mod, ln_g, ln_b, n_latent_groups, tm=1024):
    g, s, d = x.shape
    n = g * s
    per = s // tm
    nt = n // tm
    nlt = n_latent_groups * per

    def lat_map(i):
        j = jnp.minimum(i, nlt - 1)
        return (j // per, j % per, 0)

    def ctx_map(i):
        j = jnp.maximum(i - nlt, 0)
        return (j // per, j % per, 0)

    return pl.pallas_call(
        functools.partial(_finish_kernel, n_latent_tiles=nlt),
        out_shape=(jax.ShapeDtypeStruct((n_latent_groups, s, d), F32),
                   jax.ShapeDtypeStruct((g - n_latent_groups, s, d), F32)),
        grid=(nt,),
        in_specs=[
            pl.BlockSpec((tm, d), lambda i: (i, 0)),
            pl.BlockSpec((tm, d), lambda i: (nt + i, 0)),
            pl.BlockSpec((tm, LANES), lambda i: (i, 0)),
            pl.BlockSpec((None, tm, d), lambda i: (i // per, i % per, 0)),
            pl.BlockSpec((None, 6, d), lambda i: (i // per, 0, 0)),
            pl.BlockSpec((1, d), lambda i: (0, 0)),
            pl.BlockSpec((1, d), lambda i: (0, 0)),
        ],
        out_specs=(pl.BlockSpec((None, tm, d), lat_map), pl.BlockSpec((None, tm, d), ctx_map)),
        compiler_params=_cparams(("arbitrary",)),
        name="moe_finish",
    )(rows, rows, wk, x, mod, ln_g.reshape(1, d), ln_b.reshape(1, d))


def _moe(x, mod, router_w, router_b, wgu_bf, wd_bf, ln_g, ln_b, n_latent_groups):
    g, s, d = x.shape
    n = g * s
    n_blocks = 2 * n // MOE_BM + N_EXPERTS
    n_slots = n_blocks * MOE_BM
    h, wk, srank, counts = _route(x, mod, router_w, router_b)
    pos_row, e_r, jlo, jhi, n_used = _moe_plan(srank, counts, n_blocks)
    pos_row = pos_row.reshape(N_EXPERTS, n // MOE_TC, MOE_TC)
    y, meta = _experts(h, pos_row, (e_r, jlo, jhi, n_used), wgu_bf, wd_bf, n_blocks)
    tok = (meta[:, 0] * 128.0 + meta[:, 1]).astype(jnp.int32)
    choice = (meta[:, 2] == e_r[:, None].astype(F32)).astype(jnp.int32)
    unused = meta[:, 3] < 0.5
    spare = (2 * n + jnp.cumsum(unused.reshape(-1).astype(jnp.int32)) - 1).reshape(n_blocks, MOE_BM)
    dest = jnp.where(unused, spare, choice * n + tok)
    rows = _scatter_rows(y, dest, n_slots)
    return _finish(rows, wk, x, mod, ln_g, ln_b, n_latent_groups)


def _inproj_c_kernel(x_ref, mod_ref, w_ref, wg_ref, bg_ref, o_ref, og_ref):
    h = (x_ref[...] * (1.0 + mod_ref[1:2, :]) + mod_ref[0:1, :]).astype(BF16)
    o_ref[...] = jnp.dot(h, w_ref[:, 0:o_ref.shape[1]], preferred_element_type=F32)
    og_ref[...] = jnp.dot(h, wg_ref[...], preferred_element_type=F32) + bg_ref[...]


def _inproj_c(x, mod, w_bf, n, wg_bf, bg, tm=512):
    g, s, d = x.shape
    ng = wg_bf.shape[1]
    return pl.pallas_call(
        _inproj_c_kernel,
        out_shape=(jax.ShapeDtypeStruct((g, s, n), F32), jax.ShapeDtypeStruct((g, s, ng), F32)),
        grid=(g, s // tm),
        in_specs=[
            pl.BlockSpec((None, tm, d), lambda gi, ti: (gi, ti, 0)),
            pl.BlockSpec((None, 6, d), lambda gi, ti: (gi, 0, 0)),
            _resident(w_bf.shape, lambda gi, ti: (0, 0)),
            _resident((d, ng), lambda gi, ti: (0, 0)),
            pl.BlockSpec((1, ng), lambda gi, ti: (0, 0)),
        ],
        out_specs=(pl.BlockSpec((None, tm, n), lambda gi, ti: (gi, ti, 0)),
                   pl.BlockSpec((None, tm, ng), lambda gi, ti: (gi, ti, 0))),
        compiler_params=_cparams(("arbitrary", "arbitrary")),
        name="inproj_c",
    )(x, mod, w_bf, wg_bf, bg)


def _log_sigmoid(x):
    return jnp.minimum(x, 0.0) - jnp.log(1.0 + jnp.exp(-jnp.abs(x)))


MLSTM_L = 128


def _split3_bf16(x):
    hi = x.astype(BF16)
    r1 = x - hi.astype(F32)
    mid = r1.astype(BF16)
    lo = (r1 - mid.astype(F32)).astype(BF16)
    return hi, mid, lo


def _mlstm_kernel(*refs, seq, hg, has_init, emit_state):
    q_ref, k_ref, v_ref, o_ref, gi_ref, gf_ref, hgain_ref = refs[:7]
    pos = 7
    if has_init:
        c0_ref, n0_ref, m0_ref = refs[pos:pos + 3]
        pos += 3
    out_ref = refs[pos]
    pos += 1
    if emit_state:
        co_ref, no_ref, mo_ref = refs[pos:pos + 3]
        pos += 3
    cext_ref, hf_ref, hb_ref, b_ref, g_ref, gmax_ref, mt_ref, wi_ref, en_ref, ws_ref, gt_ref, wc_ref = refs[pos:]

    L = MLSTM_L
    dh = MLSTM_DH
    nh = MLSTM_HEADS
    nc = seq // L
    head0 = pl.program_id(1) * hg
    neg = -jnp.inf

    lane = lax.broadcasted_iota(jnp.int32, (L, LANES), 1)
    lane1 = lax.broadcasted_iota(jnp.int32, (1, LANES), 1)
    row = lax.broadcasted_iota(jnp.int32, (L, L), 0)
    col = lax.broadcasted_iota(jnp.int32, (L, L), 1)
    lower = col <= row
    upper = col >= row
    tri_l = jnp.where(lower, 1.0, 0.0).astype(BF16)
    tri_u = jnp.where(upper, 1.0, 0.0).astype(BF16)
    fwd_lane = lane < nh
    fwd_lane1 = lane1 < nh
    trow = lax.broadcasted_iota(jnp.int32, (L, LANES), 0)

    btot, glast = [], []
    for c in range(nc):
        rows = slice(c * L, (c + 1) * L)
        f = _log_sigmoid(gf_ref[rows, :])
        parts = _split3_bf16(f)
        pre = sum(jnp.dot(tri_l, p, preferred_element_type=F32) for p in parts)
        suf = sum(jnp.dot(tri_u, p, preferred_element_type=F32) for p in parts)
        b = jnp.where(fwd_lane, pre, suf)
        g = gi_ref[rows, :] - b
        gp, gs = g, g
        k = 1
        while k < L:
            gp = jnp.where(trow >= k, jnp.maximum(gp, pltpu.roll(gp, k, 0)), gp)
            gs = jnp.where(trow < L - k, jnp.maximum(gs, pltpu.roll(gs, L - k, 0)), gs)
            k *= 2
        gmax = jnp.where(fwd_lane, gp, gs)
        b_ref[rows, :] = b
        g_ref[rows, :] = g
        gmax_ref[rows, :] = gmax
        btot.append(jnp.where(fwd_lane1, b[L - 1:L, :], b[0:1, :]))
        glast.append(jnp.where(fwd_lane1, gmax[L - 1:L, :], gmax[0:1, :]))

    m_init = m0_ref[...] if has_init else jnp.zeros((1, LANES), F32)
    mf, mb = m_init, m_init
    ms_f, mn_f, ms_b, mn_b = [None] * nc, [None] * nc, [None] * nc, [None] * nc
    for c in range(nc):
        ms_f[c] = mf
        mf = btot[c] + jnp.maximum(mf, glast[c])
        mn_f[c] = mf
        cb = nc - 1 - c
        ms_b[cb] = mb
        mb = btot[cb] + jnp.maximum(mb, glast[cb])
        mn_b[cb] = mb
    m_final = jnp.where(fwd_lane1, mf, mb)

    for c in range(nc):
        rows = slice(c * L, (c + 1) * L)
        m_start = jnp.where(fwd_lane1, ms_f[c], ms_b[c])
        m_next = jnp.where(fwd_lane1, mn_f[c], mn_b[c])
        g = g_ref[rows, :]
        mt = jnp.maximum(m_start, gmax_ref[rows, :])
        mt_ref[rows, :] = mt
        wi_ref[rows, :] = jnp.exp(m_start - mt)
        en_ref[rows, :] = jnp.exp(-(b_ref[rows, :] + mt))
        ws_ref[rows, :] = jnp.exp(btot[c] + g - m_next)
        gt_ref[c] = g.T
        wc_ref[c:c + 1, :] = jnp.exp(btot[c] + m_start - m_next)

    for d in range(2):
        for hh in range(hg):
            idx = d * hg + hh
            if has_init:
                cext_ref[idx, :, 0:dh] = c0_ref[d, hh]
                n0_tile = jnp.where(lax.broadcasted_iota(jnp.int32, (dh, dh), 0) == 0, n0_ref[d, hh], 0.0)
                cext_ref[idx, :, dh:2 * dh] = n0_tile.T
            else:
                cext_ref[idx] = jnp.zeros((dh, 2 * dh), F32)

    ones_col = jnp.where(lane == 0, 1.0, 0.0).astype(BF16)
    nt = (((1,), (1,)), ((), ()))
    tn = (((0,), (0,)), ((), ()))

    def column(x, j):
        return jnp.sum(jnp.where(lane == j, x, 0.0), axis=-1, keepdims=True)

    def one_direction(d, hh, c, s_qk, q_bf, k_s, v_ext, v_bf):
        idx = d * hg + hh
        j = d * nh + head0 + hh
        rows = pl.ds(pl.multiple_of(c * L, L), L)
        mt = column(mt_ref[rows, :], j)
        wi = column(wi_ref[rows, :], j)
        en = column(en_ref[rows, :], j)
        ws = column(ws_ref[rows, :], j)
        g_r = gt_ref[c, pl.ds(j, 1), :]
        w_c = jnp.sum(jnp.where(lane1 == j, wc_ref[pl.ds(c, 1), :], 0.0), axis=-1, keepdims=True)
        causal = lower if d == 0 else upper
        p = s_qk * jnp.exp(jnp.where(causal, g_r - mt, neg))
        qc = jnp.dot(q_bf, cext_ref[idx].astype(BF16), preferred_element_type=F32)
        num = wi * qc[:, 0:dh] + jnp.dot(p.astype(BF16), v_bf, preferred_element_type=F32)
        den = wi * qc[:, dh:dh + 1] + jnp.sum(p, axis=-1, keepdims=True)
        h = num / jnp.maximum(jnp.abs(den), en)
        upd = lax.dot_general((ws * k_s).astype(BF16), v_ext, tn, preferred_element_type=F32)
        cext_ref[idx] = w_c * cext_ref[idx] + upd
        return h

    def load_chunk(hh, c):
        sl = (pl.ds(pl.multiple_of(c * L, L), L), slice(hh * dh, (hh + 1) * dh))
        q_bf = q_ref[sl].astype(BF16)
        k_s = k_ref[sl] * (dh ** -0.5)
        v_bf = v_ref[sl].astype(BF16)
        v_ext = jnp.concatenate([v_bf, ones_col], axis=-1)
        s_qk = lax.dot_general(q_bf, k_s.astype(BF16), nt, preferred_element_type=F32)
        return s_qk, q_bf, k_s, v_ext, v_bf

    def step(c, carry):
        cb = nc - 1 - c
        for hh in range(hg):
            h = one_direction(0, hh, c, *load_chunk(hh, c))
            hf_ref[pl.ds(pl.multiple_of(c * L, L), L), hh * dh:(hh + 1) * dh] = h
        for hh in range(hg):
            h = one_direction(1, hh, cb, *load_chunk(hh, cb))
            hb_ref[pl.ds(pl.multiple_of(cb * L, L), L), hh * dh:(hh + 1) * dh] = h
        return carry

    if nc <= 2:
        for c in range(nc):
            step(c, 0)
    else:
        lax.fori_loop(0, nc, step, 0)

    for hh in range(hg):
        cs = slice(hh * dh, (hh + 1) * dh)
        hs = hf_ref[:, cs] + hb_ref[:, cs]
        mu = jnp.mean(hs, axis=-1, keepdims=True)
        hc = hs - mu
        var = jnp.mean(hc * hc, axis=-1, keepdims=True)
        hn = hc * lax.rsqrt(var + LN_EPS) * hgain_ref[:, cs]
        out_ref[:, cs] = (_sigmoid(o_ref[:, cs]) * hn).astype(out_ref.dtype)

    if emit_state:
        for d in range(2):
            for hh in range(hg):
                idx = d * hg + hh
                co_ref[d, hh] = cext_ref[idx, :, 0:dh]
                no_ref[d, hh] = cext_ref[idx, :, dh:2 * dh].T[0:1, :]
        mo_ref[...] = m_final


def _mlstm(proj, gates, head_g, g0, n_seq, seq, hg, init=None, emit_state=False):
    g, s, _ = proj.shape
    per_group = s // seq
    n_hg = MLSTM_HEADS // hg
    w = hg * MLSTM_DH
    nc = seq // MLSTM_L
    n_blocks = D_MODEL // w

    def tok_map(colblock):
        return lambda b, hi: (g0 + b // per_group, b % per_group, colblock * n_blocks + hi)

    def gate_map(half):
        return lambda b, hi: (g0 + b // per_group, b % per_group, half)

    args = [proj, proj, proj, proj, gates, gates, head_g.reshape(1, D_MODEL)]
    in_specs = [
        pl.BlockSpec((None, seq, w), tok_map(0)),
        pl.BlockSpec((None, seq, w), tok_map(1)),
        pl.BlockSpec((None, seq, w), tok_map(2)),
        pl.BlockSpec((None, seq, w), tok_map(3)),
        pl.BlockSpec((None, seq, LANES), gate_map(0)),
        pl.BlockSpec((None, seq, LANES), gate_map(1)),
        pl.BlockSpec((1, w), lambda b, hi: (0, hi)),
    ]
    if init is not None:
        c0, n0, m0 = init
        m0_lanes = jnp.pad(m0.reshape(n_seq, 1, 2 * MLSTM_HEADS), ((0, 0), (0, 0), (0, LANES - 2 * MLSTM_HEADS)))
        args += [c0, n0.reshape(n0.shape[:-1] + (1, MLSTM_DH)), m0_lanes]
        in_specs += [
            pl.BlockSpec((None, None, 2, hg, MLSTM_DH, MLSTM_DH), lambda b, hi: (b, 0, 0, hi, 0, 0)),
            pl.BlockSpec((None, 2, hg, 1, MLSTM_DH), lambda b, hi: (b, 0, hi, 0, 0)),
            pl.BlockSpec((None, 1, LANES), lambda b, hi: (b, 0, 0)),
        ]

    out_shape = [jax.ShapeDtypeStruct((n_seq // per_group, s, D_MODEL), BF16)]
    out_specs = [pl.BlockSpec((None, seq, w), lambda b, hi: (b // per_group, b % per_group, hi))]
    if emit_state:
        out_shape += [
            jax.ShapeDtypeStruct((n_seq, 2, MLSTM_HEADS, MLSTM_DH, MLSTM_DH), F32),
            jax.ShapeDtypeStruct((n_seq, 2, MLSTM_HEADS, 1, MLSTM_DH), F32),
            jax.ShapeDtypeStruct((n_seq, n_hg, 1, LANES), F32),
        ]
        out_specs += [
            pl.BlockSpec((None, 2, hg, MLSTM_DH, MLSTM_DH), lambda b, hi: (b, 0, hi, 0, 0)),
            pl.BlockSpec((None, 2, hg, 1, MLSTM_DH), lambda b, hi: (b, 0, hi, 0, 0)),
            pl.BlockSpec((None, None, 1, LANES), lambda b, hi: (b, hi, 0, 0)),
        ]

    tok_scratch = pltpu.VMEM((seq, LANES), F32)
    return pl.pallas_call(
        functools.partial(_mlstm_kernel, seq=seq, hg=hg, has_init=init is not None, emit_state=emit_state),
        out_shape=tuple(out_shape),
        grid=(n_seq, n_hg),
        in_specs=in_specs,
        out_specs=tuple(out_specs),
        scratch_shapes=[
            pltpu.VMEM((2 * hg, MLSTM_DH, 2 * MLSTM_DH), F32),
            pltpu.VMEM((seq, w), F32),
            pltpu.VMEM((seq, w), F32),
        ] + [tok_scratch] * 7 + [
            pltpu.VMEM((nc, LANES, MLSTM_L), F32),
            pltpu.VMEM((max(nc, 8), LANES), F32),
        ],
        compiler_params=_cparams(("arbitrary", "arbitrary"), HIGH_VMEM_LIMIT),
        name="mlstm_%d" % seq,
    )(*args)


def kernel(x_prompt, x_sample, c, cache_k, cache_v, state_C, state_n, state_m, c_ctx, ada_w, ada_b, ln_g, ln_b, w_in_a, diff_lambda, diff_norm_g, pool_w, pool_scale, w_out_a, ffn_w_gu, ffn_w_down, w_in_c, b_gates_c, mlstm_norm_g, w_out_c, router_w, router_b, moe_w_gu, moe_w_down):
    n_ctx, seq_ctx, d = x_prompt.shape
    n_lat, seq_lat, _ = x_sample.shape
    assert d == D_MODEL and (n_ctx * seq_ctx) % seq_lat == 0 and seq_lat % seq_ctx == 0
    gl = n_lat
    gc = n_ctx * seq_ctx // seq_lat
    s = seq_lat

    x_ctx = x_prompt.reshape(gc, s, d)
    cvec = jnp.concatenate([c, jnp.broadcast_to(c_ctx[None, :], (gc, d))], axis=0)
    mod_all = _modulation(cvec, ada_w, ada_b).reshape(DEPTH, gl + gc, 6, d)

    mod = mod_all[0]
    lam_init = 0.8 - 0.6 * math.exp(-0.3 * 0)
    cos_t, sin_t = _rope_tables(s)
    proj, (ffn_wgu_bf, ffn_wd_bf) = _inproj_a(x_sample, x_ctx, mod, w_in_a[0].astype(BF16), cos_t, sin_t,
                                              (ffn_w_gu, ffn_w_down))
    norm_g = diff_norm_g[0].reshape(1, LANES)
    attn_c, new_k, new_v = _attn_context(proj, diff_lambda[0], norm_g, gl, n_ctx, seq_ctx, lam_init)
    attn_l = _attn_latent(proj, cache_k, cache_v, diff_lambda[0], norm_g, gl, lam_init)
    pool_c = _pool(proj, pool_w[0], pool_scale[0], gl, gc, seq_ctx)
    pool_l = _pool(proj, pool_w[0], pool_scale[0], 0, gl, seq_lat)
    w_out = w_out_a[0].astype(BF16)
    x = _outproj([(attn_l, attn_c), (pool_l, pool_c)], [w_out[:DIFF_WIDTH], w_out[DIFF_WIDTH:]],
                 x_sample, x_ctx, mod, ln_g[0, 0], ln_b[0, 0], 2)
    x, (moe_wgu_bf, moe_wd_bf, w_in_c_bf) = _ffn(x, mod, ffn_wgu_bf[0], ffn_wd_bf[0], ln_g[0, 1], ln_b[0, 1],
                                                  (moe_w_gu[0], moe_w_down[0], w_in_c))

    mod = mod_all[1]
    n_main = 4 * D_MODEL
    nh = MLSTM_HEADS
    wg4 = w_in_c[0][:, n_main:].reshape(d, N_GATES, nh)
    bg4 = b_gates_c[0].reshape(1, N_GATES, nh)
    lane_pad = ((0, 0), (0, LANES - 2 * nh))

    def gate_lanes(a):
        return jnp.concatenate([jnp.pad(jnp.concatenate([a[:, 0], a[:, 2]], axis=-1), lane_pad),
                                jnp.pad(jnp.concatenate([a[:, 1], a[:, 3]], axis=-1), lane_pad)], axis=-1)

    proj, gates = _inproj_c(x, mod, w_in_c_bf[0], n_main, gate_lanes(wg4).astype(BF16), gate_lanes(bg4))
    mix_c, new_c, new_n, new_m = _mlstm(proj, gates, mlstm_norm_g[0], gl, n_ctx, seq_ctx, MLSTM_HEADS,
                                        emit_state=True)
    (mix_l,) = _mlstm(proj, gates, mlstm_norm_g[0], 0, n_lat, seq_lat, MLSTM_HEADS,
                      init=(state_C, state_n[:, 0], state_m[:, 0]))
    x = _outproj([(mix_l, mix_c)], [w_out_c[0].astype(BF16)], x, None, mod, ln_g[1, 0], ln_b[1, 0], 2)
    y_sample, y_ctx = _moe(x, mod, router_w[0], router_b[0], moe_wgu_bf, moe_wd_bf,
                           ln_g[1, 1], ln_b[1, 1], gl)
    y_prompt = y_ctx.reshape(n_ctx, seq_ctx, d)
    new_m = new_m[:, 0, 0, :2 * MLSTM_HEADS].reshape(n_ctx, 2, MLSTM_HEADS)
    return (y_prompt, y_sample, new_k, new_v, new_c[:, None], new_n[..., 0, :][:, None], new_m[:, None])
```
